```python
import math
import jax, jax.numpy as jnp
from jax import lax
import numpy as np

D_MODEL = 1024
BATCH = 8
SEQ = 2048
DEPTH = 2

D_MIX = D_MODEL
D_GROUP = D_MIX // 4
HEAD_DIM = 64
N_HEADS_GROUP = D_GROUP // HEAD_DIM

RWKV_DECAY_LORA = 32
RWKV_AAA_LORA = 32
RWKV_MV_LORA = 32
RWKV_GATE_LORA = 64
RWKV_GN_EPS = HEAD_DIM * 1e-5
RWKV_SIZES = (D_GROUP, D_GROUP, D_GROUP, RWKV_DECAY_LORA, RWKV_AAA_LORA, RWKV_GATE_LORA)
RWKV_COLS = sum(RWKV_SIZES)

DILATED_BRANCHES = ((128, 1), (512, 4), (2048, 16))
ALIBI_SLOPES = tuple(2.0 ** (-8.0 * (h + 1) / N_HEADS_GROUP) for h in range(N_HEADS_GROUP))
ATTN_COLS = 3 * D_GROUP

SSD_STATE = 128
SSD_GROUPS = 2
SSD_CONV = 4
SSD_CHUNK = 128
SSD_XBC = D_GROUP + 2 * SSD_GROUPS * SSD_STATE
SSD_COLS = D_GROUP + SSD_XBC + N_HEADS_GROUP

HGRN_CHUNK = 16
HGRN_COLS = 4 * D_GROUP

IN_COLS = RWKV_COLS + ATTN_COLS + SSD_COLS + HGRN_COLS

D_FF = 4 * D_MODEL
ALPHA = (2.0 * DEPTH) ** 0.25
BETA = (8.0 * DEPTH) ** -0.25
LN_EPS = 1e-5
RMS_EPS = 1e-5

kernel_name = "hymba_rwkv7_dilated_ssd_hgrn2_deepnorm"


def _split(t, sizes):
    out, o = [], 0
    for s in sizes:
        out.append(t[..., o:o + s])
        o += s
    return out


def _layer_norm(x, w, b):
    x32 = x.astype(jnp.float32)
    mu = jnp.mean(x32, -1, keepdims=True)
    var = jnp.mean(jnp.square(x32 - mu), -1, keepdims=True)
    return ((x32 - mu) * lax.rsqrt(var + LN_EPS) * w + b).astype(x.dtype)


def _rms(t):
    return t * lax.rsqrt(jnp.mean(jnp.square(t), -1, keepdims=True) + RMS_EPS)


def _token_shift_lerp(f, mu):
    prev = jnp.pad(f, ((0, 0), (1, 0), (0, 0)))[:, :-1]
    return f + (prev - f) * mu


def rwkv7_time_mix(feat, w0, w2, a0, a2, g2, k_k, k_a, r_k, lnx_w, lnx_b,
                   v_first, v_feat, v0, v2):
    bsz, slen, _ = feat.shape
    H, N = N_HEADS_GROUP, HEAD_DIM
    r, k, v, fw, fa, fg = _split(feat, RWKV_SIZES)
    w_log = -jax.nn.softplus(-(w0 + jnp.tanh(fw) @ w2)) - 0.5
    decay = jnp.exp(-jnp.exp(w_log))
    a = jax.nn.sigmoid(a0 + fa @ a2)
    g = jax.nn.sigmoid(fg) @ g2
    if v_first is None:
        v_first = v
    else:
        v = v + (v_first - v) * jax.nn.sigmoid(v0 + v_feat @ v2)

    def heads(t):
        return t.reshape(bsz, slen, H, N)

    kk = heads(k * k_k)
    kk = kk / jnp.maximum(jnp.sqrt(jnp.sum(jnp.square(kk), -1, keepdims=True)), 1e-12)
    k = k * (1.0 + (a - 1.0) * k_a)

    def step(state, inp):
        r_t, w_t, k_t, v_t, kk_t, a_t = inp
        sa = jnp.einsum('bhvk,bhk->bhv', state, -kk_t)
        state = (state * w_t[:, :, None, :]
                 + sa[..., None] * (kk_t * a_t)[:, :, None, :]
                 + v_t[..., None] * k_t[:, :, None, :])
        return state, jnp.einsum('bhvk,bhk->bhv', state, r_t)

    xs = tuple(jnp.swapaxes(heads(t), 0, 1) for t in (r, decay, k, v, kk, a))
    _, y = lax.scan(step, jnp.zeros((bsz, H, N, N), jnp.float32), xs)
    y = jnp.swapaxes(y, 0, 1)
    mu = jnp.mean(y, -1, keepdims=True)
    var = jnp.mean(jnp.square(y - mu), -1, keepdims=True)
    y = ((y - mu) * lax.rsqrt(var + RWKV_GN_EPS)).reshape(bsz, slen, D_GROUP) * lnx_w + lnx_b
    bonus = jnp.sum(heads(r) * heads(k) * r_k, -1, keepdims=True) * heads(v)
    return (y + bonus.reshape(bsz, slen, D_GROUP)) * g, v_first


def _dilated_branch(q, k, v, window, dilation):
    bsz, slen, H, Dh = q.shape
    L = slen // dilation
    blk = window // dilation
    nb = -(-L // blk)
    Lp = nb * blk

    def to_sub(t):
        t = t.reshape(bsz, L, dilation, H, Dh).transpose(0, 2, 1, 3, 4)
        t = t.reshape(bsz * dilation, L, H, Dh)
        return jnp.pad(t, ((0, 0), (0, Lp - L), (0, 0), (0, 0)))

    qs, ks, vs = to_sub(q), to_sub(k), to_sub(v)
    qb = qs.reshape(-1, nb, blk, H, Dh)

    def band(t):
        tp = jnp.pad(t, ((0, 0), (blk, 0), (0, 0), (0, 0))).reshape(-1, nb + 1, blk, H, Dh)
        return jnp.concatenate([tp[:, :-1], tp[:, 1:]], axis=2)

    kb, vb = band(ks), band(vs)
    i = jnp.arange(blk)[:, None]
    j = jnp.arange(2 * blk)[None, :]
    dist = blk + i - j
    n = jnp.arange(nb)[:, None, None]
    valid = (dist >= 0) & (dist <= blk) & ((n > 0) | (j >= blk))
    slopes = jnp.asarray(ALIBI_SLOPES, jnp.float32)
    bias = -slopes[:, None, None] * (dist * dilation).astype(jnp.float32)
    s = jnp.einsum('znqhd,znkhd->znhqk', qb, kb) * (Dh ** -0.5) + bias[None, None]
    s = jnp.where(valid[None, :, None], s, -jnp.inf)
    m = jnp.max(s, -1, keepdims=True)
    p = jnp.exp(s - m)
    l = jnp.sum(p, -1, keepdims=True)
    o = jnp.einsum('znhqk,znkhd->znqhd', p, vb) / jnp.transpose(l[..., 0], (0, 1, 3, 2))[..., None]
    lse = jnp.transpose((m + jnp.log(l))[..., 0], (0, 1, 3, 2))

    def from_sub(t):
        t = t.reshape(bsz * dilation, Lp, *t.shape[3:])[:, :L]
        t = t.reshape(bsz, dilation, L, *t.shape[2:])
        t = jnp.swapaxes(t, 1, 2)
        return t.reshape(bsz, slen, *t.shape[3:])

    return from_sub(o), from_sub(lse)


def dilated_attention(q, k, v):
    outs, lses = [], []
    for window, dilation in DILATED_BRANCHES:
        o, lse = _dilated_branch(q, k, v, window, dilation)
        outs.append(o)
        lses.append(lse)
    wts = jax.nn.softmax(jnp.stack(lses), axis=0)
    return jnp.sum(jnp.stack(outs) * wts[..., None], axis=0)


def _causal_depthwise_conv(x, w, b):
    y = lax.conv_general_dilated(x, w[:, None, :], window_strides=(1,),
                                 padding=[(w.shape[0] - 1, 0)],
                                 dimension_numbers=('NWC', 'WIO', 'NWC'),
                                 feature_group_count=x.shape[-1])
    return y + b


def _segsum_exp(a):
    cs = jnp.cumsum(a, -1)
    diff = cs[..., :, None] - cs[..., None, :]
    mask = jnp.tril(jnp.ones((a.shape[-1], a.shape[-1]), bool))
    return jnp.where(mask, jnp.exp(jnp.where(mask, diff, 0.0)), 0.0)


def ssd_mixer(feat, conv_w, conv_b, dt_bias, A_log, D, norm_w):
    bsz, slen, _ = feat.shape
    H, P, G, N = N_HEADS_GROUP, HEAD_DIM, SSD_GROUPS, SSD_STATE
    z, xbc, dt = _split(feat, (D_GROUP, SSD_XBC, H))
    xbc = jax.nn.silu(_causal_depthwise_conv(xbc, conv_w.astype(jnp.float32), conv_b))
    xs, Bm, Cm = _split(xbc, (D_GROUP, G * N, G * N))
    dt = jax.nn.softplus(dt + dt_bias)
    A = -jnp.exp(A_log.astype(jnp.float32))
    Lc = SSD_CHUNK
    nc = slen // Lc
    hpg = H // G
    x = xs.reshape(bsz, nc, Lc, H, P)
    Bh = jnp.repeat(Bm.reshape(bsz, nc, Lc, G, N), hpg, axis=3)
    Ch = jnp.repeat(Cm.reshape(bsz, nc, Lc, G, N), hpg, axis=3)
    dtc = dt.reshape(bsz, nc, Lc, H)
    dA = jnp.transpose(dtc * A, (0, 3, 1, 2))
    cs = jnp.cumsum(dA, -1)
    xdt = x * dtc[..., None]
    scores = jnp.einsum('bclhn,bcshn->bhcls', Ch, Bh) * _segsum_exp(dA)
    y_diag = jnp.einsum('bhcls,bcshp->bclhp', scores, xdt)
    decay_states = jnp.exp(cs[..., -1:] - cs)
    chunk_states = jnp.einsum('bclhn,bhcl,bclhp->cbhpn', Bh, decay_states, xdt)
    chunk_decay = jnp.transpose(jnp.exp(cs[..., -1]), (2, 0, 1))

    def step(s, inp):
        st, dec = inp
        return s * dec[..., None, None] + st, s

    _, prev = lax.scan(step, jnp.zeros((bsz, H, P, N), jnp.float32), (chunk_states, chunk_decay))
    y_off = jnp.einsum('bclhn,cbhpn,bhcl->bclhp', Ch, prev, jnp.exp(cs))
    y = (y_diag + y_off).reshape(bsz, slen, H, P) + x.reshape(bsz, slen, H, P) * D[:, None]
    y = y.reshape(bsz, slen, D_GROUP) * jax.nn.silu(z)
    y = _rms(y.reshape(bsz, slen, G, D_GROUP // G)).reshape(bsz, slen, D_GROUP)
    return y * norm_w


def hgrn2_mixer(feat, lower_bound, norm_w):
    bsz, slen, _ = feat.shape
    H, K, V = N_HEADS_GROUP, HEAD_DIM, HEAD_DIM
    q, f, i, g = _split(feat, (D_GROUP,) * 4)
    forget = lower_bound + (1.0 - lower_bound) * jax.nn.sigmoid(f)
    log_f = jnp.log(forget)
    k = 1.0 - forget
    q = jax.nn.silu(q)
    C = HGRN_CHUNK
    nc = slen // C
    q, k, log_f = (t.reshape(bsz, nc, C, H, K) for t in (q, k, log_f))
    v = i.reshape(bsz, nc, C, H, V)
    b = jnp.cumsum(log_f, axis=2)
    diff = b[:, :, :, None] - b[:, :, None, :]
    causal = jnp.tril(jnp.ones((C, C), bool))[None, None, :, :, None, None]
    dec = jnp.exp(jnp.where(causal, diff, -jnp.inf))
    att = jnp.einsum('bnthk,bnshk,bntshk->bnhts', q, k, dec)
    o_intra = jnp.einsum('bnhts,bnshv->bnthv', att, v)
    kdec = k * jnp.exp(b[:, :, -1:] - b)
    U = jnp.einsum('bnshk,bnshv->nbhkv', kdec, v)
    tot = jnp.transpose(jnp.exp(b[:, :, -1]), (1, 0, 2, 3))

    def step(s, inp):
        u, d = inp
        return s * d[..., None] + u, s

    _, prev = lax.scan(step, jnp.zeros((bsz, H, K, V), jnp.float32), (U, tot))
    o_inter = jnp.einsum('bnthk,nbhkv->bnthv', q * jnp.exp(b), prev)
    o = _rms((o_intra + o_inter).reshape(bsz, slen, H, V)).reshape(bsz, slen, D_GROUP)
    return o * norm_w * jax.nn.silu(g)


def _fwd_setup_inputs(seed: int = 0) -> dict:
    key = jax.random.key(seed)
    ks = iter(jax.random.split(key, 40))
    nrm = lambda shape, scale: jax.random.normal(next(ks), shape, jnp.float32) * scale
    uni = lambda shape, lo, hi: jax.random.uniform(next(ks), shape, jnp.float32, lo, hi)
    L1 = DEPTH - 1
    H = N_HEADS_GROUP
    dt0 = jnp.exp(uni((DEPTH, H), math.log(1e-3), math.log(1e-1)))
    return {
        "x": nrm((BATCH, SEQ, D_MODEL), 1.0),
        "lower_bounds": nrm((DEPTH, D_GROUP), 0.5),
        "w_in": nrm((DEPTH, D_MODEL, IN_COLS), D_MODEL ** -0.5),
        "w_in_vres": nrm((L1, D_MODEL, RWKV_MV_LORA), D_MODEL ** -0.5),
        "mu_shift": uni((DEPTH, RWKV_COLS), 0.0, 1.0),
        "mu_vres": uni((L1, RWKV_MV_LORA), 0.0, 1.0),
        "rwkv_w0": uni((DEPTH, D_GROUP), -6.0, 1.0),
        "rwkv_w2": nrm((DEPTH, RWKV_DECAY_LORA, D_GROUP), RWKV_DECAY_LORA ** -0.5),
        "rwkv_a0": nrm((DEPTH, D_GROUP), 0.1),
        "rwkv_a2": nrm((DEPTH, RWKV_AAA_LORA, D_GROUP), RWKV_AAA_LORA ** -0.5),
        "rwkv_g2": nrm((DEPTH, RWKV_GATE_LORA, D_GROUP), RWKV_GATE_LORA ** -0.5),
        "rwkv_k_k": 0.85 + nrm((DEPTH, D_GROUP), 0.05),
        "rwkv_k_a": 1.0 + nrm((DEPTH, D_GROUP), 0.05),
        "rwkv_r_k": nrm((DEPTH, H, HEAD_DIM), 0.1),
        "rwkv_lnx_w": 1.0 + nrm((DEPTH, D_GROUP), 0.05),
        "rwkv_lnx_b": nrm((DEPTH, D_GROUP), 0.01),
        "rwkv_v0": nrm((L1, D_GROUP), 0.5),
        "rwkv_v2": nrm((L1, RWKV_MV_LORA, D_GROUP), RWKV_MV_LORA ** -0.5),
        "ssd_conv_w": nrm((DEPTH, SSD_CONV, SSD_XBC), SSD_CONV ** -0.5),
        "ssd_conv_b": nrm((DEPTH, SSD_XBC), 0.01),
        "ssd_dt_bias": dt0 + jnp.log(-jnp.expm1(-dt0)),
        "ssd_A_log": jnp.log(uni((DEPTH, H), 1.0, 16.0)),
        "ssd_D": 1.0 + nrm((DEPTH, H), 0.05),
        "ssd_norm_w": 1.0 + nrm((DEPTH, D_GROUP), 0.05),
        "hgrn_norm_w": 1.0 + nrm((DEPTH, D_GROUP), 0.05),
        "w_out": nrm((DEPTH, D_MIX, D_MODEL), BETA * D_MIX ** -0.5),
        "ln1_w": 1.0 + nrm((DEPTH, D_MODEL), 0.05),
        "ln1_b": nrm((DEPTH, D_MODEL), 0.01),
        "w_up": nrm((DEPTH, D_MODEL, D_FF), D_MODEL ** -0.5),
        "w_down": nrm((DEPTH, D_FF, D_MODEL), BETA * D_FF ** -0.5),
        "ln2_w": 1.0 + nrm((DEPTH, D_MODEL), 0.05),
        "ln2_b": nrm((DEPTH, D_MODEL), 0.01),
    }


def _fwd_reference(x, lower_bounds, w_in, w_in_vres, mu_shift, mu_vres, rwkv_w0, rwkv_w2,
              rwkv_a0, rwkv_a2, rwkv_g2, rwkv_k_k, rwkv_k_a, rwkv_r_k, rwkv_lnx_w,
              rwkv_lnx_b, rwkv_v0, rwkv_v2, ssd_conv_w, ssd_conv_b, ssd_dt_bias, ssd_A_log,
              ssd_D, ssd_norm_w, hgrn_norm_w, w_out, ln1_w, ln1_b, w_up, w_down, ln2_w, ln2_b):
    bsz, slen, _ = x.shape
    lb = jax.nn.softmax(lower_bounds.astype(jnp.float32), axis=0)
    lb = jnp.cumsum(lb, axis=0) - lb[0]
    v_first = None
    for l in range(DEPTH):
        if l == 0:
            proj = x @ w_in[l]
        else:
            proj = x @ jnp.concatenate([w_in[l], w_in_vres[l - 1]], axis=1)
        proj = proj.astype(jnp.float32)
        parts = _split(proj, (RWKV_COLS, ATTN_COLS, SSD_COLS, HGRN_COLS))
        f_rwkv, f_attn, f_ssd, f_hgrn = parts
        f_rwkv = _token_shift_lerp(f_rwkv, mu_shift[l])
        if l == 0:
            y_a, v_first = rwkv7_time_mix(f_rwkv, rwkv_w0[l], rwkv_w2[l], rwkv_a0[l], rwkv_a2[l],
                                          rwkv_g2[l], rwkv_k_k[l], rwkv_k_a[l], rwkv_r_k[l],
                                          rwkv_lnx_w[l], rwkv_lnx_b[l], None, None, None, None)
        else:
            f_vres = _token_shift_lerp(proj[..., IN_COLS:], mu_vres[l - 1])
            y_a, v_first = rwkv7_time_mix(f_rwkv, rwkv_w0[l], rwkv_w2[l], rwkv_a0[l], rwkv_a2[l],
                                          rwkv_g2[l], rwkv_k_k[l], rwkv_k_a[l], rwkv_r_k[l],
                                          rwkv_lnx_w[l], rwkv_lnx_b[l], v_first, f_vres,
                                          rwkv_v0[l - 1], rwkv_v2[l - 1])
        q, k, v = (t.reshape(bsz, slen, N_HEADS_GROUP, HEAD_DIM)
                   for t in _split(f_attn, (D_GROUP,) * 3))
        y_b = dilated_attention(q, k, v).reshape(bsz, slen, D_GROUP)
        y_c = ssd_mixer(f_ssd, ssd_conv_w[l], ssd_conv_b[l], ssd_dt_bias[l], ssd_A_log[l],
                        ssd_D[l], ssd_norm_w[l])
        y_d = hgrn2_mixer(f_hgrn, lb[l], hgrn_norm_w[l])
        mix = jnp.concatenate([y_a, y_b, y_c, y_d], axis=-1).astype(x.dtype) @ w_out[l]
        x = _layer_norm(ALPHA * x + mix, ln1_w[l], ln1_b[l])
        h = jnp.square(jax.nn.relu(x @ w_up[l]))
        x = _layer_norm(ALPHA * x + h @ w_down[l], ln2_w[l], ln2_b[l])
    return x


import jax as _jax
import jax.numpy as _jnp

TWIN_FORMAT = 'train_step'
FWD_PARAMS = ['x', 'lower_bounds', 'w_in', 'w_in_vres', 'mu_shift', 'mu_vres', 'rwkv_w0', 'rwkv_w2', 'rwkv_a0', 'rwkv_a2', 'rwkv_g2', 'rwkv_k_k', 'rwkv_k_a', 'rwkv_r_k', 'rwkv_lnx_w', 'rwkv_lnx_b', 'rwkv_v0', 'rwkv_v2', 'ssd_conv_w', 'ssd_conv_b', 'ssd_dt_bias', 'ssd_A_log', 'ssd_D', 'ssd_norm_w', 'hgrn_norm_w', 'w_out', 'ln1_w', 'ln1_b', 'w_up', 'w_down', 'ln2_w', 'ln2_b']
TWIN_WEIGHTS = ['lower_bounds', 'w_in', 'w_in_vres', 'mu_shift', 'mu_vres', 'rwkv_w0', 'rwkv_w2', 'rwkv_a0', 'rwkv_a2', 'rwkv_g2', 'rwkv_k_k', 'rwkv_k_a', 'rwkv_r_k', 'rwkv_lnx_w', 'rwkv_lnx_b', 'rwkv_v0', 'rwkv_v2', 'ssd_conv_w', 'ssd_conv_b', 'ssd_dt_bias', 'ssd_A_log', 'ssd_D', 'ssd_norm_w', 'hgrn_norm_w', 'w_out', 'ln1_w', 'ln1_b', 'w_up', 'w_down', 'ln2_w', 'ln2_b']
TWIN_DIFF_INPUT = 'x'
TWIN_INPUTS = ['x', 'lower_bounds', 'w_in', 'w_in_vres', 'mu_shift', 'mu_vres', 'rwkv_w0', 'rwkv_w2', 'rwkv_a0', 'rwkv_a2', 'rwkv_g2', 'rwkv_k_k', 'rwkv_k_a', 'rwkv_r_k', 'rwkv_lnx_w', 'rwkv_lnx_b', 'rwkv_v0', 'rwkv_v2', 'ssd_conv_w', 'ssd_conv_b', 'ssd_dt_bias', 'ssd_A_log', 'ssd_D', 'ssd_norm_w', 'hgrn_norm_w', 'w_out', 'ln1_w', 'ln1_b', 'w_up', 'w_down', 'ln2_w', 'ln2_b', 'loss_target', 'm_lower_bounds', 'm_w_in', 'm_w_in_vres', 'm_mu_shift', 'm_mu_vres', 'm_rwkv_w0', 'm_rwkv_w2', 'm_rwkv_a0', 'm_rwkv_a2', 'm_rwkv_g2', 'm_rwkv_k_k', 'm_rwkv_k_a', 'm_rwkv_r_k', 'm_rwkv_lnx_w', 'm_rwkv_lnx_b', 'm_rwkv_v0', 'm_rwkv_v2', 'm_ssd_conv_w', 'm_ssd_conv_b', 'm_ssd_dt_bias', 'm_ssd_A_log', 'm_ssd_D', 'm_ssd_norm_w', 'm_hgrn_norm_w', 'm_w_out', 'm_ln1_w', 'm_ln1_b', 'm_w_up', 'm_w_down', 'm_ln2_w', 'm_ln2_b', 'v_lower_bounds', 'v_w_in', 'v_w_in_vres', 'v_mu_shift', 'v_mu_vres', 'v_rwkv_w0', 'v_rwkv_w2', 'v_rwkv_a0', 'v_rwkv_a2', 'v_rwkv_g2', 'v_rwkv_k_k', 'v_rwkv_k_a', 'v_rwkv_r_k', 'v_rwkv_lnx_w', 'v_rwkv_lnx_b', 'v_rwkv_v0', 'v_rwkv_v2', 'v_ssd_conv_w', 'v_ssd_conv_b', 'v_ssd_dt_bias', 'v_ssd_A_log', 'v_ssd_D', 'v_ssd_norm_w', 'v_hgrn_norm_w', 'v_w_out', 'v_ln1_w', 'v_ln1_b', 'v_w_up', 'v_w_down', 'v_ln2_w', 'v_ln2_b']
TWIN_OUTPUTS = ['loss', 'grad_x', 'grad_lower_bounds', 'grad_w_in', 'grad_w_in_vres', 'grad_mu_shift', 'grad_mu_vres', 'grad_rwkv_w0', 'grad_rwkv_w2', 'grad_rwkv_a0', 'grad_rwkv_a2', 'grad_rwkv_g2', 'grad_rwkv_k_k', 'grad_rwkv_k_a', 'grad_rwkv_r_k', 'grad_rwkv_lnx_w', 'grad_rwkv_lnx_b', 'grad_rwkv_v0', 'grad_rwkv_v2', 'grad_ssd_conv_w', 'grad_ssd_conv_b', 'grad_ssd_dt_bias', 'grad_ssd_A_log', 'grad_ssd_D', 'grad_ssd_norm_w', 'grad_hgrn_norm_w', 'grad_w_out', 'grad_ln1_w', 'grad_ln1_b', 'grad_w_up', 'grad_w_down', 'grad_ln2_w', 'grad_ln2_b', 'delta_lower_bounds', 'delta_w_in', 'delta_w_in_vres', 'delta_mu_shift', 'delta_mu_vres', 'delta_rwkv_w0', 'delta_rwkv_w2', 'delta_rwkv_a0', 'delta_rwkv_a2', 'delta_rwkv_g2', 'delta_rwkv_k_k', 'delta_rwkv_k_a', 'delta_rwkv_r_k', 'delta_rwkv_lnx_w', 'delta_rwkv_lnx_b', 'delta_rwkv_v0', 'delta_rwkv_v2', 'delta_ssd_conv_w', 'delta_ssd_conv_b', 'delta_ssd_dt_bias', 'delta_ssd_A_log', 'delta_ssd_D', 'delta_ssd_norm_w', 'delta_hgrn_norm_w', 'delta_w_out', 'delta_ln1_w', 'delta_ln1_b', 'delta_w_up', 'delta_w_down', 'delta_ln2_w', 'delta_ln2_b', 'new_m_lower_bounds', 'new_m_w_in', 'new_m_w_in_vres', 'new_m_mu_shift', 'new_m_mu_vres', 'new_m_rwkv_w0', 'new_m_rwkv_w2', 'new_m_rwkv_a0', 'new_m_rwkv_a2', 'new_m_rwkv_g2', 'new_m_rwkv_k_k', 'new_m_rwkv_k_a', 'new_m_rwkv_r_k', 'new_m_rwkv_lnx_w', 'new_m_rwkv_lnx_b', 'new_m_rwkv_v0', 'new_m_rwkv_v2', 'new_m_ssd_conv_w', 'new_m_ssd_conv_b', 'new_m_ssd_dt_bias', 'new_m_ssd_A_log', 'new_m_ssd_D', 'new_m_ssd_norm_w', 'new_m_hgrn_norm_w', 'new_m_w_out', 'new_m_ln1_w', 'new_m_ln1_b', 'new_m_w_up', 'new_m_w_down', 'new_m_ln2_w', 'new_m_ln2_b', 'new_v_lower_bounds', 'new_v_w_in', 'new_v_w_in_vres', 'new_v_mu_shift', 'new_v_mu_vres', 'new_v_rwkv_w0', 'new_v_rwkv_w2', 'new_v_rwkv_a0', 'new_v_rwkv_a2', 'new_v_rwkv_g2', 'new_v_rwkv_k_k', 'new_v_rwkv_k_a', 'new_v_rwkv_r_k', 'new_v_rwkv_lnx_w', 'new_v_rwkv_lnx_b', 'new_v_rwkv_v0', 'new_v_rwkv_v2', 'new_v_ssd_conv_w', 'new_v_ssd_conv_b', 'new_v_ssd_dt_bias', 'new_v_ssd_A_log', 'new_v_ssd_D', 'new_v_ssd_norm_w', 'new_v_hgrn_norm_w', 'new_v_w_out', 'new_v_ln1_w', 'new_v_ln1_b', 'new_v_w_up', 'new_v_w_down', 'new_v_ln2_w', 'new_v_ln2_b']
TWIN_LEAF_KINDS = {'loss': 'loss', 'grad_x': 'grad_x', 'grad_lower_bounds': 'grad_w', 'grad_w_in': 'grad_w', 'grad_w_in_vres': 'grad_w', 'grad_mu_shift': 'grad_w', 'grad_mu_vres': 'grad_w', 'grad_rwkv_w0': 'grad_w', 'grad_rwkv_w2': 'grad_w', 'grad_rwkv_a0': 'grad_w', 'grad_rwkv_a2': 'grad_w', 'grad_rwkv_g2': 'grad_w', 'grad_rwkv_k_k': 'grad_w', 'grad_rwkv_k_a': 'grad_w', 'grad_rwkv_r_k': 'grad_w', 'grad_rwkv_lnx_w': 'grad_w', 'grad_rwkv_lnx_b': 'grad_w', 'grad_rwkv_v0': 'grad_w', 'grad_rwkv_v2': 'grad_w', 'grad_ssd_conv_w': 'grad_w', 'grad_ssd_conv_b': 'grad_w', 'grad_ssd_dt_bias': 'grad_w', 'grad_ssd_A_log': 'grad_w', 'grad_ssd_D': 'grad_w', 'grad_ssd_norm_w': 'grad_w', 'grad_hgrn_norm_w': 'grad_w', 'grad_w_out': 'grad_w', 'grad_ln1_w': 'grad_w', 'grad_ln1_b': 'grad_w', 'grad_w_up': 'grad_w', 'grad_w_down': 'grad_w', 'grad_ln2_w': 'grad_w', 'grad_ln2_b': 'grad_w', 'delta_lower_bounds': 'delta_w', 'delta_w_in': 'delta_w', 'delta_w_in_vres': 'delta_w', 'delta_mu_shift': 'delta_w', 'delta_mu_vres': 'delta_w', 'delta_rwkv_w0': 'delta_w', 'delta_rwkv_w2': 'delta_w', 'delta_rwkv_a0': 'delta_w', 'delta_rwkv_a2': 'delta_w', 'delta_rwkv_g2': 'delta_w', 'delta_rwkv_k_k': 'delta_w', 'delta_rwkv_k_a': 'delta_w', 'delta_rwkv_r_k': 'delta_w', 'delta_rwkv_lnx_w': 'delta_w', 'delta_rwkv_lnx_b': 'delta_w', 'delta_rwkv_v0': 'delta_w', 'delta_rwkv_v2': 'delta_w', 'delta_ssd_conv_w': 'delta_w', 'delta_ssd_conv_b': 'delta_w', 'delta_ssd_dt_bias': 'delta_w', 'delta_ssd_A_log': 'delta_w', 'delta_ssd_D': 'delta_w', 'delta_ssd_norm_w': 'delta_w', 'delta_hgrn_norm_w': 'delta_w', 'delta_w_out': 'delta_w', 'delta_ln1_w': 'delta_w', 'delta_ln1_b': 'delta_w', 'delta_w_up': 'delta_w', 'delta_w_down': 'delta_w', 'delta_ln2_w': 'delta_w', 'delta_ln2_b': 'delta_w', 'new_m_lower_bounds': 'new_m', 'new_m_w_in': 'new_m', 'new_m_w_in_vres': 'new_m', 'new_m_mu_shift': 'new_m', 'new_m_mu_vres': 'new_m', 'new_m_rwkv_w0': 'new_m', 'new_m_rwkv_w2': 'new_m', 'new_m_rwkv_a0': 'new_m', 'new_m_rwkv_a2': 'new_m', 'new_m_rwkv_g2': 'new_m', 'new_m_rwkv_k_k': 'new_m', 'new_m_rwkv_k_a': 'new_m', 'new_m_rwkv_r_k': 'new_m', 'new_m_rwkv_lnx_w': 'new_m', 'new_m_rwkv_lnx_b': 'new_m', 'new_m_rwkv_v0': 'new_m', 'new_m_rwkv_v2': 'new_m', 'new_m_ssd_conv_w': 'new_m', 'new_m_ssd_conv_b': 'new_m', 'new_m_ssd_dt_bias': 'new_m', 'new_m_ssd_A_log': 'new_m', 'new_m_ssd_D': 'new_m', 'new_m_ssd_norm_w': 'new_m', 'new_m_hgrn_norm_w': 'new_m', 'new_m_w_out': 'new_m', 'new_m_ln1_w': 'new_m', 'new_m_ln1_b': 'new_m', 'new_m_w_up': 'new_m', 'new_m_w_down': 'new_m', 'new_m_ln2_w': 'new_m', 'new_m_ln2_b': 'new_m', 'new_v_lower_bounds': 'new_v', 'new_v_w_in': 'new_v', 'new_v_w_in_vres': 'new_v', 'new_v_mu_shift': 'new_v', 'new_v_mu_vres': 'new_v', 'new_v_rwkv_w0': 'new_v', 'new_v_rwkv_w2': 'new_v', 'new_v_rwkv_a0': 'new_v', 'new_v_rwkv_a2': 'new_v', 'new_v_rwkv_g2': 'new_v', 'new_v_rwkv_k_k': 'new_v', 'new_v_rwkv_k_a': 'new_v', 'new_v_rwkv_r_k': 'new_v', 'new_v_rwkv_lnx_w': 'new_v', 'new_v_rwkv_lnx_b': 'new_v', 'new_v_rwkv_v0': 'new_v', 'new_v_rwkv_v2': 'new_v', 'new_v_ssd_conv_w': 'new_v', 'new_v_ssd_conv_b': 'new_v', 'new_v_ssd_dt_bias': 'new_v', 'new_v_ssd_A_log': 'new_v', 'new_v_ssd_D': 'new_v', 'new_v_ssd_norm_w': 'new_v', 'new_v_hgrn_norm_w': 'new_v', 'new_v_w_out': 'new_v', 'new_v_ln1_w': 'new_v', 'new_v_ln1_b': 'new_v', 'new_v_w_up': 'new_v', 'new_v_w_down': 'new_v', 'new_v_ln2_w': 'new_v', 'new_v_ln2_b': 'new_v'}


def _forward(args):
    return _fwd_reference(*[args[k] for k in FWD_PARAMS])


def _output_shape():
    out = _jax.eval_shape(lambda: _forward(_fwd_setup_inputs(0)))
    return out.shape, out.dtype

N_MICROBATCH = 1
ADAM_LR = 0.001
ADAM_B1 = 0.9
ADAM_B2 = 0.999
ADAM_EPS = 1e-08
ADAM_WD = 0.01
ADAM_STEP = 10
PER_EXAMPLE_BATCH_AXIS = {'x': 0, 'loss_target': 0}
SHARED_INPUTS = []
_WEIGHT_DTYPES = {'lower_bounds': _jnp.float32, 'w_in': _jnp.float32, 'w_in_vres': _jnp.float32, 'mu_shift': _jnp.float32, 'mu_vres': _jnp.float32, 'rwkv_w0': _jnp.float32, 'rwkv_w2': _jnp.float32, 'rwkv_a0': _jnp.float32, 'rwkv_a2': _jnp.float32, 'rwkv_g2': _jnp.float32, 'rwkv_k_k': _jnp.float32, 'rwkv_k_a': _jnp.float32, 'rwkv_r_k': _jnp.float32, 'rwkv_lnx_w': _jnp.float32, 'rwkv_lnx_b': _jnp.float32, 'rwkv_v0': _jnp.float32, 'rwkv_v2': _jnp.float32, 'ssd_conv_w': _jnp.float32, 'ssd_conv_b': _jnp.float32, 'ssd_dt_bias': _jnp.float32, 'ssd_A_log': _jnp.float32, 'ssd_D': _jnp.float32, 'ssd_norm_w': _jnp.float32, 'hgrn_norm_w': _jnp.float32, 'w_out': _jnp.float32, 'ln1_w': _jnp.float32, 'ln1_b': _jnp.float32, 'w_up': _jnp.float32, 'w_down': _jnp.float32, 'ln2_w': _jnp.float32, 'ln2_b': _jnp.float32}
MOMENT_SCALE = {'lower_bounds': 3.765999e-03, 'w_in': 2.526036e-02, 'w_in_vres': 2.103536e-02, 'mu_shift': 3.869558e-02, 'mu_vres': 3.073917e-02, 'rwkv_w0': 1.073566e-02, 'rwkv_w2': 1.540589e-03, 'rwkv_a0': 9.429296e-03, 'rwkv_a2': 8.344347e-03, 'rwkv_g2': 2.519894e-02, 'rwkv_k_k': 3.360369e-02, 'rwkv_k_a': 2.750911e-02, 'rwkv_r_k': 5.083404e-02, 'rwkv_lnx_w': 2.525514e-02, 'rwkv_lnx_b': 5.905310e-02, 'rwkv_v0': 9.451244e-03, 'rwkv_v2': 7.788008e-03, 'ssd_conv_w': 3.084019e-02, 'ssd_conv_b': 4.775271e-02, 'ssd_dt_bias': 8.534491e-02, 'ssd_A_log': 1.080804e-01, 'ssd_D': 2.213661e-01, 'ssd_norm_w': 5.244127e-02, 'hgrn_norm_w': 2.918211e-02, 'w_out': 6.465777e-02, 'ln1_w': 1.172465e+00, 'ln1_b': 3.450164e-01, 'w_up': 3.039841e-02, 'w_down': 1.435945e-01, 'ln2_w': 1.148679e+01, 'ln2_b': 2.607336e+00}


def _to_microbatches(a, axis):
    t = _jnp.moveaxis(a, axis, 0)
    t = t.reshape((N_MICROBATCH, t.shape[0] // N_MICROBATCH) + t.shape[1:])
    return _jnp.moveaxis(t, 1, axis + 1)


def setup_inputs(seed: int = 0) -> dict:
    inp = _fwd_setup_inputs(seed)
    key = _jax.random.fold_in(_jax.random.key(seed), 7919)
    shape, _ = _output_shape()
    out = dict(inp)
    out["loss_target"] = _jax.random.normal(_jax.random.fold_in(key, 0), shape, _jnp.float32)
    for i, name in enumerate(TWIN_WEIGHTS):
        w = inp[name].astype(_jnp.float32)
        if MOMENT_SCALE is None:
            s = _jnp.sqrt(_jnp.mean(_jnp.square(w)) + 1e-30)
        else:
            s = MOMENT_SCALE[name]
        km, kv = _jax.random.split(_jax.random.fold_in(key, i + 1))
        out[name] = w
        out["m_" + name] = s * _jax.random.normal(km, w.shape, _jnp.float32)
        out["v_" + name] = (s * s) * _jax.random.uniform(kv, w.shape, _jnp.float32, 0.5, 1.5)
    if N_MICROBATCH > 1:
        for name, axis in PER_EXAMPLE_BATCH_AXIS.items():
            out[name] = _to_microbatches(out[name], axis)
    return {'x': out['x'], 'lower_bounds': out['lower_bounds'], 'w_in': out['w_in'], 'w_in_vres': out['w_in_vres'], 'mu_shift': out['mu_shift'], 'mu_vres': out['mu_vres'], 'rwkv_w0': out['rwkv_w0'], 'rwkv_w2': out['rwkv_w2'], 'rwkv_a0': out['rwkv_a0'], 'rwkv_a2': out['rwkv_a2'], 'rwkv_g2': out['rwkv_g2'], 'rwkv_k_k': out['rwkv_k_k'], 'rwkv_k_a': out['rwkv_k_a'], 'rwkv_r_k': out['rwkv_r_k'], 'rwkv_lnx_w': out['rwkv_lnx_w'], 'rwkv_lnx_b': out['rwkv_lnx_b'], 'rwkv_v0': out['rwkv_v0'], 'rwkv_v2': out['rwkv_v2'], 'ssd_conv_w': out['ssd_conv_w'], 'ssd_conv_b': out['ssd_conv_b'], 'ssd_dt_bias': out['ssd_dt_bias'], 'ssd_A_log': out['ssd_A_log'], 'ssd_D': out['ssd_D'], 'ssd_norm_w': out['ssd_norm_w'], 'hgrn_norm_w': out['hgrn_norm_w'], 'w_out': out['w_out'], 'ln1_w': out['ln1_w'], 'ln1_b': out['ln1_b'], 'w_up': out['w_up'], 'w_down': out['w_down'], 'ln2_w': out['ln2_w'], 'ln2_b': out['ln2_b'], 'loss_target': out['loss_target'], 'm_lower_bounds': out['m_lower_bounds'], 'm_w_in': out['m_w_in'], 'm_w_in_vres': out['m_w_in_vres'], 'm_mu_shift': out['m_mu_shift'], 'm_mu_vres': out['m_mu_vres'], 'm_rwkv_w0': out['m_rwkv_w0'], 'm_rwkv_w2': out['m_rwkv_w2'], 'm_rwkv_a0': out['m_rwkv_a0'], 'm_rwkv_a2': out['m_rwkv_a2'], 'm_rwkv_g2': out['m_rwkv_g2'], 'm_rwkv_k_k': out['m_rwkv_k_k'], 'm_rwkv_k_a': out['m_rwkv_k_a'], 'm_rwkv_r_k': out['m_rwkv_r_k'], 'm_rwkv_lnx_w': out['m_rwkv_lnx_w'], 'm_rwkv_lnx_b': out['m_rwkv_lnx_b'], 'm_rwkv_v0': out['m_rwkv_v0'], 'm_rwkv_v2': out['m_rwkv_v2'], 'm_ssd_conv_w': out['m_ssd_conv_w'], 'm_ssd_conv_b': out['m_ssd_conv_b'], 'm_ssd_dt_bias': out['m_ssd_dt_bias'], 'm_ssd_A_log': out['m_ssd_A_log'], 'm_ssd_D': out['m_ssd_D'], 'm_ssd_norm_w': out['m_ssd_norm_w'], 'm_hgrn_norm_w': out['m_hgrn_norm_w'], 'm_w_out': out['m_w_out'], 'm_ln1_w': out['m_ln1_w'], 'm_ln1_b': out['m_ln1_b'], 'm_w_up': out['m_w_up'], 'm_w_down': out['m_w_down'], 'm_ln2_w': out['m_ln2_w'], 'm_ln2_b': out['m_ln2_b'], 'v_lower_bounds': out['v_lower_bounds'], 'v_w_in': out['v_w_in'], 'v_w_in_vres': out['v_w_in_vres'], 'v_mu_shift': out['v_mu_shift'], 'v_mu_vres': out['v_mu_vres'], 'v_rwkv_w0': out['v_rwkv_w0'], 'v_rwkv_w2': out['v_rwkv_w2'], 'v_rwkv_a0': out['v_rwkv_a0'], 'v_rwkv_a2': out['v_rwkv_a2'], 'v_rwkv_g2': out['v_rwkv_g2'], 'v_rwkv_k_k': out['v_rwkv_k_k'], 'v_rwkv_k_a': out['v_rwkv_k_a'], 'v_rwkv_r_k': out['v_rwkv_r_k'], 'v_rwkv_lnx_w': out['v_rwkv_lnx_w'], 'v_rwkv_lnx_b': out['v_rwkv_lnx_b'], 'v_rwkv_v0': out['v_rwkv_v0'], 'v_rwkv_v2': out['v_rwkv_v2'], 'v_ssd_conv_w': out['v_ssd_conv_w'], 'v_ssd_conv_b': out['v_ssd_conv_b'], 'v_ssd_dt_bias': out['v_ssd_dt_bias'], 'v_ssd_A_log': out['v_ssd_A_log'], 'v_ssd_D': out['v_ssd_D'], 'v_ssd_norm_w': out['v_ssd_norm_w'], 'v_hgrn_norm_w': out['v_hgrn_norm_w'], 'v_w_out': out['v_w_out'], 'v_ln1_w': out['v_ln1_w'], 'v_ln1_b': out['v_ln1_b'], 'v_w_up': out['v_w_up'], 'v_w_down': out['v_w_down'], 'v_ln2_w': out['v_ln2_w'], 'v_ln2_b': out['v_ln2_b']}


def _loss(weights, diff, rest, loss_target):
    with _jax.named_scope("forward"):
        args = {**rest, TWIN_DIFF_INPUT: diff, **{k: w.astype(_WEIGHT_DTYPES[k]) for k, w in weights.items()}}
        y = _forward(args)
    with _jax.named_scope("loss_head"):
        err = _jnp.square(y.astype(_jnp.float32) - loss_target)
        return 0.5 * _jnp.sum(_jnp.mean(err, axis=-1)) if err.ndim else 0.5 * err


def _adamw(w, g, m, v):
    m = ADAM_B1 * m + (1.0 - ADAM_B1) * g
    v = ADAM_B2 * v + (1.0 - ADAM_B2) * _jnp.square(g)
    m_hat = m / (1.0 - ADAM_B1 ** ADAM_STEP)
    v_hat = v / (1.0 - ADAM_B2 ** ADAM_STEP)
    delta = -ADAM_LR * (m_hat / (_jnp.sqrt(v_hat) + ADAM_EPS) + ADAM_WD * w)
    return delta, m, v


def reference(x, lower_bounds, w_in, w_in_vres, mu_shift, mu_vres, rwkv_w0, rwkv_w2, rwkv_a0, rwkv_a2, rwkv_g2, rwkv_k_k, rwkv_k_a, rwkv_r_k, rwkv_lnx_w, rwkv_lnx_b, rwkv_v0, rwkv_v2, ssd_conv_w, ssd_conv_b, ssd_dt_bias, ssd_A_log, ssd_D, ssd_norm_w, hgrn_norm_w, w_out, ln1_w, ln1_b, w_up, w_down, ln2_w, ln2_b, loss_target, m_lower_bounds, m_w_in, m_w_in_vres, m_mu_shift, m_mu_vres, m_rwkv_w0, m_rwkv_w2, m_rwkv_a0, m_rwkv_a2, m_rwkv_g2, m_rwkv_k_k, m_rwkv_k_a, m_rwkv_r_k, m_rwkv_lnx_w, m_rwkv_lnx_b, m_rwkv_v0, m_rwkv_v2, m_ssd_conv_w, m_ssd_conv_b, m_ssd_dt_bias, m_ssd_A_log, m_ssd_D, m_ssd_norm_w, m_hgrn_norm_w, m_w_out, m_ln1_w, m_ln1_b, m_w_up, m_w_down, m_ln2_w, m_ln2_b, v_lower_bounds, v_w_in, v_w_in_vres, v_mu_shift, v_mu_vres, v_rwkv_w0, v_rwkv_w2, v_rwkv_a0, v_rwkv_a2, v_rwkv_g2, v_rwkv_k_k, v_rwkv_k_a, v_rwkv_r_k, v_rwkv_lnx_w, v_rwkv_lnx_b, v_rwkv_v0, v_rwkv_v2, v_ssd_conv_w, v_ssd_conv_b, v_ssd_dt_bias, v_ssd_A_log, v_ssd_D, v_ssd_norm_w, v_hgrn_norm_w, v_w_out, v_ln1_w, v_ln1_b, v_w_up, v_w_down, v_ln2_w, v_ln2_b):
    given = dict(x=x, lower_bounds=lower_bounds, w_in=w_in, w_in_vres=w_in_vres, mu_shift=mu_shift, mu_vres=mu_vres, rwkv_w0=rwkv_w0, rwkv_w2=rwkv_w2, rwkv_a0=rwkv_a0, rwkv_a2=rwkv_a2, rwkv_g2=rwkv_g2, rwkv_k_k=rwkv_k_k, rwkv_k_a=rwkv_k_a, rwkv_r_k=rwkv_r_k, rwkv_lnx_w=rwkv_lnx_w, rwkv_lnx_b=rwkv_lnx_b, rwkv_v0=rwkv_v0, rwkv_v2=rwkv_v2, ssd_conv_w=ssd_conv_w, ssd_conv_b=ssd_conv_b, ssd_dt_bias=ssd_dt_bias, ssd_A_log=ssd_A_log, ssd_D=ssd_D, ssd_norm_w=ssd_norm_w, hgrn_norm_w=hgrn_norm_w, w_out=w_out, ln1_w=ln1_w, ln1_b=ln1_b, w_up=w_up, w_down=w_down, ln2_w=ln2_w, ln2_b=ln2_b, loss_target=loss_target, m_lower_bounds=m_lower_bounds, m_w_in=m_w_in, m_w_in_vres=m_w_in_vres, m_mu_shift=m_mu_shift, m_mu_vres=m_mu_vres, m_rwkv_w0=m_rwkv_w0, m_rwkv_w2=m_rwkv_w2, m_rwkv_a0=m_rwkv_a0, m_rwkv_a2=m_rwkv_a2, m_rwkv_g2=m_rwkv_g2, m_rwkv_k_k=m_rwkv_k_k, m_rwkv_k_a=m_rwkv_k_a, m_rwkv_r_k=m_rwkv_r_k, m_rwkv_lnx_w=m_rwkv_lnx_w, m_rwkv_lnx_b=m_rwkv_lnx_b, m_rwkv_v0=m_rwkv_v0, m_rwkv_v2=m_rwkv_v2, m_ssd_conv_w=m_ssd_conv_w, m_ssd_conv_b=m_ssd_conv_b, m_ssd_dt_bias=m_ssd_dt_bias, m_ssd_A_log=m_ssd_A_log, m_ssd_D=m_ssd_D, m_ssd_norm_w=m_ssd_norm_w, m_hgrn_norm_w=m_hgrn_norm_w, m_w_out=m_w_out, m_ln1_w=m_ln1_w, m_ln1_b=m_ln1_b, m_w_up=m_w_up, m_w_down=m_w_down, m_ln2_w=m_ln2_w, m_ln2_b=m_ln2_b, v_lower_bounds=v_lower_bounds, v_w_in=v_w_in, v_w_in_vres=v_w_in_vres, v_mu_shift=v_mu_shift, v_mu_vres=v_mu_vres, v_rwkv_w0=v_rwkv_w0, v_rwkv_w2=v_rwkv_w2, v_rwkv_a0=v_rwkv_a0, v_rwkv_a2=v_rwkv_a2, v_rwkv_g2=v_rwkv_g2, v_rwkv_k_k=v_rwkv_k_k, v_rwkv_k_a=v_rwkv_k_a, v_rwkv_r_k=v_rwkv_r_k, v_rwkv_lnx_w=v_rwkv_lnx_w, v_rwkv_lnx_b=v_rwkv_lnx_b, v_rwkv_v0=v_rwkv_v0, v_rwkv_v2=v_rwkv_v2, v_ssd_conv_w=v_ssd_conv_w, v_ssd_conv_b=v_ssd_conv_b, v_ssd_dt_bias=v_ssd_dt_bias, v_ssd_A_log=v_ssd_A_log, v_ssd_D=v_ssd_D, v_ssd_norm_w=v_ssd_norm_w, v_hgrn_norm_w=v_hgrn_norm_w, v_w_out=v_w_out, v_ln1_w=v_ln1_w, v_ln1_b=v_ln1_b, v_w_up=v_w_up, v_w_down=v_w_down, v_ln2_w=v_ln2_w, v_ln2_b=v_ln2_b)
    weights = {n: given[n] for n in TWIN_WEIGHTS}
    shared = {n: given[n] for n in SHARED_INPUTS}
    per_example = {n: given[n] for n in ['x']}
    grad_fn = _jax.value_and_grad(_loss, argnums=(0, 1))

    def one_microbatch(ex, loss_target):
        ex = dict(ex)
        diff = ex.pop(TWIN_DIFF_INPUT)
        return grad_fn(weights, diff, {**shared, **ex}, loss_target)

    if N_MICROBATCH == 1:
        loss, (grad_w, grad_x) = one_microbatch(per_example, given["loss_target"])
    else:
        def body(carry, xs):
            loss_sum, grad_sum = carry
            l_k, (gw_k, gx_k) = one_microbatch(xs[0], xs[1])
            with _jax.named_scope("update"):
                return (loss_sum + l_k, _jax.tree.map(_jnp.add, grad_sum, gw_k)), gx_k

        init = (_jnp.zeros((), _jnp.float32), _jax.tree.map(_jnp.zeros_like, weights))
        (loss, grad_w), grad_x = _jax.lax.scan(body, init, (per_example, given["loss_target"]))
    with _jax.named_scope("update"):
        delta_w, new_m, new_v = {}, {}, {}
        for n in TWIN_WEIGHTS:
            delta_w[n], new_m[n], new_v[n] = _adamw(weights[n], grad_w[n], given["m_" + n], given["v_" + n])
    return (loss, grad_x, *[grad_w[n] for n in TWIN_WEIGHTS], *[delta_w[n] for n in TWIN_WEIGHTS],
            *[new_m[n] for n in TWIN_WEIGHTS], *[new_v[n] for n in TWIN_WEIGHTS])
```

```python
import functools
import math

import jax
import jax.numpy as jnp
from jax import lax
from jax.experimental import pallas as pl
from jax.experimental.pallas import tpu as pltpu

F32 = jnp.float32
BF16 = jnp.bfloat16
HI = lax.Precision.HIGHEST

DEPTH = 2
D_MODEL = 1024
D_GROUP = 256
HEAD_DIM = 64
N_HEADS = 4
SSD_STATE = 128
SSD_XBC = 768
SSD_CONV = 4
D_FF = 4096
ALPHA = (2.0 * DEPTH) ** 0.25
LN_EPS = 1e-5
RMS_EPS = 1e-5
RWKV_GN_EPS = HEAD_DIM * 1e-5
DILATED_BRANCHES = ((128, 1), (512, 4), (2048, 16))
ALIBI_SLOPES = tuple(2.0 ** (-8.0 * (h + 1) / N_HEADS) for h in range(N_HEADS))
ATTN_BLK = 128

ADAM_LR, ADAM_B1, ADAM_B2, ADAM_EPS, ADAM_WD, ADAM_STEP = 0.001, 0.9, 0.999, 1e-08, 0.01, 10

IN_COLS = 3716
PROJ_W = 4096
SEG_HGRN, SEG_RWKV, SEG_Q, SEG_Z, SEG_XBC, SEG_DT = 0, 1024, 2048, 2816, 3072, 3840
_PIECES = ((0, 896, SEG_RWKV), (896, 768, SEG_Q), (1664, 256, SEG_Z), (1920, 768, SEG_XBC),
           (2688, 4, SEG_DT), (2692, 1024, SEG_HGRN))
VRES_COL = SEG_RWKV + 896

ROW_TILE = 256
SCAN_CHUNK = 128
VMEM_LIMIT = 48 * 1024 * 1024


def _cparams(sem=None):
    if sem is None:
        return pltpu.CompilerParams(vmem_limit_bytes=VMEM_LIMIT)
    return pltpu.CompilerParams(dimension_semantics=sem, vmem_limit_bytes=VMEM_LIMIT)


def _pick(n, pref):
    for t in pref:
        if n % t == 0:
            return t
    return n


def matmul(a, b, mode, name, add=None, out_dtype=F32):
    if mode == "nn":
        (M, K), (_, N) = a.shape, b.shape
    elif mode == "nt":
        (M, K), (N, _) = a.shape, b.shape
    else:
        (K, M), (_, N) = a.shape, b.shape
    tm, tn, tk = _pick(M, (512, 256, 128)), _pick(N, (512, 256, 128)), _pick(K, (512, 256, 128))
    nk = K // tk
    dims = {"nn": (((1,), (0,)), ((), ())), "nt": (((1,), (1,)), ((), ())), "tn": (((0,), (0,)), ((), ()))}[mode]

    def body(*refs):
        if add is None:
            a_ref, b_ref, o_ref, acc = refs
            add_ref = None
        else:
            a_ref, b_ref, add_ref, o_ref, acc = refs
        k = pl.program_id(2)

        @pl.when(k == 0)
        def _():
            acc[...] = jnp.zeros_like(acc)

        acc[...] += lax.dot_general(a_ref[...].astype(BF16), b_ref[...].astype(BF16), dims,
                                    preferred_element_type=F32)

        @pl.when(k == nk - 1)
        def _():
            r = acc[...]
            if add_ref is not None:
                r = r + add_ref[...]
            o_ref[...] = r.astype(o_ref.dtype)

    a_spec = pl.BlockSpec((tk, tm), lambda i, j, k: (k, i)) if mode == "tn" else pl.BlockSpec((tm, tk), lambda i, j, k: (i, k))
    b_spec = pl.BlockSpec((tn, tk), lambda i, j, k: (j, k)) if mode == "nt" else pl.BlockSpec((tk, tn), lambda i, j, k: (k, j))
    o_spec = pl.BlockSpec((tm, tn), lambda i, j, k: (i, j))
    ins, specs = [a, b], [a_spec, b_spec]
    if add is not None:
        ins.append(add)
        specs.append(o_spec)
    return pl.pallas_call(
        body, grid=(M // tm, N // tn, nk), in_specs=specs, out_specs=o_spec,
        out_shape=jax.ShapeDtypeStruct((M, N), out_dtype), scratch_shapes=[pltpu.VMEM((tm, tn), F32)],
        compiler_params=_cparams(("parallel", "parallel", "arbitrary")), name=name)(*ins)


def _row_spec(w, tile):
    return pl.BlockSpec((tile, w), lambda i: (i, 0))


def _par_spec(shape):
    return pl.BlockSpec(shape, lambda i: (0,) * len(shape))


def tl_fwd(fn, name, rows, pars, out_widths, tile=ROW_TILE):
    S = rows[0].shape[0]
    nr = len(rows)

    def body(*refs):
        ins = [r[...] for r in refs[:nr + len(pars)]]
        outs = fn(*ins)
        for o_ref, o in zip(refs[nr + len(pars):], outs):
            o_ref[...] = o

    return pl.pallas_call(
        body, grid=(S // tile,),
        in_specs=[_row_spec(r.shape[1], tile) for r in rows] + [_par_spec(p.shape) for p in pars],
        out_specs=[_row_spec(w, tile) for w in out_widths],
        out_shape=[jax.ShapeDtypeStruct((S, w), F32) for w in out_widths],
        compiler_params=_cparams(("parallel",)), name=name)(*rows, *pars)


def tl_bwd(fn, name, rows, pars, cts, tile=ROW_TILE, row_grad=None):
    S = rows[0].shape[0]
    nr, npar = len(rows), len(pars)
    row_grad = [True] * nr if row_grad is None else row_grad
    flat_cts = [c for group in cts for c in group]
    ncts = len(flat_cts)
    gi = [i for i in range(nr) if row_grad[i]]

    def body(*refs):
        row_v = [r[...] for r in refs[:nr]]
        par_v = [r[...] for r in refs[nr:nr + npar]]
        ct_refs = refs[nr + npar:nr + npar + ncts]
        out_refs = refs[nr + npar + ncts:]
        ct_v, pos = [], 0
        for group in cts:
            acc = ct_refs[pos][...]
            for q in range(1, len(group)):
                acc = acc + ct_refs[pos + q][...]
            pos += len(group)
            ct_v.append(acc)

        def f(diff_rows, par_vals):
            full = list(row_v)
            for idx, val in zip(gi, diff_rows):
                full[idx] = val
            return tuple(fn(*full, *par_vals))

        _, vjp = jax.vjp(f, [row_v[i] for i in gi], par_v)
        d_rows, d_pars = vjp(tuple(ct_v))
        for o_ref, g in zip(out_refs[:len(gi)], d_rows):
            o_ref[...] = g
        first = pl.program_id(0) == 0
        for o_ref, g in zip(out_refs[len(gi):], d_pars):
            @pl.when(first)
            def _(o_ref=o_ref):
                o_ref[...] = jnp.zeros_like(o_ref)
            o_ref[...] += g

    outs = pl.pallas_call(
        body, grid=(S // tile,),
        in_specs=[_row_spec(r.shape[1], tile) for r in rows] + [_par_spec(p.shape) for p in pars]
        + [_row_spec(c.shape[1], tile) for c in flat_cts],
        out_specs=[_row_spec(rows[i].shape[1], tile) for i in gi] + [_par_spec(p.shape) for p in pars],
        out_shape=[jax.ShapeDtypeStruct(rows[i].shape, F32) for i in gi] + [jax.ShapeDtypeStruct(p.shape, F32) for p in pars],
        compiler_params=_cparams(("arbitrary",)), name=name)(*rows, *pars, *flat_cts)
    return list(outs[:len(gi)]), list(outs[len(gi):])


def _shift_rows(x, j):
    if j == 0:
        return x
    rolled = pltpu.roll(x, j, 0)
    row = lax.broadcasted_iota(jnp.int32, x.shape, 0)
    return jnp.where(row >= j, rolled, 0.0)


def _unshift_rows(x, j):
    if j == 0:
        return x
    S = x.shape[0]
    rolled = pltpu.roll(x, S - j, 0)
    row = lax.broadcasted_iota(jnp.int32, x.shape, 0)
    return jnp.where(row < S - j, rolled, 0.0)


def fir_fwd(x, taps, name):
    S, C = x.shape
    K = taps.shape[0]

    def body(x_ref, w_ref, y_ref):
        xv = x_ref[...]
        acc = jnp.zeros_like(xv)
        for k in range(K):
            acc = acc + _shift_rows(xv, K - 1 - k) * w_ref[pl.ds(k, 1), :]
        y_ref[...] = acc

    cs = pl.BlockSpec((S, 128), lambda j: (0, j))
    return pl.pallas_call(body, grid=(C // 128,), in_specs=[cs, pl.BlockSpec((K, 128), lambda j: (0, j))],
                          out_specs=cs, out_shape=jax.ShapeDtypeStruct((S, C), F32),
                          compiler_params=_cparams(("parallel",)), name=name)(x, taps)


def fir_bwd(x, taps, dy_list, name):
    S, C = x.shape
    K = taps.shape[0]
    n = len(dy_list)

    def body(*refs):
        x_ref, w_ref = refs[:2]
        dy = refs[2][...]
        for q in range(1, n):
            dy = dy + refs[2 + q][...]
        dx_ref, dw_ref, db_ref = refs[2 + n:]
        xv = x_ref[...]
        dx = jnp.zeros_like(xv)
        for k in range(K):
            j = K - 1 - k
            dx = dx + _unshift_rows(dy, j) * w_ref[pl.ds(k, 1), :]
            dw_ref[pl.ds(k, 1), :] = jnp.sum(dy * _shift_rows(xv, j), axis=0, keepdims=True)
        dx_ref[...] = dx
        db_ref[...] = jnp.sum(dy, axis=0, keepdims=True)

    cs = pl.BlockSpec((S, 128), lambda j: (0, j))
    ks = pl.BlockSpec((K, 128), lambda j: (0, j))
    bs = pl.BlockSpec((1, 128), lambda j: (0, j))
    return pl.pallas_call(body, grid=(C // 128,), in_specs=[cs, ks] + [cs] * n, out_specs=[cs, ks, bs],
                          out_shape=[jax.ShapeDtypeStruct((S, C), F32), jax.ShapeDtypeStruct((K, C), F32),
                                     jax.ShapeDtypeStruct((1, C), F32)],
                          compiler_params=_cparams(("parallel",)), name=name)(x, taps, *dy_list)


def _col(tile, lane, t):
    return jnp.sum(jnp.where(lane == t, tile, 0.0), axis=1, keepdims=True)


def scan_fwd(r, w, k, vT, kk, a, name):
    delta = kk is not None
    H, S, Dk = r.shape
    Dv = vT.shape[1]
    Tc = SCAN_CHUNK
    nc = S // Tc
    rows = [r, w, k] + ([kk, a] if delta else [])
    nrow = len(rows)

    def body(*refs):
        row_refs = refs[:nrow]
        vT_ref, yT_ref, sb_ref, s_ref = refs[nrow:]
        c = pl.program_id(1)

        @pl.when(c == 0)
        def _():
            s_ref[...] = jnp.zeros_like(s_ref)

        sb_ref[0, 0] = s_ref[...]
        vt = vT_ref[0]
        lane = lax.broadcasted_iota(jnp.int32, (Dv, Tc), 1)

        def step(t, carry):
            s, y = carry
            rv = [ref[0, pl.ds(t, 1), :] for ref in row_refs]
            vcol = _col(vt, lane, t)
            if delta:
                sa = jnp.sum(s * (-rv[3]), axis=1, keepdims=True)
                s = s * rv[1] + sa * (rv[3] * rv[4]) + vcol * rv[2]
            else:
                s = s * rv[1] + vcol * rv[2]
            ycol = jnp.sum(s * rv[0], axis=1, keepdims=True)
            return s, jnp.where(lane == t, ycol, y)

        s, y = lax.fori_loop(0, Tc, step, (s_ref[...], jnp.zeros((Dv, Tc), F32)))
        s_ref[...] = s
        yT_ref[0] = y

    rs = pl.BlockSpec((1, Tc, Dk), lambda h, c: (h, c, 0))
    vs = pl.BlockSpec((1, Dv, Tc), lambda h, c: (h, 0, c))
    yT, sb = pl.pallas_call(
        body, grid=(H, nc), in_specs=[rs] * nrow + [vs],
        out_specs=[vs, pl.BlockSpec((1, 1, Dv, Dk), lambda h, c: (h, c, 0, 0))],
        out_shape=[jax.ShapeDtypeStruct((H, Dv, S), F32), jax.ShapeDtypeStruct((H, nc, Dv, Dk), F32)],
        scratch_shapes=[pltpu.VMEM((Dv, Dk), F32)],
        compiler_params=_cparams(("parallel", "arbitrary")), name=name)(*rows, vT)
    return yT, sb


def scan_bwd(r, w, k, vT, kk, a, sb, dyT, name):
    delta = kk is not None
    H, S, Dk = r.shape
    Dv = vT.shape[1]
    Tc = SCAN_CHUNK
    nc = S // Tc
    rows = [r, w, k] + ([kk, a] if delta else [])
    nrow = len(rows)

    def body(*refs):
        row_refs = refs[:nrow]
        vT_ref, dyT_ref, sb_ref = refs[nrow:nrow + 3]
        drow_refs = refs[nrow + 3:2 * nrow + 3]
        dvT_ref, sbuf, ds_ref = refs[2 * nrow + 3:]
        c = pl.program_id(1)

        @pl.when(c == 0)
        def _():
            ds_ref[...] = jnp.zeros_like(ds_ref)

        vt = vT_ref[0]
        dyt = dyT_ref[0]
        lane = lax.broadcasted_iota(jnp.int32, (Dv, Tc), 1)

        def new_state(s, rv, vcol):
            if delta:
                sa = jnp.sum(s * (-rv[3]), axis=1, keepdims=True)
                return s * rv[1] + sa * (rv[3] * rv[4]) + vcol * rv[2], sa
            return s * rv[1] + vcol * rv[2], None

        def fstep(t, s):
            sbuf[t] = s
            rv = [ref[0, pl.ds(t, 1), :] for ref in row_refs]
            return new_state(s, rv, _col(vt, lane, t))[0]

        lax.fori_loop(0, Tc, fstep, sb_ref[0, 0])

        def bstep(i, carry):
            ds, dv = carry
            t = Tc - 1 - i
            sp = sbuf[t]
            rv = [ref[0, pl.ds(t, 1), :] for ref in row_refs]
            vcol = _col(vt, lane, t)
            dycol = _col(dyt, lane, t)
            st, sa = new_state(sp, rv, vcol)
            drow_refs[0][0, pl.ds(t, 1), :] = jnp.sum(st * dycol, axis=0, keepdims=True)
            g = ds + dycol * rv[0]
            drow_refs[1][0, pl.ds(t, 1), :] = jnp.sum(g * sp, axis=0, keepdims=True)
            drow_refs[2][0, pl.ds(t, 1), :] = jnp.sum(g * vcol, axis=0, keepdims=True)
            dvcol = jnp.sum(g * rv[2], axis=1, keepdims=True)
            ds_prev = g * rv[1]
            if delta:
                b = rv[3] * rv[4]
                dsa = jnp.sum(g * b, axis=1, keepdims=True)
                db = jnp.sum(g * sa, axis=0, keepdims=True)
                dnkk = jnp.sum(sp * dsa, axis=0, keepdims=True)
                drow_refs[3][0, pl.ds(t, 1), :] = db * rv[4] - dnkk
                drow_refs[4][0, pl.ds(t, 1), :] = db * rv[3]
                ds_prev = ds_prev - dsa * rv[3]
            return ds_prev, jnp.where(lane == t, dvcol, dv)

        ds, dv = lax.fori_loop(0, Tc, bstep, (ds_ref[...], jnp.zeros((Dv, Tc), F32)))
        ds_ref[...] = ds
        dvT_ref[0] = dv

    rs = pl.BlockSpec((1, Tc, Dk), lambda h, c: (h, nc - 1 - c, 0))
    vs = pl.BlockSpec((1, Dv, Tc), lambda h, c: (h, 0, nc - 1 - c))
    outs = pl.pallas_call(
        body, grid=(H, nc),
        in_specs=[rs] * nrow + [vs, vs, pl.BlockSpec((1, 1, Dv, Dk), lambda h, c: (h, nc - 1 - c, 0, 0))],
        out_specs=[rs] * nrow + [vs],
        out_shape=[jax.ShapeDtypeStruct((H, S, Dk), F32)] * nrow + [jax.ShapeDtypeStruct((H, Dv, S), F32)],
        scratch_shapes=[pltpu.VMEM((Tc, Dv, Dk), F32), pltpu.VMEM((Dv, Dk), F32)],
        compiler_params=_cparams(("parallel", "arbitrary")), name=name)(*rows, vT, dyT, sb)
    return list(outs[:nrow]), outs[nrow]


def _to_heads(x, dk):
    S = x.shape[0]
    return jnp.transpose(x.reshape(S, N_HEADS, dk), (1, 0, 2))


def _from_heads(x):
    H, S, dk = x.shape
    return jnp.transpose(x, (1, 0, 2)).reshape(S, H * dk)


def _to_headsT(x):
    S = x.shape[0]
    return jnp.transpose(x.reshape(S, N_HEADS, HEAD_DIM), (1, 2, 0))


def _from_headsT(x):
    H, dv, S = x.shape
    return jnp.transpose(x, (2, 0, 1)).reshape(S, H * dv)


def run_scan_fwd(R, W, K, V, KK, A, dk, name):
    yT, sb = scan_fwd(_to_heads(R, dk), _to_heads(W, dk), _to_heads(K, dk), _to_headsT(V),
                      None if KK is None else _to_heads(KK, dk), None if A is None else _to_heads(A, dk), name)
    return _from_headsT(yT), sb


def run_scan_bwd(R, W, K, V, KK, A, sb, dY, dk, name):
    drows, dvT = scan_bwd(_to_heads(R, dk), _to_heads(W, dk), _to_heads(K, dk), _to_headsT(V),
                          None if KK is None else _to_heads(KK, dk), None if A is None else _to_heads(A, dk),
                          sb, _to_headsT(dY), name)
    return [_from_heads(d) for d in drows], _from_headsT(dvT)


def _attn_block(q, kp, kc, vp, vc, n, slope, dilation):
    blk = ATTN_BLK
    k2 = jnp.concatenate([kp, kc], axis=0)
    v2 = jnp.concatenate([vp, vc], axis=0)
    s = lax.dot_general(q, k2, (((1,), (1,)), ((), ())), precision=HI, preferred_element_type=F32) * (HEAD_DIM ** -0.5)
    i = lax.broadcasted_iota(jnp.int32, (blk, 2 * blk), 0)
    j = lax.broadcasted_iota(jnp.int32, (blk, 2 * blk), 1)
    dist = blk + i - j
    first_key = jnp.where(n > 0, 0, blk)
    valid = (dist >= 0) & (dist <= blk) & (j >= first_key)
    s = s - slope * (dist * dilation).astype(F32)
    s = jnp.where(valid, s, -1e30)
    m = jnp.max(s, axis=-1, keepdims=True)
    p = jnp.exp(s - m)
    l = jnp.sum(p, axis=-1, keepdims=True)
    o = jnp.dot(p, v2, precision=HI, preferred_element_type=F32) / l
    lse = jnp.broadcast_to(m + jnp.log(l), o.shape)
    return o, lse


def _slope_of(h):
    s = jnp.float32(ALIBI_SLOPES[N_HEADS - 1])
    for hh in range(N_HEADS - 2, -1, -1):
        s = jnp.where(h == hh, jnp.float32(ALIBI_SLOPES[hh]), s)
    return s


def _attn_specs(nb):
    blk = ATTN_BLK
    cur = pl.BlockSpec((1, 1, blk, HEAD_DIM), lambda h, z, n: (h, z, n, 0))
    prev = pl.BlockSpec((1, 1, blk, HEAD_DIM), lambda h, z, n: (h, z, jnp.maximum(n - 1, 0), 0))
    return cur, prev


def attn_fwd(q, k, v, dilation, name):
    H, Z, L, _ = q.shape
    nb = L // ATTN_BLK
    cur, prev = _attn_specs(nb)

    def body(q_ref, kp_ref, kc_ref, vp_ref, vc_ref, o_ref, l_ref):
        o, lse = _attn_block(q_ref[0, 0], kp_ref[0, 0], kc_ref[0, 0], vp_ref[0, 0], vc_ref[0, 0],
                             pl.program_id(2), _slope_of(pl.program_id(0)), dilation)
        o_ref[0, 0] = o
        l_ref[0, 0] = lse

    return pl.pallas_call(body, grid=(H, Z, nb), in_specs=[cur, prev, cur, prev, cur], out_specs=[cur, cur],
                          out_shape=[jax.ShapeDtypeStruct(q.shape, F32)] * 2,
                          compiler_params=_cparams(("parallel", "parallel", "arbitrary")), name=name)(q, k, k, v, v)


def attn_bwd(q, k, v, do, dlse, dilation, name):
    H, Z, L, _ = q.shape
    blk = ATTN_BLK
    nb = L // blk
    cur, prev = _attn_specs(nb)
    full = pl.BlockSpec((1, 1, L, HEAD_DIM), lambda h, z, n: (h, z, 0, 0))

    def body(q_ref, kp_ref, kc_ref, vp_ref, vc_ref, do_ref, dl_ref, dq_ref, dk_ref, dv_ref):
        n = pl.program_id(2)
        slope = _slope_of(pl.program_id(0))
        f = lambda q_, kp_, kc_, vp_, vc_: _attn_block(q_, kp_, kc_, vp_, vc_, n, slope, dilation)
        _, vjp = jax.vjp(f, q_ref[0, 0], kp_ref[0, 0], kc_ref[0, 0], vp_ref[0, 0], vc_ref[0, 0])
        dq, dkp, dkc, dvp, dvc = vjp((do_ref[0, 0], dl_ref[0, 0]))
        dq_ref[0, 0] = dq

        @pl.when(n == 0)
        def _():
            dk_ref[...] = jnp.zeros_like(dk_ref)
            dv_ref[...] = jnp.zeros_like(dv_ref)

        here = pl.ds(pl.multiple_of(n * blk, blk), blk)
        dk_ref[0, 0, here, :] += dkc
        dv_ref[0, 0, here, :] += dvc

        @pl.when(n > 0)
        def _():
            before = pl.ds(pl.multiple_of((n - 1) * blk, blk), blk)
            dk_ref[0, 0, before, :] += dkp
            dv_ref[0, 0, before, :] += dvp

    return pl.pallas_call(body, grid=(H, Z, nb), in_specs=[cur, prev, cur, prev, cur, cur, cur],
                          out_specs=[cur, full, full], out_shape=[jax.ShapeDtypeStruct(q.shape, F32)] * 3,
                          compiler_params=_cparams(("parallel", "parallel", "arbitrary")), name=name)(q, k, k, v, v, do, dlse)


def _to_sub(t, d):
    S = t.shape[0]
    return jnp.transpose(t.reshape(S // d, d, N_HEADS, HEAD_DIM), (2, 1, 0, 3))


def _from_sub(t):
    H, d, L, _ = t.shape
    return jnp.transpose(t, (2, 1, 0, 3)).reshape(L * d, H * HEAD_DIM)


def _head_ones(width, group):
    i = lax.broadcasted_iota(jnp.int32, (width, width), 0) // group
    j = lax.broadcasted_iota(jnp.int32, (width, width), 1) // group
    return (i == j).astype(F32)


def _group_sum(x, group):
    return jnp.dot(x, _head_ones(x.shape[1], group), precision=HI, preferred_element_type=F32)


def _spread(width_in, width_out, rep):
    i = lax.broadcasted_iota(jnp.int32, (width_in, width_out), 0)
    j = lax.broadcasted_iota(jnp.int32, (width_in, width_out), 1) // rep
    return (i == j).astype(F32)


def _hdot(a, b):
    return jnp.dot(a, b, precision=HI, preferred_element_type=F32)


def _sigmoid(x):
    return 1.0 / (1.0 + jnp.exp(-x))


def _softplus(x):
    return jnp.maximum(x, 0.0) + jnp.log(1.0 + jnp.exp(jnp.minimum(x, -x)))


def _silu(x):
    return x * _sigmoid(x)


def rwkv_pre(layer):
    def fn(*args):
        if layer == 0:
            fs, w0, w2p, a0, a2p, g2p, k_k, k_a = args
        else:
            fs, vfirst, w0, w2p, a0, a2p, g2p, k_k, k_a, v0, v2p = args
        r, k, v = fs[:, 0:256], fs[:, 256:512], fs[:, 512:768]
        lora = fs[:, 768:896]
        w_log = -_softplus(-(w0 + _hdot(jnp.tanh(lora), w2p))) - 0.5
        decay = jnp.exp(-jnp.exp(w_log))
        a = _sigmoid(a0 + _hdot(lora, a2p))
        g = _hdot(_sigmoid(lora), g2p)
        if layer > 0:
            v = v + (vfirst - v) * _sigmoid(v0 + _hdot(fs[:, 896:1024], v2p))
        kk = k * k_k
        kk = kk / jnp.maximum(jnp.sqrt(_group_sum(kk * kk, HEAD_DIM)), 1e-12)
        k = k * (1.0 + (a - 1.0) * k_a)
        return r, decay, k, v, kk, a, g
    return fn


def rwkv_post(y, r, k, v, g, lnx_w, lnx_b, r_k):
    mu = _group_sum(y, HEAD_DIM) * (1.0 / HEAD_DIM)
    yc = y - mu
    var = _group_sum(yc * yc, HEAD_DIM) * (1.0 / HEAD_DIM)
    yn = yc * lax.rsqrt(var + RWKV_GN_EPS) * lnx_w + lnx_b
    bonus = _group_sum(r * k * r_k, HEAD_DIM) * v
    return ((yn + bonus) * g,)


def attn_combine(o1, o2, o3, l1, l2, l3):
    m = jnp.maximum(jnp.maximum(l1, l2), l3)
    e1, e2, e3 = jnp.exp(l1 - m), jnp.exp(l2 - m), jnp.exp(l3 - m)
    return ((o1 * e1 + o2 * e2 + o3 * e3) / (e1 + e2 + e3),)


def ssd_pre(xc, dtr, conv_b, dt_bias, a_log):
    xbc = _silu(xc + conv_b)
    xs, bm, cm = xbc[:, 0:256], xbc[:, 256:512], xbc[:, 512:768]
    dt = _softplus(dtr + dt_bias)
    a_neg = -jnp.exp(a_log)
    wide = _spread(128, N_HEADS * SSD_STATE, SSD_STATE)
    w = jnp.exp(_hdot(dt, wide) * _hdot(a_neg, wide))
    xdt = xs * _hdot(dt, _spread(128, D_GROUP, HEAD_DIM))
    rr = jnp.concatenate([cm[:, 0:128], cm[:, 0:128], cm[:, 128:256], cm[:, 128:256]], axis=1)
    kk = jnp.concatenate([bm[:, 0:128], bm[:, 0:128], bm[:, 128:256], bm[:, 128:256]], axis=1)
    return rr, w, kk, xdt, xs


def ssd_post(ys, z, xs, d_skip, norm_w):
    y = ys + xs * _hdot(d_skip, _spread(128, D_GROUP, HEAD_DIM))
    y = y * _silu(z)
    half = D_GROUP // 2
    parts = []
    for g in range(2):
        t = y[:, g * half:(g + 1) * half]
        parts.append(t * lax.rsqrt(jnp.mean(t * t, axis=-1, keepdims=True) + RMS_EPS))
    return (jnp.concatenate(parts, axis=1) * norm_w,)


def hgrn_pre(seg, lb):
    q, f, i = seg[:, 0:256], seg[:, 256:512], seg[:, 512:768]
    forget = lb + (1.0 - lb) * _sigmoid(f)
    return _silu(q), forget, 1.0 - forget, i


def hgrn_post(o, seg, norm_w):
    g = seg[:, 768:1024]
    ms = _group_sum(o * o, HEAD_DIM) * (1.0 / HEAD_DIM)
    return (o * lax.rsqrt(ms + RMS_EPS) * norm_w * _silu(g),)


def ln_res(x, y, w, b):
    z = ALPHA * x + y
    mu = jnp.mean(z, axis=-1, keepdims=True)
    zc = z - mu
    var = jnp.mean(zc * zc, axis=-1, keepdims=True)
    return (zc * lax.rsqrt(var + LN_EPS) * w + b,)


def relu2(u):
    r = jnp.maximum(u, 0.0)
    return (r * r,)


def loss_and_grad(y, tgt, name):
    S, D = y.shape
    tile = ROW_TILE

    def body(y_ref, t_ref, l_ref, dy_ref):
        e = y_ref[...] - t_ref[...]
        dy_ref[...] = e * (1.0 / D)

        @pl.when(pl.program_id(0) == 0)
        def _():
            l_ref[...] = jnp.zeros_like(l_ref)

        per_row = 0.5 * jnp.mean(e * e, axis=-1, keepdims=True)
        l_ref[...] += jnp.sum(per_row, axis=0, keepdims=True) * jnp.ones((1, 128), F32)

    return pl.pallas_call(body, grid=(S // tile,), in_specs=[_row_spec(D, tile)] * 2,
                          out_specs=[_par_spec((1, 128)), _row_spec(D, tile)],
                          out_shape=[jax.ShapeDtypeStruct((1, 128), F32), jax.ShapeDtypeStruct((S, D), F32)],
                          compiler_params=_cparams(("arbitrary",)), name=name)(y, tgt)


def add_rows(arrs, name):
    (out,) = tl_fwd(lambda *a: (functools.reduce(lambda p, q: p + q, a),), name, arrs, [], [arrs[0].shape[1]])
    return out


def small_fwd(fn, name, ins, out_shapes):
    n = len(ins)

    def body(*refs):
        outs = fn(*[r[...] for r in refs[:n]])
        for o_ref, o in zip(refs[n:], outs):
            o_ref[...] = o

    return pl.pallas_call(body, out_shape=[jax.ShapeDtypeStruct(s, F32) for s in out_shapes], name=name)(*ins)


def small_bwd(fn, name, ins, cts):
    n, m = len(ins), len(cts)

    def body(*refs):
        _, vjp = jax.vjp(lambda *a: tuple(fn(*a)), *[r[...] for r in refs[:n]])
        grads = vjp(tuple(r[...] for r in refs[n:n + m]))
        for o_ref, g in zip(refs[n + m:], grads):
            o_ref[...] = g

    return pl.pallas_call(body, out_shape=[jax.ShapeDtypeStruct(a.shape, F32) for a in ins], name=name)(*ins, *cts)


def param_prep(lower_bounds, mu0, mu1):
    e = jnp.exp(lower_bounds - jnp.max(lower_bounds, axis=0, keepdims=True))
    sm = e / jnp.sum(e, axis=0, keepdims=True)
    lb0 = sm[0:1] - sm[0:1]
    lb1 = sm[0:1] + sm[1:2] - sm[0:1]
    return lb0, lb1, mu0, 1.0 - mu0, mu1, 1.0 - mu1


def _rows_tile(rows):
    return _pick(rows, (256, 128, 64, 32, 16, 8))


def sum_parts(parts, name):
    P, rows, cols = parts.shape
    tile = _rows_tile(rows)

    def body(p_ref, o_ref):
        acc = p_ref[0]
        for p in range(1, P):
            acc = acc + p_ref[p]
        o_ref[...] = acc

    return pl.pallas_call(body, grid=(rows // tile,), in_specs=[pl.BlockSpec((P, tile, cols), lambda i: (0, i, 0))],
                          out_specs=pl.BlockSpec((tile, cols), lambda i: (i, 0)),
                          out_shape=jax.ShapeDtypeStruct((rows, cols), F32),
                          compiler_params=_cparams(("parallel",)), name=name)(parts)


def adamw(w, g, m, v, name):
    rows, cols = w.shape
    tile = _rows_tile(rows)

    def body(w_ref, g_ref, m_ref, v_ref, d_ref, nm_ref, nv_ref):
        gv = g_ref[...]
        nm = ADAM_B1 * m_ref[...] + (1.0 - ADAM_B1) * gv
        nv = ADAM_B2 * v_ref[...] + (1.0 - ADAM_B2) * jnp.square(gv)
        m_hat = nm / (1.0 - ADAM_B1 ** ADAM_STEP)
        v_hat = nv / (1.0 - ADAM_B2 ** ADAM_STEP)
        d_ref[...] = -ADAM_LR * (m_hat / (jnp.sqrt(v_hat) + ADAM_EPS) + ADAM_WD * w_ref[...])
        nm_ref[...] = nm
        nv_ref[...] = nv

    spec = pl.BlockSpec((tile, cols), lambda i: (i, 0))
    return pl.pallas_call(body, grid=(rows // tile,), in_specs=[spec] * 4, out_specs=[spec] * 3,
                          out_shape=[jax.ShapeDtypeStruct((rows, cols), F32)] * 3,
                          compiler_params=_cparams(("parallel",)), name=name)(w, g, m, v)


MESH = pl.DeviceIdType.MESH
ANY = pl.BlockSpec(memory_space=pl.ANY)


def _flip(v, bit):
    return 1 - v if bit else v


def gather_chips(arrs, name):
    n = len(arrs)

    def body(*refs):
        ins, outs = refs[:n], refs[n:2 * n]
        send, recv, loc = refs[2 * n:]
        x, y, c = lax.axis_index("x"), lax.axis_index("y"), lax.axis_index("c")
        me = 2 * x + y
        pending = []
        for i in range(n):
            lc = pltpu.make_async_copy(ins[i], outs[i].at[me], loc.at[i])
            lc.start()
            pending.append(lc)
            for r, (bx, by) in enumerate(((1, 0), (0, 1), (1, 1))):
                cp = pltpu.make_async_remote_copy(src_ref=ins[i], dst_ref=outs[i].at[me], send_sem=send.at[i, r],
                                                  recv_sem=recv.at[i, r], device_id=(_flip(x, bx), _flip(y, by), c),
                                                  device_id_type=MESH)
                cp.start()
                pending.append(cp)
        for cp in pending:
            cp.wait()

    return pl.pallas_call(
        body, in_specs=[ANY] * n, out_specs=[ANY] * n,
        out_shape=[jax.ShapeDtypeStruct((4,) + a.shape, a.dtype) for a in arrs],
        scratch_shapes=[pltpu.SemaphoreType.DMA((n, 3)), pltpu.SemaphoreType.DMA((n, 3)), pltpu.SemaphoreType.DMA((n,))],
        name=name)(*arrs)


_RELATIONS = tuple((r >> 2 & 1, r >> 1 & 1, r & 1) for r in range(1, 8))


def gather_devices(arr, name):
    def body(in_ref, out_ref, send, recv, loc):
        x, y, c = lax.axis_index("x"), lax.axis_index("y"), lax.axis_index("c")
        me = 4 * x + 2 * y + c
        lc = pltpu.make_async_copy(in_ref, out_ref.at[me], loc)
        lc.start()
        pending = [lc]
        for r, (bx, by, bc) in enumerate(_RELATIONS):
            cp = pltpu.make_async_remote_copy(src_ref=in_ref, dst_ref=out_ref.at[me], send_sem=send.at[r],
                                              recv_sem=recv.at[r], device_id=(_flip(x, bx), _flip(y, by), _flip(c, bc)),
                                              device_id_type=MESH)
            cp.start()
            pending.append(cp)
        for cp in pending:
            cp.wait()

    return pl.pallas_call(
        body, in_specs=[ANY], out_specs=ANY, out_shape=jax.ShapeDtypeStruct((8,) + arr.shape, arr.dtype),
        scratch_shapes=[pltpu.SemaphoreType.DMA((7,)), pltpu.SemaphoreType.DMA((7,)), pltpu.SemaphoreType.DMA(())],
        name=name)(arr)


def scatter_partials(arrs, name):
    n = len(arrs)

    def body(*refs):
        ins, outs = refs[:n], refs[n:2 * n]
        send, recv, loc = refs[2 * n:]
        x, y, c = lax.axis_index("x"), lax.axis_index("y"), lax.axis_index("c")
        me = 4 * x + 2 * y + c
        pending = []
        for i in range(n):
            lc = pltpu.make_async_copy(ins[i].at[c, 2 * x + y], outs[i].at[me], loc.at[i])
            lc.start()
            pending.append(lc)
            for r, (bx, by, bc) in enumerate(_RELATIONS):
                px, py, pc = _flip(x, bx), _flip(y, by), _flip(c, bc)
                cp = pltpu.make_async_remote_copy(src_ref=ins[i].at[pc, 2 * px + py], dst_ref=outs[i].at[me],
                                                  send_sem=send.at[i, r], recv_sem=recv.at[i, r],
                                                  device_id=(px, py, pc), device_id_type=MESH)
                cp.start()
                pending.append(cp)
        for cp in pending:
            cp.wait()

    return pl.pallas_call(
        body, in_specs=[ANY] * n, out_specs=[ANY] * n,
        out_shape=[jax.ShapeDtypeStruct((8,) + a.shape[2:], a.dtype) for a in arrs],
        scratch_shapes=[pltpu.SemaphoreType.DMA((n, 7)), pltpu.SemaphoreType.DMA((n, 7)), pltpu.SemaphoreType.DMA((n,))],
        name=name)(*arrs)


def sibling_exchange(arrs, name):
    n = len(arrs)

    def body(*refs):
        ins, outs = refs[:n], refs[n:2 * n]
        send, recv, loc = refs[2 * n:]
        x, y, c = lax.axis_index("x"), lax.axis_index("y"), lax.axis_index("c")
        pending = []
        for i in range(n):
            lc = pltpu.make_async_copy(ins[i], outs[i].at[c], loc.at[i])
            lc.start()
            cp = pltpu.make_async_remote_copy(src_ref=ins[i], dst_ref=outs[i].at[c], send_sem=send.at[i],
                                              recv_sem=recv.at[i], device_id=(x, y, 1 - c), device_id_type=MESH)
            cp.start()
            pending += [lc, cp]
        for cp in pending:
            cp.wait()

    return pl.pallas_call(
        body, in_specs=[ANY] * n, out_specs=[ANY] * n,
        out_shape=[jax.ShapeDtypeStruct((2,) + a.shape, a.dtype) for a in arrs],
        scratch_shapes=[pltpu.SemaphoreType.DMA((n,)), pltpu.SemaphoreType.DMA((n,)), pltpu.SemaphoreType.DMA((n,))],
        name=name)(*arrs)


def rwkv_fwd(l, seg, taps, pars, vfirst):
    fs = fir_fwd(seg, taps, f"rwkv_shift_fwd{l}")
    rows = [fs] + ([vfirst] if l else [])
    R, W, K, V, KK, A, G = tl_fwd(rwkv_pre(l), f"rwkv_pre_fwd{l}", rows, pars["pre"], [D_GROUP] * 7)
    Y, sb = run_scan_fwd(R, W, K, V, KK, A, HEAD_DIM, f"rwkv_scan_fwd{l}")
    (out,) = tl_fwd(rwkv_post, f"rwkv_post_fwd{l}", [Y, R, K, V, G], pars["post"], [D_GROUP])
    return out, V, (seg, taps, rows, R, W, K, V, KK, A, G, Y, sb)


def rwkv_bwd(l, saved, pars, dout, dv_extra):
    seg, taps, rows, R, W, K, V, KK, A, G, Y, sb = saved
    (dY, dR1, dK1, dV1, dG), dpost = tl_bwd(rwkv_post, f"rwkv_post_bwd{l}", [Y, R, K, V, G], pars["post"], [[dout]])
    (dR2, dW, dK2, dKK, dA), dV2 = run_scan_bwd(R, W, K, V, KK, A, sb, dY, HEAD_DIM, f"rwkv_scan_bwd{l}")
    cts = [[dR1, dR2], [dW], [dK1, dK2], [dV1, dV2] + dv_extra, [dKK], [dA], [dG]]
    drows, dpre = tl_bwd(rwkv_pre(l), f"rwkv_pre_bwd{l}", rows, pars["pre"], cts)
    dseg, dtaps, _ = fir_bwd(seg, taps, [drows[0]], f"rwkv_shift_bwd{l}")
    return dseg, (drows[1] if l else None), dtaps, dpre, dpost


def attn_mix_fwd(l, q, k, v):
    subs, os_, ls_ = [], [], []
    for b, (_, d) in enumerate(DILATED_BRANCHES):
        qs, ks, vs = _to_sub(q, d), _to_sub(k, d), _to_sub(v, d)
        o, lse = attn_fwd(qs, ks, vs, d, f"attn_fwd{l}_{b}")
        subs.append((qs, ks, vs))
        os_.append(_from_sub(o))
        ls_.append(_from_sub(lse))
    (out,) = tl_fwd(attn_combine, f"attn_combine_fwd{l}", os_ + ls_, [], [D_GROUP])
    return out, (subs, os_, ls_)


def attn_mix_bwd(l, saved, dout):
    subs, os_, ls_ = saved
    drows, _ = tl_bwd(attn_combine, f"attn_combine_bwd{l}", os_ + ls_, [], [[dout]])
    dqs, dks, dvs = [], [], []
    for b, (_, d) in enumerate(DILATED_BRANCHES):
        qs, ks, vs = subs[b]
        dq, dk, dv = attn_bwd(qs, ks, vs, _to_sub(drows[b], d), _to_sub(drows[3 + b], d), d, f"attn_bwd{l}_{b}")
        dqs.append(_from_sub(dq))
        dks.append(_from_sub(dk))
        dvs.append(_from_sub(dv))
    return add_rows(dqs, f"attn_dq{l}"), add_rows(dks, f"attn_dk{l}"), add_rows(dvs, f"attn_dv{l}")


def ssd_fwd(l, z, xbc, dtr, pars):
    xc = fir_fwd(xbc, pars["taps"], f"ssd_conv_fwd{l}")
    rr, w, kk, xdt, xs = tl_fwd(ssd_pre, f"ssd_pre_fwd{l}", [xc, dtr], pars["pre"], [512, 512, 512, D_GROUP, D_GROUP])
    ys, sb = run_scan_fwd(rr, w, kk, xdt, None, None, SSD_STATE, f"ssd_scan_fwd{l}")
    (out,) = tl_fwd(ssd_post, f"ssd_post_fwd{l}", [ys, z, xs], pars["post"], [D_GROUP])
    return out, (z, xbc, dtr, xc, rr, w, kk, xdt, xs, ys, sb)


def ssd_bwd(l, saved, pars, dout):
    z, xbc, dtr, xc, rr, w, kk, xdt, xs, ys, sb = saved
    (dys, dz, dxs), dpost = tl_bwd(ssd_post, f"ssd_post_bwd{l}", [ys, z, xs], pars["post"], [[dout]])
    (drr, dw, dkk), dxdt = run_scan_bwd(rr, w, kk, xdt, None, None, sb, dys, SSD_STATE, f"ssd_scan_bwd{l}")
    (dxc, ddtr), dpre = tl_bwd(ssd_pre, f"ssd_pre_bwd{l}", [xc, dtr], pars["pre"], [[drr], [dw], [dkk], [dxdt], [dxs]])
    dxbc, dtaps, _ = fir_bwd(xbc, pars["taps"], [dxc], f"ssd_conv_bwd{l}")
    return dz, dxbc, ddtr, dtaps, dpre, dpost


def hgrn_fwd(l, seg, pars):
    q, f, kk, i = tl_fwd(hgrn_pre, f"hgrn_pre_fwd{l}", [seg], pars["pre"], [D_GROUP] * 4)
    o, sb = run_scan_fwd(q, f, kk, i, None, None, HEAD_DIM, f"hgrn_scan_fwd{l}")
    (out,) = tl_fwd(hgrn_post, f"hgrn_post_fwd{l}", [o, seg], pars["post"], [D_GROUP])
    return out, (seg, q, f, kk, i, o, sb)


def hgrn_bwd(l, saved, pars, dout):
    seg, q, f, kk, i, o, sb = saved
    (do, dseg1), dpost = tl_bwd(hgrn_post, f"hgrn_post_bwd{l}", [o, seg], pars["post"], [[dout]])
    (dq, df, dkk), di = run_scan_bwd(q, f, kk, i, None, None, sb, do, HEAD_DIM, f"hgrn_scan_bwd{l}")
    (dseg2,), dpre = tl_bwd(hgrn_pre, f"hgrn_pre_bwd{l}", [seg], pars["pre"], [[dq], [df], [dkk], [di]])
    return add_rows([dseg1, dseg2], f"hgrn_dseg{l}"), dpre, dpost


def layer_fwd(l, x, wts, pars, vfirst):
    proj = matmul(x, wts["in"], "nn", f"proj_fwd{l}")
    seg_h = proj[:, SEG_HGRN:SEG_HGRN + 1024]
    seg_r = proj[:, SEG_RWKV:SEG_RWKV + 1024]
    q, k, v = (proj[:, SEG_Q + j * D_GROUP:SEG_Q + (j + 1) * D_GROUP] for j in range(3))
    z = proj[:, SEG_Z:SEG_Z + D_GROUP]
    xbc = proj[:, SEG_XBC:SEG_XBC + SSD_XBC]
    dtr = proj[:, SEG_DT:SEG_DT + 128]
    ya, v_rwkv, sa = rwkv_fwd(l, seg_r, pars["rwkv"]["taps"], pars["rwkv"], vfirst)
    yb, sb = attn_mix_fwd(l, q, k, v)
    yc, sc = ssd_fwd(l, z, xbc, dtr, pars["ssd"])
    yd, sd = hgrn_fwd(l, seg_h, pars["hgrn"])
    mix = jnp.concatenate([ya, yb, yc, yd], axis=1)
    mo = matmul(mix, wts["out"], "nn", f"out_fwd{l}")
    (x1,) = tl_fwd(ln_res, f"ln1_fwd{l}", [x, mo], pars["ln1"], [D_MODEL])
    u = matmul(x1, wts["up"], "nn", f"up_fwd{l}")
    (h,) = tl_fwd(relu2, f"relu2_fwd{l}", [u], [], [D_FF])
    dn = matmul(h, wts["down"], "nn", f"down_fwd{l}")
    (x2,) = tl_fwd(ln_res, f"ln2_fwd{l}", [x1, dn], pars["ln2"], [D_MODEL])
    return x2, v_rwkv, (x, sa, sb, sc, sd, mix, mo, x1, u, h, dn)


def layer_bwd(l, saved, wts, pars, dx2, dv_extra):
    x, sa, sb, sc, sd, mix, mo, x1, u, h, dn = saved
    S = x.shape[0]
    g = {}
    (dx1a, ddn), g["ln2"] = tl_bwd(ln_res, f"ln2_bwd{l}", [x1, dn], pars["ln2"], [[dx2]])
    g["down"] = matmul(h, ddn, "tn", f"down_dw{l}")
    dh = matmul(ddn, wts["down"], "nt", f"down_dx{l}")
    (du,), _ = tl_bwd(relu2, f"relu2_bwd{l}", [u], [], [[dh]])
    g["up"] = matmul(x1, du, "tn", f"up_dw{l}")
    dx1 = matmul(du, wts["up"], "nt", f"up_dx{l}", add=dx1a)
    (dxa, dmo), g["ln1"] = tl_bwd(ln_res, f"ln1_bwd{l}", [x, mo], pars["ln1"], [[dx1]])
    g["out"] = matmul(mix, dmo, "tn", f"out_dw{l}")
    dmix = matmul(dmo, wts["out"], "nt", f"out_dx{l}")
    dya, dyb, dyc, dyd = (dmix[:, j * D_GROUP:(j + 1) * D_GROUP] for j in range(4))
    dseg_r, dvfirst, g["rwkv_taps"], g["rwkv_pre"], g["rwkv_post"] = rwkv_bwd(l, sa, pars["rwkv"], dya, dv_extra)
    dq, dk, dv = attn_mix_bwd(l, sb, dyb)
    dz, dxbc, ddtr, g["ssd_taps"], g["ssd_pre"], g["ssd_post"] = ssd_bwd(l, sc, pars["ssd"], dyc)
    dseg_h, g["hgrn_pre"], g["hgrn_post"] = hgrn_bwd(l, sd, pars["hgrn"], dyd)
    dproj = jnp.concatenate([dseg_h, dseg_r, dq, dk, dv, dz, dxbc, ddtr, jnp.zeros((S, PROJ_W - SEG_DT - 128), F32)], axis=1)
    g["in"] = matmul(x, dproj, "tn", f"proj_dw{l}")
    dx = matmul(dproj, wts["in"], "nt", f"proj_dx{l}", add=dxa)
    return dx, dvfirst, g


SMALL = ("lower_bounds", "w_in_vres", "mu_shift", "mu_vres", "rwkv_w0", "rwkv_w2", "rwkv_a0", "rwkv_a2", "rwkv_g2",
         "rwkv_k_k", "rwkv_k_a", "rwkv_r_k", "rwkv_lnx_w", "rwkv_lnx_b", "rwkv_v0", "rwkv_v2", "ssd_conv_w",
         "ssd_conv_b", "ssd_dt_bias", "ssd_A_log", "ssd_D", "ssd_norm_w", "hgrn_norm_w", "ln1_w", "ln1_b", "ln2_w", "ln2_b")
BIG = ("w_in", "w_out", "w_up", "w_down")
SMALL_SHARDED = {"w_in_vres": 1, "rwkv_w2": 2, "rwkv_a2": 2, "rwkv_g2": 2, "rwkv_v2": 2, "ssd_conv_w": 2}
WEIGHTS = ("lower_bounds", "w_in", "w_in_vres", "mu_shift", "mu_vres", "rwkv_w0", "rwkv_w2", "rwkv_a0", "rwkv_a2",
           "rwkv_g2", "rwkv_k_k", "rwkv_k_a", "rwkv_r_k", "rwkv_lnx_w", "rwkv_lnx_b", "rwkv_v0", "rwkv_v2",
           "ssd_conv_w", "ssd_conv_b", "ssd_dt_bias", "ssd_A_log", "ssd_D", "ssd_norm_w", "hgrn_norm_w", "w_out",
           "ln1_w", "ln1_b", "w_up", "w_down", "ln2_w", "ln2_b")


def _row(v, width=None):
    v = v.reshape(1, -1).astype(F32)
    if width is not None and v.shape[1] < width:
        v = jnp.pad(v, ((0, 0), (0, width - v.shape[1])))
    return v


def _rows_at(m, rows, at):
    return jnp.pad(m.astype(F32), ((at, rows - at - m.shape[0]), (0, 0)))


def _pad_w_in(w_in_l, vres):
    rows = w_in_l.shape[0]
    out = []
    order = sorted(_PIECES, key=lambda p: p[2])
    pos = 0
    for start, width, at in order:
        if at > pos:
            out.append(jnp.zeros((rows, at - pos), w_in_l.dtype))
        out.append(w_in_l[:, start:start + width])
        pos = at + width
        if at == SEG_RWKV and vres is not None:
            out.append(vres.astype(w_in_l.dtype))
            pos += vres.shape[1]
    out.append(jnp.zeros((rows, PROJ_W - pos), w_in_l.dtype))
    return jnp.concatenate(out, axis=1)


def _unpad_w_in(g):
    return jnp.concatenate([g[:, at:at + width] for _, width, at in _PIECES], axis=1)


def layer_params(l, sp, prep):
    lb, mu, om = prep[l], prep[2 + 2 * l], prep[3 + 2 * l]
    pre = [_row(sp["rwkv_w0"][l]), _rows_at(sp["rwkv_w2"][l], 128, 0), _row(sp["rwkv_a0"][l]),
           _rows_at(sp["rwkv_a2"][l], 128, 32), _rows_at(sp["rwkv_g2"][l], 128, 64),
           _row(sp["rwkv_k_k"][l]), _row(sp["rwkv_k_a"][l])]
    if l:
        pre += [_row(sp["rwkv_v0"][l - 1]), _rows_at(sp["rwkv_v2"][l - 1], 128, 0)]
    return {
        "rwkv": {"taps": jnp.concatenate([mu, om], axis=0), "pre": pre,
                 "post": [_row(sp["rwkv_lnx_w"][l]), _row(sp["rwkv_lnx_b"][l]), _row(sp["rwkv_r_k"][l])]},
        "ssd": {"taps": sp["ssd_conv_w"][l].astype(F32),
                "pre": [_row(sp["ssd_conv_b"][l]), _row(sp["ssd_dt_bias"][l], 128), _row(sp["ssd_A_log"][l], 128)],
                "post": [_row(sp["ssd_D"][l], 128), _row(sp["ssd_norm_w"][l])]},
        "hgrn": {"pre": [lb], "post": [_row(sp["hgrn_norm_w"][l])]},
        "ln1": [_row(sp["ln1_w"][l]), _row(sp["ln1_b"][l])],
        "ln2": [_row(sp["ln2_w"][l]), _row(sp["ln2_b"][l])],
    }


def _mu_full(sp, l):
    parts = [sp["mu_shift"][l].reshape(1, -1)]
    if l:
        parts.append(sp["mu_vres"][l - 1].reshape(1, -1))
    return _row(jnp.concatenate(parts, axis=1), 1024)


def local_step(x, target, big, sp):
    prep_in = [sp["lower_bounds"].astype(F32), _mu_full(sp, 0), _mu_full(sp, 1)]
    prep = small_fwd(param_prep, "param_prep_fwd", prep_in,
                     [(1, D_GROUP), (1, D_GROUP), (1, 1024), (1, 1024), (1, 1024), (1, 1024)])
    pars, wts = [], []
    for l in range(DEPTH):
        pars.append(layer_params(l, sp, prep))
        vres = sp["w_in_vres"][l - 1].astype(BF16) if l else None
        wts.append({"in": _pad_w_in(big["w_in"][l], vres), "out": big["w_out"][l], "up": big["w_up"][l],
                    "down": big["w_down"][l]})
    h, vfirst, saved = x, None, []
    for l in range(DEPTH):
        h, v_l, sv = layer_fwd(l, h, wts[l], pars[l], vfirst)
        vfirst = v_l if l == 0 else vfirst
        saved.append(sv)
    loss_row, dh = loss_and_grad(h, target, "loss")
    grads, dv_extra = [None] * DEPTH, []
    for l in reversed(range(DEPTH)):
        dh, dvfirst, grads[l] = layer_bwd(l, saved[l], wts[l], pars[l], dh, dv_extra)
        dv_extra = [dvfirst] if l else []
    cts = [grads[0]["hgrn_pre"][0], grads[1]["hgrn_pre"][0]]
    for l in range(DEPTH):
        cts += [grads[l]["rwkv_taps"][0:1], grads[l]["rwkv_taps"][1:2]]
    d_lower, d_mu0, d_mu1 = small_bwd(param_prep, "param_prep_bwd", prep_in, cts)
    d_mu = [d_mu0, d_mu1]
    gb = {"w_in": [_unpad_w_in(grads[l]["in"]) for l in range(DEPTH)], "w_out": [grads[l]["out"] for l in range(DEPTH)],
          "w_up": [grads[l]["up"] for l in range(DEPTH)], "w_down": [grads[l]["down"] for l in range(DEPTH)]}
    st = lambda f: jnp.stack([f(l) for l in range(DEPTH)])
    g1 = grads[1]
    gs = {
        "lower_bounds": d_lower,
        "w_in_vres": g1["in"][None, :, VRES_COL:VRES_COL + 32],
        "mu_shift": st(lambda l: d_mu[l][0, :896]),
        "mu_vres": d_mu[1][:, 896:928],
        "rwkv_w0": st(lambda l: grads[l]["rwkv_pre"][0][0]),
        "rwkv_w2": st(lambda l: grads[l]["rwkv_pre"][1][0:32]),
        "rwkv_a0": st(lambda l: grads[l]["rwkv_pre"][2][0]),
        "rwkv_a2": st(lambda l: grads[l]["rwkv_pre"][3][32:64]),
        "rwkv_g2": st(lambda l: grads[l]["rwkv_pre"][4][64:128]),
        "rwkv_k_k": st(lambda l: grads[l]["rwkv_pre"][5][0]),
        "rwkv_k_a": st(lambda l: grads[l]["rwkv_pre"][6][0]),
        "rwkv_r_k": st(lambda l: grads[l]["rwkv_post"][2].reshape(N_HEADS, HEAD_DIM)),
        "rwkv_lnx_w": st(lambda l: grads[l]["rwkv_post"][0][0]),
        "rwkv_lnx_b": st(lambda l: grads[l]["rwkv_post"][1][0]),
        "rwkv_v0": g1["rwkv_pre"][7],
        "rwkv_v2": g1["rwkv_pre"][8][None, 0:32],
        "ssd_conv_w": st(lambda l: grads[l]["ssd_taps"]),
        "ssd_conv_b": st(lambda l: grads[l]["ssd_pre"][0][0]),
        "ssd_dt_bias": st(lambda l: grads[l]["ssd_pre"][1][0, :N_HEADS]),
        "ssd_A_log": st(lambda l: grads[l]["ssd_pre"][2][0, :N_HEADS]),
        "ssd_D": st(lambda l: grads[l]["ssd_post"][0][0, :N_HEADS]),
        "ssd_norm_w": st(lambda l: grads[l]["ssd_post"][1][0]),
        "hgrn_norm_w": st(lambda l: grads[l]["hgrn_post"][0][0]),
        "ln1_w": st(lambda l: grads[l]["ln1"][0][0]),
        "ln1_b": st(lambda l: grads[l]["ln1"][1][0]),
        "ln2_w": st(lambda l: grads[l]["ln2"][0][0]),
        "ln2_b": st(lambda l: grads[l]["ln2"][1][0]),
    }
    return loss_row, dh, gb, gs


def _pack(vecs):
    flat, meta, pos = [], [], 0
    for v in vecs:
        flat.append(v.reshape(-1).astype(F32))
        meta.append((pos, v.shape))
        pos += v.size
    total = -(-pos // 1024) * 1024
    flat.append(jnp.zeros((total - pos,), F32))
    return jnp.concatenate(flat).reshape(total // 128, 128), meta


def _unpack(packed, meta):
    flat = packed.reshape(-1)
    return [flat[off:off + math.prod(shape)].reshape(shape) for off, shape in meta]


def _to_shards(name, g):
    if name == "w_in":
        return jnp.transpose(g.reshape(g.shape[0], 4, g.shape[1] // 4), (1, 0, 2))
    if name == "w_up":
        return jnp.transpose(g.reshape(g.shape[0], 4, g.shape[1] // 4), (1, 0, 2))
    return g.reshape(4, g.shape[0] // 4, g.shape[1])


def _from_chips(name, g):
    if name in ("w_in", "w_up"):
        return jnp.transpose(g, (1, 2, 0, 3)).reshape(g.shape[1], g.shape[2], 4 * g.shape[3])
    return jnp.transpose(g, (1, 0, 2, 3)).reshape(g.shape[1], 4 * g.shape[2], g.shape[3])


INPUT_NAMES = ("x",) + WEIGHTS + ("loss_target",) + tuple("m_" + n for n in WEIGHTS) + tuple("v_" + n for n in WEIGHTS)


def _step(*args):
    a = dict(zip(INPUT_NAMES, args, strict=True))
    chip = 2 * lax.axis_index("x") + lax.axis_index("y")

    sharded_names = list(SMALL_SHARDED)
    small_pack, small_meta = _pack([a[n] for n in sharded_names])
    gathered = gather_chips([a[n].astype(BF16) for n in BIG] + [small_pack], "gather_weights")
    big = {n: _from_chips(n, g) for n, g in zip(BIG, gathered)}
    sp = {n: a[n] for n in SMALL if n not in SMALL_SHARDED}
    per_chip = [_unpack(gathered[-1][s], small_meta) for s in range(4)]
    for j, n in enumerate(sharded_names):
        sp[n] = jnp.concatenate([per_chip[s][j] for s in range(4)], axis=SMALL_SHARDED[n])

    loss_row, gx, gb, gs = local_step(a["x"][0], a["loss_target"][0], big, sp)

    partials = [jnp.stack([_to_shards(n, gb[n][l]) for l in range(DEPTH)]) for n in BIG]
    slots = scatter_partials(partials, "reduce_big")
    mine = []
    for n, sl in zip(BIG, slots):
        rows, cols = sl.shape[-2:]
        mine.append(sum_parts(sl.reshape(8, rows, cols), f"sum_{n}"))
    summed = sibling_exchange(mine, "exchange_big")
    out_g, out_d, out_m, out_v = {}, {}, {}, {}
    for n, g in zip(BIG, summed):
        shape = a[n].shape
        flat = lambda t: t.reshape(shape[0] * shape[1], shape[2])
        d, nm, nv = adamw(flat(a[n]), flat(g), flat(a["m_" + n]), flat(a["v_" + n]), f"adamw_{n}")
        out_g[n], out_d[n], out_m[n], out_v[n] = g.reshape(shape), d.reshape(shape), nm.reshape(shape), nv.reshape(shape)

    vec, meta = _pack([loss_row] + [gs[n] for n in SMALL])
    total = sum_parts(gather_devices(vec, "gather_small"), "sum_small")
    parts = _unpack(total, meta)
    loss = parts[0][0, 0]
    g_small = {}
    for n, g in zip(SMALL, parts[1:]):
        if n in SMALL_SHARDED:
            ax = SMALL_SHARDED[n]
            size = a[n].shape[ax]
            g = lax.dynamic_slice_in_dim(g, chip * size, size, axis=ax)
        g_small[n] = g
    pw, pmeta = _pack([a[n] for n in SMALL])
    pg, _ = _pack([g_small[n] for n in SMALL])
    pm, _ = _pack([a["m_" + n] for n in SMALL])
    pv, _ = _pack([a["v_" + n] for n in SMALL])
    d, nm, nv = adamw(pw, pg, pm, pv, "adamw_small")
    for n, dd, mm, vv in zip(SMALL, _unpack(d, pmeta), _unpack(nm, pmeta), _unpack(nv, pmeta)):
        out_g[n], out_d[n], out_m[n], out_v[n] = g_small[n], dd, mm, vv

    return (loss, gx[None], *[out_g[n] for n in WEIGHTS], *[out_d[n] for n in WEIGHTS],
            *[out_m[n] for n in WEIGHTS], *[out_v[n] for n in WEIGHTS])


def kernel(x, lower_bounds, w_in, w_in_vres, mu_shift, mu_vres, rwkv_w0, rwkv_w2, rwkv_a0, rwkv_a2, rwkv_g2, rwkv_k_k, rwkv_k_a, rwkv_r_k, rwkv_lnx_w, rwkv_lnx_b, rwkv_v0, rwkv_v2, ssd_conv_w, ssd_conv_b, ssd_dt_bias, ssd_A_log, ssd_D, ssd_norm_w, hgrn_norm_w, w_out, ln1_w, ln1_b, w_up, w_down, ln2_w, ln2_b, loss_target, m_lower_bounds, m_w_in, m_w_in_vres, m_mu_shift, m_mu_vres, m_rwkv_w0, m_rwkv_w2, m_rwkv_a0, m_rwkv_a2, m_rwkv_g2, m_rwkv_k_k, m_rwkv_k_a, m_rwkv_r_k, m_rwkv_lnx_w, m_rwkv_lnx_b, m_rwkv_v0, m_rwkv_v2, m_ssd_conv_w, m_ssd_conv_b, m_ssd_dt_bias, m_ssd_A_log, m_ssd_D, m_ssd_norm_w, m_hgrn_norm_w, m_w_out, m_ln1_w, m_ln1_b, m_w_up, m_w_down, m_ln2_w, m_ln2_b, v_lower_bounds, v_w_in, v_w_in_vres, v_mu_shift, v_mu_vres, v_rwkv_w0, v_rwkv_w2, v_rwkv_a0, v_rwkv_a2, v_rwkv_g2, v_rwkv_k_k, v_rwkv_k_a, v_rwkv_r_k, v_rwkv_lnx_w, v_rwkv_lnx_b, v_rwkv_v0, v_rwkv_v2, v_ssd_conv_w, v_ssd_conv_b, v_ssd_dt_bias, v_ssd_A_log, v_ssd_D, v_ssd_norm_w, v_hgrn_norm_w, v_w_out, v_ln1_w, v_ln1_b, v_w_up, v_w_down, v_ln2_w, v_ln2_b):
    return _step(x, lower_bounds, w_in, w_in_vres, mu_shift, mu_vres, rwkv_w0, rwkv_w2, rwkv_a0, rwkv_a2, rwkv_g2, rwkv_k_k, rwkv_k_a, rwkv_r_k, rwkv_lnx_w, rwkv_lnx_b, rwkv_v0, rwkv_v2, ssd_conv_w, ssd_conv_b, ssd_dt_bias, ssd_A_log, ssd_D, ssd_norm_w, hgrn_norm_w, w_out, ln1_w, ln1_b, w_up, w_down, ln2_w, ln2_b, loss_target, m_lower_bounds, m_w_in, m_w_in_vres, m_mu_shift, m_mu_vres, m_rwkv_w0, m_rwkv_w2, m_rwkv_a0, m_rwkv_a2, m_rwkv_g2, m_rwkv_k_k, m_rwkv_k_a, m_rwkv_r_k, m_rwkv_lnx_w, m_rwkv_lnx_b, m_rwkv_v0, m_rwkv_v2, m_ssd_conv_w, m_ssd_conv_b, m_ssd_dt_bias, m_ssd_A_log, m_ssd_D, m_ssd_norm_w, m_hgrn_norm_w, m_w_out, m_ln1_w, m_ln1_b, m_w_up, m_w_down, m_ln2_w, m_ln2_b, v_lower_bounds, v_w_in, v_w_in_vres, v_mu_shift, v_mu_vres, v_rwkv_w0, v_rwkv_w2, v_rwkv_a0, v_rwkv_a2, v_rwkv_g2, v_rwkv_k_k, v_rwkv_k_a, v_rwkv_r_k, v_rwkv_lnx_w, v_rwkv_lnx_b, v_rwkv_v0, v_rwkv_v2, v_ssd_conv_w, v_ssd_conv_b, v_ssd_dt_bias, v_ssd_A_log, v_ssd_D, v_ssd_norm_w, v_hgrn_norm_w, v_w_out, v_ln1_w, v_ln1_b, v_w_up, v_w_down, v_ln2_w, v_ln2_b)
```

```python
import functools
import math

import jax
import jax.numpy as jnp
from jax import lax
from jax.experimental import pallas as pl
from jax.experimental.pallas import tpu as pltpu

F32 = jnp.float32
BF16 = jnp.bfloat16
HI = lax.Precision.HIGHEST

DEPTH = 2
D_MODEL = 1024
D_GROUP = 256
HEAD_DIM = 64
N_HEADS = 4
SSD_STATE = 128
SSD_XBC = 768
SSD_CONV = 4
D_FF = 4096
ALPHA = (2.0 * DEPTH) ** 0.25
LN_EPS = 1e-5
RMS_EPS = 1e-5
RWKV_GN_EPS = HEAD_DIM * 1e-5
DILATED_BRANCHES = ((128, 1), (512, 4), (2048, 16))
ALIBI_SLOPES = tuple(2.0 ** (-8.0 * (h + 1) / N_HEADS) for h in range(N_HEADS))
ATTN_BLK = 128

ADAM_LR, ADAM_B1, ADAM_B2, ADAM_EPS, ADAM_WD, ADAM_STEP = 0.001, 0.9, 0.999, 1e-08, 0.01, 10

IN_COLS = 3716
PROJ_W = 4096
SEG_HGRN, SEG_RWKV, SEG_Q, SEG_Z, SEG_XBC, SEG_DT = 0, 1024, 2048, 2816, 3072, 3840
_PIECES = ((0, 896, SEG_RWKV), (896, 768, SEG_Q), (1664, 256, SEG_Z), (1920, 768, SEG_XBC),
           (2688, 4, SEG_DT), (2692, 1024, SEG_HGRN))
VRES_COL = SEG_RWKV + 896

ROW_TILE = 256
SCAN_CHUNK = 128
VMEM_LIMIT = 48 * 1024 * 1024


def _cparams(sem=None):
    if sem is None:
        return pltpu.CompilerParams(vmem_limit_bytes=VMEM_LIMIT)
    return pltpu.CompilerParams(dimension_semantics=sem, vmem_limit_bytes=VMEM_LIMIT)


def _pick(n, pref):
    for t in pref:
        if n % t == 0:
            return t
    return n


def matmul(a, b, mode, name, add=None, out_dtype=F32):
    if mode == "nn":
        (M, K), (_, N) = a.shape, b.shape
    elif mode == "nt":
        (M, K), (N, _) = a.shape, b.shape
    else:
        (K, M), (_, N) = a.shape, b.shape
    tm, tn, tk = _pick(M, (512, 256, 128)), _pick(N, (512, 256, 128)), _pick(K, (512, 256, 128))
    nk = K // tk
    dims = {"nn": (((1,), (0,)), ((), ())), "nt": (((1,), (1,)), ((), ())), "tn": (((0,), (0,)), ((), ()))}[mode]

    def body(*refs):
        if add is None:
            a_ref, b_ref, o_ref, acc = refs
            add_ref = None
        else:
            a_ref, b_ref, add_ref, o_ref, acc = refs
        k = pl.program_id(2)

        @pl.when(k == 0)
        def _():
            acc[...] = jnp.zeros_like(acc)

        acc[...] += lax.dot_general(a_ref[...].astype(BF16), b_ref[...].astype(BF16), dims,
                                    preferred_element_type=F32)

        @pl.when(k == nk - 1)
        def _():
            r = acc[...]
            if add_ref is not None:
                r = r + add_ref[...]
            o_ref[...] = r.astype(o_ref.dtype)

    a_spec = pl.BlockSpec((tk, tm), lambda i, j, k: (k, i)) if mode == "tn" else pl.BlockSpec((tm, tk), lambda i, j, k: (i, k))
    b_spec = pl.BlockSpec((tn, tk), lambda i, j, k: (j, k)) if mode == "nt" else pl.BlockSpec((tk, tn), lambda i, j, k: (k, j))
    o_spec = pl.BlockSpec((tm, tn), lambda i, j, k: (i, j))
    ins, specs = [a, b], [a_spec, b_spec]
    if add is not None:
        ins.append(add)
        specs.append(o_spec)
    return pl.pallas_call(
        body, grid=(M // tm, N // tn, nk), in_specs=specs, out_specs=o_spec,
        out_shape=jax.ShapeDtypeStruct((M, N), out_dtype), scratch_shapes=[pltpu.VMEM((tm, tn), F32)],
        compiler_params=_cparams(("parallel", "parallel", "arbitrary")), name=name)(*ins)


def _row_spec(w, tile):
    return pl.BlockSpec((tile, w), lambda i: (i, 0))


def _par_spec(shape):
    return pl.BlockSpec(shape, lambda i: (0,) * len(shape))


def tl_fwd(fn, name, rows, pars, out_widths, tile=ROW_TILE):
    S = rows[0].shape[0]
    nr = len(rows)

    def body(*refs):
        ins = [r[...] for r in refs[:nr + len(pars)]]
        outs = fn(*ins)
        for o_ref, o in zip(refs[nr + len(pars):], outs):
            o_ref[...] = o

    return pl.pallas_call(
        body, grid=(S // tile,),
        in_specs=[_row_spec(r.shape[1], tile) for r in rows] + [_par_spec(p.shape) for p in pars],
        out_specs=[_row_spec(w, tile) for w in out_widths],
        out_shape=[jax.ShapeDtypeStruct((S, w), F32) for w in out_widths],
        compiler_params=_cparams(("parallel",)), name=name)(*rows, *pars)


def tl_bwd(fn, name, rows, pars, cts, tile=ROW_TILE, row_grad=None):
    S = rows[0].shape[0]
    nr, npar = len(rows), len(pars)
    row_grad = [True] * nr if row_grad is None else row_grad
    flat_cts = [c for group in cts for c in group]
    ncts = len(flat_cts)
    gi = [i for i in range(nr) if row_grad[i]]

    def body(*refs):
        row_v = [r[...] for r in refs[:nr]]
        par_v = [r[...] for r in refs[nr:nr + npar]]
        ct_refs = refs[nr + npar:nr + npar + ncts]
        out_refs = refs[nr + npar + ncts:]
        ct_v, pos = [], 0
        for group in cts:
            acc = ct_refs[pos][...]
            for q in range(1, len(group)):
                acc = acc + ct_refs[pos + q][...]
            pos += len(group)
            ct_v.append(acc)

        def f(diff_rows, par_vals):
            full = list(row_v)
            for idx, val in zip(gi, diff_rows):
                full[idx] = val
            return tuple(fn(*full, *par_vals))

        _, vjp = jax.vjp(f, [row_v[i] for i in gi], par_v)
        d_rows, d_pars = vjp(tuple(ct_v))
        for o_ref, g in zip(out_refs[:len(gi)], d_rows):
            o_ref[...] = g
        first = pl.program_id(0) == 0
        for o_ref, g in zip(out_refs[len(gi):], d_pars):
            @pl.when(first)
            def _(o_ref=o_ref):
                o_ref[...] = jnp.zeros_like(o_ref)
            o_ref[...] += g

    outs = pl.pallas_call(
        body, grid=(S // tile,),
        in_specs=[_row_spec(r.shape[1], tile) for r in rows] + [_par_spec(p.shape) for p in pars]
        + [_row_spec(c.shape[1], tile) for c in flat_cts],
        out_specs=[_row_spec(rows[i].shape[1], tile) for i in gi] + [_par_spec(p.shape) for p in pars],
        out_shape=[jax.ShapeDtypeStruct(rows[i].shape, F32) for i in gi] + [jax.ShapeDtypeStruct(p.shape, F32) for p in pars],
        compiler_params=_cparams(("arbitrary",)), name=name)(*rows, *pars, *flat_cts)
    return list(outs[:len(gi)]), list(outs[len(gi):])


def _shift_rows(x, j):
    if j == 0:
        return x
    rolled = pltpu.roll(x, j, 0)
    row = lax.broadcasted_iota(jnp.int32, x.shape, 0)
    return jnp.where(row >= j, rolled, 0.0)


def _unshift_rows(x, j):
    if j == 0:
        return x
    S = x.shape[0]
    rolled = pltpu.roll(x, S - j, 0)
    row = lax.broadcasted_iota(jnp.int32, x.shape, 0)
    return jnp.where(row < S - j, rolled, 0.0)


def fir_fwd(x, taps, name):
    S, C = x.shape
    K = taps.shape[0]

    def body(x_ref, w_ref, y_ref):
        xv = x_ref[...]
        acc = jnp.zeros_like(xv)
        for k in range(K):
            acc = acc + _shift_rows(xv, K - 1 - k) * w_ref[pl.ds(k, 1), :]
        y_ref[...] = acc

    cs = pl.BlockSpec((S, 128), lambda j: (0, j))
    return pl.pallas_call(body, grid=(C // 128,), in_specs=[cs, pl.BlockSpec((K, 128), lambda j: (0, j))],
                          out_specs=cs, out_shape=jax.ShapeDtypeStruct((S, C), F32),
                          compiler_params=_cparams(("parallel",)), name=name)(x, taps)


def fir_bwd(x, taps, dy_list, name):
    S, C = x.shape
    K = taps.shape[0]
    n = len(dy_list)

    def body(*refs):
        x_ref, w_ref = refs[:2]
        dy = refs[2][...]
        for q in range(1, n):
            dy = dy + refs[2 + q][...]
        dx_ref, dw_ref, db_ref = refs[2 + n:]
        xv = x_ref[...]
        dx = jnp.zeros_like(xv)
        for k in range(K):
            j = K - 1 - k
            dx = dx + _unshift_rows(dy, j) * w_ref[pl.ds(k, 1), :]
            dw_ref[pl.ds(k, 1), :] = jnp.sum(dy * _shift_rows(xv, j), axis=0, keepdims=True)
        dx_ref[...] = dx
        db_ref[...] = jnp.sum(dy, axis=0, keepdims=True)

    cs = pl.BlockSpec((S, 128), lambda j: (0, j))
    ks = pl.BlockSpec((K, 128), lambda j: (0, j))
    bs = pl.BlockSpec((1, 128), lambda j: (0, j))
    return pl.pallas_call(body, grid=(C // 128,), in_specs=[cs, ks] + [cs] * n, out_specs=[cs, ks, bs],
                          out_shape=[jax.ShapeDtypeStruct((S, C), F32), jax.ShapeDtypeStruct((K, C), F32),
                                     jax.ShapeDtypeStruct((1, C), F32)],
                          compiler_params=_cparams(("parallel",)), name=name)(x, taps, *dy_list)


def _col(tile, lane, t):
    return jnp.sum(jnp.where(lane == t, tile, 0.0), axis=1, keepdims=True)


def scan_fwd(r, w, k, vT, kk, a, name):
    delta = kk is not None
    H, S, Dk = r.shape
    Dv = vT.shape[1]
    Tc = SCAN_CHUNK
    nc = S // Tc
    rows = [r, w, k] + ([kk, a] if delta else [])
    nrow = len(rows)

    def body(*refs):
        row_refs = refs[:nrow]
        vT_ref, yT_ref, sb_ref, s_ref = refs[nrow:]
        c = pl.program_id(1)

        @pl.when(c == 0)
        def _():
            s_ref[...] = jnp.zeros_like(s_ref)

        sb_ref[0, 0] = s_ref[...]
        vt = vT_ref[0]
        lane = lax.broadcasted_iota(jnp.int32, (Dv, Tc), 1)

        def step(t, carry):
            s, y = carry
            rv = [ref[0, pl.ds(t, 1), :] for ref in row_refs]
            vcol = _col(vt, lane, t)
            if delta:
                sa = jnp.sum(s * (-rv[3]), axis=1, keepdims=True)
                s = s * rv[1] + sa * (rv[3] * rv[4]) + vcol * rv[2]
            else:
                s = s * rv[1] + vcol * rv[2]
            ycol = jnp.sum(s * rv[0], axis=1, keepdims=True)
            return s, jnp.where(lane == t, ycol, y)

        s, y = lax.fori_loop(0, Tc, step, (s_ref[...], jnp.zeros((Dv, Tc), F32)))
        s_ref[...] = s
        yT_ref[0] = y

    rs = pl.BlockSpec((1, Tc, Dk), lambda h, c: (h, c, 0))
    vs = pl.BlockSpec((1, Dv, Tc), lambda h, c: (h, 0, c))
    yT, sb = pl.pallas_call(
        body, grid=(H, nc), in_specs=[rs] * nrow + [vs],
        out_specs=[vs, pl.BlockSpec((1, 1, Dv, Dk), lambda h, c: (h, c, 0, 0))],
        out_shape=[jax.ShapeDtypeStruct((H, Dv, S), F32), jax.ShapeDtypeStruct((H, nc, Dv, Dk), F32)],
        scratch_shapes=[pltpu.VMEM((Dv, Dk), F32)],
        compiler_params=_cparams(("parallel", "arbitrary")), name=name)(*rows, vT)
    return yT, sb


def scan_bwd(r, w, k, vT, kk, a, sb, dyT, name):
    delta = kk is not None
    H, S, Dk = r.shape
    Dv = vT.shape[1]
    Tc = SCAN_CHUNK
    nc = S // Tc
    rows = [r, w, k] + ([kk, a] if delta else [])
    nrow = len(rows)

    def body(*refs):
        row_refs = refs[:nrow]
        vT_ref, dyT_ref, sb_ref = refs[nrow:nrow + 3]
        drow_refs = refs[nrow + 3:2 * nrow + 3]
        dvT_ref, sbuf, ds_ref = refs[2 * nrow + 3:]
        c = pl.program_id(1)

        @pl.when(c == 0)
        def _():
            ds_ref[...] = jnp.zeros_like(ds_ref)

        vt = vT_ref[0]
        dyt = dyT_ref[0]
        lane = lax.broadcasted_iota(jnp.int32, (Dv, Tc), 1)

        def new_state(s, rv, vcol):
            if delta:
                sa = jnp.sum(s * (-rv[3]), axis=1, keepdims=True)
                return s * rv[1] + sa * (rv[3] * rv[4]) + vcol * rv[2], sa
            return s * rv[1] + vcol * rv[2], None

        def fstep(t, s):
            sbuf[t] = s
            rv = [ref[0, pl.ds(t, 1), :] for ref in row_refs]
            return new_state(s, rv, _col(vt, lane, t))[0]

        lax.fori_loop(0, Tc, fstep, sb_ref[0, 0])

        def bstep(i, carry):
            ds, dv = carry
            t = Tc - 1 - i
            sp = sbuf[t]
            rv = [ref[0, pl.ds(t, 1), :] for ref in row_refs]
            vcol = _col(vt, lane, t)
            dycol = _col(dyt, lane, t)
            st, sa = new_state(sp, rv, vcol)
            drow_refs[0][0, pl.ds(t, 1), :] = jnp.sum(st * dycol, axis=0, keepdims=True)
            g = ds + dycol * rv[0]
            drow_refs[1][0, pl.ds(t, 1), :] = jnp.sum(g * sp, axis=0, keepdims=True)
            drow_refs[2][0, pl.ds(t, 1), :] = jnp.sum(g * vcol, axis=0, keepdims=True)
            dvcol = jnp.sum(g * rv[2], axis=1, keepdims=True)
            ds_prev = g * rv[1]
            if delta:
                b = rv[3] * rv[4]
                dsa = jnp.sum(g * b, axis=1, keepdims=True)
                db = jnp.sum(g * sa, axis=0, keepdims=True)
                dnkk = jnp.sum(sp * dsa, axis=0, keepdims=True)
                drow_refs[3][0, pl.ds(t, 1), :] = db * rv[4] - dnkk
                drow_refs[4][0, pl.ds(t, 1), :] = db * rv[3]
                ds_prev = ds_prev - dsa * rv[3]
            return ds_prev, jnp.where(lane == t, dvcol, dv)

        ds, dv = lax.fori_loop(0, Tc, bstep, (ds_ref[...], jnp.zeros((Dv, Tc), F32)))
        ds_ref[...] = ds
        dvT_ref[0] = dv

    rs = pl.BlockSpec((1, Tc, Dk), lambda h, c: (h, nc - 1 - c, 0))
    vs = pl.BlockSpec((1, Dv, Tc), lambda h, c: (h, 0, nc - 1 - c))
    outs = pl.pallas_call(
        body, grid=(H, nc),
        in_specs=[rs] * nrow + [vs, vs, pl.BlockSpec((1, 1, Dv, Dk), lambda h, c: (h, nc - 1 - c, 0, 0))],
        out_specs=[rs] * nrow + [vs],
        out_shape=[jax.ShapeDtypeStruct((H, S, Dk), F32)] * nrow + [jax.ShapeDtypeStruct((H, Dv, S), F32)],
        scratch_shapes=[pltpu.VMEM((Tc, Dv, Dk), F32), pltpu.VMEM((Dv, Dk), F32)],
        compiler_params=_cparams(("parallel", "arbitrary")), name=name)(*rows, vT, dyT, sb)
    return list(outs[:nrow]), outs[nrow]


CHUNK = 128


def chunk_fwd(fn, name, blocks, state_shape, out_width):
    H, S, _ = blocks[0].shape
    nc = S // CHUNK
    nb = len(blocks)

    def body(*refs):
        o_ref, sv_ref, st = refs[nb:]

        @pl.when(pl.program_id(1) == 0)
        def _():
            st[...] = jnp.zeros_like(st)

        s0 = st[...]
        sv_ref[0, 0] = s0
        s1, out = fn(s0, *[r[0] for r in refs[:nb]])
        st[...] = s1
        o_ref[0] = out

    spec = lambda w: pl.BlockSpec((1, CHUNK, w), lambda h, c: (h, c, 0))
    return pl.pallas_call(
        body, grid=(H, nc), in_specs=[spec(b.shape[2]) for b in blocks],
        out_specs=[spec(out_width), pl.BlockSpec((1, 1) + state_shape, lambda h, c: (h, c, 0, 0))],
        out_shape=[jax.ShapeDtypeStruct((H, S, out_width), F32), jax.ShapeDtypeStruct((H, nc) + state_shape, F32)],
        scratch_shapes=[pltpu.VMEM(state_shape, F32)],
        compiler_params=_cparams(("parallel", "arbitrary")), name=name)(*blocks)


def chunk_bwd(fn, name, blocks, states, dout):
    H, S, _ = blocks[0].shape
    nc = S // CHUNK
    nb = len(blocks)
    state_shape = states.shape[2:]

    def body(*refs):
        sv_ref, do_ref = refs[nb], refs[nb + 1]
        d_refs = refs[nb + 2:2 * nb + 2]
        dst = refs[2 * nb + 2]

        @pl.when(pl.program_id(1) == 0)
        def _():
            dst[...] = jnp.zeros_like(dst)

        _, vjp = jax.vjp(fn, sv_ref[0, 0], *[r[0] for r in refs[:nb]])
        grads = vjp((dst[...], do_ref[0]))
        dst[...] = grads[0]
        for d_ref, g in zip(d_refs, grads[1:]):
            d_ref[0] = g

    spec = lambda w: pl.BlockSpec((1, CHUNK, w), lambda h, c: (h, nc - 1 - c, 0))
    return pl.pallas_call(
        body, grid=(H, nc),
        in_specs=[spec(b.shape[2]) for b in blocks]
        + [pl.BlockSpec((1, 1) + state_shape, lambda h, c: (h, nc - 1 - c, 0, 0)), spec(dout.shape[2])],
        out_specs=[spec(b.shape[2]) for b in blocks],
        out_shape=[jax.ShapeDtypeStruct(b.shape, F32) for b in blocks],
        scratch_shapes=[pltpu.VMEM(state_shape, F32)],
        compiler_params=_cparams(("parallel", "arbitrary")), name=name)(*blocks, states, dout)


def _bdot(a, b, dims):
    return lax.dot_general(a.astype(BF16), b.astype(BF16), (dims, ((), ())), preferred_element_type=F32)


def ssd_chunk(state, cb, bb, da, xdt):
    T = cb.shape[0]
    ti = lax.broadcasted_iota(jnp.int32, (T, T), 0)
    si = lax.broadcasted_iota(jnp.int32, (T, T), 1)
    mask = ti >= si
    cs = jnp.dot(mask.astype(F32), da, precision=HI, preferred_element_type=F32)
    pick = (lax.broadcasted_iota(jnp.int32, cs.shape, 1) == 0).astype(F32)
    cs_row = lax.dot_general(pick, cs, (((1,), (1,)), ((), ())), precision=HI, preferred_element_type=F32)
    lmat = jnp.where(mask, jnp.exp(jnp.where(mask, cs - cs_row, 0.0)), 0.0)
    scores = _bdot(cb, bb, ((1,), (1,))) * lmat
    y = _bdot(scores, xdt, ((1,), (0,))) + _bdot(cb, state, ((1,), (1,))) * jnp.exp(cs[:, :HEAD_DIM])
    last = cs[T - 1:T, :]
    new_state = state * jnp.exp(last) + _bdot(xdt, bb * jnp.exp(last - cs), ((0,), (0,)))
    return new_state, y


HGRN_SUB = 16


def hgrn_chunk(state, q, k, lf, v):
    T, C = q.shape[0], HGRN_SUB
    ti = lax.broadcasted_iota(jnp.int32, (C, C), 0)
    si = lax.broadcasted_iota(jnp.int32, (C, C), 1)
    tril = (ti >= si).astype(F32)
    row = lax.broadcasted_iota(jnp.int32, (C, q.shape[1]), 0)
    outs = []
    for j in range(T // C):
        qj, kj, lj, vj = (a[j * C:(j + 1) * C] for a in (q, k, lf, v))
        b = jnp.dot(tril, lj, precision=HI, preferred_element_type=F32)
        o = _bdot(qj * jnp.exp(b), state, ((1,), (1,)))
        for s in range(C):
            m = row >= s
            e = jnp.where(m, jnp.exp(jnp.where(m, b - b[s:s + 1], 0.0)), 0.0)
            o = o + jnp.sum(qj * kj[s:s + 1] * e, axis=1, keepdims=True) * vj[s:s + 1]
        last = b[C - 1:C]
        state = state * jnp.exp(last) + _bdot(vj, kj * jnp.exp(last - b), ((0,), (0,)))
        outs.append(o)
    return state, jnp.concatenate(outs, axis=0)


def _to_heads(x, dk):
    S = x.shape[0]
    return jnp.transpose(x.reshape(S, N_HEADS, dk), (1, 0, 2))


def _from_heads(x):
    H, S, dk = x.shape
    return jnp.transpose(x, (1, 0, 2)).reshape(S, H * dk)


def _to_headsT(x):
    S = x.shape[0]
    return jnp.transpose(x.reshape(S, N_HEADS, HEAD_DIM), (1, 2, 0))


def _from_headsT(x):
    H, dv, S = x.shape
    return jnp.transpose(x, (2, 0, 1)).reshape(S, H * dv)


def run_scan_fwd(R, W, K, V, KK, A, dk, name):
    yT, sb = scan_fwd(_to_heads(R, dk), _to_heads(W, dk), _to_heads(K, dk), _to_headsT(V),
                      None if KK is None else _to_heads(KK, dk), None if A is None else _to_heads(A, dk), name)
    return _from_headsT(yT), sb


def run_scan_bwd(R, W, K, V, KK, A, sb, dY, dk, name):
    drows, dvT = scan_bwd(_to_heads(R, dk), _to_heads(W, dk), _to_heads(K, dk), _to_headsT(V),
                          None if KK is None else _to_heads(KK, dk), None if A is None else _to_heads(A, dk),
                          sb, _to_headsT(dY), name)
    return [_from_heads(d) for d in drows], _from_headsT(dvT)


def _attn_block(q, kp, kc, vp, vc, n, slope, dilation):
    blk = ATTN_BLK
    k2 = jnp.concatenate([kp, kc], axis=0)
    v2 = jnp.concatenate([vp, vc], axis=0)
    s = lax.dot_general(q, k2, (((1,), (1,)), ((), ())), precision=HI, preferred_element_type=F32) * (HEAD_DIM ** -0.5)
    i = lax.broadcasted_iota(jnp.int32, (blk, 2 * blk), 0)
    j = lax.broadcasted_iota(jnp.int32, (blk, 2 * blk), 1)
    dist = blk + i - j
    first_key = jnp.where(n > 0, 0, blk)
    valid = (dist >= 0) & (dist <= blk) & (j >= first_key)
    s = s - slope * (dist * dilation).astype(F32)
    s = jnp.where(valid, s, -1e30)
    m = jnp.max(s, axis=-1, keepdims=True)
    p = jnp.exp(s - m)
    l = jnp.sum(p, axis=-1, keepdims=True)
    o = jnp.dot(p, v2, precision=HI, preferred_element_type=F32) / l
    lse = jnp.broadcast_to(m + jnp.log(l), o.shape)
    return o, lse


def _slope_of(h):
    s = jnp.float32(ALIBI_SLOPES[N_HEADS - 1])
    for hh in range(N_HEADS - 2, -1, -1):
        s = jnp.where(h == hh, jnp.float32(ALIBI_SLOPES[hh]), s)
    return s


def _attn_specs(nb):
    blk = ATTN_BLK
    cur = pl.BlockSpec((1, 1, blk, HEAD_DIM), lambda h, z, n: (h, z, n, 0))
    prev = pl.BlockSpec((1, 1, blk, HEAD_DIM), lambda h, z, n: (h, z, jnp.maximum(n - 1, 0), 0))
    return cur, prev


def attn_fwd(q, k, v, dilation, name):
    H, Z, L, _ = q.shape
    nb = L // ATTN_BLK
    cur, prev = _attn_specs(nb)

    def body(q_ref, kp_ref, kc_ref, vp_ref, vc_ref, o_ref, l_ref):
        o, lse = _attn_block(q_ref[0, 0], kp_ref[0, 0], kc_ref[0, 0], vp_ref[0, 0], vc_ref[0, 0],
                             pl.program_id(2), _slope_of(pl.program_id(0)), dilation)
        o_ref[0, 0] = o
        l_ref[0, 0] = lse

    return pl.pallas_call(body, grid=(H, Z, nb), in_specs=[cur, prev, cur, prev, cur], out_specs=[cur, cur],
                          out_shape=[jax.ShapeDtypeStruct(q.shape, F32)] * 2,
                          compiler_params=_cparams(("parallel", "parallel", "arbitrary")), name=name)(q, k, k, v, v)


def attn_bwd(q, k, v, do, dlse, dilation, name):
    H, Z, L, _ = q.shape
    blk = ATTN_BLK
    nb = L // blk
    cur, prev = _attn_specs(nb)
    full = pl.BlockSpec((1, 1, L, HEAD_DIM), lambda h, z, n: (h, z, 0, 0))

    def body(q_ref, kp_ref, kc_ref, vp_ref, vc_ref, do_ref, dl_ref, dq_ref, dk_ref, dv_ref):
        n = pl.program_id(2)
        slope = _slope_of(pl.program_id(0))
        f = lambda q_, kp_, kc_, vp_, vc_: _attn_block(q_, kp_, kc_, vp_, vc_, n, slope, dilation)
        _, vjp = jax.vjp(f, q_ref[0, 0], kp_ref[0, 0], kc_ref[0, 0], vp_ref[0, 0], vc_ref[0, 0])
        dq, dkp, dkc, dvp, dvc = vjp((do_ref[0, 0], dl_ref[0, 0]))
        dq_ref[0, 0] = dq

        @pl.when(n == 0)
        def _():
            dk_ref[...] = jnp.zeros_like(dk_ref)
            dv_ref[...] = jnp.zeros_like(dv_ref)

        here = pl.ds(pl.multiple_of(n * blk, blk), blk)
        dk_ref[0, 0, here, :] += dkc
        dv_ref[0, 0, here, :] += dvc

        @pl.when(n > 0)
        def _():
            before = pl.ds(pl.multiple_of((n - 1) * blk, blk), blk)
            dk_ref[0, 0, before, :] += dkp
            dv_ref[0, 0, before, :] += dvp

    return pl.pallas_call(body, grid=(H, Z, nb), in_specs=[cur, prev, cur, prev, cur, cur, cur],
                          out_specs=[cur, full, full], out_shape=[jax.ShapeDtypeStruct(q.shape, F32)] * 3,
                          compiler_params=_cparams(("parallel", "parallel", "arbitrary")), name=name)(q, k, k, v, v, do, dlse)


def _to_sub(t, d):
    S = t.shape[0]
    return jnp.transpose(t.reshape(S // d, d, N_HEADS, HEAD_DIM), (2, 1, 0, 3))


def _from_sub(t):
    H, d, L, _ = t.shape
    return jnp.transpose(t, (2, 1, 0, 3)).reshape(L * d, H * HEAD_DIM)


def _head_ones(width, group):
    i = lax.broadcasted_iota(jnp.int32, (width, width), 0) // group
    j = lax.broadcasted_iota(jnp.int32, (width, width), 1) // group
    return (i == j).astype(F32)


def _group_sum(x, group):
    return jnp.dot(x, _head_ones(x.shape[1], group), precision=HI, preferred_element_type=F32)


def _spread(width_in, width_out, rep):
    i = lax.broadcasted_iota(jnp.int32, (width_in, width_out), 0)
    j = lax.broadcasted_iota(jnp.int32, (width_in, width_out), 1) // rep
    return (i == j).astype(F32)


def _hdot(a, b):
    return jnp.dot(a, b, precision=HI, preferred_element_type=F32)


def _sigmoid(x):
    return 1.0 / (1.0 + jnp.exp(-x))


def _softplus(x):
    return jnp.maximum(x, 0.0) + jnp.log(1.0 + jnp.exp(jnp.minimum(x, -x)))


def _silu(x):
    return x * _sigmoid(x)


def rwkv_pre(layer):
    def fn(*args):
        if layer == 0:
            fs, w0, w2p, a0, a2p, g2p, k_k, k_a = args
        else:
            fs, vfirst, w0, w2p, a0, a2p, g2p, k_k, k_a, v0, v2p = args
        r, k, v = fs[:, 0:256], fs[:, 256:512], fs[:, 512:768]
        lora = fs[:, 768:896]
        w_log = -_softplus(-(w0 + _hdot(jnp.tanh(lora), w2p))) - 0.5
        decay = jnp.exp(-jnp.exp(w_log))
        a = _sigmoid(a0 + _hdot(lora, a2p))
        g = _hdot(_sigmoid(lora), g2p)
        if layer > 0:
            v = v + (vfirst - v) * _sigmoid(v0 + _hdot(fs[:, 896:1024], v2p))
        kk = k * k_k
        kk = kk / jnp.maximum(jnp.sqrt(_group_sum(kk * kk, HEAD_DIM)), 1e-12)
        k = k * (1.0 + (a - 1.0) * k_a)
        return r, decay, k, v, kk, a, g
    return fn


def rwkv_post(y, r, k, v, g, lnx_w, lnx_b, r_k):
    mu = _group_sum(y, HEAD_DIM) * (1.0 / HEAD_DIM)
    yc = y - mu
    var = _group_sum(yc * yc, HEAD_DIM) * (1.0 / HEAD_DIM)
    yn = yc * lax.rsqrt(var + RWKV_GN_EPS) * lnx_w + lnx_b
    bonus = _group_sum(r * k * r_k, HEAD_DIM) * v
    return ((yn + bonus) * g,)


def attn_combine(o1, o2, o3, l1, l2, l3):
    m = jnp.maximum(jnp.maximum(l1, l2), l3)
    e1, e2, e3 = jnp.exp(l1 - m), jnp.exp(l2 - m), jnp.exp(l3 - m)
    return ((o1 * e1 + o2 * e2 + o3 * e3) / (e1 + e2 + e3),)


def ssd_pre(xc, dtr, conv_b, dt_bias, a_log):
    xbc = _silu(xc + conv_b)
    xs, bm, cm = xbc[:, 0:256], xbc[:, 256:512], xbc[:, 512:768]
    dt = _softplus(dtr + dt_bias)
    a_neg = -jnp.exp(a_log)
    wide = _spread(128, N_HEADS * SSD_STATE, SSD_STATE)
    w = _hdot(dt, wide) * _hdot(a_neg, wide)
    xdt = xs * _hdot(dt, _spread(128, D_GROUP, HEAD_DIM))
    rr = jnp.concatenate([cm[:, 0:128], cm[:, 0:128], cm[:, 128:256], cm[:, 128:256]], axis=1)
    kk = jnp.concatenate([bm[:, 0:128], bm[:, 0:128], bm[:, 128:256], bm[:, 128:256]], axis=1)
    return rr, w, kk, xdt, xs


def ssd_post(ys, z, xs, d_skip, norm_w):
    y = ys + xs * _hdot(d_skip, _spread(128, D_GROUP, HEAD_DIM))
    y = y * _silu(z)
    half = D_GROUP // 2
    parts = []
    for g in range(2):
        t = y[:, g * half:(g + 1) * half]
        parts.append(t * lax.rsqrt(jnp.mean(t * t, axis=-1, keepdims=True) + RMS_EPS))
    return (jnp.concatenate(parts, axis=1) * norm_w,)


def hgrn_pre(seg, lb):
    q, f, i = seg[:, 0:256], seg[:, 256:512], seg[:, 512:768]
    forget = lb + (1.0 - lb) * _sigmoid(f)
    return _silu(q), 1.0 - forget, jnp.log(forget), i


def hgrn_post(o, seg, norm_w):
    g = seg[:, 768:1024]
    ms = _group_sum(o * o, HEAD_DIM) * (1.0 / HEAD_DIM)
    return (o * lax.rsqrt(ms + RMS_EPS) * norm_w * _silu(g),)


def ln_res(x, y, w, b):
    z = ALPHA * x + y
    mu = jnp.mean(z, axis=-1, keepdims=True)
    zc = z - mu
    var = jnp.mean(zc * zc, axis=-1, keepdims=True)
    return (zc * lax.rsqrt(var + LN_EPS) * w + b,)


def relu2(u):
    r = jnp.maximum(u, 0.0)
    return (r * r,)


def loss_and_grad(y, tgt, name):
    S, D = y.shape
    tile = ROW_TILE

    def body(y_ref, t_ref, l_ref, dy_ref):
        e = y_ref[...] - t_ref[...]
        dy_ref[...] = e * (1.0 / D)

        @pl.when(pl.program_id(0) == 0)
        def _():
            l_ref[...] = jnp.zeros_like(l_ref)

        per_row = 0.5 * jnp.mean(e * e, axis=-1, keepdims=True)
        l_ref[...] += jnp.sum(per_row, axis=0, keepdims=True) * jnp.ones((1, 128), F32)

    return pl.pallas_call(body, grid=(S // tile,), in_specs=[_row_spec(D, tile)] * 2,
                          out_specs=[_par_spec((1, 128)), _row_spec(D, tile)],
                          out_shape=[jax.ShapeDtypeStruct((1, 128), F32), jax.ShapeDtypeStruct((S, D), F32)],
                          compiler_params=_cparams(("arbitrary",)), name=name)(y, tgt)


def add_rows(arrs, name):
    (out,) = tl_fwd(lambda *a: (functools.reduce(lambda p, q: p + q, a),), name, arrs, [], [arrs[0].shape[1]])
    return out


def small_fwd(fn, name, ins, out_shapes):
    n = len(ins)

    def body(*refs):
        outs = fn(*[r[...] for r in refs[:n]])
        for o_ref, o in zip(refs[n:], outs):
            o_ref[...] = o

    return pl.pallas_call(body, out_shape=[jax.ShapeDtypeStruct(s, F32) for s in out_shapes], name=name)(*ins)


def small_bwd(fn, name, ins, cts):
    n, m = len(ins), len(cts)

    def body(*refs):
        _, vjp = jax.vjp(lambda *a: tuple(fn(*a)), *[r[...] for r in refs[:n]])
        grads = vjp(tuple(r[...] for r in refs[n:n + m]))
        for o_ref, g in zip(refs[n + m:], grads):
            o_ref[...] = g

    return pl.pallas_call(body, out_shape=[jax.ShapeDtypeStruct(a.shape, F32) for a in ins], name=name)(*ins, *cts)


def param_prep(lower_bounds, mu0, mu1):
    e = jnp.exp(lower_bounds - jnp.max(lower_bounds, axis=0, keepdims=True))
    sm = e / jnp.sum(e, axis=0, keepdims=True)
    lb0 = sm[0:1] - sm[0:1]
    lb1 = sm[0:1] + sm[1:2] - sm[0:1]
    return lb0, lb1, mu0, 1.0 - mu0, mu1, 1.0 - mu1


def _rows_tile(rows):
    return _pick(rows, (256, 128, 64, 32, 16, 8))


def sum_parts(parts, name):
    P, rows, cols = parts.shape
    tile = _rows_tile(rows)

    def body(p_ref, o_ref):
        acc = p_ref[0]
        for p in range(1, P):
            acc = acc + p_ref[p]
        o_ref[...] = acc

    return pl.pallas_call(body, grid=(rows // tile,), in_specs=[pl.BlockSpec((P, tile, cols), lambda i: (0, i, 0))],
                          out_specs=pl.BlockSpec((tile, cols), lambda i: (i, 0)),
                          out_shape=jax.ShapeDtypeStruct((rows, cols), F32),
                          compiler_params=_cparams(("parallel",)), name=name)(parts)


def adamw(w, g, m, v, name):
    rows, cols = w.shape
    tile = _rows_tile(rows)

    def body(w_ref, g_ref, m_ref, v_ref, d_ref, nm_ref, nv_ref):
        gv = g_ref[...]
        nm = ADAM_B1 * m_ref[...] + (1.0 - ADAM_B1) * gv
        nv = ADAM_B2 * v_ref[...] + (1.0 - ADAM_B2) * jnp.square(gv)
        m_hat = nm / (1.0 - ADAM_B1 ** ADAM_STEP)
        v_hat = nv / (1.0 - ADAM_B2 ** ADAM_STEP)
        d_ref[...] = -ADAM_LR * (m_hat / (jnp.sqrt(v_hat) + ADAM_EPS) + ADAM_WD * w_ref[...])
        nm_ref[...] = nm
        nv_ref[...] = nv

    spec = pl.BlockSpec((tile, cols), lambda i: (i, 0))
    return pl.pallas_call(body, grid=(rows // tile,), in_specs=[spec] * 4, out_specs=[spec] * 3,
                          out_shape=[jax.ShapeDtypeStruct((rows, cols), F32)] * 3,
                          compiler_params=_cparams(("parallel",)), name=name)(w, g, m, v)


MESH = pl.DeviceIdType.MESH
ANY = pl.BlockSpec(memory_space=pl.ANY)


def _flip(v, bit):
    return 1 - v if bit else v


def gather_chips(arrs, name):
    n = len(arrs)

    def body(*refs):
        ins, outs = refs[:n], refs[n:2 * n]
        send, recv, loc = refs[2 * n:]
        x, y, c = lax.axis_index("x"), lax.axis_index("y"), lax.axis_index("c")
        me = 2 * x + y
        pending = []
        for i in range(n):
            lc = pltpu.make_async_copy(ins[i], outs[i].at[me], loc.at[i])
            lc.start()
            pending.append(lc)
            for r, (bx, by) in enumerate(((1, 0), (0, 1), (1, 1))):
                cp = pltpu.make_async_remote_copy(src_ref=ins[i], dst_ref=outs[i].at[me], send_sem=send.at[i, r],
                                                  recv_sem=recv.at[i, r], device_id=(_flip(x, bx), _flip(y, by), c),
                                                  device_id_type=MESH)
                cp.start()
                pending.append(cp)
        for cp in pending:
            cp.wait()

    return pl.pallas_call(
        body, in_specs=[ANY] * n, out_specs=[ANY] * n,
        out_shape=[jax.ShapeDtypeStruct((4,) + a.shape, a.dtype) for a in arrs],
        scratch_shapes=[pltpu.SemaphoreType.DMA((n, 3)), pltpu.SemaphoreType.DMA((n, 3)), pltpu.SemaphoreType.DMA((n,))],
        name=name)(*arrs)


_RELATIONS = tuple((r >> 2 & 1, r >> 1 & 1, r & 1) for r in range(1, 8))


def gather_devices(arr, name):
    def body(in_ref, out_ref, send, recv, loc):
        x, y, c = lax.axis_index("x"), lax.axis_index("y"), lax.axis_index("c")
        me = 4 * x + 2 * y + c
        lc = pltpu.make_async_copy(in_ref, out_ref.at[me], loc)
        lc.start()
        pending = [lc]
        for r, (bx, by, bc) in enumerate(_RELATIONS):
            cp = pltpu.make_async_remote_copy(src_ref=in_ref, dst_ref=out_ref.at[me], send_sem=send.at[r],
                                              recv_sem=recv.at[r], device_id=(_flip(x, bx), _flip(y, by), _flip(c, bc)),
                                              device_id_type=MESH)
            cp.start()
            pending.append(cp)
        for cp in pending:
            cp.wait()

    return pl.pallas_call(
        body, in_specs=[ANY], out_specs=ANY, out_shape=jax.ShapeDtypeStruct((8,) + arr.shape, arr.dtype),
        scratch_shapes=[pltpu.SemaphoreType.DMA((7,)), pltpu.SemaphoreType.DMA((7,)), pltpu.SemaphoreType.DMA(())],
        name=name)(arr)


def scatter_partials(arrs, name):
    n = len(arrs)

    def body(*refs):
        ins, outs = refs[:n], refs[n:2 * n]
        send, recv, loc = refs[2 * n:]
        x, y, c = lax.axis_index("x"), lax.axis_index("y"), lax.axis_index("c")
        me = 4 * x + 2 * y + c
        pending = []
        for i in range(n):
            lc = pltpu.make_async_copy(ins[i].at[c, 2 * x + y], outs[i].at[me], loc.at[i])
            lc.start()
            pending.append(lc)
            for r, (bx, by, bc) in enumerate(_RELATIONS):
                px, py, pc = _flip(x, bx), _flip(y, by), _flip(c, bc)
                cp = pltpu.make_async_remote_copy(src_ref=ins[i].at[pc, 2 * px + py], dst_ref=outs[i].at[me],
                                                  send_sem=send.at[i, r], recv_sem=recv.at[i, r],
                                                  device_id=(px, py, pc), device_id_type=MESH)
                cp.start()
                pending.append(cp)
        for cp in pending:
            cp.wait()

    return pl.pallas_call(
        body, in_specs=[ANY] * n, out_specs=[ANY] * n,
        out_shape=[jax.ShapeDtypeStruct((8,) + a.shape[2:], a.dtype) for a in arrs],
        scratch_shapes=[pltpu.SemaphoreType.DMA((n, 7)), pltpu.SemaphoreType.DMA((n, 7)), pltpu.SemaphoreType.DMA((n,))],
        name=name)(*arrs)


def sibling_exchange(arrs, name):
    n = len(arrs)

    def body(*refs):
        ins, outs = refs[:n], refs[n:2 * n]
        send, recv, loc = refs[2 * n:]
        x, y, c = lax.axis_index("x"), lax.axis_index("y"), lax.axis_index("c")
        pending = []
        for i in range(n):
            lc = pltpu.make_async_copy(ins[i], outs[i].at[c], loc.at[i])
            lc.start()
            cp = pltpu.make_async_remote_copy(src_ref=ins[i], dst_ref=outs[i].at[c], send_sem=send.at[i],
                                              recv_sem=recv.at[i], device_id=(x, y, 1 - c), device_id_type=MESH)
            cp.start()
            pending += [lc, cp]
        for cp in pending:
            cp.wait()

    return pl.pallas_call(
        body, in_specs=[ANY] * n, out_specs=[ANY] * n,
        out_shape=[jax.ShapeDtypeStruct((2,) + a.shape, a.dtype) for a in arrs],
        scratch_shapes=[pltpu.SemaphoreType.DMA((n,)), pltpu.SemaphoreType.DMA((n,)), pltpu.SemaphoreType.DMA((n,))],
        name=name)(*arrs)


def rwkv_fwd(l, seg, taps, pars, vfirst):
    fs = fir_fwd(seg, taps, f"rwkv_shift_fwd{l}")
    rows = [fs] + ([vfirst] if l else [])
    R, W, K, V, KK, A, G = tl_fwd(rwkv_pre(l), f"rwkv_pre_fwd{l}", rows, pars["pre"], [D_GROUP] * 7)
    Y, sb = run_scan_fwd(R, W, K, V, KK, A, HEAD_DIM, f"rwkv_scan_fwd{l}")
    (out,) = tl_fwd(rwkv_post, f"rwkv_post_fwd{l}", [Y, R, K, V, G], pars["post"], [D_GROUP])
    return out, V, (seg, taps, rows, R, W, K, V, KK, A, G, Y, sb)


def rwkv_bwd(l, saved, pars, dout, dv_extra):
    seg, taps, rows, R, W, K, V, KK, A, G, Y, sb = saved
    (dY, dR1, dK1, dV1, dG), dpost = tl_bwd(rwkv_post, f"rwkv_post_bwd{l}", [Y, R, K, V, G], pars["post"], [[dout]])
    (dR2, dW, dK2, dKK, dA), dV2 = run_scan_bwd(R, W, K, V, KK, A, sb, dY, HEAD_DIM, f"rwkv_scan_bwd{l}")
    cts = [[dR1, dR2], [dW], [dK1, dK2], [dV1, dV2] + dv_extra, [dKK], [dA], [dG]]
    drows, dpre = tl_bwd(rwkv_pre(l), f"rwkv_pre_bwd{l}", rows, pars["pre"], cts)
    dseg, dtaps, _ = fir_bwd(seg, taps, [drows[0]], f"rwkv_shift_bwd{l}")
    return dseg, (drows[1] if l else None), dtaps, dpre, dpost


def attn_mix_fwd(l, q, k, v):
    subs, os_, ls_ = [], [], []
    for b, (_, d) in enumerate(DILATED_BRANCHES):
        qs, ks, vs = _to_sub(q, d), _to_sub(k, d), _to_sub(v, d)
        o, lse = attn_fwd(qs, ks, vs, d, f"attn_fwd{l}_{b}")
        subs.append((qs, ks, vs))
        os_.append(_from_sub(o))
        ls_.append(_from_sub(lse))
    (out,) = tl_fwd(attn_combine, f"attn_combine_fwd{l}", os_ + ls_, [], [D_GROUP])
    return out, (subs, os_, ls_)


def attn_mix_bwd(l, saved, dout):
    subs, os_, ls_ = saved
    drows, _ = tl_bwd(attn_combine, f"attn_combine_bwd{l}", os_ + ls_, [], [[dout]])
    dqs, dks, dvs = [], [], []
    for b, (_, d) in enumerate(DILATED_BRANCHES):
        qs, ks, vs = subs[b]
        dq, dk, dv = attn_bwd(qs, ks, vs, _to_sub(drows[b], d), _to_sub(drows[3 + b], d), d, f"attn_bwd{l}_{b}")
        dqs.append(_from_sub(dq))
        dks.append(_from_sub(dk))
        dvs.append(_from_sub(dv))
    return add_rows(dqs, f"attn_dq{l}"), add_rows(dks, f"attn_dk{l}"), add_rows(dvs, f"attn_dv{l}")


def ssd_fwd(l, z, xbc, dtr, pars):
    xc = fir_fwd(xbc, pars["taps"], f"ssd_conv_fwd{l}")
    rr, w, kk, xdt, xs = tl_fwd(ssd_pre, f"ssd_pre_fwd{l}", [xc, dtr], pars["pre"], [512, 512, 512, D_GROUP, D_GROUP])
    blocks = [_to_heads(rr, SSD_STATE), _to_heads(kk, SSD_STATE), _to_heads(w, SSD_STATE), _to_heads(xdt, HEAD_DIM)]
    yh, states = chunk_fwd(ssd_chunk, f"ssd_scan_fwd{l}", blocks, (HEAD_DIM, SSD_STATE), HEAD_DIM)
    ys = _from_heads(yh)
    (out,) = tl_fwd(ssd_post, f"ssd_post_fwd{l}", [ys, z, xs], pars["post"], [D_GROUP])
    return out, (z, xbc, dtr, xc, blocks, states, xs, ys)


def ssd_bwd(l, saved, pars, dout):
    z, xbc, dtr, xc, blocks, states, xs, ys = saved
    (dys, dz, dxs), dpost = tl_bwd(ssd_post, f"ssd_post_bwd{l}", [ys, z, xs], pars["post"], [[dout]])
    drr, dkk, dw, dxdt = (_from_heads(g) for g in
                          chunk_bwd(ssd_chunk, f"ssd_scan_bwd{l}", blocks, states, _to_heads(dys, HEAD_DIM)))
    (dxc, ddtr), dpre = tl_bwd(ssd_pre, f"ssd_pre_bwd{l}", [xc, dtr], pars["pre"], [[drr], [dw], [dkk], [dxdt], [dxs]])
    dxbc, dtaps, _ = fir_bwd(xbc, pars["taps"], [dxc], f"ssd_conv_bwd{l}")
    return dz, dxbc, ddtr, dtaps, dpre, dpost


def hgrn_fwd(l, seg, pars):
    q, kk, lf, i = tl_fwd(hgrn_pre, f"hgrn_pre_fwd{l}", [seg], pars["pre"], [D_GROUP] * 4)
    blocks = [_to_heads(t, HEAD_DIM) for t in (q, kk, lf, i)]
    oh, states = chunk_fwd(hgrn_chunk, f"hgrn_scan_fwd{l}", blocks, (HEAD_DIM, HEAD_DIM), HEAD_DIM)
    o = _from_heads(oh)
    (out,) = tl_fwd(hgrn_post, f"hgrn_post_fwd{l}", [o, seg], pars["post"], [D_GROUP])
    return out, (seg, blocks, states, o)


def hgrn_bwd(l, saved, pars, dout):
    seg, blocks, states, o = saved
    (do, dseg1), dpost = tl_bwd(hgrn_post, f"hgrn_post_bwd{l}", [o, seg], pars["post"], [[dout]])
    dq, dkk, dlf, di = (_from_heads(g) for g in
                        chunk_bwd(hgrn_chunk, f"hgrn_scan_bwd{l}", blocks, states, _to_heads(do, HEAD_DIM)))
    (dseg2,), dpre = tl_bwd(hgrn_pre, f"hgrn_pre_bwd{l}", [seg], pars["pre"], [[dq], [dkk], [dlf], [di]])
    return add_rows([dseg1, dseg2], f"hgrn_dseg{l}"), dpre, dpost


def layer_fwd(l, x, wts, pars, vfirst):
    proj = matmul(x, wts["in"], "nn", f"proj_fwd{l}")
    seg_h = proj[:, SEG_HGRN:SEG_HGRN + 1024]
    seg_r = proj[:, SEG_RWKV:SEG_RWKV + 1024]
    q, k, v = (proj[:, SEG_Q + j * D_GROUP:SEG_Q + (j + 1) * D_GROUP] for j in range(3))
    z = proj[:, SEG_Z:SEG_Z + D_GROUP]
    xbc = proj[:, SEG_XBC:SEG_XBC + SSD_XBC]
    dtr = proj[:, SEG_DT:SEG_DT + 128]
    ya, v_rwkv, sa = rwkv_fwd(l, seg_r, pars["rwkv"]["taps"], pars["rwkv"], vfirst)
    yb, sb = attn_mix_fwd(l, q, k, v)
    yc, sc = ssd_fwd(l, z, xbc, dtr, pars["ssd"])
    yd, sd = hgrn_fwd(l, seg_h, pars["hgrn"])
    mix = jnp.concatenate([ya, yb, yc, yd], axis=1)
    mo = matmul(mix, wts["out"], "nn", f"out_fwd{l}")
    (x1,) = tl_fwd(ln_res, f"ln1_fwd{l}", [x, mo], pars["ln1"], [D_MODEL])
    u = matmul(x1, wts["up"], "nn", f"up_fwd{l}")
    (h,) = tl_fwd(relu2, f"relu2_fwd{l}", [u], [], [D_FF])
    dn = matmul(h, wts["down"], "nn", f"down_fwd{l}")
    (x2,) = tl_fwd(ln_res, f"ln2_fwd{l}", [x1, dn], pars["ln2"], [D_MODEL])
    return x2, v_rwkv, (x, sa, sb, sc, sd, mix, mo, x1, u, h, dn)


def layer_bwd(l, saved, wts, pars, dx2, dv_extra):
    x, sa, sb, sc, sd, mix, mo, x1, u, h, dn = saved
    S = x.shape[0]
    g = {}
    (dx1a, ddn), g["ln2"] = tl_bwd(ln_res, f"ln2_bwd{l}", [x1, dn], pars["ln2"], [[dx2]])
    g["down"] = matmul(h, ddn, "tn", f"down_dw{l}")
    dh = matmul(ddn, wts["down"], "nt", f"down_dx{l}")
    (du,), _ = tl_bwd(relu2, f"relu2_bwd{l}", [u], [], [[dh]])
    g["up"] = matmul(x1, du, "tn", f"up_dw{l}")
    dx1 = matmul(du, wts["up"], "nt", f"up_dx{l}", add=dx1a)
    (dxa, dmo), g["ln1"] = tl_bwd(ln_res, f"ln1_bwd{l}", [x, mo], pars["ln1"], [[dx1]])
    g["out"] = matmul(mix, dmo, "tn", f"out_dw{l}")
    dmix = matmul(dmo, wts["out"], "nt", f"out_dx{l}")
    dya, dyb, dyc, dyd = (dmix[:, j * D_GROUP:(j + 1) * D_GROUP] for j in range(4))
    dseg_r, dvfirst, g["rwkv_taps"], g["rwkv_pre"], g["rwkv_post"] = rwkv_bwd(l, sa, pars["rwkv"], dya, dv_extra)
    dq, dk, dv = attn_mix_bwd(l, sb, dyb)
    dz, dxbc, ddtr, g["ssd_taps"], g["ssd_pre"], g["ssd_post"] = ssd_bwd(l, sc, pars["ssd"], dyc)
    dseg_h, g["hgrn_pre"], g["hgrn_post"] = hgrn_bwd(l, sd, pars["hgrn"], dyd)
    dproj = jnp.concatenate([dseg_h, dseg_r, dq, dk, dv, dz, dxbc, ddtr, jnp.zeros((S, PROJ_W - SEG_DT - 128), F32)], axis=1)
    g["in"] = matmul(x, dproj, "tn", f"proj_dw{l}")
    dx = matmul(dproj, wts["in"], "nt", f"proj_dx{l}", add=dxa)
    return dx, dvfirst, g


SMALL = ("lower_bounds", "w_in_vres", "mu_shift", "mu_vres", "rwkv_w0", "rwkv_w2", "rwkv_a0", "rwkv_a2", "rwkv_g2",
         "rwkv_k_k", "rwkv_k_a", "rwkv_r_k", "rwkv_lnx_w", "rwkv_lnx_b", "rwkv_v0", "rwkv_v2", "ssd_conv_w",
         "ssd_conv_b", "ssd_dt_bias", "ssd_A_log", "ssd_D", "ssd_norm_w", "hgrn_norm_w", "ln1_w", "ln1_b", "ln2_w", "ln2_b")
BIG = ("w_in", "w_out", "w_up", "w_down")
SMALL_SHARDED = {"w_in_vres": 1, "rwkv_w2": 2, "rwkv_a2": 2, "rwkv_g2": 2, "rwkv_v2": 2, "ssd_conv_w": 2}
WEIGHTS = ("lower_bounds", "w_in", "w_in_vres", "mu_shift", "mu_vres", "rwkv_w0", "rwkv_w2", "rwkv_a0", "rwkv_a2",
           "rwkv_g2", "rwkv_k_k", "rwkv_k_a", "rwkv_r_k", "rwkv_lnx_w", "rwkv_lnx_b", "rwkv_v0", "rwkv_v2",
           "ssd_conv_w", "ssd_conv_b", "ssd_dt_bias", "ssd_A_log", "ssd_D", "ssd_norm_w", "hgrn_norm_w", "w_out",
           "ln1_w", "ln1_b", "w_up", "w_down", "ln2_w", "ln2_b")


def _row(v, width=None):
    v = v.reshape(1, -1).astype(F32)
    if width is not None and v.shape[1] < width:
        v = jnp.pad(v, ((0, 0), (0, width - v.shape[1])))
    return v


def _rows_at(m, rows, at):
    return jnp.pad(m.astype(F32), ((at, rows - at - m.shape[0]), (0, 0)))


def _pad_w_in(w_in_l, vres):
    rows = w_in_l.shape[0]
    out = []
    order = sorted(_PIECES, key=lambda p: p[2])
    pos = 0
    for start, width, at in order:
        if at > pos:
            out.append(jnp.zeros((rows, at - pos), w_in_l.dtype))
        out.append(w_in_l[:, start:start + width])
        pos = at + width
        if at == SEG_RWKV and vres is not None:
            out.append(vres.astype(w_in_l.dtype))
            pos += vres.shape[1]
    out.append(jnp.zeros((rows, PROJ_W - pos), w_in_l.dtype))
    return jnp.concatenate(out, axis=1)


def _unpad_w_in(g):
    return jnp.concatenate([g[:, at:at + width] for _, width, at in _PIECES], axis=1)


def layer_params(l, sp, prep):
    lb, mu, om = prep[l], prep[2 + 2 * l], prep[3 + 2 * l]
    pre = [_row(sp["rwkv_w0"][l]), _rows_at(sp["rwkv_w2"][l], 128, 0), _row(sp["rwkv_a0"][l]),
           _rows_at(sp["rwkv_a2"][l], 128, 32), _rows_at(sp["rwkv_g2"][l], 128, 64),
           _row(sp["rwkv_k_k"][l]), _row(sp["rwkv_k_a"][l])]
    if l:
        pre += [_row(sp["rwkv_v0"][l - 1]), _rows_at(sp["rwkv_v2"][l - 1], 128, 0)]
    return {
        "rwkv": {"taps": jnp.concatenate([mu, om], axis=0), "pre": pre,
                 "post": [_row(sp["rwkv_lnx_w"][l]), _row(sp["rwkv_lnx_b"][l]), _row(sp["rwkv_r_k"][l])]},
        "ssd": {"taps": sp["ssd_conv_w"][l].astype(F32),
                "pre": [_row(sp["ssd_conv_b"][l]), _row(sp["ssd_dt_bias"][l], 128), _row(sp["ssd_A_log"][l], 128)],
                "post": [_row(sp["ssd_D"][l], 128), _row(sp["ssd_norm_w"][l])]},
        "hgrn": {"pre": [lb], "post": [_row(sp["hgrn_norm_w"][l])]},
        "ln1": [_row(sp["ln1_w"][l]), _row(sp["ln1_b"][l])],
        "ln2": [_row(sp["ln2_w"][l]), _row(sp["ln2_b"][l])],
    }


def _mu_full(sp, l):
    parts = [sp["mu_shift"][l].reshape(1, -1)]
    if l:
        parts.append(sp["mu_vres"][l - 1].reshape(1, -1))
    return _row(jnp.concatenate(parts, axis=1), 1024)


def local_step(x, target, big, sp):
    prep_in = [sp["lower_bounds"].astype(F32), _mu_full(sp, 0), _mu_full(sp, 1)]
    prep = small_fwd(param_prep, "param_prep_fwd", prep_in,
                     [(1, D_GROUP), (1, D_GROUP), (1, 1024), (1, 1024), (1, 1024), (1, 1024)])
    pars, wts = [], []
    for l in range(DEPTH):
        pars.append(layer_params(l, sp, prep))
        vres = sp["w_in_vres"][l - 1].astype(BF16) if l else None
        wts.append({"in": _pad_w_in(big["w_in"][l], vres), "out": big["w_out"][l], "up": big["w_up"][l],
                    "down": big["w_down"][l]})
    h, vfirst, saved = x, None, []
    for l in range(DEPTH):
        h, v_l, sv = layer_fwd(l, h, wts[l], pars[l], vfirst)
        vfirst = v_l if l == 0 else vfirst
        saved.append(sv)
    loss_row, dh = loss_and_grad(h, target, "loss")
    grads, dv_extra = [None] * DEPTH, []
    for l in reversed(range(DEPTH)):
        dh, dvfirst, grads[l] = layer_bwd(l, saved[l], wts[l], pars[l], dh, dv_extra)
        dv_extra = [dvfirst] if l else []
    cts = [grads[0]["hgrn_pre"][0], grads[1]["hgrn_pre"][0]]
    for l in range(DEPTH):
        cts += [grads[l]["rwkv_taps"][0:1], grads[l]["rwkv_taps"][1:2]]
    d_lower, d_mu0, d_mu1 = small_bwd(param_prep, "param_prep_bwd", prep_in, cts)
    d_mu = [d_mu0, d_mu1]
    gb = {"w_in": [_unpad_w_in(grads[l]["in"]) for l in range(DEPTH)], "w_out": [grads[l]["out"] for l in range(DEPTH)],
          "w_up": [grads[l]["up"] for l in range(DEPTH)], "w_down": [grads[l]["down"] for l in range(DEPTH)]}
    st = lambda f: jnp.stack([f(l) for l in range(DEPTH)])
    g1 = grads[1]
    gs = {
        "lower_bounds": d_lower,
        "w_in_vres": g1["in"][None, :, VRES_COL:VRES_COL + 32],
        "mu_shift": st(lambda l: d_mu[l][0, :896]),
        "mu_vres": d_mu[1][:, 896:928],
        "rwkv_w0": st(lambda l: grads[l]["rwkv_pre"][0][0]),
        "rwkv_w2": st(lambda l: grads[l]["rwkv_pre"][1][0:32]),
        "rwkv_a0": st(lambda l: grads[l]["rwkv_pre"][2][0]),
        "rwkv_a2": st(lambda l: grads[l]["rwkv_pre"][3][32:64]),
        "rwkv_g2": st(lambda l: grads[l]["rwkv_pre"][4][64:128]),
        "rwkv_k_k": st(lambda l: grads[l]["rwkv_pre"][5][0]),
        "rwkv_k_a": st(lambda l: grads[l]["rwkv_pre"][6][0]),
        "rwkv_r_k": st(lambda l: grads[l]["rwkv_post"][2].reshape(N_HEADS, HEAD_DIM)),
        "rwkv_lnx_w": st(lambda l: grads[l]["rwkv_post"][0][0]),
        "rwkv_lnx_b": st(lambda l: grads[l]["rwkv_post"][1][0]),
        "rwkv_v0": g1["rwkv_pre"][7],
        "rwkv_v2": g1["rwkv_pre"][8][None, 0:32],
        "ssd_conv_w": st(lambda l: grads[l]["ssd_taps"]),
        "ssd_conv_b": st(lambda l: grads[l]["ssd_pre"][0][0]),
        "ssd_dt_bias": st(lambda l: grads[l]["ssd_pre"][1][0, :N_HEADS]),
        "ssd_A_log": st(lambda l: grads[l]["ssd_pre"][2][0, :N_HEADS]),
        "ssd_D": st(lambda l: grads[l]["ssd_post"][0][0, :N_HEADS]),
        "ssd_norm_w": st(lambda l: grads[l]["ssd_post"][1][0]),
        "hgrn_norm_w": st(lambda l: grads[l]["hgrn_post"][0][0]),
        "ln1_w": st(lambda l: grads[l]["ln1"][0][0]),
        "ln1_b": st(lambda l: grads[l]["ln1"][1][0]),
        "ln2_w": st(lambda l: grads[l]["ln2"][0][0]),
        "ln2_b": st(lambda l: grads[l]["ln2"][1][0]),
    }
    return loss_row, dh, gb, gs


def _pack(vecs):
    flat, meta, pos = [], [], 0
    for v in vecs:
        flat.append(v.reshape(-1).astype(F32))
        meta.append((pos, v.shape))
        pos += v.size
    total = -(-pos // 1024) * 1024
    flat.append(jnp.zeros((total - pos,), F32))
    return jnp.concatenate(flat).reshape(total // 128, 128), meta


def _unpack(packed, meta):
    flat = packed.reshape(-1)
    return [flat[off:off + math.prod(shape)].reshape(shape) for off, shape in meta]


def _to_shards(name, g):
    if name == "w_in":
        return jnp.transpose(g.reshape(g.shape[0], 4, g.shape[1] // 4), (1, 0, 2))
    if name == "w_up":
        return jnp.transpose(g.reshape(g.shape[0], 4, g.shape[1] // 4), (1, 0, 2))
    return g.reshape(4, g.shape[0] // 4, g.shape[1])


def _from_chips(name, g):
    if name in ("w_in", "w_up"):
        return jnp.transpose(g, (1, 2, 0, 3)).reshape(g.shape[1], g.shape[2], 4 * g.shape[3])
    return jnp.transpose(g, (1, 0, 2, 3)).reshape(g.shape[1], 4 * g.shape[2], g.shape[3])


INPUT_NAMES = ("x",) + WEIGHTS + ("loss_target",) + tuple("m_" + n for n in WEIGHTS) + tuple("v_" + n for n in WEIGHTS)


def _step(*args):
    a = dict(zip(INPUT_NAMES, args, strict=True))
    chip = 2 * lax.axis_index("x") + lax.axis_index("y")

    sharded_names = list(SMALL_SHARDED)
    small_pack, small_meta = _pack([a[n] for n in sharded_names])
    gathered = gather_chips([a[n].astype(BF16) for n in BIG] + [small_pack], "gather_weights")
    big = {n: _from_chips(n, g) for n, g in zip(BIG, gathered)}
    sp = {n: a[n] for n in SMALL if n not in SMALL_SHARDED}
    per_chip = [_unpack(gathered[-1][s], small_meta) for s in range(4)]
    for j, n in enumerate(sharded_names):
        sp[n] = jnp.concatenate([per_chip[s][j] for s in range(4)], axis=SMALL_SHARDED[n])

    loss_row, gx, gb, gs = local_step(a["x"][0], a["loss_target"][0], big, sp)

    partials = [jnp.stack([_to_shards(n, gb[n][l]) for l in range(DEPTH)]) for n in BIG]
    slots = scatter_partials(partials, "reduce_big")
    mine = []
    for n, sl in zip(BIG, slots):
        rows, cols = sl.shape[-2:]
        mine.append(sum_parts(sl.reshape(8, rows, cols), f"sum_{n}"))
    summed = sibling_exchange(mine, "exchange_big")
    out_g, out_d, out_m, out_v = {}, {}, {}, {}
    for n, g in zip(BIG, summed):
        shape = a[n].shape
        flat = lambda t: t.reshape(shape[0] * shape[1], shape[2])
        d, nm, nv = adamw(flat(a[n]), flat(g), flat(a["m_" + n]), flat(a["v_" + n]), f"adamw_{n}")
        out_g[n], out_d[n], out_m[n], out_v[n] = g.reshape(shape), d.reshape(shape), nm.reshape(shape), nv.reshape(shape)

    vec, meta = _pack([loss_row] + [gs[n] for n in SMALL])
    total = sum_parts(gather_devices(vec, "gather_small"), "sum_small")
    parts = _unpack(total, meta)
    loss = parts[0][0, 0]
    g_small = {}
    for n, g in zip(SMALL, parts[1:]):
        if n in SMALL_SHARDED:
            ax = SMALL_SHARDED[n]
            size = a[n].shape[ax]
            g = lax.dynamic_slice_in_dim(g, chip * size, size, axis=ax)
        g_small[n] = g
    pw, pmeta = _pack([a[n] for n in SMALL])
    pg, _ = _pack([g_small[n] for n in SMALL])
    pm, _ = _pack([a["m_" + n] for n in SMALL])
    pv, _ = _pack([a["v_" + n] for n in SMALL])
    d, nm, nv = adamw(pw, pg, pm, pv, "adamw_small")
    for n, dd, mm, vv in zip(SMALL, _unpack(d, pmeta), _unpack(nm, pmeta), _unpack(nv, pmeta)):
        out_g[n], out_d[n], out_m[n], out_v[n] = g_small[n], dd, mm, vv

    return (loss, gx[None], *[out_g[n] for n in WEIGHTS], *[out_d[n] for n in WEIGHTS],
            *[out_m[n] for n in WEIGHTS], *[out_v[n] for n in WEIGHTS])


def kernel(x, lower_bounds, w_in, w_in_vres, mu_shift, mu_vres, rwkv_w0, rwkv_w2, rwkv_a0, rwkv_a2, rwkv_g2, rwkv_k_k, rwkv_k_a, rwkv_r_k, rwkv_lnx_w, rwkv_lnx_b, rwkv_v0, rwkv_v2, ssd_conv_w, ssd_conv_b, ssd_dt_bias, ssd_A_log, ssd_D, ssd_norm_w, hgrn_norm_w, w_out, ln1_w, ln1_b, w_up, w_down, ln2_w, ln2_b, loss_target, m_lower_bounds, m_w_in, m_w_in_vres, m_mu_shift, m_mu_vres, m_rwkv_w0, m_rwkv_w2, m_rwkv_a0, m_rwkv_a2, m_rwkv_g2, m_rwkv_k_k, m_rwkv_k_a, m_rwkv_r_k, m_rwkv_lnx_w, m_rwkv_lnx_b, m_rwkv_v0, m_rwkv_v2, m_ssd_conv_w, m_ssd_conv_b, m_ssd_dt_bias, m_ssd_A_log, m_ssd_D, m_ssd_norm_w, m_hgrn_norm_w, m_w_out, m_ln1_w, m_ln1_b, m_w_up, m_w_down, m_ln2_w, m_ln2_b, v_lower_bounds, v_w_in, v_w_in_vres, v_mu_shift, v_mu_vres, v_rwkv_w0, v_rwkv_w2, v_rwkv_a0, v_rwkv_a2, v_rwkv_g2, v_rwkv_k_k, v_rwkv_k_a, v_rwkv_r_k, v_rwkv_lnx_w, v_rwkv_lnx_b, v_rwkv_v0, v_rwkv_v2, v_ssd_conv_w, v_ssd_conv_b, v_ssd_dt_bias, v_ssd_A_log, v_ssd_D, v_ssd_norm_w, v_hgrn_norm_w, v_w_out, v_ln1_w, v_ln1_b, v_w_up, v_w_down, v_ln2_w, v_ln2_b):
    return _step(x, lower_bounds, w_in, w_in_vres, mu_shift, mu_vres, rwkv_w0, rwkv_w2, rwkv_a0, rwkv_a2, rwkv_g2, rwkv_k_k, rwkv_k_a, rwkv_r_k, rwkv_lnx_w, rwkv_lnx_b, rwkv_v0, rwkv_v2, ssd_conv_w, ssd_conv_b, ssd_dt_bias, ssd_A_log, ssd_D, ssd_norm_w, hgrn_norm_w, w_out, ln1_w, ln1_b, w_up, w_down, ln2_w, ln2_b, loss_target, m_lower_bounds, m_w_in, m_w_in_vres, m_mu_shift, m_mu_vres, m_rwkv_w0, m_rwkv_w2, m_rwkv_a0, m_rwkv_a2, m_rwkv_g2, m_rwkv_k_k, m_rwkv_k_a, m_rwkv_r_k, m_rwkv_lnx_w, m_rwkv_lnx_b, m_rwkv_v0, m_rwkv_v2, m_ssd_conv_w, m_ssd_conv_b, m_ssd_dt_bias, m_ssd_A_log, m_ssd_D, m_ssd_norm_w, m_hgrn_norm_w, m_w_out, m_ln1_w, m_ln1_b, m_w_up, m_w_down, m_ln2_w, m_ln2_b, v_lower_bounds, v_w_in, v_w_in_vres, v_mu_shift, v_mu_vres, v_rwkv_w0, v_rwkv_w2, v_rwkv_a0, v_rwkv_a2, v_rwkv_g2, v_rwkv_k_k, v_rwkv_k_a, v_rwkv_r_k, v_rwkv_lnx_w, v_rwkv_lnx_b, v_rwkv_v0, v_rwkv_v2, v_ssd_conv_w, v_ssd_conv_b, v_ssd_dt_bias, v_ssd_A_log, v_ssd_D, v_ssd_norm_w, v_hgrn_norm_w, v_w_out, v_ln1_w, v_ln1_b, v_w_up, v_w_down, v_ln2_w, v_ln2_b)
```

```python
import functools
import math

import jax
import jax.numpy as jnp
from jax import lax
from jax.experimental import pallas as pl
from jax.experimental.pallas import tpu as pltpu

F32 = jnp.float32
BF16 = jnp.bfloat16
HI = lax.Precision.HIGHEST

DEPTH = 2
D_MODEL = 1024
D_GROUP = 256
HEAD_DIM = 64
N_HEADS = 4
SSD_STATE = 128
SSD_XBC = 768
SSD_CONV = 4
D_FF = 4096
ALPHA = (2.0 * DEPTH) ** 0.25
LN_EPS = 1e-5
RMS_EPS = 1e-5
RWKV_GN_EPS = HEAD_DIM * 1e-5
DILATED_BRANCHES = ((128, 1), (512, 4), (2048, 16))
ALIBI_SLOPES = tuple(2.0 ** (-8.0 * (h + 1) / N_HEADS) for h in range(N_HEADS))
ATTN_BLK = 128

ADAM_LR, ADAM_B1, ADAM_B2, ADAM_EPS, ADAM_WD, ADAM_STEP = 0.001, 0.9, 0.999, 1e-08, 0.01, 10

IN_COLS = 3716
PROJ_W = 4096
SEG_HGRN, SEG_RWKV, SEG_Q, SEG_Z, SEG_XBC, SEG_DT = 0, 1024, 2048, 2816, 3072, 3840
_PIECES = ((0, 896, SEG_RWKV), (896, 768, SEG_Q), (1664, 256, SEG_Z), (1920, 768, SEG_XBC),
           (2688, 4, SEG_DT), (2692, 1024, SEG_HGRN))
VRES_COL = SEG_RWKV + 896

ROW_TILE = 256
SCAN_CHUNK = 128
VMEM_LIMIT = 48 * 1024 * 1024


def _cparams(sem=None):
    if sem is None:
        return pltpu.CompilerParams(vmem_limit_bytes=VMEM_LIMIT)
    return pltpu.CompilerParams(dimension_semantics=sem, vmem_limit_bytes=VMEM_LIMIT)


def _pick(n, pref):
    for t in pref:
        if n % t == 0:
            return t
    return n


def matmul(a, b, mode, name, add=None, out_dtype=F32):
    if mode == "nn":
        (M, K), (_, N) = a.shape, b.shape
    elif mode == "nt":
        (M, K), (N, _) = a.shape, b.shape
    else:
        (K, M), (_, N) = a.shape, b.shape
    tm, tn, tk = _pick(M, (1024, 512, 256, 128)), _pick(N, (1024, 512, 256, 128)), _pick(K, (512, 256, 128))
    nk = K // tk
    dims = {"nn": (((1,), (0,)), ((), ())), "nt": (((1,), (1,)), ((), ())), "tn": (((0,), (0,)), ((), ()))}[mode]

    def body(*refs):
        if add is None:
            a_ref, b_ref, o_ref, acc = refs
            add_ref = None
        else:
            a_ref, b_ref, add_ref, o_ref, acc = refs
        k = pl.program_id(2)

        @pl.when(k == 0)
        def _():
            acc[...] = jnp.zeros_like(acc)

        acc[...] += lax.dot_general(a_ref[...].astype(BF16), b_ref[...].astype(BF16), dims,
                                    preferred_element_type=F32)

        @pl.when(k == nk - 1)
        def _():
            r = acc[...]
            if add_ref is not None:
                r = r + add_ref[...]
            o_ref[...] = r.astype(o_ref.dtype)

    a_spec = pl.BlockSpec((tk, tm), lambda i, j, k: (k, i)) if mode == "tn" else pl.BlockSpec((tm, tk), lambda i, j, k: (i, k))
    b_spec = pl.BlockSpec((tn, tk), lambda i, j, k: (j, k)) if mode == "nt" else pl.BlockSpec((tk, tn), lambda i, j, k: (k, j))
    o_spec = pl.BlockSpec((tm, tn), lambda i, j, k: (i, j))
    ins, specs = [a, b], [a_spec, b_spec]
    if add is not None:
        ins.append(add)
        specs.append(o_spec)
    return pl.pallas_call(
        body, grid=(M // tm, N // tn, nk), in_specs=specs, out_specs=o_spec,
        out_shape=jax.ShapeDtypeStruct((M, N), out_dtype), scratch_shapes=[pltpu.VMEM((tm, tn), F32)],
        compiler_params=_cparams(("parallel", "parallel", "arbitrary")), name=name)(*ins)


def _row_spec(w, tile):
    return pl.BlockSpec((tile, w), lambda i: (i, 0))


def _par_spec(shape):
    return pl.BlockSpec(shape, lambda i: (0,) * len(shape))


def tl_fwd(fn, name, rows, pars, out_widths, tile=ROW_TILE):
    S = rows[0].shape[0]
    nr = len(rows)

    def body(*refs):
        ins = [r[...] for r in refs[:nr + len(pars)]]
        outs = fn(*ins)
        for o_ref, o in zip(refs[nr + len(pars):], outs):
            o_ref[...] = o

    return pl.pallas_call(
        body, grid=(S // tile,),
        in_specs=[_row_spec(r.shape[1], tile) for r in rows] + [_par_spec(p.shape) for p in pars],
        out_specs=[_row_spec(w, tile) for w in out_widths],
        out_shape=[jax.ShapeDtypeStruct((S, w), F32) for w in out_widths],
        compiler_params=_cparams(("parallel",)), name=name)(*rows, *pars)


def tl_bwd(fn, name, rows, pars, cts, tile=ROW_TILE, row_grad=None):
    S = rows[0].shape[0]
    nr, npar = len(rows), len(pars)
    row_grad = [True] * nr if row_grad is None else row_grad
    flat_cts = [c for group in cts for c in group]
    ncts = len(flat_cts)
    gi = [i for i in range(nr) if row_grad[i]]

    def body(*refs):
        row_v = [r[...] for r in refs[:nr]]
        par_v = [r[...] for r in refs[nr:nr + npar]]
        ct_refs = refs[nr + npar:nr + npar + ncts]
        out_refs = refs[nr + npar + ncts:]
        ct_v, pos = [], 0
        for group in cts:
            acc = ct_refs[pos][...]
            for q in range(1, len(group)):
                acc = acc + ct_refs[pos + q][...]
            pos += len(group)
            ct_v.append(acc)

        def f(diff_rows, par_vals):
            full = list(row_v)
            for idx, val in zip(gi, diff_rows):
                full[idx] = val
            return tuple(fn(*full, *par_vals))

        _, vjp = jax.vjp(f, [row_v[i] for i in gi], par_v)
        d_rows, d_pars = vjp(tuple(ct_v))
        for o_ref, g in zip(out_refs[:len(gi)], d_rows):
            o_ref[...] = g
        first = pl.program_id(0) == 0
        for o_ref, g in zip(out_refs[len(gi):], d_pars):
            @pl.when(first)
            def _(o_ref=o_ref):
                o_ref[...] = jnp.zeros_like(o_ref)
            o_ref[...] += g

    outs = pl.pallas_call(
        body, grid=(S // tile,),
        in_specs=[_row_spec(r.shape[1], tile) for r in rows] + [_par_spec(p.shape) for p in pars]
        + [_row_spec(c.shape[1], tile) for c in flat_cts],
        out_specs=[_row_spec(rows[i].shape[1], tile) for i in gi] + [_par_spec(p.shape) for p in pars],
        out_shape=[jax.ShapeDtypeStruct(rows[i].shape, F32) for i in gi] + [jax.ShapeDtypeStruct(p.shape, F32) for p in pars],
        compiler_params=_cparams(("arbitrary",)), name=name)(*rows, *pars, *flat_cts)
    return list(outs[:len(gi)]), list(outs[len(gi):])


def _shift_rows(x, j):
    if j == 0:
        return x
    rolled = pltpu.roll(x, j, 0)
    row = lax.broadcasted_iota(jnp.int32, x.shape, 0)
    return jnp.where(row >= j, rolled, 0.0)


def _unshift_rows(x, j):
    if j == 0:
        return x
    S = x.shape[0]
    rolled = pltpu.roll(x, S - j, 0)
    row = lax.broadcasted_iota(jnp.int32, x.shape, 0)
    return jnp.where(row < S - j, rolled, 0.0)


def fir_fwd(x, taps, name):
    S, C = x.shape
    K = taps.shape[0]

    def body(x_ref, w_ref, y_ref):
        xv = x_ref[...]
        acc = jnp.zeros_like(xv)
        for k in range(K):
            acc = acc + _shift_rows(xv, K - 1 - k) * w_ref[pl.ds(k, 1), :]
        y_ref[...] = acc

    cs = pl.BlockSpec((S, 128), lambda j: (0, j))
    return pl.pallas_call(body, grid=(C // 128,), in_specs=[cs, pl.BlockSpec((K, 128), lambda j: (0, j))],
                          out_specs=cs, out_shape=jax.ShapeDtypeStruct((S, C), F32),
                          compiler_params=_cparams(("parallel",)), name=name)(x, taps)


def fir_bwd(x, taps, dy_list, name):
    S, C = x.shape
    K = taps.shape[0]
    n = len(dy_list)

    def body(*refs):
        x_ref, w_ref = refs[:2]
        dy = refs[2][...]
        for q in range(1, n):
            dy = dy + refs[2 + q][...]
        dx_ref, dw_ref, db_ref = refs[2 + n:]
        xv = x_ref[...]
        dx = jnp.zeros_like(xv)
        for k in range(K):
            j = K - 1 - k
            dx = dx + _unshift_rows(dy, j) * w_ref[pl.ds(k, 1), :]
            dw_ref[pl.ds(k, 1), :] = jnp.sum(dy * _shift_rows(xv, j), axis=0, keepdims=True)
        dx_ref[...] = dx
        db_ref[...] = jnp.sum(dy, axis=0, keepdims=True)

    cs = pl.BlockSpec((S, 128), lambda j: (0, j))
    ks = pl.BlockSpec((K, 128), lambda j: (0, j))
    bs = pl.BlockSpec((1, 128), lambda j: (0, j))
    return pl.pallas_call(body, grid=(C // 128,), in_specs=[cs, ks] + [cs] * n, out_specs=[cs, ks, bs],
                          out_shape=[jax.ShapeDtypeStruct((S, C), F32), jax.ShapeDtypeStruct((K, C), F32),
                                     jax.ShapeDtypeStruct((1, C), F32)],
                          compiler_params=_cparams(("parallel",)), name=name)(x, taps, *dy_list)


def _col(tile, lane, t):
    return jnp.sum(jnp.where(lane == t, tile, 0.0), axis=1, keepdims=True)


def _rwkv_step(s, rv, vcol):
    sa = jnp.sum(s * (-rv[3]), axis=1, keepdims=True)
    return s * rv[1] + sa * (rv[3] * rv[4]) + vcol * rv[2], sa


def scan_fwd(r, w, k, vT, kk, a, name):
    H, S, Dk = r.shape
    Dv = vT.shape[1]
    Tc = SCAN_CHUNK
    nc = S // Tc
    rows = [r, w, k, kk, a]

    def body(*refs):
        row_refs = refs[:5]
        vT_ref, yT_ref, sall_ref, s_ref = refs[5:]

        @pl.when(pl.program_id(0) == 0)
        def _():
            s_ref[...] = jnp.zeros_like(s_ref)

        yT_ref[...] = jnp.zeros_like(yT_ref)
        lane = lax.broadcasted_iota(jnp.int32, (Dv, Tc), 1)

        def step(t, states):
            new = []
            for h in range(H):
                s = states[h]
                sall_ref[t, h] = s
                rv = [ref[h, pl.ds(t, 1), :] for ref in row_refs]
                s, _ = _rwkv_step(s, rv, _col(vT_ref[h], lane, t))
                ycol = jnp.sum(s * rv[0], axis=1, keepdims=True)
                yT_ref[h] = jnp.where(lane == t, ycol, yT_ref[h])
                new.append(s)
            return tuple(new)

        states = lax.fori_loop(0, Tc, step, tuple(s_ref[h] for h in range(H)))
        for h in range(H):
            s_ref[h] = states[h]

    rs = pl.BlockSpec((H, Tc, Dk), lambda c: (0, c, 0))
    vs = pl.BlockSpec((H, Dv, Tc), lambda c: (0, 0, c))
    yT, sall = pl.pallas_call(
        body, grid=(nc,), in_specs=[rs] * 5 + [vs],
        out_specs=[vs, pl.BlockSpec((Tc, H, Dv, Dk), lambda c: (c, 0, 0, 0))],
        out_shape=[jax.ShapeDtypeStruct((H, Dv, S), F32), jax.ShapeDtypeStruct((S, H, Dv, Dk), F32)],
        scratch_shapes=[pltpu.VMEM((H, Dv, Dk), F32)],
        compiler_params=_cparams(("arbitrary",)), name=name)(*rows, vT)
    return yT, sall


def scan_bwd(r, w, k, vT, kk, a, sall, dyT, name):
    H, S, Dk = r.shape
    Dv = vT.shape[1]
    Tc = SCAN_CHUNK
    nc = S // Tc
    rows = [r, w, k, kk, a]

    def body(*refs):
        row_refs = refs[:5]
        vT_ref, dyT_ref, sall_ref = refs[5:8]
        drow_refs = refs[8:13]
        dvT_ref, ds_ref = refs[13:]

        @pl.when(pl.program_id(0) == 0)
        def _():
            ds_ref[...] = jnp.zeros_like(ds_ref)

        dvT_ref[...] = jnp.zeros_like(dvT_ref)
        lane = lax.broadcasted_iota(jnp.int32, (Dv, Tc), 1)

        def bstep(i, carry):
            t = Tc - 1 - i
            new = []
            for h in range(H):
                ds = carry[h]
                sp = sall_ref[t, h]
                rv = [ref[h, pl.ds(t, 1), :] for ref in row_refs]
                vcol = _col(vT_ref[h], lane, t)
                dycol = _col(dyT_ref[h], lane, t)
                st, sa = _rwkv_step(sp, rv, vcol)
                drow_refs[0][h, pl.ds(t, 1), :] = jnp.sum(st * dycol, axis=0, keepdims=True)
                g = ds + dycol * rv[0]
                drow_refs[1][h, pl.ds(t, 1), :] = jnp.sum(g * sp, axis=0, keepdims=True)
                drow_refs[2][h, pl.ds(t, 1), :] = jnp.sum(g * vcol, axis=0, keepdims=True)
                dvcol = jnp.sum(g * rv[2], axis=1, keepdims=True)
                dsa = jnp.sum(g * (rv[3] * rv[4]), axis=1, keepdims=True)
                db = jnp.sum(g * sa, axis=0, keepdims=True)
                dnkk = jnp.sum(sp * dsa, axis=0, keepdims=True)
                drow_refs[3][h, pl.ds(t, 1), :] = db * rv[4] - dnkk
                drow_refs[4][h, pl.ds(t, 1), :] = db * rv[3]
                dvT_ref[h] = jnp.where(lane == t, dvcol, dvT_ref[h])
                new.append(g * rv[1] - dsa * rv[3])
            return tuple(new)

        carry = lax.fori_loop(0, Tc, bstep, tuple(ds_ref[h] for h in range(H)))
        for h in range(H):
            ds_ref[h] = carry[h]

    rs = pl.BlockSpec((H, Tc, Dk), lambda c: (0, nc - 1 - c, 0))
    vs = pl.BlockSpec((H, Dv, Tc), lambda c: (0, 0, nc - 1 - c))
    outs = pl.pallas_call(
        body, grid=(nc,),
        in_specs=[rs] * 5 + [vs, vs, pl.BlockSpec((Tc, H, Dv, Dk), lambda c: (nc - 1 - c, 0, 0, 0))],
        out_specs=[rs] * 5 + [vs],
        out_shape=[jax.ShapeDtypeStruct((H, S, Dk), F32)] * 5 + [jax.ShapeDtypeStruct((H, Dv, S), F32)],
        scratch_shapes=[pltpu.VMEM((H, Dv, Dk), F32)],
        compiler_params=_cparams(("arbitrary",)), name=name)(*rows, vT, dyT, sall)
    return list(outs[:5]), outs[5]


CHUNK = 128


def chunk_fwd(fn, name, blocks, state_shape, out_width):
    H, S, _ = blocks[0].shape
    nc = S // CHUNK
    nb = len(blocks)

    def body(*refs):
        o_ref, sv_ref, st = refs[nb:]

        @pl.when(pl.program_id(1) == 0)
        def _():
            st[...] = jnp.zeros_like(st)

        s0 = st[...]
        sv_ref[0, 0] = s0
        s1, out = fn(s0, *[r[0] for r in refs[:nb]])
        st[...] = s1
        o_ref[0] = out

    spec = lambda w: pl.BlockSpec((1, CHUNK, w), lambda h, c: (h, c, 0))
    return pl.pallas_call(
        body, grid=(H, nc), in_specs=[spec(b.shape[2]) for b in blocks],
        out_specs=[spec(out_width), pl.BlockSpec((1, 1) + state_shape, lambda h, c: (h, c, 0, 0))],
        out_shape=[jax.ShapeDtypeStruct((H, S, out_width), F32), jax.ShapeDtypeStruct((H, nc) + state_shape, F32)],
        scratch_shapes=[pltpu.VMEM(state_shape, F32)],
        compiler_params=_cparams(("parallel", "arbitrary")), name=name)(*blocks)


def chunk_bwd(fn, name, blocks, states, dout):
    H, S, _ = blocks[0].shape
    nc = S // CHUNK
    nb = len(blocks)
    state_shape = states.shape[2:]

    def body(*refs):
        sv_ref, do_ref = refs[nb], refs[nb + 1]
        d_refs = refs[nb + 2:2 * nb + 2]
        dst = refs[2 * nb + 2]

        @pl.when(pl.program_id(1) == 0)
        def _():
            dst[...] = jnp.zeros_like(dst)

        _, vjp = jax.vjp(fn, sv_ref[0, 0], *[r[0] for r in refs[:nb]])
        grads = vjp((dst[...], do_ref[0]))
        dst[...] = grads[0]
        for d_ref, g in zip(d_refs, grads[1:]):
            d_ref[0] = g

    spec = lambda w: pl.BlockSpec((1, CHUNK, w), lambda h, c: (h, nc - 1 - c, 0))
    return pl.pallas_call(
        body, grid=(H, nc),
        in_specs=[spec(b.shape[2]) for b in blocks]
        + [pl.BlockSpec((1, 1) + state_shape, lambda h, c: (h, nc - 1 - c, 0, 0)), spec(dout.shape[2])],
        out_specs=[spec(b.shape[2]) for b in blocks],
        out_shape=[jax.ShapeDtypeStruct(b.shape, F32) for b in blocks],
        scratch_shapes=[pltpu.VMEM(state_shape, F32)],
        compiler_params=_cparams(("parallel", "arbitrary")), name=name)(*blocks, states, dout)


def _bdot(a, b, dims):
    return lax.dot_general(a.astype(BF16), b.astype(BF16), (dims, ((), ())), preferred_element_type=F32)


def ssd_chunk(state, cb, bb, da, xdt):
    T = cb.shape[0]
    ti = lax.broadcasted_iota(jnp.int32, (T, T), 0)
    si = lax.broadcasted_iota(jnp.int32, (T, T), 1)
    mask = ti >= si
    cs = jnp.dot(mask.astype(F32), da, precision=HI, preferred_element_type=F32)
    pick = (lax.broadcasted_iota(jnp.int32, cs.shape, 1) == 0).astype(F32)
    cs_row = lax.dot_general(pick, cs, (((1,), (1,)), ((), ())), precision=HI, preferred_element_type=F32)
    lmat = jnp.where(mask, jnp.exp(jnp.where(mask, cs - cs_row, 0.0)), 0.0)
    scores = _bdot(cb, bb, ((1,), (1,))) * lmat
    y = _bdot(scores, xdt, ((1,), (0,))) + _bdot(cb, state, ((1,), (1,))) * jnp.exp(cs[:, :HEAD_DIM])
    last = cs[T - 1:T, :]
    new_state = state * jnp.exp(last) + _bdot(xdt, bb * jnp.exp(last - cs), ((0,), (0,)))
    return new_state, y


HGRN_SUB = 16


def hgrn_chunk(state, q, k, lf, v):
    T, C = q.shape[0], HGRN_SUB
    ti = lax.broadcasted_iota(jnp.int32, (C, C), 0)
    si = lax.broadcasted_iota(jnp.int32, (C, C), 1)
    tril = (ti >= si).astype(F32)
    row = lax.broadcasted_iota(jnp.int32, (C, q.shape[1]), 0)
    outs = []
    for j in range(T // C):
        qj, kj, lj, vj = (a[j * C:(j + 1) * C] for a in (q, k, lf, v))
        b = jnp.dot(tril, lj, precision=HI, preferred_element_type=F32)
        o = _bdot(qj * jnp.exp(b), state, ((1,), (1,)))
        for s in range(C):
            m = row >= s
            e = jnp.where(m, jnp.exp(jnp.where(m, b - b[s:s + 1], 0.0)), 0.0)
            o = o + jnp.sum(qj * kj[s:s + 1] * e, axis=1, keepdims=True) * vj[s:s + 1]
        last = b[C - 1:C]
        state = state * jnp.exp(last) + _bdot(vj, kj * jnp.exp(last - b), ((0,), (0,)))
        outs.append(o)
    return state, jnp.concatenate(outs, axis=0)


def _to_heads(x, dk):
    S = x.shape[0]
    return jnp.transpose(x.reshape(S, N_HEADS, dk), (1, 0, 2))


def _from_heads(x):
    H, S, dk = x.shape
    return jnp.transpose(x, (1, 0, 2)).reshape(S, H * dk)


def _to_headsT(x):
    S = x.shape[0]
    return jnp.transpose(x.reshape(S, N_HEADS, HEAD_DIM), (1, 2, 0))


def _from_headsT(x):
    H, dv, S = x.shape
    return jnp.transpose(x, (2, 0, 1)).reshape(S, H * dv)


def run_scan_fwd(R, W, K, V, KK, A, dk, name):
    heads = [_to_heads(t, dk) for t in (R, W, K, KK, A)] + [_to_headsT(V)]
    yT, sall = scan_fwd(heads[0], heads[1], heads[2], heads[5], heads[3], heads[4], name)
    return _from_headsT(yT), (heads, sall)


def run_scan_bwd(saved, dY, name):
    heads, sall = saved
    drows, dvT = scan_bwd(heads[0], heads[1], heads[2], heads[5], heads[3], heads[4], sall, _to_headsT(dY), name)
    return [_from_heads(d) for d in drows], _from_headsT(dvT)


def _attn_block(q, kp, kc, vp, vc, n, slope, dilation):
    blk = ATTN_BLK
    k2 = jnp.concatenate([kp, kc], axis=0)
    v2 = jnp.concatenate([vp, vc], axis=0)
    s = lax.dot_general(q, k2, (((1,), (1,)), ((), ())), precision=HI, preferred_element_type=F32) * (HEAD_DIM ** -0.5)
    i = lax.broadcasted_iota(jnp.int32, (blk, 2 * blk), 0)
    j = lax.broadcasted_iota(jnp.int32, (blk, 2 * blk), 1)
    dist = blk + i - j
    first_key = jnp.where(n > 0, 0, blk)
    valid = (dist >= 0) & (dist <= blk) & (j >= first_key)
    s = s - slope * (dist * dilation).astype(F32)
    s = jnp.where(valid, s, -1e30)
    m = jnp.max(s, axis=-1, keepdims=True)
    p = jnp.exp(s - m)
    l = jnp.sum(p, axis=-1, keepdims=True)
    o = jnp.dot(p, v2, precision=HI, preferred_element_type=F32) / l
    lse = jnp.broadcast_to(m + jnp.log(l), o.shape)
    return o, lse


def _slope_of(h):
    s = jnp.float32(ALIBI_SLOPES[N_HEADS - 1])
    for hh in range(N_HEADS - 2, -1, -1):
        s = jnp.where(h == hh, jnp.float32(ALIBI_SLOPES[hh]), s)
    return s


def _attn_specs(nb):
    blk = ATTN_BLK
    cur = pl.BlockSpec((1, 1, blk, HEAD_DIM), lambda h, z, n: (h, z, n, 0))
    prev = pl.BlockSpec((1, 1, blk, HEAD_DIM), lambda h, z, n: (h, z, jnp.maximum(n - 1, 0), 0))
    return cur, prev


def attn_fwd(q, k, v, dilation, name):
    H, Z, L, _ = q.shape
    nb = L // ATTN_BLK
    cur, prev = _attn_specs(nb)

    def body(q_ref, kp_ref, kc_ref, vp_ref, vc_ref, o_ref, l_ref):
        o, lse = _attn_block(q_ref[0, 0], kp_ref[0, 0], kc_ref[0, 0], vp_ref[0, 0], vc_ref[0, 0],
                             pl.program_id(2), _slope_of(pl.program_id(0)), dilation)
        o_ref[0, 0] = o
        l_ref[0, 0] = lse

    return pl.pallas_call(body, grid=(H, Z, nb), in_specs=[cur, prev, cur, prev, cur], out_specs=[cur, cur],
                          out_shape=[jax.ShapeDtypeStruct(q.shape, F32)] * 2,
                          compiler_params=_cparams(("parallel", "parallel", "arbitrary")), name=name)(q, k, k, v, v)


def attn_bwd(q, k, v, do, dlse, dilation, name):
    H, Z, L, _ = q.shape
    blk = ATTN_BLK
    nb = L // blk
    cur, prev = _attn_specs(nb)
    full = pl.BlockSpec((1, 1, L, HEAD_DIM), lambda h, z, n: (h, z, 0, 0))

    def body(q_ref, kp_ref, kc_ref, vp_ref, vc_ref, do_ref, dl_ref, dq_ref, dk_ref, dv_ref):
        n = pl.program_id(2)
        slope = _slope_of(pl.program_id(0))
        f = lambda q_, kp_, kc_, vp_, vc_: _attn_block(q_, kp_, kc_, vp_, vc_, n, slope, dilation)
        _, vjp = jax.vjp(f, q_ref[0, 0], kp_ref[0, 0], kc_ref[0, 0], vp_ref[0, 0], vc_ref[0, 0])
        dq, dkp, dkc, dvp, dvc = vjp((do_ref[0, 0], dl_ref[0, 0]))
        dq_ref[0, 0] = dq

        @pl.when(n == 0)
        def _():
            dk_ref[...] = jnp.zeros_like(dk_ref)
            dv_ref[...] = jnp.zeros_like(dv_ref)

        here = pl.ds(pl.multiple_of(n * blk, blk), blk)
        dk_ref[0, 0, here, :] += dkc
        dv_ref[0, 0, here, :] += dvc

        @pl.when(n > 0)
        def _():
            before = pl.ds(pl.multiple_of((n - 1) * blk, blk), blk)
            dk_ref[0, 0, before, :] += dkp
            dv_ref[0, 0, before, :] += dvp

    return pl.pallas_call(body, grid=(H, Z, nb), in_specs=[cur, prev, cur, prev, cur, cur, cur],
                          out_specs=[cur, full, full], out_shape=[jax.ShapeDtypeStruct(q.shape, F32)] * 3,
                          compiler_params=_cparams(("parallel", "parallel", "arbitrary")), name=name)(q, k, k, v, v, do, dlse)


def _to_sub(t, d):
    S = t.shape[0]
    return jnp.transpose(t.reshape(S // d, d, N_HEADS, HEAD_DIM), (2, 1, 0, 3))


def _from_sub(t):
    H, d, L, _ = t.shape
    return jnp.transpose(t, (2, 1, 0, 3)).reshape(L * d, H * HEAD_DIM)


def _head_ones(width, group):
    i = lax.broadcasted_iota(jnp.int32, (width, width), 0) // group
    j = lax.broadcasted_iota(jnp.int32, (width, width), 1) // group
    return (i == j).astype(F32)


def _group_sum(x, group):
    return jnp.dot(x, _head_ones(x.shape[1], group), precision=HI, preferred_element_type=F32)


def _spread(width_in, width_out, rep):
    i = lax.broadcasted_iota(jnp.int32, (width_in, width_out), 0)
    j = lax.broadcasted_iota(jnp.int32, (width_in, width_out), 1) // rep
    return (i == j).astype(F32)


def _hdot(a, b):
    return jnp.dot(a, b, precision=HI, preferred_element_type=F32)


def _sigmoid(x):
    return 1.0 / (1.0 + jnp.exp(-x))


def _softplus(x):
    return jnp.maximum(x, 0.0) + jnp.log(1.0 + jnp.exp(jnp.minimum(x, -x)))


def _silu(x):
    return x * _sigmoid(x)


def rwkv_pre(layer):
    def fn(*args):
        if layer == 0:
            fs, w0, w2p, a0, a2p, g2p, k_k, k_a = args
        else:
            fs, vfirst, w0, w2p, a0, a2p, g2p, k_k, k_a, v0, v2p = args
        r, k, v = fs[:, 0:256], fs[:, 256:512], fs[:, 512:768]
        lora = fs[:, 768:896]
        w_log = -_softplus(-(w0 + _hdot(jnp.tanh(lora), w2p))) - 0.5
        decay = jnp.exp(-jnp.exp(w_log))
        a = _sigmoid(a0 + _hdot(lora, a2p))
        g = _hdot(_sigmoid(lora), g2p)
        if layer > 0:
            v = v + (vfirst - v) * _sigmoid(v0 + _hdot(fs[:, 896:1024], v2p))
        kk = k * k_k
        kk = kk / jnp.maximum(jnp.sqrt(_group_sum(kk * kk, HEAD_DIM)), 1e-12)
        k = k * (1.0 + (a - 1.0) * k_a)
        return r, decay, k, v, kk, a, g
    return fn


def rwkv_post(y, r, k, v, g, lnx_w, lnx_b, r_k):
    mu = _group_sum(y, HEAD_DIM) * (1.0 / HEAD_DIM)
    yc = y - mu
    var = _group_sum(yc * yc, HEAD_DIM) * (1.0 / HEAD_DIM)
    yn = yc * lax.rsqrt(var + RWKV_GN_EPS) * lnx_w + lnx_b
    bonus = _group_sum(r * k * r_k, HEAD_DIM) * v
    return ((yn + bonus) * g,)


def attn_combine(o1, o2, o3, l1, l2, l3):
    m = jnp.maximum(jnp.maximum(l1, l2), l3)
    e1, e2, e3 = jnp.exp(l1 - m), jnp.exp(l2 - m), jnp.exp(l3 - m)
    return ((o1 * e1 + o2 * e2 + o3 * e3) / (e1 + e2 + e3),)


def ssd_pre(xc, dtr, conv_b, dt_bias, a_log):
    xbc = _silu(xc + conv_b)
    xs, bm, cm = xbc[:, 0:256], xbc[:, 256:512], xbc[:, 512:768]
    dt = _softplus(dtr + dt_bias)
    a_neg = -jnp.exp(a_log)
    wide = _spread(128, N_HEADS * SSD_STATE, SSD_STATE)
    w = _hdot(dt, wide) * _hdot(a_neg, wide)
    xdt = xs * _hdot(dt, _spread(128, D_GROUP, HEAD_DIM))
    rr = jnp.concatenate([cm[:, 0:128], cm[:, 0:128], cm[:, 128:256], cm[:, 128:256]], axis=1)
    kk = jnp.concatenate([bm[:, 0:128], bm[:, 0:128], bm[:, 128:256], bm[:, 128:256]], axis=1)
    return rr, w, kk, xdt, xs


def ssd_post(ys, z, xs, d_skip, norm_w):
    y = ys + xs * _hdot(d_skip, _spread(128, D_GROUP, HEAD_DIM))
    y = y * _silu(z)
    half = D_GROUP // 2
    parts = []
    for g in range(2):
        t = y[:, g * half:(g + 1) * half]
        parts.append(t * lax.rsqrt(jnp.mean(t * t, axis=-1, keepdims=True) + RMS_EPS))
    return (jnp.concatenate(parts, axis=1) * norm_w,)


def hgrn_pre(seg, lb):
    q, f, i = seg[:, 0:256], seg[:, 256:512], seg[:, 512:768]
    forget = lb + (1.0 - lb) * _sigmoid(f)
    return _silu(q), 1.0 - forget, jnp.log(forget), i


def hgrn_post(o, seg, norm_w):
    g = seg[:, 768:1024]
    ms = _group_sum(o * o, HEAD_DIM) * (1.0 / HEAD_DIM)
    return (o * lax.rsqrt(ms + RMS_EPS) * norm_w * _silu(g),)


def ln_res(x, y, w, b):
    z = ALPHA * x + y
    mu = jnp.mean(z, axis=-1, keepdims=True)
    zc = z - mu
    var = jnp.mean(zc * zc, axis=-1, keepdims=True)
    return (zc * lax.rsqrt(var + LN_EPS) * w + b,)


def relu2(u):
    r = jnp.maximum(u, 0.0)
    return (r * r,)


def loss_and_grad(y, tgt, name):
    S, D = y.shape
    tile = ROW_TILE

    def body(y_ref, t_ref, l_ref, dy_ref):
        e = y_ref[...] - t_ref[...]
        dy_ref[...] = e * (1.0 / D)

        @pl.when(pl.program_id(0) == 0)
        def _():
            l_ref[...] = jnp.zeros_like(l_ref)

        per_row = 0.5 * jnp.mean(e * e, axis=-1, keepdims=True)
        l_ref[...] += jnp.sum(per_row, axis=0, keepdims=True) * jnp.ones((1, 128), F32)

    return pl.pallas_call(body, grid=(S // tile,), in_specs=[_row_spec(D, tile)] * 2,
                          out_specs=[_par_spec((1, 128)), _row_spec(D, tile)],
                          out_shape=[jax.ShapeDtypeStruct((1, 128), F32), jax.ShapeDtypeStruct((S, D), F32)],
                          compiler_params=_cparams(("arbitrary",)), name=name)(y, tgt)


def add_rows(arrs, name):
    (out,) = tl_fwd(lambda *a: (functools.reduce(lambda p, q: p + q, a),), name, arrs, [], [arrs[0].shape[1]])
    return out


def small_fwd(fn, name, ins, out_shapes):
    n = len(ins)

    def body(*refs):
        outs = fn(*[r[...] for r in refs[:n]])
        for o_ref, o in zip(refs[n:], outs):
            o_ref[...] = o

    return pl.pallas_call(body, out_shape=[jax.ShapeDtypeStruct(s, F32) for s in out_shapes], name=name)(*ins)


def small_bwd(fn, name, ins, cts):
    n, m = len(ins), len(cts)

    def body(*refs):
        _, vjp = jax.vjp(lambda *a: tuple(fn(*a)), *[r[...] for r in refs[:n]])
        grads = vjp(tuple(r[...] for r in refs[n:n + m]))
        for o_ref, g in zip(refs[n + m:], grads):
            o_ref[...] = g

    return pl.pallas_call(body, out_shape=[jax.ShapeDtypeStruct(a.shape, F32) for a in ins], name=name)(*ins, *cts)


def param_prep(lower_bounds, mu0, mu1):
    e = jnp.exp(lower_bounds - jnp.max(lower_bounds, axis=0, keepdims=True))
    sm = e / jnp.sum(e, axis=0, keepdims=True)
    lb0 = sm[0:1] - sm[0:1]
    lb1 = sm[0:1] + sm[1:2] - sm[0:1]
    return lb0, lb1, mu0, 1.0 - mu0, mu1, 1.0 - mu1


def _rows_tile(rows):
    return _pick(rows, (256, 128, 64, 32, 16, 8))


def sum_parts(parts, name):
    P, rows, cols = parts.shape
    tile = _rows_tile(rows)

    def body(p_ref, o_ref):
        acc = p_ref[0]
        for p in range(1, P):
            acc = acc + p_ref[p]
        o_ref[...] = acc

    return pl.pallas_call(body, grid=(rows // tile,), in_specs=[pl.BlockSpec((P, tile, cols), lambda i: (0, i, 0))],
                          out_specs=pl.BlockSpec((tile, cols), lambda i: (i, 0)),
                          out_shape=jax.ShapeDtypeStruct((rows, cols), F32),
                          compiler_params=_cparams(("parallel",)), name=name)(parts)


def adamw(w, g, m, v, name):
    rows, cols = w.shape
    tile = _rows_tile(rows)

    def body(w_ref, g_ref, m_ref, v_ref, d_ref, nm_ref, nv_ref):
        gv = g_ref[...]
        nm = ADAM_B1 * m_ref[...] + (1.0 - ADAM_B1) * gv
        nv = ADAM_B2 * v_ref[...] + (1.0 - ADAM_B2) * jnp.square(gv)
        m_hat = nm / (1.0 - ADAM_B1 ** ADAM_STEP)
        v_hat = nv / (1.0 - ADAM_B2 ** ADAM_STEP)
        d_ref[...] = -ADAM_LR * (m_hat / (jnp.sqrt(v_hat) + ADAM_EPS) + ADAM_WD * w_ref[...])
        nm_ref[...] = nm
        nv_ref[...] = nv

    spec = pl.BlockSpec((tile, cols), lambda i: (i, 0))
    return pl.pallas_call(body, grid=(rows // tile,), in_specs=[spec] * 4, out_specs=[spec] * 3,
                          out_shape=[jax.ShapeDtypeStruct((rows, cols), F32)] * 3,
                          compiler_params=_cparams(("parallel",)), name=name)(w, g, m, v)


MESH = pl.DeviceIdType.MESH
ANY = pl.BlockSpec(memory_space=pl.ANY)


def _flip(v, bit):
    return 1 - v if bit else v


def gather_chips(arrs, name):
    n = len(arrs)

    def body(*refs):
        ins, outs = refs[:n], refs[n:2 * n]
        send, recv, loc = refs[2 * n:]
        x, y, c = lax.axis_index("x"), lax.axis_index("y"), lax.axis_index("c")
        me = 2 * x + y
        pending = []
        for i in range(n):
            lc = pltpu.make_async_copy(ins[i], outs[i].at[me], loc.at[i])
            lc.start()
            pending.append(lc)
            for r, (bx, by) in enumerate(((1, 0), (0, 1), (1, 1))):
                cp = pltpu.make_async_remote_copy(src_ref=ins[i], dst_ref=outs[i].at[me], send_sem=send.at[i, r],
                                                  recv_sem=recv.at[i, r], device_id=(_flip(x, bx), _flip(y, by), c),
                                                  device_id_type=MESH)
                cp.start()
                pending.append(cp)
        for cp in pending:
            cp.wait()

    return pl.pallas_call(
        body, in_specs=[ANY] * n, out_specs=[ANY] * n,
        out_shape=[jax.ShapeDtypeStruct((4,) + a.shape, a.dtype) for a in arrs],
        scratch_shapes=[pltpu.SemaphoreType.DMA((n, 3)), pltpu.SemaphoreType.DMA((n, 3)), pltpu.SemaphoreType.DMA((n,))],
        name=name)(*arrs)


_RELATIONS = tuple((r >> 2 & 1, r >> 1 & 1, r & 1) for r in range(1, 8))


def gather_devices(arr, name):
    def body(in_ref, out_ref, send, recv, loc):
        x, y, c = lax.axis_index("x"), lax.axis_index("y"), lax.axis_index("c")
        me = 4 * x + 2 * y + c
        lc = pltpu.make_async_copy(in_ref, out_ref.at[me], loc)
        lc.start()
        pending = [lc]
        for r, (bx, by, bc) in enumerate(_RELATIONS):
            cp = pltpu.make_async_remote_copy(src_ref=in_ref, dst_ref=out_ref.at[me], send_sem=send.at[r],
                                              recv_sem=recv.at[r], device_id=(_flip(x, bx), _flip(y, by), _flip(c, bc)),
                                              device_id_type=MESH)
            cp.start()
            pending.append(cp)
        for cp in pending:
            cp.wait()

    return pl.pallas_call(
        body, in_specs=[ANY], out_specs=ANY, out_shape=jax.ShapeDtypeStruct((8,) + arr.shape, arr.dtype),
        scratch_shapes=[pltpu.SemaphoreType.DMA((7,)), pltpu.SemaphoreType.DMA((7,)), pltpu.SemaphoreType.DMA(())],
        name=name)(arr)


def scatter_partials(arrs, name):
    n = len(arrs)

    def body(*refs):
        ins, outs = refs[:n], refs[n:2 * n]
        send, recv, loc = refs[2 * n:]
        x, y, c = lax.axis_index("x"), lax.axis_index("y"), lax.axis_index("c")
        me = 4 * x + 2 * y + c
        pending = []
        for i in range(n):
            lc = pltpu.make_async_copy(ins[i].at[c, 2 * x + y], outs[i].at[me], loc.at[i])
            lc.start()
            pending.append(lc)
            for r, (bx, by, bc) in enumerate(_RELATIONS):
                px, py, pc = _flip(x, bx), _flip(y, by), _flip(c, bc)
                cp = pltpu.make_async_remote_copy(src_ref=ins[i].at[pc, 2 * px + py], dst_ref=outs[i].at[me],
                                                  send_sem=send.at[i, r], recv_sem=recv.at[i, r],
                                                  device_id=(px, py, pc), device_id_type=MESH)
                cp.start()
                pending.append(cp)
        for cp in pending:
            cp.wait()

    return pl.pallas_call(
        body, in_specs=[ANY] * n, out_specs=[ANY] * n,
        out_shape=[jax.ShapeDtypeStruct((8,) + a.shape[2:], a.dtype) for a in arrs],
        scratch_shapes=[pltpu.SemaphoreType.DMA((n, 7)), pltpu.SemaphoreType.DMA((n, 7)), pltpu.SemaphoreType.DMA((n,))],
        name=name)(*arrs)


def sibling_exchange(arrs, name):
    n = len(arrs)

    def body(*refs):
        ins, outs = refs[:n], refs[n:2 * n]
        send, recv, loc = refs[2 * n:]
        x, y, c = lax.axis_index("x"), lax.axis_index("y"), lax.axis_index("c")
        pending = []
        for i in range(n):
            lc = pltpu.make_async_copy(ins[i], outs[i].at[c], loc.at[i])
            lc.start()
            cp = pltpu.make_async_remote_copy(src_ref=ins[i], dst_ref=outs[i].at[c], send_sem=send.at[i],
                                              recv_sem=recv.at[i], device_id=(x, y, 1 - c), device_id_type=MESH)
            cp.start()
            pending += [lc, cp]
        for cp in pending:
            cp.wait()

    return pl.pallas_call(
        body, in_specs=[ANY] * n, out_specs=[ANY] * n,
        out_shape=[jax.ShapeDtypeStruct((2,) + a.shape, a.dtype) for a in arrs],
        scratch_shapes=[pltpu.SemaphoreType.DMA((n,)), pltpu.SemaphoreType.DMA((n,)), pltpu.SemaphoreType.DMA((n,))],
        name=name)(*arrs)


def rwkv_fwd(l, seg, taps, pars, vfirst):
    fs = fir_fwd(seg, taps, f"rwkv_shift_fwd{l}")
    rows = [fs] + ([vfirst] if l else [])
    R, W, K, V, KK, A, G = tl_fwd(rwkv_pre(l), f"rwkv_pre_fwd{l}", rows, pars["pre"], [D_GROUP] * 7)
    Y, sb = run_scan_fwd(R, W, K, V, KK, A, HEAD_DIM, f"rwkv_scan_fwd{l}")
    (out,) = tl_fwd(rwkv_post, f"rwkv_post_fwd{l}", [Y, R, K, V, G], pars["post"], [D_GROUP])
    return out, V, (seg, taps, rows, R, W, K, V, KK, A, G, Y, sb)


def rwkv_bwd(l, saved, pars, dout, dv_extra):
    seg, taps, rows, R, W, K, V, KK, A, G, Y, sb = saved
    (dY, dR1, dK1, dV1, dG), dpost = tl_bwd(rwkv_post, f"rwkv_post_bwd{l}", [Y, R, K, V, G], pars["post"], [[dout]])
    (dR2, dW, dK2, dKK, dA), dV2 = run_scan_bwd(sb, dY, f"rwkv_scan_bwd{l}")
    cts = [[dR1, dR2], [dW], [dK1, dK2], [dV1, dV2] + dv_extra, [dKK], [dA], [dG]]
    drows, dpre = tl_bwd(rwkv_pre(l), f"rwkv_pre_bwd{l}", rows, pars["pre"], cts)
    dseg, dtaps, _ = fir_bwd(seg, taps, [drows[0]], f"rwkv_shift_bwd{l}")
    return dseg, (drows[1] if l else None), dtaps, dpre, dpost


def attn_mix_fwd(l, q, k, v):
    subs, os_, ls_ = [], [], []
    for b, (_, d) in enumerate(DILATED_BRANCHES):
        qs, ks, vs = _to_sub(q, d), _to_sub(k, d), _to_sub(v, d)
        o, lse = attn_fwd(qs, ks, vs, d, f"attn_fwd{l}_{b}")
        subs.append((qs, ks, vs))
        os_.append(_from_sub(o))
        ls_.append(_from_sub(lse))
    (out,) = tl_fwd(attn_combine, f"attn_combine_fwd{l}", os_ + ls_, [], [D_GROUP])
    return out, (subs, os_, ls_)


def attn_mix_bwd(l, saved, dout):
    subs, os_, ls_ = saved
    drows, _ = tl_bwd(attn_combine, f"attn_combine_bwd{l}", os_ + ls_, [], [[dout]])
    dqs, dks, dvs = [], [], []
    for b, (_, d) in enumerate(DILATED_BRANCHES):
        qs, ks, vs = subs[b]
        dq, dk, dv = attn_bwd(qs, ks, vs, _to_sub(drows[b], d), _to_sub(drows[3 + b], d), d, f"attn_bwd{l}_{b}")
        dqs.append(_from_sub(dq))
        dks.append(_from_sub(dk))
        dvs.append(_from_sub(dv))
    return add_rows(dqs, f"attn_dq{l}"), add_rows(dks, f"attn_dk{l}"), add_rows(dvs, f"attn_dv{l}")


def ssd_fwd(l, z, xbc, dtr, pars):
    xc = fir_fwd(xbc, pars["taps"], f"ssd_conv_fwd{l}")
    rr, w, kk, xdt, xs = tl_fwd(ssd_pre, f"ssd_pre_fwd{l}", [xc, dtr], pars["pre"], [512, 512, 512, D_GROUP, D_GROUP])
    blocks = [_to_heads(rr, SSD_STATE), _to_heads(kk, SSD_STATE), _to_heads(w, SSD_STATE), _to_heads(xdt, HEAD_DIM)]
    yh, states = chunk_fwd(ssd_chunk, f"ssd_scan_fwd{l}", blocks, (HEAD_DIM, SSD_STATE), HEAD_DIM)
    ys = _from_heads(yh)
    (out,) = tl_fwd(ssd_post, f"ssd_post_fwd{l}", [ys, z, xs], pars["post"], [D_GROUP])
    return out, (z, xbc, dtr, xc, blocks, states, xs, ys)


def ssd_bwd(l, saved, pars, dout):
    z, xbc, dtr, xc, blocks, states, xs, ys = saved
    (dys, dz, dxs), dpost = tl_bwd(ssd_post, f"ssd_post_bwd{l}", [ys, z, xs], pars["post"], [[dout]])
    drr, dkk, dw, dxdt = (_from_heads(g) for g in
                          chunk_bwd(ssd_chunk, f"ssd_scan_bwd{l}", blocks, states, _to_heads(dys, HEAD_DIM)))
    (dxc, ddtr), dpre = tl_bwd(ssd_pre, f"ssd_pre_bwd{l}", [xc, dtr], pars["pre"], [[drr], [dw], [dkk], [dxdt], [dxs]])
    dxbc, dtaps, _ = fir_bwd(xbc, pars["taps"], [dxc], f"ssd_conv_bwd{l}")
    return dz, dxbc, ddtr, dtaps, dpre, dpost


def hgrn_fwd(l, seg, pars):
    q, kk, lf, i = tl_fwd(hgrn_pre, f"hgrn_pre_fwd{l}", [seg], pars["pre"], [D_GROUP] * 4)
    blocks = [_to_heads(t, HEAD_DIM) for t in (q, kk, lf, i)]
    oh, states = chunk_fwd(hgrn_chunk, f"hgrn_scan_fwd{l}", blocks, (HEAD_DIM, HEAD_DIM), HEAD_DIM)
    o = _from_heads(oh)
    (out,) = tl_fwd(hgrn_post, f"hgrn_post_fwd{l}", [o, seg], pars["post"], [D_GROUP])
    return out, (seg, blocks, states, o)


def hgrn_bwd(l, saved, pars, dout):
    seg, blocks, states, o = saved
    (do, dseg1), dpost = tl_bwd(hgrn_post, f"hgrn_post_bwd{l}", [o, seg], pars["post"], [[dout]])
    dq, dkk, dlf, di = (_from_heads(g) for g in
                        chunk_bwd(hgrn_chunk, f"hgrn_scan_bwd{l}", blocks, states, _to_heads(do, HEAD_DIM)))
    (dseg2,), dpre = tl_bwd(hgrn_pre, f"hgrn_pre_bwd{l}", [seg], pars["pre"], [[dq], [dkk], [dlf], [di]])
    return add_rows([dseg1, dseg2], f"hgrn_dseg{l}"), dpre, dpost


def layer_fwd(l, x, wts, pars, vfirst):
    proj = matmul(x, wts["in"], "nn", f"proj_fwd{l}")
    seg_h = proj[:, SEG_HGRN:SEG_HGRN + 1024]
    seg_r = proj[:, SEG_RWKV:SEG_RWKV + 1024]
    q, k, v = (proj[:, SEG_Q + j * D_GROUP:SEG_Q + (j + 1) * D_GROUP] for j in range(3))
    z = proj[:, SEG_Z:SEG_Z + D_GROUP]
    xbc = proj[:, SEG_XBC:SEG_XBC + SSD_XBC]
    dtr = proj[:, SEG_DT:SEG_DT + 128]
    ya, v_rwkv, sa = rwkv_fwd(l, seg_r, pars["rwkv"]["taps"], pars["rwkv"], vfirst)
    yb, sb = attn_mix_fwd(l, q, k, v)
    yc, sc = ssd_fwd(l, z, xbc, dtr, pars["ssd"])
    yd, sd = hgrn_fwd(l, seg_h, pars["hgrn"])
    mix = jnp.concatenate([ya, yb, yc, yd], axis=1)
    mo = matmul(mix, wts["out"], "nn", f"out_fwd{l}")
    (x1,) = tl_fwd(ln_res, f"ln1_fwd{l}", [x, mo], pars["ln1"], [D_MODEL])
    u = matmul(x1, wts["up"], "nn", f"up_fwd{l}")
    (h,) = tl_fwd(relu2, f"relu2_fwd{l}", [u], [], [D_FF])
    dn = matmul(h, wts["down"], "nn", f"down_fwd{l}")
    (x2,) = tl_fwd(ln_res, f"ln2_fwd{l}", [x1, dn], pars["ln2"], [D_MODEL])
    return x2, v_rwkv, (x, sa, sb, sc, sd, mix, mo, x1, u, h, dn)


def layer_bwd(l, saved, wts, pars, dx2, dv_extra):
    x, sa, sb, sc, sd, mix, mo, x1, u, h, dn = saved
    S = x.shape[0]
    g = {}
    (dx1a, ddn), g["ln2"] = tl_bwd(ln_res, f"ln2_bwd{l}", [x1, dn], pars["ln2"], [[dx2]])
    g["down"] = matmul(h, ddn, "tn", f"down_dw{l}")
    dh = matmul(ddn, wts["down"], "nt", f"down_dx{l}")
    (du,), _ = tl_bwd(relu2, f"relu2_bwd{l}", [u], [], [[dh]])
    g["up"] = matmul(x1, du, "tn", f"up_dw{l}")
    dx1 = matmul(du, wts["up"], "nt", f"up_dx{l}", add=dx1a)
    (dxa, dmo), g["ln1"] = tl_bwd(ln_res, f"ln1_bwd{l}", [x, mo], pars["ln1"], [[dx1]])
    g["out"] = matmul(mix, dmo, "tn", f"out_dw{l}")
    dmix = matmul(dmo, wts["out"], "nt", f"out_dx{l}")
    dya, dyb, dyc, dyd = (dmix[:, j * D_GROUP:(j + 1) * D_GROUP] for j in range(4))
    dseg_r, dvfirst, g["rwkv_taps"], g["rwkv_pre"], g["rwkv_post"] = rwkv_bwd(l, sa, pars["rwkv"], dya, dv_extra)
    dq, dk, dv = attn_mix_bwd(l, sb, dyb)
    dz, dxbc, ddtr, g["ssd_taps"], g["ssd_pre"], g["ssd_post"] = ssd_bwd(l, sc, pars["ssd"], dyc)
    dseg_h, g["hgrn_pre"], g["hgrn_post"] = hgrn_bwd(l, sd, pars["hgrn"], dyd)
    dproj = jnp.concatenate([dseg_h, dseg_r, dq, dk, dv, dz, dxbc, ddtr, jnp.zeros((S, PROJ_W - SEG_DT - 128), F32)], axis=1)
    g["in"] = matmul(x, dproj, "tn", f"proj_dw{l}")
    dx = matmul(dproj, wts["in"], "nt", f"proj_dx{l}", add=dxa)
    return dx, dvfirst, g


SMALL = ("lower_bounds", "w_in_vres", "mu_shift", "mu_vres", "rwkv_w0", "rwkv_w2", "rwkv_a0", "rwkv_a2", "rwkv_g2",
         "rwkv_k_k", "rwkv_k_a", "rwkv_r_k", "rwkv_lnx_w", "rwkv_lnx_b", "rwkv_v0", "rwkv_v2", "ssd_conv_w",
         "ssd_conv_b", "ssd_dt_bias", "ssd_A_log", "ssd_D", "ssd_norm_w", "hgrn_norm_w", "ln1_w", "ln1_b", "ln2_w", "ln2_b")
BIG = ("w_in", "w_out", "w_up", "w_down")
SMALL_SHARDED = {"w_in_vres": 1, "rwkv_w2": 2, "rwkv_a2": 2, "rwkv_g2": 2, "rwkv_v2": 2, "ssd_conv_w": 2}
WEIGHTS = ("lower_bounds", "w_in", "w_in_vres", "mu_shift", "mu_vres", "rwkv_w0", "rwkv_w2", "rwkv_a0", "rwkv_a2",
           "rwkv_g2", "rwkv_k_k", "rwkv_k_a", "rwkv_r_k", "rwkv_lnx_w", "rwkv_lnx_b", "rwkv_v0", "rwkv_v2",
           "ssd_conv_w", "ssd_conv_b", "ssd_dt_bias", "ssd_A_log", "ssd_D", "ssd_norm_w", "hgrn_norm_w", "w_out",
           "ln1_w", "ln1_b", "w_up", "w_down", "ln2_w", "ln2_b")


def _row(v, width=None):
    v = v.reshape(1, -1).astype(F32)
    if width is not None and v.shape[1] < width:
        v = jnp.pad(v, ((0, 0), (0, width - v.shape[1])))
    return v


def _rows_at(m, rows, at):
    return jnp.pad(m.astype(F32), ((at, rows - at - m.shape[0]), (0, 0)))


def _pad_w_in(w_in_l, vres):
    rows = w_in_l.shape[0]
    out = []
    order = sorted(_PIECES, key=lambda p: p[2])
    pos = 0
    for start, width, at in order:
        if at > pos:
            out.append(jnp.zeros((rows, at - pos), w_in_l.dtype))
        out.append(w_in_l[:, start:start + width])
        pos = at + width
        if at == SEG_RWKV and vres is not None:
            out.append(vres.astype(w_in_l.dtype))
            pos += vres.shape[1]
    out.append(jnp.zeros((rows, PROJ_W - pos), w_in_l.dtype))
    return jnp.concatenate(out, axis=1)


def _unpad_w_in(g):
    return jnp.concatenate([g[:, at:at + width] for _, width, at in _PIECES], axis=1)


def layer_params(l, sp, prep):
    lb, mu, om = prep[l], prep[2 + 2 * l], prep[3 + 2 * l]
    pre = [_row(sp["rwkv_w0"][l]), _rows_at(sp["rwkv_w2"][l], 128, 0), _row(sp["rwkv_a0"][l]),
           _rows_at(sp["rwkv_a2"][l], 128, 32), _rows_at(sp["rwkv_g2"][l], 128, 64),
           _row(sp["rwkv_k_k"][l]), _row(sp["rwkv_k_a"][l])]
    if l:
        pre += [_row(sp["rwkv_v0"][l - 1]), _rows_at(sp["rwkv_v2"][l - 1], 128, 0)]
    return {
        "rwkv": {"taps": jnp.concatenate([mu, om], axis=0), "pre": pre,
                 "post": [_row(sp["rwkv_lnx_w"][l]), _row(sp["rwkv_lnx_b"][l]), _row(sp["rwkv_r_k"][l])]},
        "ssd": {"taps": sp["ssd_conv_w"][l].astype(F32),
                "pre": [_row(sp["ssd_conv_b"][l]), _row(sp["ssd_dt_bias"][l], 128), _row(sp["ssd_A_log"][l], 128)],
                "post": [_row(sp["ssd_D"][l], 128), _row(sp["ssd_norm_w"][l])]},
        "hgrn": {"pre": [lb], "post": [_row(sp["hgrn_norm_w"][l])]},
        "ln1": [_row(sp["ln1_w"][l]), _row(sp["ln1_b"][l])],
        "ln2": [_row(sp["ln2_w"][l]), _row(sp["ln2_b"][l])],
    }


def _mu_full(sp, l):
    parts = [sp["mu_shift"][l].reshape(1, -1)]
    if l:
        parts.append(sp["mu_vres"][l - 1].reshape(1, -1))
    return _row(jnp.concatenate(parts, axis=1), 1024)


def local_step(x, target, big, sp):
    prep_in = [sp["lower_bounds"].astype(F32), _mu_full(sp, 0), _mu_full(sp, 1)]
    prep = small_fwd(param_prep, "param_prep_fwd", prep_in,
                     [(1, D_GROUP), (1, D_GROUP), (1, 1024), (1, 1024), (1, 1024), (1, 1024)])
    pars, wts = [], []
    for l in range(DEPTH):
        pars.append(layer_params(l, sp, prep))
        vres = sp["w_in_vres"][l - 1].astype(BF16) if l else None
        wts.append({"in": _pad_w_in(big["w_in"][l], vres), "out": big["w_out"][l], "up": big["w_up"][l],
                    "down": big["w_down"][l]})
    h, vfirst, saved = x, None, []
    for l in range(DEPTH):
        h, v_l, sv = layer_fwd(l, h, wts[l], pars[l], vfirst)
        vfirst = v_l if l == 0 else vfirst
        saved.append(sv)
    loss_row, dh = loss_and_grad(h, target, "loss")
    grads, dv_extra = [None] * DEPTH, []
    for l in reversed(range(DEPTH)):
        dh, dvfirst, grads[l] = layer_bwd(l, saved[l], wts[l], pars[l], dh, dv_extra)
        dv_extra = [dvfirst] if l else []
    cts = [grads[0]["hgrn_pre"][0], grads[1]["hgrn_pre"][0]]
    for l in range(DEPTH):
        cts += [grads[l]["rwkv_taps"][0:1], grads[l]["rwkv_taps"][1:2]]
    d_lower, d_mu0, d_mu1 = small_bwd(param_prep, "param_prep_bwd", prep_in, cts)
    d_mu = [d_mu0, d_mu1]
    gb = {"w_in": [_unpad_w_in(grads[l]["in"]) for l in range(DEPTH)], "w_out": [grads[l]["out"] for l in range(DEPTH)],
          "w_up": [grads[l]["up"] for l in range(DEPTH)], "w_down": [grads[l]["down"] for l in range(DEPTH)]}
    st = lambda f: jnp.stack([f(l) for l in range(DEPTH)])
    g1 = grads[1]
    gs = {
        "lower_bounds": d_lower,
        "w_in_vres": g1["in"][None, :, VRES_COL:VRES_COL + 32],
        "mu_shift": st(lambda l: d_mu[l][0, :896]),
        "mu_vres": d_mu[1][:, 896:928],
        "rwkv_w0": st(lambda l: grads[l]["rwkv_pre"][0][0]),
        "rwkv_w2": st(lambda l: grads[l]["rwkv_pre"][1][0:32]),
        "rwkv_a0": st(lambda l: grads[l]["rwkv_pre"][2][0]),
        "rwkv_a2": st(lambda l: grads[l]["rwkv_pre"][3][32:64]),
        "rwkv_g2": st(lambda l: grads[l]["rwkv_pre"][4][64:128]),
        "rwkv_k_k": st(lambda l: grads[l]["rwkv_pre"][5][0]),
        "rwkv_k_a": st(lambda l: grads[l]["rwkv_pre"][6][0]),
        "rwkv_r_k": st(lambda l: grads[l]["rwkv_post"][2].reshape(N_HEADS, HEAD_DIM)),
        "rwkv_lnx_w": st(lambda l: grads[l]["rwkv_post"][0][0]),
        "rwkv_lnx_b": st(lambda l: grads[l]["rwkv_post"][1][0]),
        "rwkv_v0": g1["rwkv_pre"][7],
        "rwkv_v2": g1["rwkv_pre"][8][None, 0:32],
        "ssd_conv_w": st(lambda l: grads[l]["ssd_taps"]),
        "ssd_conv_b": st(lambda l: grads[l]["ssd_pre"][0][0]),
        "ssd_dt_bias": st(lambda l: grads[l]["ssd_pre"][1][0, :N_HEADS]),
        "ssd_A_log": st(lambda l: grads[l]["ssd_pre"][2][0, :N_HEADS]),
        "ssd_D": st(lambda l: grads[l]["ssd_post"][0][0, :N_HEADS]),
        "ssd_norm_w": st(lambda l: grads[l]["ssd_post"][1][0]),
        "hgrn_norm_w": st(lambda l: grads[l]["hgrn_post"][0][0]),
        "ln1_w": st(lambda l: grads[l]["ln1"][0][0]),
        "ln1_b": st(lambda l: grads[l]["ln1"][1][0]),
        "ln2_w": st(lambda l: grads[l]["ln2"][0][0]),
        "ln2_b": st(lambda l: grads[l]["ln2"][1][0]),
    }
    return loss_row, dh, gb, gs


def _pack(vecs):
    flat, meta, pos = [], [], 0
    for v in vecs:
        flat.append(v.reshape(-1).astype(F32))
        meta.append((pos, v.shape))
        pos += v.size
    total = -(-pos // 1024) * 1024
    flat.append(jnp.zeros((total - pos,), F32))
    return jnp.concatenate(flat).reshape(total // 128, 128), meta


def _unpack(packed, meta):
    flat = packed.reshape(-1)
    return [flat[off:off + math.prod(shape)].reshape(shape) for off, shape in meta]


def _to_shards(name, g):
    if name == "w_in":
        return jnp.transpose(g.reshape(g.shape[0], 4, g.shape[1] // 4), (1, 0, 2))
    if name == "w_up":
        return jnp.transpose(g.reshape(g.shape[0], 4, g.shape[1] // 4), (1, 0, 2))
    return g.reshape(4, g.shape[0] // 4, g.shape[1])


def _from_chips(name, g):
    if name in ("w_in", "w_up"):
        return jnp.transpose(g, (1, 2, 0, 3)).reshape(g.shape[1], g.shape[2], 4 * g.shape[3])
    return jnp.transpose(g, (1, 0, 2, 3)).reshape(g.shape[1], 4 * g.shape[2], g.shape[3])


INPUT_NAMES = ("x",) + WEIGHTS + ("loss_target",) + tuple("m_" + n for n in WEIGHTS) + tuple("v_" + n for n in WEIGHTS)


def _step(*args):
    a = dict(zip(INPUT_NAMES, args, strict=True))
    chip = 2 * lax.axis_index("x") + lax.axis_index("y")

    sharded_names = list(SMALL_SHARDED)
    small_pack, small_meta = _pack([a[n] for n in sharded_names])
    gathered = gather_chips([a[n].astype(BF16) for n in BIG] + [small_pack], "gather_weights")
    big = {n: _from_chips(n, g) for n, g in zip(BIG, gathered)}
    sp = {n: a[n] for n in SMALL if n not in SMALL_SHARDED}
    per_chip = [_unpack(gathered[-1][s], small_meta) for s in range(4)]
    for j, n in enumerate(sharded_names):
        sp[n] = jnp.concatenate([per_chip[s][j] for s in range(4)], axis=SMALL_SHARDED[n])

    loss_row, gx, gb, gs = local_step(a["x"][0], a["loss_target"][0], big, sp)

    partials = [jnp.stack([_to_shards(n, gb[n][l]) for l in range(DEPTH)]) for n in BIG]
    slots = scatter_partials(partials, "reduce_big")
    mine = []
    for n, sl in zip(BIG, slots):
        rows, cols = sl.shape[-2:]
        mine.append(sum_parts(sl.reshape(8, rows, cols), f"sum_{n}"))
    summed = sibling_exchange(mine, "exchange_big")
    out_g, out_d, out_m, out_v = {}, {}, {}, {}
    for n, g in zip(BIG, summed):
        shape = a[n].shape
        flat = lambda t: t.reshape(shape[0] * shape[1], shape[2])
        d, nm, nv = adamw(flat(a[n]), flat(g), flat(a["m_" + n]), flat(a["v_" + n]), f"adamw_{n}")
        out_g[n], out_d[n], out_m[n], out_v[n] = g.reshape(shape), d.reshape(shape), nm.reshape(shape), nv.reshape(shape)

    vec, meta = _pack([loss_row] + [gs[n] for n in SMALL])
    total = sum_parts(gather_devices(vec, "gather_small"), "sum_small")
    parts = _unpack(total, meta)
    loss = parts[0][0, 0]
    g_small = {}
    for n, g in zip(SMALL, parts[1:]):
        if n in SMALL_SHARDED:
            ax = SMALL_SHARDED[n]
            size = a[n].shape[ax]
            g = lax.dynamic_slice_in_dim(g, chip * size, size, axis=ax)
        g_small[n] = g
    pw, pmeta = _pack([a[n] for n in SMALL])
    pg, _ = _pack([g_small[n] for n in SMALL])
    pm, _ = _pack([a["m_" + n] for n in SMALL])
    pv, _ = _pack([a["v_" + n] for n in SMALL])
    d, nm, nv = adamw(pw, pg, pm, pv, "adamw_small")
    for n, dd, mm, vv in zip(SMALL, _unpack(d, pmeta), _unpack(nm, pmeta), _unpack(nv, pmeta)):
        out_g[n], out_d[n], out_m[n], out_v[n] = g_small[n], dd, mm, vv

    return (loss, gx[None], *[out_g[n] for n in WEIGHTS], *[out_d[n] for n in WEIGHTS],
            *[out_m[n] for n in WEIGHTS], *[out_v[n] for n in WEIGHTS])


def kernel(x, lower_bounds, w_in, w_in_vres, mu_shift, mu_vres, rwkv_w0, rwkv_w2, rwkv_a0, rwkv_a2, rwkv_g2, rwkv_k_k, rwkv_k_a, rwkv_r_k, rwkv_lnx_w, rwkv_lnx_b, rwkv_v0, rwkv_v2, ssd_conv_w, ssd_conv_b, ssd_dt_bias, ssd_A_log, ssd_D, ssd_norm_w, hgrn_norm_w, w_out, ln1_w, ln1_b, w_up, w_down, ln2_w, ln2_b, loss_target, m_lower_bounds, m_w_in, m_w_in_vres, m_mu_shift, m_mu_vres, m_rwkv_w0, m_rwkv_w2, m_rwkv_a0, m_rwkv_a2, m_rwkv_g2, m_rwkv_k_k, m_rwkv_k_a, m_rwkv_r_k, m_rwkv_lnx_w, m_rwkv_lnx_b, m_rwkv_v0, m_rwkv_v2, m_ssd_conv_w, m_ssd_conv_b, m_ssd_dt_bias, m_ssd_A_log, m_ssd_D, m_ssd_norm_w, m_hgrn_norm_w, m_w_out, m_ln1_w, m_ln1_b, m_w_up, m_w_down, m_ln2_w, m_ln2_b, v_lower_bounds, v_w_in, v_w_in_vres, v_mu_shift, v_mu_vres, v_rwkv_w0, v_rwkv_w2, v_rwkv_a0, v_rwkv_a2, v_rwkv_g2, v_rwkv_k_k, v_rwkv_k_a, v_rwkv_r_k, v_rwkv_lnx_w, v_rwkv_lnx_b, v_rwkv_v0, v_rwkv_v2, v_ssd_conv_w, v_ssd_conv_b, v_ssd_dt_bias, v_ssd_A_log, v_ssd_D, v_ssd_norm_w, v_hgrn_norm_w, v_w_out, v_ln1_w, v_ln1_b, v_w_up, v_w_down, v_ln2_w, v_ln2_b):
    return _step(x, lower_bounds, w_in, w_in_vres, mu_shift, mu_vres, rwkv_w0, rwkv_w2, rwkv_a0, rwkv_a2, rwkv_g2, rwkv_k_k, rwkv_k_a, rwkv_r_k, rwkv_lnx_w, rwkv_lnx_b, rwkv_v0, rwkv_v2, ssd_conv_w, ssd_conv_b, ssd_dt_bias, ssd_A_log, ssd_D, ssd_norm_w, hgrn_norm_w, w_out, ln1_w, ln1_b, w_up, w_down, ln2_w, ln2_b, loss_target, m_lower_bounds, m_w_in, m_w_in_vres, m_mu_shift, m_mu_vres, m_rwkv_w0, m_rwkv_w2, m_rwkv_a0, m_rwkv_a2, m_rwkv_g2, m_rwkv_k_k, m_rwkv_k_a, m_rwkv_r_k, m_rwkv_lnx_w, m_rwkv_lnx_b, m_rwkv_v0, m_rwkv_v2, m_ssd_conv_w, m_ssd_conv_b, m_ssd_dt_bias, m_ssd_A_log, m_ssd_D, m_ssd_norm_w, m_hgrn_norm_w, m_w_out, m_ln1_w, m_ln1_b, m_w_up, m_w_down, m_ln2_w, m_ln2_b, v_lower_bounds, v_w_in, v_w_in_vres, v_mu_shift, v_mu_vres, v_rwkv_w0, v_rwkv_w2, v_rwkv_a0, v_rwkv_a2, v_rwkv_g2, v_rwkv_k_k, v_rwkv_k_a, v_rwkv_r_k, v_rwkv_lnx_w, v_rwkv_lnx_b, v_rwkv_v0, v_rwkv_v2, v_ssd_conv_w, v_ssd_conv_b, v_ssd_dt_bias, v_ssd_A_log, v_ssd_D, v_ssd_norm_w, v_hgrn_norm_w, v_w_out, v_ln1_w, v_ln1_b, v_w_up, v_w_down, v_ln2_w, v_ln2_b)
```

```python
import functools
import math

import jax
import jax.numpy as jnp
from jax import lax
from jax.experimental import pallas as pl
from jax.experimental.pallas import tpu as pltpu

F32 = jnp.float32
BF16 = jnp.bfloat16
HI = lax.Precision.HIGHEST

DEPTH = 2
D_MODEL = 1024
D_GROUP = 256
HEAD_DIM = 64
N_HEADS = 4
SSD_STATE = 128
SSD_XBC = 768
SSD_CONV = 4
D_FF = 4096
ALPHA = (2.0 * DEPTH) ** 0.25
LN_EPS = 1e-5
RMS_EPS = 1e-5
RWKV_GN_EPS = HEAD_DIM * 1e-5
DILATED_BRANCHES = ((128, 1), (512, 4), (2048, 16))
ALIBI_SLOPES = tuple(2.0 ** (-8.0 * (h + 1) / N_HEADS) for h in range(N_HEADS))
ATTN_BLK = 128

ADAM_LR, ADAM_B1, ADAM_B2, ADAM_EPS, ADAM_WD, ADAM_STEP = 0.001, 0.9, 0.999, 1e-08, 0.01, 10

IN_COLS = 3716
PROJ_W = 4096
SEG_HGRN, SEG_RWKV, SEG_Q, SEG_Z, SEG_XBC, SEG_DT = 0, 1024, 2048, 2816, 3072, 3840
_PIECES = ((0, 896, SEG_RWKV), (896, 768, SEG_Q), (1664, 256, SEG_Z), (1920, 768, SEG_XBC),
           (2688, 4, SEG_DT), (2692, 1024, SEG_HGRN))
VRES_COL = SEG_RWKV + 896

ROW_TILE = 256
SCAN_CHUNK = 128
VMEM_LIMIT = 48 * 1024 * 1024


def _cparams(sem=None):
    if sem is None:
        return pltpu.CompilerParams(vmem_limit_bytes=VMEM_LIMIT)
    return pltpu.CompilerParams(dimension_semantics=sem, vmem_limit_bytes=VMEM_LIMIT)


def _pick(n, pref):
    for t in pref:
        if n % t == 0:
            return t
    return n


def matmul(a, b, mode, name, add=None, out_dtype=F32):
    if mode == "nn":
        (M, K), (_, N) = a.shape, b.shape
    elif mode == "nt":
        (M, K), (N, _) = a.shape, b.shape
    else:
        (K, M), (_, N) = a.shape, b.shape
    tm, tn, tk = _pick(M, (1024, 512, 256, 128)), _pick(N, (1024, 512, 256, 128)), _pick(K, (512, 256, 128))
    nk = K // tk
    dims = {"nn": (((1,), (0,)), ((), ())), "nt": (((1,), (1,)), ((), ())), "tn": (((0,), (0,)), ((), ()))}[mode]

    def body(*refs):
        if add is None:
            a_ref, b_ref, o_ref, acc = refs
            add_ref = None
        else:
            a_ref, b_ref, add_ref, o_ref, acc = refs
        k = pl.program_id(2)

        @pl.when(k == 0)
        def _():
            acc[...] = jnp.zeros_like(acc)

        acc[...] += lax.dot_general(a_ref[...].astype(BF16), b_ref[...].astype(BF16), dims,
                                    preferred_element_type=F32)

        @pl.when(k == nk - 1)
        def _():
            r = acc[...]
            if add_ref is not None:
                r = r + add_ref[...]
            o_ref[...] = r.astype(o_ref.dtype)

    a_spec = pl.BlockSpec((tk, tm), lambda i, j, k: (k, i)) if mode == "tn" else pl.BlockSpec((tm, tk), lambda i, j, k: (i, k))
    b_spec = pl.BlockSpec((tn, tk), lambda i, j, k: (j, k)) if mode == "nt" else pl.BlockSpec((tk, tn), lambda i, j, k: (k, j))
    o_spec = pl.BlockSpec((tm, tn), lambda i, j, k: (i, j))
    ins, specs = [a, b], [a_spec, b_spec]
    if add is not None:
        ins.append(add)
        specs.append(o_spec)
    return pl.pallas_call(
        body, grid=(M // tm, N // tn, nk), in_specs=specs, out_specs=o_spec,
        out_shape=jax.ShapeDtypeStruct((M, N), out_dtype), scratch_shapes=[pltpu.VMEM((tm, tn), F32)],
        compiler_params=_cparams(("parallel", "parallel", "arbitrary")), name=name)(*ins)


def _row_spec(w, tile):
    return pl.BlockSpec((tile, w), lambda i: (i, 0))


def _par_spec(shape):
    return pl.BlockSpec(shape, lambda i: (0,) * len(shape))


def tl_fwd(fn, name, rows, pars, out_widths, tile=ROW_TILE):
    S = rows[0].shape[0]
    nr = len(rows)

    def body(*refs):
        ins = [r[...] for r in refs[:nr + len(pars)]]
        outs = fn(*ins)
        for o_ref, o in zip(refs[nr + len(pars):], outs):
            o_ref[...] = o

    return pl.pallas_call(
        body, grid=(S // tile,),
        in_specs=[_row_spec(r.shape[1], tile) for r in rows] + [_par_spec(p.shape) for p in pars],
        out_specs=[_row_spec(w, tile) for w in out_widths],
        out_shape=[jax.ShapeDtypeStruct((S, w), F32) for w in out_widths],
        compiler_params=_cparams(("parallel",)), name=name)(*rows, *pars)


def tl_bwd(fn, name, rows, pars, cts, tile=ROW_TILE, row_grad=None):
    S = rows[0].shape[0]
    nr, npar = len(rows), len(pars)
    row_grad = [True] * nr if row_grad is None else row_grad
    flat_cts = [c for group in cts for c in group]
    ncts = len(flat_cts)
    gi = [i for i in range(nr) if row_grad[i]]

    def body(*refs):
        row_v = [r[...] for r in refs[:nr]]
        par_v = [r[...] for r in refs[nr:nr + npar]]
        ct_refs = refs[nr + npar:nr + npar + ncts]
        out_refs = refs[nr + npar + ncts:]
        ct_v, pos = [], 0
        for group in cts:
            acc = ct_refs[pos][...]
            for q in range(1, len(group)):
                acc = acc + ct_refs[pos + q][...]
            pos += len(group)
            ct_v.append(acc)

        def f(diff_rows, par_vals):
            full = list(row_v)
            for idx, val in zip(gi, diff_rows):
                full[idx] = val
            return tuple(fn(*full, *par_vals))

        _, vjp = jax.vjp(f, [row_v[i] for i in gi], par_v)
        d_rows, d_pars = vjp(tuple(ct_v))
        for o_ref, g in zip(out_refs[:len(gi)], d_rows):
            o_ref[...] = g
        first = pl.program_id(0) == 0
        for o_ref, g in zip(out_refs[len(gi):], d_pars):
            @pl.when(first)
            def _(o_ref=o_ref):
                o_ref[...] = jnp.zeros_like(o_ref)
            o_ref[...] += g

    outs = pl.pallas_call(
        body, grid=(S // tile,),
        in_specs=[_row_spec(r.shape[1], tile) for r in rows] + [_par_spec(p.shape) for p in pars]
        + [_row_spec(c.shape[1], tile) for c in flat_cts],
        out_specs=[_row_spec(rows[i].shape[1], tile) for i in gi] + [_par_spec(p.shape) for p in pars],
        out_shape=[jax.ShapeDtypeStruct(rows[i].shape, F32) for i in gi] + [jax.ShapeDtypeStruct(p.shape, F32) for p in pars],
        compiler_params=_cparams(("arbitrary",)), name=name)(*rows, *pars, *flat_cts)
    return list(outs[:len(gi)]), list(outs[len(gi):])


def _shift_rows(x, j):
    if j == 0:
        return x
    rolled = pltpu.roll(x, j, 0)
    row = lax.broadcasted_iota(jnp.int32, x.shape, 0)
    return jnp.where(row >= j, rolled, 0.0)


def _unshift_rows(x, j):
    if j == 0:
        return x
    S = x.shape[0]
    rolled = pltpu.roll(x, S - j, 0)
    row = lax.broadcasted_iota(jnp.int32, x.shape, 0)
    return jnp.where(row < S - j, rolled, 0.0)


def fir_fwd(x, taps, name):
    S, C = x.shape
    K = taps.shape[0]

    def body(x_ref, w_ref, y_ref):
        xv = x_ref[...]
        acc = jnp.zeros_like(xv)
        for k in range(K):
            acc = acc + _shift_rows(xv, K - 1 - k) * w_ref[pl.ds(k, 1), :]
        y_ref[...] = acc

    cs = pl.BlockSpec((S, 128), lambda j: (0, j))
    return pl.pallas_call(body, grid=(C // 128,), in_specs=[cs, pl.BlockSpec((K, 128), lambda j: (0, j))],
                          out_specs=cs, out_shape=jax.ShapeDtypeStruct((S, C), F32),
                          compiler_params=_cparams(("parallel",)), name=name)(x, taps)


def fir_bwd(x, taps, dy_list, name):
    S, C = x.shape
    K = taps.shape[0]
    n = len(dy_list)

    def body(*refs):
        x_ref, w_ref = refs[:2]
        dy = refs[2][...]
        for q in range(1, n):
            dy = dy + refs[2 + q][...]
        dx_ref, dw_ref, db_ref = refs[2 + n:]
        xv = x_ref[...]
        dx = jnp.zeros_like(xv)
        for k in range(K):
            j = K - 1 - k
            dx = dx + _unshift_rows(dy, j) * w_ref[pl.ds(k, 1), :]
            dw_ref[pl.ds(k, 1), :] = jnp.sum(dy * _shift_rows(xv, j), axis=0, keepdims=True)
        dx_ref[...] = dx
        db_ref[...] = jnp.sum(dy, axis=0, keepdims=True)

    cs = pl.BlockSpec((S, 128), lambda j: (0, j))
    ks = pl.BlockSpec((K, 128), lambda j: (0, j))
    bs = pl.BlockSpec((1, 128), lambda j: (0, j))
    return pl.pallas_call(body, grid=(C // 128,), in_specs=[cs, ks] + [cs] * n, out_specs=[cs, ks, bs],
                          out_shape=[jax.ShapeDtypeStruct((S, C), F32), jax.ShapeDtypeStruct((K, C), F32),
                                     jax.ShapeDtypeStruct((1, C), F32)],
                          compiler_params=_cparams(("parallel",)), name=name)(x, taps, *dy_list)


def _col(tile, lane, t):
    return jnp.sum(jnp.where(lane == t, tile, 0.0), axis=1, keepdims=True)


def _rwkv_step(s, rv, vcol):
    sa = jnp.sum(s * (-rv[3]), axis=1, keepdims=True)
    return s * rv[1] + sa * (rv[3] * rv[4]) + vcol * rv[2], sa


def scan_fwd(r, w, k, vT, kk, a, name):
    H, S, Dk = r.shape
    Dv = vT.shape[1]
    Tc = SCAN_CHUNK
    nc = S // Tc
    rows = [r, w, k, kk, a]

    def body(*refs):
        row_refs = refs[:5]
        vT_ref, yT_ref, sall_ref, s_ref = refs[5:]

        @pl.when(pl.program_id(0) == 0)
        def _():
            s_ref[...] = jnp.zeros_like(s_ref)

        yT_ref[...] = jnp.zeros_like(yT_ref)
        lane = lax.broadcasted_iota(jnp.int32, (Dv, Tc), 1)

        def step(t, states):
            new = []
            for h in range(H):
                s = states[h]
                sall_ref[t, h] = s
                rv = [ref[h, pl.ds(t, 1), :] for ref in row_refs]
                s, _ = _rwkv_step(s, rv, _col(vT_ref[h], lane, t))
                ycol = jnp.sum(s * rv[0], axis=1, keepdims=True)
                yT_ref[h] = jnp.where(lane == t, ycol, yT_ref[h])
                new.append(s)
            return tuple(new)

        states = lax.fori_loop(0, Tc, step, tuple(s_ref[h] for h in range(H)))
        for h in range(H):
            s_ref[h] = states[h]

    rs = pl.BlockSpec((H, Tc, Dk), lambda c: (0, c, 0))
    vs = pl.BlockSpec((H, Dv, Tc), lambda c: (0, 0, c))
    yT, sall = pl.pallas_call(
        body, grid=(nc,), in_specs=[rs] * 5 + [vs],
        out_specs=[vs, pl.BlockSpec((Tc, H, Dv, Dk), lambda c: (c, 0, 0, 0))],
        out_shape=[jax.ShapeDtypeStruct((H, Dv, S), F32), jax.ShapeDtypeStruct((S, H, Dv, Dk), F32)],
        scratch_shapes=[pltpu.VMEM((H, Dv, Dk), F32)],
        compiler_params=_cparams(("arbitrary",)), name=name)(*rows, vT)
    return yT, sall


def scan_bwd(r, w, k, vT, kk, a, sall, dyT, name):
    H, S, Dk = r.shape
    Dv = vT.shape[1]
    Tc = SCAN_CHUNK
    nc = S // Tc
    rows = [r, w, k, kk, a]

    def body(*refs):
        row_refs = refs[:5]
        vT_ref, dyT_ref, sall_ref = refs[5:8]
        drow_refs = refs[8:13]
        dvT_ref, ds_ref = refs[13:]

        @pl.when(pl.program_id(0) == 0)
        def _():
            ds_ref[...] = jnp.zeros_like(ds_ref)

        dvT_ref[...] = jnp.zeros_like(dvT_ref)
        lane = lax.broadcasted_iota(jnp.int32, (Dv, Tc), 1)

        def bstep(i, carry):
            t = Tc - 1 - i
            new = []
            for h in range(H):
                ds = carry[h]
                sp = sall_ref[t, h]
                rv = [ref[h, pl.ds(t, 1), :] for ref in row_refs]
                vcol = _col(vT_ref[h], lane, t)
                dycol = _col(dyT_ref[h], lane, t)
                st, sa = _rwkv_step(sp, rv, vcol)
                drow_refs[0][h, pl.ds(t, 1), :] = jnp.sum(st * dycol, axis=0, keepdims=True)
                g = ds + dycol * rv[0]
                drow_refs[1][h, pl.ds(t, 1), :] = jnp.sum(g * sp, axis=0, keepdims=True)
                drow_refs[2][h, pl.ds(t, 1), :] = jnp.sum(g * vcol, axis=0, keepdims=True)
                dvcol = jnp.sum(g * rv[2], axis=1, keepdims=True)
                dsa = jnp.sum(g * (rv[3] * rv[4]), axis=1, keepdims=True)
                db = jnp.sum(g * sa, axis=0, keepdims=True)
                dnkk = jnp.sum(sp * dsa, axis=0, keepdims=True)
                drow_refs[3][h, pl.ds(t, 1), :] = db * rv[4] - dnkk
                drow_refs[4][h, pl.ds(t, 1), :] = db * rv[3]
                dvT_ref[h] = jnp.where(lane == t, dvcol, dvT_ref[h])
                new.append(g * rv[1] - dsa * rv[3])
            return tuple(new)

        carry = lax.fori_loop(0, Tc, bstep, tuple(ds_ref[h] for h in range(H)))
        for h in range(H):
            ds_ref[h] = carry[h]

    rs = pl.BlockSpec((H, Tc, Dk), lambda c: (0, nc - 1 - c, 0))
    vs = pl.BlockSpec((H, Dv, Tc), lambda c: (0, 0, nc - 1 - c))
    outs = pl.pallas_call(
        body, grid=(nc,),
        in_specs=[rs] * 5 + [vs, vs, pl.BlockSpec((Tc, H, Dv, Dk), lambda c: (nc - 1 - c, 0, 0, 0))],
        out_specs=[rs] * 5 + [vs],
        out_shape=[jax.ShapeDtypeStruct((H, S, Dk), F32)] * 5 + [jax.ShapeDtypeStruct((H, Dv, S), F32)],
        scratch_shapes=[pltpu.VMEM((H, Dv, Dk), F32)],
        compiler_params=_cparams(("arbitrary",)), name=name)(*rows, vT, dyT, sall)
    return list(outs[:5]), outs[5]


CHUNK = 128


def chunk_fwd(fn, name, blocks, state_shape, out_width):
    H, S, _ = blocks[0].shape
    nc = S // CHUNK
    nb = len(blocks)

    def body(*refs):
        o_ref, sv_ref, st = refs[nb:]

        @pl.when(pl.program_id(1) == 0)
        def _():
            st[...] = jnp.zeros_like(st)

        s0 = st[...]
        sv_ref[0, 0] = s0
        s1, out = fn(s0, *[r[0] for r in refs[:nb]])
        st[...] = s1
        o_ref[0] = out

    spec = lambda w: pl.BlockSpec((1, CHUNK, w), lambda h, c: (h, c, 0))
    return pl.pallas_call(
        body, grid=(H, nc), in_specs=[spec(b.shape[2]) for b in blocks],
        out_specs=[spec(out_width), pl.BlockSpec((1, 1) + state_shape, lambda h, c: (h, c, 0, 0))],
        out_shape=[jax.ShapeDtypeStruct((H, S, out_width), F32), jax.ShapeDtypeStruct((H, nc) + state_shape, F32)],
        scratch_shapes=[pltpu.VMEM(state_shape, F32)],
        compiler_params=_cparams(("parallel", "arbitrary")), name=name)(*blocks)


def chunk_bwd(fn, name, blocks, states, dout):
    H, S, _ = blocks[0].shape
    nc = S // CHUNK
    nb = len(blocks)
    state_shape = states.shape[2:]

    def body(*refs):
        sv_ref, do_ref = refs[nb], refs[nb + 1]
        d_refs = refs[nb + 2:2 * nb + 2]
        dst = refs[2 * nb + 2]

        @pl.when(pl.program_id(1) == 0)
        def _():
            dst[...] = jnp.zeros_like(dst)

        _, vjp = jax.vjp(fn, sv_ref[0, 0], *[r[0] for r in refs[:nb]])
        grads = vjp((dst[...], do_ref[0]))
        dst[...] = grads[0]
        for d_ref, g in zip(d_refs, grads[1:]):
            d_ref[0] = g

    spec = lambda w: pl.BlockSpec((1, CHUNK, w), lambda h, c: (h, nc - 1 - c, 0))
    return pl.pallas_call(
        body, grid=(H, nc),
        in_specs=[spec(b.shape[2]) for b in blocks]
        + [pl.BlockSpec((1, 1) + state_shape, lambda h, c: (h, nc - 1 - c, 0, 0)), spec(dout.shape[2])],
        out_specs=[spec(b.shape[2]) for b in blocks],
        out_shape=[jax.ShapeDtypeStruct(b.shape, F32) for b in blocks],
        scratch_shapes=[pltpu.VMEM(state_shape, F32)],
        compiler_params=_cparams(("parallel", "arbitrary")), name=name)(*blocks, states, dout)


def _bdot(a, b, dims):
    return lax.dot_general(a.astype(BF16), b.astype(BF16), (dims, ((), ())), preferred_element_type=F32)


def ssd_chunk(state, cb, bb, da, xdt):
    T = cb.shape[0]
    ti = lax.broadcasted_iota(jnp.int32, (T, T), 0)
    si = lax.broadcasted_iota(jnp.int32, (T, T), 1)
    mask = ti >= si
    cs = jnp.dot(mask.astype(F32), da, precision=HI, preferred_element_type=F32)
    pick = (lax.broadcasted_iota(jnp.int32, cs.shape, 1) == 0).astype(F32)
    cs_row = lax.dot_general(pick, cs, (((1,), (1,)), ((), ())), precision=HI, preferred_element_type=F32)
    lmat = jnp.where(mask, jnp.exp(jnp.where(mask, cs - cs_row, 0.0)), 0.0)
    scores = _bdot(cb, bb, ((1,), (1,))) * lmat
    y = _bdot(scores, xdt, ((1,), (0,))) + _bdot(cb, state, ((1,), (1,))) * jnp.exp(cs[:, :HEAD_DIM])
    last = cs[T - 1:T, :]
    new_state = state * jnp.exp(last) + _bdot(xdt, bb * jnp.exp(last - cs), ((0,), (0,)))
    return new_state, y


HGRN_SUB = 16


def hgrn_chunk(state, q, k, lf, v):
    T, C = q.shape[0], HGRN_SUB
    ti = lax.broadcasted_iota(jnp.int32, (C, C), 0)
    si = lax.broadcasted_iota(jnp.int32, (C, C), 1)
    tril = (ti >= si).astype(F32)
    row = lax.broadcasted_iota(jnp.int32, (C, q.shape[1]), 0)
    outs = []
    for j in range(T // C):
        qj, kj, lj, vj = (a[j * C:(j + 1) * C] for a in (q, k, lf, v))
        b = jnp.dot(tril, lj, precision=HI, preferred_element_type=F32)
        o = _bdot(qj * jnp.exp(b), state, ((1,), (1,)))
        for s in range(C):
            m = row >= s
            e = jnp.where(m, jnp.exp(jnp.where(m, b - b[s:s + 1], 0.0)), 0.0)
            o = o + jnp.sum(qj * kj[s:s + 1] * e, axis=1, keepdims=True) * vj[s:s + 1]
        last = b[C - 1:C]
        state = state * jnp.exp(last) + _bdot(vj, kj * jnp.exp(last - b), ((0,), (0,)))
        outs.append(o)
    return state, jnp.concatenate(outs, axis=0)


def _to_heads(x, dk):
    S = x.shape[0]
    return jnp.transpose(x.reshape(S, N_HEADS, dk), (1, 0, 2))


def _from_heads(x):
    H, S, dk = x.shape
    return jnp.transpose(x, (1, 0, 2)).reshape(S, H * dk)


def _to_headsT(x):
    S = x.shape[0]
    return jnp.transpose(x.reshape(S, N_HEADS, HEAD_DIM), (1, 2, 0))


def _from_headsT(x):
    H, dv, S = x.shape
    return jnp.transpose(x, (2, 0, 1)).reshape(S, H * dv)


def run_scan_fwd(R, W, K, V, KK, A, dk, name):
    heads = [_to_heads(t, dk) for t in (R, W, K, KK, A)] + [_to_headsT(V)]
    yT, sall = scan_fwd(heads[0], heads[1], heads[2], heads[5], heads[3], heads[4], name)
    return _from_headsT(yT), (heads, sall)


def run_scan_bwd(saved, dY, name):
    heads, sall = saved
    drows, dvT = scan_bwd(heads[0], heads[1], heads[2], heads[5], heads[3], heads[4], sall, _to_headsT(dY), name)
    return [_from_heads(d) for d in drows], _from_headsT(dvT)


def _attn_block(q, kp, kc, vp, vc, n, slope, dilation):
    blk = ATTN_BLK
    k2 = jnp.concatenate([kp, kc], axis=0)
    v2 = jnp.concatenate([vp, vc], axis=0)
    s = _bdot(q, k2, ((1,), (1,))) * (HEAD_DIM ** -0.5)
    i = lax.broadcasted_iota(jnp.int32, (blk, 2 * blk), 0)
    j = lax.broadcasted_iota(jnp.int32, (blk, 2 * blk), 1)
    dist = blk + i - j
    first_key = jnp.where(n > 0, 0, blk)
    valid = (dist >= 0) & (dist <= blk) & (j >= first_key)
    s = s - slope * (dist * dilation).astype(F32)
    s = jnp.where(valid, s, -1e30)
    m = jnp.max(s, axis=-1, keepdims=True)
    p = jnp.exp(s - m)
    l = jnp.sum(p, axis=-1, keepdims=True)
    o = _bdot(p, v2, ((1,), (0,))) / l
    lse = jnp.broadcast_to(m + jnp.log(l), o.shape)
    return o, lse


def _slope_of(h):
    s = jnp.float32(ALIBI_SLOPES[N_HEADS - 1])
    for hh in range(N_HEADS - 2, -1, -1):
        s = jnp.where(h == hh, jnp.float32(ALIBI_SLOPES[hh]), s)
    return s


def _attn_specs(nb):
    blk = ATTN_BLK
    cur = pl.BlockSpec((1, 1, blk, HEAD_DIM), lambda h, z, n: (h, z, n, 0))
    prev = pl.BlockSpec((1, 1, blk, HEAD_DIM), lambda h, z, n: (h, z, jnp.maximum(n - 1, 0), 0))
    return cur, prev


def attn_fwd(q, k, v, dilation, name):
    H, Z, L, _ = q.shape
    nb = L // ATTN_BLK
    cur, prev = _attn_specs(nb)

    def body(q_ref, kp_ref, kc_ref, vp_ref, vc_ref, o_ref, l_ref):
        o, lse = _attn_block(q_ref[0, 0], kp_ref[0, 0], kc_ref[0, 0], vp_ref[0, 0], vc_ref[0, 0],
                             pl.program_id(2), _slope_of(pl.program_id(0)), dilation)
        o_ref[0, 0] = o
        l_ref[0, 0] = lse

    return pl.pallas_call(body, grid=(H, Z, nb), in_specs=[cur, prev, cur, prev, cur], out_specs=[cur, cur],
                          out_shape=[jax.ShapeDtypeStruct(q.shape, F32)] * 2,
                          compiler_params=_cparams(("parallel", "parallel", "arbitrary")), name=name)(q, k, k, v, v)


def attn_bwd(q, k, v, do, dlse, dilation, name):
    H, Z, L, _ = q.shape
    blk = ATTN_BLK
    nb = L // blk
    cur, prev = _attn_specs(nb)
    full = pl.BlockSpec((1, 1, L, HEAD_DIM), lambda h, z, n: (h, z, 0, 0))

    def body(q_ref, kp_ref, kc_ref, vp_ref, vc_ref, do_ref, dl_ref, dq_ref, dk_ref, dv_ref):
        n = pl.program_id(2)
        slope = _slope_of(pl.program_id(0))
        f = lambda q_, kp_, kc_, vp_, vc_: _attn_block(q_, kp_, kc_, vp_, vc_, n, slope, dilation)
        _, vjp = jax.vjp(f, q_ref[0, 0], kp_ref[0, 0], kc_ref[0, 0], vp_ref[0, 0], vc_ref[0, 0])
        dq, dkp, dkc, dvp, dvc = vjp((do_ref[0, 0], dl_ref[0, 0]))
        dq_ref[0, 0] = dq

        @pl.when(n == 0)
        def _():
            dk_ref[...] = jnp.zeros_like(dk_ref)
            dv_ref[...] = jnp.zeros_like(dv_ref)

        here = pl.ds(pl.multiple_of(n * blk, blk), blk)
        dk_ref[0, 0, here, :] += dkc
        dv_ref[0, 0, here, :] += dvc

        @pl.when(n > 0)
        def _():
            before = pl.ds(pl.multiple_of((n - 1) * blk, blk), blk)
            dk_ref[0, 0, before, :] += dkp
            dv_ref[0, 0, before, :] += dvp

    return pl.pallas_call(body, grid=(H, Z, nb), in_specs=[cur, prev, cur, prev, cur, cur, cur],
                          out_specs=[cur, full, full], out_shape=[jax.ShapeDtypeStruct(q.shape, F32)] * 3,
                          compiler_params=_cparams(("parallel", "parallel", "arbitrary")), name=name)(q, k, k, v, v, do, dlse)


def _to_sub(t, d):
    S = t.shape[0]
    return jnp.transpose(t.reshape(S // d, d, N_HEADS, HEAD_DIM), (2, 1, 0, 3))


def _from_sub(t):
    H, d, L, _ = t.shape
    return jnp.transpose(t, (2, 1, 0, 3)).reshape(L * d, H * HEAD_DIM)


def _head_ones(width, group):
    i = lax.broadcasted_iota(jnp.int32, (width, width), 0) // group
    j = lax.broadcasted_iota(jnp.int32, (width, width), 1) // group
    return (i == j).astype(F32)


def _group_sum(x, group):
    return jnp.dot(x, _head_ones(x.shape[1], group), precision=HI, preferred_element_type=F32)


def _spread(width_in, width_out, rep):
    i = lax.broadcasted_iota(jnp.int32, (width_in, width_out), 0)
    j = lax.broadcasted_iota(jnp.int32, (width_in, width_out), 1) // rep
    return (i == j).astype(F32)


def _hdot(a, b):
    return jnp.dot(a, b, precision=HI, preferred_element_type=F32)


def _sigmoid(x):
    return 1.0 / (1.0 + jnp.exp(-x))


def _softplus(x):
    return jnp.maximum(x, 0.0) + jnp.log(1.0 + jnp.exp(jnp.minimum(x, -x)))


def _silu(x):
    return x * _sigmoid(x)


def rwkv_pre(layer):
    def fn(*args):
        if layer == 0:
            fs, w0, w2p, a0, a2p, g2p, k_k, k_a = args
        else:
            fs, vfirst, w0, w2p, a0, a2p, g2p, k_k, k_a, v0, v2p = args
        r, k, v = fs[:, 0:256], fs[:, 256:512], fs[:, 512:768]
        lora = fs[:, 768:896]
        w_log = -_softplus(-(w0 + _hdot(jnp.tanh(lora), w2p))) - 0.5
        decay = jnp.exp(-jnp.exp(w_log))
        a = _sigmoid(a0 + _hdot(lora, a2p))
        g = _hdot(_sigmoid(lora), g2p)
        if layer > 0:
            v = v + (vfirst - v) * _sigmoid(v0 + _hdot(fs[:, 896:1024], v2p))
        kk = k * k_k
        kk = kk / jnp.maximum(jnp.sqrt(_group_sum(kk * kk, HEAD_DIM)), 1e-12)
        k = k * (1.0 + (a - 1.0) * k_a)
        return r, decay, k, v, kk, a, g
    return fn


def rwkv_post(y, r, k, v, g, lnx_w, lnx_b, r_k):
    mu = _group_sum(y, HEAD_DIM) * (1.0 / HEAD_DIM)
    yc = y - mu
    var = _group_sum(yc * yc, HEAD_DIM) * (1.0 / HEAD_DIM)
    yn = yc * lax.rsqrt(var + RWKV_GN_EPS) * lnx_w + lnx_b
    bonus = _group_sum(r * k * r_k, HEAD_DIM) * v
    return ((yn + bonus) * g,)


def attn_combine(o1, o2, o3, l1, l2, l3):
    m = jnp.maximum(jnp.maximum(l1, l2), l3)
    e1, e2, e3 = jnp.exp(l1 - m), jnp.exp(l2 - m), jnp.exp(l3 - m)
    return ((o1 * e1 + o2 * e2 + o3 * e3) / (e1 + e2 + e3),)


def ssd_pre(xc, dtr, conv_b, dt_bias, a_log):
    xbc = _silu(xc + conv_b)
    xs, bm, cm = xbc[:, 0:256], xbc[:, 256:512], xbc[:, 512:768]
    dt = _softplus(dtr + dt_bias)
    a_neg = -jnp.exp(a_log)
    wide = _spread(128, N_HEADS * SSD_STATE, SSD_STATE)
    w = _hdot(dt, wide) * _hdot(a_neg, wide)
    xdt = xs * _hdot(dt, _spread(128, D_GROUP, HEAD_DIM))
    rr = jnp.concatenate([cm[:, 0:128], cm[:, 0:128], cm[:, 128:256], cm[:, 128:256]], axis=1)
    kk = jnp.concatenate([bm[:, 0:128], bm[:, 0:128], bm[:, 128:256], bm[:, 128:256]], axis=1)
    return rr, w, kk, xdt, xs


def ssd_post(ys, z, xs, d_skip, norm_w):
    y = ys + xs * _hdot(d_skip, _spread(128, D_GROUP, HEAD_DIM))
    y = y * _silu(z)
    half = D_GROUP // 2
    parts = []
    for g in range(2):
        t = y[:, g * half:(g + 1) * half]
        parts.append(t * lax.rsqrt(jnp.mean(t * t, axis=-1, keepdims=True) + RMS_EPS))
    return (jnp.concatenate(parts, axis=1) * norm_w,)


def hgrn_pre(seg, lb):
    q, f, i = seg[:, 0:256], seg[:, 256:512], seg[:, 512:768]
    forget = lb + (1.0 - lb) * _sigmoid(f)
    return _silu(q), 1.0 - forget, jnp.log(forget), i


def hgrn_post(o, seg, norm_w):
    g = seg[:, 768:1024]
    ms = _group_sum(o * o, HEAD_DIM) * (1.0 / HEAD_DIM)
    return (o * lax.rsqrt(ms + RMS_EPS) * norm_w * _silu(g),)


def ln_res(x, y, w, b):
    z = ALPHA * x + y
    mu = jnp.mean(z, axis=-1, keepdims=True)
    zc = z - mu
    var = jnp.mean(zc * zc, axis=-1, keepdims=True)
    return (zc * lax.rsqrt(var + LN_EPS) * w + b,)


def relu2(u):
    r = jnp.maximum(u, 0.0)
    return (r * r,)


def loss_and_grad(y, tgt, name):
    S, D = y.shape
    tile = ROW_TILE

    def body(y_ref, t_ref, l_ref, dy_ref):
        e = y_ref[...] - t_ref[...]
        dy_ref[...] = e * (1.0 / D)

        @pl.when(pl.program_id(0) == 0)
        def _():
            l_ref[...] = jnp.zeros_like(l_ref)

        per_row = 0.5 * jnp.mean(e * e, axis=-1, keepdims=True)
        l_ref[...] += jnp.sum(per_row, axis=0, keepdims=True) * jnp.ones((1, 128), F32)

    return pl.pallas_call(body, grid=(S // tile,), in_specs=[_row_spec(D, tile)] * 2,
                          out_specs=[_par_spec((1, 128)), _row_spec(D, tile)],
                          out_shape=[jax.ShapeDtypeStruct((1, 128), F32), jax.ShapeDtypeStruct((S, D), F32)],
                          compiler_params=_cparams(("arbitrary",)), name=name)(y, tgt)


def add_rows(arrs, name):
    (out,) = tl_fwd(lambda *a: (functools.reduce(lambda p, q: p + q, a),), name, arrs, [], [arrs[0].shape[1]])
    return out


def small_fwd(fn, name, ins, out_shapes):
    n = len(ins)

    def body(*refs):
        outs = fn(*[r[...] for r in refs[:n]])
        for o_ref, o in zip(refs[n:], outs):
            o_ref[...] = o

    return pl.pallas_call(body, out_shape=[jax.ShapeDtypeStruct(s, F32) for s in out_shapes], name=name)(*ins)


def small_bwd(fn, name, ins, cts):
    n, m = len(ins), len(cts)

    def body(*refs):
        _, vjp = jax.vjp(lambda *a: tuple(fn(*a)), *[r[...] for r in refs[:n]])
        grads = vjp(tuple(r[...] for r in refs[n:n + m]))
        for o_ref, g in zip(refs[n + m:], grads):
            o_ref[...] = g

    return pl.pallas_call(body, out_shape=[jax.ShapeDtypeStruct(a.shape, F32) for a in ins], name=name)(*ins, *cts)


def param_prep(lower_bounds, mu0, mu1):
    e = jnp.exp(lower_bounds - jnp.max(lower_bounds, axis=0, keepdims=True))
    sm = e / jnp.sum(e, axis=0, keepdims=True)
    lb0 = sm[0:1] - sm[0:1]
    lb1 = sm[0:1] + sm[1:2] - sm[0:1]
    return lb0, lb1, mu0, 1.0 - mu0, mu1, 1.0 - mu1


def _rows_tile(rows):
    return _pick(rows, (256, 128, 64, 32, 16, 8))


def sum_parts(parts, name):
    P, rows, cols = parts.shape
    tile = _rows_tile(rows)

    def body(p_ref, o_ref):
        acc = p_ref[0]
        for p in range(1, P):
            acc = acc + p_ref[p]
        o_ref[...] = acc

    return pl.pallas_call(body, grid=(rows // tile,), in_specs=[pl.BlockSpec((P, tile, cols), lambda i: (0, i, 0))],
                          out_specs=pl.BlockSpec((tile, cols), lambda i: (i, 0)),
                          out_shape=jax.ShapeDtypeStruct((rows, cols), F32),
                          compiler_params=_cparams(("parallel",)), name=name)(parts)


def sum_slots(slots, own, ids, name):
    P, rows, cols = slots.shape
    tile = _rows_tile(rows)

    def body(ids_ref, s_ref, own_ref, o_ref):
        me = ids_ref[2]
        acc = None
        for p in range(P):
            term = jnp.where(me == p, own_ref[0, 0], s_ref[p].astype(F32))
            acc = term if acc is None else acc + term
        o_ref[...] = acc

    grid_spec = pltpu.PrefetchScalarGridSpec(
        num_scalar_prefetch=1, grid=(rows // tile,),
        in_specs=[pl.BlockSpec((P, tile, cols), lambda i, ids: (0, i, 0)),
                  pl.BlockSpec((1, 1, tile, cols), lambda i, ids: (ids[0], ids[1], i, 0))],
        out_specs=pl.BlockSpec((tile, cols), lambda i, ids: (i, 0)))
    return pl.pallas_call(body, grid_spec=grid_spec, out_shape=jax.ShapeDtypeStruct((rows, cols), F32),
                          compiler_params=_cparams(("parallel",)), name=name)(ids, slots, own)


def adamw(w, g, m, v, name):
    rows, cols = w.shape
    tile = _rows_tile(rows)

    def body(w_ref, g_ref, m_ref, v_ref, d_ref, nm_ref, nv_ref):
        gv = g_ref[...]
        nm = ADAM_B1 * m_ref[...] + (1.0 - ADAM_B1) * gv
        nv = ADAM_B2 * v_ref[...] + (1.0 - ADAM_B2) * jnp.square(gv)
        m_hat = nm / (1.0 - ADAM_B1 ** ADAM_STEP)
        v_hat = nv / (1.0 - ADAM_B2 ** ADAM_STEP)
        d_ref[...] = -ADAM_LR * (m_hat / (jnp.sqrt(v_hat) + ADAM_EPS) + ADAM_WD * w_ref[...])
        nm_ref[...] = nm
        nv_ref[...] = nv

    spec = pl.BlockSpec((tile, cols), lambda i: (i, 0))
    return pl.pallas_call(body, grid=(rows // tile,), in_specs=[spec] * 4, out_specs=[spec] * 3,
                          out_shape=[jax.ShapeDtypeStruct((rows, cols), F32)] * 3,
                          compiler_params=_cparams(("parallel",)), name=name)(w, g, m, v)


MESH = pl.DeviceIdType.MESH
ANY = pl.BlockSpec(memory_space=pl.ANY)


def _flip(v, bit):
    return 1 - v if bit else v


_CHIP_RELATIONS = ((1, 0), (0, 1), (1, 1))


def gather_chips(arrs, small, name):
    n = len(arrs)

    def body(*refs):
        ins, small_in = refs[:n], refs[n]
        outs, small_out = refs[n + 1:2 * n + 1], refs[2 * n + 1]
        send, recv, fsend, frecv, loc, ssend, srecv = refs[2 * n + 2:]
        x, y, c = lax.axis_index("x"), lax.axis_index("y"), lax.axis_index("c")
        me = 2 * x + y
        chips = [(_flip(x, bx), _flip(y, by)) for bx, by in _CHIP_RELATIONS]

        def over_ici(i, r, block_chip):
            return pltpu.make_async_remote_copy(src_ref=ins[i].at[c], dst_ref=outs[i].at[block_chip, c],
                                                send_sem=send.at[i, r], recv_sem=recv.at[i, r],
                                                device_id=(chips[r][0], chips[r][1], c), device_id_type=MESH)

        def to_sibling(i, r, layer):
            blk = outs[i].at[2 * chips[r][0] + chips[r][1], layer]
            return pltpu.make_async_remote_copy(src_ref=blk, dst_ref=blk, send_sem=fsend.at[i, r],
                                                recv_sem=frecv.at[i, r], device_id=(x, y, 1 - c), device_id_type=MESH)

        local = [pltpu.make_async_copy(ins[i], outs[i].at[me], loc.at[i]) for i in range(n)]
        local.append(pltpu.make_async_copy(small_in, small_out.at[me], loc.at[n]))
        for cp in local:
            cp.start()
        first = [over_ici(i, r, me) for i in range(n) for r in range(3)]
        smalls = [pltpu.make_async_remote_copy(src_ref=small_in, dst_ref=small_out.at[me], send_sem=ssend.at[r],
                                               recv_sem=srecv.at[r], device_id=(chips[r][0], chips[r][1], c),
                                               device_id_type=MESH) for r in range(3)]
        for cp in first + smalls:
            cp.start()
        passed = []
        for r in range(3):
            for i in range(n):
                over_ici(i, r, 2 * chips[r][0] + chips[r][1]).wait_recv()
                fw = to_sibling(i, r, c)
                fw.start()
                passed.append(fw)
        for r in range(3):
            for i in range(n):
                to_sibling(i, r, 1 - c).wait_recv()
        for cp in first + passed:
            cp.wait_send()
        for cp in smalls + local:
            cp.wait()

    return pl.pallas_call(
        body, in_specs=[ANY] * (n + 1), out_specs=[ANY] * (n + 1),
        out_shape=[jax.ShapeDtypeStruct((4,) + a.shape, a.dtype) for a in arrs]
        + [jax.ShapeDtypeStruct((4,) + small.shape, small.dtype)],
        scratch_shapes=[pltpu.SemaphoreType.DMA((n, 3)), pltpu.SemaphoreType.DMA((n, 3)), pltpu.SemaphoreType.DMA((n, 3)),
                        pltpu.SemaphoreType.DMA((n, 3)), pltpu.SemaphoreType.DMA((n + 1,)),
                        pltpu.SemaphoreType.DMA((3,)), pltpu.SemaphoreType.DMA((3,))],
        name=name)(*arrs, small)


_RELATIONS = tuple((r >> 2 & 1, r >> 1 & 1, r & 1) for r in range(1, 8))


def gather_devices(arr, name):
    def body(in_ref, out_ref, send, recv, loc):
        x, y, c = lax.axis_index("x"), lax.axis_index("y"), lax.axis_index("c")
        me = 4 * x + 2 * y + c
        lc = pltpu.make_async_copy(in_ref, out_ref.at[me], loc)
        lc.start()
        pending = [lc]
        for r, (bx, by, bc) in enumerate(_RELATIONS):
            cp = pltpu.make_async_remote_copy(src_ref=in_ref, dst_ref=out_ref.at[me], send_sem=send.at[r],
                                              recv_sem=recv.at[r], device_id=(_flip(x, bx), _flip(y, by), _flip(c, bc)),
                                              device_id_type=MESH)
            cp.start()
            pending.append(cp)
        for cp in pending:
            cp.wait()

    return pl.pallas_call(
        body, in_specs=[ANY], out_specs=ANY, out_shape=jax.ShapeDtypeStruct((8,) + arr.shape, arr.dtype),
        scratch_shapes=[pltpu.SemaphoreType.DMA((7,)), pltpu.SemaphoreType.DMA((7,)), pltpu.SemaphoreType.DMA(())],
        name=name)(arr)


def scatter_partials(arrs, name):
    n = len(arrs)

    def body(*refs):
        ins, outs = refs[:n], refs[n:2 * n]
        send, recv, loc = refs[2 * n:]
        x, y, c = lax.axis_index("x"), lax.axis_index("y"), lax.axis_index("c")
        me = 4 * x + 2 * y + c
        pending = []
        for i in range(n):
            lc = pltpu.make_async_copy(ins[i].at[c, 2 * x + y], outs[i].at[me], loc.at[i])
            lc.start()
            pending.append(lc)
            for r, (bx, by, bc) in enumerate(_RELATIONS):
                px, py, pc = _flip(x, bx), _flip(y, by), _flip(c, bc)
                cp = pltpu.make_async_remote_copy(src_ref=ins[i].at[pc, 2 * px + py], dst_ref=outs[i].at[me],
                                                  send_sem=send.at[i, r], recv_sem=recv.at[i, r],
                                                  device_id=(px, py, pc), device_id_type=MESH)
                cp.start()
                pending.append(cp)
        for cp in pending:
            cp.wait()

    return pl.pallas_call(
        body, in_specs=[ANY] * n, out_specs=[ANY] * n,
        out_shape=[jax.ShapeDtypeStruct((8,) + a.shape[2:], a.dtype) for a in arrs],
        scratch_shapes=[pltpu.SemaphoreType.DMA((n, 7)), pltpu.SemaphoreType.DMA((n, 7)), pltpu.SemaphoreType.DMA((n,))],
        name=name)(*arrs)


EXCHANGE_PIECES = 8


def sibling_exchange(arrs, name):
    n = len(arrs)

    def body(*refs):
        ins, outs = refs[:n], refs[n:2 * n]
        send, recv, loc = refs[2 * n:]
        x, y, c = lax.axis_index("x"), lax.axis_index("y"), lax.axis_index("c")
        pending = []
        for i in range(n):
            lc = pltpu.make_async_copy(ins[i], outs[i].at[c], loc.at[i])
            lc.start()
            pending.append(lc)
            rows = ins[i].shape[0] // EXCHANGE_PIECES
            for j in range(EXCHANGE_PIECES):
                piece = pl.ds(j * rows, rows)
                cp = pltpu.make_async_remote_copy(src_ref=ins[i].at[piece], dst_ref=outs[i].at[c, piece],
                                                  send_sem=send.at[i, j], recv_sem=recv.at[i, j],
                                                  device_id=(x, y, 1 - c), device_id_type=MESH)
                cp.start()
                pending.append(cp)
        for cp in pending:
            cp.wait()

    return pl.pallas_call(
        body, in_specs=[ANY] * n, out_specs=[ANY] * n,
        out_shape=[jax.ShapeDtypeStruct((2,) + a.shape, a.dtype) for a in arrs],
        scratch_shapes=[pltpu.SemaphoreType.DMA((n, EXCHANGE_PIECES)), pltpu.SemaphoreType.DMA((n, EXCHANGE_PIECES)),
                        pltpu.SemaphoreType.DMA((n,))],
        name=name)(*arrs)


def rwkv_fwd(l, seg, taps, pars, vfirst):
    fs = fir_fwd(seg, taps, f"rwkv_shift_fwd{l}")
    rows = [fs] + ([vfirst] if l else [])
    R, W, K, V, KK, A, G = tl_fwd(rwkv_pre(l), f"rwkv_pre_fwd{l}", rows, pars["pre"], [D_GROUP] * 7)
    Y, sb = run_scan_fwd(R, W, K, V, KK, A, HEAD_DIM, f"rwkv_scan_fwd{l}")
    (out,) = tl_fwd(rwkv_post, f"rwkv_post_fwd{l}", [Y, R, K, V, G], pars["post"], [D_GROUP])
    return out, V, (seg, taps, rows, R, W, K, V, KK, A, G, Y, sb)


def rwkv_bwd(l, saved, pars, dout, dv_extra):
    seg, taps, rows, R, W, K, V, KK, A, G, Y, sb = saved
    (dY, dR1, dK1, dV1, dG), dpost = tl_bwd(rwkv_post, f"rwkv_post_bwd{l}", [Y, R, K, V, G], pars["post"], [[dout]])
    (dR2, dW, dK2, dKK, dA), dV2 = run_scan_bwd(sb, dY, f"rwkv_scan_bwd{l}")
    cts = [[dR1, dR2], [dW], [dK1, dK2], [dV1, dV2] + dv_extra, [dKK], [dA], [dG]]
    drows, dpre = tl_bwd(rwkv_pre(l), f"rwkv_pre_bwd{l}", rows, pars["pre"], cts)
    dseg, dtaps, _ = fir_bwd(seg, taps, [drows[0]], f"rwkv_shift_bwd{l}")
    return dseg, (drows[1] if l else None), dtaps, dpre, dpost


def attn_mix_fwd(l, q, k, v):
    subs, os_, ls_ = [], [], []
    for b, (_, d) in enumerate(DILATED_BRANCHES):
        qs, ks, vs = _to_sub(q, d), _to_sub(k, d), _to_sub(v, d)
        o, lse = attn_fwd(qs, ks, vs, d, f"attn_fwd{l}_{b}")
        subs.append((qs, ks, vs))
        os_.append(_from_sub(o))
        ls_.append(_from_sub(lse))
    (out,) = tl_fwd(attn_combine, f"attn_combine_fwd{l}", os_ + ls_, [], [D_GROUP])
    return out, (subs, os_, ls_)


def attn_mix_bwd(l, saved, dout):
    subs, os_, ls_ = saved
    drows, _ = tl_bwd(attn_combine, f"attn_combine_bwd{l}", os_ + ls_, [], [[dout]])
    dqs, dks, dvs = [], [], []
    for b, (_, d) in enumerate(DILATED_BRANCHES):
        qs, ks, vs = subs[b]
        dq, dk, dv = attn_bwd(qs, ks, vs, _to_sub(drows[b], d), _to_sub(drows[3 + b], d), d, f"attn_bwd{l}_{b}")
        dqs.append(_from_sub(dq))
        dks.append(_from_sub(dk))
        dvs.append(_from_sub(dv))
    return add_rows(dqs, f"attn_dq{l}"), add_rows(dks, f"attn_dk{l}"), add_rows(dvs, f"attn_dv{l}")


def ssd_fwd(l, z, xbc, dtr, pars):
    xc = fir_fwd(xbc, pars["taps"], f"ssd_conv_fwd{l}")
    rr, w, kk, xdt, xs = tl_fwd(ssd_pre, f"ssd_pre_fwd{l}", [xc, dtr], pars["pre"], [512, 512, 512, D_GROUP, D_GROUP])
    blocks = [_to_heads(rr, SSD_STATE), _to_heads(kk, SSD_STATE), _to_heads(w, SSD_STATE), _to_heads(xdt, HEAD_DIM)]
    yh, states = chunk_fwd(ssd_chunk, f"ssd_scan_fwd{l}", blocks, (HEAD_DIM, SSD_STATE), HEAD_DIM)
    ys = _from_heads(yh)
    (out,) = tl_fwd(ssd_post, f"ssd_post_fwd{l}", [ys, z, xs], pars["post"], [D_GROUP])
    return out, (z, xbc, dtr, xc, blocks, states, xs, ys)


def ssd_bwd(l, saved, pars, dout):
    z, xbc, dtr, xc, blocks, states, xs, ys = saved
    (dys, dz, dxs), dpost = tl_bwd(ssd_post, f"ssd_post_bwd{l}", [ys, z, xs], pars["post"], [[dout]])
    drr, dkk, dw, dxdt = (_from_heads(g) for g in
                          chunk_bwd(ssd_chunk, f"ssd_scan_bwd{l}", blocks, states, _to_heads(dys, HEAD_DIM)))
    (dxc, ddtr), dpre = tl_bwd(ssd_pre, f"ssd_pre_bwd{l}", [xc, dtr], pars["pre"], [[drr], [dw], [dkk], [dxdt], [dxs]])
    dxbc, dtaps, _ = fir_bwd(xbc, pars["taps"], [dxc], f"ssd_conv_bwd{l}")
    return dz, dxbc, ddtr, dtaps, dpre, dpost


def hgrn_fwd(l, seg, pars):
    q, kk, lf, i = tl_fwd(hgrn_pre, f"hgrn_pre_fwd{l}", [seg], pars["pre"], [D_GROUP] * 4)
    blocks = [_to_heads(t, HEAD_DIM) for t in (q, kk, lf, i)]
    oh, states = chunk_fwd(hgrn_chunk, f"hgrn_scan_fwd{l}", blocks, (HEAD_DIM, HEAD_DIM), HEAD_DIM)
    o = _from_heads(oh)
    (out,) = tl_fwd(hgrn_post, f"hgrn_post_fwd{l}", [o, seg], pars["post"], [D_GROUP])
    return out, (seg, blocks, states, o)


def hgrn_bwd(l, saved, pars, dout):
    seg, blocks, states, o = saved
    (do, dseg1), dpost = tl_bwd(hgrn_post, f"hgrn_post_bwd{l}", [o, seg], pars["post"], [[dout]])
    dq, dkk, dlf, di = (_from_heads(g) for g in
                        chunk_bwd(hgrn_chunk, f"hgrn_scan_bwd{l}", blocks, states, _to_heads(do, HEAD_DIM)))
    (dseg2,), dpre = tl_bwd(hgrn_pre, f"hgrn_pre_bwd{l}", [seg], pars["pre"], [[dq], [dkk], [dlf], [di]])
    return add_rows([dseg1, dseg2], f"hgrn_dseg{l}"), dpre, dpost


def layer_fwd(l, x, wts, pars, vfirst):
    proj = matmul(x, wts["in"], "nn", f"proj_fwd{l}")
    seg_h = proj[:, SEG_HGRN:SEG_HGRN + 1024]
    seg_r = proj[:, SEG_RWKV:SEG_RWKV + 1024]
    q, k, v = (proj[:, SEG_Q + j * D_GROUP:SEG_Q + (j + 1) * D_GROUP] for j in range(3))
    z = proj[:, SEG_Z:SEG_Z + D_GROUP]
    xbc = proj[:, SEG_XBC:SEG_XBC + SSD_XBC]
    dtr = proj[:, SEG_DT:SEG_DT + 128]
    ya, v_rwkv, sa = rwkv_fwd(l, seg_r, pars["rwkv"]["taps"], pars["rwkv"], vfirst)
    yb, sb = attn_mix_fwd(l, q, k, v)
    yc, sc = ssd_fwd(l, z, xbc, dtr, pars["ssd"])
    yd, sd = hgrn_fwd(l, seg_h, pars["hgrn"])
    mix = jnp.concatenate([ya, yb, yc, yd], axis=1)
    mo = matmul(mix, wts["out"], "nn", f"out_fwd{l}")
    (x1,) = tl_fwd(ln_res, f"ln1_fwd{l}", [x, mo], pars["ln1"], [D_MODEL])
    u = matmul(x1, wts["up"], "nn", f"up_fwd{l}")
    (h,) = tl_fwd(relu2, f"relu2_fwd{l}", [u], [], [D_FF])
    dn = matmul(h, wts["down"], "nn", f"down_fwd{l}")
    (x2,) = tl_fwd(ln_res, f"ln2_fwd{l}", [x1, dn], pars["ln2"], [D_MODEL])
    return x2, v_rwkv, (x, sa, sb, sc, sd, mix, mo, x1, u, h, dn)


def layer_bwd(l, saved, wts, pars, dx2, dv_extra):
    x, sa, sb, sc, sd, mix, mo, x1, u, h, dn = saved
    S = x.shape[0]
    g = {}
    (dx1a, ddn), g["ln2"] = tl_bwd(ln_res, f"ln2_bwd{l}", [x1, dn], pars["ln2"], [[dx2]])
    g["down"] = matmul(h, ddn, "tn", f"down_dw{l}")
    dh = matmul(ddn, wts["down"], "nt", f"down_dx{l}")
    (du,), _ = tl_bwd(relu2, f"relu2_bwd{l}", [u], [], [[dh]])
    g["up"] = matmul(x1, du, "tn", f"up_dw{l}")
    dx1 = matmul(du, wts["up"], "nt", f"up_dx{l}", add=dx1a)
    (dxa, dmo), g["ln1"] = tl_bwd(ln_res, f"ln1_bwd{l}", [x, mo], pars["ln1"], [[dx1]])
    g["out"] = matmul(mix, dmo, "tn", f"out_dw{l}")
    dmix = matmul(dmo, wts["out"], "nt", f"out_dx{l}")
    dya, dyb, dyc, dyd = (dmix[:, j * D_GROUP:(j + 1) * D_GROUP] for j in range(4))
    dseg_r, dvfirst, g["rwkv_taps"], g["rwkv_pre"], g["rwkv_post"] = rwkv_bwd(l, sa, pars["rwkv"], dya, dv_extra)
    dq, dk, dv = attn_mix_bwd(l, sb, dyb)
    dz, dxbc, ddtr, g["ssd_taps"], g["ssd_pre"], g["ssd_post"] = ssd_bwd(l, sc, pars["ssd"], dyc)
    dseg_h, g["hgrn_pre"], g["hgrn_post"] = hgrn_bwd(l, sd, pars["hgrn"], dyd)
    dproj = jnp.concatenate([dseg_h, dseg_r, dq, dk, dv, dz, dxbc, ddtr, jnp.zeros((S, PROJ_W - SEG_DT - 128), F32)], axis=1)
    g["in"] = matmul(x, dproj, "tn", f"proj_dw{l}")
    dx = matmul(dproj, wts["in"], "nt", f"proj_dx{l}", add=dxa)
    return dx, dvfirst, g


SMALL = ("lower_bounds", "w_in_vres", "mu_shift", "mu_vres", "rwkv_w0", "rwkv_w2", "rwkv_a0", "rwkv_a2", "rwkv_g2",
         "rwkv_k_k", "rwkv_k_a", "rwkv_r_k", "rwkv_lnx_w", "rwkv_lnx_b", "rwkv_v0", "rwkv_v2", "ssd_conv_w",
         "ssd_conv_b", "ssd_dt_bias", "ssd_A_log", "ssd_D", "ssd_norm_w", "hgrn_norm_w", "ln1_w", "ln1_b", "ln2_w", "ln2_b")
BIG = ("w_in", "w_out", "w_up", "w_down")
SMALL_SHARDED = {"w_in_vres": 1, "rwkv_w2": 2, "rwkv_a2": 2, "rwkv_g2": 2, "rwkv_v2": 2, "ssd_conv_w": 2}
WEIGHTS = ("lower_bounds", "w_in", "w_in_vres", "mu_shift", "mu_vres", "rwkv_w0", "rwkv_w2", "rwkv_a0", "rwkv_a2",
           "rwkv_g2", "rwkv_k_k", "rwkv_k_a", "rwkv_r_k", "rwkv_lnx_w", "rwkv_lnx_b", "rwkv_v0", "rwkv_v2",
           "ssd_conv_w", "ssd_conv_b", "ssd_dt_bias", "ssd_A_log", "ssd_D", "ssd_norm_w", "hgrn_norm_w", "w_out",
           "ln1_w", "ln1_b", "w_up", "w_down", "ln2_w", "ln2_b")


def _row(v, width=None):
    v = v.reshape(1, -1).astype(F32)
    if width is not None and v.shape[1] < width:
        v = jnp.pad(v, ((0, 0), (0, width - v.shape[1])))
    return v


def _rows_at(m, rows, at):
    return jnp.pad(m.astype(F32), ((at, rows - at - m.shape[0]), (0, 0)))


def _pad_w_in(w_in_l, vres):
    rows = w_in_l.shape[0]
    out = []
    order = sorted(_PIECES, key=lambda p: p[2])
    pos = 0
    for start, width, at in order:
        if at > pos:
            out.append(jnp.zeros((rows, at - pos), w_in_l.dtype))
        out.append(w_in_l[:, start:start + width])
        pos = at + width
        if at == SEG_RWKV and vres is not None:
            out.append(vres.astype(w_in_l.dtype))
            pos += vres.shape[1]
    out.append(jnp.zeros((rows, PROJ_W - pos), w_in_l.dtype))
    return jnp.concatenate(out, axis=1)


def _unpad_w_in(g):
    return jnp.concatenate([g[:, at:at + width] for _, width, at in _PIECES], axis=1)


def layer_params(l, sp, prep):
    lb, mu, om = prep[l], prep[2 + 2 * l], prep[3 + 2 * l]
    pre = [_row(sp["rwkv_w0"][l]), _rows_at(sp["rwkv_w2"][l], 128, 0), _row(sp["rwkv_a0"][l]),
           _rows_at(sp["rwkv_a2"][l], 128, 32), _rows_at(sp["rwkv_g2"][l], 128, 64),
           _row(sp["rwkv_k_k"][l]), _row(sp["rwkv_k_a"][l])]
    if l:
        pre += [_row(sp["rwkv_v0"][l - 1]), _rows_at(sp["rwkv_v2"][l - 1], 128, 0)]
    return {
        "rwkv": {"taps": jnp.concatenate([mu, om], axis=0), "pre": pre,
                 "post": [_row(sp["rwkv_lnx_w"][l]), _row(sp["rwkv_lnx_b"][l]), _row(sp["rwkv_r_k"][l])]},
        "ssd": {"taps": sp["ssd_conv_w"][l].astype(F32),
                "pre": [_row(sp["ssd_conv_b"][l]), _row(sp["ssd_dt_bias"][l], 128), _row(sp["ssd_A_log"][l], 128)],
                "post": [_row(sp["ssd_D"][l], 128), _row(sp["ssd_norm_w"][l])]},
        "hgrn": {"pre": [lb], "post": [_row(sp["hgrn_norm_w"][l])]},
        "ln1": [_row(sp["ln1_w"][l]), _row(sp["ln1_b"][l])],
        "ln2": [_row(sp["ln2_w"][l]), _row(sp["ln2_b"][l])],
    }


def _mu_full(sp, l):
    parts = [sp["mu_shift"][l].reshape(1, -1)]
    if l:
        parts.append(sp["mu_vres"][l - 1].reshape(1, -1))
    return _row(jnp.concatenate(parts, axis=1), 1024)


def local_step(x, target, big, sp):
    prep_in = [sp["lower_bounds"].astype(F32), _mu_full(sp, 0), _mu_full(sp, 1)]
    prep = small_fwd(param_prep, "param_prep_fwd", prep_in,
                     [(1, D_GROUP), (1, D_GROUP), (1, 1024), (1, 1024), (1, 1024), (1, 1024)])
    pars, wts = [], []
    for l in range(DEPTH):
        pars.append(layer_params(l, sp, prep))
        vres = sp["w_in_vres"][l - 1].astype(BF16) if l else None
        wts.append({"in": _pad_w_in(big["w_in"][l], vres), "out": big["w_out"][l], "up": big["w_up"][l],
                    "down": big["w_down"][l]})
    h, vfirst, saved = x, None, []
    for l in range(DEPTH):
        h, v_l, sv = layer_fwd(l, h, wts[l], pars[l], vfirst)
        vfirst = v_l if l == 0 else vfirst
        saved.append(sv)
    loss_row, dh = loss_and_grad(h, target, "loss")
    grads, dv_extra = [None] * DEPTH, []
    for l in reversed(range(DEPTH)):
        dh, dvfirst, grads[l] = layer_bwd(l, saved[l], wts[l], pars[l], dh, dv_extra)
        dv_extra = [dvfirst] if l else []
    cts = [grads[0]["hgrn_pre"][0], grads[1]["hgrn_pre"][0]]
    for l in range(DEPTH):
        cts += [grads[l]["rwkv_taps"][0:1], grads[l]["rwkv_taps"][1:2]]
    d_lower, d_mu0, d_mu1 = small_bwd(param_prep, "param_prep_bwd", prep_in, cts)
    d_mu = [d_mu0, d_mu1]
    gb = {"w_in": [_unpad_w_in(grads[l]["in"]) for l in range(DEPTH)], "w_out": [grads[l]["out"] for l in range(DEPTH)],
          "w_up": [grads[l]["up"] for l in range(DEPTH)], "w_down": [grads[l]["down"] for l in range(DEPTH)]}
    st = lambda f: jnp.stack([f(l) for l in range(DEPTH)])
    g1 = grads[1]
    gs = {
        "lower_bounds": d_lower,
        "w_in_vres": g1["in"][None, :, VRES_COL:VRES_COL + 32],
        "mu_shift": st(lambda l: d_mu[l][0, :896]),
        "mu_vres": d_mu[1][:, 896:928],
        "rwkv_w0": st(lambda l: grads[l]["rwkv_pre"][0][0]),
        "rwkv_w2": st(lambda l: grads[l]["rwkv_pre"][1][0:32]),
        "rwkv_a0": st(lambda l: grads[l]["rwkv_pre"][2][0]),
        "rwkv_a2": st(lambda l: grads[l]["rwkv_pre"][3][32:64]),
        "rwkv_g2": st(lambda l: grads[l]["rwkv_pre"][4][64:128]),
        "rwkv_k_k": st(lambda l: grads[l]["rwkv_pre"][5][0]),
        "rwkv_k_a": st(lambda l: grads[l]["rwkv_pre"][6][0]),
        "rwkv_r_k": st(lambda l: grads[l]["rwkv_post"][2].reshape(N_HEADS, HEAD_DIM)),
        "rwkv_lnx_w": st(lambda l: grads[l]["rwkv_post"][0][0]),
        "rwkv_lnx_b": st(lambda l: grads[l]["rwkv_post"][1][0]),
        "rwkv_v0": g1["rwkv_pre"][7],
        "rwkv_v2": g1["rwkv_pre"][8][None, 0:32],
        "ssd_conv_w": st(lambda l: grads[l]["ssd_taps"]),
        "ssd_conv_b": st(lambda l: grads[l]["ssd_pre"][0][0]),
        "ssd_dt_bias": st(lambda l: grads[l]["ssd_pre"][1][0, :N_HEADS]),
        "ssd_A_log": st(lambda l: grads[l]["ssd_pre"][2][0, :N_HEADS]),
        "ssd_D": st(lambda l: grads[l]["ssd_post"][0][0, :N_HEADS]),
        "ssd_norm_w": st(lambda l: grads[l]["ssd_post"][1][0]),
        "hgrn_norm_w": st(lambda l: grads[l]["hgrn_post"][0][0]),
        "ln1_w": st(lambda l: grads[l]["ln1"][0][0]),
        "ln1_b": st(lambda l: grads[l]["ln1"][1][0]),
        "ln2_w": st(lambda l: grads[l]["ln2"][0][0]),
        "ln2_b": st(lambda l: grads[l]["ln2"][1][0]),
    }
    return loss_row, dh, gb, gs


def _pack(vecs):
    flat, meta, pos = [], [], 0
    for v in vecs:
        flat.append(v.reshape(-1).astype(F32))
        meta.append((pos, v.shape))
        pos += v.size
    total = -(-pos // 1024) * 1024
    flat.append(jnp.zeros((total - pos,), F32))
    return jnp.concatenate(flat).reshape(total // 128, 128), meta


def _unpack(packed, meta):
    flat = packed.reshape(-1)
    return [flat[off:off + math.prod(shape)].reshape(shape) for off, shape in meta]


def _to_shards(name, g):
    if name == "w_in":
        return jnp.transpose(g.reshape(g.shape[0], 4, g.shape[1] // 4), (1, 0, 2))
    if name == "w_up":
        return jnp.transpose(g.reshape(g.shape[0], 4, g.shape[1] // 4), (1, 0, 2))
    return g.reshape(4, g.shape[0] // 4, g.shape[1])


def _from_chips(name, g):
    if name in ("w_in", "w_up"):
        return jnp.transpose(g, (1, 2, 0, 3)).reshape(g.shape[1], g.shape[2], 4 * g.shape[3])
    return jnp.transpose(g, (1, 0, 2, 3)).reshape(g.shape[1], 4 * g.shape[2], g.shape[3])


INPUT_NAMES = ("x",) + WEIGHTS + ("loss_target",) + tuple("m_" + n for n in WEIGHTS) + tuple("v_" + n for n in WEIGHTS)


def _step(*args):
    a = dict(zip(INPUT_NAMES, args, strict=True))
    chip = 2 * lax.axis_index("x") + lax.axis_index("y")

    sharded_names = list(SMALL_SHARDED)
    small_pack, small_meta = _pack([a[n] for n in sharded_names])
    gathered = gather_chips([a[n].astype(BF16) for n in BIG], small_pack, "gather_weights")
    big = {n: _from_chips(n, g) for n, g in zip(BIG, gathered)}
    sp = {n: a[n] for n in SMALL if n not in SMALL_SHARDED}
    per_chip = [_unpack(gathered[-1][s], small_meta) for s in range(4)]
    for j, n in enumerate(sharded_names):
        sp[n] = jnp.concatenate([per_chip[s][j] for s in range(4)], axis=SMALL_SHARDED[n])

    loss_row, gx, gb, gs = local_step(a["x"][0], a["loss_target"][0], big, sp)

    partials = [jnp.stack([_to_shards(n, gb[n][l]) for l in range(DEPTH)]) for n in BIG]
    slots = scatter_partials([p.astype(BF16) for p in partials], "reduce_big")
    core = lax.axis_index("c")
    ids = jnp.stack([core, chip, 2 * chip + core]).astype(jnp.int32)
    mine = [sum_slots(sl, p, ids, f"sum_{n}") for n, sl, p in zip(BIG, slots, partials)]
    summed = sibling_exchange(mine, "exchange_big")
    out_g, out_d, out_m, out_v = {}, {}, {}, {}
    for n, g in zip(BIG, summed):
        shape = a[n].shape
        flat = lambda t: t.reshape(shape[0] * shape[1], shape[2])
        d, nm, nv = adamw(flat(a[n]), flat(g), flat(a["m_" + n]), flat(a["v_" + n]), f"adamw_{n}")
        out_g[n], out_d[n], out_m[n], out_v[n] = g.reshape(shape), d.reshape(shape), nm.reshape(shape), nv.reshape(shape)

    vec, meta = _pack([loss_row] + [gs[n] for n in SMALL])
    total = sum_parts(gather_devices(vec, "gather_small"), "sum_small")
    parts = _unpack(total, meta)
    loss = parts[0][0, 0]
    g_small = {}
    for n, g in zip(SMALL, parts[1:]):
        if n in SMALL_SHARDED:
            ax = SMALL_SHARDED[n]
            size = a[n].shape[ax]
            g = lax.dynamic_slice_in_dim(g, chip * size, size, axis=ax)
        g_small[n] = g
    pw, pmeta = _pack([a[n] for n in SMALL])
    pg, _ = _pack([g_small[n] for n in SMALL])
    pm, _ = _pack([a["m_" + n] for n in SMALL])
    pv, _ = _pack([a["v_" + n] for n in SMALL])
    d, nm, nv = adamw(pw, pg, pm, pv, "adamw_small")
    for n, dd, mm, vv in zip(SMALL, _unpack(d, pmeta), _unpack(nm, pmeta), _unpack(nv, pmeta)):
        out_g[n], out_d[n], out_m[n], out_v[n] = g_small[n], dd, mm, vv

    return (loss, gx[None], *[out_g[n] for n in WEIGHTS], *[out_d[n] for n in WEIGHTS],
            *[out_m[n] for n in WEIGHTS], *[out_v[n] for n in WEIGHTS])


def kernel(x, lower_bounds, w_in, w_in_vres, mu_shift, mu_vres, rwkv_w0, rwkv_w2, rwkv_a0, rwkv_a2, rwkv_g2, rwkv_k_k, rwkv_k_a, rwkv_r_k, rwkv_lnx_w, rwkv_lnx_b, rwkv_v0, rwkv_v2, ssd_conv_w, ssd_conv_b, ssd_dt_bias, ssd_A_log, ssd_D, ssd_norm_w, hgrn_norm_w, w_out, ln1_w, ln1_b, w_up, w_down, ln2_w, ln2_b, loss_target, m_lower_bounds, m_w_in, m_w_in_vres, m_mu_shift, m_mu_vres, m_rwkv_w0, m_rwkv_w2, m_rwkv_a0, m_rwkv_a2, m_rwkv_g2, m_rwkv_k_k, m_rwkv_k_a, m_rwkv_r_k, m_rwkv_lnx_w, m_rwkv_lnx_b, m_rwkv_v0, m_rwkv_v2, m_ssd_conv_w, m_ssd_conv_b, m_ssd_dt_bias, m_ssd_A_log, m_ssd_D, m_ssd_norm_w, m_hgrn_norm_w, m_w_out, m_ln1_w, m_ln1_b, m_w_up, m_w_down, m_ln2_w, m_ln2_b, v_lower_bounds, v_w_in, v_w_in_vres, v_mu_shift, v_mu_vres, v_rwkv_w0, v_rwkv_w2, v_rwkv_a0, v_rwkv_a2, v_rwkv_g2, v_rwkv_k_k, v_rwkv_k_a, v_rwkv_r_k, v_rwkv_lnx_w, v_rwkv_lnx_b, v_rwkv_v0, v_rwkv_v2, v_ssd_conv_w, v_ssd_conv_b, v_ssd_dt_bias, v_ssd_A_log, v_ssd_D, v_ssd_norm_w, v_hgrn_norm_w, v_w_out, v_ln1_w, v_ln1_b, v_w_up, v_w_down, v_ln2_w, v_ln2_b):
    return _step(x, lower_bounds, w_in, w_in_vres, mu_shift, mu_vres, rwkv_w0, rwkv_w2, rwkv_a0, rwkv_a2, rwkv_g2, rwkv_k_k, rwkv_k_a, rwkv_r_k, rwkv_lnx_w, rwkv_lnx_b, rwkv_v0, rwkv_v2, ssd_conv_w, ssd_conv_b, ssd_dt_bias, ssd_A_log, ssd_D, ssd_norm_w, hgrn_norm_w, w_out, ln1_w, ln1_b, w_up, w_down, ln2_w, ln2_b, loss_target, m_lower_bounds, m_w_in, m_w_in_vres, m_mu_shift, m_mu_vres, m_rwkv_w0, m_rwkv_w2, m_rwkv_a0, m_rwkv_a2, m_rwkv_g2, m_rwkv_k_k, m_rwkv_k_a, m_rwkv_r_k, m_rwkv_lnx_w, m_rwkv_lnx_b, m_rwkv_v0, m_rwkv_v2, m_ssd_conv_w, m_ssd_conv_b, m_ssd_dt_bias, m_ssd_A_log, m_ssd_D, m_ssd_norm_w, m_hgrn_norm_w, m_w_out, m_ln1_w, m_ln1_b, m_w_up, m_w_down, m_ln2_w, m_ln2_b, v_lower_bounds, v_w_in, v_w_in_vres, v_mu_shift, v_mu_vres, v_rwkv_w0, v_rwkv_w2, v_rwkv_a0, v_rwkv_a2, v_rwkv_g2, v_rwkv_k_k, v_rwkv_k_a, v_rwkv_r_k, v_rwkv_lnx_w, v_rwkv_lnx_b, v_rwkv_v0, v_rwkv_v2, v_ssd_conv_w, v_ssd_conv_b, v_ssd_dt_bias, v_ssd_A_log, v_ssd_D, v_ssd_norm_w, v_hgrn_norm_w, v_w_out, v_ln1_w, v_ln1_b, v_w_up, v_w_down, v_ln2_w, v_ln2_b)
```

```python
import functools
import math

import jax
import jax.numpy as jnp
from jax import lax
from jax.experimental import pallas as pl
from jax.experimental.pallas import tpu as pltpu

F32 = jnp.float32
BF16 = jnp.bfloat16
HI = lax.Precision.HIGHEST

DEPTH = 2
D_MODEL = 1024
D_GROUP = 256
HEAD_DIM = 64
N_HEADS = 4
SSD_STATE = 128
SSD_XBC = 768
SSD_CONV = 4
D_FF = 4096
ALPHA = (2.0 * DEPTH) ** 0.25
LN_EPS = 1e-5
RMS_EPS = 1e-5
RWKV_GN_EPS = HEAD_DIM * 1e-5
DILATED_BRANCHES = ((128, 1), (512, 4), (2048, 16))
ALIBI_SLOPES = tuple(2.0 ** (-8.0 * (h + 1) / N_HEADS) for h in range(N_HEADS))
ATTN_BLK = 128

ADAM_LR, ADAM_B1, ADAM_B2, ADAM_EPS, ADAM_WD, ADAM_STEP = 0.001, 0.9, 0.999, 1e-08, 0.01, 10

IN_COLS = 3716
PROJ_W = 4096
SEG_HGRN, SEG_RWKV, SEG_Q, SEG_Z, SEG_XBC, SEG_DT = 0, 1024, 2048, 2816, 3072, 3840
_PIECES = ((0, 896, SEG_RWKV), (896, 768, SEG_Q), (1664, 256, SEG_Z), (1920, 768, SEG_XBC),
           (2688, 4, SEG_DT), (2692, 1024, SEG_HGRN))
VRES_COL = SEG_RWKV + 896

HM64 = (N_HEADS, HEAD_DIM)
HM128 = (N_HEADS, SSD_STATE)
ROW_TILE = 256
SCAN_CHUNK = 128
VMEM_LIMIT = 48 * 1024 * 1024


def _cparams(sem=None):
    if sem is None:
        return pltpu.CompilerParams(vmem_limit_bytes=VMEM_LIMIT)
    return pltpu.CompilerParams(dimension_semantics=sem, vmem_limit_bytes=VMEM_LIMIT)


def _pick(n, pref):
    for t in pref:
        if n % t == 0:
            return t
    return n


def matmul(a, b, mode, name, add=None, out_dtype=F32):
    if mode == "nn":
        (M, K), (_, N) = a.shape, b.shape
    elif mode == "nt":
        (M, K), (N, _) = a.shape, b.shape
    else:
        (K, M), (_, N) = a.shape, b.shape
    tm, tn, tk = _pick(M, (1024, 512, 256, 128)), _pick(N, (1024, 512, 256, 128)), _pick(K, (512, 256, 128))
    nk = K // tk
    dims = {"nn": (((1,), (0,)), ((), ())), "nt": (((1,), (1,)), ((), ())), "tn": (((0,), (0,)), ((), ()))}[mode]

    def body(*refs):
        if add is None:
            a_ref, b_ref, o_ref, acc = refs
            add_ref = None
        else:
            a_ref, b_ref, add_ref, o_ref, acc = refs
        k = pl.program_id(2)

        @pl.when(k == 0)
        def _():
            acc[...] = jnp.zeros_like(acc)

        acc[...] += lax.dot_general(a_ref[...].astype(BF16), b_ref[...].astype(BF16), dims,
                                    preferred_element_type=F32)

        @pl.when(k == nk - 1)
        def _():
            r = acc[...]
            if add_ref is not None:
                r = r + add_ref[...]
            o_ref[...] = r.astype(o_ref.dtype)

    a_spec = pl.BlockSpec((tk, tm), lambda i, j, k: (k, i)) if mode == "tn" else pl.BlockSpec((tm, tk), lambda i, j, k: (i, k))
    b_spec = pl.BlockSpec((tn, tk), lambda i, j, k: (j, k)) if mode == "nt" else pl.BlockSpec((tk, tn), lambda i, j, k: (k, j))
    o_spec = pl.BlockSpec((tm, tn), lambda i, j, k: (i, j))
    ins, specs = [a, b], [a_spec, b_spec]
    if add is not None:
        ins.append(add)
        specs.append(o_spec)
    return pl.pallas_call(
        body, grid=(M // tm, N // tn, nk), in_specs=specs, out_specs=o_spec,
        out_shape=jax.ShapeDtypeStruct((M, N), out_dtype), scratch_shapes=[pltpu.VMEM((tm, tn), F32)],
        compiler_params=_cparams(("parallel", "parallel", "arbitrary")), name=name)(*ins)


def _row_spec(w, tile):
    return pl.BlockSpec((tile, w), lambda i: (i, 0))


def _par_spec(shape):
    return pl.BlockSpec(shape, lambda i: (0,) * len(shape))


def _rows_spec(shape, tile):
    if len(shape) == 3:
        return pl.BlockSpec((shape[0], tile, shape[2]), lambda i: (0, i, 0))
    return _row_spec(shape[1], tile)


def _rows_shape(S, w):
    return (w[0], S, w[1]) if isinstance(w, tuple) else (S, w)


def _rows_load(ref):
    if len(ref.shape) == 3:
        return jnp.concatenate([ref[h] for h in range(ref.shape[0])], axis=1)
    return ref[...]


def _rows_store(ref, val):
    if len(ref.shape) == 3:
        w = ref.shape[2]
        for h in range(ref.shape[0]):
            ref[h] = val[:, h * w:(h + 1) * w]
    else:
        ref[...] = val


def tl_fwd(fn, name, rows, pars, out_widths, tile=ROW_TILE):
    S = rows[0].shape[-2]
    nr = len(rows)

    def body(*refs):
        ins = [_rows_load(r) for r in refs[:nr]] + [r[...] for r in refs[nr:nr + len(pars)]]
        outs = fn(*ins)
        for o_ref, o in zip(refs[nr + len(pars):], outs):
            _rows_store(o_ref, o)

    shapes = [_rows_shape(S, w) for w in out_widths]
    return pl.pallas_call(
        body, grid=(S // tile,),
        in_specs=[_rows_spec(r.shape, tile) for r in rows] + [_par_spec(p.shape) for p in pars],
        out_specs=[_rows_spec(s, tile) for s in shapes],
        out_shape=[jax.ShapeDtypeStruct(s, F32) for s in shapes],
        compiler_params=_cparams(("parallel",)), name=name)(*rows, *pars)


def tl_bwd(fn, name, rows, pars, cts, tile=ROW_TILE, row_grad=None):
    S = rows[0].shape[-2]
    nr, npar = len(rows), len(pars)
    row_grad = [True] * nr if row_grad is None else row_grad
    flat_cts = [c for group in cts for c in group]
    ncts = len(flat_cts)
    gi = [i for i in range(nr) if row_grad[i]]

    def body(*refs):
        row_v = [_rows_load(r) for r in refs[:nr]]
        par_v = [r[...] for r in refs[nr:nr + npar]]
        ct_refs = refs[nr + npar:nr + npar + ncts]
        out_refs = refs[nr + npar + ncts:]
        ct_v, pos = [], 0
        for group in cts:
            acc = _rows_load(ct_refs[pos])
            for q in range(1, len(group)):
                acc = acc + _rows_load(ct_refs[pos + q])
            pos += len(group)
            ct_v.append(acc)

        def f(diff_rows, par_vals):
            full = list(row_v)
            for idx, val in zip(gi, diff_rows):
                full[idx] = val
            return tuple(fn(*full, *par_vals))

        _, vjp = jax.vjp(f, [row_v[i] for i in gi], par_v)
        d_rows, d_pars = vjp(tuple(ct_v))
        for o_ref, g in zip(out_refs[:len(gi)], d_rows):
            _rows_store(o_ref, g)
        first = pl.program_id(0) == 0
        for o_ref, g in zip(out_refs[len(gi):], d_pars):
            @pl.when(first)
            def _(o_ref=o_ref):
                o_ref[...] = jnp.zeros_like(o_ref)
            o_ref[...] += g

    outs = pl.pallas_call(
        body, grid=(S // tile,),
        in_specs=[_rows_spec(r.shape, tile) for r in rows] + [_par_spec(p.shape) for p in pars]
        + [_rows_spec(c.shape, tile) for c in flat_cts],
        out_specs=[_rows_spec(rows[i].shape, tile) for i in gi] + [_par_spec(p.shape) for p in pars],
        out_shape=[jax.ShapeDtypeStruct(rows[i].shape, F32) for i in gi] + [jax.ShapeDtypeStruct(p.shape, F32) for p in pars],
        compiler_params=_cparams(("arbitrary",)), name=name)(*rows, *pars, *flat_cts)
    return list(outs[:len(gi)]), list(outs[len(gi):])


def _shift_rows(x, j):
    if j == 0:
        return x
    rolled = pltpu.roll(x, j, 0)
    row = lax.broadcasted_iota(jnp.int32, x.shape, 0)
    return jnp.where(row >= j, rolled, 0.0)


def _unshift_rows(x, j):
    if j == 0:
        return x
    S = x.shape[0]
    rolled = pltpu.roll(x, S - j, 0)
    row = lax.broadcasted_iota(jnp.int32, x.shape, 0)
    return jnp.where(row < S - j, rolled, 0.0)


def fir_fwd(x, taps, name):
    S, C = x.shape
    K = taps.shape[0]

    def body(x_ref, w_ref, y_ref):
        xv = x_ref[...]
        acc = jnp.zeros_like(xv)
        for k in range(K):
            acc = acc + _shift_rows(xv, K - 1 - k) * w_ref[pl.ds(k, 1), :]
        y_ref[...] = acc

    cs = pl.BlockSpec((S, 128), lambda j: (0, j))
    return pl.pallas_call(body, grid=(C // 128,), in_specs=[cs, pl.BlockSpec((K, 128), lambda j: (0, j))],
                          out_specs=cs, out_shape=jax.ShapeDtypeStruct((S, C), F32),
                          compiler_params=_cparams(("parallel",)), name=name)(x, taps)


def fir_bwd(x, taps, dy_list, name):
    S, C = x.shape
    K = taps.shape[0]
    n = len(dy_list)

    def body(*refs):
        x_ref, w_ref = refs[:2]
        dy = refs[2][...]
        for q in range(1, n):
            dy = dy + refs[2 + q][...]
        dx_ref, dw_ref, db_ref = refs[2 + n:]
        xv = x_ref[...]
        dx = jnp.zeros_like(xv)
        for k in range(K):
            j = K - 1 - k
            dx = dx + _unshift_rows(dy, j) * w_ref[pl.ds(k, 1), :]
            dw_ref[pl.ds(k, 1), :] = jnp.sum(dy * _shift_rows(xv, j), axis=0, keepdims=True)
        dx_ref[...] = dx
        db_ref[...] = jnp.sum(dy, axis=0, keepdims=True)

    cs = pl.BlockSpec((S, 128), lambda j: (0, j))
    ks = pl.BlockSpec((K, 128), lambda j: (0, j))
    bs = pl.BlockSpec((1, 128), lambda j: (0, j))
    return pl.pallas_call(body, grid=(C // 128,), in_specs=[cs, ks] + [cs] * n, out_specs=[cs, ks, bs],
                          out_shape=[jax.ShapeDtypeStruct((S, C), F32), jax.ShapeDtypeStruct((K, C), F32),
                                     jax.ShapeDtypeStruct((1, C), F32)],
                          compiler_params=_cparams(("parallel",)), name=name)(x, taps, *dy_list)


def _col(tile, lane, t):
    return jnp.sum(jnp.where(lane == t, tile, 0.0), axis=1, keepdims=True)


def _rwkv_step(s, rv, vcol):
    sa = jnp.sum(s * (-rv[3]), axis=1, keepdims=True)
    return s * rv[1] + sa * (rv[3] * rv[4]) + vcol * rv[2], sa


def scan_fwd(r, w, k, vT, kk, a, name):
    H, S, Dk = r.shape
    Dv = vT.shape[1]
    Tc = SCAN_CHUNK
    nc = S // Tc
    rows = [r, w, k, kk, a]

    def body(*refs):
        row_refs = refs[:5]
        vT_ref, yT_ref, sall_ref, s_ref = refs[5:]

        @pl.when(pl.program_id(0) == 0)
        def _():
            s_ref[...] = jnp.zeros_like(s_ref)

        yT_ref[...] = jnp.zeros_like(yT_ref)
        lane = lax.broadcasted_iota(jnp.int32, (Dv, Tc), 1)

        def step(t, states):
            new = []
            for h in range(H):
                s = states[h]
                sall_ref[t, h] = s
                rv = [ref[h, pl.ds(t, 1), :] for ref in row_refs]
                s, _ = _rwkv_step(s, rv, _col(vT_ref[h], lane, t))
                ycol = jnp.sum(s * rv[0], axis=1, keepdims=True)
                yT_ref[h] = jnp.where(lane == t, ycol, yT_ref[h])
                new.append(s)
            return tuple(new)

        states = lax.fori_loop(0, Tc, step, tuple(s_ref[h] for h in range(H)))
        for h in range(H):
            s_ref[h] = states[h]

    rs = pl.BlockSpec((H, Tc, Dk), lambda c: (0, c, 0))
    vs = pl.BlockSpec((H, Dv, Tc), lambda c: (0, 0, c))
    yT, sall = pl.pallas_call(
        body, grid=(nc,), in_specs=[rs] * 5 + [vs],
        out_specs=[vs, pl.BlockSpec((Tc, H, Dv, Dk), lambda c: (c, 0, 0, 0))],
        out_shape=[jax.ShapeDtypeStruct((H, Dv, S), F32), jax.ShapeDtypeStruct((S, H, Dv, Dk), F32)],
        scratch_shapes=[pltpu.VMEM((H, Dv, Dk), F32)],
        compiler_params=_cparams(("arbitrary",)), name=name)(*rows, vT)
    return yT, sall


def scan_bwd(r, w, k, vT, kk, a, sall, dyT, name):
    H, S, Dk = r.shape
    Dv = vT.shape[1]
    Tc = SCAN_CHUNK
    nc = S // Tc
    rows = [r, w, k, kk, a]

    def body(*refs):
        row_refs = refs[:5]
        vT_ref, dyT_ref, sall_ref = refs[5:8]
        drow_refs = refs[8:13]
        dvT_ref, ds_ref = refs[13:]

        @pl.when(pl.program_id(0) == 0)
        def _():
            ds_ref[...] = jnp.zeros_like(ds_ref)

        dvT_ref[...] = jnp.zeros_like(dvT_ref)
        lane = lax.broadcasted_iota(jnp.int32, (Dv, Tc), 1)

        def bstep(i, carry):
            t = Tc - 1 - i
            new = []
            for h in range(H):
                ds = carry[h]
                sp = sall_ref[t, h]
                rv = [ref[h, pl.ds(t, 1), :] for ref in row_refs]
                vcol = _col(vT_ref[h], lane, t)
                dycol = _col(dyT_ref[h], lane, t)
                st, sa = _rwkv_step(sp, rv, vcol)
                drow_refs[0][h, pl.ds(t, 1), :] = jnp.sum(st * dycol, axis=0, keepdims=True)
                g = ds + dycol * rv[0]
                drow_refs[1][h, pl.ds(t, 1), :] = jnp.sum(g * sp, axis=0, keepdims=True)
                drow_refs[2][h, pl.ds(t, 1), :] = jnp.sum(g * vcol, axis=0, keepdims=True)
                dvcol = jnp.sum(g * rv[2], axis=1, keepdims=True)
                dsa = jnp.sum(g * (rv[3] * rv[4]), axis=1, keepdims=True)
                db = jnp.sum(g * sa, axis=0, keepdims=True)
                dnkk = jnp.sum(sp * dsa, axis=0, keepdims=True)
                drow_refs[3][h, pl.ds(t, 1), :] = db * rv[4] - dnkk
                drow_refs[4][h, pl.ds(t, 1), :] = db * rv[3]
                dvT_ref[h] = jnp.where(lane == t, dvcol, dvT_ref[h])
                new.append(g * rv[1] - dsa * rv[3])
            return tuple(new)

        carry = lax.fori_loop(0, Tc, bstep, tuple(ds_ref[h] for h in range(H)))
        for h in range(H):
            ds_ref[h] = carry[h]

    rs = pl.BlockSpec((H, Tc, Dk), lambda c: (0, nc - 1 - c, 0))
    vs = pl.BlockSpec((H, Dv, Tc), lambda c: (0, 0, nc - 1 - c))
    outs = pl.pallas_call(
        body, grid=(nc,),
        in_specs=[rs] * 5 + [vs, vs, pl.BlockSpec((Tc, H, Dv, Dk), lambda c: (nc - 1 - c, 0, 0, 0))],
        out_specs=[rs] * 5 + [vs],
        out_shape=[jax.ShapeDtypeStruct((H, S, Dk), F32)] * 5 + [jax.ShapeDtypeStruct((H, Dv, S), F32)],
        scratch_shapes=[pltpu.VMEM((H, Dv, Dk), F32)],
        compiler_params=_cparams(("arbitrary",)), name=name)(*rows, vT, dyT, sall)
    return list(outs[:5]), outs[5]


CHUNK = 128


def chunk_fwd(fn, name, blocks, state_shape, out_width):
    H, S, _ = blocks[0].shape
    nc = S // CHUNK
    nb = len(blocks)

    def body(*refs):
        o_ref, sv_ref, st = refs[nb:]

        @pl.when(pl.program_id(1) == 0)
        def _():
            st[...] = jnp.zeros_like(st)

        s0 = st[...]
        sv_ref[0, 0] = s0
        s1, out = fn(s0, *[r[0] for r in refs[:nb]])
        st[...] = s1
        o_ref[0] = out

    spec = lambda w: pl.BlockSpec((1, CHUNK, w), lambda h, c: (h, c, 0))
    return pl.pallas_call(
        body, grid=(H, nc), in_specs=[spec(b.shape[2]) for b in blocks],
        out_specs=[spec(out_width), pl.BlockSpec((1, 1) + state_shape, lambda h, c: (h, c, 0, 0))],
        out_shape=[jax.ShapeDtypeStruct((H, S, out_width), F32), jax.ShapeDtypeStruct((H, nc) + state_shape, F32)],
        scratch_shapes=[pltpu.VMEM(state_shape, F32)],
        compiler_params=_cparams(("parallel", "arbitrary")), name=name)(*blocks)


def chunk_bwd(fn, name, blocks, states, dout):
    H, S, _ = blocks[0].shape
    nc = S // CHUNK
    nb = len(blocks)
    state_shape = states.shape[2:]

    def body(*refs):
        sv_ref, do_ref = refs[nb], refs[nb + 1]
        d_refs = refs[nb + 2:2 * nb + 2]
        dst = refs[2 * nb + 2]

        @pl.when(pl.program_id(1) == 0)
        def _():
            dst[...] = jnp.zeros_like(dst)

        _, vjp = jax.vjp(fn, sv_ref[0, 0], *[r[0] for r in refs[:nb]])
        grads = vjp((dst[...], do_ref[0]))
        dst[...] = grads[0]
        for d_ref, g in zip(d_refs, grads[1:]):
            d_ref[0] = g

    spec = lambda w: pl.BlockSpec((1, CHUNK, w), lambda h, c: (h, nc - 1 - c, 0))
    return pl.pallas_call(
        body, grid=(H, nc),
        in_specs=[spec(b.shape[2]) for b in blocks]
        + [pl.BlockSpec((1, 1) + state_shape, lambda h, c: (h, nc - 1 - c, 0, 0)), spec(dout.shape[2])],
        out_specs=[spec(b.shape[2]) for b in blocks],
        out_shape=[jax.ShapeDtypeStruct(b.shape, F32) for b in blocks],
        scratch_shapes=[pltpu.VMEM(state_shape, F32)],
        compiler_params=_cparams(("parallel", "arbitrary")), name=name)(*blocks, states, dout)


def _bdot(a, b, dims):
    return lax.dot_general(a.astype(BF16), b.astype(BF16), (dims, ((), ())), preferred_element_type=F32)


def ssd_chunk(state, cb, bb, da, xdt):
    T = cb.shape[0]
    ti = lax.broadcasted_iota(jnp.int32, (T, T), 0)
    si = lax.broadcasted_iota(jnp.int32, (T, T), 1)
    mask = ti >= si
    cs = jnp.dot(mask.astype(F32), da, precision=HI, preferred_element_type=F32)
    pick = (lax.broadcasted_iota(jnp.int32, cs.shape, 1) == 0).astype(F32)
    cs_row = lax.dot_general(pick, cs, (((1,), (1,)), ((), ())), precision=HI, preferred_element_type=F32)
    lmat = jnp.where(mask, jnp.exp(jnp.where(mask, cs - cs_row, 0.0)), 0.0)
    scores = _bdot(cb, bb, ((1,), (1,))) * lmat
    y = _bdot(scores, xdt, ((1,), (0,))) + _bdot(cb, state, ((1,), (1,))) * jnp.exp(cs[:, :HEAD_DIM])
    last = cs[T - 1:T, :]
    new_state = state * jnp.exp(last) + _bdot(xdt, bb * jnp.exp(last - cs), ((0,), (0,)))
    return new_state, y


HGRN_SUB = 16


def hgrn_chunk(state, q, k, lf, v):
    T, C = q.shape[0], HGRN_SUB
    ti = lax.broadcasted_iota(jnp.int32, (C, C), 0)
    si = lax.broadcasted_iota(jnp.int32, (C, C), 1)
    tril = (ti >= si).astype(F32)
    row = lax.broadcasted_iota(jnp.int32, (C, q.shape[1]), 0)
    outs = []
    for j in range(T // C):
        qj, kj, lj, vj = (a[j * C:(j + 1) * C] for a in (q, k, lf, v))
        b = jnp.dot(tril, lj, precision=HI, preferred_element_type=F32)
        o = _bdot(qj * jnp.exp(b), state, ((1,), (1,)))
        for s in range(C):
            m = row >= s
            e = jnp.where(m, jnp.exp(jnp.where(m, b - b[s:s + 1], 0.0)), 0.0)
            o = o + jnp.sum(qj * kj[s:s + 1] * e, axis=1, keepdims=True) * vj[s:s + 1]
        last = b[C - 1:C]
        state = state * jnp.exp(last) + _bdot(vj, kj * jnp.exp(last - b), ((0,), (0,)))
        outs.append(o)
    return state, jnp.concatenate(outs, axis=0)


def _to_headsT(x):
    S = x.shape[0]
    return jnp.transpose(x.reshape(S, N_HEADS, HEAD_DIM), (1, 2, 0))


def _from_headsT(x):
    H, dv, S = x.shape
    return jnp.transpose(x, (2, 0, 1)).reshape(S, H * dv)


def _attn_block(q, kp, kc, vp, vc, n, slope, dilation):
    blk = ATTN_BLK
    k2 = jnp.concatenate([kp, kc], axis=0)
    v2 = jnp.concatenate([vp, vc], axis=0)
    s = _bdot(q, k2, ((1,), (1,))) * (HEAD_DIM ** -0.5)
    i = lax.broadcasted_iota(jnp.int32, (blk, 2 * blk), 0)
    j = lax.broadcasted_iota(jnp.int32, (blk, 2 * blk), 1)
    dist = blk + i - j
    first_key = jnp.where(n > 0, 0, blk)
    valid = (dist >= 0) & (dist <= blk) & (j >= first_key)
    s = s - slope * (dist * dilation).astype(F32)
    s = jnp.where(valid, s, -1e30)
    m = jnp.max(s, axis=-1, keepdims=True)
    p = jnp.exp(s - m)
    l = jnp.sum(p, axis=-1, keepdims=True)
    o = _bdot(p, v2, ((1,), (0,))) / l
    lse = jnp.broadcast_to(m + jnp.log(l), o.shape)
    return o, lse


_QCOL = SEG_Q // D_GROUP


def _attn_specs(width):
    blk, per = ATTN_BLK, width // D_GROUP
    cur = lambda j: pl.BlockSpec((blk, D_GROUP), lambda z, n: (n, z * per + j))
    prev = lambda j: pl.BlockSpec((blk, D_GROUP), lambda z, n: (jnp.maximum(n - 1, 0), z * per + j))
    return cur, prev


def attn_fwd(proj, dilation, name):
    S = proj.shape[0]
    L = S // dilation
    nb = L // ATTN_BLK
    cur, prev = _attn_specs(proj.shape[1])
    out, _ = _attn_specs(D_GROUP)
    p3 = proj.reshape(L, dilation * proj.shape[1])

    def body(q_ref, kp_ref, kc_ref, vp_ref, vc_ref, o_ref, l_ref):
        n = pl.program_id(1)
        q, kp, kc, vp, vc = q_ref[...], kp_ref[...], kc_ref[...], vp_ref[...], vc_ref[...]
        for h in range(N_HEADS):
            hs = slice(h * HEAD_DIM, (h + 1) * HEAD_DIM)
            o, lse = _attn_block(q[:, hs], kp[:, hs], kc[:, hs], vp[:, hs], vc[:, hs], n, ALIBI_SLOPES[h], dilation)
            o_ref[:, hs] = o
            l_ref[:, hs] = lse

    o, lse = pl.pallas_call(
        body, grid=(dilation, nb),
        in_specs=[cur(_QCOL), prev(_QCOL + 1), cur(_QCOL + 1), prev(_QCOL + 2), cur(_QCOL + 2)],
        out_specs=[out(0), out(0)], out_shape=[jax.ShapeDtypeStruct((L, dilation * D_GROUP), F32)] * 2,
        compiler_params=_cparams(("parallel", "arbitrary")), name=name)(p3, p3, p3, p3, p3)
    return o.reshape(S, D_GROUP), lse.reshape(S, D_GROUP)


def attn_bwd(proj, do, dlse, dilation, name):
    S = proj.shape[0]
    L = S // dilation
    blk = ATTN_BLK
    nb = L // blk
    cur, prev = _attn_specs(proj.shape[1])
    out, _ = _attn_specs(D_GROUP)
    full = pl.BlockSpec((L, D_GROUP), lambda z, n: (0, z))
    p3 = proj.reshape(L, dilation * proj.shape[1])
    to3 = lambda t: t.reshape(L, dilation * D_GROUP)

    def body(q_ref, kp_ref, kc_ref, vp_ref, vc_ref, do_ref, dl_ref, dq_ref, dk_ref, dv_ref):
        n = pl.program_id(1)

        @pl.when(n == 0)
        def _():
            dk_ref[...] = jnp.zeros_like(dk_ref)
            dv_ref[...] = jnp.zeros_like(dv_ref)

        q, kp, kc, vp, vc = q_ref[...], kp_ref[...], kc_ref[...], vp_ref[...], vc_ref[...]
        do_v, dl_v = do_ref[...], dl_ref[...]
        here = pl.ds(pl.multiple_of(n * blk, blk), blk)
        before = pl.ds(pl.multiple_of(jnp.maximum(n - 1, 0) * blk, blk), blk)
        for h in range(N_HEADS):
            hs = slice(h * HEAD_DIM, (h + 1) * HEAD_DIM)
            f = lambda q_, kp_, kc_, vp_, vc_, h=h: _attn_block(q_, kp_, kc_, vp_, vc_, n, ALIBI_SLOPES[h], dilation)
            _, vjp = jax.vjp(f, q[:, hs], kp[:, hs], kc[:, hs], vp[:, hs], vc[:, hs])
            dq, dkp, dkc, dvp, dvc = vjp((do_v[:, hs], dl_v[:, hs]))
            dq_ref[:, hs] = dq
            dk_ref[here, hs] += dkc
            dv_ref[here, hs] += dvc
            dk_ref[before, hs] += dkp
            dv_ref[before, hs] += dvp

    dq, dk, dv = pl.pallas_call(
        body, grid=(dilation, nb),
        in_specs=[cur(_QCOL), prev(_QCOL + 1), cur(_QCOL + 1), prev(_QCOL + 2), cur(_QCOL + 2), out(0), out(0)],
        out_specs=[out(0), full, full], out_shape=[jax.ShapeDtypeStruct((L, dilation * D_GROUP), F32)] * 3,
        compiler_params=_cparams(("parallel", "arbitrary")), name=name)(p3, p3, p3, p3, p3, to3(do), to3(dlse))
    return dq.reshape(S, D_GROUP), dk.reshape(S, D_GROUP), dv.reshape(S, D_GROUP)


def _head_ones(width, group):
    i = lax.broadcasted_iota(jnp.int32, (width, width), 0) // group
    j = lax.broadcasted_iota(jnp.int32, (width, width), 1) // group
    return (i == j).astype(F32)


def _group_sum(x, group):
    return jnp.dot(x, _head_ones(x.shape[1], group), precision=HI, preferred_element_type=F32)


def _spread(width_in, width_out, rep):
    i = lax.broadcasted_iota(jnp.int32, (width_in, width_out), 0)
    j = lax.broadcasted_iota(jnp.int32, (width_in, width_out), 1) // rep
    return (i == j).astype(F32)


def _hdot(a, b):
    return jnp.dot(a, b, precision=HI, preferred_element_type=F32)


def _sigmoid(x):
    return 1.0 / (1.0 + jnp.exp(-x))


def _softplus(x):
    return jnp.maximum(x, 0.0) + jnp.log(1.0 + jnp.exp(jnp.minimum(x, -x)))


def _silu(x):
    return x * _sigmoid(x)


def rwkv_pre(layer):
    def fn(*args):
        if layer == 0:
            fs, w0, w2p, a0, a2p, g2p, k_k, k_a = args
        else:
            fs, vfirst, w0, w2p, a0, a2p, g2p, k_k, k_a, v0, v2p = args
        r, k, v = fs[:, 0:256], fs[:, 256:512], fs[:, 512:768]
        lora = fs[:, 768:896]
        w_log = -_softplus(-(w0 + _hdot(jnp.tanh(lora), w2p))) - 0.5
        decay = jnp.exp(-jnp.exp(w_log))
        a = _sigmoid(a0 + _hdot(lora, a2p))
        g = _hdot(_sigmoid(lora), g2p)
        if layer > 0:
            v = v + (vfirst - v) * _sigmoid(v0 + _hdot(fs[:, 896:1024], v2p))
        kk = k * k_k
        kk = kk / jnp.maximum(jnp.sqrt(_group_sum(kk * kk, HEAD_DIM)), 1e-12)
        k = k * (1.0 + (a - 1.0) * k_a)
        return r, decay, k, v, kk, a, g
    return fn


def rwkv_post(y, r, k, v, g, lnx_w, lnx_b, r_k):
    mu = _group_sum(y, HEAD_DIM) * (1.0 / HEAD_DIM)
    yc = y - mu
    var = _group_sum(yc * yc, HEAD_DIM) * (1.0 / HEAD_DIM)
    yn = yc * lax.rsqrt(var + RWKV_GN_EPS) * lnx_w + lnx_b
    bonus = _group_sum(r * k * r_k, HEAD_DIM) * v
    return ((yn + bonus) * g,)


def attn_combine(o1, o2, o3, l1, l2, l3):
    m = jnp.maximum(jnp.maximum(l1, l2), l3)
    e1, e2, e3 = jnp.exp(l1 - m), jnp.exp(l2 - m), jnp.exp(l3 - m)
    return ((o1 * e1 + o2 * e2 + o3 * e3) / (e1 + e2 + e3),)


def ssd_pre(xc, dtr, conv_b, dt_bias, a_log):
    xbc = _silu(xc + conv_b)
    xs, bm, cm = xbc[:, 0:256], xbc[:, 256:512], xbc[:, 512:768]
    dt = _softplus(dtr + dt_bias)
    a_neg = -jnp.exp(a_log)
    wide = _spread(128, N_HEADS * SSD_STATE, SSD_STATE)
    w = _hdot(dt, wide) * _hdot(a_neg, wide)
    xdt = xs * _hdot(dt, _spread(128, D_GROUP, HEAD_DIM))
    rr = jnp.concatenate([cm[:, 0:128], cm[:, 0:128], cm[:, 128:256], cm[:, 128:256]], axis=1)
    kk = jnp.concatenate([bm[:, 0:128], bm[:, 0:128], bm[:, 128:256], bm[:, 128:256]], axis=1)
    return rr, w, kk, xdt, xs


def ssd_post(ys, z, xs, d_skip, norm_w):
    y = ys + xs * _hdot(d_skip, _spread(128, D_GROUP, HEAD_DIM))
    y = y * _silu(z)
    half = D_GROUP // 2
    parts = []
    for g in range(2):
        t = y[:, g * half:(g + 1) * half]
        parts.append(t * lax.rsqrt(jnp.mean(t * t, axis=-1, keepdims=True) + RMS_EPS))
    return (jnp.concatenate(parts, axis=1) * norm_w,)


def hgrn_pre(seg, lb):
    q, f, i = seg[:, 0:256], seg[:, 256:512], seg[:, 512:768]
    forget = lb + (1.0 - lb) * _sigmoid(f)
    return _silu(q), 1.0 - forget, jnp.log(forget), i


def hgrn_post(o, seg, norm_w):
    g = seg[:, 768:1024]
    ms = _group_sum(o * o, HEAD_DIM) * (1.0 / HEAD_DIM)
    return (o * lax.rsqrt(ms + RMS_EPS) * norm_w * _silu(g),)


def ln_res(x, y, w, b):
    z = ALPHA * x + y
    mu = jnp.mean(z, axis=-1, keepdims=True)
    zc = z - mu
    var = jnp.mean(zc * zc, axis=-1, keepdims=True)
    return (zc * lax.rsqrt(var + LN_EPS) * w + b,)


def relu2(u):
    r = jnp.maximum(u, 0.0)
    return (r * r,)


def loss_and_grad(y, tgt, name):
    S, D = y.shape
    tile = ROW_TILE

    def body(y_ref, t_ref, l_ref, dy_ref):
        e = y_ref[...] - t_ref[...]
        dy_ref[...] = e * (1.0 / D)

        @pl.when(pl.program_id(0) == 0)
        def _():
            l_ref[...] = jnp.zeros_like(l_ref)

        per_row = 0.5 * jnp.mean(e * e, axis=-1, keepdims=True)
        l_ref[...] += jnp.sum(per_row, axis=0, keepdims=True) * jnp.ones((1, 128), F32)

    return pl.pallas_call(body, grid=(S // tile,), in_specs=[_row_spec(D, tile)] * 2,
                          out_specs=[_par_spec((1, 128)), _row_spec(D, tile)],
                          out_shape=[jax.ShapeDtypeStruct((1, 128), F32), jax.ShapeDtypeStruct((S, D), F32)],
                          compiler_params=_cparams(("arbitrary",)), name=name)(y, tgt)


def add_rows(arrs, name):
    (out,) = tl_fwd(lambda *a: (functools.reduce(lambda p, q: p + q, a),), name, arrs, [], [arrs[0].shape[1]])
    return out


def small_fwd(fn, name, ins, out_shapes):
    n = len(ins)

    def body(*refs):
        outs = fn(*[r[...] for r in refs[:n]])
        for o_ref, o in zip(refs[n:], outs):
            o_ref[...] = o

    return pl.pallas_call(body, out_shape=[jax.ShapeDtypeStruct(s, F32) for s in out_shapes], name=name)(*ins)


def small_bwd(fn, name, ins, cts):
    n, m = len(ins), len(cts)

    def body(*refs):
        _, vjp = jax.vjp(lambda *a: tuple(fn(*a)), *[r[...] for r in refs[:n]])
        grads = vjp(tuple(r[...] for r in refs[n:n + m]))
        for o_ref, g in zip(refs[n + m:], grads):
            o_ref[...] = g

    return pl.pallas_call(body, out_shape=[jax.ShapeDtypeStruct(a.shape, F32) for a in ins], name=name)(*ins, *cts)


def param_prep(lower_bounds, mu0, mu1):
    e = jnp.exp(lower_bounds - jnp.max(lower_bounds, axis=0, keepdims=True))
    sm = e / jnp.sum(e, axis=0, keepdims=True)
    lb0 = sm[0:1] - sm[0:1]
    lb1 = sm[0:1] + sm[1:2] - sm[0:1]
    return lb0, lb1, mu0, 1.0 - mu0, mu1, 1.0 - mu1


def _rows_tile(rows):
    return _pick(rows, (256, 128, 64, 32, 16, 8))


def sum_parts(parts, name):
    P, rows, cols = parts.shape
    tile = _rows_tile(rows)

    def body(p_ref, o_ref):
        acc = p_ref[0]
        for p in range(1, P):
            acc = acc + p_ref[p]
        o_ref[...] = acc

    return pl.pallas_call(body, grid=(rows // tile,), in_specs=[pl.BlockSpec((P, tile, cols), lambda i: (0, i, 0))],
                          out_specs=pl.BlockSpec((tile, cols), lambda i: (i, 0)),
                          out_shape=jax.ShapeDtypeStruct((rows, cols), F32),
                          compiler_params=_cparams(("parallel",)), name=name)(parts)


def sum_slots(slots, own, ids, name):
    P, rows, cols = slots.shape
    tile = _rows_tile(rows)

    def body(ids_ref, s_ref, own_ref, o_ref):
        me = ids_ref[2]
        acc = None
        for p in range(P):
            term = jnp.where(me == p, own_ref[0, 0], s_ref[p].astype(F32))
            acc = term if acc is None else acc + term
        o_ref[0] = acc

    grid_spec = pltpu.PrefetchScalarGridSpec(
        num_scalar_prefetch=1, grid=(rows // tile,),
        in_specs=[pl.BlockSpec((P, tile, cols), lambda i, ids: (0, i, 0)),
                  pl.BlockSpec((1, 1, tile, cols), lambda i, ids: (ids[0], ids[1], i, 0))],
        out_specs=pl.BlockSpec((1, tile, cols), lambda i, ids: (ids[0], i, 0)))
    return pl.pallas_call(body, grid_spec=grid_spec, out_shape=jax.ShapeDtypeStruct((2, rows, cols), F32),
                          compiler_params=_cparams(("parallel",)), name=name)(ids, slots, own)


def adamw(w, g, m, v, name):
    rows, cols = w.shape
    tile = _rows_tile(rows)

    def body(w_ref, g_ref, m_ref, v_ref, d_ref, nm_ref, nv_ref):
        gv = g_ref[...]
        nm = ADAM_B1 * m_ref[...] + (1.0 - ADAM_B1) * gv
        nv = ADAM_B2 * v_ref[...] + (1.0 - ADAM_B2) * jnp.square(gv)
        m_hat = nm / (1.0 - ADAM_B1 ** ADAM_STEP)
        v_hat = nv / (1.0 - ADAM_B2 ** ADAM_STEP)
        d_ref[...] = -ADAM_LR * (m_hat / (jnp.sqrt(v_hat) + ADAM_EPS) + ADAM_WD * w_ref[...])
        nm_ref[...] = nm
        nv_ref[...] = nv

    spec = pl.BlockSpec((tile, cols), lambda i: (i, 0))
    return pl.pallas_call(body, grid=(rows // tile,), in_specs=[spec] * 4, out_specs=[spec] * 3,
                          out_shape=[jax.ShapeDtypeStruct((rows, cols), F32)] * 3,
                          compiler_params=_cparams(("parallel",)), name=name)(w, g, m, v)


MESH = pl.DeviceIdType.MESH
ANY = pl.BlockSpec(memory_space=pl.ANY)


def _flip(v, bit):
    return 1 - v if bit else v


_CHIP_RELATIONS = ((1, 0), (0, 1), (1, 1))


def gather_chips(arrs, small, name):
    n = len(arrs)

    def body(*refs):
        ins, small_in = refs[:n], refs[n]
        outs, small_out = refs[n + 1:2 * n + 1], refs[2 * n + 1]
        send, recv, fsend, frecv, loc, ssend, srecv = refs[2 * n + 2:]
        x, y, c = lax.axis_index("x"), lax.axis_index("y"), lax.axis_index("c")
        me = 2 * x + y
        chips = [(_flip(x, bx), _flip(y, by)) for bx, by in _CHIP_RELATIONS]

        def over_ici(i, r, block_chip):
            return pltpu.make_async_remote_copy(src_ref=ins[i].at[c], dst_ref=outs[i].at[block_chip, c],
                                                send_sem=send.at[i, r], recv_sem=recv.at[i, r],
                                                device_id=(chips[r][0], chips[r][1], c), device_id_type=MESH)

        def to_sibling(i, r, layer):
            blk = outs[i].at[2 * chips[r][0] + chips[r][1], layer]
            return pltpu.make_async_remote_copy(src_ref=blk, dst_ref=blk, send_sem=fsend.at[i, r],
                                                recv_sem=frecv.at[i, r], device_id=(x, y, 1 - c), device_id_type=MESH)

        local = [pltpu.make_async_copy(ins[i], outs[i].at[me], loc.at[i]) for i in range(n)]
        local.append(pltpu.make_async_copy(small_in, small_out.at[me], loc.at[n]))
        for cp in local:
            cp.start()
        first = [over_ici(i, r, me) for i in range(n) for r in range(3)]
        smalls = [pltpu.make_async_remote_copy(src_ref=small_in, dst_ref=small_out.at[me], send_sem=ssend.at[r],
                                               recv_sem=srecv.at[r], device_id=(chips[r][0], chips[r][1], c),
                                               device_id_type=MESH) for r in range(3)]
        for cp in first + smalls:
            cp.start()
        passed = []
        for r in range(3):
            for i in range(n):
                over_ici(i, r, 2 * chips[r][0] + chips[r][1]).wait_recv()
                fw = to_sibling(i, r, c)
                fw.start()
                passed.append(fw)
        for r in range(3):
            for i in range(n):
                to_sibling(i, r, 1 - c).wait_recv()
        for cp in first + passed:
            cp.wait_send()
        for cp in smalls + local:
            cp.wait()

    return pl.pallas_call(
        body, in_specs=[ANY] * (n + 1), out_specs=[ANY] * (n + 1),
        out_shape=[jax.ShapeDtypeStruct((4,) + a.shape, a.dtype) for a in arrs]
        + [jax.ShapeDtypeStruct((4,) + small.shape, small.dtype)],
        scratch_shapes=[pltpu.SemaphoreType.DMA((n, 3)), pltpu.SemaphoreType.DMA((n, 3)), pltpu.SemaphoreType.DMA((n, 3)),
                        pltpu.SemaphoreType.DMA((n, 3)), pltpu.SemaphoreType.DMA((n + 1,)),
                        pltpu.SemaphoreType.DMA((3,)), pltpu.SemaphoreType.DMA((3,))],
        name=name)(*arrs, small)


_RELATIONS = tuple((r >> 2 & 1, r >> 1 & 1, r & 1) for r in range(1, 8))


def gather_devices(arr, name):
    def body(in_ref, out_ref, send, recv, loc):
        x, y, c = lax.axis_index("x"), lax.axis_index("y"), lax.axis_index("c")
        me = 4 * x + 2 * y + c
        lc = pltpu.make_async_copy(in_ref, out_ref.at[me], loc)
        lc.start()
        pending = [lc]
        for r, (bx, by, bc) in enumerate(_RELATIONS):
            cp = pltpu.make_async_remote_copy(src_ref=in_ref, dst_ref=out_ref.at[me], send_sem=send.at[r],
                                              recv_sem=recv.at[r], device_id=(_flip(x, bx), _flip(y, by), _flip(c, bc)),
                                              device_id_type=MESH)
            cp.start()
            pending.append(cp)
        for cp in pending:
            cp.wait()

    return pl.pallas_call(
        body, in_specs=[ANY], out_specs=ANY, out_shape=jax.ShapeDtypeStruct((8,) + arr.shape, arr.dtype),
        scratch_shapes=[pltpu.SemaphoreType.DMA((7,)), pltpu.SemaphoreType.DMA((7,)), pltpu.SemaphoreType.DMA(())],
        name=name)(arr)


def scatter_partials(arrs, name):
    n = len(arrs)

    def body(*refs):
        ins, outs = refs[:n], refs[n:2 * n]
        send, recv, loc = refs[2 * n:]
        x, y, c = lax.axis_index("x"), lax.axis_index("y"), lax.axis_index("c")
        me = 4 * x + 2 * y + c
        pending = []
        for i in range(n):
            lc = pltpu.make_async_copy(ins[i].at[c, 2 * x + y], outs[i].at[me], loc.at[i])
            lc.start()
            pending.append(lc)
            for r, (bx, by, bc) in enumerate(_RELATIONS):
                px, py, pc = _flip(x, bx), _flip(y, by), _flip(c, bc)
                cp = pltpu.make_async_remote_copy(src_ref=ins[i].at[pc, 2 * px + py], dst_ref=outs[i].at[me],
                                                  send_sem=send.at[i, r], recv_sem=recv.at[i, r],
                                                  device_id=(px, py, pc), device_id_type=MESH)
                cp.start()
                pending.append(cp)
        for cp in pending:
            cp.wait()

    return pl.pallas_call(
        body, in_specs=[ANY] * n, out_specs=[ANY] * n,
        out_shape=[jax.ShapeDtypeStruct((8,) + a.shape[2:], a.dtype) for a in arrs],
        scratch_shapes=[pltpu.SemaphoreType.DMA((n, 7)), pltpu.SemaphoreType.DMA((n, 7)), pltpu.SemaphoreType.DMA((n,))],
        name=name)(*arrs)


EXCHANGE_PIECES = 8


def sibling_exchange(arrs, name):
    n = len(arrs)

    def body(*refs):
        bufs = refs[n:2 * n]
        send, recv = refs[2 * n:]
        x, y, c = lax.axis_index("x"), lax.axis_index("y"), lax.axis_index("c")
        pending = []
        for i in range(n):
            rows = bufs[i].shape[1] // EXCHANGE_PIECES
            for j in range(EXCHANGE_PIECES):
                piece = bufs[i].at[c, pl.ds(j * rows, rows)]
                cp = pltpu.make_async_remote_copy(src_ref=piece, dst_ref=piece, send_sem=send.at[i, j],
                                                  recv_sem=recv.at[i, j], device_id=(x, y, 1 - c), device_id_type=MESH)
                cp.start()
                pending.append(cp)
        for i in range(n):
            rows = bufs[i].shape[1] // EXCHANGE_PIECES
            for j in range(EXCHANGE_PIECES):
                landed = bufs[i].at[1 - c, pl.ds(j * rows, rows)]
                pltpu.make_async_remote_copy(src_ref=landed, dst_ref=landed, send_sem=send.at[i, j], recv_sem=recv.at[i, j],
                                             device_id=(x, y, 1 - c), device_id_type=MESH).wait_recv()
        for cp in pending:
            cp.wait_send()

    return pl.pallas_call(
        body, in_specs=[ANY] * n, out_specs=[ANY] * n,
        out_shape=[jax.ShapeDtypeStruct(a.shape, a.dtype) for a in arrs], input_output_aliases={i: i for i in range(n)},
        scratch_shapes=[pltpu.SemaphoreType.DMA((n, EXCHANGE_PIECES)), pltpu.SemaphoreType.DMA((n, EXCHANGE_PIECES))],
        name=name)(*arrs)


def rwkv_fwd(l, seg, taps, pars, vfirst):
    fs = fir_fwd(seg, taps, f"rwkv_shift_fwd{l}")
    rows = [fs] + ([vfirst] if l else [])
    R, W, K, V, KK, A, G = tl_fwd(rwkv_pre(l), f"rwkv_pre_fwd{l}", rows, pars["pre"],
                                  [HM64, HM64, HM64, D_GROUP, HM64, HM64, D_GROUP])
    vT = _to_headsT(V)
    yT, sall = scan_fwd(R, W, K, vT, KK, A, f"rwkv_scan_fwd{l}")
    Y = _from_headsT(yT)
    (out,) = tl_fwd(rwkv_post, f"rwkv_post_fwd{l}", [Y, R, K, V, G], pars["post"], [D_GROUP])
    return out, V, (seg, taps, rows, R, W, K, V, KK, A, G, Y, vT, sall)


def rwkv_bwd(l, saved, pars, dout, dv_extra):
    seg, taps, rows, R, W, K, V, KK, A, G, Y, vT, sall = saved
    (dY, dR1, dK1, dV1, dG), dpost = tl_bwd(rwkv_post, f"rwkv_post_bwd{l}", [Y, R, K, V, G], pars["post"], [[dout]])
    (dR2, dW, dK2, dKK, dA), dvT = scan_bwd(R, W, K, vT, KK, A, sall, _to_headsT(dY), f"rwkv_scan_bwd{l}")
    cts = [[dR1, dR2], [dW], [dK1, dK2], [dV1, _from_headsT(dvT)] + dv_extra, [dKK], [dA], [dG]]
    drows, dpre = tl_bwd(rwkv_pre(l), f"rwkv_pre_bwd{l}", rows, pars["pre"], cts)
    dseg, dtaps, _ = fir_bwd(seg, taps, [drows[0]], f"rwkv_shift_bwd{l}")
    return dseg, (drows[1] if l else None), dtaps, dpre, dpost


def attn_mix_fwd(l, proj):
    os_, ls_ = [], []
    for b, (_, d) in enumerate(DILATED_BRANCHES):
        o, lse = attn_fwd(proj, d, f"attn_fwd{l}_{b}")
        os_.append(o)
        ls_.append(lse)
    (out,) = tl_fwd(attn_combine, f"attn_combine_fwd{l}", os_ + ls_, [], [D_GROUP])
    return out, (proj, os_, ls_)


def attn_mix_bwd(l, saved, dout):
    proj, os_, ls_ = saved
    drows, _ = tl_bwd(attn_combine, f"attn_combine_bwd{l}", os_ + ls_, [], [[dout]])
    grads = [attn_bwd(proj, drows[b], drows[3 + b], d, f"attn_bwd{l}_{b}") for b, (_, d) in enumerate(DILATED_BRANCHES)]
    return tuple(add_rows([g[j] for g in grads], f"attn_d{'qkv'[j]}{l}") for j in range(3))


def ssd_fwd(l, z, xbc, dtr, pars):
    xc = fir_fwd(xbc, pars["taps"], f"ssd_conv_fwd{l}")
    rr, w, kk, xdt, xs = tl_fwd(ssd_pre, f"ssd_pre_fwd{l}", [xc, dtr], pars["pre"], [HM128, HM128, HM128, HM64, D_GROUP])
    blocks = [rr, kk, w, xdt]
    ys, states = chunk_fwd(ssd_chunk, f"ssd_scan_fwd{l}", blocks, (HEAD_DIM, SSD_STATE), HEAD_DIM)
    (out,) = tl_fwd(ssd_post, f"ssd_post_fwd{l}", [ys, z, xs], pars["post"], [D_GROUP])
    return out, (z, xbc, dtr, xc, blocks, states, xs, ys)


def ssd_bwd(l, saved, pars, dout):
    z, xbc, dtr, xc, blocks, states, xs, ys = saved
    (dys, dz, dxs), dpost = tl_bwd(ssd_post, f"ssd_post_bwd{l}", [ys, z, xs], pars["post"], [[dout]])
    drr, dkk, dw, dxdt = chunk_bwd(ssd_chunk, f"ssd_scan_bwd{l}", blocks, states, dys)
    (dxc, ddtr), dpre = tl_bwd(ssd_pre, f"ssd_pre_bwd{l}", [xc, dtr], pars["pre"], [[drr], [dw], [dkk], [dxdt], [dxs]])
    dxbc, dtaps, _ = fir_bwd(xbc, pars["taps"], [dxc], f"ssd_conv_bwd{l}")
    return dz, dxbc, ddtr, dtaps, dpre, dpost


def hgrn_fwd(l, seg, pars):
    blocks = tl_fwd(hgrn_pre, f"hgrn_pre_fwd{l}", [seg], pars["pre"], [HM64] * 4)
    o, states = chunk_fwd(hgrn_chunk, f"hgrn_scan_fwd{l}", blocks, (HEAD_DIM, HEAD_DIM), HEAD_DIM)
    (out,) = tl_fwd(hgrn_post, f"hgrn_post_fwd{l}", [o, seg], pars["post"], [D_GROUP])
    return out, (seg, blocks, states, o)


def hgrn_bwd(l, saved, pars, dout):
    seg, blocks, states, o = saved
    (do, dseg1), dpost = tl_bwd(hgrn_post, f"hgrn_post_bwd{l}", [o, seg], pars["post"], [[dout]])
    dq, dkk, dlf, di = chunk_bwd(hgrn_chunk, f"hgrn_scan_bwd{l}", blocks, states, do)
    (dseg2,), dpre = tl_bwd(hgrn_pre, f"hgrn_pre_bwd{l}", [seg], pars["pre"], [[dq], [dkk], [dlf], [di]])
    return add_rows([dseg1, dseg2], f"hgrn_dseg{l}"), dpre, dpost


def layer_fwd(l, x, wts, pars, vfirst):
    proj = matmul(x, wts["in"], "nn", f"proj_fwd{l}")
    seg_h = proj[:, SEG_HGRN:SEG_HGRN + 1024]
    seg_r = proj[:, SEG_RWKV:SEG_RWKV + 1024]
    z = proj[:, SEG_Z:SEG_Z + D_GROUP]
    xbc = proj[:, SEG_XBC:SEG_XBC + SSD_XBC]
    dtr = proj[:, SEG_DT:SEG_DT + 128]
    ya, v_rwkv, sa = rwkv_fwd(l, seg_r, pars["rwkv"]["taps"], pars["rwkv"], vfirst)
    yb, sb = attn_mix_fwd(l, proj)
    yc, sc = ssd_fwd(l, z, xbc, dtr, pars["ssd"])
    yd, sd = hgrn_fwd(l, seg_h, pars["hgrn"])
    mix = jnp.concatenate([ya, yb, yc, yd], axis=1)
    mo = matmul(mix, wts["out"], "nn", f"out_fwd{l}")
    (x1,) = tl_fwd(ln_res, f"ln1_fwd{l}", [x, mo], pars["ln1"], [D_MODEL])
    u = matmul(x1, wts["up"], "nn", f"up_fwd{l}")
    (h,) = tl_fwd(relu2, f"relu2_fwd{l}", [u], [], [D_FF])
    dn = matmul(h, wts["down"], "nn", f"down_fwd{l}")
    (x2,) = tl_fwd(ln_res, f"ln2_fwd{l}", [x1, dn], pars["ln2"], [D_MODEL])
    return x2, v_rwkv, (x, sa, sb, sc, sd, mix, mo, x1, u, h, dn)


def layer_bwd(l, saved, wts, pars, dx2, dv_extra):
    x, sa, sb, sc, sd, mix, mo, x1, u, h, dn = saved
    S = x.shape[0]
    g = {}
    (dx1a, ddn), g["ln2"] = tl_bwd(ln_res, f"ln2_bwd{l}", [x1, dn], pars["ln2"], [[dx2]])
    g["down"] = matmul(h, ddn, "tn", f"down_dw{l}")
    dh = matmul(ddn, wts["down"], "nt", f"down_dx{l}")
    (du,), _ = tl_bwd(relu2, f"relu2_bwd{l}", [u], [], [[dh]])
    g["up"] = matmul(x1, du, "tn", f"up_dw{l}")
    dx1 = matmul(du, wts["up"], "nt", f"up_dx{l}", add=dx1a)
    (dxa, dmo), g["ln1"] = tl_bwd(ln_res, f"ln1_bwd{l}", [x, mo], pars["ln1"], [[dx1]])
    g["out"] = matmul(mix, dmo, "tn", f"out_dw{l}")
    dmix = matmul(dmo, wts["out"], "nt", f"out_dx{l}")
    dya, dyb, dyc, dyd = (dmix[:, j * D_GROUP:(j + 1) * D_GROUP] for j in range(4))
    dseg_r, dvfirst, g["rwkv_taps"], g["rwkv_pre"], g["rwkv_post"] = rwkv_bwd(l, sa, pars["rwkv"], dya, dv_extra)
    dq, dk, dv = attn_mix_bwd(l, sb, dyb)
    dz, dxbc, ddtr, g["ssd_taps"], g["ssd_pre"], g["ssd_post"] = ssd_bwd(l, sc, pars["ssd"], dyc)
    dseg_h, g["hgrn_pre"], g["hgrn_post"] = hgrn_bwd(l, sd, pars["hgrn"], dyd)
    dproj = jnp.concatenate([dseg_h, dseg_r, dq, dk, dv, dz, dxbc, ddtr, jnp.zeros((S, PROJ_W - SEG_DT - 128), F32)], axis=1)
    g["in"] = matmul(x, dproj, "tn", f"proj_dw{l}")
    dx = matmul(dproj, wts["in"], "nt", f"proj_dx{l}", add=dxa)
    return dx, dvfirst, g


SMALL = ("lower_bounds", "w_in_vres", "mu_shift", "mu_vres", "rwkv_w0", "rwkv_w2", "rwkv_a0", "rwkv_a2", "rwkv_g2",
         "rwkv_k_k", "rwkv_k_a", "rwkv_r_k", "rwkv_lnx_w", "rwkv_lnx_b", "rwkv_v0", "rwkv_v2", "ssd_conv_w",
         "ssd_conv_b", "ssd_dt_bias", "ssd_A_log", "ssd_D", "ssd_norm_w", "hgrn_norm_w", "ln1_w", "ln1_b", "ln2_w", "ln2_b")
BIG = ("w_in", "w_out", "w_up", "w_down")
SMALL_SHARDED = {"w_in_vres": 1, "rwkv_w2": 2, "rwkv_a2": 2, "rwkv_g2": 2, "rwkv_v2": 2, "ssd_conv_w": 2}
WEIGHTS = ("lower_bounds", "w_in", "w_in_vres", "mu_shift", "mu_vres", "rwkv_w0", "rwkv_w2", "rwkv_a0", "rwkv_a2",
           "rwkv_g2", "rwkv_k_k", "rwkv_k_a", "rwkv_r_k", "rwkv_lnx_w", "rwkv_lnx_b", "rwkv_v0", "rwkv_v2",
           "ssd_conv_w", "ssd_conv_b", "ssd_dt_bias", "ssd_A_log", "ssd_D", "ssd_norm_w", "hgrn_norm_w", "w_out",
           "ln1_w", "ln1_b", "w_up", "w_down", "ln2_w", "ln2_b")


def _row(v, width=None):
    v = v.reshape(1, -1).astype(F32)
    if width is not None and v.shape[1] < width:
        v = jnp.pad(v, ((0, 0), (0, width - v.shape[1])))
    return v


def _rows_at(m, rows, at):
    return jnp.pad(m.astype(F32), ((at, rows - at - m.shape[0]), (0, 0)))


def _pad_w_in(w_in_l, vres):
    rows = w_in_l.shape[0]
    out = []
    order = sorted(_PIECES, key=lambda p: p[2])
    pos = 0
    for start, width, at in order:
        if at > pos:
            out.append(jnp.zeros((rows, at - pos), w_in_l.dtype))
        out.append(w_in_l[:, start:start + width])
        pos = at + width
        if at == SEG_RWKV and vres is not None:
            out.append(vres.astype(w_in_l.dtype))
            pos += vres.shape[1]
    out.append(jnp.zeros((rows, PROJ_W - pos), w_in_l.dtype))
    return jnp.concatenate(out, axis=1)


def _unpad_w_in(g):
    return jnp.concatenate([g[:, at:at + width] for _, width, at in _PIECES], axis=1)


def layer_params(l, sp, prep):
    lb, mu, om = prep[l], prep[2 + 2 * l], prep[3 + 2 * l]
    pre = [_row(sp["rwkv_w0"][l]), _rows_at(sp["rwkv_w2"][l], 128, 0), _row(sp["rwkv_a0"][l]),
           _rows_at(sp["rwkv_a2"][l], 128, 32), _rows_at(sp["rwkv_g2"][l], 128, 64),
           _row(sp["rwkv_k_k"][l]), _row(sp["rwkv_k_a"][l])]
    if l:
        pre += [_row(sp["rwkv_v0"][l - 1]), _rows_at(sp["rwkv_v2"][l - 1], 128, 0)]
    return {
        "rwkv": {"taps": jnp.concatenate([mu, om], axis=0), "pre": pre,
                 "post": [_row(sp["rwkv_lnx_w"][l]), _row(sp["rwkv_lnx_b"][l]), _row(sp["rwkv_r_k"][l])]},
        "ssd": {"taps": sp["ssd_conv_w"][l].astype(F32),
                "pre": [_row(sp["ssd_conv_b"][l]), _row(sp["ssd_dt_bias"][l], 128), _row(sp["ssd_A_log"][l], 128)],
                "post": [_row(sp["ssd_D"][l], 128), _row(sp["ssd_norm_w"][l])]},
        "hgrn": {"pre": [lb], "post": [_row(sp["hgrn_norm_w"][l])]},
        "ln1": [_row(sp["ln1_w"][l]), _row(sp["ln1_b"][l])],
        "ln2": [_row(sp["ln2_w"][l]), _row(sp["ln2_b"][l])],
    }


def _mu_full(sp, l):
    parts = [sp["mu_shift"][l].reshape(1, -1)]
    if l:
        parts.append(sp["mu_vres"][l - 1].reshape(1, -1))
    return _row(jnp.concatenate(parts, axis=1), 1024)


def local_step(x, target, big, sp):
    prep_in = [sp["lower_bounds"].astype(F32), _mu_full(sp, 0), _mu_full(sp, 1)]
    prep = small_fwd(param_prep, "param_prep_fwd", prep_in,
                     [(1, D_GROUP), (1, D_GROUP), (1, 1024), (1, 1024), (1, 1024), (1, 1024)])
    pars, wts = [], []
    for l in range(DEPTH):
        pars.append(layer_params(l, sp, prep))
        vres = sp["w_in_vres"][l - 1].astype(BF16) if l else None
        wts.append({"in": _pad_w_in(big["w_in"][l], vres), "out": big["w_out"][l], "up": big["w_up"][l],
                    "down": big["w_down"][l]})
    h, vfirst, saved = x, None, []
    for l in range(DEPTH):
        h, v_l, sv = layer_fwd(l, h, wts[l], pars[l], vfirst)
        vfirst = v_l if l == 0 else vfirst
        saved.append(sv)
    loss_row, dh = loss_and_grad(h, target, "loss")
    grads, dv_extra = [None] * DEPTH, []
    for l in reversed(range(DEPTH)):
        dh, dvfirst, grads[l] = layer_bwd(l, saved[l], wts[l], pars[l], dh, dv_extra)
        dv_extra = [dvfirst] if l else []
    cts = [grads[0]["hgrn_pre"][0], grads[1]["hgrn_pre"][0]]
    for l in range(DEPTH):
        cts += [grads[l]["rwkv_taps"][0:1], grads[l]["rwkv_taps"][1:2]]
    d_lower, d_mu0, d_mu1 = small_bwd(param_prep, "param_prep_bwd", prep_in, cts)
    d_mu = [d_mu0, d_mu1]
    gb = {"w_in": [_unpad_w_in(grads[l]["in"]) for l in range(DEPTH)], "w_out": [grads[l]["out"] for l in range(DEPTH)],
          "w_up": [grads[l]["up"] for l in range(DEPTH)], "w_down": [grads[l]["down"] for l in range(DEPTH)]}
    st = lambda f: jnp.stack([f(l) for l in range(DEPTH)])
    g1 = grads[1]
    gs = {
        "lower_bounds": d_lower,
        "w_in_vres": g1["in"][None, :, VRES_COL:VRES_COL + 32],
        "mu_shift": st(lambda l: d_mu[l][0, :896]),
        "mu_vres": d_mu[1][:, 896:928],
        "rwkv_w0": st(lambda l: grads[l]["rwkv_pre"][0][0]),
        "rwkv_w2": st(lambda l: grads[l]["rwkv_pre"][1][0:32]),
        "rwkv_a0": st(lambda l: grads[l]["rwkv_pre"][2][0]),
        "rwkv_a2": st(lambda l: grads[l]["rwkv_pre"][3][32:64]),
        "rwkv_g2": st(lambda l: grads[l]["rwkv_pre"][4][64:128]),
        "rwkv_k_k": st(lambda l: grads[l]["rwkv_pre"][5][0]),
        "rwkv_k_a": st(lambda l: grads[l]["rwkv_pre"][6][0]),
        "rwkv_r_k": st(lambda l: grads[l]["rwkv_post"][2].reshape(N_HEADS, HEAD_DIM)),
        "rwkv_lnx_w": st(lambda l: grads[l]["rwkv_post"][0][0]),
        "rwkv_lnx_b": st(lambda l: grads[l]["rwkv_post"][1][0]),
        "rwkv_v0": g1["rwkv_pre"][7],
        "rwkv_v2": g1["rwkv_pre"][8][None, 0:32],
        "ssd_conv_w": st(lambda l: grads[l]["ssd_taps"]),
        "ssd_conv_b": st(lambda l: grads[l]["ssd_pre"][0][0]),
        "ssd_dt_bias": st(lambda l: grads[l]["ssd_pre"][1][0, :N_HEADS]),
        "ssd_A_log": st(lambda l: grads[l]["ssd_pre"][2][0, :N_HEADS]),
        "ssd_D": st(lambda l: grads[l]["ssd_post"][0][0, :N_HEADS]),
        "ssd_norm_w": st(lambda l: grads[l]["ssd_post"][1][0]),
        "hgrn_norm_w": st(lambda l: grads[l]["hgrn_post"][0][0]),
        "ln1_w": st(lambda l: grads[l]["ln1"][0][0]),
        "ln1_b": st(lambda l: grads[l]["ln1"][1][0]),
        "ln2_w": st(lambda l: grads[l]["ln2"][0][0]),
        "ln2_b": st(lambda l: grads[l]["ln2"][1][0]),
    }
    return loss_row, dh, gb, gs


def _pack(vecs):
    flat, meta, pos = [], [], 0
    for v in vecs:
        flat.append(v.reshape(-1).astype(F32))
        meta.append((pos, v.shape))
        pos += v.size
    total = -(-pos // 1024) * 1024
    flat.append(jnp.zeros((total - pos,), F32))
    return jnp.concatenate(flat).reshape(total // 128, 128), meta


def _unpack(packed, meta):
    flat = packed.reshape(-1)
    return [flat[off:off + math.prod(shape)].reshape(shape) for off, shape in meta]


def _to_shards(name, g):
    if name == "w_in":
        return jnp.transpose(g.reshape(g.shape[0], 4, g.shape[1] // 4), (1, 0, 2))
    if name == "w_up":
        return jnp.transpose(g.reshape(g.shape[0], 4, g.shape[1] // 4), (1, 0, 2))
    return g.reshape(4, g.shape[0] // 4, g.shape[1])


def _from_chips(name, g):
    if name in ("w_in", "w_up"):
        return jnp.transpose(g, (1, 2, 0, 3)).reshape(g.shape[1], g.shape[2], 4 * g.shape[3])
    return jnp.transpose(g, (1, 0, 2, 3)).reshape(g.shape[1], 4 * g.shape[2], g.shape[3])


INPUT_NAMES = ("x",) + WEIGHTS + ("loss_target",) + tuple("m_" + n for n in WEIGHTS) + tuple("v_" + n for n in WEIGHTS)


def _step(*args):
    a = dict(zip(INPUT_NAMES, args, strict=True))
    chip = 2 * lax.axis_index("x") + lax.axis_index("y")

    sharded_names = list(SMALL_SHARDED)
    small_pack, small_meta = _pack([a[n] for n in sharded_names])
    gathered = gather_chips([a[n].astype(BF16) for n in BIG], small_pack, "gather_weights")
    big = {n: _from_chips(n, g) for n, g in zip(BIG, gathered)}
    sp = {n: a[n] for n in SMALL if n not in SMALL_SHARDED}
    per_chip = [_unpack(gathered[-1][s], small_meta) for s in range(4)]
    for j, n in enumerate(sharded_names):
        sp[n] = jnp.concatenate([per_chip[s][j] for s in range(4)], axis=SMALL_SHARDED[n])

    loss_row, gx, gb, gs = local_step(a["x"][0], a["loss_target"][0], big, sp)

    partials = [jnp.stack([_to_shards(n, gb[n][l]) for l in range(DEPTH)]) for n in BIG]
    slots = scatter_partials([p.astype(BF16) for p in partials], "reduce_big")
    core = lax.axis_index("c")
    ids = jnp.stack([core, chip, 2 * chip + core]).astype(jnp.int32)
    mine = [sum_slots(sl, p, ids, f"sum_{n}") for n, sl, p in zip(BIG, slots, partials)]
    summed = sibling_exchange(mine, "exchange_big")
    out_g, out_d, out_m, out_v = {}, {}, {}, {}
    for n, g in zip(BIG, summed):
        shape = a[n].shape
        flat = lambda t: t.reshape(shape[0] * shape[1], shape[2])
        d, nm, nv = adamw(flat(a[n]), flat(g), flat(a["m_" + n]), flat(a["v_" + n]), f"adamw_{n}")
        out_g[n], out_d[n], out_m[n], out_v[n] = g.reshape(shape), d.reshape(shape), nm.reshape(shape), nv.reshape(shape)

    vec, meta = _pack([loss_row] + [gs[n] for n in SMALL])
    total = sum_parts(gather_devices(vec, "gather_small"), "sum_small")
    parts = _unpack(total, meta)
    loss = parts[0][0, 0]
    g_small = {}
    for n, g in zip(SMALL, parts[1:]):
        if n in SMALL_SHARDED:
            ax = SMALL_SHARDED[n]
            size = a[n].shape[ax]
            g = lax.dynamic_slice_in_dim(g, chip * size, size, axis=ax)
        g_small[n] = g
    pw, pmeta = _pack([a[n] for n in SMALL])
    pg, _ = _pack([g_small[n] for n in SMALL])
    pm, _ = _pack([a["m_" + n] for n in SMALL])
    pv, _ = _pack([a["v_" + n] for n in SMALL])
    d, nm, nv = adamw(pw, pg, pm, pv, "adamw_small")
    for n, dd, mm, vv in zip(SMALL, _unpack(d, pmeta), _unpack(nm, pmeta), _unpack(nv, pmeta)):
        out_g[n], out_d[n], out_m[n], out_v[n] = g_small[n], dd, mm, vv

    return (loss, gx[None], *[out_g[n] for n in WEIGHTS], *[out_d[n] for n in WEIGHTS],
            *[out_m[n] for n in WEIGHTS], *[out_v[n] for n in WEIGHTS])


def kernel(x, lower_bounds, w_in, w_in_vres, mu_shift, mu_vres, rwkv_w0, rwkv_w2, rwkv_a0, rwkv_a2, rwkv_g2, rwkv_k_k, rwkv_k_a, rwkv_r_k, rwkv_lnx_w, rwkv_lnx_b, rwkv_v0, rwkv_v2, ssd_conv_w, ssd_conv_b, ssd_dt_bias, ssd_A_log, ssd_D, ssd_norm_w, hgrn_norm_w, w_out, ln1_w, ln1_b, w_up, w_down, ln2_w, ln2_b, loss_target, m_lower_bounds, m_w_in, m_w_in_vres, m_mu_shift, m_mu_vres, m_rwkv_w0, m_rwkv_w2, m_rwkv_a0, m_rwkv_a2, m_rwkv_g2, m_rwkv_k_k, m_rwkv_k_a, m_rwkv_r_k, m_rwkv_lnx_w, m_rwkv_lnx_b, m_rwkv_v0, m_rwkv_v2, m_ssd_conv_w, m_ssd_conv_b, m_ssd_dt_bias, m_ssd_A_log, m_ssd_D, m_ssd_norm_w, m_hgrn_norm_w, m_w_out, m_ln1_w, m_ln1_b, m_w_up, m_w_down, m_ln2_w, m_ln2_b, v_lower_bounds, v_w_in, v_w_in_vres, v_mu_shift, v_mu_vres, v_rwkv_w0, v_rwkv_w2, v_rwkv_a0, v_rwkv_a2, v_rwkv_g2, v_rwkv_k_k, v_rwkv_k_a, v_rwkv_r_k, v_rwkv_lnx_w, v_rwkv_lnx_b, v_rwkv_v0, v_rwkv_v2, v_ssd_conv_w, v_ssd_conv_b, v_ssd_dt_bias, v_ssd_A_log, v_ssd_D, v_ssd_norm_w, v_hgrn_norm_w, v_w_out, v_ln1_w, v_ln1_b, v_w_up, v_w_down, v_ln2_w, v_ln2_b):
    return _step(x, lower_bounds, w_in, w_in_vres, mu_shift, mu_vres, rwkv_w0, rwkv_w2, rwkv_a0, rwkv_a2, rwkv_g2, rwkv_k_k, rwkv_k_a, rwkv_r_k, rwkv_lnx_w, rwkv_lnx_b, rwkv_v0, rwkv_v2, ssd_conv_w, ssd_conv_b, ssd_dt_bias, ssd_A_log, ssd_D, ssd_norm_w, hgrn_norm_w, w_out, ln1_w, ln1_b, w_up, w_down, ln2_w, ln2_b, loss_target, m_lower_bounds, m_w_in, m_w_in_vres, m_mu_shift, m_mu_vres, m_rwkv_w0, m_rwkv_w2, m_rwkv_a0, m_rwkv_a2, m_rwkv_g2, m_rwkv_k_k, m_rwkv_k_a, m_rwkv_r_k, m_rwkv_lnx_w, m_rwkv_lnx_b, m_rwkv_v0, m_rwkv_v2, m_ssd_conv_w, m_ssd_conv_b, m_ssd_dt_bias, m_ssd_A_log, m_ssd_D, m_ssd_norm_w, m_hgrn_norm_w, m_w_out, m_ln1_w, m_ln1_b, m_w_up, m_w_down, m_ln2_w, m_ln2_b, v_lower_bounds, v_w_in, v_w_in_vres, v_mu_shift, v_mu_vres, v_rwkv_w0, v_rwkv_w2, v_rwkv_a0, v_rwkv_a2, v_rwkv_g2, v_rwkv_k_k, v_rwkv_k_a, v_rwkv_r_k, v_rwkv_lnx_w, v_rwkv_lnx_b, v_rwkv_v0, v_rwkv_v2, v_ssd_conv_w, v_ssd_conv_b, v_ssd_dt_bias, v_ssd_A_log, v_ssd_D, v_ssd_norm_w, v_hgrn_norm_w, v_w_out, v_ln1_w, v_ln1_b, v_w_up, v_w_down, v_ln2_w, v_ln2_b)
```

```python
import functools
import math

import jax
import jax.numpy as jnp
from jax import lax
from jax.experimental import pallas as pl
from jax.experimental.pallas import tpu as pltpu

F32 = jnp.float32
BF16 = jnp.bfloat16
HI = lax.Precision.HIGHEST

DEPTH = 2
D_MODEL = 1024
D_GROUP = 256
HEAD_DIM = 64
N_HEADS = 4
SSD_STATE = 128
SSD_XBC = 768
SSD_CONV = 4
D_FF = 4096
ALPHA = (2.0 * DEPTH) ** 0.25
LN_EPS = 1e-5
RMS_EPS = 1e-5
RWKV_GN_EPS = HEAD_DIM * 1e-5
DILATED_BRANCHES = ((128, 1), (512, 4), (2048, 16))
ALIBI_SLOPES = tuple(2.0 ** (-8.0 * (h + 1) / N_HEADS) for h in range(N_HEADS))
ATTN_BLK = 128

ADAM_LR, ADAM_B1, ADAM_B2, ADAM_EPS, ADAM_WD, ADAM_STEP = 0.001, 0.9, 0.999, 1e-08, 0.01, 10

IN_COLS = 3716
PROJ_W = 4096
SEG_HGRN, SEG_RWKV, SEG_Q, SEG_Z, SEG_XBC, SEG_DT = 0, 1024, 2048, 2816, 3072, 3840
_PIECES = ((0, 896, SEG_RWKV), (896, 768, SEG_Q), (1664, 256, SEG_Z), (1920, 768, SEG_XBC),
           (2688, 4, SEG_DT), (2692, 1024, SEG_HGRN))
VRES_COL = SEG_RWKV + 896

HM64 = (N_HEADS, HEAD_DIM)
HM128 = (N_HEADS, SSD_STATE)
ROW_TILE = 256
SCAN_CHUNK = 128
VMEM_LIMIT = 48 * 1024 * 1024


def _cparams(sem=None):
    if sem is None:
        return pltpu.CompilerParams(vmem_limit_bytes=VMEM_LIMIT)
    return pltpu.CompilerParams(dimension_semantics=sem, vmem_limit_bytes=VMEM_LIMIT)


def _pick(n, pref):
    for t in pref:
        if n % t == 0:
            return t
    return n


def matmul(a, b, mode, name, add=None, out_dtype=F32):
    if mode == "nn":
        (M, K), (_, N) = a.shape, b.shape
    elif mode == "nt":
        (M, K), (N, _) = a.shape, b.shape
    else:
        (K, M), (_, N) = a.shape, b.shape
    tm, tn, tk = _pick(M, (1024, 512, 256, 128)), _pick(N, (1024, 512, 256, 128)), _pick(K, (512, 256, 128))
    nk = K // tk
    dims = {"nn": (((1,), (0,)), ((), ())), "nt": (((1,), (1,)), ((), ())), "tn": (((0,), (0,)), ((), ()))}[mode]

    def body(*refs):
        if add is None:
            a_ref, b_ref, o_ref, acc = refs
            add_ref = None
        else:
            a_ref, b_ref, add_ref, o_ref, acc = refs
        k = pl.program_id(2)

        @pl.when(k == 0)
        def _():
            acc[...] = jnp.zeros_like(acc)

        acc[...] += lax.dot_general(a_ref[...].astype(BF16), b_ref[...].astype(BF16), dims,
                                    preferred_element_type=F32)

        @pl.when(k == nk - 1)
        def _():
            r = acc[...]
            if add_ref is not None:
                r = r + add_ref[...]
            o_ref[...] = r.astype(o_ref.dtype)

    a_spec = pl.BlockSpec((tk, tm), lambda i, j, k: (k, i)) if mode == "tn" else pl.BlockSpec((tm, tk), lambda i, j, k: (i, k))
    b_spec = pl.BlockSpec((tn, tk), lambda i, j, k: (j, k)) if mode == "nt" else pl.BlockSpec((tk, tn), lambda i, j, k: (k, j))
    o_spec = pl.BlockSpec((tm, tn), lambda i, j, k: (i, j))
    ins, specs = [a, b], [a_spec, b_spec]
    if add is not None:
        ins.append(add)
        specs.append(o_spec)
    return pl.pallas_call(
        body, grid=(M // tm, N // tn, nk), in_specs=specs, out_specs=o_spec,
        out_shape=jax.ShapeDtypeStruct((M, N), out_dtype), scratch_shapes=[pltpu.VMEM((tm, tn), F32)],
        compiler_params=_cparams(("parallel", "parallel", "arbitrary")), name=name)(*ins)


def _row_spec(w, tile):
    return pl.BlockSpec((tile, w), lambda i: (i, 0))


def _par_spec(shape):
    return pl.BlockSpec(shape, lambda i: (0,) * len(shape))


def _rows_spec(shape, tile):
    if len(shape) == 3:
        return pl.BlockSpec((shape[0], tile, shape[2]), lambda i: (0, i, 0))
    return _row_spec(shape[1], tile)


def _rows_shape(S, w):
    return (w[0], S, w[1]) if isinstance(w, tuple) else (S, w)


def _rows_load(ref):
    if len(ref.shape) == 3:
        return jnp.concatenate([ref[h] for h in range(ref.shape[0])], axis=1)
    return ref[...]


def _rows_store(ref, val):
    if len(ref.shape) == 3:
        w = ref.shape[2]
        for h in range(ref.shape[0]):
            ref[h] = val[:, h * w:(h + 1) * w]
    else:
        ref[...] = val


def tl_fwd(fn, name, rows, pars, out_widths, tile=ROW_TILE):
    S = rows[0].shape[-2]
    nr = len(rows)

    def body(*refs):
        ins = [_rows_load(r) for r in refs[:nr]] + [r[...] for r in refs[nr:nr + len(pars)]]
        outs = fn(*ins)
        for o_ref, o in zip(refs[nr + len(pars):], outs):
            _rows_store(o_ref, o)

    shapes = [_rows_shape(S, w) for w in out_widths]
    return pl.pallas_call(
        body, grid=(S // tile,),
        in_specs=[_rows_spec(r.shape, tile) for r in rows] + [_par_spec(p.shape) for p in pars],
        out_specs=[_rows_spec(s, tile) for s in shapes],
        out_shape=[jax.ShapeDtypeStruct(s, F32) for s in shapes],
        compiler_params=_cparams(("parallel",)), name=name)(*rows, *pars)


def tl_bwd(fn, name, rows, pars, cts, tile=ROW_TILE, row_grad=None):
    S = rows[0].shape[-2]
    nr, npar = len(rows), len(pars)
    row_grad = [True] * nr if row_grad is None else row_grad
    flat_cts = [c for group in cts for c in group]
    ncts = len(flat_cts)
    gi = [i for i in range(nr) if row_grad[i]]

    def body(*refs):
        row_v = [_rows_load(r) for r in refs[:nr]]
        par_v = [r[...] for r in refs[nr:nr + npar]]
        ct_refs = refs[nr + npar:nr + npar + ncts]
        out_refs = refs[nr + npar + ncts:]
        ct_v, pos = [], 0
        for group in cts:
            acc = _rows_load(ct_refs[pos])
            for q in range(1, len(group)):
                acc = acc + _rows_load(ct_refs[pos + q])
            pos += len(group)
            ct_v.append(acc)

        def f(diff_rows, par_vals):
            full = list(row_v)
            for idx, val in zip(gi, diff_rows):
                full[idx] = val
            return tuple(fn(*full, *par_vals))

        _, vjp = jax.vjp(f, [row_v[i] for i in gi], par_v)
        d_rows, d_pars = vjp(tuple(ct_v))
        for o_ref, g in zip(out_refs[:len(gi)], d_rows):
            _rows_store(o_ref, g)
        first = pl.program_id(0) == 0
        for o_ref, g in zip(out_refs[len(gi):], d_pars):
            @pl.when(first)
            def _(o_ref=o_ref):
                o_ref[...] = jnp.zeros_like(o_ref)
            o_ref[...] += g

    outs = pl.pallas_call(
        body, grid=(S // tile,),
        in_specs=[_rows_spec(r.shape, tile) for r in rows] + [_par_spec(p.shape) for p in pars]
        + [_rows_spec(c.shape, tile) for c in flat_cts],
        out_specs=[_rows_spec(rows[i].shape, tile) for i in gi] + [_par_spec(p.shape) for p in pars],
        out_shape=[jax.ShapeDtypeStruct(rows[i].shape, F32) for i in gi] + [jax.ShapeDtypeStruct(p.shape, F32) for p in pars],
        compiler_params=_cparams(("arbitrary",)), name=name)(*rows, *pars, *flat_cts)
    return list(outs[:len(gi)]), list(outs[len(gi):])


def _shift_rows(x, j):
    if j == 0:
        return x
    rolled = pltpu.roll(x, j, 0)
    row = lax.broadcasted_iota(jnp.int32, x.shape, 0)
    return jnp.where(row >= j, rolled, 0.0)


def _unshift_rows(x, j):
    if j == 0:
        return x
    S = x.shape[0]
    rolled = pltpu.roll(x, S - j, 0)
    row = lax.broadcasted_iota(jnp.int32, x.shape, 0)
    return jnp.where(row < S - j, rolled, 0.0)


def fir_fwd(x, taps, name):
    S, C = x.shape
    K = taps.shape[0]

    def body(x_ref, w_ref, y_ref):
        xv = x_ref[...]
        acc = jnp.zeros_like(xv)
        for k in range(K):
            acc = acc + _shift_rows(xv, K - 1 - k) * w_ref[pl.ds(k, 1), :]
        y_ref[...] = acc

    cs = pl.BlockSpec((S, 128), lambda j: (0, j))
    return pl.pallas_call(body, grid=(C // 128,), in_specs=[cs, pl.BlockSpec((K, 128), lambda j: (0, j))],
                          out_specs=cs, out_shape=jax.ShapeDtypeStruct((S, C), F32),
                          compiler_params=_cparams(("parallel",)), name=name)(x, taps)


def fir_bwd(x, taps, dy_list, name):
    S, C = x.shape
    K = taps.shape[0]
    n = len(dy_list)

    def body(*refs):
        x_ref, w_ref = refs[:2]
        dy = refs[2][...]
        for q in range(1, n):
            dy = dy + refs[2 + q][...]
        dx_ref, dw_ref, db_ref = refs[2 + n:]
        xv = x_ref[...]
        dx = jnp.zeros_like(xv)
        for k in range(K):
            j = K - 1 - k
            dx = dx + _unshift_rows(dy, j) * w_ref[pl.ds(k, 1), :]
            dw_ref[pl.ds(k, 1), :] = jnp.sum(dy * _shift_rows(xv, j), axis=0, keepdims=True)
        dx_ref[...] = dx
        db_ref[...] = jnp.sum(dy, axis=0, keepdims=True)

    cs = pl.BlockSpec((S, 128), lambda j: (0, j))
    ks = pl.BlockSpec((K, 128), lambda j: (0, j))
    bs = pl.BlockSpec((1, 128), lambda j: (0, j))
    return pl.pallas_call(body, grid=(C // 128,), in_specs=[cs, ks] + [cs] * n, out_specs=[cs, ks, bs],
                          out_shape=[jax.ShapeDtypeStruct((S, C), F32), jax.ShapeDtypeStruct((K, C), F32),
                                     jax.ShapeDtypeStruct((1, C), F32)],
                          compiler_params=_cparams(("parallel",)), name=name)(x, taps, *dy_list)


def _col(tile, lane, t):
    return jnp.sum(jnp.where(lane == t, tile, 0.0), axis=1, keepdims=True)


def _rwkv_step(s, rv, vcol):
    sa = jnp.sum(s * (-rv[3]), axis=1, keepdims=True)
    return s * rv[1] + sa * (rv[3] * rv[4]) + vcol * rv[2], sa


def scan_fwd(r, w, k, vT, kk, a, name):
    H, S, Dk = r.shape
    Dv = vT.shape[1]
    Tc = SCAN_CHUNK
    nc = S // Tc
    rows = [r, w, k, kk, a]

    def body(*refs):
        row_refs = refs[:5]
        vT_ref, yT_ref, sall_ref, s_ref = refs[5:]

        @pl.when(pl.program_id(0) == 0)
        def _():
            s_ref[...] = jnp.zeros_like(s_ref)

        yT_ref[...] = jnp.zeros_like(yT_ref)
        lane = lax.broadcasted_iota(jnp.int32, (Dv, Tc), 1)

        def step(t, states):
            new = []
            for h in range(H):
                s = states[h]
                sall_ref[t, h] = s
                rv = [ref[h, pl.ds(t, 1), :] for ref in row_refs]
                s, _ = _rwkv_step(s, rv, _col(vT_ref[h], lane, t))
                ycol = jnp.sum(s * rv[0], axis=1, keepdims=True)
                yT_ref[h] = jnp.where(lane == t, ycol, yT_ref[h])
                new.append(s)
            return tuple(new)

        states = lax.fori_loop(0, Tc, step, tuple(s_ref[h] for h in range(H)))
        for h in range(H):
            s_ref[h] = states[h]

    rs = pl.BlockSpec((H, Tc, Dk), lambda c: (0, c, 0))
    vs = pl.BlockSpec((H, Dv, Tc), lambda c: (0, 0, c))
    yT, sall = pl.pallas_call(
        body, grid=(nc,), in_specs=[rs] * 5 + [vs],
        out_specs=[vs, pl.BlockSpec((Tc, H, Dv, Dk), lambda c: (c, 0, 0, 0))],
        out_shape=[jax.ShapeDtypeStruct((H, Dv, S), F32), jax.ShapeDtypeStruct((S, H, Dv, Dk), F32)],
        scratch_shapes=[pltpu.VMEM((H, Dv, Dk), F32)],
        compiler_params=_cparams(("arbitrary",)), name=name)(*rows, vT)
    return yT, sall


def scan_bwd(r, w, k, vT, kk, a, sall, dyT, name):
    H, S, Dk = r.shape
    Dv = vT.shape[1]
    Tc = SCAN_CHUNK
    nc = S // Tc
    rows = [r, w, k, kk, a]

    def body(*refs):
        row_refs = refs[:5]
        vT_ref, dyT_ref, sall_ref = refs[5:8]
        drow_refs = refs[8:13]
        dvT_ref, ds_ref = refs[13:]

        @pl.when(pl.program_id(0) == 0)
        def _():
            ds_ref[...] = jnp.zeros_like(ds_ref)

        dvT_ref[...] = jnp.zeros_like(dvT_ref)
        lane = lax.broadcasted_iota(jnp.int32, (Dv, Tc), 1)

        def bstep(i, carry):
            t = Tc - 1 - i
            new = []
            for h in range(H):
                ds = carry[h]
                sp = sall_ref[t, h]
                rv = [ref[h, pl.ds(t, 1), :] for ref in row_refs]
                vcol = _col(vT_ref[h], lane, t)
                dycol = _col(dyT_ref[h], lane, t)
                st, sa = _rwkv_step(sp, rv, vcol)
                drow_refs[0][h, pl.ds(t, 1), :] = jnp.sum(st * dycol, axis=0, keepdims=True)
                g = ds + dycol * rv[0]
                drow_refs[1][h, pl.ds(t, 1), :] = jnp.sum(g * sp, axis=0, keepdims=True)
                drow_refs[2][h, pl.ds(t, 1), :] = jnp.sum(g * vcol, axis=0, keepdims=True)
                dvcol = jnp.sum(g * rv[2], axis=1, keepdims=True)
                dsa = jnp.sum(g * (rv[3] * rv[4]), axis=1, keepdims=True)
                db = jnp.sum(g * sa, axis=0, keepdims=True)
                dnkk = jnp.sum(sp * dsa, axis=0, keepdims=True)
                drow_refs[3][h, pl.ds(t, 1), :] = db * rv[4] - dnkk
                drow_refs[4][h, pl.ds(t, 1), :] = db * rv[3]
                dvT_ref[h] = jnp.where(lane == t, dvcol, dvT_ref[h])
                new.append(g * rv[1] - dsa * rv[3])
            return tuple(new)

        carry = lax.fori_loop(0, Tc, bstep, tuple(ds_ref[h] for h in range(H)))
        for h in range(H):
            ds_ref[h] = carry[h]

    rs = pl.BlockSpec((H, Tc, Dk), lambda c: (0, nc - 1 - c, 0))
    vs = pl.BlockSpec((H, Dv, Tc), lambda c: (0, 0, nc - 1 - c))
    outs = pl.pallas_call(
        body, grid=(nc,),
        in_specs=[rs] * 5 + [vs, vs, pl.BlockSpec((Tc, H, Dv, Dk), lambda c: (nc - 1 - c, 0, 0, 0))],
        out_specs=[rs] * 5 + [vs],
        out_shape=[jax.ShapeDtypeStruct((H, S, Dk), F32)] * 5 + [jax.ShapeDtypeStruct((H, Dv, S), F32)],
        scratch_shapes=[pltpu.VMEM((H, Dv, Dk), F32)],
        compiler_params=_cparams(("arbitrary",)), name=name)(*rows, vT, dyT, sall)
    return list(outs[:5]), outs[5]


CHUNK = 128


def chunk_fwd(fn, name, blocks, state_shape, out_width):
    H, S, _ = blocks[0].shape
    nc = S // CHUNK
    nb = len(blocks)

    def body(*refs):
        o_ref, sv_ref, st = refs[nb:]

        @pl.when(pl.program_id(1) == 0)
        def _():
            st[...] = jnp.zeros_like(st)

        s0 = st[...]
        sv_ref[0, 0] = s0
        s1, out = fn(s0, *[r[0] for r in refs[:nb]])
        st[...] = s1
        o_ref[0] = out

    spec = lambda w: pl.BlockSpec((1, CHUNK, w), lambda h, c: (h, c, 0))
    return pl.pallas_call(
        body, grid=(H, nc), in_specs=[spec(b.shape[2]) for b in blocks],
        out_specs=[spec(out_width), pl.BlockSpec((1, 1) + state_shape, lambda h, c: (h, c, 0, 0))],
        out_shape=[jax.ShapeDtypeStruct((H, S, out_width), F32), jax.ShapeDtypeStruct((H, nc) + state_shape, F32)],
        scratch_shapes=[pltpu.VMEM(state_shape, F32)],
        compiler_params=_cparams(("parallel", "arbitrary")), name=name)(*blocks)


def chunk_bwd(fn, name, blocks, states, dout):
    H, S, _ = blocks[0].shape
    nc = S // CHUNK
    nb = len(blocks)
    state_shape = states.shape[2:]

    def body(*refs):
        sv_ref, do_ref = refs[nb], refs[nb + 1]
        d_refs = refs[nb + 2:2 * nb + 2]
        dst = refs[2 * nb + 2]

        @pl.when(pl.program_id(1) == 0)
        def _():
            dst[...] = jnp.zeros_like(dst)

        _, vjp = jax.vjp(fn, sv_ref[0, 0], *[r[0] for r in refs[:nb]])
        grads = vjp((dst[...], do_ref[0]))
        dst[...] = grads[0]
        for d_ref, g in zip(d_refs, grads[1:]):
            d_ref[0] = g

    spec = lambda w: pl.BlockSpec((1, CHUNK, w), lambda h, c: (h, nc - 1 - c, 0))
    return pl.pallas_call(
        body, grid=(H, nc),
        in_specs=[spec(b.shape[2]) for b in blocks]
        + [pl.BlockSpec((1, 1) + state_shape, lambda h, c: (h, nc - 1 - c, 0, 0)), spec(dout.shape[2])],
        out_specs=[spec(b.shape[2]) for b in blocks],
        out_shape=[jax.ShapeDtypeStruct(b.shape, F32) for b in blocks],
        scratch_shapes=[pltpu.VMEM(state_shape, F32)],
        compiler_params=_cparams(("parallel", "arbitrary")), name=name)(*blocks, states, dout)


def _bdot(a, b, dims):
    return lax.dot_general(a.astype(BF16), b.astype(BF16), (dims, ((), ())), preferred_element_type=F32)


def ssd_chunk(state, cb, bb, da, xdt):
    T = cb.shape[0]
    ti = lax.broadcasted_iota(jnp.int32, (T, T), 0)
    si = lax.broadcasted_iota(jnp.int32, (T, T), 1)
    mask = ti >= si
    cs = jnp.dot(mask.astype(F32), da, precision=HI, preferred_element_type=F32)
    pick = (lax.broadcasted_iota(jnp.int32, cs.shape, 1) == 0).astype(F32)
    cs_row = lax.dot_general(pick, cs, (((1,), (1,)), ((), ())), precision=HI, preferred_element_type=F32)
    lmat = jnp.where(mask, jnp.exp(jnp.where(mask, cs - cs_row, 0.0)), 0.0)
    scores = _bdot(cb, bb, ((1,), (1,))) * lmat
    y = _bdot(scores, xdt, ((1,), (0,))) + _bdot(cb, state, ((1,), (1,))) * jnp.exp(cs[:, :HEAD_DIM])
    last = cs[T - 1:T, :]
    new_state = state * jnp.exp(last) + _bdot(xdt, bb * jnp.exp(last - cs), ((0,), (0,)))
    return new_state, y


HGRN_SUB = 16


def hgrn_chunk(state, q, k, lf, v):
    T, C = q.shape[0], HGRN_SUB
    ti = lax.broadcasted_iota(jnp.int32, (C, C), 0)
    si = lax.broadcasted_iota(jnp.int32, (C, C), 1)
    tril = (ti >= si).astype(F32)
    row = lax.broadcasted_iota(jnp.int32, (C, q.shape[1]), 0)
    outs = []
    for j in range(T // C):
        qj, kj, lj, vj = (a[j * C:(j + 1) * C] for a in (q, k, lf, v))
        b = jnp.dot(tril, lj, precision=HI, preferred_element_type=F32)
        o = _bdot(qj * jnp.exp(b), state, ((1,), (1,)))
        for s in range(C):
            m = row >= s
            e = jnp.where(m, jnp.exp(jnp.where(m, b - b[s:s + 1], 0.0)), 0.0)
            o = o + jnp.sum(qj * kj[s:s + 1] * e, axis=1, keepdims=True) * vj[s:s + 1]
        last = b[C - 1:C]
        state = state * jnp.exp(last) + _bdot(vj, kj * jnp.exp(last - b), ((0,), (0,)))
        outs.append(o)
    return state, jnp.concatenate(outs, axis=0)


def _to_headsT(x):
    S = x.shape[0]
    return jnp.transpose(x.reshape(S, N_HEADS, HEAD_DIM), (1, 2, 0))


def _from_headsT(x):
    H, dv, S = x.shape
    return jnp.transpose(x, (2, 0, 1)).reshape(S, H * dv)


def _attn_block(q, kp, kc, vp, vc, n, slope, dilation):
    blk = ATTN_BLK
    k2 = jnp.concatenate([kp, kc], axis=0)
    v2 = jnp.concatenate([vp, vc], axis=0)
    s = _bdot(q, k2, ((1,), (1,))) * (HEAD_DIM ** -0.5)
    i = lax.broadcasted_iota(jnp.int32, (blk, 2 * blk), 0)
    j = lax.broadcasted_iota(jnp.int32, (blk, 2 * blk), 1)
    dist = blk + i - j
    first_key = jnp.where(n > 0, 0, blk)
    valid = (dist >= 0) & (dist <= blk) & (j >= first_key)
    s = s - slope * (dist * dilation).astype(F32)
    s = jnp.where(valid, s, -1e30)
    m = jnp.max(s, axis=-1, keepdims=True)
    p = jnp.exp(s - m)
    l = jnp.sum(p, axis=-1, keepdims=True)
    o = _bdot(p, v2, ((1,), (0,))) / l
    lse = jnp.broadcast_to(m + jnp.log(l), o.shape)
    return o, lse


_QCOL = SEG_Q // 128
PAIR = 2 * HEAD_DIM


def _attn_specs(rows):
    cur = lambda j: pl.BlockSpec((rows, PAIR), lambda p, n: (n, j + p))
    prev = lambda j: pl.BlockSpec((rows, PAIR), lambda p, n: (jnp.maximum(n - 1, 0), j + p))
    return cur, prev


def _pair_slope(pair, h):
    return jnp.where(pair == 0, jnp.float32(ALIBI_SLOPES[h]), jnp.float32(ALIBI_SLOPES[2 + h]))


def _halves(t):
    return [t[:, h * HEAD_DIM:(h + 1) * HEAD_DIM] for h in range(2)]


def _for_classes(dilation, step):
    if dilation == 1:
        step(0)
    else:
        lax.fori_loop(0, dilation, lambda z, c: (step(z), c)[1], 0)


def attn_fwd(proj, dilation, name):
    S = proj.shape[0]
    blk = ATTN_BLK
    rows = blk * dilation
    cur, prev = _attn_specs(rows)

    def body(q_ref, kp_ref, kc_ref, vp_ref, vc_ref, o_ref, l_ref):
        pair, n = pl.program_id(0), pl.program_id(1)

        def one_class(z):
            sel = pl.ds(z, blk, stride=dilation) if dilation > 1 else pl.ds(0, blk)
            q, kp, kc, vp, vc = (_halves(r[sel, :]) for r in (q_ref, kp_ref, kc_ref, vp_ref, vc_ref))
            res = [_attn_block(q[h], kp[h], kc[h], vp[h], vc[h], n, _pair_slope(pair, h), dilation) for h in range(2)]
            o_ref[sel, :] = jnp.concatenate([r[0] for r in res], axis=1)
            l_ref[sel, :] = jnp.concatenate([r[1] for r in res], axis=1)

        _for_classes(dilation, one_class)

    return pl.pallas_call(
        body, grid=(2, S // rows),
        in_specs=[cur(_QCOL), prev(_QCOL + 2), cur(_QCOL + 2), prev(_QCOL + 4), cur(_QCOL + 4)],
        out_specs=[cur(0), cur(0)], out_shape=[jax.ShapeDtypeStruct((S, D_GROUP), F32)] * 2,
        compiler_params=_cparams(("parallel", "arbitrary")), name=name)(proj, proj, proj, proj, proj)


def attn_bwd(proj, do, dlse, dilation, name):
    S = proj.shape[0]
    blk = ATTN_BLK
    rows = blk * dilation
    cur, prev = _attn_specs(rows)
    full = pl.BlockSpec((S, PAIR), lambda p, n: (0, p))

    def body(q_ref, kp_ref, kc_ref, vp_ref, vc_ref, do_ref, dl_ref, dq_ref, dk_ref, dv_ref):
        pair, n = pl.program_id(0), pl.program_id(1)

        @pl.when(n == 0)
        def _():
            dk_ref[...] = jnp.zeros_like(dk_ref)
            dv_ref[...] = jnp.zeros_like(dv_ref)

        def one_class(z):
            sel = pl.ds(z, blk, stride=dilation) if dilation > 1 else pl.ds(0, blk)
            q, kp, kc, vp, vc, do_v, dl_v = (_halves(r[sel, :]) for r in
                                             (q_ref, kp_ref, kc_ref, vp_ref, vc_ref, do_ref, dl_ref))
            grads = []
            for h in range(2):
                f = lambda q_, kp_, kc_, vp_, vc_, h=h: _attn_block(q_, kp_, kc_, vp_, vc_, n, _pair_slope(pair, h), dilation)
                _, vjp = jax.vjp(f, q[h], kp[h], kc[h], vp[h], vc[h])
                grads.append(vjp((do_v[h], dl_v[h])))
            both = lambda j: jnp.concatenate([grads[0][j], grads[1][j]], axis=1)
            dq_ref[sel, :] = both(0)
            if dilation > 1:
                here = pl.ds(n * rows + z, blk, stride=dilation)
                before = pl.ds(jnp.maximum(n - 1, 0) * rows + z, blk, stride=dilation)
            else:
                here = pl.ds(pl.multiple_of(n * blk, blk), blk)
                before = pl.ds(pl.multiple_of(jnp.maximum(n - 1, 0) * blk, blk), blk)
            dk_ref[here, :] = dk_ref[here, :] + both(2)
            dv_ref[here, :] = dv_ref[here, :] + both(4)
            dk_ref[before, :] = dk_ref[before, :] + both(1)
            dv_ref[before, :] = dv_ref[before, :] + both(3)

        _for_classes(dilation, one_class)

    return pl.pallas_call(
        body, grid=(2, S // rows),
        in_specs=[cur(_QCOL), prev(_QCOL + 2), cur(_QCOL + 2), prev(_QCOL + 4), cur(_QCOL + 4), cur(0), cur(0)],
        out_specs=[cur(0), full, full], out_shape=[jax.ShapeDtypeStruct((S, D_GROUP), F32)] * 3,
        compiler_params=_cparams(("parallel", "arbitrary")), name=name)(proj, proj, proj, proj, proj, do, dlse)


def _head_ones(width, group):
    i = lax.broadcasted_iota(jnp.int32, (width, width), 0) // group
    j = lax.broadcasted_iota(jnp.int32, (width, width), 1) // group
    return (i == j).astype(F32)


def _group_sum(x, group):
    return jnp.dot(x, _head_ones(x.shape[1], group), precision=HI, preferred_element_type=F32)


def _spread(width_in, width_out, rep):
    i = lax.broadcasted_iota(jnp.int32, (width_in, width_out), 0)
    j = lax.broadcasted_iota(jnp.int32, (width_in, width_out), 1) // rep
    return (i == j).astype(F32)


def _hdot(a, b):
    return jnp.dot(a, b, precision=HI, preferred_element_type=F32)


def _sigmoid(x):
    return 1.0 / (1.0 + jnp.exp(-x))


def _softplus(x):
    return jnp.maximum(x, 0.0) + jnp.log(1.0 + jnp.exp(jnp.minimum(x, -x)))


def _silu(x):
    return x * _sigmoid(x)


def rwkv_pre(layer):
    def fn(*args):
        if layer == 0:
            fs, w0, w2p, a0, a2p, g2p, k_k, k_a = args
        else:
            fs, vfirst, w0, w2p, a0, a2p, g2p, k_k, k_a, v0, v2p = args
        r, k, v = fs[:, 0:256], fs[:, 256:512], fs[:, 512:768]
        lora = fs[:, 768:896]
        w_log = -_softplus(-(w0 + _hdot(jnp.tanh(lora), w2p))) - 0.5
        decay = jnp.exp(-jnp.exp(w_log))
        a = _sigmoid(a0 + _hdot(lora, a2p))
        g = _hdot(_sigmoid(lora), g2p)
        if layer > 0:
            v = v + (vfirst - v) * _sigmoid(v0 + _hdot(fs[:, 896:1024], v2p))
        kk = k * k_k
        kk = kk / jnp.maximum(jnp.sqrt(_group_sum(kk * kk, HEAD_DIM)), 1e-12)
        k = k * (1.0 + (a - 1.0) * k_a)
        return r, decay, k, v, kk, a, g
    return fn


def rwkv_post(y, r, k, v, g, lnx_w, lnx_b, r_k):
    mu = _group_sum(y, HEAD_DIM) * (1.0 / HEAD_DIM)
    yc = y - mu
    var = _group_sum(yc * yc, HEAD_DIM) * (1.0 / HEAD_DIM)
    yn = yc * lax.rsqrt(var + RWKV_GN_EPS) * lnx_w + lnx_b
    bonus = _group_sum(r * k * r_k, HEAD_DIM) * v
    return ((yn + bonus) * g,)


def attn_combine(o1, o2, o3, l1, l2, l3):
    m = jnp.maximum(jnp.maximum(l1, l2), l3)
    e1, e2, e3 = jnp.exp(l1 - m), jnp.exp(l2 - m), jnp.exp(l3 - m)
    return ((o1 * e1 + o2 * e2 + o3 * e3) / (e1 + e2 + e3),)


def ssd_pre(xc, dtr, conv_b, dt_bias, a_log):
    xbc = _silu(xc + conv_b)
    xs, bm, cm = xbc[:, 0:256], xbc[:, 256:512], xbc[:, 512:768]
    dt = _softplus(dtr + dt_bias)
    a_neg = -jnp.exp(a_log)
    wide = _spread(128, N_HEADS * SSD_STATE, SSD_STATE)
    w = _hdot(dt, wide) * _hdot(a_neg, wide)
    xdt = xs * _hdot(dt, _spread(128, D_GROUP, HEAD_DIM))
    rr = jnp.concatenate([cm[:, 0:128], cm[:, 0:128], cm[:, 128:256], cm[:, 128:256]], axis=1)
    kk = jnp.concatenate([bm[:, 0:128], bm[:, 0:128], bm[:, 128:256], bm[:, 128:256]], axis=1)
    return rr, w, kk, xdt, xs


def ssd_post(ys, z, xs, d_skip, norm_w):
    y = ys + xs * _hdot(d_skip, _spread(128, D_GROUP, HEAD_DIM))
    y = y * _silu(z)
    half = D_GROUP // 2
    parts = []
    for g in range(2):
        t = y[:, g * half:(g + 1) * half]
        parts.append(t * lax.rsqrt(jnp.mean(t * t, axis=-1, keepdims=True) + RMS_EPS))
    return (jnp.concatenate(parts, axis=1) * norm_w,)


def hgrn_pre(seg, lb):
    q, f, i = seg[:, 0:256], seg[:, 256:512], seg[:, 512:768]
    forget = lb + (1.0 - lb) * _sigmoid(f)
    return _silu(q), 1.0 - forget, jnp.log(forget), i


def hgrn_post(o, seg, norm_w):
    g = seg[:, 768:1024]
    ms = _group_sum(o * o, HEAD_DIM) * (1.0 / HEAD_DIM)
    return (o * lax.rsqrt(ms + RMS_EPS) * norm_w * _silu(g),)


def ln_res(x, y, w, b):
    z = ALPHA * x + y
    mu = jnp.mean(z, axis=-1, keepdims=True)
    zc = z - mu
    var = jnp.mean(zc * zc, axis=-1, keepdims=True)
    return (zc * lax.rsqrt(var + LN_EPS) * w + b,)


def relu2(u):
    r = jnp.maximum(u, 0.0)
    return (r * r,)


def loss_and_grad(y, tgt, name):
    S, D = y.shape
    tile = ROW_TILE

    def body(y_ref, t_ref, l_ref, dy_ref):
        e = y_ref[...] - t_ref[...]
        dy_ref[...] = e * (1.0 / D)

        @pl.when(pl.program_id(0) == 0)
        def _():
            l_ref[...] = jnp.zeros_like(l_ref)

        per_row = 0.5 * jnp.mean(e * e, axis=-1, keepdims=True)
        l_ref[...] += jnp.sum(per_row, axis=0, keepdims=True) * jnp.ones((1, 128), F32)

    return pl.pallas_call(body, grid=(S // tile,), in_specs=[_row_spec(D, tile)] * 2,
                          out_specs=[_par_spec((1, 128)), _row_spec(D, tile)],
                          out_shape=[jax.ShapeDtypeStruct((1, 128), F32), jax.ShapeDtypeStruct((S, D), F32)],
                          compiler_params=_cparams(("arbitrary",)), name=name)(y, tgt)


def add_rows(arrs, name):
    (out,) = tl_fwd(lambda *a: (functools.reduce(lambda p, q: p + q, a),), name, arrs, [], [arrs[0].shape[1]])
    return out


def small_fwd(fn, name, ins, out_shapes):
    n = len(ins)

    def body(*refs):
        outs = fn(*[r[...] for r in refs[:n]])
        for o_ref, o in zip(refs[n:], outs):
            o_ref[...] = o

    return pl.pallas_call(body, out_shape=[jax.ShapeDtypeStruct(s, F32) for s in out_shapes], name=name)(*ins)


def small_bwd(fn, name, ins, cts):
    n, m = len(ins), len(cts)

    def body(*refs):
        _, vjp = jax.vjp(lambda *a: tuple(fn(*a)), *[r[...] for r in refs[:n]])
        grads = vjp(tuple(r[...] for r in refs[n:n + m]))
        for o_ref, g in zip(refs[n + m:], grads):
            o_ref[...] = g

    return pl.pallas_call(body, out_shape=[jax.ShapeDtypeStruct(a.shape, F32) for a in ins], name=name)(*ins, *cts)


def param_prep(lower_bounds, mu0, mu1):
    e = jnp.exp(lower_bounds - jnp.max(lower_bounds, axis=0, keepdims=True))
    sm = e / jnp.sum(e, axis=0, keepdims=True)
    lb0 = sm[0:1] - sm[0:1]
    lb1 = sm[0:1] + sm[1:2] - sm[0:1]
    return lb0, lb1, mu0, 1.0 - mu0, mu1, 1.0 - mu1


def _rows_tile(rows):
    return _pick(rows, (256, 128, 64, 32, 16, 8))


def sum_parts(parts, name):
    P, rows, cols = parts.shape
    tile = _rows_tile(rows)

    def body(p_ref, o_ref):
        acc = p_ref[0]
        for p in range(1, P):
            acc = acc + p_ref[p]
        o_ref[...] = acc

    return pl.pallas_call(body, grid=(rows // tile,), in_specs=[pl.BlockSpec((P, tile, cols), lambda i: (0, i, 0))],
                          out_specs=pl.BlockSpec((tile, cols), lambda i: (i, 0)),
                          out_shape=jax.ShapeDtypeStruct((rows, cols), F32),
                          compiler_params=_cparams(("parallel",)), name=name)(parts)


def pair_sum(own, got, ids, name):
    _, P, rows, cols = own.shape
    tile = _rows_tile(rows)

    def body(ids_ref, own_ref, got_ref, o_ref):
        o_ref[0] = (own_ref[0, 0] + got_ref[0]).astype(BF16)

    grid_spec = pltpu.PrefetchScalarGridSpec(
        num_scalar_prefetch=1, grid=(P, rows // tile),
        in_specs=[pl.BlockSpec((1, 1, tile, cols), lambda s, i, ids: (ids[0], s, i, 0)),
                  pl.BlockSpec((1, tile, cols), lambda s, i, ids: (s, i, 0))],
        out_specs=pl.BlockSpec((1, tile, cols), lambda s, i, ids: (s, i, 0)))
    return pl.pallas_call(body, grid_spec=grid_spec, out_shape=jax.ShapeDtypeStruct((P, rows, cols), BF16),
                          compiler_params=_cparams(("parallel", "parallel")), name=name)(ids, own, got)


def sum_chips(slots, own, got, ids, name):
    P, rows, cols = slots.shape
    tile = _rows_tile(rows)

    def body(ids_ref, s_ref, own_ref, got_ref, o_ref):
        chip = ids_ref[1]
        mine = own_ref[0, 0] + got_ref[0]
        acc = None
        for p in range(P):
            term = jnp.where(chip == p, mine, s_ref[p].astype(F32))
            acc = term if acc is None else acc + term
        o_ref[0] = acc

    grid_spec = pltpu.PrefetchScalarGridSpec(
        num_scalar_prefetch=1, grid=(rows // tile,),
        in_specs=[pl.BlockSpec((P, tile, cols), lambda i, ids: (0, i, 0)),
                  pl.BlockSpec((1, 1, tile, cols), lambda i, ids: (ids[0], ids[1], i, 0)),
                  pl.BlockSpec((1, tile, cols), lambda i, ids: (ids[1], i, 0))],
        out_specs=pl.BlockSpec((1, tile, cols), lambda i, ids: (ids[0], i, 0)))
    return pl.pallas_call(body, grid_spec=grid_spec, out_shape=jax.ShapeDtypeStruct((2, rows, cols), F32),
                          compiler_params=_cparams(("parallel",)), name=name)(ids, slots, own, got)


def adamw(w, g, m, v, name):
    rows, cols = w.shape
    tile = _rows_tile(rows)

    def body(w_ref, g_ref, m_ref, v_ref, d_ref, nm_ref, nv_ref):
        gv = g_ref[...]
        nm = ADAM_B1 * m_ref[...] + (1.0 - ADAM_B1) * gv
        nv = ADAM_B2 * v_ref[...] + (1.0 - ADAM_B2) * jnp.square(gv)
        m_hat = nm / (1.0 - ADAM_B1 ** ADAM_STEP)
        v_hat = nv / (1.0 - ADAM_B2 ** ADAM_STEP)
        d_ref[...] = -ADAM_LR * (m_hat / (jnp.sqrt(v_hat) + ADAM_EPS) + ADAM_WD * w_ref[...])
        nm_ref[...] = nm
        nv_ref[...] = nv

    spec = pl.BlockSpec((tile, cols), lambda i: (i, 0))
    return pl.pallas_call(body, grid=(rows // tile,), in_specs=[spec] * 4, out_specs=[spec] * 3,
                          out_shape=[jax.ShapeDtypeStruct((rows, cols), F32)] * 3,
                          compiler_params=_cparams(("parallel",)), name=name)(w, g, m, v)


MESH = pl.DeviceIdType.MESH
ANY = pl.BlockSpec(memory_space=pl.ANY)


def _flip(v, bit):
    return 1 - v if bit else v


_CHIP_RELATIONS = ((1, 0), (0, 1), (1, 1))


def gather_chips(arrs, small, name):
    n = len(arrs)

    def body(*refs):
        ins, small_in = refs[:n], refs[n]
        outs, small_out = refs[n + 1:2 * n + 1], refs[2 * n + 1]
        send, recv, fsend, frecv, loc, ssend, srecv = refs[2 * n + 2:]
        x, y, c = lax.axis_index("x"), lax.axis_index("y"), lax.axis_index("c")
        me = 2 * x + y
        chips = [(_flip(x, bx), _flip(y, by)) for bx, by in _CHIP_RELATIONS]

        def over_ici(i, r, block_chip):
            return pltpu.make_async_remote_copy(src_ref=ins[i].at[c], dst_ref=outs[i].at[block_chip, c],
                                                send_sem=send.at[i, r], recv_sem=recv.at[i, r],
                                                device_id=(chips[r][0], chips[r][1], c), device_id_type=MESH)

        def to_sibling(i, r, layer):
            blk = outs[i].at[2 * chips[r][0] + chips[r][1], layer]
            return pltpu.make_async_remote_copy(src_ref=blk, dst_ref=blk, send_sem=fsend.at[i, r],
                                                recv_sem=frecv.at[i, r], device_id=(x, y, 1 - c), device_id_type=MESH)

        first = [over_ici(i, r, me) for i in range(n) for r in range(3)]
        smalls = [pltpu.make_async_remote_copy(src_ref=small_in, dst_ref=small_out.at[me], send_sem=ssend.at[r],
                                               recv_sem=srecv.at[r], device_id=(chips[r][0], chips[r][1], c),
                                               device_id_type=MESH) for r in range(3)]
        for cp in first + smalls:
            cp.start()
        local = [pltpu.make_async_copy(ins[i], outs[i].at[me], loc.at[i]) for i in range(n)]
        local.append(pltpu.make_async_copy(small_in, small_out.at[me], loc.at[n]))
        for cp in local:
            cp.start()
        passed = []
        for r in range(3):
            for i in range(n):
                over_ici(i, r, 2 * chips[r][0] + chips[r][1]).wait_recv()
                fw = to_sibling(i, r, c)
                fw.start()
                passed.append(fw)
        for r in range(3):
            for i in range(n):
                to_sibling(i, r, 1 - c).wait_recv()
        for cp in first + passed:
            cp.wait_send()
        for cp in smalls + local:
            cp.wait()

    return pl.pallas_call(
        body, in_specs=[ANY] * (n + 1), out_specs=[ANY] * (n + 1),
        out_shape=[jax.ShapeDtypeStruct((4,) + a.shape, a.dtype) for a in arrs]
        + [jax.ShapeDtypeStruct((4,) + small.shape, small.dtype)],
        scratch_shapes=[pltpu.SemaphoreType.DMA((n, 3)), pltpu.SemaphoreType.DMA((n, 3)), pltpu.SemaphoreType.DMA((n, 3)),
                        pltpu.SemaphoreType.DMA((n, 3)), pltpu.SemaphoreType.DMA((n + 1,)),
                        pltpu.SemaphoreType.DMA((3,)), pltpu.SemaphoreType.DMA((3,))],
        name=name)(*arrs, small)


_RELATIONS = tuple((r >> 2 & 1, r >> 1 & 1, r & 1) for r in range(1, 8))


def gather_devices(arr, name):
    def body(in_ref, out_ref, send, recv, loc):
        x, y, c = lax.axis_index("x"), lax.axis_index("y"), lax.axis_index("c")
        me = 4 * x + 2 * y + c
        lc = pltpu.make_async_copy(in_ref, out_ref.at[me], loc)
        lc.start()
        pending = [lc]
        for r, (bx, by, bc) in enumerate(_RELATIONS):
            cp = pltpu.make_async_remote_copy(src_ref=in_ref, dst_ref=out_ref.at[me], send_sem=send.at[r],
                                              recv_sem=recv.at[r], device_id=(_flip(x, bx), _flip(y, by), _flip(c, bc)),
                                              device_id_type=MESH)
            cp.start()
            pending.append(cp)
        for cp in pending:
            cp.wait()

    return pl.pallas_call(
        body, in_specs=[ANY], out_specs=ANY, out_shape=jax.ShapeDtypeStruct((8,) + arr.shape, arr.dtype),
        scratch_shapes=[pltpu.SemaphoreType.DMA((7,)), pltpu.SemaphoreType.DMA((7,)), pltpu.SemaphoreType.DMA(())],
        name=name)(arr)


def pair_exchange(arrs, name):
    n = len(arrs)

    def body(*refs):
        ins, outs = refs[:n], refs[n:2 * n]
        send, recv = refs[2 * n:]
        x, y, c = lax.axis_index("x"), lax.axis_index("y"), lax.axis_index("c")
        pending = []
        for i in range(n):
            for s in range(4):
                cp = pltpu.make_async_remote_copy(src_ref=ins[i].at[1 - c, s], dst_ref=outs[i].at[s],
                                                  send_sem=send.at[i, s], recv_sem=recv.at[i, s],
                                                  device_id=(x, y, 1 - c), device_id_type=MESH)
                cp.start()
                pending.append(cp)
        for cp in pending:
            cp.wait()

    return pl.pallas_call(
        body, in_specs=[ANY] * n, out_specs=[ANY] * n,
        out_shape=[jax.ShapeDtypeStruct(a.shape[1:], a.dtype) for a in arrs],
        scratch_shapes=[pltpu.SemaphoreType.DMA((n, 4)), pltpu.SemaphoreType.DMA((n, 4))],
        name=name)(*arrs)


def reduce_chips(arrs, name):
    n = len(arrs)

    def body(*refs):
        ins, outs = refs[:n], refs[n:2 * n]
        send, recv, loc = refs[2 * n:]
        x, y, c = lax.axis_index("x"), lax.axis_index("y"), lax.axis_index("c")
        me = 2 * x + y
        pending = []
        for i in range(n):
            for r, (bx, by) in enumerate(_CHIP_RELATIONS):
                px, py = _flip(x, bx), _flip(y, by)
                cp = pltpu.make_async_remote_copy(src_ref=ins[i].at[2 * px + py], dst_ref=outs[i].at[me],
                                                  send_sem=send.at[i, r], recv_sem=recv.at[i, r],
                                                  device_id=(px, py, c), device_id_type=MESH)
                cp.start()
                pending.append(cp)
        for i in range(n):
            lc = pltpu.make_async_copy(ins[i].at[me], outs[i].at[me], loc.at[i])
            lc.start()
            pending.append(lc)
        for cp in pending:
            cp.wait()

    return pl.pallas_call(
        body, in_specs=[ANY] * n, out_specs=[ANY] * n,
        out_shape=[jax.ShapeDtypeStruct(a.shape, a.dtype) for a in arrs],
        scratch_shapes=[pltpu.SemaphoreType.DMA((n, 3)), pltpu.SemaphoreType.DMA((n, 3)), pltpu.SemaphoreType.DMA((n,))],
        name=name)(*arrs)


EXCHANGE_PIECES = 8


def sibling_exchange(arrs, name):
    n = len(arrs)

    def body(*refs):
        bufs = refs[n:2 * n]
        send, recv = refs[2 * n:]
        x, y, c = lax.axis_index("x"), lax.axis_index("y"), lax.axis_index("c")
        pending = []
        for i in range(n):
            rows = bufs[i].shape[1] // EXCHANGE_PIECES
            for j in range(EXCHANGE_PIECES):
                piece = bufs[i].at[c, pl.ds(j * rows, rows)]
                cp = pltpu.make_async_remote_copy(src_ref=piece, dst_ref=piece, send_sem=send.at[i, j],
                                                  recv_sem=recv.at[i, j], device_id=(x, y, 1 - c), device_id_type=MESH)
                cp.start()
                pending.append(cp)
        for i in range(n):
            rows = bufs[i].shape[1] // EXCHANGE_PIECES
            for j in range(EXCHANGE_PIECES):
                landed = bufs[i].at[1 - c, pl.ds(j * rows, rows)]
                pltpu.make_async_remote_copy(src_ref=landed, dst_ref=landed, send_sem=send.at[i, j], recv_sem=recv.at[i, j],
                                             device_id=(x, y, 1 - c), device_id_type=MESH).wait_recv()
        for cp in pending:
            cp.wait_send()

    return pl.pallas_call(
        body, in_specs=[ANY] * n, out_specs=[ANY] * n,
        out_shape=[jax.ShapeDtypeStruct(a.shape, a.dtype) for a in arrs], input_output_aliases={i: i for i in range(n)},
        scratch_shapes=[pltpu.SemaphoreType.DMA((n, EXCHANGE_PIECES)), pltpu.SemaphoreType.DMA((n, EXCHANGE_PIECES))],
        name=name)(*arrs)


def rwkv_fwd(l, seg, taps, pars, vfirst):
    fs = fir_fwd(seg, taps, f"rwkv_shift_fwd{l}")
    rows = [fs] + ([vfirst] if l else [])
    R, W, K, V, KK, A, G = tl_fwd(rwkv_pre(l), f"rwkv_pre_fwd{l}", rows, pars["pre"],
                                  [HM64, HM64, HM64, D_GROUP, HM64, HM64, D_GROUP])
    vT = _to_headsT(V)
    yT, sall = scan_fwd(R, W, K, vT, KK, A, f"rwkv_scan_fwd{l}")
    Y = _from_headsT(yT)
    (out,) = tl_fwd(rwkv_post, f"rwkv_post_fwd{l}", [Y, R, K, V, G], pars["post"], [D_GROUP])
    return out, V, (seg, taps, rows, R, W, K, V, KK, A, G, Y, vT, sall)


def rwkv_bwd(l, saved, pars, dout, dv_extra):
    seg, taps, rows, R, W, K, V, KK, A, G, Y, vT, sall = saved
    (dY, dR1, dK1, dV1, dG), dpost = tl_bwd(rwkv_post, f"rwkv_post_bwd{l}", [Y, R, K, V, G], pars["post"], [[dout]])
    (dR2, dW, dK2, dKK, dA), dvT = scan_bwd(R, W, K, vT, KK, A, sall, _to_headsT(dY), f"rwkv_scan_bwd{l}")
    cts = [[dR1, dR2], [dW], [dK1, dK2], [dV1, _from_headsT(dvT)] + dv_extra, [dKK], [dA], [dG]]
    drows, dpre = tl_bwd(rwkv_pre(l), f"rwkv_pre_bwd{l}", rows, pars["pre"], cts)
    dseg, dtaps, _ = fir_bwd(seg, taps, [drows[0]], f"rwkv_shift_bwd{l}")
    return dseg, (drows[1] if l else None), dtaps, dpre, dpost


def attn_mix_fwd(l, proj):
    os_, ls_ = [], []
    for b, (_, d) in enumerate(DILATED_BRANCHES):
        o, lse = attn_fwd(proj, d, f"attn_fwd{l}_{b}")
        os_.append(o)
        ls_.append(lse)
    (out,) = tl_fwd(attn_combine, f"attn_combine_fwd{l}", os_ + ls_, [], [D_GROUP])
    return out, (proj, os_, ls_)


def attn_mix_bwd(l, saved, dout):
    proj, os_, ls_ = saved
    drows, _ = tl_bwd(attn_combine, f"attn_combine_bwd{l}", os_ + ls_, [], [[dout]])
    grads = [attn_bwd(proj, drows[b], drows[3 + b], d, f"attn_bwd{l}_{b}") for b, (_, d) in enumerate(DILATED_BRANCHES)]
    return tuple(add_rows([g[j] for g in grads], f"attn_d{'qkv'[j]}{l}") for j in range(3))


def ssd_fwd(l, z, xbc, dtr, pars):
    xc = fir_fwd(xbc, pars["taps"], f"ssd_conv_fwd{l}")
    rr, w, kk, xdt, xs = tl_fwd(ssd_pre, f"ssd_pre_fwd{l}", [xc, dtr], pars["pre"], [HM128, HM128, HM128, HM64, D_GROUP])
    blocks = [rr, kk, w, xdt]
    ys, states = chunk_fwd(ssd_chunk, f"ssd_scan_fwd{l}", blocks, (HEAD_DIM, SSD_STATE), HEAD_DIM)
    (out,) = tl_fwd(ssd_post, f"ssd_post_fwd{l}", [ys, z, xs], pars["post"], [D_GROUP])
    return out, (z, xbc, dtr, xc, blocks, states, xs, ys)


def ssd_bwd(l, saved, pars, dout):
    z, xbc, dtr, xc, blocks, states, xs, ys = saved
    (dys, dz, dxs), dpost = tl_bwd(ssd_post, f"ssd_post_bwd{l}", [ys, z, xs], pars["post"], [[dout]])
    drr, dkk, dw, dxdt = chunk_bwd(ssd_chunk, f"ssd_scan_bwd{l}", blocks, states, dys)
    (dxc, ddtr), dpre = tl_bwd(ssd_pre, f"ssd_pre_bwd{l}", [xc, dtr], pars["pre"], [[drr], [dw], [dkk], [dxdt], [dxs]])
    dxbc, dtaps, _ = fir_bwd(xbc, pars["taps"], [dxc], f"ssd_conv_bwd{l}")
    return dz, dxbc, ddtr, dtaps, dpre, dpost


def hgrn_fwd(l, seg, pars):
    blocks = tl_fwd(hgrn_pre, f"hgrn_pre_fwd{l}", [seg], pars["pre"], [HM64] * 4)
    o, states = chunk_fwd(hgrn_chunk, f"hgrn_scan_fwd{l}", blocks, (HEAD_DIM, HEAD_DIM), HEAD_DIM)
    (out,) = tl_fwd(hgrn_post, f"hgrn_post_fwd{l}", [o, seg], pars["post"], [D_GROUP])
    return out, (seg, blocks, states, o)


def hgrn_bwd(l, saved, pars, dout):
    seg, blocks, states, o = saved
    (do, dseg1), dpost = tl_bwd(hgrn_post, f"hgrn_post_bwd{l}", [o, seg], pars["post"], [[dout]])
    dq, dkk, dlf, di = chunk_bwd(hgrn_chunk, f"hgrn_scan_bwd{l}", blocks, states, do)
    (dseg2,), dpre = tl_bwd(hgrn_pre, f"hgrn_pre_bwd{l}", [seg], pars["pre"], [[dq], [dkk], [dlf], [di]])
    return add_rows([dseg1, dseg2], f"hgrn_dseg{l}"), dpre, dpost


def layer_fwd(l, x, wts, pars, vfirst):
    proj = matmul(x, wts["in"], "nn", f"proj_fwd{l}")
    seg_h = proj[:, SEG_HGRN:SEG_HGRN + 1024]
    seg_r = proj[:, SEG_RWKV:SEG_RWKV + 1024]
    z = proj[:, SEG_Z:SEG_Z + D_GROUP]
    xbc = proj[:, SEG_XBC:SEG_XBC + SSD_XBC]
    dtr = proj[:, SEG_DT:SEG_DT + 128]
    ya, v_rwkv, sa = rwkv_fwd(l, seg_r, pars["rwkv"]["taps"], pars["rwkv"], vfirst)
    yb, sb = attn_mix_fwd(l, proj)
    yc, sc = ssd_fwd(l, z, xbc, dtr, pars["ssd"])
    yd, sd = hgrn_fwd(l, seg_h, pars["hgrn"])
    mix = jnp.concatenate([ya, yb, yc, yd], axis=1)
    mo = matmul(mix, wts["out"], "nn", f"out_fwd{l}")
    (x1,) = tl_fwd(ln_res, f"ln1_fwd{l}", [x, mo], pars["ln1"], [D_MODEL])
    u = matmul(x1, wts["up"], "nn", f"up_fwd{l}")
    (h,) = tl_fwd(relu2, f"relu2_fwd{l}", [u], [], [D_FF])
    dn = matmul(h, wts["down"], "nn", f"down_fwd{l}")
    (x2,) = tl_fwd(ln_res, f"ln2_fwd{l}", [x1, dn], pars["ln2"], [D_MODEL])
    return x2, v_rwkv, (x, sa, sb, sc, sd, mix, mo, x1, u, h, dn)


def layer_bwd(l, saved, wts, pars, dx2, dv_extra):
    x, sa, sb, sc, sd, mix, mo, x1, u, h, dn = saved
    S = x.shape[0]
    g = {}
    (dx1a, ddn), g["ln2"] = tl_bwd(ln_res, f"ln2_bwd{l}", [x1, dn], pars["ln2"], [[dx2]])
    g["down"] = matmul(h, ddn, "tn", f"down_dw{l}")
    dh = matmul(ddn, wts["down"], "nt", f"down_dx{l}")
    (du,), _ = tl_bwd(relu2, f"relu2_bwd{l}", [u], [], [[dh]])
    g["up"] = matmul(x1, du, "tn", f"up_dw{l}")
    dx1 = matmul(du, wts["up"], "nt", f"up_dx{l}", add=dx1a)
    (dxa, dmo), g["ln1"] = tl_bwd(ln_res, f"ln1_bwd{l}", [x, mo], pars["ln1"], [[dx1]])
    g["out"] = matmul(mix, dmo, "tn", f"out_dw{l}")
    dmix = matmul(dmo, wts["out"], "nt", f"out_dx{l}")
    dya, dyb, dyc, dyd = (dmix[:, j * D_GROUP:(j + 1) * D_GROUP] for j in range(4))
    dseg_r, dvfirst, g["rwkv_taps"], g["rwkv_pre"], g["rwkv_post"] = rwkv_bwd(l, sa, pars["rwkv"], dya, dv_extra)
    dq, dk, dv = attn_mix_bwd(l, sb, dyb)
    dz, dxbc, ddtr, g["ssd_taps"], g["ssd_pre"], g["ssd_post"] = ssd_bwd(l, sc, pars["ssd"], dyc)
    dseg_h, g["hgrn_pre"], g["hgrn_post"] = hgrn_bwd(l, sd, pars["hgrn"], dyd)
    dproj = jnp.concatenate([dseg_h, dseg_r, dq, dk, dv, dz, dxbc, ddtr, jnp.zeros((S, PROJ_W - SEG_DT - 128), F32)], axis=1)
    g["in"] = matmul(x, dproj, "tn", f"proj_dw{l}")
    dx = matmul(dproj, wts["in"], "nt", f"proj_dx{l}", add=dxa)
    return dx, dvfirst, g


SMALL = ("lower_bounds", "w_in_vres", "mu_shift", "mu_vres", "rwkv_w0", "rwkv_w2", "rwkv_a0", "rwkv_a2", "rwkv_g2",
         "rwkv_k_k", "rwkv_k_a", "rwkv_r_k", "rwkv_lnx_w", "rwkv_lnx_b", "rwkv_v0", "rwkv_v2", "ssd_conv_w",
         "ssd_conv_b", "ssd_dt_bias", "ssd_A_log", "ssd_D", "ssd_norm_w", "hgrn_norm_w", "ln1_w", "ln1_b", "ln2_w", "ln2_b")
BIG = ("w_in", "w_out", "w_up", "w_down")
SMALL_SHARDED = {"w_in_vres": 1, "rwkv_w2": 2, "rwkv_a2": 2, "rwkv_g2": 2, "rwkv_v2": 2, "ssd_conv_w": 2}
WEIGHTS = ("lower_bounds", "w_in", "w_in_vres", "mu_shift", "mu_vres", "rwkv_w0", "rwkv_w2", "rwkv_a0", "rwkv_a2",
           "rwkv_g2", "rwkv_k_k", "rwkv_k_a", "rwkv_r_k", "rwkv_lnx_w", "rwkv_lnx_b", "rwkv_v0", "rwkv_v2",
           "ssd_conv_w", "ssd_conv_b", "ssd_dt_bias", "ssd_A_log", "ssd_D", "ssd_norm_w", "hgrn_norm_w", "w_out",
           "ln1_w", "ln1_b", "w_up", "w_down", "ln2_w", "ln2_b")


def _row(v, width=None):
    v = v.reshape(1, -1).astype(F32)
    if width is not None and v.shape[1] < width:
        v = jnp.pad(v, ((0, 0), (0, width - v.shape[1])))
    return v


def _rows_at(m, rows, at):
    return jnp.pad(m.astype(F32), ((at, rows - at - m.shape[0]), (0, 0)))


def _pad_w_in(w_in_l, vres):
    rows = w_in_l.shape[0]
    out = []
    order = sorted(_PIECES, key=lambda p: p[2])
    pos = 0
    for start, width, at in order:
        if at > pos:
            out.append(jnp.zeros((rows, at - pos), w_in_l.dtype))
        out.append(w_in_l[:, start:start + width])
        pos = at + width
        if at == SEG_RWKV and vres is not None:
            out.append(vres.astype(w_in_l.dtype))
            pos += vres.shape[1]
    out.append(jnp.zeros((rows, PROJ_W - pos), w_in_l.dtype))
    return jnp.concatenate(out, axis=1)


def _unpad_w_in(g):
    return jnp.concatenate([g[:, at:at + width] for _, width, at in _PIECES], axis=1)


def layer_params(l, sp, prep):
    lb, mu, om = prep[l], prep[2 + 2 * l], prep[3 + 2 * l]
    pre = [_row(sp["rwkv_w0"][l]), _rows_at(sp["rwkv_w2"][l], 128, 0), _row(sp["rwkv_a0"][l]),
           _rows_at(sp["rwkv_a2"][l], 128, 32), _rows_at(sp["rwkv_g2"][l], 128, 64),
           _row(sp["rwkv_k_k"][l]), _row(sp["rwkv_k_a"][l])]
    if l:
        pre += [_row(sp["rwkv_v0"][l - 1]), _rows_at(sp["rwkv_v2"][l - 1], 128, 0)]
    return {
        "rwkv": {"taps": jnp.concatenate([mu, om], axis=0), "pre": pre,
                 "post": [_row(sp["rwkv_lnx_w"][l]), _row(sp["rwkv_lnx_b"][l]), _row(sp["rwkv_r_k"][l])]},
        "ssd": {"taps": sp["ssd_conv_w"][l].astype(F32),
                "pre": [_row(sp["ssd_conv_b"][l]), _row(sp["ssd_dt_bias"][l], 128), _row(sp["ssd_A_log"][l], 128)],
                "post": [_row(sp["ssd_D"][l], 128), _row(sp["ssd_norm_w"][l])]},
        "hgrn": {"pre": [lb], "post": [_row(sp["hgrn_norm_w"][l])]},
        "ln1": [_row(sp["ln1_w"][l]), _row(sp["ln1_b"][l])],
        "ln2": [_row(sp["ln2_w"][l]), _row(sp["ln2_b"][l])],
    }


def _mu_full(sp, l):
    parts = [sp["mu_shift"][l].reshape(1, -1)]
    if l:
        parts.append(sp["mu_vres"][l - 1].reshape(1, -1))
    return _row(jnp.concatenate(parts, axis=1), 1024)


def local_step(x, target, big, sp):
    prep_in = [sp["lower_bounds"].astype(F32), _mu_full(sp, 0), _mu_full(sp, 1)]
    prep = small_fwd(param_prep, "param_prep_fwd", prep_in,
                     [(1, D_GROUP), (1, D_GROUP), (1, 1024), (1, 1024), (1, 1024), (1, 1024)])
    pars, wts = [], []
    for l in range(DEPTH):
        pars.append(layer_params(l, sp, prep))
        vres = sp["w_in_vres"][l - 1].astype(BF16) if l else None
        wts.append({"in": _pad_w_in(big["w_in"][l], vres), "out": big["w_out"][l], "up": big["w_up"][l],
                    "down": big["w_down"][l]})
    h, vfirst, saved = x, None, []
    for l in range(DEPTH):
        h, v_l, sv = layer_fwd(l, h, wts[l], pars[l], vfirst)
        vfirst = v_l if l == 0 else vfirst
        saved.append(sv)
    loss_row, dh = loss_and_grad(h, target, "loss")
    grads, dv_extra = [None] * DEPTH, []
    for l in reversed(range(DEPTH)):
        dh, dvfirst, grads[l] = layer_bwd(l, saved[l], wts[l], pars[l], dh, dv_extra)
        dv_extra = [dvfirst] if l else []
    cts = [grads[0]["hgrn_pre"][0], grads[1]["hgrn_pre"][0]]
    for l in range(DEPTH):
        cts += [grads[l]["rwkv_taps"][0:1], grads[l]["rwkv_taps"][1:2]]
    d_lower, d_mu0, d_mu1 = small_bwd(param_prep, "param_prep_bwd", prep_in, cts)
    d_mu = [d_mu0, d_mu1]
    gb = {"w_in": [_unpad_w_in(grads[l]["in"]) for l in range(DEPTH)], "w_out": [grads[l]["out"] for l in range(DEPTH)],
          "w_up": [grads[l]["up"] for l in range(DEPTH)], "w_down": [grads[l]["down"] for l in range(DEPTH)]}
    st = lambda f: jnp.stack([f(l) for l in range(DEPTH)])
    g1 = grads[1]
    gs = {
        "lower_bounds": d_lower,
        "w_in_vres": g1["in"][None, :, VRES_COL:VRES_COL + 32],
        "mu_shift": st(lambda l: d_mu[l][0, :896]),
        "mu_vres": d_mu[1][:, 896:928],
        "rwkv_w0": st(lambda l: grads[l]["rwkv_pre"][0][0]),
        "rwkv_w2": st(lambda l: grads[l]["rwkv_pre"][1][0:32]),
        "rwkv_a0": st(lambda l: grads[l]["rwkv_pre"][2][0]),
        "rwkv_a2": st(lambda l: grads[l]["rwkv_pre"][3][32:64]),
        "rwkv_g2": st(lambda l: grads[l]["rwkv_pre"][4][64:128]),
        "rwkv_k_k": st(lambda l: grads[l]["rwkv_pre"][5][0]),
        "rwkv_k_a": st(lambda l: grads[l]["rwkv_pre"][6][0]),
        "rwkv_r_k": st(lambda l: grads[l]["rwkv_post"][2].reshape(N_HEADS, HEAD_DIM)),
        "rwkv_lnx_w": st(lambda l: grads[l]["rwkv_post"][0][0]),
        "rwkv_lnx_b": st(lambda l: grads[l]["rwkv_post"][1][0]),
        "rwkv_v0": g1["rwkv_pre"][7],
        "rwkv_v2": g1["rwkv_pre"][8][None, 0:32],
        "ssd_conv_w": st(lambda l: grads[l]["ssd_taps"]),
        "ssd_conv_b": st(lambda l: grads[l]["ssd_pre"][0][0]),
        "ssd_dt_bias": st(lambda l: grads[l]["ssd_pre"][1][0, :N_HEADS]),
        "ssd_A_log": st(lambda l: grads[l]["ssd_pre"][2][0, :N_HEADS]),
        "ssd_D": st(lambda l: grads[l]["ssd_post"][0][0, :N_HEADS]),
        "ssd_norm_w": st(lambda l: grads[l]["ssd_post"][1][0]),
        "hgrn_norm_w": st(lambda l: grads[l]["hgrn_post"][0][0]),
        "ln1_w": st(lambda l: grads[l]["ln1"][0][0]),
        "ln1_b": st(lambda l: grads[l]["ln1"][1][0]),
        "ln2_w": st(lambda l: grads[l]["ln2"][0][0]),
        "ln2_b": st(lambda l: grads[l]["ln2"][1][0]),
    }
    return loss_row, dh, gb, gs


def _pack(vecs):
    flat, meta, pos = [], [], 0
    for v in vecs:
        flat.append(v.reshape(-1).astype(F32))
        meta.append((pos, v.shape))
        pos += v.size
    total = -(-pos // 1024) * 1024
    flat.append(jnp.zeros((total - pos,), F32))
    return jnp.concatenate(flat).reshape(total // 128, 128), meta


def _unpack(packed, meta):
    flat = packed.reshape(-1)
    return [flat[off:off + math.prod(shape)].reshape(shape) for off, shape in meta]


def _to_shards(name, g):
    if name == "w_in":
        return jnp.transpose(g.reshape(g.shape[0], 4, g.shape[1] // 4), (1, 0, 2))
    if name == "w_up":
        return jnp.transpose(g.reshape(g.shape[0], 4, g.shape[1] // 4), (1, 0, 2))
    return g.reshape(4, g.shape[0] // 4, g.shape[1])


def _from_chips(name, g):
    if name in ("w_in", "w_up"):
        return jnp.transpose(g, (1, 2, 0, 3)).reshape(g.shape[1], g.shape[2], 4 * g.shape[3])
    return jnp.transpose(g, (1, 0, 2, 3)).reshape(g.shape[1], 4 * g.shape[2], g.shape[3])


INPUT_NAMES = ("x",) + WEIGHTS + ("loss_target",) + tuple("m_" + n for n in WEIGHTS) + tuple("v_" + n for n in WEIGHTS)


def _step(*args):
    a = dict(zip(INPUT_NAMES, args, strict=True))
    chip = 2 * lax.axis_index("x") + lax.axis_index("y")

    sharded_names = list(SMALL_SHARDED)
    small_pack, small_meta = _pack([a[n] for n in sharded_names])
    gathered = gather_chips([a[n].astype(BF16) for n in BIG], small_pack, "gather_weights")
    big = {n: _from_chips(n, g) for n, g in zip(BIG, gathered)}
    sp = {n: a[n] for n in SMALL if n not in SMALL_SHARDED}
    per_chip = [_unpack(gathered[-1][s], small_meta) for s in range(4)]
    for j, n in enumerate(sharded_names):
        sp[n] = jnp.concatenate([per_chip[s][j] for s in range(4)], axis=SMALL_SHARDED[n])

    loss_row, gx, gb, gs = local_step(a["x"][0], a["loss_target"][0], big, sp)

    partials = [jnp.stack([_to_shards(n, gb[n][l]) for l in range(DEPTH)]) for n in BIG]
    got = pair_exchange(partials, "pair_exchange")
    ids = jnp.stack([lax.axis_index("c"), chip]).astype(jnp.int32)
    chip_sums = [pair_sum(p, q, ids, f"pair_sum_{n}") for n, p, q in zip(BIG, partials, got)]
    slots = reduce_chips(chip_sums, "reduce_big")
    mine = [sum_chips(sl, p, q, ids, f"sum_{n}") for n, sl, p, q in zip(BIG, slots, partials, got)]
    summed = sibling_exchange(mine, "exchange_big")
    out_g, out_d, out_m, out_v = {}, {}, {}, {}
    for n, g in zip(BIG, summed):
        shape = a[n].shape
        flat = lambda t: t.reshape(shape[0] * shape[1], shape[2])
        d, nm, nv = adamw(flat(a[n]), flat(g), flat(a["m_" + n]), flat(a["v_" + n]), f"adamw_{n}")
        out_g[n], out_d[n], out_m[n], out_v[n] = g.reshape(shape), d.reshape(shape), nm.reshape(shape), nv.reshape(shape)

    vec, meta = _pack([loss_row] + [gs[n] for n in SMALL])
    total = sum_parts(gather_devices(vec, "gather_small"), "sum_small")
    parts = _unpack(total, meta)
    loss = parts[0][0, 0]
    g_small = {}
    for n, g in zip(SMALL, parts[1:]):
        if n in SMALL_SHARDED:
            ax = SMALL_SHARDED[n]
            size = a[n].shape[ax]
            g = lax.dynamic_slice_in_dim(g, chip * size, size, axis=ax)
        g_small[n] = g
    pw, pmeta = _pack([a[n] for n in SMALL])
    pg, _ = _pack([g_small[n] for n in SMALL])
    pm, _ = _pack([a["m_" + n] for n in SMALL])
    pv, _ = _pack([a["v_" + n] for n in SMALL])
    d, nm, nv = adamw(pw, pg, pm, pv, "adamw_small")
    for n, dd, mm, vv in zip(SMALL, _unpack(d, pmeta), _unpack(nm, pmeta), _unpack(nv, pmeta)):
        out_g[n], out_d[n], out_m[n], out_v[n] = g_small[n], dd, mm, vv

    return (loss, gx[None], *[out_g[n] for n in WEIGHTS], *[out_d[n] for n in WEIGHTS],
            *[out_m[n] for n in WEIGHTS], *[out_v[n] for n in WEIGHTS])


def kernel(x, lower_bounds, w_in, w_in_vres, mu_shift, mu_vres, rwkv_w0, rwkv_w2, rwkv_a0, rwkv_a2, rwkv_g2, rwkv_k_k, rwkv_k_a, rwkv_r_k, rwkv_lnx_w, rwkv_lnx_b, rwkv_v0, rwkv_v2, ssd_conv_w, ssd_conv_b, ssd_dt_bias, ssd_A_log, ssd_D, ssd_norm_w, hgrn_norm_w, w_out, ln1_w, ln1_b, w_up, w_down, ln2_w, ln2_b, loss_target, m_lower_bounds, m_w_in, m_w_in_vres, m_mu_shift, m_mu_vres, m_rwkv_w0, m_rwkv_w2, m_rwkv_a0, m_rwkv_a2, m_rwkv_g2, m_rwkv_k_k, m_rwkv_k_a, m_rwkv_r_k, m_rwkv_lnx_w, m_rwkv_lnx_b, m_rwkv_v0, m_rwkv_v2, m_ssd_conv_w, m_ssd_conv_b, m_ssd_dt_bias, m_ssd_A_log, m_ssd_D, m_ssd_norm_w, m_hgrn_norm_w, m_w_out, m_ln1_w, m_ln1_b, m_w_up, m_w_down, m_ln2_w, m_ln2_b, v_lower_bounds, v_w_in, v_w_in_vres, v_mu_shift, v_mu_vres, v_rwkv_w0, v_rwkv_w2, v_rwkv_a0, v_rwkv_a2, v_rwkv_g2, v_rwkv_k_k, v_rwkv_k_a, v_rwkv_r_k, v_rwkv_lnx_w, v_rwkv_lnx_b, v_rwkv_v0, v_rwkv_v2, v_ssd_conv_w, v_ssd_conv_b, v_ssd_dt_bias, v_ssd_A_log, v_ssd_D, v_ssd_norm_w, v_hgrn_norm_w, v_w_out, v_ln1_w, v_ln1_b, v_w_up, v_w_down, v_ln2_w, v_ln2_b):
    return _step(x, lower_bounds, w_in, w_in_vres, mu_shift, mu_vres, rwkv_w0, rwkv_w2, rwkv_a0, rwkv_a2, rwkv_g2, rwkv_k_k, rwkv_k_a, rwkv_r_k, rwkv_lnx_w, rwkv_lnx_b, rwkv_v0, rwkv_v2, ssd_conv_w, ssd_conv_b, ssd_dt_bias, ssd_A_log, ssd_D, ssd_norm_w, hgrn_norm_w, w_out, ln1_w, ln1_b, w_up, w_down, ln2_w, ln2_b, loss_target, m_lower_bounds, m_w_in, m_w_in_vres, m_mu_shift, m_mu_vres, m_rwkv_w0, m_rwkv_w2, m_rwkv_a0, m_rwkv_a2, m_rwkv_g2, m_rwkv_k_k, m_rwkv_k_a, m_rwkv_r_k, m_rwkv_lnx_w, m_rwkv_lnx_b, m_rwkv_v0, m_rwkv_v2, m_ssd_conv_w, m_ssd_conv_b, m_ssd_dt_bias, m_ssd_A_log, m_ssd_D, m_ssd_norm_w, m_hgrn_norm_w, m_w_out, m_ln1_w, m_ln1_b, m_w_up, m_w_down, m_ln2_w, m_ln2_b, v_lower_bounds, v_w_in, v_w_in_vres, v_mu_shift, v_mu_vres, v_rwkv_w0, v_rwkv_w2, v_rwkv_a0, v_rwkv_a2, v_rwkv_g2, v_rwkv_k_k, v_rwkv_k_a, v_rwkv_r_k, v_rwkv_lnx_w, v_rwkv_lnx_b, v_rwkv_v0, v_rwkv_v2, v_ssd_conv_w, v_ssd_conv_b, v_ssd_dt_bias, v_ssd_A_log, v_ssd_D, v_ssd_norm_w, v_hgrn_norm_w, v_w_out, v_ln1_w, v_ln1_b, v_w_up, v_w_down, v_ln2_w, v_ln2_b)
```

```python
import functools
import math

import jax
import jax.numpy as jnp
from jax import lax
from jax.experimental import pallas as pl
from jax.experimental.pallas import tpu as pltpu

F32 = jnp.float32
BF16 = jnp.bfloat16
HI = lax.Precision.HIGHEST

DEPTH = 2
D_MODEL = 1024
D_GROUP = 256
HEAD_DIM = 64
N_HEADS = 4
SSD_STATE = 128
SSD_XBC = 768
SSD_CONV = 4
D_FF = 4096
ALPHA = (2.0 * DEPTH) ** 0.25
LN_EPS = 1e-5
RMS_EPS = 1e-5
RWKV_GN_EPS = HEAD_DIM * 1e-5
DILATED_BRANCHES = ((128, 1), (512, 4), (2048, 16))
ALIBI_SLOPES = tuple(2.0 ** (-8.0 * (h + 1) / N_HEADS) for h in range(N_HEADS))
ATTN_BLK = 128

ADAM_LR, ADAM_B1, ADAM_B2, ADAM_EPS, ADAM_WD, ADAM_STEP = 0.001, 0.9, 0.999, 1e-08, 0.01, 10

IN_COLS = 3716
PROJ_W = 4096
SEG_HGRN, SEG_RWKV, SEG_Q, SEG_Z, SEG_XBC, SEG_DT = 0, 1024, 2048, 2816, 3072, 3840
_PIECES = ((0, 896, SEG_RWKV), (896, 768, SEG_Q), (1664, 256, SEG_Z), (1920, 768, SEG_XBC),
           (2688, 4, SEG_DT), (2692, 1024, SEG_HGRN))
VRES_COL = SEG_RWKV + 896

HM64 = (N_HEADS, HEAD_DIM)
HM128 = (N_HEADS, SSD_STATE)
ROW_TILE = 256
SCAN_CHUNK = 128
VMEM_LIMIT = 48 * 1024 * 1024


def _cparams(sem=None):
    if sem is None:
        return pltpu.CompilerParams(vmem_limit_bytes=VMEM_LIMIT)
    return pltpu.CompilerParams(dimension_semantics=sem, vmem_limit_bytes=VMEM_LIMIT)


def _pick(n, pref):
    for t in pref:
        if n % t == 0:
            return t
    return n


def matmul(a, b, mode, name, add=None):
    if mode == "nn":
        (M, K), (_, N) = a.shape, b.shape
    elif mode == "nt":
        (M, K), (N, _) = a.shape, b.shape
    else:
        (K, M), (_, N) = a.shape, b.shape
    tm, tn, tk = _pick(M, (1024, 512, 256, 128)), _pick(N, (1024, 512, 256, 128)), _pick(K, (1024, 512, 256, 128))
    nk = K // tk
    dims = {"nn": (((1,), (0,)), ((), ())), "nt": (((1,), (1,)), ((), ())), "tn": (((0,), (0,)), ((), ()))}[mode]

    def body(*refs):
        a_ref, b_ref = refs[:2]
        add_ref, o_ref = (None, refs[2]) if add is None else refs[2:]
        k = pl.program_id(2)
        d = lax.dot_general(a_ref[...].astype(BF16), b_ref[...].astype(BF16), dims, preferred_element_type=F32)

        @pl.when(k == 0)
        def _():
            o_ref[...] = d if add_ref is None else d + add_ref[...]

        if nk > 1:
            @pl.when(k > 0)
            def _():
                o_ref[...] += d

    a_spec = pl.BlockSpec((tk, tm), lambda i, j, k: (k, i)) if mode == "tn" else pl.BlockSpec((tm, tk), lambda i, j, k: (i, k))
    b_spec = pl.BlockSpec((tn, tk), lambda i, j, k: (j, k)) if mode == "nt" else pl.BlockSpec((tk, tn), lambda i, j, k: (k, j))
    o_spec = pl.BlockSpec((tm, tn), lambda i, j, k: (i, j))
    ins, specs = [a, b], [a_spec, b_spec]
    if add is not None:
        ins.append(add)
        specs.append(o_spec)
    return pl.pallas_call(
        body, grid=(M // tm, N // tn, nk), in_specs=specs, out_specs=o_spec,
        out_shape=jax.ShapeDtypeStruct((M, N), F32),
        compiler_params=_cparams(("parallel", "parallel", "arbitrary")), name=name)(*ins)


def _row_spec(w, tile):
    return pl.BlockSpec((tile, w), lambda i: (i, 0))


def _par_spec(shape):
    return pl.BlockSpec(shape, lambda i: (0,) * len(shape))


def _rows_spec(shape, tile):
    if len(shape) == 3:
        return pl.BlockSpec((shape[0], tile, shape[2]), lambda i: (0, i, 0))
    return _row_spec(shape[1], tile)


def _rows_shape(S, w):
    return (w[0], S, w[1]) if isinstance(w, tuple) else (S, w)


def _rows_load(ref):
    if len(ref.shape) == 3:
        return jnp.concatenate([ref[h] for h in range(ref.shape[0])], axis=1)
    return ref[...]


def _rows_store(ref, val):
    if len(ref.shape) == 3:
        w = ref.shape[2]
        for h in range(ref.shape[0]):
            ref[h] = val[:, h * w:(h + 1) * w]
    else:
        ref[...] = val


def tl_fwd(fn, name, rows, pars, out_widths, tile=ROW_TILE):
    S = rows[0].shape[-2]
    nr = len(rows)

    def body(*refs):
        ins = [_rows_load(r) for r in refs[:nr]] + [r[...] for r in refs[nr:nr + len(pars)]]
        outs = fn(*ins)
        for o_ref, o in zip(refs[nr + len(pars):], outs):
            _rows_store(o_ref, o)

    shapes = [_rows_shape(S, w) for w in out_widths]
    return pl.pallas_call(
        body, grid=(S // tile,),
        in_specs=[_rows_spec(r.shape, tile) for r in rows] + [_par_spec(p.shape) for p in pars],
        out_specs=[_rows_spec(s, tile) for s in shapes],
        out_shape=[jax.ShapeDtypeStruct(s, F32) for s in shapes],
        compiler_params=_cparams(("parallel",)), name=name)(*rows, *pars)


def tl_bwd(fn, name, rows, pars, cts, tile=ROW_TILE, row_grad=None):
    S = rows[0].shape[-2]
    nr, npar = len(rows), len(pars)
    row_grad = [True] * nr if row_grad is None else row_grad
    flat_cts = [c for group in cts for c in group]
    ncts = len(flat_cts)
    gi = [i for i in range(nr) if row_grad[i]]

    def body(*refs):
        row_v = [_rows_load(r) for r in refs[:nr]]
        par_v = [r[...] for r in refs[nr:nr + npar]]
        ct_refs = refs[nr + npar:nr + npar + ncts]
        out_refs = refs[nr + npar + ncts:]
        ct_v, pos = [], 0
        for group in cts:
            acc = _rows_load(ct_refs[pos])
            for q in range(1, len(group)):
                acc = acc + _rows_load(ct_refs[pos + q])
            pos += len(group)
            ct_v.append(acc)

        def f(diff_rows, par_vals):
            full = list(row_v)
            for idx, val in zip(gi, diff_rows):
                full[idx] = val
            return tuple(fn(*full, *par_vals))

        _, vjp = jax.vjp(f, [row_v[i] for i in gi], par_v)
        d_rows, d_pars = vjp(tuple(ct_v))
        for o_ref, g in zip(out_refs[:len(gi)], d_rows):
            _rows_store(o_ref, g)
        first = pl.program_id(0) == 0
        for o_ref, g in zip(out_refs[len(gi):], d_pars):
            @pl.when(first)
            def _(o_ref=o_ref):
                o_ref[...] = jnp.zeros_like(o_ref)
            o_ref[...] += g

    outs = pl.pallas_call(
        body, grid=(S // tile,),
        in_specs=[_rows_spec(r.shape, tile) for r in rows] + [_par_spec(p.shape) for p in pars]
        + [_rows_spec(c.shape, tile) for c in flat_cts],
        out_specs=[_rows_spec(rows[i].shape, tile) for i in gi] + [_par_spec(p.shape) for p in pars],
        out_shape=[jax.ShapeDtypeStruct(rows[i].shape, F32) for i in gi] + [jax.ShapeDtypeStruct(p.shape, F32) for p in pars],
        compiler_params=_cparams(("arbitrary",)), name=name)(*rows, *pars, *flat_cts)
    return list(outs[:len(gi)]), list(outs[len(gi):])


def _shift_rows(x, j):
    if j == 0:
        return x
    rolled = pltpu.roll(x, j, 0)
    row = lax.broadcasted_iota(jnp.int32, x.shape, 0)
    return jnp.where(row >= j, rolled, 0.0)


def _unshift_rows(x, j):
    if j == 0:
        return x
    S = x.shape[0]
    rolled = pltpu.roll(x, S - j, 0)
    row = lax.broadcasted_iota(jnp.int32, x.shape, 0)
    return jnp.where(row < S - j, rolled, 0.0)


def fir_fwd(x, taps, name):
    S, C = x.shape
    K = taps.shape[0]

    def body(x_ref, w_ref, y_ref):
        xv = x_ref[...]
        acc = jnp.zeros_like(xv)
        for k in range(K):
            acc = acc + _shift_rows(xv, K - 1 - k) * w_ref[pl.ds(k, 1), :]
        y_ref[...] = acc

    cs = pl.BlockSpec((S, 128), lambda j: (0, j))
    return pl.pallas_call(body, grid=(C // 128,), in_specs=[cs, pl.BlockSpec((K, 128), lambda j: (0, j))],
                          out_specs=cs, out_shape=jax.ShapeDtypeStruct((S, C), F32),
                          compiler_params=_cparams(("parallel",)), name=name)(x, taps)


def fir_bwd(x, taps, dy_list, name):
    S, C = x.shape
    K = taps.shape[0]
    n = len(dy_list)

    def body(*refs):
        x_ref, w_ref = refs[:2]
        dy = refs[2][...]
        for q in range(1, n):
            dy = dy + refs[2 + q][...]
        dx_ref, dw_ref, db_ref = refs[2 + n:]
        xv = x_ref[...]
        dx = jnp.zeros_like(xv)
        for k in range(K):
            j = K - 1 - k
            dx = dx + _unshift_rows(dy, j) * w_ref[pl.ds(k, 1), :]
            dw_ref[pl.ds(k, 1), :] = jnp.sum(dy * _shift_rows(xv, j), axis=0, keepdims=True)
        dx_ref[...] = dx
        db_ref[...] = jnp.sum(dy, axis=0, keepdims=True)

    cs = pl.BlockSpec((S, 128), lambda j: (0, j))
    ks = pl.BlockSpec((K, 128), lambda j: (0, j))
    bs = pl.BlockSpec((1, 128), lambda j: (0, j))
    return pl.pallas_call(body, grid=(C // 128,), in_specs=[cs, ks] + [cs] * n, out_specs=[cs, ks, bs],
                          out_shape=[jax.ShapeDtypeStruct((S, C), F32), jax.ShapeDtypeStruct((K, C), F32),
                                     jax.ShapeDtypeStruct((1, C), F32)],
                          compiler_params=_cparams(("parallel",)), name=name)(x, taps, *dy_list)


def _col(tile, lane, t):
    return jnp.sum(jnp.where(lane == t, tile, 0.0), axis=1, keepdims=True)


def _rwkv_step(s, rv, vcol):
    sa = jnp.sum(s * (-rv[3]), axis=1, keepdims=True)
    return s * rv[1] + sa * (rv[3] * rv[4]) + vcol * rv[2], sa


def scan_fwd(r, w, k, vT, kk, a, name):
    H, S, Dk = r.shape
    Dv = vT.shape[1]
    Tc = SCAN_CHUNK
    nc = S // Tc
    rows = [r, w, k, kk, a]

    def body(*refs):
        row_refs = refs[:5]
        vT_ref, yT_ref, sall_ref, s_ref = refs[5:]

        @pl.when(pl.program_id(0) == 0)
        def _():
            s_ref[...] = jnp.zeros_like(s_ref)

        yT_ref[...] = jnp.zeros_like(yT_ref)
        lane = lax.broadcasted_iota(jnp.int32, (Dv, Tc), 1)

        def step(t, states):
            new = []
            for h in range(H):
                s = states[h]
                sall_ref[t, h] = s
                rv = [ref[h, pl.ds(t, 1), :] for ref in row_refs]
                s, _ = _rwkv_step(s, rv, _col(vT_ref[h], lane, t))
                ycol = jnp.sum(s * rv[0], axis=1, keepdims=True)
                yT_ref[h] = jnp.where(lane == t, ycol, yT_ref[h])
                new.append(s)
            return tuple(new)

        states = lax.fori_loop(0, Tc, step, tuple(s_ref[h] for h in range(H)))
        for h in range(H):
            s_ref[h] = states[h]

    rs = pl.BlockSpec((H, Tc, Dk), lambda c: (0, c, 0))
    vs = pl.BlockSpec((H, Dv, Tc), lambda c: (0, 0, c))
    yT, sall = pl.pallas_call(
        body, grid=(nc,), in_specs=[rs] * 5 + [vs],
        out_specs=[vs, pl.BlockSpec((Tc, H, Dv, Dk), lambda c: (c, 0, 0, 0))],
        out_shape=[jax.ShapeDtypeStruct((H, Dv, S), F32), jax.ShapeDtypeStruct((S, H, Dv, Dk), F32)],
        scratch_shapes=[pltpu.VMEM((H, Dv, Dk), F32)],
        compiler_params=_cparams(("arbitrary",)), name=name)(*rows, vT)
    return yT, sall


def scan_bwd(r, w, k, vT, kk, a, sall, dyT, name):
    H, S, Dk = r.shape
    Dv = vT.shape[1]
    Tc = SCAN_CHUNK
    nc = S // Tc
    rows = [r, w, k, kk, a]

    def body(*refs):
        row_refs = refs[:5]
        vT_ref, dyT_ref, sall_ref = refs[5:8]
        drow_refs = refs[8:13]
        dvT_ref, ds_ref = refs[13:]

        @pl.when(pl.program_id(0) == 0)
        def _():
            ds_ref[...] = jnp.zeros_like(ds_ref)

        dvT_ref[...] = jnp.zeros_like(dvT_ref)
        lane = lax.broadcasted_iota(jnp.int32, (Dv, Tc), 1)

        def bstep(i, carry):
            t = Tc - 1 - i
            new = []
            for h in range(H):
                ds = carry[h]
                sp = sall_ref[t, h]
                rv = [ref[h, pl.ds(t, 1), :] for ref in row_refs]
                vcol = _col(vT_ref[h], lane, t)
                dycol = _col(dyT_ref[h], lane, t)
                st, sa = _rwkv_step(sp, rv, vcol)
                drow_refs[0][h, pl.ds(t, 1), :] = jnp.sum(st * dycol, axis=0, keepdims=True)
                g = ds + dycol * rv[0]
                drow_refs[1][h, pl.ds(t, 1), :] = jnp.sum(g * sp, axis=0, keepdims=True)
                drow_refs[2][h, pl.ds(t, 1), :] = jnp.sum(g * vcol, axis=0, keepdims=True)
                dvcol = jnp.sum(g * rv[2], axis=1, keepdims=True)
                dsa = jnp.sum(g * (rv[3] * rv[4]), axis=1, keepdims=True)
                db = jnp.sum(g * sa, axis=0, keepdims=True)
                dnkk = jnp.sum(sp * dsa, axis=0, keepdims=True)
                drow_refs[3][h, pl.ds(t, 1), :] = db * rv[4] - dnkk
                drow_refs[4][h, pl.ds(t, 1), :] = db * rv[3]
                dvT_ref[h] = jnp.where(lane == t, dvcol, dvT_ref[h])
                new.append(g * rv[1] - dsa * rv[3])
            return tuple(new)

        carry = lax.fori_loop(0, Tc, bstep, tuple(ds_ref[h] for h in range(H)))
        for h in range(H):
            ds_ref[h] = carry[h]

    rs = pl.BlockSpec((H, Tc, Dk), lambda c: (0, nc - 1 - c, 0))
    vs = pl.BlockSpec((H, Dv, Tc), lambda c: (0, 0, nc - 1 - c))
    outs = pl.pallas_call(
        body, grid=(nc,),
        in_specs=[rs] * 5 + [vs, vs, pl.BlockSpec((Tc, H, Dv, Dk), lambda c: (nc - 1 - c, 0, 0, 0))],
        out_specs=[rs] * 5 + [vs],
        out_shape=[jax.ShapeDtypeStruct((H, S, Dk), F32)] * 5 + [jax.ShapeDtypeStruct((H, Dv, S), F32)],
        scratch_shapes=[pltpu.VMEM((H, Dv, Dk), F32)],
        compiler_params=_cparams(("arbitrary",)), name=name)(*rows, vT, dyT, sall)
    return list(outs[:5]), outs[5]


CHUNK = 128
SSD_GROUP = 2
HGRN_GROUP = 4


def chunk_fwd(fn, name, blocks, state_shape, out_width, group):
    H, S, _ = blocks[0].shape
    nc = S // CHUNK
    nb = len(blocks)

    def body(*refs):
        o_ref, sv_ref, st = refs[nb:]

        @pl.when(pl.program_id(1) == 0)
        def _():
            st[...] = jnp.zeros_like(st)

        for g in range(group):
            s0 = st[g]
            sv_ref[g, 0] = s0
            s1, out = fn(s0, *[r[g] for r in refs[:nb]])
            st[g] = s1
            o_ref[g] = out

    spec = lambda w: pl.BlockSpec((group, CHUNK, w), lambda h, c: (h, c, 0))
    return pl.pallas_call(
        body, grid=(H // group, nc), in_specs=[spec(b.shape[2]) for b in blocks],
        out_specs=[spec(out_width), pl.BlockSpec((group, 1) + state_shape, lambda h, c: (h, c, 0, 0))],
        out_shape=[jax.ShapeDtypeStruct((H, S, out_width), F32), jax.ShapeDtypeStruct((H, nc) + state_shape, F32)],
        scratch_shapes=[pltpu.VMEM((group,) + state_shape, F32)],
        compiler_params=_cparams(("parallel", "arbitrary")), name=name)(*blocks)


def chunk_bwd(fn, name, blocks, states, dout, group):
    H, S, _ = blocks[0].shape
    nc = S // CHUNK
    nb = len(blocks)
    state_shape = states.shape[2:]

    def body(*refs):
        sv_ref, do_ref = refs[nb], refs[nb + 1]
        d_refs = refs[nb + 2:2 * nb + 2]
        dst = refs[2 * nb + 2]

        @pl.when(pl.program_id(1) == 0)
        def _():
            dst[...] = jnp.zeros_like(dst)

        for g in range(group):
            _, vjp = jax.vjp(fn, sv_ref[g, 0], *[r[g] for r in refs[:nb]])
            grads = vjp((dst[g], do_ref[g]))
            dst[g] = grads[0]
            for d_ref, gr in zip(d_refs, grads[1:]):
                d_ref[g] = gr

    spec = lambda w: pl.BlockSpec((group, CHUNK, w), lambda h, c: (h, nc - 1 - c, 0))
    return pl.pallas_call(
        body, grid=(H // group, nc),
        in_specs=[spec(b.shape[2]) for b in blocks]
        + [pl.BlockSpec((group, 1) + state_shape, lambda h, c: (h, nc - 1 - c, 0, 0)), spec(dout.shape[2])],
        out_specs=[spec(b.shape[2]) for b in blocks],
        out_shape=[jax.ShapeDtypeStruct(b.shape, F32) for b in blocks],
        scratch_shapes=[pltpu.VMEM((group,) + state_shape, F32)],
        compiler_params=_cparams(("parallel", "arbitrary")), name=name)(*blocks, states, dout)


def _bdot(a, b, dims):
    return lax.dot_general(a.astype(BF16), b.astype(BF16), (dims, ((), ())), preferred_element_type=F32)


def ssd_chunk(state, cb, bb, da, xdt):
    T = cb.shape[0]
    ti = lax.broadcasted_iota(jnp.int32, (T, T), 0)
    si = lax.broadcasted_iota(jnp.int32, (T, T), 1)
    mask = ti >= si
    cs = jnp.dot(mask.astype(F32), da, precision=HI, preferred_element_type=F32)
    pick = (lax.broadcasted_iota(jnp.int32, cs.shape, 1) == 0).astype(F32)
    cs_row = lax.dot_general(pick, cs, (((1,), (1,)), ((), ())), precision=HI, preferred_element_type=F32)
    lmat = jnp.where(mask, jnp.exp(jnp.where(mask, cs - cs_row, 0.0)), 0.0)
    scores = _bdot(cb, bb, ((1,), (1,))) * lmat
    y = _bdot(scores, xdt, ((1,), (0,))) + _bdot(cb, state, ((1,), (1,))) * jnp.exp(cs[:, :HEAD_DIM])
    last = cs[T - 1:T, :]
    new_state = state * jnp.exp(last) + _bdot(xdt, bb * jnp.exp(last - cs), ((0,), (0,)))
    return new_state, y


HGRN_SUB = 16


def hgrn_chunk(state, q, k, lf, v):
    T, C = q.shape[0], HGRN_SUB
    ti = lax.broadcasted_iota(jnp.int32, (C, C), 0)
    si = lax.broadcasted_iota(jnp.int32, (C, C), 1)
    tril = (ti >= si).astype(F32)
    row = lax.broadcasted_iota(jnp.int32, (C, q.shape[1]), 0)
    outs = []
    for j in range(T // C):
        qj, kj, lj, vj = (a[j * C:(j + 1) * C] for a in (q, k, lf, v))
        b = jnp.dot(tril, lj, precision=HI, preferred_element_type=F32)
        o = _bdot(qj * jnp.exp(b), state, ((1,), (1,)))
        for s in range(C):
            m = row >= s
            e = jnp.where(m, jnp.exp(jnp.where(m, b - b[s:s + 1], 0.0)), 0.0)
            o = o + jnp.sum(qj * kj[s:s + 1] * e, axis=1, keepdims=True) * vj[s:s + 1]
        last = b[C - 1:C]
        state = state * jnp.exp(last) + _bdot(vj, kj * jnp.exp(last - b), ((0,), (0,)))
        outs.append(o)
    return state, jnp.concatenate(outs, axis=0)


def _to_headsT(x):
    S = x.shape[0]
    return jnp.transpose(x.reshape(S, N_HEADS, HEAD_DIM), (1, 2, 0))


def _from_headsT(x):
    H, dv, S = x.shape
    return jnp.transpose(x, (2, 0, 1)).reshape(S, H * dv)


def _attn_block(q, kp, kc, vp, vc, n, slope, dilation):
    blk = ATTN_BLK
    k2 = jnp.concatenate([kp, kc], axis=0)
    v2 = jnp.concatenate([vp, vc], axis=0)
    s = _bdot(q, k2, ((1,), (1,))) * (HEAD_DIM ** -0.5)
    i = lax.broadcasted_iota(jnp.int32, (blk, 2 * blk), 0)
    j = lax.broadcasted_iota(jnp.int32, (blk, 2 * blk), 1)
    dist = blk + i - j
    first_key = jnp.where(n > 0, 0, blk)
    valid = (dist >= 0) & (dist <= blk) & (j >= first_key)
    s = s - slope * (dist * dilation).astype(F32)
    s = jnp.where(valid, s, -1e30)
    m = jnp.max(s, axis=-1, keepdims=True)
    p = jnp.exp(s - m)
    l = jnp.sum(p, axis=-1, keepdims=True)
    o = _bdot(p, v2, ((1,), (0,))) / l
    lse = jnp.broadcast_to(m + jnp.log(l), o.shape)
    return o, lse


_QCOL = SEG_Q // 128
PAIR = 2 * HEAD_DIM


def _attn_specs(rows):
    cur = lambda j: pl.BlockSpec((rows, PAIR), lambda p, n: (n, j + p))
    prev = lambda j: pl.BlockSpec((rows, PAIR), lambda p, n: (jnp.maximum(n - 1, 0), j + p))
    return cur, prev


def _pair_slope(pair, h):
    return jnp.where(pair == 0, jnp.float32(ALIBI_SLOPES[h]), jnp.float32(ALIBI_SLOPES[2 + h]))


def _halves(t):
    return [t[:, h * HEAD_DIM:(h + 1) * HEAD_DIM] for h in range(2)]


def _for_classes(dilation, step):
    if dilation == 1:
        step(0)
    else:
        lax.fori_loop(0, dilation, lambda z, c: (step(z), c)[1], 0)


def attn_fwd(proj, dilation, name):
    S = proj.shape[0]
    blk = ATTN_BLK
    rows = blk * dilation
    cur, prev = _attn_specs(rows)

    def body(q_ref, kp_ref, kc_ref, vp_ref, vc_ref, o_ref, l_ref):
        pair, n = pl.program_id(0), pl.program_id(1)

        def one_class(z):
            sel = pl.ds(z, blk, stride=dilation) if dilation > 1 else pl.ds(0, blk)
            q, kp, kc, vp, vc = (_halves(r[sel, :]) for r in (q_ref, kp_ref, kc_ref, vp_ref, vc_ref))
            res = [_attn_block(q[h], kp[h], kc[h], vp[h], vc[h], n, _pair_slope(pair, h), dilation) for h in range(2)]
            o_ref[sel, :] = jnp.concatenate([r[0] for r in res], axis=1)
            l_ref[sel, :] = jnp.concatenate([r[1] for r in res], axis=1)

        _for_classes(dilation, one_class)

    return pl.pallas_call(
        body, grid=(2, S // rows),
        in_specs=[cur(_QCOL), prev(_QCOL + 2), cur(_QCOL + 2), prev(_QCOL + 4), cur(_QCOL + 4)],
        out_specs=[cur(0), cur(0)], out_shape=[jax.ShapeDtypeStruct((S, D_GROUP), F32)] * 2,
        compiler_params=_cparams(("parallel", "arbitrary")), name=name)(proj, proj, proj, proj, proj)


def attn_bwd(proj, do, dlse, dilation, name):
    S = proj.shape[0]
    blk = ATTN_BLK
    rows = blk * dilation
    cur, prev = _attn_specs(rows)
    full = pl.BlockSpec((S, PAIR), lambda p, n: (0, p))

    def body(q_ref, kp_ref, kc_ref, vp_ref, vc_ref, do_ref, dl_ref, dq_ref, dk_ref, dv_ref):
        pair, n = pl.program_id(0), pl.program_id(1)

        @pl.when(n == 0)
        def _():
            dk_ref[...] = jnp.zeros_like(dk_ref)
            dv_ref[...] = jnp.zeros_like(dv_ref)

        def one_class(z):
            sel = pl.ds(z, blk, stride=dilation) if dilation > 1 else pl.ds(0, blk)
            q, kp, kc, vp, vc, do_v, dl_v = (_halves(r[sel, :]) for r in
                                             (q_ref, kp_ref, kc_ref, vp_ref, vc_ref, do_ref, dl_ref))
            grads = []
            for h in range(2):
                f = lambda q_, kp_, kc_, vp_, vc_, h=h: _attn_block(q_, kp_, kc_, vp_, vc_, n, _pair_slope(pair, h), dilation)
                _, vjp = jax.vjp(f, q[h], kp[h], kc[h], vp[h], vc[h])
                grads.append(vjp((do_v[h], dl_v[h])))
            both = lambda j: jnp.concatenate([grads[0][j], grads[1][j]], axis=1)
            dq_ref[sel, :] = both(0)
            if dilation > 1:
                here = pl.ds(n * rows + z, blk, stride=dilation)
                before = pl.ds(jnp.maximum(n - 1, 0) * rows + z, blk, stride=dilation)
            else:
                here = pl.ds(pl.multiple_of(n * blk, blk), blk)
                before = pl.ds(pl.multiple_of(jnp.maximum(n - 1, 0) * blk, blk), blk)
            dk_ref[here, :] = dk_ref[here, :] + both(2)
            dv_ref[here, :] = dv_ref[here, :] + both(4)
            dk_ref[before, :] = dk_ref[before, :] + both(1)
            dv_ref[before, :] = dv_ref[before, :] + both(3)

        _for_classes(dilation, one_class)

    return pl.pallas_call(
        body, grid=(2, S // rows),
        in_specs=[cur(_QCOL), prev(_QCOL + 2), cur(_QCOL + 2), prev(_QCOL + 4), cur(_QCOL + 4), cur(0), cur(0)],
        out_specs=[cur(0), full, full], out_shape=[jax.ShapeDtypeStruct((S, D_GROUP), F32)] * 3,
        compiler_params=_cparams(("parallel", "arbitrary")), name=name)(proj, proj, proj, proj, proj, do, dlse)


def _head_ones(width, group):
    i = lax.broadcasted_iota(jnp.int32, (width, width), 0) // group
    j = lax.broadcasted_iota(jnp.int32, (width, width), 1) // group
    return (i == j).astype(F32)


def _group_sum(x, group):
    return jnp.dot(x, _head_ones(x.shape[1], group), precision=HI, preferred_element_type=F32)


def _spread(width_in, width_out, rep):
    i = lax.broadcasted_iota(jnp.int32, (width_in, width_out), 0)
    j = lax.broadcasted_iota(jnp.int32, (width_in, width_out), 1) // rep
    return (i == j).astype(F32)


def _hdot(a, b):
    return jnp.dot(a, b, precision=HI, preferred_element_type=F32)


def _sigmoid(x):
    return 1.0 / (1.0 + jnp.exp(-x))


def _softplus(x):
    return jnp.maximum(x, 0.0) + jnp.log(1.0 + jnp.exp(jnp.minimum(x, -x)))


def _silu(x):
    return x * _sigmoid(x)


def rwkv_pre(layer):
    def fn(*args):
        if layer == 0:
            fs, w0, w2p, a0, a2p, g2p, k_k, k_a = args
        else:
            fs, vfirst, w0, w2p, a0, a2p, g2p, k_k, k_a, v0, v2p = args
        r, k, v = fs[:, 0:256], fs[:, 256:512], fs[:, 512:768]
        lora = fs[:, 768:896]
        w_log = -_softplus(-(w0 + _hdot(jnp.tanh(lora), w2p))) - 0.5
        decay = jnp.exp(-jnp.exp(w_log))
        a = _sigmoid(a0 + _hdot(lora, a2p))
        g = _hdot(_sigmoid(lora), g2p)
        if layer > 0:
            v = v + (vfirst - v) * _sigmoid(v0 + _hdot(fs[:, 896:1024], v2p))
        kk = k * k_k
        kk = kk / jnp.maximum(jnp.sqrt(_group_sum(kk * kk, HEAD_DIM)), 1e-12)
        k = k * (1.0 + (a - 1.0) * k_a)
        return r, decay, k, v, kk, a, g
    return fn


def rwkv_post(y, r, k, v, g, lnx_w, lnx_b, r_k):
    mu = _group_sum(y, HEAD_DIM) * (1.0 / HEAD_DIM)
    yc = y - mu
    var = _group_sum(yc * yc, HEAD_DIM) * (1.0 / HEAD_DIM)
    yn = yc * lax.rsqrt(var + RWKV_GN_EPS) * lnx_w + lnx_b
    bonus = _group_sum(r * k * r_k, HEAD_DIM) * v
    return ((yn + bonus) * g,)


def attn_combine(o1, o2, o3, l1, l2, l3):
    m = jnp.maximum(jnp.maximum(l1, l2), l3)
    e1, e2, e3 = jnp.exp(l1 - m), jnp.exp(l2 - m), jnp.exp(l3 - m)
    return ((o1 * e1 + o2 * e2 + o3 * e3) / (e1 + e2 + e3),)


def ssd_pre(xc, dtr, conv_b, dt_bias, a_log):
    xbc = _silu(xc + conv_b)
    xs, bm, cm = xbc[:, 0:256], xbc[:, 256:512], xbc[:, 512:768]
    dt = _softplus(dtr + dt_bias)
    a_neg = -jnp.exp(a_log)
    wide = _spread(128, N_HEADS * SSD_STATE, SSD_STATE)
    w = _hdot(dt, wide) * _hdot(a_neg, wide)
    xdt = xs * _hdot(dt, _spread(128, D_GROUP, HEAD_DIM))
    rr = jnp.concatenate([cm[:, 0:128], cm[:, 0:128], cm[:, 128:256], cm[:, 128:256]], axis=1)
    kk = jnp.concatenate([bm[:, 0:128], bm[:, 0:128], bm[:, 128:256], bm[:, 128:256]], axis=1)
    return rr, w, kk, xdt, xs


def ssd_post(ys, z, xs, d_skip, norm_w):
    y = ys + xs * _hdot(d_skip, _spread(128, D_GROUP, HEAD_DIM))
    y = y * _silu(z)
    half = D_GROUP // 2
    parts = []
    for g in range(2):
        t = y[:, g * half:(g + 1) * half]
        parts.append(t * lax.rsqrt(jnp.mean(t * t, axis=-1, keepdims=True) + RMS_EPS))
    return (jnp.concatenate(parts, axis=1) * norm_w,)


def hgrn_pre(seg, lb):
    q, f, i = seg[:, 0:256], seg[:, 256:512], seg[:, 512:768]
    forget = lb + (1.0 - lb) * _sigmoid(f)
    return _silu(q), 1.0 - forget, jnp.log(forget), i


def hgrn_post(o, seg, norm_w):
    g = seg[:, 768:1024]
    ms = _group_sum(o * o, HEAD_DIM) * (1.0 / HEAD_DIM)
    return (o * lax.rsqrt(ms + RMS_EPS) * norm_w * _silu(g),)


def ln_res(x, y, w, b):
    z = ALPHA * x + y
    mu = jnp.mean(z, axis=-1, keepdims=True)
    zc = z - mu
    var = jnp.mean(zc * zc, axis=-1, keepdims=True)
    return (zc * lax.rsqrt(var + LN_EPS) * w + b,)


def relu2(u):
    r = jnp.maximum(u, 0.0)
    return (r * r,)


def loss_and_grad(y, tgt, name):
    S, D = y.shape
    tile = ROW_TILE

    def body(y_ref, t_ref, l_ref, dy_ref):
        e = y_ref[...] - t_ref[...]
        dy_ref[...] = e * (1.0 / D)

        @pl.when(pl.program_id(0) == 0)
        def _():
            l_ref[...] = jnp.zeros_like(l_ref)

        per_row = 0.5 * jnp.mean(e * e, axis=-1, keepdims=True)
        l_ref[...] += jnp.sum(per_row, axis=0, keepdims=True) * jnp.ones((1, 128), F32)

    return pl.pallas_call(body, grid=(S // tile,), in_specs=[_row_spec(D, tile)] * 2,
                          out_specs=[_par_spec((1, 128)), _row_spec(D, tile)],
                          out_shape=[jax.ShapeDtypeStruct((1, 128), F32), jax.ShapeDtypeStruct((S, D), F32)],
                          compiler_params=_cparams(("arbitrary",)), name=name)(y, tgt)


def add_rows(arrs, name):
    (out,) = tl_fwd(lambda *a: (functools.reduce(lambda p, q: p + q, a),), name, arrs, [], [arrs[0].shape[1]])
    return out


def small_fwd(fn, name, ins, out_shapes):
    n = len(ins)

    def body(*refs):
        outs = fn(*[r[...] for r in refs[:n]])
        for o_ref, o in zip(refs[n:], outs):
            o_ref[...] = o

    return pl.pallas_call(body, out_shape=[jax.ShapeDtypeStruct(s, F32) for s in out_shapes], name=name)(*ins)


def small_bwd(fn, name, ins, cts):
    n, m = len(ins), len(cts)

    def body(*refs):
        _, vjp = jax.vjp(lambda *a: tuple(fn(*a)), *[r[...] for r in refs[:n]])
        grads = vjp(tuple(r[...] for r in refs[n:n + m]))
        for o_ref, g in zip(refs[n + m:], grads):
            o_ref[...] = g

    return pl.pallas_call(body, out_shape=[jax.ShapeDtypeStruct(a.shape, F32) for a in ins], name=name)(*ins, *cts)


def param_prep(lower_bounds, mu0, mu1):
    e = jnp.exp(lower_bounds - jnp.max(lower_bounds, axis=0, keepdims=True))
    sm = e / jnp.sum(e, axis=0, keepdims=True)
    lb0 = sm[0:1] - sm[0:1]
    lb1 = sm[0:1] + sm[1:2] - sm[0:1]
    return lb0, lb1, mu0, 1.0 - mu0, mu1, 1.0 - mu1


def _rows_tile(rows):
    return _pick(rows, (256, 128, 64, 32, 16, 8))


def sum_parts(parts, name):
    P, rows, cols = parts.shape
    tile = _rows_tile(rows)

    def body(p_ref, o_ref):
        acc = p_ref[0]
        for p in range(1, P):
            acc = acc + p_ref[p]
        o_ref[...] = acc

    return pl.pallas_call(body, grid=(rows // tile,), in_specs=[pl.BlockSpec((P, tile, cols), lambda i: (0, i, 0))],
                          out_specs=pl.BlockSpec((tile, cols), lambda i: (i, 0)),
                          out_shape=jax.ShapeDtypeStruct((rows, cols), F32),
                          compiler_params=_cparams(("parallel",)), name=name)(parts)


def pair_sum(own, got, ids, name):
    _, P, rows, cols = own.shape
    tile = _rows_tile(rows)

    def body(ids_ref, own_ref, got_ref, o_ref):
        o_ref[0] = (own_ref[0, 0] + got_ref[0]).astype(BF16)

    grid_spec = pltpu.PrefetchScalarGridSpec(
        num_scalar_prefetch=1, grid=(P, rows // tile),
        in_specs=[pl.BlockSpec((1, 1, tile, cols), lambda s, i, ids: (ids[0], s, i, 0)),
                  pl.BlockSpec((1, tile, cols), lambda s, i, ids: (s, i, 0))],
        out_specs=pl.BlockSpec((1, tile, cols), lambda s, i, ids: (s, i, 0)))
    return pl.pallas_call(body, grid_spec=grid_spec, out_shape=jax.ShapeDtypeStruct((P, rows, cols), BF16),
                          compiler_params=_cparams(("parallel", "parallel")), name=name)(ids, own, got)


def sum_chips(slots, own, got, ids, name):
    P, rows, cols = slots.shape
    tile = _rows_tile(rows)

    def body(ids_ref, s_ref, own_ref, got_ref, o_ref):
        chip = ids_ref[1]
        mine = own_ref[0, 0] + got_ref[0]
        acc = None
        for p in range(P):
            term = jnp.where(chip == p, mine, s_ref[p].astype(F32))
            acc = term if acc is None else acc + term
        o_ref[0] = acc

    grid_spec = pltpu.PrefetchScalarGridSpec(
        num_scalar_prefetch=1, grid=(rows // tile,),
        in_specs=[pl.BlockSpec((P, tile, cols), lambda i, ids: (0, i, 0)),
                  pl.BlockSpec((1, 1, tile, cols), lambda i, ids: (ids[0], ids[1], i, 0)),
                  pl.BlockSpec((1, tile, cols), lambda i, ids: (ids[1], i, 0))],
        out_specs=pl.BlockSpec((1, tile, cols), lambda i, ids: (ids[0], i, 0)))
    return pl.pallas_call(body, grid_spec=grid_spec, out_shape=jax.ShapeDtypeStruct((2, rows, cols), F32),
                          compiler_params=_cparams(("parallel",)), name=name)(ids, slots, own, got)


def adamw(w, g, m, v, name):
    rows, cols = w.shape
    tile = _rows_tile(rows)

    def body(w_ref, g_ref, m_ref, v_ref, d_ref, nm_ref, nv_ref):
        gv = g_ref[...]
        nm = ADAM_B1 * m_ref[...] + (1.0 - ADAM_B1) * gv
        nv = ADAM_B2 * v_ref[...] + (1.0 - ADAM_B2) * jnp.square(gv)
        m_hat = nm / (1.0 - ADAM_B1 ** ADAM_STEP)
        v_hat = nv / (1.0 - ADAM_B2 ** ADAM_STEP)
        d_ref[...] = -ADAM_LR * (m_hat / (jnp.sqrt(v_hat) + ADAM_EPS) + ADAM_WD * w_ref[...])
        nm_ref[...] = nm
        nv_ref[...] = nv

    spec = pl.BlockSpec((tile, cols), lambda i: (i, 0))
    return pl.pallas_call(body, grid=(rows // tile,), in_specs=[spec] * 4, out_specs=[spec] * 3,
                          out_shape=[jax.ShapeDtypeStruct((rows, cols), F32)] * 3,
                          compiler_params=_cparams(("parallel",)), name=name)(w, g, m, v)


MESH = pl.DeviceIdType.MESH
ANY = pl.BlockSpec(memory_space=pl.ANY)


def _flip(v, bit):
    return 1 - v if bit else v


_CHIP_RELATIONS = ((1, 0), (0, 1), (1, 1))


def gather_chips(arrs, small, name):
    n = len(arrs)

    def body(*refs):
        ins, small_in = refs[:n], refs[n]
        outs, small_out = refs[n + 1:2 * n + 1], refs[2 * n + 1]
        send, recv, fsend, frecv, loc, ssend, srecv = refs[2 * n + 2:]
        x, y, c = lax.axis_index("x"), lax.axis_index("y"), lax.axis_index("c")
        me = 2 * x + y
        chips = [(_flip(x, bx), _flip(y, by)) for bx, by in _CHIP_RELATIONS]

        def over_ici(i, r, block_chip):
            return pltpu.make_async_remote_copy(src_ref=ins[i].at[c], dst_ref=outs[i].at[block_chip, c],
                                                send_sem=send.at[i, r], recv_sem=recv.at[i, r],
                                                device_id=(chips[r][0], chips[r][1], c), device_id_type=MESH)

        def to_sibling(i, r, layer):
            blk = outs[i].at[2 * chips[r][0] + chips[r][1], layer]
            return pltpu.make_async_remote_copy(src_ref=blk, dst_ref=blk, send_sem=fsend.at[i, r],
                                                recv_sem=frecv.at[i, r], device_id=(x, y, 1 - c), device_id_type=MESH)

        first = [over_ici(i, r, me) for i in range(n) for r in range(3)]
        smalls = [pltpu.make_async_remote_copy(src_ref=small_in, dst_ref=small_out.at[me], send_sem=ssend.at[r],
                                               recv_sem=srecv.at[r], device_id=(chips[r][0], chips[r][1], c),
                                               device_id_type=MESH) for r in range(3)]
        for cp in first + smalls:
            cp.start()
        local = [pltpu.make_async_copy(ins[i], outs[i].at[me], loc.at[i]) for i in range(n)]
        local.append(pltpu.make_async_copy(small_in, small_out.at[me], loc.at[n]))
        for cp in local:
            cp.start()
        passed = []
        for r in range(3):
            for i in range(n):
                over_ici(i, r, 2 * chips[r][0] + chips[r][1]).wait_recv()
                fw = to_sibling(i, r, c)
                fw.start()
                passed.append(fw)
        for r in range(3):
            for i in range(n):
                to_sibling(i, r, 1 - c).wait_recv()
        for cp in first + passed:
            cp.wait_send()
        for cp in smalls + local:
            cp.wait()

    return pl.pallas_call(
        body, in_specs=[ANY] * (n + 1), out_specs=[ANY] * (n + 1),
        out_shape=[jax.ShapeDtypeStruct((4,) + a.shape, a.dtype) for a in arrs]
        + [jax.ShapeDtypeStruct((4,) + small.shape, small.dtype)],
        scratch_shapes=[pltpu.SemaphoreType.DMA((n, 3)), pltpu.SemaphoreType.DMA((n, 3)), pltpu.SemaphoreType.DMA((n, 3)),
                        pltpu.SemaphoreType.DMA((n, 3)), pltpu.SemaphoreType.DMA((n + 1,)),
                        pltpu.SemaphoreType.DMA((3,)), pltpu.SemaphoreType.DMA((3,))],
        name=name)(*arrs, small)


_RELATIONS = tuple((r >> 2 & 1, r >> 1 & 1, r & 1) for r in range(1, 8))


def gather_devices(arr, name):
    def body(in_ref, out_ref, send, recv, loc):
        x, y, c = lax.axis_index("x"), lax.axis_index("y"), lax.axis_index("c")
        me = 4 * x + 2 * y + c
        lc = pltpu.make_async_copy(in_ref, out_ref.at[me], loc)
        lc.start()
        pending = [lc]
        for r, (bx, by, bc) in enumerate(_RELATIONS):
            cp = pltpu.make_async_remote_copy(src_ref=in_ref, dst_ref=out_ref.at[me], send_sem=send.at[r],
                                              recv_sem=recv.at[r], device_id=(_flip(x, bx), _flip(y, by), _flip(c, bc)),
                                              device_id_type=MESH)
            cp.start()
            pending.append(cp)
        for cp in pending:
            cp.wait()

    return pl.pallas_call(
        body, in_specs=[ANY], out_specs=ANY, out_shape=jax.ShapeDtypeStruct((8,) + arr.shape, arr.dtype),
        scratch_shapes=[pltpu.SemaphoreType.DMA((7,)), pltpu.SemaphoreType.DMA((7,)), pltpu.SemaphoreType.DMA(())],
        name=name)(arr)


def pair_exchange(arrs, name):
    n = len(arrs)

    def body(*refs):
        ins, outs = refs[:n], refs[n:2 * n]
        send, recv = refs[2 * n:]
        x, y, c = lax.axis_index("x"), lax.axis_index("y"), lax.axis_index("c")
        pending = []
        for i in range(n):
            for s in range(4):
                cp = pltpu.make_async_remote_copy(src_ref=ins[i].at[1 - c, s], dst_ref=outs[i].at[s],
                                                  send_sem=send.at[i, s], recv_sem=recv.at[i, s],
                                                  device_id=(x, y, 1 - c), device_id_type=MESH)
                cp.start()
                pending.append(cp)
        for cp in pending:
            cp.wait()

    return pl.pallas_call(
        body, in_specs=[ANY] * n, out_specs=[ANY] * n,
        out_shape=[jax.ShapeDtypeStruct(a.shape[1:], a.dtype) for a in arrs],
        scratch_shapes=[pltpu.SemaphoreType.DMA((n, 4)), pltpu.SemaphoreType.DMA((n, 4))],
        name=name)(*arrs)


def reduce_chips(arrs, name):
    n = len(arrs)

    def body(*refs):
        ins, outs = refs[:n], refs[n:2 * n]
        send, recv, loc = refs[2 * n:]
        x, y, c = lax.axis_index("x"), lax.axis_index("y"), lax.axis_index("c")
        me = 2 * x + y
        pending = []
        for i in range(n):
            for r, (bx, by) in enumerate(_CHIP_RELATIONS):
                px, py = _flip(x, bx), _flip(y, by)
                cp = pltpu.make_async_remote_copy(src_ref=ins[i].at[2 * px + py], dst_ref=outs[i].at[me],
                                                  send_sem=send.at[i, r], recv_sem=recv.at[i, r],
                                                  device_id=(px, py, c), device_id_type=MESH)
                cp.start()
                pending.append(cp)
        for i in range(n):
            lc = pltpu.make_async_copy(ins[i].at[me], outs[i].at[me], loc.at[i])
            lc.start()
            pending.append(lc)
        for cp in pending:
            cp.wait()

    return pl.pallas_call(
        body, in_specs=[ANY] * n, out_specs=[ANY] * n,
        out_shape=[jax.ShapeDtypeStruct(a.shape, a.dtype) for a in arrs],
        scratch_shapes=[pltpu.SemaphoreType.DMA((n, 3)), pltpu.SemaphoreType.DMA((n, 3)), pltpu.SemaphoreType.DMA((n,))],
        name=name)(*arrs)


EXCHANGE_PIECES = 8


def sibling_exchange(arrs, name):
    n = len(arrs)

    def body(*refs):
        bufs = refs[n:2 * n]
        send, recv = refs[2 * n:]
        x, y, c = lax.axis_index("x"), lax.axis_index("y"), lax.axis_index("c")
        pending = []
        for i in range(n):
            rows = bufs[i].shape[1] // EXCHANGE_PIECES
            for j in range(EXCHANGE_PIECES):
                piece = bufs[i].at[c, pl.ds(j * rows, rows)]
                cp = pltpu.make_async_remote_copy(src_ref=piece, dst_ref=piece, send_sem=send.at[i, j],
                                                  recv_sem=recv.at[i, j], device_id=(x, y, 1 - c), device_id_type=MESH)
                cp.start()
                pending.append(cp)
        for i in range(n):
            rows = bufs[i].shape[1] // EXCHANGE_PIECES
            for j in range(EXCHANGE_PIECES):
                landed = bufs[i].at[1 - c, pl.ds(j * rows, rows)]
                pltpu.make_async_remote_copy(src_ref=landed, dst_ref=landed, send_sem=send.at[i, j], recv_sem=recv.at[i, j],
                                             device_id=(x, y, 1 - c), device_id_type=MESH).wait_recv()
        for cp in pending:
            cp.wait_send()

    return pl.pallas_call(
        body, in_specs=[ANY] * n, out_specs=[ANY] * n,
        out_shape=[jax.ShapeDtypeStruct(a.shape, a.dtype) for a in arrs], input_output_aliases={i: i for i in range(n)},
        scratch_shapes=[pltpu.SemaphoreType.DMA((n, EXCHANGE_PIECES)), pltpu.SemaphoreType.DMA((n, EXCHANGE_PIECES))],
        name=name)(*arrs)


def rwkv_fwd(l, seg, taps, pars, vfirst):
    fs = fir_fwd(seg, taps, f"rwkv_shift_fwd{l}")
    rows = [fs] + ([vfirst] if l else [])
    R, W, K, V, KK, A, G = tl_fwd(rwkv_pre(l), f"rwkv_pre_fwd{l}", rows, pars["pre"],
                                  [HM64, HM64, HM64, D_GROUP, HM64, HM64, D_GROUP])
    vT = _to_headsT(V)
    yT, sall = scan_fwd(R, W, K, vT, KK, A, f"rwkv_scan_fwd{l}")
    Y = _from_headsT(yT)
    (out,) = tl_fwd(rwkv_post, f"rwkv_post_fwd{l}", [Y, R, K, V, G], pars["post"], [D_GROUP])
    return out, V, (seg, taps, rows, R, W, K, V, KK, A, G, Y, vT, sall)


def rwkv_bwd(l, saved, pars, dout, dv_extra):
    seg, taps, rows, R, W, K, V, KK, A, G, Y, vT, sall = saved
    (dY, dR1, dK1, dV1, dG), dpost = tl_bwd(rwkv_post, f"rwkv_post_bwd{l}", [Y, R, K, V, G], pars["post"], [[dout]])
    (dR2, dW, dK2, dKK, dA), dvT = scan_bwd(R, W, K, vT, KK, A, sall, _to_headsT(dY), f"rwkv_scan_bwd{l}")
    cts = [[dR1, dR2], [dW], [dK1, dK2], [dV1, _from_headsT(dvT)] + dv_extra, [dKK], [dA], [dG]]
    drows, dpre = tl_bwd(rwkv_pre(l), f"rwkv_pre_bwd{l}", rows, pars["pre"], cts)
    dseg, dtaps, _ = fir_bwd(seg, taps, [drows[0]], f"rwkv_shift_bwd{l}")
    return dseg, (drows[1] if l else None), dtaps, dpre, dpost


def attn_mix_fwd(l, proj):
    os_, ls_ = [], []
    for b, (_, d) in enumerate(DILATED_BRANCHES):
        o, lse = attn_fwd(proj, d, f"attn_fwd{l}_{b}")
        os_.append(o)
        ls_.append(lse)
    (out,) = tl_fwd(attn_combine, f"attn_combine_fwd{l}", os_ + ls_, [], [D_GROUP])
    return out, (proj, os_, ls_)


def attn_mix_bwd(l, saved, dout):
    proj, os_, ls_ = saved
    drows, _ = tl_bwd(attn_combine, f"attn_combine_bwd{l}", os_ + ls_, [], [[dout]])
    grads = [attn_bwd(proj, drows[b], drows[3 + b], d, f"attn_bwd{l}_{b}") for b, (_, d) in enumerate(DILATED_BRANCHES)]
    return tuple(add_rows([g[j] for g in grads], f"attn_d{'qkv'[j]}{l}") for j in range(3))


def ssd_fwd(l, z, xbc, dtr, pars):
    xc = fir_fwd(xbc, pars["taps"], f"ssd_conv_fwd{l}")
    rr, w, kk, xdt, xs = tl_fwd(ssd_pre, f"ssd_pre_fwd{l}", [xc, dtr], pars["pre"], [HM128, HM128, HM128, HM64, D_GROUP])
    blocks = [rr, kk, w, xdt]
    ys, states = chunk_fwd(ssd_chunk, f"ssd_scan_fwd{l}", blocks, (HEAD_DIM, SSD_STATE), HEAD_DIM, SSD_GROUP)
    (out,) = tl_fwd(ssd_post, f"ssd_post_fwd{l}", [ys, z, xs], pars["post"], [D_GROUP])
    return out, (z, xbc, dtr, xc, blocks, states, xs, ys)


def ssd_bwd(l, saved, pars, dout):
    z, xbc, dtr, xc, blocks, states, xs, ys = saved
    (dys, dz, dxs), dpost = tl_bwd(ssd_post, f"ssd_post_bwd{l}", [ys, z, xs], pars["post"], [[dout]])
    drr, dkk, dw, dxdt = chunk_bwd(ssd_chunk, f"ssd_scan_bwd{l}", blocks, states, dys, SSD_GROUP)
    (dxc, ddtr), dpre = tl_bwd(ssd_pre, f"ssd_pre_bwd{l}", [xc, dtr], pars["pre"], [[drr], [dw], [dkk], [dxdt], [dxs]])
    dxbc, dtaps, _ = fir_bwd(xbc, pars["taps"], [dxc], f"ssd_conv_bwd{l}")
    return dz, dxbc, ddtr, dtaps, dpre, dpost


def hgrn_fwd(l, seg, pars):
    blocks = tl_fwd(hgrn_pre, f"hgrn_pre_fwd{l}", [seg], pars["pre"], [HM64] * 4)
    o, states = chunk_fwd(hgrn_chunk, f"hgrn_scan_fwd{l}", blocks, (HEAD_DIM, HEAD_DIM), HEAD_DIM, HGRN_GROUP)
    (out,) = tl_fwd(hgrn_post, f"hgrn_post_fwd{l}", [o, seg], pars["post"], [D_GROUP])
    return out, (seg, blocks, states, o)


def hgrn_bwd(l, saved, pars, dout):
    seg, blocks, states, o = saved
    (do, dseg1), dpost = tl_bwd(hgrn_post, f"hgrn_post_bwd{l}", [o, seg], pars["post"], [[dout]])
    dq, dkk, dlf, di = chunk_bwd(hgrn_chunk, f"hgrn_scan_bwd{l}", blocks, states, do, HGRN_GROUP)
    (dseg2,), dpre = tl_bwd(hgrn_pre, f"hgrn_pre_bwd{l}", [seg], pars["pre"], [[dq], [dkk], [dlf], [di]])
    return add_rows([dseg1, dseg2], f"hgrn_dseg{l}"), dpre, dpost


def layer_fwd(l, x, wts, pars, vfirst):
    proj = matmul(x, wts["in"], "nn", f"proj_fwd{l}")
    seg_h = proj[:, SEG_HGRN:SEG_HGRN + 1024]
    seg_r = proj[:, SEG_RWKV:SEG_RWKV + 1024]
    z = proj[:, SEG_Z:SEG_Z + D_GROUP]
    xbc = proj[:, SEG_XBC:SEG_XBC + SSD_XBC]
    dtr = proj[:, SEG_DT:SEG_DT + 128]
    ya, v_rwkv, sa = rwkv_fwd(l, seg_r, pars["rwkv"]["taps"], pars["rwkv"], vfirst)
    yb, sb = attn_mix_fwd(l, proj)
    yc, sc = ssd_fwd(l, z, xbc, dtr, pars["ssd"])
    yd, sd = hgrn_fwd(l, seg_h, pars["hgrn"])
    mix = jnp.concatenate([ya, yb, yc, yd], axis=1)
    mo = matmul(mix, wts["out"], "nn", f"out_fwd{l}")
    (x1,) = tl_fwd(ln_res, f"ln1_fwd{l}", [x, mo], pars["ln1"], [D_MODEL])
    u = matmul(x1, wts["up"], "nn", f"up_fwd{l}")
    (h,) = tl_fwd(relu2, f"relu2_fwd{l}", [u], [], [D_FF])
    dn = matmul(h, wts["down"], "nn", f"down_fwd{l}")
    (x2,) = tl_fwd(ln_res, f"ln2_fwd{l}", [x1, dn], pars["ln2"], [D_MODEL])
    return x2, v_rwkv, (x, sa, sb, sc, sd, mix, mo, x1, u, h, dn)


def layer_bwd(l, saved, wts, pars, dx2, dv_extra):
    x, sa, sb, sc, sd, mix, mo, x1, u, h, dn = saved
    S = x.shape[0]
    g = {}
    (dx1a, ddn), g["ln2"] = tl_bwd(ln_res, f"ln2_bwd{l}", [x1, dn], pars["ln2"], [[dx2]])
    g["down"] = matmul(h, ddn, "tn", f"down_dw{l}")
    dh = matmul(ddn, wts["down"], "nt", f"down_dx{l}")
    (du,), _ = tl_bwd(relu2, f"relu2_bwd{l}", [u], [], [[dh]])
    g["up"] = matmul(x1, du, "tn", f"up_dw{l}")
    dx1 = matmul(du, wts["up"], "nt", f"up_dx{l}", add=dx1a)
    (dxa, dmo), g["ln1"] = tl_bwd(ln_res, f"ln1_bwd{l}", [x, mo], pars["ln1"], [[dx1]])
    g["out"] = matmul(mix, dmo, "tn", f"out_dw{l}")
    dmix = matmul(dmo, wts["out"], "nt", f"out_dx{l}")
    dya, dyb, dyc, dyd = (dmix[:, j * D_GROUP:(j + 1) * D_GROUP] for j in range(4))
    dseg_r, dvfirst, g["rwkv_taps"], g["rwkv_pre"], g["rwkv_post"] = rwkv_bwd(l, sa, pars["rwkv"], dya, dv_extra)
    dq, dk, dv = attn_mix_bwd(l, sb, dyb)
    dz, dxbc, ddtr, g["ssd_taps"], g["ssd_pre"], g["ssd_post"] = ssd_bwd(l, sc, pars["ssd"], dyc)
    dseg_h, g["hgrn_pre"], g["hgrn_post"] = hgrn_bwd(l, sd, pars["hgrn"], dyd)
    dproj = jnp.concatenate([dseg_h, dseg_r, dq, dk, dv, dz, dxbc, ddtr, jnp.zeros((S, PROJ_W - SEG_DT - 128), F32)], axis=1)
    g["in"] = matmul(x, dproj, "tn", f"proj_dw{l}")
    dx = matmul(dproj, wts["in"], "nt", f"proj_dx{l}", add=dxa)
    return dx, dvfirst, g


SMALL = ("lower_bounds", "w_in_vres", "mu_shift", "mu_vres", "rwkv_w0", "rwkv_w2", "rwkv_a0", "rwkv_a2", "rwkv_g2",
         "rwkv_k_k", "rwkv_k_a", "rwkv_r_k", "rwkv_lnx_w", "rwkv_lnx_b", "rwkv_v0", "rwkv_v2", "ssd_conv_w",
         "ssd_conv_b", "ssd_dt_bias", "ssd_A_log", "ssd_D", "ssd_norm_w", "hgrn_norm_w", "ln1_w", "ln1_b", "ln2_w", "ln2_b")
BIG = ("w_in", "w_out", "w_up", "w_down")
SMALL_SHARDED = {"w_in_vres": 1, "rwkv_w2": 2, "rwkv_a2": 2, "rwkv_g2": 2, "rwkv_v2": 2, "ssd_conv_w": 2}
WEIGHTS = ("lower_bounds", "w_in", "w_in_vres", "mu_shift", "mu_vres", "rwkv_w0", "rwkv_w2", "rwkv_a0", "rwkv_a2",
           "rwkv_g2", "rwkv_k_k", "rwkv_k_a", "rwkv_r_k", "rwkv_lnx_w", "rwkv_lnx_b", "rwkv_v0", "rwkv_v2",
           "ssd_conv_w", "ssd_conv_b", "ssd_dt_bias", "ssd_A_log", "ssd_D", "ssd_norm_w", "hgrn_norm_w", "w_out",
           "ln1_w", "ln1_b", "w_up", "w_down", "ln2_w", "ln2_b")


def _row(v, width=None):
    v = v.reshape(1, -1).astype(F32)
    if width is not None and v.shape[1] < width:
        v = jnp.pad(v, ((0, 0), (0, width - v.shape[1])))
    return v


def _rows_at(m, rows, at):
    return jnp.pad(m.astype(F32), ((at, rows - at - m.shape[0]), (0, 0)))


def _pad_w_in(w_in_l, vres):
    rows = w_in_l.shape[0]
    out = []
    order = sorted(_PIECES, key=lambda p: p[2])
    pos = 0
    for start, width, at in order:
        if at > pos:
            out.append(jnp.zeros((rows, at - pos), w_in_l.dtype))
        out.append(w_in_l[:, start:start + width])
        pos = at + width
        if at == SEG_RWKV and vres is not None:
            out.append(vres.astype(w_in_l.dtype))
            pos += vres.shape[1]
    out.append(jnp.zeros((rows, PROJ_W - pos), w_in_l.dtype))
    return jnp.concatenate(out, axis=1)


def _unpad_w_in(g):
    return jnp.concatenate([g[:, at:at + width] for _, width, at in _PIECES], axis=1)


def layer_params(l, sp, prep):
    lb, mu, om = prep[l], prep[2 + 2 * l], prep[3 + 2 * l]
    pre = [_row(sp["rwkv_w0"][l]), _rows_at(sp["rwkv_w2"][l], 128, 0), _row(sp["rwkv_a0"][l]),
           _rows_at(sp["rwkv_a2"][l], 128, 32), _rows_at(sp["rwkv_g2"][l], 128, 64),
           _row(sp["rwkv_k_k"][l]), _row(sp["rwkv_k_a"][l])]
    if l:
        pre += [_row(sp["rwkv_v0"][l - 1]), _rows_at(sp["rwkv_v2"][l - 1], 128, 0)]
    return {
        "rwkv": {"taps": jnp.concatenate([mu, om], axis=0), "pre": pre,
                 "post": [_row(sp["rwkv_lnx_w"][l]), _row(sp["rwkv_lnx_b"][l]), _row(sp["rwkv_r_k"][l])]},
        "ssd": {"taps": sp["ssd_conv_w"][l].astype(F32),
                "pre": [_row(sp["ssd_conv_b"][l]), _row(sp["ssd_dt_bias"][l], 128), _row(sp["ssd_A_log"][l], 128)],
                "post": [_row(sp["ssd_D"][l], 128), _row(sp["ssd_norm_w"][l])]},
        "hgrn": {"pre": [lb], "post": [_row(sp["hgrn_norm_w"][l])]},
        "ln1": [_row(sp["ln1_w"][l]), _row(sp["ln1_b"][l])],
        "ln2": [_row(sp["ln2_w"][l]), _row(sp["ln2_b"][l])],
    }


def _mu_full(sp, l):
    parts = [sp["mu_shift"][l].reshape(1, -1)]
    if l:
        parts.append(sp["mu_vres"][l - 1].reshape(1, -1))
    return _row(jnp.concatenate(parts, axis=1), 1024)


def local_step(x, target, big, sp):
    prep_in = [sp["lower_bounds"].astype(F32), _mu_full(sp, 0), _mu_full(sp, 1)]
    prep = small_fwd(param_prep, "param_prep_fwd", prep_in,
                     [(1, D_GROUP), (1, D_GROUP), (1, 1024), (1, 1024), (1, 1024), (1, 1024)])
    pars, wts = [], []
    for l in range(DEPTH):
        pars.append(layer_params(l, sp, prep))
        vres = sp["w_in_vres"][l - 1].astype(BF16) if l else None
        wts.append({"in": _pad_w_in(big["w_in"][l], vres), "out": big["w_out"][l], "up": big["w_up"][l],
                    "down": big["w_down"][l]})
    h, vfirst, saved = x, None, []
    for l in range(DEPTH):
        h, v_l, sv = layer_fwd(l, h, wts[l], pars[l], vfirst)
        vfirst = v_l if l == 0 else vfirst
        saved.append(sv)
    loss_row, dh = loss_and_grad(h, target, "loss")
    grads, dv_extra = [None] * DEPTH, []
    for l in reversed(range(DEPTH)):
        dh, dvfirst, grads[l] = layer_bwd(l, saved[l], wts[l], pars[l], dh, dv_extra)
        dv_extra = [dvfirst] if l else []
    cts = [grads[0]["hgrn_pre"][0], grads[1]["hgrn_pre"][0]]
    for l in range(DEPTH):
        cts += [grads[l]["rwkv_taps"][0:1], grads[l]["rwkv_taps"][1:2]]
    d_lower, d_mu0, d_mu1 = small_bwd(param_prep, "param_prep_bwd", prep_in, cts)
    d_mu = [d_mu0, d_mu1]
    gb = {"w_in": [_unpad_w_in(grads[l]["in"]) for l in range(DEPTH)], "w_out": [grads[l]["out"] for l in range(DEPTH)],
          "w_up": [grads[l]["up"] for l in range(DEPTH)], "w_down": [grads[l]["down"] for l in range(DEPTH)]}
    st = lambda f: jnp.stack([f(l) for l in range(DEPTH)])
    g1 = grads[1]
    gs = {
        "lower_bounds": d_lower,
        "w_in_vres": g1["in"][None, :, VRES_COL:VRES_COL + 32],
        "mu_shift": st(lambda l: d_mu[l][0, :896]),
        "mu_vres": d_mu[1][:, 896:928],
        "rwkv_w0": st(lambda l: grads[l]["rwkv_pre"][0][0]),
        "rwkv_w2": st(lambda l: grads[l]["rwkv_pre"][1][0:32]),
        "rwkv_a0": st(lambda l: grads[l]["rwkv_pre"][2][0]),
        "rwkv_a2": st(lambda l: grads[l]["rwkv_pre"][3][32:64]),
        "rwkv_g2": st(lambda l: grads[l]["rwkv_pre"][4][64:128]),
        "rwkv_k_k": st(lambda l: grads[l]["rwkv_pre"][5][0]),
        "rwkv_k_a": st(lambda l: grads[l]["rwkv_pre"][6][0]),
        "rwkv_r_k": st(lambda l: grads[l]["rwkv_post"][2].reshape(N_HEADS, HEAD_DIM)),
        "rwkv_lnx_w": st(lambda l: grads[l]["rwkv_post"][0][0]),
        "rwkv_lnx_b": st(lambda l: grads[l]["rwkv_post"][1][0]),
        "rwkv_v0": g1["rwkv_pre"][7],
        "rwkv_v2": g1["rwkv_pre"][8][None, 0:32],
        "ssd_conv_w": st(lambda l: grads[l]["ssd_taps"]),
        "ssd_conv_b": st(lambda l: grads[l]["ssd_pre"][0][0]),
        "ssd_dt_bias": st(lambda l: grads[l]["ssd_pre"][1][0, :N_HEADS]),
        "ssd_A_log": st(lambda l: grads[l]["ssd_pre"][2][0, :N_HEADS]),
        "ssd_D": st(lambda l: grads[l]["ssd_post"][0][0, :N_HEADS]),
        "ssd_norm_w": st(lambda l: grads[l]["ssd_post"][1][0]),
        "hgrn_norm_w": st(lambda l: grads[l]["hgrn_post"][0][0]),
        "ln1_w": st(lambda l: grads[l]["ln1"][0][0]),
        "ln1_b": st(lambda l: grads[l]["ln1"][1][0]),
        "ln2_w": st(lambda l: grads[l]["ln2"][0][0]),
        "ln2_b": st(lambda l: grads[l]["ln2"][1][0]),
    }
    return loss_row, dh, gb, gs


def _pack(vecs):
    flat, meta, pos = [], [], 0
    for v in vecs:
        flat.append(v.reshape(-1).astype(F32))
        meta.append((pos, v.shape))
        pos += v.size
    total = -(-pos // 1024) * 1024
    flat.append(jnp.zeros((total - pos,), F32))
    return jnp.concatenate(flat).reshape(total // 128, 128), meta


def _unpack(packed, meta):
    flat = packed.reshape(-1)
    return [flat[off:off + math.prod(shape)].reshape(shape) for off, shape in meta]


def _to_shards(name, g):
    if name == "w_in":
        return jnp.transpose(g.reshape(g.shape[0], 4, g.shape[1] // 4), (1, 0, 2))
    if name == "w_up":
        return jnp.transpose(g.reshape(g.shape[0], 4, g.shape[1] // 4), (1, 0, 2))
    return g.reshape(4, g.shape[0] // 4, g.shape[1])


def _from_chips(name, g):
    if name in ("w_in", "w_up"):
        return jnp.transpose(g, (1, 2, 0, 3)).reshape(g.shape[1], g.shape[2], 4 * g.shape[3])
    return jnp.transpose(g, (1, 0, 2, 3)).reshape(g.shape[1], 4 * g.shape[2], g.shape[3])


INPUT_NAMES = ("x",) + WEIGHTS + ("loss_target",) + tuple("m_" + n for n in WEIGHTS) + tuple("v_" + n for n in WEIGHTS)


def _step(*args):
    a = dict(zip(INPUT_NAMES, args, strict=True))
    chip = 2 * lax.axis_index("x") + lax.axis_index("y")

    sharded_names = list(SMALL_SHARDED)
    small_pack, small_meta = _pack([a[n] for n in sharded_names])
    gathered = gather_chips([a[n].astype(BF16) for n in BIG], small_pack, "gather_weights")
    big = {n: _from_chips(n, g) for n, g in zip(BIG, gathered)}
    sp = {n: a[n] for n in SMALL if n not in SMALL_SHARDED}
    per_chip = [_unpack(gathered[-1][s], small_meta) for s in range(4)]
    for j, n in enumerate(sharded_names):
        sp[n] = jnp.concatenate([per_chip[s][j] for s in range(4)], axis=SMALL_SHARDED[n])

    loss_row, gx, gb, gs = local_step(a["x"][0], a["loss_target"][0], big, sp)

    partials = [jnp.stack([_to_shards(n, gb[n][l]) for l in range(DEPTH)]) for n in BIG]
    got = pair_exchange(partials, "pair_exchange")
    ids = jnp.stack([lax.axis_index("c"), chip]).astype(jnp.int32)
    chip_sums = [pair_sum(p, q, ids, f"pair_sum_{n}") for n, p, q in zip(BIG, partials, got)]
    slots = reduce_chips(chip_sums, "reduce_big")
    mine = [sum_chips(sl, p, q, ids, f"sum_{n}") for n, sl, p, q in zip(BIG, slots, partials, got)]
    summed = sibling_exchange(mine, "exchange_big")
    out_g, out_d, out_m, out_v = {}, {}, {}, {}
    for n, g in zip(BIG, summed):
        shape = a[n].shape
        flat = lambda t: t.reshape(shape[0] * shape[1], shape[2])
        d, nm, nv = adamw(flat(a[n]), flat(g), flat(a["m_" + n]), flat(a["v_" + n]), f"adamw_{n}")
        out_g[n], out_d[n], out_m[n], out_v[n] = g.reshape(shape), d.reshape(shape), nm.reshape(shape), nv.reshape(shape)

    vec, meta = _pack([loss_row] + [gs[n] for n in SMALL])
    total = sum_parts(gather_devices(vec, "gather_small"), "sum_small")
    parts = _unpack(total, meta)
    loss = parts[0][0, 0]
    g_small = {}
    for n, g in zip(SMALL, parts[1:]):
        if n in SMALL_SHARDED:
            ax = SMALL_SHARDED[n]
            size = a[n].shape[ax]
            g = lax.dynamic_slice_in_dim(g, chip * size, size, axis=ax)
        g_small[n] = g
    pw, pmeta = _pack([a[n] for n in SMALL])
    pg, _ = _pack([g_small[n] for n in SMALL])
    pm, _ = _pack([a["m_" + n] for n in SMALL])
    pv, _ = _pack([a["v_" + n] for n in SMALL])
    d, nm, nv = adamw(pw, pg, pm, pv, "adamw_small")
    for n, dd, mm, vv in zip(SMALL, _unpack(d, pmeta), _unpack(nm, pmeta), _unpack(nv, pmeta)):
        out_g[n], out_d[n], out_m[n], out_v[n] = g_small[n], dd, mm, vv

    return (loss, gx[None], *[out_g[n] for n in WEIGHTS], *[out_d[n] for n in WEIGHTS],
            *[out_m[n] for n in WEIGHTS], *[out_v[n] for n in WEIGHTS])


def kernel(x, lower_bounds, w_in, w_in_vres, mu_shift, mu_vres, rwkv_w0, rwkv_w2, rwkv_a0, rwkv_a2, rwkv_g2, rwkv_k_k, rwkv_k_a, rwkv_r_k, rwkv_lnx_w, rwkv_lnx_b, rwkv_v0, rwkv_v2, ssd_conv_w, ssd_conv_b, ssd_dt_bias, ssd_A_log, ssd_D, ssd_norm_w, hgrn_norm_w, w_out, ln1_w, ln1_b, w_up, w_down, ln2_w, ln2_b, loss_target, m_lower_bounds, m_w_in, m_w_in_vres, m_mu_shift, m_mu_vres, m_rwkv_w0, m_rwkv_w2, m_rwkv_a0, m_rwkv_a2, m_rwkv_g2, m_rwkv_k_k, m_rwkv_k_a, m_rwkv_r_k, m_rwkv_lnx_w, m_rwkv_lnx_b, m_rwkv_v0, m_rwkv_v2, m_ssd_conv_w, m_ssd_conv_b, m_ssd_dt_bias, m_ssd_A_log, m_ssd_D, m_ssd_norm_w, m_hgrn_norm_w, m_w_out, m_ln1_w, m_ln1_b, m_w_up, m_w_down, m_ln2_w, m_ln2_b, v_lower_bounds, v_w_in, v_w_in_vres, v_mu_shift, v_mu_vres, v_rwkv_w0, v_rwkv_w2, v_rwkv_a0, v_rwkv_a2, v_rwkv_g2, v_rwkv_k_k, v_rwkv_k_a, v_rwkv_r_k, v_rwkv_lnx_w, v_rwkv_lnx_b, v_rwkv_v0, v_rwkv_v2, v_ssd_conv_w, v_ssd_conv_b, v_ssd_dt_bias, v_ssd_A_log, v_ssd_D, v_ssd_norm_w, v_hgrn_norm_w, v_w_out, v_ln1_w, v_ln1_b, v_w_up, v_w_down, v_ln2_w, v_ln2_b):
    return _step(x, lower_bounds, w_in, w_in_vres, mu_shift, mu_vres, rwkv_w0, rwkv_w2, rwkv_a0, rwkv_a2, rwkv_g2, rwkv_k_k, rwkv_k_a, rwkv_r_k, rwkv_lnx_w, rwkv_lnx_b, rwkv_v0, rwkv_v2, ssd_conv_w, ssd_conv_b, ssd_dt_bias, ssd_A_log, ssd_D, ssd_norm_w, hgrn_norm_w, w_out, ln1_w, ln1_b, w_up, w_down, ln2_w, ln2_b, loss_target, m_lower_bounds, m_w_in, m_w_in_vres, m_mu_shift, m_mu_vres, m_rwkv_w0, m_rwkv_w2, m_rwkv_a0, m_rwkv_a2, m_rwkv_g2, m_rwkv_k_k, m_rwkv_k_a, m_rwkv_r_k, m_rwkv_lnx_w, m_rwkv_lnx_b, m_rwkv_v0, m_rwkv_v2, m_ssd_conv_w, m_ssd_conv_b, m_ssd_dt_bias, m_ssd_A_log, m_ssd_D, m_ssd_norm_w, m_hgrn_norm_w, m_w_out, m_ln1_w, m_ln1_b, m_w_up, m_w_down, m_ln2_w, m_ln2_b, v_lower_bounds, v_w_in, v_w_in_vres, v_mu_shift, v_mu_vres, v_rwkv_w0, v_rwkv_w2, v_rwkv_a0, v_rwkv_a2, v_rwkv_g2, v_rwkv_k_k, v_rwkv_k_a, v_rwkv_r_k, v_rwkv_lnx_w, v_rwkv_lnx_b, v_rwkv_v0, v_rwkv_v2, v_ssd_conv_w, v_ssd_conv_b, v_ssd_dt_bias, v_ssd_A_log, v_ssd_D, v_ssd_norm_w, v_hgrn_norm_w, v_w_out, v_ln1_w, v_ln1_b, v_w_up, v_w_down, v_ln2_w, v_ln2_b)
```

```python
import functools
import math

import jax
import jax.numpy as jnp
from jax import lax
from jax.experimental import pallas as pl
from jax.experimental.pallas import tpu as pltpu

F32 = jnp.float32
BF16 = jnp.bfloat16
HI = lax.Precision.HIGHEST

DEPTH = 2
D_MODEL = 1024
D_GROUP = 256
HEAD_DIM = 64
N_HEADS = 4
SSD_STATE = 128
SSD_XBC = 768
SSD_CONV = 4
D_FF = 4096
ALPHA = (2.0 * DEPTH) ** 0.25
LN_EPS = 1e-5
RMS_EPS = 1e-5
RWKV_GN_EPS = HEAD_DIM * 1e-5
DILATED_BRANCHES = ((128, 1), (512, 4), (2048, 16))
ALIBI_SLOPES = tuple(2.0 ** (-8.0 * (h + 1) / N_HEADS) for h in range(N_HEADS))
ATTN_BLK = 128

ADAM_LR, ADAM_B1, ADAM_B2, ADAM_EPS, ADAM_WD, ADAM_STEP = 0.001, 0.9, 0.999, 1e-08, 0.01, 10

IN_COLS = 3716
PROJ_W = 4096
SEG_HGRN, SEG_RWKV, SEG_Q, SEG_Z, SEG_XBC, SEG_DT = 0, 1024, 2048, 2816, 3072, 3840
_PIECES = ((0, 896, SEG_RWKV), (896, 768, SEG_Q), (1664, 256, SEG_Z), (1920, 768, SEG_XBC),
           (2688, 4, SEG_DT), (2692, 1024, SEG_HGRN))
VRES_COL = SEG_RWKV + 896

HM64 = (N_HEADS, HEAD_DIM)
HM128 = (N_HEADS, SSD_STATE)
ROW_TILE = 256
SCAN_CHUNK = 128
VMEM_LIMIT = 48 * 1024 * 1024


def _cparams(sem=None):
    if sem is None:
        return pltpu.CompilerParams(vmem_limit_bytes=VMEM_LIMIT)
    return pltpu.CompilerParams(dimension_semantics=sem, vmem_limit_bytes=VMEM_LIMIT)


def _pick(n, pref):
    for t in pref:
        if n % t == 0:
            return t
    return n


def _relu2(u):
    r = jnp.maximum(u, 0.0)
    return r * r


def matmul(a, b, mode, name, add=None, a_relu2=False, relu2_grad_of=None):
    if mode == "nn":
        (M, K), (_, N) = a.shape, b.shape
    elif mode == "nt":
        (M, K), (N, _) = a.shape, b.shape
    else:
        (K, M), (_, N) = a.shape, b.shape
    tm, tn, tk = _pick(M, (1024, 512, 256, 128)), _pick(N, (1024, 512, 256, 128)), _pick(K, (1024, 512, 256, 128))
    nk = K // tk
    dims = {"nn": (((1,), (0,)), ((), ())), "nt": (((1,), (1,)), ((), ())), "tn": (((0,), (0,)), ((), ()))}[mode]
    extras = [e for e in (add, relu2_grad_of) if e is not None]

    def body(*refs):
        a_ref, b_ref = refs[:2]
        o_ref = refs[-1]
        rest = list(refs[2:-1])
        add_ref = rest.pop(0) if add is not None else None
        u_ref = rest.pop(0) if relu2_grad_of is not None else None
        k = pl.program_id(2)
        av = a_ref[...]
        if a_relu2:
            av = _relu2(av)
        d = lax.dot_general(av.astype(BF16), b_ref[...].astype(BF16), dims, preferred_element_type=F32)

        @pl.when(k == 0)
        def _():
            o_ref[...] = d if add_ref is None else d + add_ref[...]

        if nk > 1:
            @pl.when(k > 0)
            def _():
                o_ref[...] += d

        if u_ref is not None:
            @pl.when(k == nk - 1)
            def _():
                o_ref[...] = o_ref[...] * (2.0 * jnp.maximum(u_ref[...], 0.0))

    a_spec = pl.BlockSpec((tk, tm), lambda i, j, k: (k, i)) if mode == "tn" else pl.BlockSpec((tm, tk), lambda i, j, k: (i, k))
    b_spec = pl.BlockSpec((tn, tk), lambda i, j, k: (j, k)) if mode == "nt" else pl.BlockSpec((tk, tn), lambda i, j, k: (k, j))
    o_spec = pl.BlockSpec((tm, tn), lambda i, j, k: (i, j))
    ins, specs = [a, b] + extras, [a_spec, b_spec] + [o_spec] * len(extras)
    return pl.pallas_call(
        body, grid=(M // tm, N // tn, nk), in_specs=specs, out_specs=o_spec,
        out_shape=jax.ShapeDtypeStruct((M, N), F32),
        compiler_params=_cparams(("parallel", "parallel", "arbitrary")), name=name)(*ins)


def _row_spec(w, tile):
    return pl.BlockSpec((tile, w), lambda i: (i, 0))


def _par_spec(shape):
    return pl.BlockSpec(shape, lambda i: (0,) * len(shape))


class Cols:
    def __init__(self, arr, start, width):
        assert start % width == 0 or (start % 128 == 0 and width % 128 == 0)
        self.arr, self.start, self.width = arr, start, width
        self.shape, self.ndim = (arr.shape[0], width), 2


def _arr(r):
    return r.arr if isinstance(r, Cols) else r


def _rows_spec(a, tile):
    if isinstance(a, Cols):
        assert a.start % a.width == 0
        return pl.BlockSpec((tile, a.width), lambda i, blk=a.start // a.width: (i, blk))
    shape = a if isinstance(a, tuple) else a.shape
    if len(shape) == 3:
        return pl.BlockSpec((shape[0], tile, shape[2]), lambda i: (0, i, 0))
    return _row_spec(shape[1], tile)


def _rows_shape(S, w):
    return (w[0], S, w[1]) if isinstance(w, tuple) else (S, w)


def _rows_load(ref):
    if len(ref.shape) == 3:
        return jnp.concatenate([ref[h] for h in range(ref.shape[0])], axis=1)
    return ref[...]


def _rows_store(ref, val):
    if len(ref.shape) == 3:
        w = ref.shape[2]
        for h in range(ref.shape[0]):
            ref[h] = val[:, h * w:(h + 1) * w]
    else:
        ref[...] = val


def tl_fwd(fn, name, rows, pars, out_widths, tile=ROW_TILE):
    S = rows[0].shape[-2]
    nr = len(rows)

    def body(*refs):
        ins = [_rows_load(r) for r in refs[:nr]] + [r[...] for r in refs[nr:nr + len(pars)]]
        outs = fn(*ins)
        for o_ref, o in zip(refs[nr + len(pars):], outs):
            _rows_store(o_ref, o)

    shapes = [_rows_shape(S, w) for w in out_widths]
    return pl.pallas_call(
        body, grid=(S // tile,),
        in_specs=[_rows_spec(r, tile) for r in rows] + [_par_spec(p.shape) for p in pars],
        out_specs=[_rows_spec(s, tile) for s in shapes],
        out_shape=[jax.ShapeDtypeStruct(s, F32) for s in shapes],
        compiler_params=_cparams(("parallel",)), name=name)(*[_arr(r) for r in rows], *pars)


def tl_bwd(fn, name, rows, pars, cts, tile=ROW_TILE, row_grad=None):
    S = rows[0].shape[-2]
    nr, npar = len(rows), len(pars)
    row_grad = [True] * nr if row_grad is None else row_grad
    flat_cts = [c for group in cts for c in group]
    ncts = len(flat_cts)
    gi = [i for i in range(nr) if row_grad[i]]

    def body(*refs):
        row_v = [_rows_load(r) for r in refs[:nr]]
        par_v = [r[...] for r in refs[nr:nr + npar]]
        ct_refs = refs[nr + npar:nr + npar + ncts]
        out_refs = refs[nr + npar + ncts:]
        ct_v, pos = [], 0
        for group in cts:
            acc = _rows_load(ct_refs[pos])
            for q in range(1, len(group)):
                acc = acc + _rows_load(ct_refs[pos + q])
            pos += len(group)
            ct_v.append(acc)

        def f(diff_rows, par_vals):
            full = list(row_v)
            for idx, val in zip(gi, diff_rows):
                full[idx] = val
            return tuple(fn(*full, *par_vals))

        _, vjp = jax.vjp(f, [row_v[i] for i in gi], par_v)
        d_rows, d_pars = vjp(tuple(ct_v))
        for o_ref, g in zip(out_refs[:len(gi)], d_rows):
            _rows_store(o_ref, g)
        first = pl.program_id(0) == 0
        for o_ref, g in zip(out_refs[len(gi):], d_pars):
            @pl.when(first)
            def _(o_ref=o_ref):
                o_ref[...] = jnp.zeros_like(o_ref)
            o_ref[...] += g

    outs = pl.pallas_call(
        body, grid=(S // tile,),
        in_specs=[_rows_spec(r, tile) for r in rows] + [_par_spec(p.shape) for p in pars]
        + [_rows_spec(c, tile) for c in flat_cts],
        out_specs=[_rows_spec(rows[i].shape, tile) for i in gi] + [_par_spec(p.shape) for p in pars],
        out_shape=[jax.ShapeDtypeStruct(rows[i].shape, F32) for i in gi] + [jax.ShapeDtypeStruct(p.shape, F32) for p in pars],
        compiler_params=_cparams(("arbitrary",)), name=name)(*[_arr(r) for r in rows], *pars, *[_arr(c) for c in flat_cts])
    return list(outs[:len(gi)]), list(outs[len(gi):])


def _shift_rows(x, j):
    if j == 0:
        return x
    rolled = pltpu.roll(x, j, 0)
    row = lax.broadcasted_iota(jnp.int32, x.shape, 0)
    return jnp.where(row >= j, rolled, 0.0)


def _unshift_rows(x, j):
    if j == 0:
        return x
    S = x.shape[0]
    rolled = pltpu.roll(x, S - j, 0)
    row = lax.broadcasted_iota(jnp.int32, x.shape, 0)
    return jnp.where(row < S - j, rolled, 0.0)


def _fir_in_spec(x):
    first = x.start // 128 if isinstance(x, Cols) else 0
    return pl.BlockSpec((x.shape[0], 128), lambda j: (0, first + j))


def fir_fwd(x, taps, name):
    S, C = x.shape
    K = taps.shape[0]

    def body(x_ref, w_ref, y_ref):
        xv = x_ref[...]
        acc = jnp.zeros_like(xv)
        for k in range(K):
            acc = acc + _shift_rows(xv, K - 1 - k) * w_ref[pl.ds(k, 1), :]
        y_ref[...] = acc

    cs = pl.BlockSpec((S, 128), lambda j: (0, j))
    return pl.pallas_call(body, grid=(C // 128,), in_specs=[_fir_in_spec(x), pl.BlockSpec((K, 128), lambda j: (0, j))],
                          out_specs=cs, out_shape=jax.ShapeDtypeStruct((S, C), F32),
                          compiler_params=_cparams(("parallel",)), name=name)(_arr(x), taps)


def fir_bwd(x, taps, dy_list, name):
    S, C = x.shape
    K = taps.shape[0]
    n = len(dy_list)

    def body(*refs):
        x_ref, w_ref = refs[:2]
        dy = refs[2][...]
        for q in range(1, n):
            dy = dy + refs[2 + q][...]
        dx_ref, dw_ref, db_ref = refs[2 + n:]
        xv = x_ref[...]
        dx = jnp.zeros_like(xv)
        for k in range(K):
            j = K - 1 - k
            dx = dx + _unshift_rows(dy, j) * w_ref[pl.ds(k, 1), :]
            dw_ref[pl.ds(k, 1), :] = jnp.sum(dy * _shift_rows(xv, j), axis=0, keepdims=True)
        dx_ref[...] = dx
        db_ref[...] = jnp.sum(dy, axis=0, keepdims=True)

    cs = pl.BlockSpec((S, 128), lambda j: (0, j))
    ks = pl.BlockSpec((K, 128), lambda j: (0, j))
    bs = pl.BlockSpec((1, 128), lambda j: (0, j))
    return pl.pallas_call(body, grid=(C // 128,), in_specs=[_fir_in_spec(x), ks] + [cs] * n, out_specs=[cs, ks, bs],
                          out_shape=[jax.ShapeDtypeStruct((S, C), F32), jax.ShapeDtypeStruct((K, C), F32),
                                     jax.ShapeDtypeStruct((1, C), F32)],
                          compiler_params=_cparams(("parallel",)), name=name)(_arr(x), taps, *dy_list)


def _col(tile, lane, t):
    return jnp.sum(jnp.where(lane == t, tile, 0.0), axis=1, keepdims=True)


def _rwkv_step(s, rv, vcol):
    sa = jnp.sum(s * (-rv[3]), axis=1, keepdims=True)
    return s * rv[1] + sa * (rv[3] * rv[4]) + vcol * rv[2], sa


def scan_fwd(r, w, k, vT, kk, a, name):
    H, S, Dk = r.shape
    Dv = vT.shape[1]
    Tc = SCAN_CHUNK
    nc = S // Tc
    rows = [r, w, k, kk, a]

    def body(*refs):
        row_refs = refs[:5]
        vT_ref, yT_ref, sall_ref, s_ref = refs[5:]

        @pl.when(pl.program_id(0) == 0)
        def _():
            s_ref[...] = jnp.zeros_like(s_ref)

        yT_ref[...] = jnp.zeros_like(yT_ref)
        lane = lax.broadcasted_iota(jnp.int32, (Dv, Tc), 1)

        def step(t, states):
            new = []
            for h in range(H):
                s = states[h]
                sall_ref[t, h] = s
                rv = [ref[h, pl.ds(t, 1), :] for ref in row_refs]
                s, _ = _rwkv_step(s, rv, _col(vT_ref[h], lane, t))
                ycol = jnp.sum(s * rv[0], axis=1, keepdims=True)
                yT_ref[h] = jnp.where(lane == t, ycol, yT_ref[h])
                new.append(s)
            return tuple(new)

        states = lax.fori_loop(0, Tc, step, tuple(s_ref[h] for h in range(H)))
        for h in range(H):
            s_ref[h] = states[h]

    rs = pl.BlockSpec((H, Tc, Dk), lambda c: (0, c, 0))
    vs = pl.BlockSpec((H, Dv, Tc), lambda c: (0, 0, c))
    yT, sall = pl.pallas_call(
        body, grid=(nc,), in_specs=[rs] * 5 + [vs],
        out_specs=[vs, pl.BlockSpec((Tc, H, Dv, Dk), lambda c: (c, 0, 0, 0))],
        out_shape=[jax.ShapeDtypeStruct((H, Dv, S), F32), jax.ShapeDtypeStruct((S, H, Dv, Dk), F32)],
        scratch_shapes=[pltpu.VMEM((H, Dv, Dk), F32)],
        compiler_params=_cparams(("arbitrary",)), name=name)(*rows, vT)
    return yT, sall


def scan_bwd(r, w, k, vT, kk, a, sall, dyT, name):
    H, S, Dk = r.shape
    Dv = vT.shape[1]
    Tc = SCAN_CHUNK
    nc = S // Tc
    rows = [r, w, k, kk, a]

    def body(*refs):
        row_refs = refs[:5]
        vT_ref, dyT_ref, sall_ref = refs[5:8]
        drow_refs = refs[8:13]
        dvT_ref, ds_ref = refs[13:]

        @pl.when(pl.program_id(0) == 0)
        def _():
            ds_ref[...] = jnp.zeros_like(ds_ref)

        dvT_ref[...] = jnp.zeros_like(dvT_ref)
        lane = lax.broadcasted_iota(jnp.int32, (Dv, Tc), 1)

        def bstep(i, carry):
            t = Tc - 1 - i
            new = []
            for h in range(H):
                ds = carry[h]
                sp = sall_ref[t, h]
                rv = [ref[h, pl.ds(t, 1), :] for ref in row_refs]
                vcol = _col(vT_ref[h], lane, t)
                dycol = _col(dyT_ref[h], lane, t)
                st, sa = _rwkv_step(sp, rv, vcol)
                drow_refs[0][h, pl.ds(t, 1), :] = jnp.sum(st * dycol, axis=0, keepdims=True)
                g = ds + dycol * rv[0]
                drow_refs[1][h, pl.ds(t, 1), :] = jnp.sum(g * sp, axis=0, keepdims=True)
                drow_refs[2][h, pl.ds(t, 1), :] = jnp.sum(g * vcol, axis=0, keepdims=True)
                dvcol = jnp.sum(g * rv[2], axis=1, keepdims=True)
                dsa = jnp.sum(g * (rv[3] * rv[4]), axis=1, keepdims=True)
                db = jnp.sum(g * sa, axis=0, keepdims=True)
                dnkk = jnp.sum(sp * dsa, axis=0, keepdims=True)
                drow_refs[3][h, pl.ds(t, 1), :] = db * rv[4] - dnkk
                drow_refs[4][h, pl.ds(t, 1), :] = db * rv[3]
                dvT_ref[h] = jnp.where(lane == t, dvcol, dvT_ref[h])
                new.append(g * rv[1] - dsa * rv[3])
            return tuple(new)

        carry = lax.fori_loop(0, Tc, bstep, tuple(ds_ref[h] for h in range(H)))
        for h in range(H):
            ds_ref[h] = carry[h]

    rs = pl.BlockSpec((H, Tc, Dk), lambda c: (0, nc - 1 - c, 0))
    vs = pl.BlockSpec((H, Dv, Tc), lambda c: (0, 0, nc - 1 - c))
    outs = pl.pallas_call(
        body, grid=(nc,),
        in_specs=[rs] * 5 + [vs, vs, pl.BlockSpec((Tc, H, Dv, Dk), lambda c: (nc - 1 - c, 0, 0, 0))],
        out_specs=[rs] * 5 + [vs],
        out_shape=[jax.ShapeDtypeStruct((H, S, Dk), F32)] * 5 + [jax.ShapeDtypeStruct((H, Dv, S), F32)],
        scratch_shapes=[pltpu.VMEM((H, Dv, Dk), F32)],
        compiler_params=_cparams(("arbitrary",)), name=name)(*rows, vT, dyT, sall)
    return list(outs[:5]), outs[5]


CHUNK = 128
SSD_GROUP = 2
HGRN_GROUP = 4


def chunk_fwd(fn, name, blocks, state_shape, out_width, group):
    H, S, _ = blocks[0].shape
    nc = S // CHUNK
    nb = len(blocks)

    def body(*refs):
        o_ref, sv_ref, st = refs[nb:]

        @pl.when(pl.program_id(1) == 0)
        def _():
            st[...] = jnp.zeros_like(st)

        for g in range(group):
            s0 = st[g]
            sv_ref[g, 0] = s0
            s1, out = fn(s0, *[r[g] for r in refs[:nb]])
            st[g] = s1
            o_ref[g] = out

    spec = lambda w: pl.BlockSpec((group, CHUNK, w), lambda h, c: (h, c, 0))
    return pl.pallas_call(
        body, grid=(H // group, nc), in_specs=[spec(b.shape[2]) for b in blocks],
        out_specs=[spec(out_width), pl.BlockSpec((group, 1) + state_shape, lambda h, c: (h, c, 0, 0))],
        out_shape=[jax.ShapeDtypeStruct((H, S, out_width), F32), jax.ShapeDtypeStruct((H, nc) + state_shape, F32)],
        scratch_shapes=[pltpu.VMEM((group,) + state_shape, F32)],
        compiler_params=_cparams(("parallel", "arbitrary")), name=name)(*blocks)


def chunk_bwd(fn, name, blocks, states, dout, group):
    H, S, _ = blocks[0].shape
    nc = S // CHUNK
    nb = len(blocks)
    state_shape = states.shape[2:]

    def body(*refs):
        sv_ref, do_ref = refs[nb], refs[nb + 1]
        d_refs = refs[nb + 2:2 * nb + 2]
        dst = refs[2 * nb + 2]

        @pl.when(pl.program_id(1) == 0)
        def _():
            dst[...] = jnp.zeros_like(dst)

        for g in range(group):
            _, vjp = jax.vjp(fn, sv_ref[g, 0], *[r[g] for r in refs[:nb]])
            grads = vjp((dst[g], do_ref[g]))
            dst[g] = grads[0]
            for d_ref, gr in zip(d_refs, grads[1:]):
                d_ref[g] = gr

    spec = lambda w: pl.BlockSpec((group, CHUNK, w), lambda h, c: (h, nc - 1 - c, 0))
    return pl.pallas_call(
        body, grid=(H // group, nc),
        in_specs=[spec(b.shape[2]) for b in blocks]
        + [pl.BlockSpec((group, 1) + state_shape, lambda h, c: (h, nc - 1 - c, 0, 0)), spec(dout.shape[2])],
        out_specs=[spec(b.shape[2]) for b in blocks],
        out_shape=[jax.ShapeDtypeStruct(b.shape, F32) for b in blocks],
        scratch_shapes=[pltpu.VMEM((group,) + state_shape, F32)],
        compiler_params=_cparams(("parallel", "arbitrary")), name=name)(*blocks, states, dout)


def _bdot(a, b, dims):
    return lax.dot_general(a.astype(BF16), b.astype(BF16), (dims, ((), ())), preferred_element_type=F32)


def ssd_chunk(state, cb, bb, da, xdt):
    T = cb.shape[0]
    ti = lax.broadcasted_iota(jnp.int32, (T, T), 0)
    si = lax.broadcasted_iota(jnp.int32, (T, T), 1)
    mask = ti >= si
    cs = jnp.dot(mask.astype(F32), da, precision=HI, preferred_element_type=F32)
    pick = (lax.broadcasted_iota(jnp.int32, cs.shape, 1) == 0).astype(F32)
    cs_row = lax.dot_general(pick, cs, (((1,), (1,)), ((), ())), precision=HI, preferred_element_type=F32)
    lmat = jnp.where(mask, jnp.exp(jnp.where(mask, cs - cs_row, 0.0)), 0.0)
    scores = _bdot(cb, bb, ((1,), (1,))) * lmat
    y = _bdot(scores, xdt, ((1,), (0,))) + _bdot(cb, state, ((1,), (1,))) * jnp.exp(cs[:, :HEAD_DIM])
    last = cs[T - 1:T, :]
    new_state = state * jnp.exp(last) + _bdot(xdt, bb * jnp.exp(last - cs), ((0,), (0,)))
    return new_state, y


HGRN_SUB = 16


def hgrn_chunk(state, q, k, lf, v):
    T, C = q.shape[0], HGRN_SUB
    ti = lax.broadcasted_iota(jnp.int32, (C, C), 0)
    si = lax.broadcasted_iota(jnp.int32, (C, C), 1)
    tril = (ti >= si).astype(F32)
    row = lax.broadcasted_iota(jnp.int32, (C, q.shape[1]), 0)
    outs = []
    for j in range(T // C):
        qj, kj, lj, vj = (a[j * C:(j + 1) * C] for a in (q, k, lf, v))
        b = jnp.dot(tril, lj, precision=HI, preferred_element_type=F32)
        o = _bdot(qj * jnp.exp(b), state, ((1,), (1,)))
        for s in range(C):
            m = row >= s
            e = jnp.where(m, jnp.exp(jnp.where(m, b - b[s:s + 1], 0.0)), 0.0)
            o = o + jnp.sum(qj * kj[s:s + 1] * e, axis=1, keepdims=True) * vj[s:s + 1]
        last = b[C - 1:C]
        state = state * jnp.exp(last) + _bdot(vj, kj * jnp.exp(last - b), ((0,), (0,)))
        outs.append(o)
    return state, jnp.concatenate(outs, axis=0)


def _to_headsT(x):
    S = x.shape[0]
    return jnp.transpose(x.reshape(S, N_HEADS, HEAD_DIM), (1, 2, 0))


def _from_headsT(x):
    H, dv, S = x.shape
    return jnp.transpose(x, (2, 0, 1)).reshape(S, H * dv)


def _attn_block(q, kp, kc, vp, vc, n, slope, dilation):
    blk = ATTN_BLK
    k2 = jnp.concatenate([kp, kc], axis=0)
    v2 = jnp.concatenate([vp, vc], axis=0)
    s = _bdot(q, k2, ((1,), (1,))) * (HEAD_DIM ** -0.5)
    i = lax.broadcasted_iota(jnp.int32, (blk, 2 * blk), 0)
    j = lax.broadcasted_iota(jnp.int32, (blk, 2 * blk), 1)
    dist = blk + i - j
    first_key = jnp.where(n > 0, 0, blk)
    valid = (dist >= 0) & (dist <= blk) & (j >= first_key)
    s = s - slope * (dist * dilation).astype(F32)
    s = jnp.where(valid, s, -1e30)
    m = jnp.max(s, axis=-1, keepdims=True)
    p = jnp.exp(s - m)
    l = jnp.sum(p, axis=-1, keepdims=True)
    o = _bdot(p, v2, ((1,), (0,))) / l
    lse = jnp.broadcast_to(m + jnp.log(l), o.shape)
    return o, lse


_QCOL = SEG_Q // 128
PAIR = 2 * HEAD_DIM


def _attn_specs(rows):
    cur = lambda j: pl.BlockSpec((rows, PAIR), lambda p, n: (n, j + p))
    prev = lambda j: pl.BlockSpec((rows, PAIR), lambda p, n: (jnp.maximum(n - 1, 0), j + p))
    return cur, prev


def _pair_slope(pair, h):
    return jnp.where(pair == 0, jnp.float32(ALIBI_SLOPES[h]), jnp.float32(ALIBI_SLOPES[2 + h]))


def _halves(t):
    return [t[:, h * HEAD_DIM:(h + 1) * HEAD_DIM] for h in range(2)]


def _for_classes(dilation, step):
    if dilation == 1:
        step(0)
    else:
        lax.fori_loop(0, dilation, lambda z, c: (step(z), c)[1], 0)


def attn_fwd(proj, dilation, name):
    S = proj.shape[0]
    blk = ATTN_BLK
    rows = blk * dilation
    cur, prev = _attn_specs(rows)

    def body(q_ref, kp_ref, kc_ref, vp_ref, vc_ref, o_ref, l_ref):
        pair, n = pl.program_id(0), pl.program_id(1)

        def one_class(z):
            sel = pl.ds(z, blk, stride=dilation) if dilation > 1 else pl.ds(0, blk)
            q, kp, kc, vp, vc = (_halves(r[sel, :]) for r in (q_ref, kp_ref, kc_ref, vp_ref, vc_ref))
            res = [_attn_block(q[h], kp[h], kc[h], vp[h], vc[h], n, _pair_slope(pair, h), dilation) for h in range(2)]
            o_ref[sel, :] = jnp.concatenate([r[0] for r in res], axis=1)
            l_ref[sel, :] = jnp.concatenate([r[1] for r in res], axis=1)

        _for_classes(dilation, one_class)

    return pl.pallas_call(
        body, grid=(2, S // rows),
        in_specs=[cur(_QCOL), prev(_QCOL + 2), cur(_QCOL + 2), prev(_QCOL + 4), cur(_QCOL + 4)],
        out_specs=[cur(0), cur(0)], out_shape=[jax.ShapeDtypeStruct((S, D_GROUP), F32)] * 2,
        compiler_params=_cparams(("parallel", "arbitrary")), name=name)(proj, proj, proj, proj, proj)


def attn_bwd(proj, do, dlse, dilation, name):
    S = proj.shape[0]
    blk = ATTN_BLK
    rows = blk * dilation
    cur, prev = _attn_specs(rows)
    full = pl.BlockSpec((S, PAIR), lambda p, n: (0, p))

    def body(q_ref, kp_ref, kc_ref, vp_ref, vc_ref, do_ref, dl_ref, dq_ref, dk_ref, dv_ref):
        pair, n = pl.program_id(0), pl.program_id(1)

        @pl.when(n == 0)
        def _():
            dk_ref[...] = jnp.zeros_like(dk_ref)
            dv_ref[...] = jnp.zeros_like(dv_ref)

        def one_class(z):
            sel = pl.ds(z, blk, stride=dilation) if dilation > 1 else pl.ds(0, blk)
            q, kp, kc, vp, vc, do_v, dl_v = (_halves(r[sel, :]) for r in
                                             (q_ref, kp_ref, kc_ref, vp_ref, vc_ref, do_ref, dl_ref))
            grads = []
            for h in range(2):
                f = lambda q_, kp_, kc_, vp_, vc_, h=h: _attn_block(q_, kp_, kc_, vp_, vc_, n, _pair_slope(pair, h), dilation)
                _, vjp = jax.vjp(f, q[h], kp[h], kc[h], vp[h], vc[h])
                grads.append(vjp((do_v[h], dl_v[h])))
            both = lambda j: jnp.concatenate([grads[0][j], grads[1][j]], axis=1)
            dq_ref[sel, :] = both(0)
            if dilation > 1:
                here = pl.ds(n * rows + z, blk, stride=dilation)
                before = pl.ds(jnp.maximum(n - 1, 0) * rows + z, blk, stride=dilation)
            else:
                here = pl.ds(pl.multiple_of(n * blk, blk), blk)
                before = pl.ds(pl.multiple_of(jnp.maximum(n - 1, 0) * blk, blk), blk)
            dk_ref[here, :] = dk_ref[here, :] + both(2)
            dv_ref[here, :] = dv_ref[here, :] + both(4)
            dk_ref[before, :] = dk_ref[before, :] + both(1)
            dv_ref[before, :] = dv_ref[before, :] + both(3)

        _for_classes(dilation, one_class)

    return pl.pallas_call(
        body, grid=(2, S // rows),
        in_specs=[cur(_QCOL), prev(_QCOL + 2), cur(_QCOL + 2), prev(_QCOL + 4), cur(_QCOL + 4), cur(0), cur(0)],
        out_specs=[cur(0), full, full], out_shape=[jax.ShapeDtypeStruct((S, D_GROUP), F32)] * 3,
        compiler_params=_cparams(("parallel", "arbitrary")), name=name)(proj, proj, proj, proj, proj, do, dlse)


def _head_ones(width, group):
    i = lax.broadcasted_iota(jnp.int32, (width, width), 0) // group
    j = lax.broadcasted_iota(jnp.int32, (width, width), 1) // group
    return (i == j).astype(F32)


def _group_sum(x, group):
    return jnp.dot(x, _head_ones(x.shape[1], group), precision=HI, preferred_element_type=F32)


def _spread(width_in, width_out, rep):
    i = lax.broadcasted_iota(jnp.int32, (width_in, width_out), 0)
    j = lax.broadcasted_iota(jnp.int32, (width_in, width_out), 1) // rep
    return (i == j).astype(F32)


def _hdot(a, b):
    return jnp.dot(a, b, precision=HI, preferred_element_type=F32)


def _sigmoid(x):
    return 1.0 / (1.0 + jnp.exp(-x))


def _softplus(x):
    return jnp.maximum(x, 0.0) + jnp.log(1.0 + jnp.exp(jnp.minimum(x, -x)))


def _silu(x):
    return x * _sigmoid(x)


def rwkv_pre(layer):
    def fn(*args):
        if layer == 0:
            fs, w0, w2p, a0, a2p, g2p, k_k, k_a = args
        else:
            fs, vfirst, w0, w2p, a0, a2p, g2p, k_k, k_a, v0, v2p = args
        r, k, v = fs[:, 0:256], fs[:, 256:512], fs[:, 512:768]
        lora = fs[:, 768:896]
        w_log = -_softplus(-(w0 + _hdot(jnp.tanh(lora), w2p))) - 0.5
        decay = jnp.exp(-jnp.exp(w_log))
        a = _sigmoid(a0 + _hdot(lora, a2p))
        g = _hdot(_sigmoid(lora), g2p)
        if layer > 0:
            v = v + (vfirst - v) * _sigmoid(v0 + _hdot(fs[:, 896:1024], v2p))
        kk = k * k_k
        kk = kk / jnp.maximum(jnp.sqrt(_group_sum(kk * kk, HEAD_DIM)), 1e-12)
        k = k * (1.0 + (a - 1.0) * k_a)
        return r, decay, k, v, kk, a, g
    return fn


def rwkv_post(y, r, k, v, g, lnx_w, lnx_b, r_k):
    mu = _group_sum(y, HEAD_DIM) * (1.0 / HEAD_DIM)
    yc = y - mu
    var = _group_sum(yc * yc, HEAD_DIM) * (1.0 / HEAD_DIM)
    yn = yc * lax.rsqrt(var + RWKV_GN_EPS) * lnx_w + lnx_b
    bonus = _group_sum(r * k * r_k, HEAD_DIM) * v
    return ((yn + bonus) * g,)


def attn_combine(o1, o2, o3, l1, l2, l3):
    m = jnp.maximum(jnp.maximum(l1, l2), l3)
    e1, e2, e3 = jnp.exp(l1 - m), jnp.exp(l2 - m), jnp.exp(l3 - m)
    return ((o1 * e1 + o2 * e2 + o3 * e3) / (e1 + e2 + e3),)


def ssd_pre(xc, dtr, conv_b, dt_bias, a_log):
    xbc = _silu(xc + conv_b)
    xs, bm, cm = xbc[:, 0:256], xbc[:, 256:512], xbc[:, 512:768]
    dt = _softplus(dtr + dt_bias)
    a_neg = -jnp.exp(a_log)
    wide = _spread(128, N_HEADS * SSD_STATE, SSD_STATE)
    w = _hdot(dt, wide) * _hdot(a_neg, wide)
    xdt = xs * _hdot(dt, _spread(128, D_GROUP, HEAD_DIM))
    rr = jnp.concatenate([cm[:, 0:128], cm[:, 0:128], cm[:, 128:256], cm[:, 128:256]], axis=1)
    kk = jnp.concatenate([bm[:, 0:128], bm[:, 0:128], bm[:, 128:256], bm[:, 128:256]], axis=1)
    return rr, w, kk, xdt, xs


def ssd_post(ys, z, xs, d_skip, norm_w):
    y = ys + xs * _hdot(d_skip, _spread(128, D_GROUP, HEAD_DIM))
    y = y * _silu(z)
    half = D_GROUP // 2
    parts = []
    for g in range(2):
        t = y[:, g * half:(g + 1) * half]
        parts.append(t * lax.rsqrt(jnp.mean(t * t, axis=-1, keepdims=True) + RMS_EPS))
    return (jnp.concatenate(parts, axis=1) * norm_w,)


def hgrn_pre(seg, lb):
    q, f, i = seg[:, 0:256], seg[:, 256:512], seg[:, 512:768]
    forget = lb + (1.0 - lb) * _sigmoid(f)
    return _silu(q), 1.0 - forget, jnp.log(forget), i


def hgrn_post(o, seg, norm_w):
    g = seg[:, 768:1024]
    ms = _group_sum(o * o, HEAD_DIM) * (1.0 / HEAD_DIM)
    return (o * lax.rsqrt(ms + RMS_EPS) * norm_w * _silu(g),)


def ln_res(x, y, w, b):
    z = ALPHA * x + y
    mu = jnp.mean(z, axis=-1, keepdims=True)
    zc = z - mu
    var = jnp.mean(zc * zc, axis=-1, keepdims=True)
    return (zc * lax.rsqrt(var + LN_EPS) * w + b,)


def loss_and_grad(y, tgt, name):
    S, D = y.shape
    tile = ROW_TILE

    def body(y_ref, t_ref, l_ref, dy_ref):
        e = y_ref[...] - t_ref[...]
        dy_ref[...] = e * (1.0 / D)

        @pl.when(pl.program_id(0) == 0)
        def _():
            l_ref[...] = jnp.zeros_like(l_ref)

        per_row = 0.5 * jnp.mean(e * e, axis=-1, keepdims=True)
        l_ref[...] += jnp.sum(per_row, axis=0, keepdims=True) * jnp.ones((1, 128), F32)

    return pl.pallas_call(body, grid=(S // tile,), in_specs=[_row_spec(D, tile)] * 2,
                          out_specs=[_par_spec((1, 128)), _row_spec(D, tile)],
                          out_shape=[jax.ShapeDtypeStruct((1, 128), F32), jax.ShapeDtypeStruct((S, D), F32)],
                          compiler_params=_cparams(("arbitrary",)), name=name)(y, tgt)


def add_rows(arrs, name):
    (out,) = tl_fwd(lambda *a: (functools.reduce(lambda p, q: p + q, a),), name, arrs, [], [arrs[0].shape[1]])
    return out


def small_fwd(fn, name, ins, out_shapes):
    n = len(ins)

    def body(*refs):
        outs = fn(*[r[...] for r in refs[:n]])
        for o_ref, o in zip(refs[n:], outs):
            o_ref[...] = o

    return pl.pallas_call(body, out_shape=[jax.ShapeDtypeStruct(s, F32) for s in out_shapes], name=name)(*ins)


def small_bwd(fn, name, ins, cts):
    n, m = len(ins), len(cts)

    def body(*refs):
        _, vjp = jax.vjp(lambda *a: tuple(fn(*a)), *[r[...] for r in refs[:n]])
        grads = vjp(tuple(r[...] for r in refs[n:n + m]))
        for o_ref, g in zip(refs[n + m:], grads):
            o_ref[...] = g

    return pl.pallas_call(body, out_shape=[jax.ShapeDtypeStruct(a.shape, F32) for a in ins], name=name)(*ins, *cts)


def param_prep(lower_bounds, mu0, mu1):
    e = jnp.exp(lower_bounds - jnp.max(lower_bounds, axis=0, keepdims=True))
    sm = e / jnp.sum(e, axis=0, keepdims=True)
    lb0 = sm[0:1] - sm[0:1]
    lb1 = sm[0:1] + sm[1:2] - sm[0:1]
    return lb0, lb1, mu0, 1.0 - mu0, mu1, 1.0 - mu1


def _rows_tile(rows):
    return _pick(rows, (256, 128, 64, 32, 16, 8))


def sum_parts(parts, name):
    P, rows, cols = parts.shape
    tile = _rows_tile(rows)

    def body(p_ref, o_ref):
        acc = p_ref[0]
        for p in range(1, P):
            acc = acc + p_ref[p]
        o_ref[...] = acc

    return pl.pallas_call(body, grid=(rows // tile,), in_specs=[pl.BlockSpec((P, tile, cols), lambda i: (0, i, 0))],
                          out_specs=pl.BlockSpec((tile, cols), lambda i: (i, 0)),
                          out_shape=jax.ShapeDtypeStruct((rows, cols), F32),
                          compiler_params=_cparams(("parallel",)), name=name)(parts)


def pair_sum(own, got, ids, name):
    _, P, rows, cols = own.shape
    tile = _rows_tile(rows)

    def body(ids_ref, own_ref, got_ref, o_ref):
        o_ref[0] = (own_ref[0, 0] + got_ref[0]).astype(BF16)

    grid_spec = pltpu.PrefetchScalarGridSpec(
        num_scalar_prefetch=1, grid=(P, rows // tile),
        in_specs=[pl.BlockSpec((1, 1, tile, cols), lambda s, i, ids: (ids[0], s, i, 0)),
                  pl.BlockSpec((1, tile, cols), lambda s, i, ids: (s, i, 0))],
        out_specs=pl.BlockSpec((1, tile, cols), lambda s, i, ids: (s, i, 0)))
    return pl.pallas_call(body, grid_spec=grid_spec, out_shape=jax.ShapeDtypeStruct((P, rows, cols), BF16),
                          compiler_params=_cparams(("parallel", "parallel")), name=name)(ids, own, got)


def sum_chips(slots, own, got, ids, name):
    P, rows, cols = slots.shape
    tile = _rows_tile(rows)

    def body(ids_ref, s_ref, own_ref, got_ref, o_ref):
        chip = ids_ref[1]
        mine = own_ref[0, 0] + got_ref[0]
        acc = None
        for p in range(P):
            term = jnp.where(chip == p, mine, s_ref[p].astype(F32))
            acc = term if acc is None else acc + term
        o_ref[0] = acc

    grid_spec = pltpu.PrefetchScalarGridSpec(
        num_scalar_prefetch=1, grid=(rows // tile,),
        in_specs=[pl.BlockSpec((P, tile, cols), lambda i, ids: (0, i, 0)),
                  pl.BlockSpec((1, 1, tile, cols), lambda i, ids: (ids[0], ids[1], i, 0)),
                  pl.BlockSpec((1, tile, cols), lambda i, ids: (ids[1], i, 0))],
        out_specs=pl.BlockSpec((1, tile, cols), lambda i, ids: (ids[0], i, 0)))
    return pl.pallas_call(body, grid_spec=grid_spec, out_shape=jax.ShapeDtypeStruct((2, rows, cols), F32),
                          compiler_params=_cparams(("parallel",)), name=name)(ids, slots, own, got)


def adamw(w, g, m, v, name):
    rows, cols = w.shape
    tile = _rows_tile(rows)

    def body(w_ref, g_ref, m_ref, v_ref, d_ref, nm_ref, nv_ref):
        gv = g_ref[...]
        nm = ADAM_B1 * m_ref[...] + (1.0 - ADAM_B1) * gv
        nv = ADAM_B2 * v_ref[...] + (1.0 - ADAM_B2) * jnp.square(gv)
        m_hat = nm / (1.0 - ADAM_B1 ** ADAM_STEP)
        v_hat = nv / (1.0 - ADAM_B2 ** ADAM_STEP)
        d_ref[...] = -ADAM_LR * (m_hat / (jnp.sqrt(v_hat) + ADAM_EPS) + ADAM_WD * w_ref[...])
        nm_ref[...] = nm
        nv_ref[...] = nv

    spec = pl.BlockSpec((tile, cols), lambda i: (i, 0))
    return pl.pallas_call(body, grid=(rows // tile,), in_specs=[spec] * 4, out_specs=[spec] * 3,
                          out_shape=[jax.ShapeDtypeStruct((rows, cols), F32)] * 3,
                          compiler_params=_cparams(("parallel",)), name=name)(w, g, m, v)


MESH = pl.DeviceIdType.MESH
ANY = pl.BlockSpec(memory_space=pl.ANY)


def _flip(v, bit):
    return 1 - v if bit else v


_CHIP_RELATIONS = ((1, 0), (0, 1), (1, 1))


def gather_chips(arrs, small, name):
    n = len(arrs)

    def body(*refs):
        ins, small_in = refs[:n], refs[n]
        outs, small_out = refs[n + 1:2 * n + 1], refs[2 * n + 1]
        send, recv, fsend, frecv, ssend, srecv = refs[2 * n + 2:]
        x, y, c = lax.axis_index("x"), lax.axis_index("y"), lax.axis_index("c")
        me = 2 * x + y
        chips = [(_flip(x, bx), _flip(y, by)) for bx, by in _CHIP_RELATIONS]

        def over_ici(i, r, block_chip):
            return pltpu.make_async_remote_copy(src_ref=ins[i].at[c], dst_ref=outs[i].at[block_chip, c],
                                                send_sem=send.at[i, r], recv_sem=recv.at[i, r],
                                                device_id=(chips[r][0], chips[r][1], c), device_id_type=MESH)

        def to_sibling(i, r, layer):
            blk = outs[i].at[2 * chips[r][0] + chips[r][1], layer]
            return pltpu.make_async_remote_copy(src_ref=blk, dst_ref=blk, send_sem=fsend.at[i, r],
                                                recv_sem=frecv.at[i, r], device_id=(x, y, 1 - c), device_id_type=MESH)

        first = [over_ici(i, r, me) for i in range(n) for r in range(3)]
        smalls = [pltpu.make_async_remote_copy(src_ref=small_in, dst_ref=small_out.at[me], send_sem=ssend.at[r],
                                               recv_sem=srecv.at[r], device_id=(chips[r][0], chips[r][1], c),
                                               device_id_type=MESH) for r in range(3)]
        for cp in first + smalls:
            cp.start()
        passed = []
        for r in range(3):
            for i in range(n):
                over_ici(i, r, 2 * chips[r][0] + chips[r][1]).wait_recv()
                fw = to_sibling(i, r, c)
                fw.start()
                passed.append(fw)
        for r in range(3):
            for i in range(n):
                to_sibling(i, r, 1 - c).wait_recv()
        for cp in first + passed:
            cp.wait_send()
        for cp in smalls:
            cp.wait()

    return pl.pallas_call(
        body, in_specs=[ANY] * (n + 1), out_specs=[ANY] * (n + 1),
        out_shape=[jax.ShapeDtypeStruct((4,) + a.shape, a.dtype) for a in arrs]
        + [jax.ShapeDtypeStruct((4,) + small.shape, small.dtype)],
        scratch_shapes=[pltpu.SemaphoreType.DMA((n, 3)), pltpu.SemaphoreType.DMA((n, 3)), pltpu.SemaphoreType.DMA((n, 3)),
                        pltpu.SemaphoreType.DMA((n, 3)), pltpu.SemaphoreType.DMA((3,)), pltpu.SemaphoreType.DMA((3,))],
        name=name)(*arrs, small)


_RELATIONS = tuple((r >> 2 & 1, r >> 1 & 1, r & 1) for r in range(1, 8))


def gather_devices(arr, name):
    def body(in_ref, out_ref, send, recv, loc):
        x, y, c = lax.axis_index("x"), lax.axis_index("y"), lax.axis_index("c")
        me = 4 * x + 2 * y + c
        lc = pltpu.make_async_copy(in_ref, out_ref.at[me], loc)
        lc.start()
        pending = [lc]
        for r, (bx, by, bc) in enumerate(_RELATIONS):
            cp = pltpu.make_async_remote_copy(src_ref=in_ref, dst_ref=out_ref.at[me], send_sem=send.at[r],
                                              recv_sem=recv.at[r], device_id=(_flip(x, bx), _flip(y, by), _flip(c, bc)),
                                              device_id_type=MESH)
            cp.start()
            pending.append(cp)
        for cp in pending:
            cp.wait()

    return pl.pallas_call(
        body, in_specs=[ANY], out_specs=ANY, out_shape=jax.ShapeDtypeStruct((8,) + arr.shape, arr.dtype),
        scratch_shapes=[pltpu.SemaphoreType.DMA((7,)), pltpu.SemaphoreType.DMA((7,)), pltpu.SemaphoreType.DMA(())],
        name=name)(arr)


def pair_exchange(arrs, name):
    n = len(arrs)

    def body(*refs):
        ins, outs = refs[:n], refs[n:2 * n]
        send, recv = refs[2 * n:]
        x, y, c = lax.axis_index("x"), lax.axis_index("y"), lax.axis_index("c")
        pending = []
        for i in range(n):
            for s in range(4):
                cp = pltpu.make_async_remote_copy(src_ref=ins[i].at[1 - c, s], dst_ref=outs[i].at[s],
                                                  send_sem=send.at[i, s], recv_sem=recv.at[i, s],
                                                  device_id=(x, y, 1 - c), device_id_type=MESH)
                cp.start()
                pending.append(cp)
        for cp in pending:
            cp.wait()

    return pl.pallas_call(
        body, in_specs=[ANY] * n, out_specs=[ANY] * n,
        out_shape=[jax.ShapeDtypeStruct(a.shape[1:], a.dtype) for a in arrs],
        scratch_shapes=[pltpu.SemaphoreType.DMA((n, 4)), pltpu.SemaphoreType.DMA((n, 4))],
        name=name)(*arrs)


def reduce_chips(arrs, name):
    n = len(arrs)

    def body(*refs):
        ins, outs = refs[:n], refs[n:2 * n]
        send, recv, loc = refs[2 * n:]
        x, y, c = lax.axis_index("x"), lax.axis_index("y"), lax.axis_index("c")
        me = 2 * x + y
        pending = []
        for i in range(n):
            for r, (bx, by) in enumerate(_CHIP_RELATIONS):
                px, py = _flip(x, bx), _flip(y, by)
                cp = pltpu.make_async_remote_copy(src_ref=ins[i].at[2 * px + py], dst_ref=outs[i].at[me],
                                                  send_sem=send.at[i, r], recv_sem=recv.at[i, r],
                                                  device_id=(px, py, c), device_id_type=MESH)
                cp.start()
                pending.append(cp)
        for i in range(n):
            lc = pltpu.make_async_copy(ins[i].at[me], outs[i].at[me], loc.at[i])
            lc.start()
            pending.append(lc)
        for cp in pending:
            cp.wait()

    return pl.pallas_call(
        body, in_specs=[ANY] * n, out_specs=[ANY] * n,
        out_shape=[jax.ShapeDtypeStruct(a.shape, a.dtype) for a in arrs],
        scratch_shapes=[pltpu.SemaphoreType.DMA((n, 3)), pltpu.SemaphoreType.DMA((n, 3)), pltpu.SemaphoreType.DMA((n,))],
        name=name)(*arrs)


EXCHANGE_PIECES = 8


def sibling_exchange(arrs, name):
    n = len(arrs)

    def body(*refs):
        bufs = refs[n:2 * n]
        send, recv = refs[2 * n:]
        x, y, c = lax.axis_index("x"), lax.axis_index("y"), lax.axis_index("c")
        pending = []
        for i in range(n):
            rows = bufs[i].shape[1] // EXCHANGE_PIECES
            for j in range(EXCHANGE_PIECES):
                piece = bufs[i].at[c, pl.ds(j * rows, rows)]
                cp = pltpu.make_async_remote_copy(src_ref=piece, dst_ref=piece, send_sem=send.at[i, j],
                                                  recv_sem=recv.at[i, j], device_id=(x, y, 1 - c), device_id_type=MESH)
                cp.start()
                pending.append(cp)
        for i in range(n):
            rows = bufs[i].shape[1] // EXCHANGE_PIECES
            for j in range(EXCHANGE_PIECES):
                landed = bufs[i].at[1 - c, pl.ds(j * rows, rows)]
                pltpu.make_async_remote_copy(src_ref=landed, dst_ref=landed, send_sem=send.at[i, j], recv_sem=recv.at[i, j],
                                             device_id=(x, y, 1 - c), device_id_type=MESH).wait_recv()
        for cp in pending:
            cp.wait_send()

    return pl.pallas_call(
        body, in_specs=[ANY] * n, out_specs=[ANY] * n,
        out_shape=[jax.ShapeDtypeStruct(a.shape, a.dtype) for a in arrs], input_output_aliases={i: i for i in range(n)},
        scratch_shapes=[pltpu.SemaphoreType.DMA((n, EXCHANGE_PIECES)), pltpu.SemaphoreType.DMA((n, EXCHANGE_PIECES))],
        name=name)(*arrs)


def rwkv_fwd(l, seg, taps, pars, vfirst):
    fs = fir_fwd(seg, taps, f"rwkv_shift_fwd{l}")
    rows = [fs] + ([vfirst] if l else [])
    R, W, K, V, KK, A, G = tl_fwd(rwkv_pre(l), f"rwkv_pre_fwd{l}", rows, pars["pre"],
                                  [HM64, HM64, HM64, D_GROUP, HM64, HM64, D_GROUP])
    vT = _to_headsT(V)
    yT, sall = scan_fwd(R, W, K, vT, KK, A, f"rwkv_scan_fwd{l}")
    Y = _from_headsT(yT)
    (out,) = tl_fwd(rwkv_post, f"rwkv_post_fwd{l}", [Y, R, K, V, G], pars["post"], [D_GROUP])
    return out, V, (seg, taps, rows, R, W, K, V, KK, A, G, Y, vT, sall)


def rwkv_bwd(l, saved, pars, dout, dv_extra):
    seg, taps, rows, R, W, K, V, KK, A, G, Y, vT, sall = saved
    (dY, dR1, dK1, dV1, dG), dpost = tl_bwd(rwkv_post, f"rwkv_post_bwd{l}", [Y, R, K, V, G], pars["post"], [[dout]])
    (dR2, dW, dK2, dKK, dA), dvT = scan_bwd(R, W, K, vT, KK, A, sall, _to_headsT(dY), f"rwkv_scan_bwd{l}")
    cts = [[dR1, dR2], [dW], [dK1, dK2], [dV1, _from_headsT(dvT)] + dv_extra, [dKK], [dA], [dG]]
    drows, dpre = tl_bwd(rwkv_pre(l), f"rwkv_pre_bwd{l}", rows, pars["pre"], cts)
    dseg, dtaps, _ = fir_bwd(seg, taps, [drows[0]], f"rwkv_shift_bwd{l}")
    return dseg, (drows[1] if l else None), dtaps, dpre, dpost


def attn_mix_fwd(l, proj):
    os_, ls_ = [], []
    for b, (_, d) in enumerate(DILATED_BRANCHES):
        o, lse = attn_fwd(proj, d, f"attn_fwd{l}_{b}")
        os_.append(o)
        ls_.append(lse)
    (out,) = tl_fwd(attn_combine, f"attn_combine_fwd{l}", os_ + ls_, [], [D_GROUP])
    return out, (proj, os_, ls_)


def attn_mix_bwd(l, saved, dout):
    proj, os_, ls_ = saved
    drows, _ = tl_bwd(attn_combine, f"attn_combine_bwd{l}", os_ + ls_, [], [[dout]])
    grads = [attn_bwd(proj, drows[b], drows[3 + b], d, f"attn_bwd{l}_{b}") for b, (_, d) in enumerate(DILATED_BRANCHES)]
    return tuple(add_rows([g[j] for g in grads], f"attn_d{'qkv'[j]}{l}") for j in range(3))


def ssd_fwd(l, z, xbc, dtr, pars):
    xc = fir_fwd(xbc, pars["taps"], f"ssd_conv_fwd{l}")
    rr, w, kk, xdt, xs = tl_fwd(ssd_pre, f"ssd_pre_fwd{l}", [xc, dtr], pars["pre"], [HM128, HM128, HM128, HM64, D_GROUP])
    blocks = [rr, kk, w, xdt]
    ys, states = chunk_fwd(ssd_chunk, f"ssd_scan_fwd{l}", blocks, (HEAD_DIM, SSD_STATE), HEAD_DIM, SSD_GROUP)
    (out,) = tl_fwd(ssd_post, f"ssd_post_fwd{l}", [ys, z, xs], pars["post"], [D_GROUP])
    return out, (z, xbc, dtr, xc, blocks, states, xs, ys)


def ssd_bwd(l, saved, pars, dout):
    z, xbc, dtr, xc, blocks, states, xs, ys = saved
    (dys, dz, dxs), dpost = tl_bwd(ssd_post, f"ssd_post_bwd{l}", [ys, z, xs], pars["post"], [[dout]])
    drr, dkk, dw, dxdt = chunk_bwd(ssd_chunk, f"ssd_scan_bwd{l}", blocks, states, dys, SSD_GROUP)
    (dxc, ddtr), dpre = tl_bwd(ssd_pre, f"ssd_pre_bwd{l}", [xc, dtr], pars["pre"], [[drr], [dw], [dkk], [dxdt], [dxs]])
    dxbc, dtaps, _ = fir_bwd(xbc, pars["taps"], [dxc], f"ssd_conv_bwd{l}")
    return dz, dxbc, ddtr, dtaps, dpre, dpost


def hgrn_fwd(l, seg, pars):
    blocks = tl_fwd(hgrn_pre, f"hgrn_pre_fwd{l}", [seg], pars["pre"], [HM64] * 4)
    o, states = chunk_fwd(hgrn_chunk, f"hgrn_scan_fwd{l}", blocks, (HEAD_DIM, HEAD_DIM), HEAD_DIM, HGRN_GROUP)
    (out,) = tl_fwd(hgrn_post, f"hgrn_post_fwd{l}", [o, seg], pars["post"], [D_GROUP])
    return out, (seg, blocks, states, o)


def hgrn_bwd(l, saved, pars, dout):
    seg, blocks, states, o = saved
    (do, dseg1), dpost = tl_bwd(hgrn_post, f"hgrn_post_bwd{l}", [o, seg], pars["post"], [[dout]])
    dq, dkk, dlf, di = chunk_bwd(hgrn_chunk, f"hgrn_scan_bwd{l}", blocks, states, do, HGRN_GROUP)
    (dseg2,), dpre = tl_bwd(hgrn_pre, f"hgrn_pre_bwd{l}", [seg], pars["pre"], [[dq], [dkk], [dlf], [di]])
    return add_rows([dseg1, dseg2], f"hgrn_dseg{l}"), dpre, dpost


def layer_fwd(l, x, wts, pars, vfirst):
    proj = matmul(x, wts["in"], "nn", f"proj_fwd{l}")
    seg_h, seg_r = Cols(proj, SEG_HGRN, 1024), Cols(proj, SEG_RWKV, 1024)
    z, xbc, dtr = Cols(proj, SEG_Z, D_GROUP), Cols(proj, SEG_XBC, SSD_XBC), Cols(proj, SEG_DT, 128)
    ya, v_rwkv, sa = rwkv_fwd(l, seg_r, pars["rwkv"]["taps"], pars["rwkv"], vfirst)
    yb, sb = attn_mix_fwd(l, proj)
    yc, sc = ssd_fwd(l, z, xbc, dtr, pars["ssd"])
    yd, sd = hgrn_fwd(l, seg_h, pars["hgrn"])
    mix = jnp.concatenate([ya, yb, yc, yd], axis=1)
    mo = matmul(mix, wts["out"], "nn", f"out_fwd{l}")
    (x1,) = tl_fwd(ln_res, f"ln1_fwd{l}", [x, mo], pars["ln1"], [D_MODEL])
    u = matmul(x1, wts["up"], "nn", f"up_fwd{l}")
    dn = matmul(u, wts["down"], "nn", f"down_fwd{l}", a_relu2=True)
    (x2,) = tl_fwd(ln_res, f"ln2_fwd{l}", [x1, dn], pars["ln2"], [D_MODEL])
    return x2, v_rwkv, (x, sa, sb, sc, sd, mix, mo, x1, u, dn)


def layer_bwd(l, saved, wts, pars, dx2, dv_extra):
    x, sa, sb, sc, sd, mix, mo, x1, u, dn = saved
    S = x.shape[0]
    g = {}
    (dx1a, ddn), g["ln2"] = tl_bwd(ln_res, f"ln2_bwd{l}", [x1, dn], pars["ln2"], [[dx2]])
    g["down"] = matmul(u, ddn, "tn", f"down_dw{l}", a_relu2=True)
    du = matmul(ddn, wts["down"], "nt", f"down_dx{l}", relu2_grad_of=u)
    g["up"] = matmul(x1, du, "tn", f"up_dw{l}")
    dx1 = matmul(du, wts["up"], "nt", f"up_dx{l}", add=dx1a)
    (dxa, dmo), g["ln1"] = tl_bwd(ln_res, f"ln1_bwd{l}", [x, mo], pars["ln1"], [[dx1]])
    g["out"] = matmul(mix, dmo, "tn", f"out_dw{l}")
    dmix = matmul(dmo, wts["out"], "nt", f"out_dx{l}")
    dya, dyb, dyc, dyd = (dmix[:, j * D_GROUP:(j + 1) * D_GROUP] for j in range(4))
    dseg_r, dvfirst, g["rwkv_taps"], g["rwkv_pre"], g["rwkv_post"] = rwkv_bwd(l, sa, pars["rwkv"], dya, dv_extra)
    dq, dk, dv = attn_mix_bwd(l, sb, dyb)
    dz, dxbc, ddtr, g["ssd_taps"], g["ssd_pre"], g["ssd_post"] = ssd_bwd(l, sc, pars["ssd"], dyc)
    dseg_h, g["hgrn_pre"], g["hgrn_post"] = hgrn_bwd(l, sd, pars["hgrn"], dyd)
    dproj = jnp.concatenate([dseg_h, dseg_r, dq, dk, dv, dz, dxbc, ddtr, jnp.zeros((S, PROJ_W - SEG_DT - 128), F32)], axis=1)
    g["in"] = matmul(x, dproj, "tn", f"proj_dw{l}")
    dx = matmul(dproj, wts["in"], "nt", f"proj_dx{l}", add=dxa)
    return dx, dvfirst, g


SMALL = ("lower_bounds", "w_in_vres", "mu_shift", "mu_vres", "rwkv_w0", "rwkv_w2", "rwkv_a0", "rwkv_a2", "rwkv_g2",
         "rwkv_k_k", "rwkv_k_a", "rwkv_r_k", "rwkv_lnx_w", "rwkv_lnx_b", "rwkv_v0", "rwkv_v2", "ssd_conv_w",
         "ssd_conv_b", "ssd_dt_bias", "ssd_A_log", "ssd_D", "ssd_norm_w", "hgrn_norm_w", "ln1_w", "ln1_b", "ln2_w", "ln2_b")
BIG = ("w_in", "w_out", "w_up", "w_down")
SMALL_SHARDED = {"w_in_vres": 1, "rwkv_w2": 2, "rwkv_a2": 2, "rwkv_g2": 2, "rwkv_v2": 2, "ssd_conv_w": 2}
WEIGHTS = ("lower_bounds", "w_in", "w_in_vres", "mu_shift", "mu_vres", "rwkv_w0", "rwkv_w2", "rwkv_a0", "rwkv_a2",
           "rwkv_g2", "rwkv_k_k", "rwkv_k_a", "rwkv_r_k", "rwkv_lnx_w", "rwkv_lnx_b", "rwkv_v0", "rwkv_v2",
           "ssd_conv_w", "ssd_conv_b", "ssd_dt_bias", "ssd_A_log", "ssd_D", "ssd_norm_w", "hgrn_norm_w", "w_out",
           "ln1_w", "ln1_b", "w_up", "w_down", "ln2_w", "ln2_b")


def _row(v, width=None):
    v = v.reshape(1, -1).astype(F32)
    if width is not None and v.shape[1] < width:
        v = jnp.pad(v, ((0, 0), (0, width - v.shape[1])))
    return v


def _rows_at(m, rows, at):
    return jnp.pad(m.astype(F32), ((at, rows - at - m.shape[0]), (0, 0)))


def _pad_w_in(w_in_l, vres):
    rows = w_in_l.shape[0]
    out = []
    order = sorted(_PIECES, key=lambda p: p[2])
    pos = 0
    for start, width, at in order:
        if at > pos:
            out.append(jnp.zeros((rows, at - pos), w_in_l.dtype))
        out.append(w_in_l[:, start:start + width])
        pos = at + width
        if at == SEG_RWKV and vres is not None:
            out.append(vres.astype(w_in_l.dtype))
            pos += vres.shape[1]
    out.append(jnp.zeros((rows, PROJ_W - pos), w_in_l.dtype))
    return jnp.concatenate(out, axis=1)


def _unpad_w_in(g):
    return jnp.concatenate([g[:, at:at + width] for _, width, at in _PIECES], axis=1)


def layer_params(l, sp, prep):
    lb, mu, om = prep[l], prep[2 + 2 * l], prep[3 + 2 * l]
    pre = [_row(sp["rwkv_w0"][l]), _rows_at(sp["rwkv_w2"][l], 128, 0), _row(sp["rwkv_a0"][l]),
           _rows_at(sp["rwkv_a2"][l], 128, 32), _rows_at(sp["rwkv_g2"][l], 128, 64),
           _row(sp["rwkv_k_k"][l]), _row(sp["rwkv_k_a"][l])]
    if l:
        pre += [_row(sp["rwkv_v0"][l - 1]), _rows_at(sp["rwkv_v2"][l - 1], 128, 0)]
    return {
        "rwkv": {"taps": jnp.concatenate([mu, om], axis=0), "pre": pre,
                 "post": [_row(sp["rwkv_lnx_w"][l]), _row(sp["rwkv_lnx_b"][l]), _row(sp["rwkv_r_k"][l])]},
        "ssd": {"taps": sp["ssd_conv_w"][l].astype(F32),
                "pre": [_row(sp["ssd_conv_b"][l]), _row(sp["ssd_dt_bias"][l], 128), _row(sp["ssd_A_log"][l], 128)],
                "post": [_row(sp["ssd_D"][l], 128), _row(sp["ssd_norm_w"][l])]},
        "hgrn": {"pre": [lb], "post": [_row(sp["hgrn_norm_w"][l])]},
        "ln1": [_row(sp["ln1_w"][l]), _row(sp["ln1_b"][l])],
        "ln2": [_row(sp["ln2_w"][l]), _row(sp["ln2_b"][l])],
    }


def _mu_full(sp, l):
    parts = [sp["mu_shift"][l].reshape(1, -1)]
    if l:
        parts.append(sp["mu_vres"][l - 1].reshape(1, -1))
    return _row(jnp.concatenate(parts, axis=1), 1024)


def local_step(x, target, big, sp):
    prep_in = [sp["lower_bounds"].astype(F32), _mu_full(sp, 0), _mu_full(sp, 1)]
    prep = small_fwd(param_prep, "param_prep_fwd", prep_in,
                     [(1, D_GROUP), (1, D_GROUP), (1, 1024), (1, 1024), (1, 1024), (1, 1024)])
    pars, wts = [], []
    for l in range(DEPTH):
        pars.append(layer_params(l, sp, prep))
        vres = sp["w_in_vres"][l - 1].astype(BF16) if l else None
        wts.append({"in": _pad_w_in(big["w_in"][l], vres), "out": big["w_out"][l], "up": big["w_up"][l],
                    "down": big["w_down"][l]})
    h, vfirst, saved = x, None, []
    for l in range(DEPTH):
        h, v_l, sv = layer_fwd(l, h, wts[l], pars[l], vfirst)
        vfirst = v_l if l == 0 else vfirst
        saved.append(sv)
    loss_row, dh = loss_and_grad(h, target, "loss")
    grads, dv_extra = [None] * DEPTH, []
    for l in reversed(range(DEPTH)):
        dh, dvfirst, grads[l] = layer_bwd(l, saved[l], wts[l], pars[l], dh, dv_extra)
        dv_extra = [dvfirst] if l else []
    cts = [grads[0]["hgrn_pre"][0], grads[1]["hgrn_pre"][0]]
    for l in range(DEPTH):
        cts += [grads[l]["rwkv_taps"][0:1], grads[l]["rwkv_taps"][1:2]]
    d_lower, d_mu0, d_mu1 = small_bwd(param_prep, "param_prep_bwd", prep_in, cts)
    d_mu = [d_mu0, d_mu1]
    gb = {"w_in": [_unpad_w_in(grads[l]["in"]) for l in range(DEPTH)], "w_out": [grads[l]["out"] for l in range(DEPTH)],
          "w_up": [grads[l]["up"] for l in range(DEPTH)], "w_down": [grads[l]["down"] for l in range(DEPTH)]}
    st = lambda f: jnp.stack([f(l) for l in range(DEPTH)])
    g1 = grads[1]
    gs = {
        "lower_bounds": d_lower,
        "w_in_vres": g1["in"][None, :, VRES_COL:VRES_COL + 32],
        "mu_shift": st(lambda l: d_mu[l][0, :896]),
        "mu_vres": d_mu[1][:, 896:928],
        "rwkv_w0": st(lambda l: grads[l]["rwkv_pre"][0][0]),
        "rwkv_w2": st(lambda l: grads[l]["rwkv_pre"][1][0:32]),
        "rwkv_a0": st(lambda l: grads[l]["rwkv_pre"][2][0]),
        "rwkv_a2": st(lambda l: grads[l]["rwkv_pre"][3][32:64]),
        "rwkv_g2": st(lambda l: grads[l]["rwkv_pre"][4][64:128]),
        "rwkv_k_k": st(lambda l: grads[l]["rwkv_pre"][5][0]),
        "rwkv_k_a": st(lambda l: grads[l]["rwkv_pre"][6][0]),
        "rwkv_r_k": st(lambda l: grads[l]["rwkv_post"][2].reshape(N_HEADS, HEAD_DIM)),
        "rwkv_lnx_w": st(lambda l: grads[l]["rwkv_post"][0][0]),
        "rwkv_lnx_b": st(lambda l: grads[l]["rwkv_post"][1][0]),
        "rwkv_v0": g1["rwkv_pre"][7],
        "rwkv_v2": g1["rwkv_pre"][8][None, 0:32],
        "ssd_conv_w": st(lambda l: grads[l]["ssd_taps"]),
        "ssd_conv_b": st(lambda l: grads[l]["ssd_pre"][0][0]),
        "ssd_dt_bias": st(lambda l: grads[l]["ssd_pre"][1][0, :N_HEADS]),
        "ssd_A_log": st(lambda l: grads[l]["ssd_pre"][2][0, :N_HEADS]),
        "ssd_D": st(lambda l: grads[l]["ssd_post"][0][0, :N_HEADS]),
        "ssd_norm_w": st(lambda l: grads[l]["ssd_post"][1][0]),
        "hgrn_norm_w": st(lambda l: grads[l]["hgrn_post"][0][0]),
        "ln1_w": st(lambda l: grads[l]["ln1"][0][0]),
        "ln1_b": st(lambda l: grads[l]["ln1"][1][0]),
        "ln2_w": st(lambda l: grads[l]["ln2"][0][0]),
        "ln2_b": st(lambda l: grads[l]["ln2"][1][0]),
    }
    return loss_row, dh, gb, gs


def _pack(vecs):
    flat, meta, pos = [], [], 0
    for v in vecs:
        flat.append(v.reshape(-1).astype(F32))
        meta.append((pos, v.shape))
        pos += v.size
    total = -(-pos // 1024) * 1024
    flat.append(jnp.zeros((total - pos,), F32))
    return jnp.concatenate(flat).reshape(total // 128, 128), meta


def _unpack(packed, meta):
    flat = packed.reshape(-1)
    return [flat[off:off + math.prod(shape)].reshape(shape) for off, shape in meta]


def _to_shards(name, g):
    if name == "w_in":
        return jnp.transpose(g.reshape(g.shape[0], 4, g.shape[1] // 4), (1, 0, 2))
    if name == "w_up":
        return jnp.transpose(g.reshape(g.shape[0], 4, g.shape[1] // 4), (1, 0, 2))
    return g.reshape(4, g.shape[0] // 4, g.shape[1])


def _from_chips(name, g):
    if name in ("w_in", "w_up"):
        return jnp.transpose(g, (1, 2, 0, 3)).reshape(g.shape[1], g.shape[2], 4 * g.shape[3])
    return jnp.transpose(g, (1, 0, 2, 3)).reshape(g.shape[1], 4 * g.shape[2], g.shape[3])


INPUT_NAMES = ("x",) + WEIGHTS + ("loss_target",) + tuple("m_" + n for n in WEIGHTS) + tuple("v_" + n for n in WEIGHTS)


def _step(*args):
    a = dict(zip(INPUT_NAMES, args, strict=True))
    chip = 2 * lax.axis_index("x") + lax.axis_index("y")

    sharded_names = list(SMALL_SHARDED)
    small_pack, small_meta = _pack([a[n] for n in sharded_names])
    own = [a[n].astype(BF16) for n in BIG]
    gathered = gather_chips(own, small_pack, "gather_weights")
    here = lambda full, mine: lax.dynamic_update_slice(full, mine[None], (chip,) + (0,) * mine.ndim)
    big = {n: _from_chips(n, here(g, o)) for n, g, o in zip(BIG, gathered, own)}
    sp = {n: a[n] for n in SMALL if n not in SMALL_SHARDED}
    small_all = here(gathered[-1], small_pack)
    per_chip = [_unpack(small_all[s], small_meta) for s in range(4)]
    for j, n in enumerate(sharded_names):
        sp[n] = jnp.concatenate([per_chip[s][j] for s in range(4)], axis=SMALL_SHARDED[n])

    loss_row, gx, gb, gs = local_step(a["x"][0], a["loss_target"][0], big, sp)

    partials = [jnp.stack([_to_shards(n, gb[n][l]) for l in range(DEPTH)]) for n in BIG]
    got = pair_exchange(partials, "pair_exchange")
    ids = jnp.stack([lax.axis_index("c"), chip]).astype(jnp.int32)
    chip_sums = [pair_sum(p, q, ids, f"pair_sum_{n}") for n, p, q in zip(BIG, partials, got)]
    slots = reduce_chips(chip_sums, "reduce_big")
    mine = [sum_chips(sl, p, q, ids, f"sum_{n}") for n, sl, p, q in zip(BIG, slots, partials, got)]
    summed = sibling_exchange(mine, "exchange_big")
    out_g, out_d, out_m, out_v = {}, {}, {}, {}
    for n, g in zip(BIG, summed):
        shape = a[n].shape
        flat = lambda t: t.reshape(shape[0] * shape[1], shape[2])
        d, nm, nv = adamw(flat(a[n]), flat(g), flat(a["m_" + n]), flat(a["v_" + n]), f"adamw_{n}")
        out_g[n], out_d[n], out_m[n], out_v[n] = g.reshape(shape), d.reshape(shape), nm.reshape(shape), nv.reshape(shape)

    vec, meta = _pack([loss_row] + [gs[n] for n in SMALL])
    total = sum_parts(gather_devices(vec, "gather_small"), "sum_small")
    parts = _unpack(total, meta)
    loss = parts[0][0, 0]
    g_small = {}
    for n, g in zip(SMALL, parts[1:]):
        if n in SMALL_SHARDED:
            ax = SMALL_SHARDED[n]
            size = a[n].shape[ax]
            g = lax.dynamic_slice_in_dim(g, chip * size, size, axis=ax)
        g_small[n] = g
    pw, pmeta = _pack([a[n] for n in SMALL])
    pg, _ = _pack([g_small[n] for n in SMALL])
    pm, _ = _pack([a["m_" + n] for n in SMALL])
    pv, _ = _pack([a["v_" + n] for n in SMALL])
    d, nm, nv = adamw(pw, pg, pm, pv, "adamw_small")
    for n, dd, mm, vv in zip(SMALL, _unpack(d, pmeta), _unpack(nm, pmeta), _unpack(nv, pmeta)):
        out_g[n], out_d[n], out_m[n], out_v[n] = g_small[n], dd, mm, vv

    return (loss, gx[None], *[out_g[n] for n in WEIGHTS], *[out_d[n] for n in WEIGHTS],
            *[out_m[n] for n in WEIGHTS], *[out_v[n] for n in WEIGHTS])


def kernel(x, lower_bounds, w_in, w_in_vres, mu_shift, mu_vres, rwkv_w0, rwkv_w2, rwkv_a0, rwkv_a2, rwkv_g2, rwkv_k_k, rwkv_k_a, rwkv_r_k, rwkv_lnx_w, rwkv_lnx_b, rwkv_v0, rwkv_v2, ssd_conv_w, ssd_conv_b, ssd_dt_bias, ssd_A_log, ssd_D, ssd_norm_w, hgrn_norm_w, w_out, ln1_w, ln1_b, w_up, w_down, ln2_w, ln2_b, loss_target, m_lower_bounds, m_w_in, m_w_in_vres, m_mu_shift, m_mu_vres, m_rwkv_w0, m_rwkv_w2, m_rwkv_a0, m_rwkv_a2, m_rwkv_g2, m_rwkv_k_k, m_rwkv_k_a, m_rwkv_r_k, m_rwkv_lnx_w, m_rwkv_lnx_b, m_rwkv_v0, m_rwkv_v2, m_ssd_conv_w, m_ssd_conv_b, m_ssd_dt_bias, m_ssd_A_log, m_ssd_D, m_ssd_norm_w, m_hgrn_norm_w, m_w_out, m_ln1_w, m_ln1_b, m_w_up, m_w_down, m_ln2_w, m_ln2_b, v_lower_bounds, v_w_in, v_w_in_vres, v_mu_shift, v_mu_vres, v_rwkv_w0, v_rwkv_w2, v_rwkv_a0, v_rwkv_a2, v_rwkv_g2, v_rwkv_k_k, v_rwkv_k_a, v_rwkv_r_k, v_rwkv_lnx_w, v_rwkv_lnx_b, v_rwkv_v0, v_rwkv_v2, v_ssd_conv_w, v_ssd_conv_b, v_ssd_dt_bias, v_ssd_A_log, v_ssd_D, v_ssd_norm_w, v_hgrn_norm_w, v_w_out, v_ln1_w, v_ln1_b, v_w_up, v_w_down, v_ln2_w, v_ln2_b):
    return _step(x, lower_bounds, w_in, w_in_vres, mu_shift, mu_vres, rwkv_w0, rwkv_w2, rwkv_a0, rwkv_a2, rwkv_g2, rwkv_k_k, rwkv_k_a, rwkv_r_k, rwkv_lnx_w, rwkv_lnx_b, rwkv_v0, rwkv_v2, ssd_conv_w, ssd_conv_b, ssd_dt_bias, ssd_A_log, ssd_D, ssd_norm_w, hgrn_norm_w, w_out, ln1_w, ln1_b, w_up, w_down, ln2_w, ln2_b, loss_target, m_lower_bounds, m_w_in, m_w_in_vres, m_mu_shift, m_mu_vres, m_rwkv_w0, m_rwkv_w2, m_rwkv_a0, m_rwkv_a2, m_rwkv_g2, m_rwkv_k_k, m_rwkv_k_a, m_rwkv_r_k, m_rwkv_lnx_w, m_rwkv_lnx_b, m_rwkv_v0, m_rwkv_v2, m_ssd_conv_w, m_ssd_conv_b, m_ssd_dt_bias, m_ssd_A_log, m_ssd_D, m_ssd_norm_w, m_hgrn_norm_w, m_w_out, m_ln1_w, m_ln1_b, m_w_up, m_w_down, m_ln2_w, m_ln2_b, v_lower_bounds, v_w_in, v_w_in_vres, v_mu_shift, v_mu_vres, v_rwkv_w0, v_rwkv_w2, v_rwkv_a0, v_rwkv_a2, v_rwkv_g2, v_rwkv_k_k, v_rwkv_k_a, v_rwkv_r_k, v_rwkv_lnx_w, v_rwkv_lnx_b, v_rwkv_v0, v_rwkv_v2, v_ssd_conv_w, v_ssd_conv_b, v_ssd_dt_bias, v_ssd_A_log, v_ssd_D, v_ssd_norm_w, v_hgrn_norm_w, v_w_out, v_ln1_w, v_ln1_b, v_w_up, v_w_down, v_ln2_w, v_ln2_b)
```

```python
import functools
import math

import jax
import jax.numpy as jnp
from jax import lax
from jax.experimental import pallas as pl
from jax.experimental.pallas import tpu as pltpu

F32 = jnp.float32
BF16 = jnp.bfloat16
HI = lax.Precision.HIGHEST

DEPTH = 2
D_MODEL = 1024
D_GROUP = 256
HEAD_DIM = 64
N_HEADS = 4
SSD_STATE = 128
SSD_XBC = 768
SSD_CONV = 4
D_FF = 4096
ALPHA = (2.0 * DEPTH) ** 0.25
LN_EPS = 1e-5
RMS_EPS = 1e-5
RWKV_GN_EPS = HEAD_DIM * 1e-5
DILATED_BRANCHES = ((128, 1), (512, 4), (2048, 16))
ALIBI_SLOPES = tuple(2.0 ** (-8.0 * (h + 1) / N_HEADS) for h in range(N_HEADS))
ATTN_BLK = 128

ADAM_LR, ADAM_B1, ADAM_B2, ADAM_EPS, ADAM_WD, ADAM_STEP = 0.001, 0.9, 0.999, 1e-08, 0.01, 10

IN_COLS = 3716
PROJ_W = 4096
SEG_HGRN, SEG_RWKV, SEG_Q, SEG_Z, SEG_XBC, SEG_DT = 0, 1024, 2048, 2816, 3072, 3840
_PIECES = ((0, 896, SEG_RWKV), (896, 768, SEG_Q), (1664, 256, SEG_Z), (1920, 768, SEG_XBC),
           (2688, 4, SEG_DT), (2692, 1024, SEG_HGRN))
VRES_COL = SEG_RWKV + 896

HM64 = (N_HEADS, HEAD_DIM)
HM128 = (N_HEADS, SSD_STATE)
ROW_TILE = 256
SCAN_CHUNK = 128
VMEM_LIMIT = 48 * 1024 * 1024


def _cparams(sem=None):
    if sem is None:
        return pltpu.CompilerParams(vmem_limit_bytes=VMEM_LIMIT)
    return pltpu.CompilerParams(dimension_semantics=sem, vmem_limit_bytes=VMEM_LIMIT)


def _pick(n, pref):
    for t in pref:
        if n % t == 0:
            return t
    return n


def _relu2(u):
    r = jnp.maximum(u, 0.0)
    return r * r


def matmul(a, b, mode, name, add=None, a_relu2=False, relu2_grad_of=None):
    pieces = list(a) if isinstance(a, (list, tuple)) else [a]
    a_rows, a_cols = pieces[0].shape[0], sum(p.shape[1] for p in pieces)
    if mode == "nn":
        (M, K), N = (a_rows, a_cols), b.shape[1]
    elif mode == "nt":
        (M, K), N = (a_rows, a_cols), b.shape[0]
    else:
        (K, M), N = (a_rows, a_cols), b.shape[1]
    tm, tn, tk = _pick(M, (1024, 512, 256, 128)), _pick(N, (1024, 512, 256, 128)), _pick(K, (1024, 512, 256, 128))
    nk = K // tk
    dims = {"nn": (((1,), (0,)), ((), ())), "nt": (((1,), (1,)), ((), ())), "tn": (((0,), (0,)), ((), ()))}[mode]
    extras = [e for e in (add, relu2_grad_of) if e is not None]
    npieces = len(pieces)

    def body(*refs):
        b_ref = refs[npieces]
        o_ref = refs[-1]
        rest = list(refs[npieces + 1:-1])
        add_ref = rest.pop(0) if add is not None else None
        u_ref = rest.pop(0) if relu2_grad_of is not None else None
        k = pl.program_id(2)
        av = refs[0][...] if npieces == 1 else jnp.concatenate([r[...] for r in refs[:npieces]], axis=1)
        if a_relu2:
            av = _relu2(av)
        d = lax.dot_general(av.astype(BF16), b_ref[...].astype(BF16), dims, preferred_element_type=F32)

        @pl.when(k == 0)
        def _():
            o_ref[...] = d if add_ref is None else d + add_ref[...]

        if nk > 1:
            @pl.when(k > 0)
            def _():
                o_ref[...] += d

        if u_ref is not None:
            @pl.when(k == nk - 1)
            def _():
                o_ref[...] = o_ref[...] * (2.0 * jnp.maximum(u_ref[...], 0.0))

    if npieces == 1:
        a_specs = [pl.BlockSpec((tk, tm), lambda i, j, k: (k, i)) if mode == "tn" else pl.BlockSpec((tm, tk), lambda i, j, k: (i, k))]
    else:
        assert a_cols == (tm if mode == "tn" else tk)
        rows_of = (lambda i, j, k: (k, 0)) if mode == "tn" else (lambda i, j, k: (i, 0))
        a_specs = [pl.BlockSpec((tk if mode == "tn" else tm, p.shape[1]), rows_of) for p in pieces]
    b_spec = pl.BlockSpec((tn, tk), lambda i, j, k: (j, k)) if mode == "nt" else pl.BlockSpec((tk, tn), lambda i, j, k: (k, j))
    o_spec = pl.BlockSpec((tm, tn), lambda i, j, k: (i, j))
    ins, specs = pieces + [b] + extras, a_specs + [b_spec] + [o_spec] * len(extras)
    return pl.pallas_call(
        body, grid=(M // tm, N // tn, nk), in_specs=specs, out_specs=o_spec,
        out_shape=jax.ShapeDtypeStruct((M, N), F32),
        compiler_params=_cparams(("parallel", "parallel", "arbitrary")), name=name)(*ins)


def _row_spec(w, tile):
    return pl.BlockSpec((tile, w), lambda i: (i, 0))


def _par_spec(shape):
    return pl.BlockSpec(shape, lambda i: (0,) * len(shape))


class Cols:
    def __init__(self, arr, start, width):
        assert start % width == 0 or (start % 128 == 0 and width % 128 == 0)
        self.arr, self.start, self.width = arr, start, width
        self.shape, self.ndim = (arr.shape[0], width), 2


def _arr(r):
    return r.arr if isinstance(r, Cols) else r


def _rows_spec(a, tile):
    if isinstance(a, Cols):
        assert a.start % a.width == 0
        return pl.BlockSpec((tile, a.width), lambda i, blk=a.start // a.width: (i, blk))
    shape = a if isinstance(a, tuple) else a.shape
    if len(shape) == 3:
        return pl.BlockSpec((shape[0], tile, shape[2]), lambda i: (0, i, 0))
    return _row_spec(shape[1], tile)


def _rows_shape(S, w):
    return (w[0], S, w[1]) if isinstance(w, tuple) else (S, w)


def _rows_load(ref):
    if len(ref.shape) == 3:
        return jnp.concatenate([ref[h] for h in range(ref.shape[0])], axis=1)
    return ref[...]


def _rows_store(ref, val):
    if len(ref.shape) == 3:
        w = ref.shape[2]
        for h in range(ref.shape[0]):
            ref[h] = val[:, h * w:(h + 1) * w]
    else:
        ref[...] = val


def tl_fwd(fn, name, rows, pars, out_widths, tile=ROW_TILE):
    S = rows[0].shape[-2]
    nr = len(rows)

    def body(*refs):
        ins = [_rows_load(r) for r in refs[:nr]] + [r[...] for r in refs[nr:nr + len(pars)]]
        outs = fn(*ins)
        for o_ref, o in zip(refs[nr + len(pars):], outs):
            _rows_store(o_ref, o)

    shapes = [_rows_shape(S, w) for w in out_widths]
    return pl.pallas_call(
        body, grid=(S // tile,),
        in_specs=[_rows_spec(r, tile) for r in rows] + [_par_spec(p.shape) for p in pars],
        out_specs=[_rows_spec(s, tile) for s in shapes],
        out_shape=[jax.ShapeDtypeStruct(s, F32) for s in shapes],
        compiler_params=_cparams(("parallel",)), name=name)(*[_arr(r) for r in rows], *pars)


def tl_bwd(fn, name, rows, pars, cts, tile=ROW_TILE, row_grad=None):
    S = rows[0].shape[-2]
    nr, npar = len(rows), len(pars)
    row_grad = [True] * nr if row_grad is None else row_grad
    flat_cts = [c for group in cts for c in group]
    ncts = len(flat_cts)
    gi = [i for i in range(nr) if row_grad[i]]

    def body(*refs):
        row_v = [_rows_load(r) for r in refs[:nr]]
        par_v = [r[...] for r in refs[nr:nr + npar]]
        ct_refs = refs[nr + npar:nr + npar + ncts]
        out_refs = refs[nr + npar + ncts:]
        ct_v, pos = [], 0
        for group in cts:
            acc = _rows_load(ct_refs[pos])
            for q in range(1, len(group)):
                acc = acc + _rows_load(ct_refs[pos + q])
            pos += len(group)
            ct_v.append(acc)

        def f(diff_rows, par_vals):
            full = list(row_v)
            for idx, val in zip(gi, diff_rows):
                full[idx] = val
            return tuple(fn(*full, *par_vals))

        _, vjp = jax.vjp(f, [row_v[i] for i in gi], par_v)
        d_rows, d_pars = vjp(tuple(ct_v))
        for o_ref, g in zip(out_refs[:len(gi)], d_rows):
            _rows_store(o_ref, g)
        first = pl.program_id(0) == 0
        for o_ref, g in zip(out_refs[len(gi):], d_pars):
            @pl.when(first)
            def _(o_ref=o_ref):
                o_ref[...] = jnp.zeros_like(o_ref)
            o_ref[...] += g

    outs = pl.pallas_call(
        body, grid=(S // tile,),
        in_specs=[_rows_spec(r, tile) for r in rows] + [_par_spec(p.shape) for p in pars]
        + [_rows_spec(c, tile) for c in flat_cts],
        out_specs=[_rows_spec(rows[i].shape, tile) for i in gi] + [_par_spec(p.shape) for p in pars],
        out_shape=[jax.ShapeDtypeStruct(rows[i].shape, F32) for i in gi] + [jax.ShapeDtypeStruct(p.shape, F32) for p in pars],
        compiler_params=_cparams(("arbitrary",)), name=name)(*[_arr(r) for r in rows], *pars, *[_arr(c) for c in flat_cts])
    return list(outs[:len(gi)]), list(outs[len(gi):])


def _shift_rows(x, j):
    if j == 0:
        return x
    rolled = pltpu.roll(x, j, 0)
    row = lax.broadcasted_iota(jnp.int32, x.shape, 0)
    return jnp.where(row >= j, rolled, 0.0)


def _unshift_rows(x, j):
    if j == 0:
        return x
    S = x.shape[0]
    rolled = pltpu.roll(x, S - j, 0)
    row = lax.broadcasted_iota(jnp.int32, x.shape, 0)
    return jnp.where(row < S - j, rolled, 0.0)


def _fir_in_spec(x):
    first = x.start // 128 if isinstance(x, Cols) else 0
    return pl.BlockSpec((x.shape[0], 128), lambda j: (0, first + j))


def fir_fwd(x, taps, name):
    S, C = x.shape
    K = taps.shape[0]

    def body(x_ref, w_ref, y_ref):
        xv = x_ref[...]
        acc = jnp.zeros_like(xv)
        for k in range(K):
            acc = acc + _shift_rows(xv, K - 1 - k) * w_ref[pl.ds(k, 1), :]
        y_ref[...] = acc

    cs = pl.BlockSpec((S, 128), lambda j: (0, j))
    return pl.pallas_call(body, grid=(C // 128,), in_specs=[_fir_in_spec(x), pl.BlockSpec((K, 128), lambda j: (0, j))],
                          out_specs=cs, out_shape=jax.ShapeDtypeStruct((S, C), F32),
                          compiler_params=_cparams(("parallel",)), name=name)(_arr(x), taps)


def fir_bwd(x, taps, dy_list, name):
    S, C = x.shape
    K = taps.shape[0]
    n = len(dy_list)

    def body(*refs):
        x_ref, w_ref = refs[:2]
        dy = refs[2][...]
        for q in range(1, n):
            dy = dy + refs[2 + q][...]
        dx_ref, dw_ref, db_ref = refs[2 + n:]
        xv = x_ref[...]
        dx = jnp.zeros_like(xv)
        for k in range(K):
            j = K - 1 - k
            dx = dx + _unshift_rows(dy, j) * w_ref[pl.ds(k, 1), :]
            dw_ref[pl.ds(k, 1), :] = jnp.sum(dy * _shift_rows(xv, j), axis=0, keepdims=True)
        dx_ref[...] = dx
        db_ref[...] = jnp.sum(dy, axis=0, keepdims=True)

    cs = pl.BlockSpec((S, 128), lambda j: (0, j))
    ks = pl.BlockSpec((K, 128), lambda j: (0, j))
    bs = pl.BlockSpec((1, 128), lambda j: (0, j))
    return pl.pallas_call(body, grid=(C // 128,), in_specs=[_fir_in_spec(x), ks] + [cs] * n, out_specs=[cs, ks, bs],
                          out_shape=[jax.ShapeDtypeStruct((S, C), F32), jax.ShapeDtypeStruct((K, C), F32),
                                     jax.ShapeDtypeStruct((1, C), F32)],
                          compiler_params=_cparams(("parallel",)), name=name)(_arr(x), taps, *dy_list)


def _col(tile, lane, t):
    return jnp.sum(jnp.where(lane == t, tile, 0.0), axis=1, keepdims=True)


def _rwkv_step(s, rv, vcol):
    sa = jnp.sum(s * (-rv[3]), axis=1, keepdims=True)
    return s * rv[1] + sa * (rv[3] * rv[4]) + vcol * rv[2], sa


def _eye(n):
    return (lax.broadcasted_iota(jnp.int32, (n, n), 0) == lax.broadcasted_iota(jnp.int32, (n, n), 1)).astype(F32)


def _transposed(x):
    return lax.dot_general(_eye(x.shape[1]), x, (((1,), (1,)), ((), ())), precision=HI, preferred_element_type=F32)


def scan_fwd(r, w, k, v, kk, a, name):
    H, S, Dk = r.shape
    Dv = v.shape[1] // H
    Tc = SCAN_CHUNK
    nc = S // Tc
    rows = [r, w, k, kk, a]

    def body(*refs):
        row_refs = refs[:5]
        v_ref, y_ref, sall_ref, s_ref, vT_ref, yT_ref = refs[5:]

        @pl.when(pl.program_id(0) == 0)
        def _():
            s_ref[...] = jnp.zeros_like(s_ref)

        for h in range(H):
            vT_ref[h] = _transposed(v_ref[:, h * Dv:(h + 1) * Dv])
        yT_ref[...] = jnp.zeros_like(yT_ref)
        lane = lax.broadcasted_iota(jnp.int32, (Dv, Tc), 1)

        def step(t, states):
            new = []
            for h in range(H):
                s = states[h]
                sall_ref[t, h] = s
                rv = [ref[h, pl.ds(t, 1), :] for ref in row_refs]
                s, _ = _rwkv_step(s, rv, _col(vT_ref[h], lane, t))
                ycol = jnp.sum(s * rv[0], axis=1, keepdims=True)
                yT_ref[h] = jnp.where(lane == t, ycol, yT_ref[h])
                new.append(s)
            return tuple(new)

        states = lax.fori_loop(0, Tc, step, tuple(s_ref[h] for h in range(H)))
        for h in range(H):
            s_ref[h] = states[h]
            y_ref[:, h * Dv:(h + 1) * Dv] = _transposed(yT_ref[h])

    rs = pl.BlockSpec((H, Tc, Dk), lambda c: (0, c, 0))
    vs = pl.BlockSpec((Tc, H * Dv), lambda c: (c, 0))
    return pl.pallas_call(
        body, grid=(nc,), in_specs=[rs] * 5 + [vs],
        out_specs=[vs, pl.BlockSpec((Tc, H, Dv, Dk), lambda c: (c, 0, 0, 0))],
        out_shape=[jax.ShapeDtypeStruct((S, H * Dv), F32), jax.ShapeDtypeStruct((S, H, Dv, Dk), F32)],
        scratch_shapes=[pltpu.VMEM((H, Dv, Dk), F32), pltpu.VMEM((H, Dv, Tc), F32), pltpu.VMEM((H, Dv, Tc), F32)],
        compiler_params=_cparams(("arbitrary",)), name=name)(*rows, v)


def scan_bwd(r, w, k, v, kk, a, sall, dy, name):
    H, S, Dk = r.shape
    Dv = v.shape[1] // H
    Tc = SCAN_CHUNK
    nc = S // Tc
    rows = [r, w, k, kk, a]

    def body(*refs):
        row_refs = refs[:5]
        v_ref, dy_ref, sall_ref = refs[5:8]
        drow_refs = refs[8:13]
        dv_ref, ds_ref, vT_ref, dyT_ref, dvT_ref = refs[13:]

        @pl.when(pl.program_id(0) == 0)
        def _():
            ds_ref[...] = jnp.zeros_like(ds_ref)

        for h in range(H):
            vT_ref[h] = _transposed(v_ref[:, h * Dv:(h + 1) * Dv])
            dyT_ref[h] = _transposed(dy_ref[:, h * Dv:(h + 1) * Dv])
        dvT_ref[...] = jnp.zeros_like(dvT_ref)
        lane = lax.broadcasted_iota(jnp.int32, (Dv, Tc), 1)

        def bstep(i, carry):
            t = Tc - 1 - i
            new = []
            for h in range(H):
                ds = carry[h]
                sp = sall_ref[t, h]
                rv = [ref[h, pl.ds(t, 1), :] for ref in row_refs]
                vcol = _col(vT_ref[h], lane, t)
                dycol = _col(dyT_ref[h], lane, t)
                st, sa = _rwkv_step(sp, rv, vcol)
                drow_refs[0][h, pl.ds(t, 1), :] = jnp.sum(st * dycol, axis=0, keepdims=True)
                g = ds + dycol * rv[0]
                drow_refs[1][h, pl.ds(t, 1), :] = jnp.sum(g * sp, axis=0, keepdims=True)
                drow_refs[2][h, pl.ds(t, 1), :] = jnp.sum(g * vcol, axis=0, keepdims=True)
                dvcol = jnp.sum(g * rv[2], axis=1, keepdims=True)
                dsa = jnp.sum(g * (rv[3] * rv[4]), axis=1, keepdims=True)
                db = jnp.sum(g * sa, axis=0, keepdims=True)
                dnkk = jnp.sum(sp * dsa, axis=0, keepdims=True)
                drow_refs[3][h, pl.ds(t, 1), :] = db * rv[4] - dnkk
                drow_refs[4][h, pl.ds(t, 1), :] = db * rv[3]
                dvT_ref[h] = jnp.where(lane == t, dvcol, dvT_ref[h])
                new.append(g * rv[1] - dsa * rv[3])
            return tuple(new)

        carry = lax.fori_loop(0, Tc, bstep, tuple(ds_ref[h] for h in range(H)))
        for h in range(H):
            ds_ref[h] = carry[h]
            dv_ref[:, h * Dv:(h + 1) * Dv] = _transposed(dvT_ref[h])

    rs = pl.BlockSpec((H, Tc, Dk), lambda c: (0, nc - 1 - c, 0))
    vs = pl.BlockSpec((Tc, H * Dv), lambda c: (nc - 1 - c, 0))
    tile = pltpu.VMEM((H, Dv, Tc), F32)
    outs = pl.pallas_call(
        body, grid=(nc,),
        in_specs=[rs] * 5 + [vs, vs, pl.BlockSpec((Tc, H, Dv, Dk), lambda c: (nc - 1 - c, 0, 0, 0))],
        out_specs=[rs] * 5 + [vs],
        out_shape=[jax.ShapeDtypeStruct((H, S, Dk), F32)] * 5 + [jax.ShapeDtypeStruct((S, H * Dv), F32)],
        scratch_shapes=[pltpu.VMEM((H, Dv, Dk), F32), tile, tile, tile],
        compiler_params=_cparams(("arbitrary",)), name=name)(*rows, v, dy, sall)
    return list(outs[:5]), outs[5]


CHUNK = 128
SSD_GROUP = 2
HGRN_GROUP = 4


def chunk_fwd(fn, name, blocks, state_shape, out_width, group):
    H, S, _ = blocks[0].shape
    nc = S // CHUNK
    nb = len(blocks)

    def body(*refs):
        o_ref, sv_ref, st = refs[nb:]

        @pl.when(pl.program_id(1) == 0)
        def _():
            st[...] = jnp.zeros_like(st)

        for g in range(group):
            s0 = st[g]
            sv_ref[g, 0] = s0
            s1, out = fn(s0, *[r[g] for r in refs[:nb]])
            st[g] = s1
            o_ref[g] = out

    spec = lambda w: pl.BlockSpec((group, CHUNK, w), lambda h, c: (h, c, 0))
    return pl.pallas_call(
        body, grid=(H // group, nc), in_specs=[spec(b.shape[2]) for b in blocks],
        out_specs=[spec(out_width), pl.BlockSpec((group, 1) + state_shape, lambda h, c: (h, c, 0, 0))],
        out_shape=[jax.ShapeDtypeStruct((H, S, out_width), F32), jax.ShapeDtypeStruct((H, nc) + state_shape, F32)],
        scratch_shapes=[pltpu.VMEM((group,) + state_shape, F32)],
        compiler_params=_cparams(("parallel", "arbitrary")), name=name)(*blocks)


def chunk_bwd(fn, name, blocks, states, dout, group):
    H, S, _ = blocks[0].shape
    nc = S // CHUNK
    nb = len(blocks)
    state_shape = states.shape[2:]

    def body(*refs):
        sv_ref, do_ref = refs[nb], refs[nb + 1]
        d_refs = refs[nb + 2:2 * nb + 2]
        dst = refs[2 * nb + 2]

        @pl.when(pl.program_id(1) == 0)
        def _():
            dst[...] = jnp.zeros_like(dst)

        for g in range(group):
            _, vjp = jax.vjp(fn, sv_ref[g, 0], *[r[g] for r in refs[:nb]])
            grads = vjp((dst[g], do_ref[g]))
            dst[g] = grads[0]
            for d_ref, gr in zip(d_refs, grads[1:]):
                d_ref[g] = gr

    spec = lambda w: pl.BlockSpec((group, CHUNK, w), lambda h, c: (h, nc - 1 - c, 0))
    return pl.pallas_call(
        body, grid=(H // group, nc),
        in_specs=[spec(b.shape[2]) for b in blocks]
        + [pl.BlockSpec((group, 1) + state_shape, lambda h, c: (h, nc - 1 - c, 0, 0)), spec(dout.shape[2])],
        out_specs=[spec(b.shape[2]) for b in blocks],
        out_shape=[jax.ShapeDtypeStruct(b.shape, F32) for b in blocks],
        scratch_shapes=[pltpu.VMEM((group,) + state_shape, F32)],
        compiler_params=_cparams(("parallel", "arbitrary")), name=name)(*blocks, states, dout)


def _bdot(a, b, dims):
    return lax.dot_general(a.astype(BF16), b.astype(BF16), (dims, ((), ())), preferred_element_type=F32)


def ssd_chunk(state, cb, bb, da, xdt):
    T = cb.shape[0]
    ti = lax.broadcasted_iota(jnp.int32, (T, T), 0)
    si = lax.broadcasted_iota(jnp.int32, (T, T), 1)
    mask = ti >= si
    cs = jnp.dot(mask.astype(F32), da, precision=HI, preferred_element_type=F32)
    pick = (lax.broadcasted_iota(jnp.int32, cs.shape, 1) == 0).astype(F32)
    cs_row = lax.dot_general(pick, cs, (((1,), (1,)), ((), ())), precision=HI, preferred_element_type=F32)
    lmat = jnp.where(mask, jnp.exp(jnp.where(mask, cs - cs_row, 0.0)), 0.0)
    scores = _bdot(cb, bb, ((1,), (1,))) * lmat
    y = _bdot(scores, xdt, ((1,), (0,))) + _bdot(cb, state, ((1,), (1,))) * jnp.exp(cs[:, :HEAD_DIM])
    last = cs[T - 1:T, :]
    new_state = state * jnp.exp(last) + _bdot(xdt, bb * jnp.exp(last - cs), ((0,), (0,)))
    return new_state, y


HGRN_SUB = 16


def hgrn_chunk(state, q, k, lf, v):
    T, C = q.shape[0], HGRN_SUB
    ti = lax.broadcasted_iota(jnp.int32, (C, C), 0)
    si = lax.broadcasted_iota(jnp.int32, (C, C), 1)
    tril = (ti >= si).astype(F32)
    row = lax.broadcasted_iota(jnp.int32, (C, q.shape[1]), 0)
    outs = []
    for j in range(T // C):
        qj, kj, lj, vj = (a[j * C:(j + 1) * C] for a in (q, k, lf, v))
        b = jnp.dot(tril, lj, precision=HI, preferred_element_type=F32)
        o = _bdot(qj * jnp.exp(b), state, ((1,), (1,)))
        for s in range(C):
            m = row >= s
            e = jnp.where(m, jnp.exp(jnp.where(m, b - b[s:s + 1], 0.0)), 0.0)
            o = o + jnp.sum(qj * kj[s:s + 1] * e, axis=1, keepdims=True) * vj[s:s + 1]
        last = b[C - 1:C]
        state = state * jnp.exp(last) + _bdot(vj, kj * jnp.exp(last - b), ((0,), (0,)))
        outs.append(o)
    return state, jnp.concatenate(outs, axis=0)


def _attn_block(q, kp, kc, vp, vc, n, slope, dilation):
    blk = ATTN_BLK
    k2 = jnp.concatenate([kp, kc], axis=0)
    v2 = jnp.concatenate([vp, vc], axis=0)
    s = _bdot(q, k2, ((1,), (1,))) * (HEAD_DIM ** -0.5)
    i = lax.broadcasted_iota(jnp.int32, (blk, 2 * blk), 0)
    j = lax.broadcasted_iota(jnp.int32, (blk, 2 * blk), 1)
    dist = blk + i - j
    first_key = jnp.where(n > 0, 0, blk)
    valid = (dist >= 0) & (dist <= blk) & (j >= first_key)
    s = s - slope * (dist * dilation).astype(F32)
    s = jnp.where(valid, s, -1e30)
    m = jnp.max(s, axis=-1, keepdims=True)
    p = jnp.exp(s - m)
    l = jnp.sum(p, axis=-1, keepdims=True)
    o = _bdot(p, v2, ((1,), (0,))) / l
    lse = jnp.broadcast_to(m + jnp.log(l), o.shape)
    return o, lse


_QCOL = SEG_Q // 128
PAIR = 2 * HEAD_DIM


def _attn_specs(rows):
    cur = lambda j: pl.BlockSpec((rows, PAIR), lambda p, n: (n, j + p))
    prev = lambda j: pl.BlockSpec((rows, PAIR), lambda p, n: (jnp.maximum(n - 1, 0), j + p))
    return cur, prev


def _pair_slope(pair, h):
    return jnp.where(pair == 0, jnp.float32(ALIBI_SLOPES[h]), jnp.float32(ALIBI_SLOPES[2 + h]))


def _halves(t):
    return [t[:, h * HEAD_DIM:(h + 1) * HEAD_DIM] for h in range(2)]


def _for_classes(dilation, step):
    if dilation == 1:
        step(0)
    else:
        lax.fori_loop(0, dilation, lambda z, c: (step(z), c)[1], 0)


def attn_fwd(proj, dilation, name):
    S = proj.shape[0]
    blk = ATTN_BLK
    rows = blk * dilation
    cur, prev = _attn_specs(rows)

    def body(q_ref, kp_ref, kc_ref, vp_ref, vc_ref, o_ref, l_ref):
        pair, n = pl.program_id(0), pl.program_id(1)

        def one_class(z):
            sel = pl.ds(z, blk, stride=dilation) if dilation > 1 else pl.ds(0, blk)
            q, kp, kc, vp, vc = (_halves(r[sel, :]) for r in (q_ref, kp_ref, kc_ref, vp_ref, vc_ref))
            res = [_attn_block(q[h], kp[h], kc[h], vp[h], vc[h], n, _pair_slope(pair, h), dilation) for h in range(2)]
            o_ref[sel, :] = jnp.concatenate([r[0] for r in res], axis=1)
            l_ref[sel, :] = jnp.concatenate([r[1] for r in res], axis=1)

        _for_classes(dilation, one_class)

    return pl.pallas_call(
        body, grid=(2, S // rows),
        in_specs=[cur(_QCOL), prev(_QCOL + 2), cur(_QCOL + 2), prev(_QCOL + 4), cur(_QCOL + 4)],
        out_specs=[cur(0), cur(0)], out_shape=[jax.ShapeDtypeStruct((S, D_GROUP), F32)] * 2,
        compiler_params=_cparams(("parallel", "arbitrary")), name=name)(proj, proj, proj, proj, proj)


def attn_bwd(proj, do, dlse, dilation, name):
    S = proj.shape[0]
    blk = ATTN_BLK
    rows = blk * dilation
    cur, prev = _attn_specs(rows)
    full = pl.BlockSpec((S, PAIR), lambda p, n: (0, p))

    def body(q_ref, kp_ref, kc_ref, vp_ref, vc_ref, do_ref, dl_ref, dq_ref, dk_ref, dv_ref):
        pair, n = pl.program_id(0), pl.program_id(1)

        @pl.when(n == 0)
        def _():
            dk_ref[...] = jnp.zeros_like(dk_ref)
            dv_ref[...] = jnp.zeros_like(dv_ref)

        def one_class(z):
            sel = pl.ds(z, blk, stride=dilation) if dilation > 1 else pl.ds(0, blk)
            q, kp, kc, vp, vc, do_v, dl_v = (_halves(r[sel, :]) for r in
                                             (q_ref, kp_ref, kc_ref, vp_ref, vc_ref, do_ref, dl_ref))
            grads = []
            for h in range(2):
                f = lambda q_, kp_, kc_, vp_, vc_, h=h: _attn_block(q_, kp_, kc_, vp_, vc_, n, _pair_slope(pair, h), dilation)
                _, vjp = jax.vjp(f, q[h], kp[h], kc[h], vp[h], vc[h])
                grads.append(vjp((do_v[h], dl_v[h])))
            both = lambda j: jnp.concatenate([grads[0][j], grads[1][j]], axis=1)
            dq_ref[sel, :] = both(0)
            if dilation > 1:
                here = pl.ds(n * rows + z, blk, stride=dilation)
                before = pl.ds(jnp.maximum(n - 1, 0) * rows + z, blk, stride=dilation)
            else:
                here = pl.ds(pl.multiple_of(n * blk, blk), blk)
                before = pl.ds(pl.multiple_of(jnp.maximum(n - 1, 0) * blk, blk), blk)
            dk_ref[here, :] = dk_ref[here, :] + both(2)
            dv_ref[here, :] = dv_ref[here, :] + both(4)
            dk_ref[before, :] = dk_ref[before, :] + both(1)
            dv_ref[before, :] = dv_ref[before, :] + both(3)

        _for_classes(dilation, one_class)

    return pl.pallas_call(
        body, grid=(2, S // rows),
        in_specs=[cur(_QCOL), prev(_QCOL + 2), cur(_QCOL + 2), prev(_QCOL + 4), cur(_QCOL + 4), cur(0), cur(0)],
        out_specs=[cur(0), full, full], out_shape=[jax.ShapeDtypeStruct((S, D_GROUP), F32)] * 3,
        compiler_params=_cparams(("parallel", "arbitrary")), name=name)(proj, proj, proj, proj, proj, do, dlse)


def _head_ones(width, group):
    i = lax.broadcasted_iota(jnp.int32, (width, width), 0) // group
    j = lax.broadcasted_iota(jnp.int32, (width, width), 1) // group
    return (i == j).astype(F32)


def _group_sum(x, group):
    return jnp.dot(x, _head_ones(x.shape[1], group), precision=HI, preferred_element_type=F32)


def _spread(width_in, width_out, rep):
    i = lax.broadcasted_iota(jnp.int32, (width_in, width_out), 0)
    j = lax.broadcasted_iota(jnp.int32, (width_in, width_out), 1) // rep
    return (i == j).astype(F32)


def _hdot(a, b):
    return jnp.dot(a, b, precision=HI, preferred_element_type=F32)


def _sigmoid(x):
    return 1.0 / (1.0 + jnp.exp(-x))


def _softplus(x):
    return jnp.maximum(x, 0.0) + jnp.log(1.0 + jnp.exp(jnp.minimum(x, -x)))


def _silu(x):
    return x * _sigmoid(x)


def rwkv_pre(layer):
    def fn(*args):
        if layer == 0:
            fs, w0, w2p, a0, a2p, g2p, k_k, k_a = args
        else:
            fs, vfirst, w0, w2p, a0, a2p, g2p, k_k, k_a, v0, v2p = args
        r, k, v = fs[:, 0:256], fs[:, 256:512], fs[:, 512:768]
        lora = fs[:, 768:896]
        w_log = -_softplus(-(w0 + _hdot(jnp.tanh(lora), w2p))) - 0.5
        decay = jnp.exp(-jnp.exp(w_log))
        a = _sigmoid(a0 + _hdot(lora, a2p))
        g = _hdot(_sigmoid(lora), g2p)
        if layer > 0:
            v = v + (vfirst - v) * _sigmoid(v0 + _hdot(fs[:, 896:1024], v2p))
        kk = k * k_k
        kk = kk / jnp.maximum(jnp.sqrt(_group_sum(kk * kk, HEAD_DIM)), 1e-12)
        k = k * (1.0 + (a - 1.0) * k_a)
        return r, decay, k, v, kk, a, g
    return fn


def rwkv_post(y, r, k, v, g, lnx_w, lnx_b, r_k):
    mu = _group_sum(y, HEAD_DIM) * (1.0 / HEAD_DIM)
    yc = y - mu
    var = _group_sum(yc * yc, HEAD_DIM) * (1.0 / HEAD_DIM)
    yn = yc * lax.rsqrt(var + RWKV_GN_EPS) * lnx_w + lnx_b
    bonus = _group_sum(r * k * r_k, HEAD_DIM) * v
    return ((yn + bonus) * g,)


def attn_combine(o1, o2, o3, l1, l2, l3):
    m = jnp.maximum(jnp.maximum(l1, l2), l3)
    e1, e2, e3 = jnp.exp(l1 - m), jnp.exp(l2 - m), jnp.exp(l3 - m)
    return ((o1 * e1 + o2 * e2 + o3 * e3) / (e1 + e2 + e3),)


def ssd_pre(xc, dtr, conv_b, dt_bias, a_log):
    xbc = _silu(xc + conv_b)
    xs, bm, cm = xbc[:, 0:256], xbc[:, 256:512], xbc[:, 512:768]
    dt = _softplus(dtr + dt_bias)
    a_neg = -jnp.exp(a_log)
    wide = _spread(128, N_HEADS * SSD_STATE, SSD_STATE)
    w = _hdot(dt, wide) * _hdot(a_neg, wide)
    xdt = xs * _hdot(dt, _spread(128, D_GROUP, HEAD_DIM))
    rr = jnp.concatenate([cm[:, 0:128], cm[:, 0:128], cm[:, 128:256], cm[:, 128:256]], axis=1)
    kk = jnp.concatenate([bm[:, 0:128], bm[:, 0:128], bm[:, 128:256], bm[:, 128:256]], axis=1)
    return rr, w, kk, xdt, xs


def ssd_post(ys, z, xs, d_skip, norm_w):
    y = ys + xs * _hdot(d_skip, _spread(128, D_GROUP, HEAD_DIM))
    y = y * _silu(z)
    half = D_GROUP // 2
    parts = []
    for g in range(2):
        t = y[:, g * half:(g + 1) * half]
        parts.append(t * lax.rsqrt(jnp.mean(t * t, axis=-1, keepdims=True) + RMS_EPS))
    return (jnp.concatenate(parts, axis=1) * norm_w,)


def hgrn_pre(seg, lb):
    q, f, i = seg[:, 0:256], seg[:, 256:512], seg[:, 512:768]
    forget = lb + (1.0 - lb) * _sigmoid(f)
    return _silu(q), 1.0 - forget, jnp.log(forget), i


def hgrn_post(o, seg, norm_w):
    g = seg[:, 768:1024]
    ms = _group_sum(o * o, HEAD_DIM) * (1.0 / HEAD_DIM)
    return (o * lax.rsqrt(ms + RMS_EPS) * norm_w * _silu(g),)


def ln_res(x, y, w, b):
    z = ALPHA * x + y
    mu = jnp.mean(z, axis=-1, keepdims=True)
    zc = z - mu
    var = jnp.mean(zc * zc, axis=-1, keepdims=True)
    return (zc * lax.rsqrt(var + LN_EPS) * w + b,)


def loss_and_grad(y, tgt, name):
    S, D = y.shape
    tile = ROW_TILE

    def body(y_ref, t_ref, l_ref, dy_ref):
        e = y_ref[...] - t_ref[...]
        dy_ref[...] = e * (1.0 / D)

        @pl.when(pl.program_id(0) == 0)
        def _():
            l_ref[...] = jnp.zeros_like(l_ref)

        per_row = 0.5 * jnp.mean(e * e, axis=-1, keepdims=True)
        l_ref[...] += jnp.sum(per_row, axis=0, keepdims=True) * jnp.ones((1, 128), F32)

    return pl.pallas_call(body, grid=(S // tile,), in_specs=[_row_spec(D, tile)] * 2,
                          out_specs=[_par_spec((1, 128)), _row_spec(D, tile)],
                          out_shape=[jax.ShapeDtypeStruct((1, 128), F32), jax.ShapeDtypeStruct((S, D), F32)],
                          compiler_params=_cparams(("arbitrary",)), name=name)(y, tgt)


def add_rows(arrs, name):
    (out,) = tl_fwd(lambda *a: (functools.reduce(lambda p, q: p + q, a),), name, arrs, [], [arrs[0].shape[1]])
    return out


def small_fwd(fn, name, ins, out_shapes):
    n = len(ins)

    def body(*refs):
        outs = fn(*[r[...] for r in refs[:n]])
        for o_ref, o in zip(refs[n:], outs):
            o_ref[...] = o

    return pl.pallas_call(body, out_shape=[jax.ShapeDtypeStruct(s, F32) for s in out_shapes], name=name)(*ins)


def small_bwd(fn, name, ins, cts):
    n, m = len(ins), len(cts)

    def body(*refs):
        _, vjp = jax.vjp(lambda *a: tuple(fn(*a)), *[r[...] for r in refs[:n]])
        grads = vjp(tuple(r[...] for r in refs[n:n + m]))
        for o_ref, g in zip(refs[n + m:], grads):
            o_ref[...] = g

    return pl.pallas_call(body, out_shape=[jax.ShapeDtypeStruct(a.shape, F32) for a in ins], name=name)(*ins, *cts)


def param_prep(lower_bounds, mu0, mu1):
    e = jnp.exp(lower_bounds - jnp.max(lower_bounds, axis=0, keepdims=True))
    sm = e / jnp.sum(e, axis=0, keepdims=True)
    lb0 = sm[0:1] - sm[0:1]
    lb1 = sm[0:1] + sm[1:2] - sm[0:1]
    return lb0, lb1, mu0, 1.0 - mu0, mu1, 1.0 - mu1


def _rows_tile(rows):
    return _pick(rows, (256, 128, 64, 32, 16, 8))


def sum_parts(parts, name):
    P, rows, cols = parts.shape
    tile = _rows_tile(rows)

    def body(p_ref, o_ref):
        acc = p_ref[0]
        for p in range(1, P):
            acc = acc + p_ref[p]
        o_ref[...] = acc

    return pl.pallas_call(body, grid=(rows // tile,), in_specs=[pl.BlockSpec((P, tile, cols), lambda i: (0, i, 0))],
                          out_specs=pl.BlockSpec((tile, cols), lambda i: (i, 0)),
                          out_shape=jax.ShapeDtypeStruct((rows, cols), F32),
                          compiler_params=_cparams(("parallel",)), name=name)(parts)


def pair_sum(own, got, ids, name):
    _, P, rows, cols = own.shape
    tile = _rows_tile(rows)

    def body(ids_ref, own_ref, got_ref, o_ref):
        o_ref[0] = (own_ref[0, 0] + got_ref[0]).astype(BF16)

    grid_spec = pltpu.PrefetchScalarGridSpec(
        num_scalar_prefetch=1, grid=(P, rows // tile),
        in_specs=[pl.BlockSpec((1, 1, tile, cols), lambda s, i, ids: (ids[0], s, i, 0)),
                  pl.BlockSpec((1, tile, cols), lambda s, i, ids: (s, i, 0))],
        out_specs=pl.BlockSpec((1, tile, cols), lambda s, i, ids: (s, i, 0)))
    return pl.pallas_call(body, grid_spec=grid_spec, out_shape=jax.ShapeDtypeStruct((P, rows, cols), BF16),
                          compiler_params=_cparams(("parallel", "parallel")), name=name)(ids, own, got)


def sum_chips(slots, own, got, ids, name):
    P, rows, cols = slots.shape
    tile = _rows_tile(rows)

    def body(ids_ref, s_ref, own_ref, got_ref, o_ref):
        chip = ids_ref[1]
        mine = own_ref[0, 0] + got_ref[0]
        acc = None
        for p in range(P):
            term = jnp.where(chip == p, mine, s_ref[p].astype(F32))
            acc = term if acc is None else acc + term
        o_ref[0] = acc

    grid_spec = pltpu.PrefetchScalarGridSpec(
        num_scalar_prefetch=1, grid=(rows // tile,),
        in_specs=[pl.BlockSpec((P, tile, cols), lambda i, ids: (0, i, 0)),
                  pl.BlockSpec((1, 1, tile, cols), lambda i, ids: (ids[0], ids[1], i, 0)),
                  pl.BlockSpec((1, tile, cols), lambda i, ids: (ids[1], i, 0))],
        out_specs=pl.BlockSpec((1, tile, cols), lambda i, ids: (ids[0], i, 0)))
    return pl.pallas_call(body, grid_spec=grid_spec, out_shape=jax.ShapeDtypeStruct((2, rows, cols), F32),
                          compiler_params=_cparams(("parallel",)), name=name)(ids, slots, own, got)


def adamw(w, g, m, v, name):
    rows, cols = w.shape
    tile = _rows_tile(rows)

    def body(w_ref, g_ref, m_ref, v_ref, d_ref, nm_ref, nv_ref):
        gv = g_ref[...]
        nm = ADAM_B1 * m_ref[...] + (1.0 - ADAM_B1) * gv
        nv = ADAM_B2 * v_ref[...] + (1.0 - ADAM_B2) * jnp.square(gv)
        m_hat = nm / (1.0 - ADAM_B1 ** ADAM_STEP)
        v_hat = nv / (1.0 - ADAM_B2 ** ADAM_STEP)
        d_ref[...] = -ADAM_LR * (m_hat / (jnp.sqrt(v_hat) + ADAM_EPS) + ADAM_WD * w_ref[...])
        nm_ref[...] = nm
        nv_ref[...] = nv

    spec = pl.BlockSpec((tile, cols), lambda i: (i, 0))
    return pl.pallas_call(body, grid=(rows // tile,), in_specs=[spec] * 4, out_specs=[spec] * 3,
                          out_shape=[jax.ShapeDtypeStruct((rows, cols), F32)] * 3,
                          compiler_params=_cparams(("parallel",)), name=name)(w, g, m, v)


MESH = pl.DeviceIdType.MESH
ANY = pl.BlockSpec(memory_space=pl.ANY)


def _flip(v, bit):
    return 1 - v if bit else v


_CHIP_RELATIONS = ((1, 0), (0, 1), (1, 1))


def gather_chips(arrs, small, name):
    n = len(arrs)

    def body(*refs):
        ins, small_in = refs[:n], refs[n]
        outs, small_out = refs[n + 1:2 * n + 1], refs[2 * n + 1]
        send, recv, fsend, frecv, ssend, srecv = refs[2 * n + 2:]
        x, y, c = lax.axis_index("x"), lax.axis_index("y"), lax.axis_index("c")
        me = 2 * x + y
        chips = [(_flip(x, bx), _flip(y, by)) for bx, by in _CHIP_RELATIONS]

        def over_ici(i, r, block_chip):
            return pltpu.make_async_remote_copy(src_ref=ins[i].at[c], dst_ref=outs[i].at[block_chip, c],
                                                send_sem=send.at[i, r], recv_sem=recv.at[i, r],
                                                device_id=(chips[r][0], chips[r][1], c), device_id_type=MESH)

        def to_sibling(i, r, layer):
            blk = outs[i].at[2 * chips[r][0] + chips[r][1], layer]
            return pltpu.make_async_remote_copy(src_ref=blk, dst_ref=blk, send_sem=fsend.at[i, r],
                                                recv_sem=frecv.at[i, r], device_id=(x, y, 1 - c), device_id_type=MESH)

        first = [over_ici(i, r, me) for i in range(n) for r in range(3)]
        smalls = [pltpu.make_async_remote_copy(src_ref=small_in, dst_ref=small_out.at[me], send_sem=ssend.at[r],
                                               recv_sem=srecv.at[r], device_id=(chips[r][0], chips[r][1], c),
                                               device_id_type=MESH) for r in range(3)]
        for cp in first + smalls:
            cp.start()
        passed = []
        for r in range(3):
            for i in range(n):
                over_ici(i, r, 2 * chips[r][0] + chips[r][1]).wait_recv()
                fw = to_sibling(i, r, c)
                fw.start()
                passed.append(fw)
        for r in range(3):
            for i in range(n):
                to_sibling(i, r, 1 - c).wait_recv()
        for cp in first + passed:
            cp.wait_send()
        for cp in smalls:
            cp.wait()

    return pl.pallas_call(
        body, in_specs=[ANY] * (n + 1), out_specs=[ANY] * (n + 1),
        out_shape=[jax.ShapeDtypeStruct((4,) + a.shape, a.dtype) for a in arrs]
        + [jax.ShapeDtypeStruct((4,) + small.shape, small.dtype)],
        scratch_shapes=[pltpu.SemaphoreType.DMA((n, 3)), pltpu.SemaphoreType.DMA((n, 3)), pltpu.SemaphoreType.DMA((n, 3)),
                        pltpu.SemaphoreType.DMA((n, 3)), pltpu.SemaphoreType.DMA((3,)), pltpu.SemaphoreType.DMA((3,))],
        name=name)(*arrs, small)


_RELATIONS = tuple((r >> 2 & 1, r >> 1 & 1, r & 1) for r in range(1, 8))


def gather_devices(arr, name):
    def body(in_ref, out_ref, send, recv, loc):
        x, y, c = lax.axis_index("x"), lax.axis_index("y"), lax.axis_index("c")
        me = 4 * x + 2 * y + c
        lc = pltpu.make_async_copy(in_ref, out_ref.at[me], loc)
        lc.start()
        pending = [lc]
        for r, (bx, by, bc) in enumerate(_RELATIONS):
            cp = pltpu.make_async_remote_copy(src_ref=in_ref, dst_ref=out_ref.at[me], send_sem=send.at[r],
                                              recv_sem=recv.at[r], device_id=(_flip(x, bx), _flip(y, by), _flip(c, bc)),
                                              device_id_type=MESH)
            cp.start()
            pending.append(cp)
        for cp in pending:
            cp.wait()

    return pl.pallas_call(
        body, in_specs=[ANY], out_specs=ANY, out_shape=jax.ShapeDtypeStruct((8,) + arr.shape, arr.dtype),
        scratch_shapes=[pltpu.SemaphoreType.DMA((7,)), pltpu.SemaphoreType.DMA((7,)), pltpu.SemaphoreType.DMA(())],
        name=name)(arr)


def pair_exchange(arrs, name):
    n = len(arrs)

    def body(*refs):
        ins, outs = refs[:n], refs[n:2 * n]
        send, recv = refs[2 * n:]
        x, y, c = lax.axis_index("x"), lax.axis_index("y"), lax.axis_index("c")
        pending = []
        for i in range(n):
            for s in range(4):
                cp = pltpu.make_async_remote_copy(src_ref=ins[i].at[1 - c, s], dst_ref=outs[i].at[s],
                                                  send_sem=send.at[i, s], recv_sem=recv.at[i, s],
                                                  device_id=(x, y, 1 - c), device_id_type=MESH)
                cp.start()
                pending.append(cp)
        for cp in pending:
            cp.wait()

    return pl.pallas_call(
        body, in_specs=[ANY] * n, out_specs=[ANY] * n,
        out_shape=[jax.ShapeDtypeStruct(a.shape[1:], a.dtype) for a in arrs],
        scratch_shapes=[pltpu.SemaphoreType.DMA((n, 4)), pltpu.SemaphoreType.DMA((n, 4))],
        name=name)(*arrs)


def reduce_chips(arrs, name):
    n = len(arrs)

    def body(*refs):
        ins, outs = refs[:n], refs[n:2 * n]
        send, recv, loc = refs[2 * n:]
        x, y, c = lax.axis_index("x"), lax.axis_index("y"), lax.axis_index("c")
        me = 2 * x + y
        pending = []
        for i in range(n):
            for r, (bx, by) in enumerate(_CHIP_RELATIONS):
                px, py = _flip(x, bx), _flip(y, by)
                cp = pltpu.make_async_remote_copy(src_ref=ins[i].at[2 * px + py], dst_ref=outs[i].at[me],
                                                  send_sem=send.at[i, r], recv_sem=recv.at[i, r],
                                                  device_id=(px, py, c), device_id_type=MESH)
                cp.start()
                pending.append(cp)
        for i in range(n):
            lc = pltpu.make_async_copy(ins[i].at[me], outs[i].at[me], loc.at[i])
            lc.start()
            pending.append(lc)
        for cp in pending:
            cp.wait()

    return pl.pallas_call(
        body, in_specs=[ANY] * n, out_specs=[ANY] * n,
        out_shape=[jax.ShapeDtypeStruct(a.shape, a.dtype) for a in arrs],
        scratch_shapes=[pltpu.SemaphoreType.DMA((n, 3)), pltpu.SemaphoreType.DMA((n, 3)), pltpu.SemaphoreType.DMA((n,))],
        name=name)(*arrs)


EXCHANGE_PIECES = 8


def sibling_exchange(arrs, name):
    n = len(arrs)

    def body(*refs):
        bufs = refs[n:2 * n]
        send, recv = refs[2 * n:]
        x, y, c = lax.axis_index("x"), lax.axis_index("y"), lax.axis_index("c")
        pending = []
        for i in range(n):
            rows = bufs[i].shape[1] // EXCHANGE_PIECES
            for j in range(EXCHANGE_PIECES):
                piece = bufs[i].at[c, pl.ds(j * rows, rows)]
                cp = pltpu.make_async_remote_copy(src_ref=piece, dst_ref=piece, send_sem=send.at[i, j],
                                                  recv_sem=recv.at[i, j], device_id=(x, y, 1 - c), device_id_type=MESH)
                cp.start()
                pending.append(cp)
        for i in range(n):
            rows = bufs[i].shape[1] // EXCHANGE_PIECES
            for j in range(EXCHANGE_PIECES):
                landed = bufs[i].at[1 - c, pl.ds(j * rows, rows)]
                pltpu.make_async_remote_copy(src_ref=landed, dst_ref=landed, send_sem=send.at[i, j], recv_sem=recv.at[i, j],
                                             device_id=(x, y, 1 - c), device_id_type=MESH).wait_recv()
        for cp in pending:
            cp.wait_send()

    return pl.pallas_call(
        body, in_specs=[ANY] * n, out_specs=[ANY] * n,
        out_shape=[jax.ShapeDtypeStruct(a.shape, a.dtype) for a in arrs], input_output_aliases={i: i for i in range(n)},
        scratch_shapes=[pltpu.SemaphoreType.DMA((n, EXCHANGE_PIECES)), pltpu.SemaphoreType.DMA((n, EXCHANGE_PIECES))],
        name=name)(*arrs)


def rwkv_fwd(l, seg, taps, pars, vfirst):
    fs = fir_fwd(seg, taps, f"rwkv_shift_fwd{l}")
    rows = [fs] + ([vfirst] if l else [])
    R, W, K, V, KK, A, G = tl_fwd(rwkv_pre(l), f"rwkv_pre_fwd{l}", rows, pars["pre"],
                                  [HM64, HM64, HM64, D_GROUP, HM64, HM64, D_GROUP])
    Y, sall = scan_fwd(R, W, K, V, KK, A, f"rwkv_scan_fwd{l}")
    (out,) = tl_fwd(rwkv_post, f"rwkv_post_fwd{l}", [Y, R, K, V, G], pars["post"], [D_GROUP])
    return out, V, (seg, taps, rows, R, W, K, V, KK, A, G, Y, sall)


def rwkv_bwd(l, saved, pars, dout, dv_extra):
    seg, taps, rows, R, W, K, V, KK, A, G, Y, sall = saved
    (dY, dR1, dK1, dV1, dG), dpost = tl_bwd(rwkv_post, f"rwkv_post_bwd{l}", [Y, R, K, V, G], pars["post"], [[dout]])
    (dR2, dW, dK2, dKK, dA), dV2 = scan_bwd(R, W, K, V, KK, A, sall, dY, f"rwkv_scan_bwd{l}")
    cts = [[dR1, dR2], [dW], [dK1, dK2], [dV1, dV2] + dv_extra, [dKK], [dA], [dG]]
    drows, dpre = tl_bwd(rwkv_pre(l), f"rwkv_pre_bwd{l}", rows, pars["pre"], cts)
    dseg, dtaps, _ = fir_bwd(seg, taps, [drows[0]], f"rwkv_shift_bwd{l}")
    return dseg, (drows[1] if l else None), dtaps, dpre, dpost


def attn_mix_fwd(l, proj):
    os_, ls_ = [], []
    for b, (_, d) in enumerate(DILATED_BRANCHES):
        o, lse = attn_fwd(proj, d, f"attn_fwd{l}_{b}")
        os_.append(o)
        ls_.append(lse)
    (out,) = tl_fwd(attn_combine, f"attn_combine_fwd{l}", os_ + ls_, [], [D_GROUP])
    return out, (proj, os_, ls_)


def attn_mix_bwd(l, saved, dout):
    proj, os_, ls_ = saved
    drows, _ = tl_bwd(attn_combine, f"attn_combine_bwd{l}", os_ + ls_, [], [[dout]])
    grads = [attn_bwd(proj, drows[b], drows[3 + b], d, f"attn_bwd{l}_{b}") for b, (_, d) in enumerate(DILATED_BRANCHES)]
    return tuple(add_rows([g[j] for g in grads], f"attn_d{'qkv'[j]}{l}") for j in range(3))


def ssd_fwd(l, z, xbc, dtr, pars):
    xc = fir_fwd(xbc, pars["taps"], f"ssd_conv_fwd{l}")
    rr, w, kk, xdt, xs = tl_fwd(ssd_pre, f"ssd_pre_fwd{l}", [xc, dtr], pars["pre"], [HM128, HM128, HM128, HM64, D_GROUP])
    blocks = [rr, kk, w, xdt]
    ys, states = chunk_fwd(ssd_chunk, f"ssd_scan_fwd{l}", blocks, (HEAD_DIM, SSD_STATE), HEAD_DIM, SSD_GROUP)
    (out,) = tl_fwd(ssd_post, f"ssd_post_fwd{l}", [ys, z, xs], pars["post"], [D_GROUP])
    return out, (z, xbc, dtr, xc, blocks, states, xs, ys)


def ssd_bwd(l, saved, pars, dout):
    z, xbc, dtr, xc, blocks, states, xs, ys = saved
    (dys, dz, dxs), dpost = tl_bwd(ssd_post, f"ssd_post_bwd{l}", [ys, z, xs], pars["post"], [[dout]])
    drr, dkk, dw, dxdt = chunk_bwd(ssd_chunk, f"ssd_scan_bwd{l}", blocks, states, dys, SSD_GROUP)
    (dxc, ddtr), dpre = tl_bwd(ssd_pre, f"ssd_pre_bwd{l}", [xc, dtr], pars["pre"], [[drr], [dw], [dkk], [dxdt], [dxs]])
    dxbc, dtaps, _ = fir_bwd(xbc, pars["taps"], [dxc], f"ssd_conv_bwd{l}")
    return dz, dxbc, ddtr, dtaps, dpre, dpost


def hgrn_fwd(l, seg, pars):
    blocks = tl_fwd(hgrn_pre, f"hgrn_pre_fwd{l}", [seg], pars["pre"], [HM64] * 4)
    o, states = chunk_fwd(hgrn_chunk, f"hgrn_scan_fwd{l}", blocks, (HEAD_DIM, HEAD_DIM), HEAD_DIM, HGRN_GROUP)
    (out,) = tl_fwd(hgrn_post, f"hgrn_post_fwd{l}", [o, seg], pars["post"], [D_GROUP])
    return out, (seg, blocks, states, o)


def hgrn_bwd(l, saved, pars, dout):
    seg, blocks, states, o = saved
    (do, dseg1), dpost = tl_bwd(hgrn_post, f"hgrn_post_bwd{l}", [o, seg], pars["post"], [[dout]])
    dq, dkk, dlf, di = chunk_bwd(hgrn_chunk, f"hgrn_scan_bwd{l}", blocks, states, do, HGRN_GROUP)
    (dseg2,), dpre = tl_bwd(hgrn_pre, f"hgrn_pre_bwd{l}", [seg], pars["pre"], [[dq], [dkk], [dlf], [di]])
    return add_rows([dseg1, dseg2], f"hgrn_dseg{l}"), dpre, dpost


def layer_fwd(l, x, wts, pars, vfirst):
    proj = matmul(x, wts["in"], "nn", f"proj_fwd{l}")
    seg_h, seg_r = Cols(proj, SEG_HGRN, 1024), Cols(proj, SEG_RWKV, 1024)
    z, xbc, dtr = Cols(proj, SEG_Z, D_GROUP), Cols(proj, SEG_XBC, SSD_XBC), Cols(proj, SEG_DT, 128)
    ya, v_rwkv, sa = rwkv_fwd(l, seg_r, pars["rwkv"]["taps"], pars["rwkv"], vfirst)
    yb, sb = attn_mix_fwd(l, proj)
    yc, sc = ssd_fwd(l, z, xbc, dtr, pars["ssd"])
    yd, sd = hgrn_fwd(l, seg_h, pars["hgrn"])
    mix = [ya, yb, yc, yd]
    mo = matmul(mix, wts["out"], "nn", f"out_fwd{l}")
    (x1,) = tl_fwd(ln_res, f"ln1_fwd{l}", [x, mo], pars["ln1"], [D_MODEL])
    u = matmul(x1, wts["up"], "nn", f"up_fwd{l}")
    dn = matmul(u, wts["down"], "nn", f"down_fwd{l}", a_relu2=True)
    (x2,) = tl_fwd(ln_res, f"ln2_fwd{l}", [x1, dn], pars["ln2"], [D_MODEL])
    return x2, v_rwkv, (x, sa, sb, sc, sd, mix, mo, x1, u, dn)


def layer_bwd(l, saved, wts, pars, dx2, dv_extra):
    x, sa, sb, sc, sd, mix, mo, x1, u, dn = saved
    S = x.shape[0]
    g = {}
    (dx1a, ddn), g["ln2"] = tl_bwd(ln_res, f"ln2_bwd{l}", [x1, dn], pars["ln2"], [[dx2]])
    g["down"] = matmul(u, ddn, "tn", f"down_dw{l}", a_relu2=True)
    du = matmul(ddn, wts["down"], "nt", f"down_dx{l}", relu2_grad_of=u)
    g["up"] = matmul(x1, du, "tn", f"up_dw{l}")
    dx1 = matmul(du, wts["up"], "nt", f"up_dx{l}", add=dx1a)
    (dxa, dmo), g["ln1"] = tl_bwd(ln_res, f"ln1_bwd{l}", [x, mo], pars["ln1"], [[dx1]])
    g["out"] = matmul(mix, dmo, "tn", f"out_dw{l}")
    dmix = matmul(dmo, wts["out"], "nt", f"out_dx{l}")
    dya, dyb, dyc, dyd = (Cols(dmix, j * D_GROUP, D_GROUP) for j in range(4))
    dseg_r, dvfirst, g["rwkv_taps"], g["rwkv_pre"], g["rwkv_post"] = rwkv_bwd(l, sa, pars["rwkv"], dya, dv_extra)
    dq, dk, dv = attn_mix_bwd(l, sb, dyb)
    dz, dxbc, ddtr, g["ssd_taps"], g["ssd_pre"], g["ssd_post"] = ssd_bwd(l, sc, pars["ssd"], dyc)
    dseg_h, g["hgrn_pre"], g["hgrn_post"] = hgrn_bwd(l, sd, pars["hgrn"], dyd)
    dproj = jnp.concatenate([dseg_h, dseg_r, dq, dk, dv, dz, dxbc, ddtr, jnp.zeros((S, PROJ_W - SEG_DT - 128), F32)], axis=1)
    g["in"] = matmul(x, dproj, "tn", f"proj_dw{l}")
    dx = matmul(dproj, wts["in"], "nt", f"proj_dx{l}", add=dxa)
    return dx, dvfirst, g


SMALL = ("lower_bounds", "w_in_vres", "mu_shift", "mu_vres", "rwkv_w0", "rwkv_w2", "rwkv_a0", "rwkv_a2", "rwkv_g2",
         "rwkv_k_k", "rwkv_k_a", "rwkv_r_k", "rwkv_lnx_w", "rwkv_lnx_b", "rwkv_v0", "rwkv_v2", "ssd_conv_w",
         "ssd_conv_b", "ssd_dt_bias", "ssd_A_log", "ssd_D", "ssd_norm_w", "hgrn_norm_w", "ln1_w", "ln1_b", "ln2_w", "ln2_b")
BIG = ("w_in", "w_out", "w_up", "w_down")
SMALL_SHARDED = {"w_in_vres": 1, "rwkv_w2": 2, "rwkv_a2": 2, "rwkv_g2": 2, "rwkv_v2": 2, "ssd_conv_w": 2}
WEIGHTS = ("lower_bounds", "w_in", "w_in_vres", "mu_shift", "mu_vres", "rwkv_w0", "rwkv_w2", "rwkv_a0", "rwkv_a2",
           "rwkv_g2", "rwkv_k_k", "rwkv_k_a", "rwkv_r_k", "rwkv_lnx_w", "rwkv_lnx_b", "rwkv_v0", "rwkv_v2",
           "ssd_conv_w", "ssd_conv_b", "ssd_dt_bias", "ssd_A_log", "ssd_D", "ssd_norm_w", "hgrn_norm_w", "w_out",
           "ln1_w", "ln1_b", "w_up", "w_down", "ln2_w", "ln2_b")


def _row(v, width=None):
    v = v.reshape(1, -1).astype(F32)
    if width is not None and v.shape[1] < width:
        v = jnp.pad(v, ((0, 0), (0, width - v.shape[1])))
    return v


def _rows_at(m, rows, at):
    return jnp.pad(m.astype(F32), ((at, rows - at - m.shape[0]), (0, 0)))


def _pad_w_in(w_in_l, vres):
    rows = w_in_l.shape[0]
    out = []
    order = sorted(_PIECES, key=lambda p: p[2])
    pos = 0
    for start, width, at in order:
        if at > pos:
            out.append(jnp.zeros((rows, at - pos), w_in_l.dtype))
        out.append(w_in_l[:, start:start + width])
        pos = at + width
        if at == SEG_RWKV and vres is not None:
            out.append(vres.astype(w_in_l.dtype))
            pos += vres.shape[1]
    out.append(jnp.zeros((rows, PROJ_W - pos), w_in_l.dtype))
    return jnp.concatenate(out, axis=1)


def _unpad_w_in(g):
    return jnp.concatenate([g[:, at:at + width] for _, width, at in _PIECES], axis=1)


def layer_params(l, sp, prep):
    lb, mu, om = prep[l], prep[2 + 2 * l], prep[3 + 2 * l]
    pre = [_row(sp["rwkv_w0"][l]), _rows_at(sp["rwkv_w2"][l], 128, 0), _row(sp["rwkv_a0"][l]),
           _rows_at(sp["rwkv_a2"][l], 128, 32), _rows_at(sp["rwkv_g2"][l], 128, 64),
           _row(sp["rwkv_k_k"][l]), _row(sp["rwkv_k_a"][l])]
    if l:
        pre += [_row(sp["rwkv_v0"][l - 1]), _rows_at(sp["rwkv_v2"][l - 1], 128, 0)]
    return {
        "rwkv": {"taps": jnp.concatenate([mu, om], axis=0), "pre": pre,
                 "post": [_row(sp["rwkv_lnx_w"][l]), _row(sp["rwkv_lnx_b"][l]), _row(sp["rwkv_r_k"][l])]},
        "ssd": {"taps": sp["ssd_conv_w"][l].astype(F32),
                "pre": [_row(sp["ssd_conv_b"][l]), _row(sp["ssd_dt_bias"][l], 128), _row(sp["ssd_A_log"][l], 128)],
                "post": [_row(sp["ssd_D"][l], 128), _row(sp["ssd_norm_w"][l])]},
        "hgrn": {"pre": [lb], "post": [_row(sp["hgrn_norm_w"][l])]},
        "ln1": [_row(sp["ln1_w"][l]), _row(sp["ln1_b"][l])],
        "ln2": [_row(sp["ln2_w"][l]), _row(sp["ln2_b"][l])],
    }


def _mu_full(sp, l):
    parts = [sp["mu_shift"][l].reshape(1, -1)]
    if l:
        parts.append(sp["mu_vres"][l - 1].reshape(1, -1))
    return _row(jnp.concatenate(parts, axis=1), 1024)


def local_step(x, target, big, sp):
    prep_in = [sp["lower_bounds"].astype(F32), _mu_full(sp, 0), _mu_full(sp, 1)]
    prep = small_fwd(param_prep, "param_prep_fwd", prep_in,
                     [(1, D_GROUP), (1, D_GROUP), (1, 1024), (1, 1024), (1, 1024), (1, 1024)])
    pars, wts = [], []
    for l in range(DEPTH):
        pars.append(layer_params(l, sp, prep))
        vres = sp["w_in_vres"][l - 1].astype(BF16) if l else None
        wts.append({"in": _pad_w_in(big["w_in"][l], vres), "out": big["w_out"][l], "up": big["w_up"][l],
                    "down": big["w_down"][l]})
    h, vfirst, saved = x, None, []
    for l in range(DEPTH):
        h, v_l, sv = layer_fwd(l, h, wts[l], pars[l], vfirst)
        vfirst = v_l if l == 0 else vfirst
        saved.append(sv)
    loss_row, dh = loss_and_grad(h, target, "loss")
    grads, dv_extra = [None] * DEPTH, []
    for l in reversed(range(DEPTH)):
        dh, dvfirst, grads[l] = layer_bwd(l, saved[l], wts[l], pars[l], dh, dv_extra)
        dv_extra = [dvfirst] if l else []
    cts = [grads[0]["hgrn_pre"][0], grads[1]["hgrn_pre"][0]]
    for l in range(DEPTH):
        cts += [grads[l]["rwkv_taps"][0:1], grads[l]["rwkv_taps"][1:2]]
    d_lower, d_mu0, d_mu1 = small_bwd(param_prep, "param_prep_bwd", prep_in, cts)
    d_mu = [d_mu0, d_mu1]
    gb = {"w_in": [_unpad_w_in(grads[l]["in"]) for l in range(DEPTH)], "w_out": [grads[l]["out"] for l in range(DEPTH)],
          "w_up": [grads[l]["up"] for l in range(DEPTH)], "w_down": [grads[l]["down"] for l in range(DEPTH)]}
    st = lambda f: jnp.stack([f(l) for l in range(DEPTH)])
    g1 = grads[1]
    gs = {
        "lower_bounds": d_lower,
        "w_in_vres": g1["in"][None, :, VRES_COL:VRES_COL + 32],
        "mu_shift": st(lambda l: d_mu[l][0, :896]),
        "mu_vres": d_mu[1][:, 896:928],
        "rwkv_w0": st(lambda l: grads[l]["rwkv_pre"][0][0]),
        "rwkv_w2": st(lambda l: grads[l]["rwkv_pre"][1][0:32]),
        "rwkv_a0": st(lambda l: grads[l]["rwkv_pre"][2][0]),
        "rwkv_a2": st(lambda l: grads[l]["rwkv_pre"][3][32:64]),
        "rwkv_g2": st(lambda l: grads[l]["rwkv_pre"][4][64:128]),
        "rwkv_k_k": st(lambda l: grads[l]["rwkv_pre"][5][0]),
        "rwkv_k_a": st(lambda l: grads[l]["rwkv_pre"][6][0]),
        "rwkv_r_k": st(lambda l: grads[l]["rwkv_post"][2].reshape(N_HEADS, HEAD_DIM)),
        "rwkv_lnx_w": st(lambda l: grads[l]["rwkv_post"][0][0]),
        "rwkv_lnx_b": st(lambda l: grads[l]["rwkv_post"][1][0]),
        "rwkv_v0": g1["rwkv_pre"][7],
        "rwkv_v2": g1["rwkv_pre"][8][None, 0:32],
        "ssd_conv_w": st(lambda l: grads[l]["ssd_taps"]),
        "ssd_conv_b": st(lambda l: grads[l]["ssd_pre"][0][0]),
        "ssd_dt_bias": st(lambda l: grads[l]["ssd_pre"][1][0, :N_HEADS]),
        "ssd_A_log": st(lambda l: grads[l]["ssd_pre"][2][0, :N_HEADS]),
        "ssd_D": st(lambda l: grads[l]["ssd_post"][0][0, :N_HEADS]),
        "ssd_norm_w": st(lambda l: grads[l]["ssd_post"][1][0]),
        "hgrn_norm_w": st(lambda l: grads[l]["hgrn_post"][0][0]),
        "ln1_w": st(lambda l: grads[l]["ln1"][0][0]),
        "ln1_b": st(lambda l: grads[l]["ln1"][1][0]),
        "ln2_w": st(lambda l: grads[l]["ln2"][0][0]),
        "ln2_b": st(lambda l: grads[l]["ln2"][1][0]),
    }
    return loss_row, dh, gb, gs


def _pack(vecs):
    flat, meta, pos = [], [], 0
    for v in vecs:
        flat.append(v.reshape(-1).astype(F32))
        meta.append((pos, v.shape))
        pos += v.size
    total = -(-pos // 1024) * 1024
    flat.append(jnp.zeros((total - pos,), F32))
    return jnp.concatenate(flat).reshape(total // 128, 128), meta


def _unpack(packed, meta):
    flat = packed.reshape(-1)
    return [flat[off:off + math.prod(shape)].reshape(shape) for off, shape in meta]


def _to_shards(name, g):
    if name == "w_in":
        return jnp.transpose(g.reshape(g.shape[0], 4, g.shape[1] // 4), (1, 0, 2))
    if name == "w_up":
        return jnp.transpose(g.reshape(g.shape[0], 4, g.shape[1] // 4), (1, 0, 2))
    return g.reshape(4, g.shape[0] // 4, g.shape[1])


def _from_chips(name, g):
    if name in ("w_in", "w_up"):
        return jnp.transpose(g, (1, 2, 0, 3)).reshape(g.shape[1], g.shape[2], 4 * g.shape[3])
    return jnp.transpose(g, (1, 0, 2, 3)).reshape(g.shape[1], 4 * g.shape[2], g.shape[3])


INPUT_NAMES = ("x",) + WEIGHTS + ("loss_target",) + tuple("m_" + n for n in WEIGHTS) + tuple("v_" + n for n in WEIGHTS)


def _step(*args):
    a = dict(zip(INPUT_NAMES, args, strict=True))
    chip = 2 * lax.axis_index("x") + lax.axis_index("y")

    sharded_names = list(SMALL_SHARDED)
    small_pack, small_meta = _pack([a[n] for n in sharded_names])
    own = [a[n].astype(BF16) for n in BIG]
    gathered = gather_chips(own, small_pack, "gather_weights")
    here = lambda full, mine: lax.dynamic_update_slice(full, mine[None], (chip,) + (0,) * mine.ndim)
    big = {n: _from_chips(n, here(g, o)) for n, g, o in zip(BIG, gathered, own)}
    sp = {n: a[n] for n in SMALL if n not in SMALL_SHARDED}
    small_all = here(gathered[-1], small_pack)
    per_chip = [_unpack(small_all[s], small_meta) for s in range(4)]
    for j, n in enumerate(sharded_names):
        sp[n] = jnp.concatenate([per_chip[s][j] for s in range(4)], axis=SMALL_SHARDED[n])

    loss_row, gx, gb, gs = local_step(a["x"][0], a["loss_target"][0], big, sp)

    partials = [jnp.stack([_to_shards(n, gb[n][l]) for l in range(DEPTH)]) for n in BIG]
    got = pair_exchange(partials, "pair_exchange")
    ids = jnp.stack([lax.axis_index("c"), chip]).astype(jnp.int32)
    chip_sums = [pair_sum(p, q, ids, f"pair_sum_{n}") for n, p, q in zip(BIG, partials, got)]
    slots = reduce_chips(chip_sums, "reduce_big")
    mine = [sum_chips(sl, p, q, ids, f"sum_{n}") for n, sl, p, q in zip(BIG, slots, partials, got)]
    summed = sibling_exchange(mine, "exchange_big")
    out_g, out_d, out_m, out_v = {}, {}, {}, {}
    for n, g in zip(BIG, summed):
        shape = a[n].shape
        flat = lambda t: t.reshape(shape[0] * shape[1], shape[2])
        d, nm, nv = adamw(flat(a[n]), flat(g), flat(a["m_" + n]), flat(a["v_" + n]), f"adamw_{n}")
        out_g[n], out_d[n], out_m[n], out_v[n] = g.reshape(shape), d.reshape(shape), nm.reshape(shape), nv.reshape(shape)

    vec, meta = _pack([loss_row] + [gs[n] for n in SMALL])
    total = sum_parts(gather_devices(vec, "gather_small"), "sum_small")
    parts = _unpack(total, meta)
    loss = parts[0][0, 0]
    g_small = {}
    for n, g in zip(SMALL, parts[1:]):
        if n in SMALL_SHARDED:
            ax = SMALL_SHARDED[n]
            size = a[n].shape[ax]
            g = lax.dynamic_slice_in_dim(g, chip * size, size, axis=ax)
        g_small[n] = g
    pw, pmeta = _pack([a[n] for n in SMALL])
    pg, _ = _pack([g_small[n] for n in SMALL])
    pm, _ = _pack([a["m_" + n] for n in SMALL])
    pv, _ = _pack([a["v_" + n] for n in SMALL])
    d, nm, nv = adamw(pw, pg, pm, pv, "adamw_small")
    for n, dd, mm, vv in zip(SMALL, _unpack(d, pmeta), _unpack(nm, pmeta), _unpack(nv, pmeta)):
        out_g[n], out_d[n], out_m[n], out_v[n] = g_small[n], dd, mm, vv

    return (loss, gx[None], *[out_g[n] for n in WEIGHTS], *[out_d[n] for n in WEIGHTS],
            *[out_m[n] for n in WEIGHTS], *[out_v[n] for n in WEIGHTS])


def kernel(x, lower_bounds, w_in, w_in_vres, mu_shift, mu_vres, rwkv_w0, rwkv_w2, rwkv_a0, rwkv_a2, rwkv_g2, rwkv_k_k, rwkv_k_a, rwkv_r_k, rwkv_lnx_w, rwkv_lnx_b, rwkv_v0, rwkv_v2, ssd_conv_w, ssd_conv_b, ssd_dt_bias, ssd_A_log, ssd_D, ssd_norm_w, hgrn_norm_w, w_out, ln1_w, ln1_b, w_up, w_down, ln2_w, ln2_b, loss_target, m_lower_bounds, m_w_in, m_w_in_vres, m_mu_shift, m_mu_vres, m_rwkv_w0, m_rwkv_w2, m_rwkv_a0, m_rwkv_a2, m_rwkv_g2, m_rwkv_k_k, m_rwkv_k_a, m_rwkv_r_k, m_rwkv_lnx_w, m_rwkv_lnx_b, m_rwkv_v0, m_rwkv_v2, m_ssd_conv_w, m_ssd_conv_b, m_ssd_dt_bias, m_ssd_A_log, m_ssd_D, m_ssd_norm_w, m_hgrn_norm_w, m_w_out, m_ln1_w, m_ln1_b, m_w_up, m_w_down, m_ln2_w, m_ln2_b, v_lower_bounds, v_w_in, v_w_in_vres, v_mu_shift, v_mu_vres, v_rwkv_w0, v_rwkv_w2, v_rwkv_a0, v_rwkv_a2, v_rwkv_g2, v_rwkv_k_k, v_rwkv_k_a, v_rwkv_r_k, v_rwkv_lnx_w, v_rwkv_lnx_b, v_rwkv_v0, v_rwkv_v2, v_ssd_conv_w, v_ssd_conv_b, v_ssd_dt_bias, v_ssd_A_log, v_ssd_D, v_ssd_norm_w, v_hgrn_norm_w, v_w_out, v_ln1_w, v_ln1_b, v_w_up, v_w_down, v_ln2_w, v_ln2_b):
    return _step(x, lower_bounds, w_in, w_in_vres, mu_shift, mu_vres, rwkv_w0, rwkv_w2, rwkv_a0, rwkv_a2, rwkv_g2, rwkv_k_k, rwkv_k_a, rwkv_r_k, rwkv_lnx_w, rwkv_lnx_b, rwkv_v0, rwkv_v2, ssd_conv_w, ssd_conv_b, ssd_dt_bias, ssd_A_log, ssd_D, ssd_norm_w, hgrn_norm_w, w_out, ln1_w, ln1_b, w_up, w_down, ln2_w, ln2_b, loss_target, m_lower_bounds, m_w_in, m_w_in_vres, m_mu_shift, m_mu_vres, m_rwkv_w0, m_rwkv_w2, m_rwkv_a0, m_rwkv_a2, m_rwkv_g2, m_rwkv_k_k, m_rwkv_k_a, m_rwkv_r_k, m_rwkv_lnx_w, m_rwkv_lnx_b, m_rwkv_v0, m_rwkv_v2, m_ssd_conv_w, m_ssd_conv_b, m_ssd_dt_bias, m_ssd_A_log, m_ssd_D, m_ssd_norm_w, m_hgrn_norm_w, m_w_out, m_ln1_w, m_ln1_b, m_w_up, m_w_down, m_ln2_w, m_ln2_b, v_lower_bounds, v_w_in, v_w_in_vres, v_mu_shift, v_mu_vres, v_rwkv_w0, v_rwkv_w2, v_rwkv_a0, v_rwkv_a2, v_rwkv_g2, v_rwkv_k_k, v_rwkv_k_a, v_rwkv_r_k, v_rwkv_lnx_w, v_rwkv_lnx_b, v_rwkv_v0, v_rwkv_v2, v_ssd_conv_w, v_ssd_conv_b, v_ssd_dt_bias, v_ssd_A_log, v_ssd_D, v_ssd_norm_w, v_hgrn_norm_w, v_w_out, v_ln1_w, v_ln1_b, v_w_up, v_w_down, v_ln2_w, v_ln2_b)
```

```python
import functools
import math

import jax
import jax.numpy as jnp
from jax import lax
from jax.experimental import pallas as pl
from jax.experimental.pallas import tpu as pltpu

F32 = jnp.float32
BF16 = jnp.bfloat16
HI = lax.Precision.HIGHEST

DEPTH = 2
D_MODEL = 1024
D_GROUP = 256
HEAD_DIM = 64
N_HEADS = 4
SSD_STATE = 128
SSD_XBC = 768
SSD_CONV = 4
D_FF = 4096
ALPHA = (2.0 * DEPTH) ** 0.25
LN_EPS = 1e-5
RMS_EPS = 1e-5
RWKV_GN_EPS = HEAD_DIM * 1e-5
DILATED_BRANCHES = ((128, 1), (512, 4), (2048, 16))
ALIBI_SLOPES = tuple(2.0 ** (-8.0 * (h + 1) / N_HEADS) for h in range(N_HEADS))
ATTN_BLK = 128

ADAM_LR, ADAM_B1, ADAM_B2, ADAM_EPS, ADAM_WD, ADAM_STEP = 0.001, 0.9, 0.999, 1e-08, 0.01, 10

IN_COLS = 3716
PROJ_W = 4096
SEG_HGRN, SEG_RWKV, SEG_Q, SEG_Z, SEG_XBC, SEG_DT = 0, 1024, 2048, 2816, 3072, 3840
_PIECES = ((0, 896, SEG_RWKV), (896, 768, SEG_Q), (1664, 256, SEG_Z), (1920, 768, SEG_XBC),
           (2688, 4, SEG_DT), (2692, 1024, SEG_HGRN))
VRES_COL = SEG_RWKV + 896

HM64 = (N_HEADS, HEAD_DIM)
HM128 = (N_HEADS, SSD_STATE)
ROW_TILE = 256
SCAN_CHUNK = 128
VMEM_LIMIT = 48 * 1024 * 1024


def _cparams(sem=None):
    if sem is None:
        return pltpu.CompilerParams(vmem_limit_bytes=VMEM_LIMIT)
    return pltpu.CompilerParams(dimension_semantics=sem, vmem_limit_bytes=VMEM_LIMIT)


def _pick(n, pref):
    for t in pref:
        if n % t == 0:
            return t
    return n


def _relu2(u):
    r = jnp.maximum(u, 0.0)
    return r * r


def matmul(a, b, mode, name, add=None, a_relu2=False, relu2_grad_of=None):
    pieces = list(a) if isinstance(a, (list, tuple)) else [a]
    a_rows, a_cols = pieces[0].shape[0], sum(p.shape[1] for p in pieces)
    if mode == "nn":
        (M, K), N = (a_rows, a_cols), b.shape[1]
    elif mode == "nt":
        (M, K), N = (a_rows, a_cols), b.shape[0]
    else:
        (K, M), N = (a_rows, a_cols), b.shape[1]
    tm, tn, tk = _pick(M, (1024, 512, 256, 128)), _pick(N, (1024, 512, 256, 128)), _pick(K, (1024, 512, 256, 128))
    nk = K // tk
    dims = {"nn": (((1,), (0,)), ((), ())), "nt": (((1,), (1,)), ((), ())), "tn": (((0,), (0,)), ((), ()))}[mode]
    extras = [e for e in (add, relu2_grad_of) if e is not None]
    npieces = len(pieces)

    def body(*refs):
        b_ref = refs[npieces]
        o_ref = refs[-1]
        rest = list(refs[npieces + 1:-1])
        add_ref = rest.pop(0) if add is not None else None
        u_ref = rest.pop(0) if relu2_grad_of is not None else None
        k = pl.program_id(2)
        av = refs[0][...] if npieces == 1 else jnp.concatenate([r[...] for r in refs[:npieces]], axis=1)
        if a_relu2:
            av = _relu2(av)
        d = lax.dot_general(av.astype(BF16), b_ref[...].astype(BF16), dims, preferred_element_type=F32)

        @pl.when(k == 0)
        def _():
            o_ref[...] = d if add_ref is None else d + add_ref[...]

        if nk > 1:
            @pl.when(k > 0)
            def _():
                o_ref[...] += d

        if u_ref is not None:
            @pl.when(k == nk - 1)
            def _():
                o_ref[...] = o_ref[...] * (2.0 * jnp.maximum(u_ref[...], 0.0))

    if npieces == 1:
        a_specs = [pl.BlockSpec((tk, tm), lambda i, j, k: (k, i)) if mode == "tn" else pl.BlockSpec((tm, tk), lambda i, j, k: (i, k))]
    else:
        assert a_cols == (tm if mode == "tn" else tk)
        rows_of = (lambda i, j, k: (k, 0)) if mode == "tn" else (lambda i, j, k: (i, 0))
        a_specs = [pl.BlockSpec((tk if mode == "tn" else tm, p.shape[1]), rows_of) for p in pieces]
    b_spec = pl.BlockSpec((tn, tk), lambda i, j, k: (j, k)) if mode == "nt" else pl.BlockSpec((tk, tn), lambda i, j, k: (k, j))
    o_spec = pl.BlockSpec((tm, tn), lambda i, j, k: (i, j))
    ins, specs = pieces + [b] + extras, a_specs + [b_spec] + [o_spec] * len(extras)
    return pl.pallas_call(
        body, grid=(M // tm, N // tn, nk), in_specs=specs, out_specs=o_spec,
        out_shape=jax.ShapeDtypeStruct((M, N), F32),
        compiler_params=_cparams(("parallel", "parallel", "arbitrary")), name=name)(*ins)


def _row_spec(w, tile):
    return pl.BlockSpec((tile, w), lambda i: (i, 0))


def _par_spec(shape):
    return pl.BlockSpec(shape, lambda i: (0,) * len(shape))


class Cols:
    def __init__(self, arr, start, width):
        assert start % width == 0 or (start % 128 == 0 and width % 128 == 0)
        self.arr, self.start, self.width = arr, start, width
        self.shape, self.ndim = (arr.shape[0], width), 2


def _arr(r):
    return r.arr if isinstance(r, Cols) else r


def _rows_spec(a, tile):
    if isinstance(a, Cols):
        assert a.start % a.width == 0
        return pl.BlockSpec((tile, a.width), lambda i, blk=a.start // a.width: (i, blk))
    shape = a if isinstance(a, tuple) else a.shape
    if len(shape) == 3:
        return pl.BlockSpec((shape[0], tile, shape[2]), lambda i: (0, i, 0))
    return _row_spec(shape[1], tile)


def _rows_shape(S, w):
    return (w[0], S, w[1]) if isinstance(w, tuple) else (S, w)


def _rows_load(ref):
    if len(ref.shape) == 3:
        return jnp.concatenate([ref[h] for h in range(ref.shape[0])], axis=1)
    return ref[...]


def _rows_store(ref, val):
    if len(ref.shape) == 3:
        w = ref.shape[2]
        for h in range(ref.shape[0]):
            ref[h] = val[:, h * w:(h + 1) * w]
    else:
        ref[...] = val


def tl_fwd(fn, name, rows, pars, out_widths, tile=ROW_TILE):
    S = rows[0].shape[-2]
    nr = len(rows)

    def body(*refs):
        ins = [_rows_load(r) for r in refs[:nr]] + [r[...] for r in refs[nr:nr + len(pars)]]
        outs = fn(*ins)
        for o_ref, o in zip(refs[nr + len(pars):], outs):
            _rows_store(o_ref, o)

    shapes = [_rows_shape(S, w) for w in out_widths]
    return pl.pallas_call(
        body, grid=(S // tile,),
        in_specs=[_rows_spec(r, tile) for r in rows] + [_par_spec(p.shape) for p in pars],
        out_specs=[_rows_spec(s, tile) for s in shapes],
        out_shape=[jax.ShapeDtypeStruct(s, F32) for s in shapes],
        compiler_params=_cparams(("parallel",)), name=name)(*[_arr(r) for r in rows], *pars)


def tl_bwd(fn, name, rows, pars, cts, tile=ROW_TILE, row_grad=None):
    S = rows[0].shape[-2]
    nr, npar = len(rows), len(pars)
    row_grad = [True] * nr if row_grad is None else row_grad
    flat_cts = [c for group in cts for c in group]
    ncts = len(flat_cts)
    gi = [i for i in range(nr) if row_grad[i]]

    def body(*refs):
        row_v = [_rows_load(r) for r in refs[:nr]]
        par_v = [r[...] for r in refs[nr:nr + npar]]
        ct_refs = refs[nr + npar:nr + npar + ncts]
        out_refs = refs[nr + npar + ncts:]
        ct_v, pos = [], 0
        for group in cts:
            acc = _rows_load(ct_refs[pos])
            for q in range(1, len(group)):
                acc = acc + _rows_load(ct_refs[pos + q])
            pos += len(group)
            ct_v.append(acc)

        def f(diff_rows, par_vals):
            full = list(row_v)
            for idx, val in zip(gi, diff_rows):
                full[idx] = val
            return tuple(fn(*full, *par_vals))

        _, vjp = jax.vjp(f, [row_v[i] for i in gi], par_v)
        d_rows, d_pars = vjp(tuple(ct_v))
        for o_ref, g in zip(out_refs[:len(gi)], d_rows):
            _rows_store(o_ref, g)
        first = pl.program_id(0) == 0
        for o_ref, g in zip(out_refs[len(gi):], d_pars):
            @pl.when(first)
            def _(o_ref=o_ref):
                o_ref[...] = jnp.zeros_like(o_ref)
            o_ref[...] += g

    outs = pl.pallas_call(
        body, grid=(S // tile,),
        in_specs=[_rows_spec(r, tile) for r in rows] + [_par_spec(p.shape) for p in pars]
        + [_rows_spec(c, tile) for c in flat_cts],
        out_specs=[_rows_spec(rows[i].shape, tile) for i in gi] + [_par_spec(p.shape) for p in pars],
        out_shape=[jax.ShapeDtypeStruct(rows[i].shape, F32) for i in gi] + [jax.ShapeDtypeStruct(p.shape, F32) for p in pars],
        compiler_params=_cparams(("arbitrary",)), name=name)(*[_arr(r) for r in rows], *pars, *[_arr(c) for c in flat_cts])
    return list(outs[:len(gi)]), list(outs[len(gi):])


def _shift_rows(x, j):
    if j == 0:
        return x
    rolled = pltpu.roll(x, j, 0)
    row = lax.broadcasted_iota(jnp.int32, x.shape, 0)
    return jnp.where(row >= j, rolled, 0.0)


def _unshift_rows(x, j):
    if j == 0:
        return x
    S = x.shape[0]
    rolled = pltpu.roll(x, S - j, 0)
    row = lax.broadcasted_iota(jnp.int32, x.shape, 0)
    return jnp.where(row < S - j, rolled, 0.0)


def _fir_in_spec(x):
    first = x.start // 128 if isinstance(x, Cols) else 0
    return pl.BlockSpec((x.shape[0], 128), lambda j: (0, first + j))


def fir_fwd(x, taps, name):
    S, C = x.shape
    K = taps.shape[0]

    def body(x_ref, w_ref, y_ref):
        xv = x_ref[...]
        acc = jnp.zeros_like(xv)
        for k in range(K):
            acc = acc + _shift_rows(xv, K - 1 - k) * w_ref[pl.ds(k, 1), :]
        y_ref[...] = acc

    cs = pl.BlockSpec((S, 128), lambda j: (0, j))
    return pl.pallas_call(body, grid=(C // 128,), in_specs=[_fir_in_spec(x), pl.BlockSpec((K, 128), lambda j: (0, j))],
                          out_specs=cs, out_shape=jax.ShapeDtypeStruct((S, C), F32),
                          compiler_params=_cparams(("parallel",)), name=name)(_arr(x), taps)


def fir_bwd(x, taps, dy_list, name):
    S, C = x.shape
    K = taps.shape[0]
    n = len(dy_list)

    def body(*refs):
        x_ref, w_ref = refs[:2]
        dy = refs[2][...]
        for q in range(1, n):
            dy = dy + refs[2 + q][...]
        dx_ref, dw_ref, db_ref = refs[2 + n:]
        xv = x_ref[...]
        dx = jnp.zeros_like(xv)
        for k in range(K):
            j = K - 1 - k
            dx = dx + _unshift_rows(dy, j) * w_ref[pl.ds(k, 1), :]
            dw_ref[pl.ds(k, 1), :] = jnp.sum(dy * _shift_rows(xv, j), axis=0, keepdims=True)
        dx_ref[...] = dx
        db_ref[...] = jnp.sum(dy, axis=0, keepdims=True)

    cs = pl.BlockSpec((S, 128), lambda j: (0, j))
    ks = pl.BlockSpec((K, 128), lambda j: (0, j))
    bs = pl.BlockSpec((1, 128), lambda j: (0, j))
    return pl.pallas_call(body, grid=(C // 128,), in_specs=[_fir_in_spec(x), ks] + [cs] * n, out_specs=[cs, ks, bs],
                          out_shape=[jax.ShapeDtypeStruct((S, C), F32), jax.ShapeDtypeStruct((K, C), F32),
                                     jax.ShapeDtypeStruct((1, C), F32)],
                          compiler_params=_cparams(("parallel",)), name=name)(_arr(x), taps, *dy_list)


def _col(tile, lane, t):
    return jnp.sum(jnp.where(lane == t, tile, 0.0), axis=1, keepdims=True)


def _rwkv_step(s, rv, vcol):
    sa = jnp.sum(s * (-rv[3]), axis=1, keepdims=True)
    return s * rv[1] + sa * (rv[3] * rv[4]) + vcol * rv[2], sa


def _eye(n):
    return (lax.broadcasted_iota(jnp.int32, (n, n), 0) == lax.broadcasted_iota(jnp.int32, (n, n), 1)).astype(F32)


def _transposed(x):
    return lax.dot_general(_eye(x.shape[1]), x, (((1,), (1,)), ((), ())), precision=HI, preferred_element_type=F32)


def scan_fwd(r, w, k, v, kk, a, name):
    H, S, Dk = r.shape
    Dv = v.shape[1] // H
    Tc = SCAN_CHUNK
    nc = S // Tc
    rows = [r, w, k, kk, a]

    def body(*refs):
        row_refs = refs[:5]
        v_ref, y_ref, sall_ref, s_ref, vT_ref, yT_ref = refs[5:]

        @pl.when(pl.program_id(0) == 0)
        def _():
            s_ref[...] = jnp.zeros_like(s_ref)

        for h in range(H):
            vT_ref[h] = _transposed(v_ref[:, h * Dv:(h + 1) * Dv])
        yT_ref[...] = jnp.zeros_like(yT_ref)
        lane = lax.broadcasted_iota(jnp.int32, (Dv, Tc), 1)

        def step(t, states):
            new = []
            for h in range(H):
                s = states[h]
                sall_ref[t, h] = s
                rv = [ref[h, pl.ds(t, 1), :] for ref in row_refs]
                s, _ = _rwkv_step(s, rv, _col(vT_ref[h], lane, t))
                ycol = jnp.sum(s * rv[0], axis=1, keepdims=True)
                yT_ref[h] = jnp.where(lane == t, ycol, yT_ref[h])
                new.append(s)
            return tuple(new)

        states = lax.fori_loop(0, Tc, step, tuple(s_ref[h] for h in range(H)))
        for h in range(H):
            s_ref[h] = states[h]
            y_ref[:, h * Dv:(h + 1) * Dv] = _transposed(yT_ref[h])

    rs = pl.BlockSpec((H, Tc, Dk), lambda c: (0, c, 0))
    vs = pl.BlockSpec((Tc, H * Dv), lambda c: (c, 0))
    return pl.pallas_call(
        body, grid=(nc,), in_specs=[rs] * 5 + [vs],
        out_specs=[vs, pl.BlockSpec((Tc, H, Dv, Dk), lambda c: (c, 0, 0, 0))],
        out_shape=[jax.ShapeDtypeStruct((S, H * Dv), F32), jax.ShapeDtypeStruct((S, H, Dv, Dk), F32)],
        scratch_shapes=[pltpu.VMEM((H, Dv, Dk), F32), pltpu.VMEM((H, Dv, Tc), F32), pltpu.VMEM((H, Dv, Tc), F32)],
        compiler_params=_cparams(("arbitrary",)), name=name)(*rows, v)


def scan_bwd(r, w, k, v, kk, a, sall, dy, name):
    H, S, Dk = r.shape
    Dv = v.shape[1] // H
    Tc = SCAN_CHUNK
    nc = S // Tc
    rows = [r, w, k, kk, a]

    def body(*refs):
        row_refs = refs[:5]
        v_ref, dy_ref, sall_ref = refs[5:8]
        drow_refs = refs[8:13]
        dv_ref, ds_ref, vT_ref, dyT_ref, dvT_ref = refs[13:]

        @pl.when(pl.program_id(0) == 0)
        def _():
            ds_ref[...] = jnp.zeros_like(ds_ref)

        for h in range(H):
            vT_ref[h] = _transposed(v_ref[:, h * Dv:(h + 1) * Dv])
            dyT_ref[h] = _transposed(dy_ref[:, h * Dv:(h + 1) * Dv])
        dvT_ref[...] = jnp.zeros_like(dvT_ref)
        lane = lax.broadcasted_iota(jnp.int32, (Dv, Tc), 1)

        def bstep(i, carry):
            t = Tc - 1 - i
            new = []
            for h in range(H):
                ds = carry[h]
                sp = sall_ref[t, h]
                rv = [ref[h, pl.ds(t, 1), :] for ref in row_refs]
                vcol = _col(vT_ref[h], lane, t)
                dycol = _col(dyT_ref[h], lane, t)
                st, sa = _rwkv_step(sp, rv, vcol)
                drow_refs[0][h, pl.ds(t, 1), :] = jnp.sum(st * dycol, axis=0, keepdims=True)
                g = ds + dycol * rv[0]
                drow_refs[1][h, pl.ds(t, 1), :] = jnp.sum(g * sp, axis=0, keepdims=True)
                drow_refs[2][h, pl.ds(t, 1), :] = jnp.sum(g * vcol, axis=0, keepdims=True)
                dvcol = jnp.sum(g * rv[2], axis=1, keepdims=True)
                dsa = jnp.sum(g * (rv[3] * rv[4]), axis=1, keepdims=True)
                db = jnp.sum(g * sa, axis=0, keepdims=True)
                dnkk = jnp.sum(sp * dsa, axis=0, keepdims=True)
                drow_refs[3][h, pl.ds(t, 1), :] = db * rv[4] - dnkk
                drow_refs[4][h, pl.ds(t, 1), :] = db * rv[3]
                dvT_ref[h] = jnp.where(lane == t, dvcol, dvT_ref[h])
                new.append(g * rv[1] - dsa * rv[3])
            return tuple(new)

        carry = lax.fori_loop(0, Tc, bstep, tuple(ds_ref[h] for h in range(H)))
        for h in range(H):
            ds_ref[h] = carry[h]
            dv_ref[:, h * Dv:(h + 1) * Dv] = _transposed(dvT_ref[h])

    rs = pl.BlockSpec((H, Tc, Dk), lambda c: (0, nc - 1 - c, 0))
    vs = pl.BlockSpec((Tc, H * Dv), lambda c: (nc - 1 - c, 0))
    tile = pltpu.VMEM((H, Dv, Tc), F32)
    outs = pl.pallas_call(
        body, grid=(nc,),
        in_specs=[rs] * 5 + [vs, vs, pl.BlockSpec((Tc, H, Dv, Dk), lambda c: (nc - 1 - c, 0, 0, 0))],
        out_specs=[rs] * 5 + [vs],
        out_shape=[jax.ShapeDtypeStruct((H, S, Dk), F32)] * 5 + [jax.ShapeDtypeStruct((S, H * Dv), F32)],
        scratch_shapes=[pltpu.VMEM((H, Dv, Dk), F32), tile, tile, tile],
        compiler_params=_cparams(("arbitrary",)), name=name)(*rows, v, dy, sall)
    return list(outs[:5]), outs[5]


CHUNK = 128
SSD_GROUP = 2
HGRN_GROUP = 4


def chunk_fwd(fn, name, blocks, state_shape, out_width, group):
    H, S, _ = blocks[0].shape
    nc = S // CHUNK
    nb = len(blocks)

    def body(*refs):
        o_ref, sv_ref, st = refs[nb:]

        @pl.when(pl.program_id(1) == 0)
        def _():
            st[...] = jnp.zeros_like(st)

        for g in range(group):
            s0 = st[g]
            sv_ref[g, 0] = s0
            s1, out = fn(s0, *[r[g] for r in refs[:nb]])
            st[g] = s1
            o_ref[g] = out

    spec = lambda w: pl.BlockSpec((group, CHUNK, w), lambda h, c: (h, c, 0))
    return pl.pallas_call(
        body, grid=(H // group, nc), in_specs=[spec(b.shape[2]) for b in blocks],
        out_specs=[spec(out_width), pl.BlockSpec((group, 1) + state_shape, lambda h, c: (h, c, 0, 0))],
        out_shape=[jax.ShapeDtypeStruct((H, S, out_width), F32), jax.ShapeDtypeStruct((H, nc) + state_shape, F32)],
        scratch_shapes=[pltpu.VMEM((group,) + state_shape, F32)],
        compiler_params=_cparams(("parallel", "arbitrary")), name=name)(*blocks)


def chunk_bwd(fn, name, blocks, states, dout, group):
    H, S, _ = blocks[0].shape
    nc = S // CHUNK
    nb = len(blocks)
    state_shape = states.shape[2:]

    def body(*refs):
        sv_ref, do_ref = refs[nb], refs[nb + 1]
        d_refs = refs[nb + 2:2 * nb + 2]
        dst = refs[2 * nb + 2]

        @pl.when(pl.program_id(1) == 0)
        def _():
            dst[...] = jnp.zeros_like(dst)

        for g in range(group):
            _, vjp = jax.vjp(fn, sv_ref[g, 0], *[r[g] for r in refs[:nb]])
            grads = vjp((dst[g], do_ref[g]))
            dst[g] = grads[0]
            for d_ref, gr in zip(d_refs, grads[1:]):
                d_ref[g] = gr

    spec = lambda w: pl.BlockSpec((group, CHUNK, w), lambda h, c: (h, nc - 1 - c, 0))
    return pl.pallas_call(
        body, grid=(H // group, nc),
        in_specs=[spec(b.shape[2]) for b in blocks]
        + [pl.BlockSpec((group, 1) + state_shape, lambda h, c: (h, nc - 1 - c, 0, 0)), spec(dout.shape[2])],
        out_specs=[spec(b.shape[2]) for b in blocks],
        out_shape=[jax.ShapeDtypeStruct(b.shape, F32) for b in blocks],
        scratch_shapes=[pltpu.VMEM((group,) + state_shape, F32)],
        compiler_params=_cparams(("parallel", "arbitrary")), name=name)(*blocks, states, dout)


def _bdot(a, b, dims):
    return lax.dot_general(a.astype(BF16), b.astype(BF16), (dims, ((), ())), preferred_element_type=F32)


def ssd_chunk(state, cb, bb, da, xdt):
    T = cb.shape[0]
    ti = lax.broadcasted_iota(jnp.int32, (T, T), 0)
    si = lax.broadcasted_iota(jnp.int32, (T, T), 1)
    mask = ti >= si
    cs = jnp.dot(mask.astype(F32), da, precision=HI, preferred_element_type=F32)
    pick = (lax.broadcasted_iota(jnp.int32, cs.shape, 1) == 0).astype(F32)
    cs_row = lax.dot_general(pick, cs, (((1,), (1,)), ((), ())), precision=HI, preferred_element_type=F32)
    lmat = jnp.where(mask, jnp.exp(jnp.where(mask, cs - cs_row, 0.0)), 0.0)
    scores = _bdot(cb, bb, ((1,), (1,))) * lmat
    y = _bdot(scores, xdt, ((1,), (0,))) + _bdot(cb, state, ((1,), (1,))) * jnp.exp(cs[:, :HEAD_DIM])
    last = cs[T - 1:T, :]
    new_state = state * jnp.exp(last) + _bdot(xdt, bb * jnp.exp(last - cs), ((0,), (0,)))
    return new_state, y


HGRN_SUB = 16


def hgrn_chunk(state, q, k, lf, v):
    T, C = q.shape[0], HGRN_SUB
    ti = lax.broadcasted_iota(jnp.int32, (C, C), 0)
    si = lax.broadcasted_iota(jnp.int32, (C, C), 1)
    tril = (ti >= si).astype(F32)
    row = lax.broadcasted_iota(jnp.int32, (C, q.shape[1]), 0)
    outs = []
    for j in range(T // C):
        qj, kj, lj, vj = (a[j * C:(j + 1) * C] for a in (q, k, lf, v))
        b = jnp.dot(tril, lj, precision=HI, preferred_element_type=F32)
        o = _bdot(qj * jnp.exp(b), state, ((1,), (1,)))
        for s in range(C):
            m = row >= s
            e = jnp.where(m, jnp.exp(jnp.where(m, b - b[s:s + 1], 0.0)), 0.0)
            o = o + jnp.sum(qj * kj[s:s + 1] * e, axis=1, keepdims=True) * vj[s:s + 1]
        last = b[C - 1:C]
        state = state * jnp.exp(last) + _bdot(vj, kj * jnp.exp(last - b), ((0,), (0,)))
        outs.append(o)
    return state, jnp.concatenate(outs, axis=0)


def _attn_block(q, kp, kc, vp, vc, n, slope, dilation):
    blk = ATTN_BLK
    k2 = jnp.concatenate([kp, kc], axis=0)
    v2 = jnp.concatenate([vp, vc], axis=0)
    s = _bdot(q, k2, ((1,), (1,))) * (HEAD_DIM ** -0.5)
    i = lax.broadcasted_iota(jnp.int32, (blk, 2 * blk), 0)
    j = lax.broadcasted_iota(jnp.int32, (blk, 2 * blk), 1)
    dist = blk + i - j
    first_key = jnp.where(n > 0, 0, blk)
    valid = (dist >= 0) & (dist <= blk) & (j >= first_key)
    s = s - slope * (dist * dilation).astype(F32)
    s = jnp.where(valid, s, -1e30)
    m = jnp.max(s, axis=-1, keepdims=True)
    p = jnp.exp(s - m)
    l = jnp.sum(p, axis=-1, keepdims=True)
    o = _bdot(p, v2, ((1,), (0,))) / l
    lse = jnp.broadcast_to(m + jnp.log(l), o.shape)
    return o, lse


_QCOL = SEG_Q // 128
PAIR = 2 * HEAD_DIM


def _attn_specs(rows):
    cur = lambda j: pl.BlockSpec((rows, PAIR), lambda p, n: (n, j + p))
    prev = lambda j: pl.BlockSpec((rows, PAIR), lambda p, n: (jnp.maximum(n - 1, 0), j + p))
    return cur, prev


def _pair_slope(pair, h):
    return jnp.where(pair == 0, jnp.float32(ALIBI_SLOPES[h]), jnp.float32(ALIBI_SLOPES[2 + h]))


def _halves(t):
    return [t[:, h * HEAD_DIM:(h + 1) * HEAD_DIM] for h in range(2)]


def _for_classes(dilation, step):
    if dilation == 1:
        step(0)
    else:
        lax.fori_loop(0, dilation, lambda z, c: (step(z), c)[1], 0)


def attn_fwd(proj, dilation, name):
    S = proj.shape[0]
    blk = ATTN_BLK
    rows = blk * dilation
    cur, prev = _attn_specs(rows)

    def body(q_ref, kp_ref, kc_ref, vp_ref, vc_ref, o_ref, l_ref):
        pair, n = pl.program_id(0), pl.program_id(1)

        def one_class(z):
            sel = pl.ds(z, blk, stride=dilation) if dilation > 1 else pl.ds(0, blk)
            q, kp, kc, vp, vc = (_halves(r[sel, :]) for r in (q_ref, kp_ref, kc_ref, vp_ref, vc_ref))
            res = [_attn_block(q[h], kp[h], kc[h], vp[h], vc[h], n, _pair_slope(pair, h), dilation) for h in range(2)]
            o_ref[sel, :] = jnp.concatenate([r[0] for r in res], axis=1)
            l_ref[sel, :] = jnp.concatenate([r[1] for r in res], axis=1)

        _for_classes(dilation, one_class)

    return pl.pallas_call(
        body, grid=(2, S // rows),
        in_specs=[cur(_QCOL), prev(_QCOL + 2), cur(_QCOL + 2), prev(_QCOL + 4), cur(_QCOL + 4)],
        out_specs=[cur(0), cur(0)], out_shape=[jax.ShapeDtypeStruct((S, D_GROUP), F32)] * 2,
        compiler_params=_cparams(("parallel", "arbitrary")), name=name)(proj, proj, proj, proj, proj)


def attn_bwd(proj, do, dlse, dilation, name):
    S = proj.shape[0]
    blk = ATTN_BLK
    rows = blk * dilation
    cur, prev = _attn_specs(rows)
    full = pl.BlockSpec((S, PAIR), lambda p, n: (0, p))

    def body(q_ref, kp_ref, kc_ref, vp_ref, vc_ref, do_ref, dl_ref, dq_ref, dk_ref, dv_ref):
        pair, n = pl.program_id(0), pl.program_id(1)

        @pl.when(n == 0)
        def _():
            dk_ref[...] = jnp.zeros_like(dk_ref)
            dv_ref[...] = jnp.zeros_like(dv_ref)

        def one_class(z):
            sel = pl.ds(z, blk, stride=dilation) if dilation > 1 else pl.ds(0, blk)
            q, kp, kc, vp, vc, do_v, dl_v = (_halves(r[sel, :]) for r in
                                             (q_ref, kp_ref, kc_ref, vp_ref, vc_ref, do_ref, dl_ref))
            grads = []
            for h in range(2):
                f = lambda q_, kp_, kc_, vp_, vc_, h=h: _attn_block(q_, kp_, kc_, vp_, vc_, n, _pair_slope(pair, h), dilation)
                _, vjp = jax.vjp(f, q[h], kp[h], kc[h], vp[h], vc[h])
                grads.append(vjp((do_v[h], dl_v[h])))
            both = lambda j: jnp.concatenate([grads[0][j], grads[1][j]], axis=1)
            dq_ref[sel, :] = both(0)
            if dilation > 1:
                here = pl.ds(n * rows + z, blk, stride=dilation)
                before = pl.ds(jnp.maximum(n - 1, 0) * rows + z, blk, stride=dilation)
            else:
                here = pl.ds(pl.multiple_of(n * blk, blk), blk)
                before = pl.ds(pl.multiple_of(jnp.maximum(n - 1, 0) * blk, blk), blk)
            dk_ref[here, :] = dk_ref[here, :] + both(2)
            dv_ref[here, :] = dv_ref[here, :] + both(4)
            dk_ref[before, :] = dk_ref[before, :] + both(1)
            dv_ref[before, :] = dv_ref[before, :] + both(3)

        _for_classes(dilation, one_class)

    return pl.pallas_call(
        body, grid=(2, S // rows),
        in_specs=[cur(_QCOL), prev(_QCOL + 2), cur(_QCOL + 2), prev(_QCOL + 4), cur(_QCOL + 4), cur(0), cur(0)],
        out_specs=[cur(0), full, full], out_shape=[jax.ShapeDtypeStruct((S, D_GROUP), F32)] * 3,
        compiler_params=_cparams(("parallel", "arbitrary")), name=name)(proj, proj, proj, proj, proj, do, dlse)


def _head_ones(width, group):
    i = lax.broadcasted_iota(jnp.int32, (width, width), 0) // group
    j = lax.broadcasted_iota(jnp.int32, (width, width), 1) // group
    return (i == j).astype(F32)


def _group_sum(x, group):
    return jnp.dot(x, _head_ones(x.shape[1], group), precision=HI, preferred_element_type=F32)


def _spread(width_in, width_out, rep):
    i = lax.broadcasted_iota(jnp.int32, (width_in, width_out), 0)
    j = lax.broadcasted_iota(jnp.int32, (width_in, width_out), 1) // rep
    return (i == j).astype(F32)


def _hdot(a, b):
    return jnp.dot(a, b, precision=HI, preferred_element_type=F32)


def _sigmoid(x):
    return 1.0 / (1.0 + jnp.exp(-x))


def _softplus(x):
    return jnp.maximum(x, 0.0) + jnp.log(1.0 + jnp.exp(jnp.minimum(x, -x)))


def _silu(x):
    return x * _sigmoid(x)


def rwkv_pre(layer):
    def fn(*args):
        if layer == 0:
            fs, w0, w2p, a0, a2p, g2p, k_k, k_a = args
        else:
            fs, vfirst, w0, w2p, a0, a2p, g2p, k_k, k_a, v0, v2p = args
        r, k, v = fs[:, 0:256], fs[:, 256:512], fs[:, 512:768]
        lora = fs[:, 768:896]
        w_log = -_softplus(-(w0 + _hdot(jnp.tanh(lora), w2p))) - 0.5
        decay = jnp.exp(-jnp.exp(w_log))
        a = _sigmoid(a0 + _hdot(lora, a2p))
        g = _hdot(_sigmoid(lora), g2p)
        if layer > 0:
            v = v + (vfirst - v) * _sigmoid(v0 + _hdot(fs[:, 896:1024], v2p))
        kk = k * k_k
        kk = kk / jnp.maximum(jnp.sqrt(_group_sum(kk * kk, HEAD_DIM)), 1e-12)
        k = k * (1.0 + (a - 1.0) * k_a)
        return r, decay, k, v, kk, a, g
    return fn


def rwkv_post(y, r, k, v, g, lnx_w, lnx_b, r_k):
    mu = _group_sum(y, HEAD_DIM) * (1.0 / HEAD_DIM)
    yc = y - mu
    var = _group_sum(yc * yc, HEAD_DIM) * (1.0 / HEAD_DIM)
    yn = yc * lax.rsqrt(var + RWKV_GN_EPS) * lnx_w + lnx_b
    bonus = _group_sum(r * k * r_k, HEAD_DIM) * v
    return ((yn + bonus) * g,)


def attn_combine(o1, o2, o3, l1, l2, l3):
    m = jnp.maximum(jnp.maximum(l1, l2), l3)
    e1, e2, e3 = jnp.exp(l1 - m), jnp.exp(l2 - m), jnp.exp(l3 - m)
    return ((o1 * e1 + o2 * e2 + o3 * e3) / (e1 + e2 + e3),)


def ssd_pre(xc, dtr, conv_b, dt_bias, a_log):
    xbc = _silu(xc + conv_b)
    xs, bm, cm = xbc[:, 0:256], xbc[:, 256:512], xbc[:, 512:768]
    dt = _softplus(dtr + dt_bias)
    a_neg = -jnp.exp(a_log)
    wide = _spread(128, N_HEADS * SSD_STATE, SSD_STATE)
    w = _hdot(dt, wide) * _hdot(a_neg, wide)
    xdt = xs * _hdot(dt, _spread(128, D_GROUP, HEAD_DIM))
    rr = jnp.concatenate([cm[:, 0:128], cm[:, 0:128], cm[:, 128:256], cm[:, 128:256]], axis=1)
    kk = jnp.concatenate([bm[:, 0:128], bm[:, 0:128], bm[:, 128:256], bm[:, 128:256]], axis=1)
    return rr, w, kk, xdt, xs


def ssd_post(ys, z, xs, d_skip, norm_w):
    y = ys + xs * _hdot(d_skip, _spread(128, D_GROUP, HEAD_DIM))
    y = y * _silu(z)
    half = D_GROUP // 2
    parts = []
    for g in range(2):
        t = y[:, g * half:(g + 1) * half]
        parts.append(t * lax.rsqrt(jnp.mean(t * t, axis=-1, keepdims=True) + RMS_EPS))
    return (jnp.concatenate(parts, axis=1) * norm_w,)


def hgrn_pre(seg, lb):
    q, f, i = seg[:, 0:256], seg[:, 256:512], seg[:, 512:768]
    forget = lb + (1.0 - lb) * _sigmoid(f)
    return _silu(q), 1.0 - forget, jnp.log(forget), i


def hgrn_post(o, seg, norm_w):
    g = seg[:, 768:1024]
    ms = _group_sum(o * o, HEAD_DIM) * (1.0 / HEAD_DIM)
    return (o * lax.rsqrt(ms + RMS_EPS) * norm_w * _silu(g),)


def ln_res(x, y, w, b):
    z = ALPHA * x + y
    mu = jnp.mean(z, axis=-1, keepdims=True)
    zc = z - mu
    var = jnp.mean(zc * zc, axis=-1, keepdims=True)
    return (zc * lax.rsqrt(var + LN_EPS) * w + b,)


def loss_and_grad(y, tgt, name):
    S, D = y.shape
    tile = ROW_TILE

    def body(y_ref, t_ref, l_ref, dy_ref):
        e = y_ref[...] - t_ref[...]
        dy_ref[...] = e * (1.0 / D)

        @pl.when(pl.program_id(0) == 0)
        def _():
            l_ref[...] = jnp.zeros_like(l_ref)

        per_row = 0.5 * jnp.mean(e * e, axis=-1, keepdims=True)
        l_ref[...] += jnp.sum(per_row, axis=0, keepdims=True) * jnp.ones((1, 128), F32)

    return pl.pallas_call(body, grid=(S // tile,), in_specs=[_row_spec(D, tile)] * 2,
                          out_specs=[_par_spec((1, 128)), _row_spec(D, tile)],
                          out_shape=[jax.ShapeDtypeStruct((1, 128), F32), jax.ShapeDtypeStruct((S, D), F32)],
                          compiler_params=_cparams(("arbitrary",)), name=name)(y, tgt)


def add_rows(arrs, name):
    (out,) = tl_fwd(lambda *a: (functools.reduce(lambda p, q: p + q, a),), name, arrs, [], [arrs[0].shape[1]])
    return out


def small_fwd(fn, name, ins, out_shapes):
    n = len(ins)

    def body(*refs):
        outs = fn(*[r[...] for r in refs[:n]])
        for o_ref, o in zip(refs[n:], outs):
            o_ref[...] = o

    return pl.pallas_call(body, out_shape=[jax.ShapeDtypeStruct(s, F32) for s in out_shapes], name=name)(*ins)


def small_bwd(fn, name, ins, cts):
    n, m = len(ins), len(cts)

    def body(*refs):
        _, vjp = jax.vjp(lambda *a: tuple(fn(*a)), *[r[...] for r in refs[:n]])
        grads = vjp(tuple(r[...] for r in refs[n:n + m]))
        for o_ref, g in zip(refs[n + m:], grads):
            o_ref[...] = g

    return pl.pallas_call(body, out_shape=[jax.ShapeDtypeStruct(a.shape, F32) for a in ins], name=name)(*ins, *cts)


def param_prep(lower_bounds, mu0, mu1):
    e = jnp.exp(lower_bounds - jnp.max(lower_bounds, axis=0, keepdims=True))
    sm = e / jnp.sum(e, axis=0, keepdims=True)
    lb0 = sm[0:1] - sm[0:1]
    lb1 = sm[0:1] + sm[1:2] - sm[0:1]
    return lb0, lb1, mu0, 1.0 - mu0, mu1, 1.0 - mu1


def _rows_tile(rows):
    return _pick(rows, (256, 128, 64, 32, 16, 8))


def sum_parts(parts, name):
    P, rows, cols = parts.shape
    tile = _rows_tile(rows)

    def body(p_ref, o_ref):
        acc = p_ref[0]
        for p in range(1, P):
            acc = acc + p_ref[p]
        o_ref[...] = acc

    return pl.pallas_call(body, grid=(rows // tile,), in_specs=[pl.BlockSpec((P, tile, cols), lambda i: (0, i, 0))],
                          out_specs=pl.BlockSpec((tile, cols), lambda i: (i, 0)),
                          out_shape=jax.ShapeDtypeStruct((rows, cols), F32),
                          compiler_params=_cparams(("parallel",)), name=name)(parts)


def pair_sum(own, got, ids, name):
    _, P, rows, cols = own.shape
    tile = _rows_tile(rows)

    def body(ids_ref, own_ref, got_ref, o_ref):
        o_ref[0] = (own_ref[0, 0] + got_ref[0]).astype(BF16)

    grid_spec = pltpu.PrefetchScalarGridSpec(
        num_scalar_prefetch=1, grid=(P, rows // tile),
        in_specs=[pl.BlockSpec((1, 1, tile, cols), lambda s, i, ids: (ids[0], s, i, 0)),
                  pl.BlockSpec((1, tile, cols), lambda s, i, ids: (s, i, 0))],
        out_specs=pl.BlockSpec((1, tile, cols), lambda s, i, ids: (s, i, 0)))
    return pl.pallas_call(body, grid_spec=grid_spec, out_shape=jax.ShapeDtypeStruct((P, rows, cols), BF16),
                          compiler_params=_cparams(("parallel", "parallel")), name=name)(ids, own, got)


def sum_chips(slots, own, got, ids, name):
    P, rows, cols = slots.shape
    tile = _rows_tile(rows)

    def body(ids_ref, s_ref, own_ref, got_ref, o_ref):
        chip = ids_ref[1]
        mine = own_ref[0, 0] + got_ref[0]
        acc = None
        for p in range(P):
            term = jnp.where(chip == p, mine, s_ref[p].astype(F32))
            acc = term if acc is None else acc + term
        o_ref[0] = acc

    grid_spec = pltpu.PrefetchScalarGridSpec(
        num_scalar_prefetch=1, grid=(rows // tile,),
        in_specs=[pl.BlockSpec((P, tile, cols), lambda i, ids: (0, i, 0)),
                  pl.BlockSpec((1, 1, tile, cols), lambda i, ids: (ids[0], ids[1], i, 0)),
                  pl.BlockSpec((1, tile, cols), lambda i, ids: (ids[1], i, 0))],
        out_specs=pl.BlockSpec((1, tile, cols), lambda i, ids: (ids[0], i, 0)))
    return pl.pallas_call(body, grid_spec=grid_spec, out_shape=jax.ShapeDtypeStruct((2, rows, cols), F32),
                          compiler_params=_cparams(("parallel",)), name=name)(ids, slots, own, got)


def adamw(w, g, m, v, name):
    layers, rows, cols = w.shape
    tile = _rows_tile(rows)

    def body(w_ref, g_ref, m_ref, v_ref, d_ref, nm_ref, nv_ref):
        gv = g_ref[...]
        nm = ADAM_B1 * m_ref[...] + (1.0 - ADAM_B1) * gv
        nv = ADAM_B2 * v_ref[...] + (1.0 - ADAM_B2) * jnp.square(gv)
        m_hat = nm / (1.0 - ADAM_B1 ** ADAM_STEP)
        v_hat = nv / (1.0 - ADAM_B2 ** ADAM_STEP)
        d_ref[...] = -ADAM_LR * (m_hat / (jnp.sqrt(v_hat) + ADAM_EPS) + ADAM_WD * w_ref[...])
        nm_ref[...] = nm
        nv_ref[...] = nv

    spec = pl.BlockSpec((None, tile, cols), lambda l, i: (l, i, 0))
    return pl.pallas_call(body, grid=(layers, rows // tile), in_specs=[spec] * 4, out_specs=[spec] * 3,
                          out_shape=[jax.ShapeDtypeStruct((layers, rows, cols), F32)] * 3,
                          compiler_params=_cparams(("parallel", "parallel")), name=name)(w, g, m, v)


MESH = pl.DeviceIdType.MESH
ANY = pl.BlockSpec(memory_space=pl.ANY)


def _flip(v, bit):
    return 1 - v if bit else v


_CHIP_RELATIONS = ((1, 0), (0, 1), (1, 1))


def gather_chips(arrs, small, name):
    n = len(arrs)

    def body(*refs):
        ins, small_in = refs[:n], refs[n]
        outs, small_out = refs[n + 1:2 * n + 1], refs[2 * n + 1]
        send, recv, fsend, frecv, ssend, srecv = refs[2 * n + 2:]
        x, y, c = lax.axis_index("x"), lax.axis_index("y"), lax.axis_index("c")
        me = 2 * x + y
        chips = [(_flip(x, bx), _flip(y, by)) for bx, by in _CHIP_RELATIONS]

        def over_ici(i, r, block_chip):
            return pltpu.make_async_remote_copy(src_ref=ins[i].at[c], dst_ref=outs[i].at[block_chip, c],
                                                send_sem=send.at[i, r], recv_sem=recv.at[i, r],
                                                device_id=(chips[r][0], chips[r][1], c), device_id_type=MESH)

        def to_sibling(i, r, layer):
            blk = outs[i].at[2 * chips[r][0] + chips[r][1], layer]
            return pltpu.make_async_remote_copy(src_ref=blk, dst_ref=blk, send_sem=fsend.at[i, r],
                                                recv_sem=frecv.at[i, r], device_id=(x, y, 1 - c), device_id_type=MESH)

        first = [over_ici(i, r, me) for i in range(n) for r in range(3)]
        smalls = [pltpu.make_async_remote_copy(src_ref=small_in, dst_ref=small_out.at[me], send_sem=ssend.at[r],
                                               recv_sem=srecv.at[r], device_id=(chips[r][0], chips[r][1], c),
                                               device_id_type=MESH) for r in range(3)]
        for cp in first + smalls:
            cp.start()
        passed = []
        for r in range(3):
            for i in range(n):
                over_ici(i, r, 2 * chips[r][0] + chips[r][1]).wait_recv()
                fw = to_sibling(i, r, c)
                fw.start()
                passed.append(fw)
        for r in range(3):
            for i in range(n):
                to_sibling(i, r, 1 - c).wait_recv()
        for cp in first + passed:
            cp.wait_send()
        for cp in smalls:
            cp.wait()

    return pl.pallas_call(
        body, in_specs=[ANY] * (n + 1), out_specs=[ANY] * (n + 1),
        out_shape=[jax.ShapeDtypeStruct((4,) + a.shape, a.dtype) for a in arrs]
        + [jax.ShapeDtypeStruct((4,) + small.shape, small.dtype)],
        scratch_shapes=[pltpu.SemaphoreType.DMA((n, 3)), pltpu.SemaphoreType.DMA((n, 3)), pltpu.SemaphoreType.DMA((n, 3)),
                        pltpu.SemaphoreType.DMA((n, 3)), pltpu.SemaphoreType.DMA((3,)), pltpu.SemaphoreType.DMA((3,))],
        name=name)(*arrs, small)


_RELATIONS = tuple((r >> 2 & 1, r >> 1 & 1, r & 1) for r in range(1, 8))


def gather_devices(arr, name):
    def body(in_ref, out_ref, send, recv, loc):
        x, y, c = lax.axis_index("x"), lax.axis_index("y"), lax.axis_index("c")
        me = 4 * x + 2 * y + c
        lc = pltpu.make_async_copy(in_ref, out_ref.at[me], loc)
        lc.start()
        pending = [lc]
        for r, (bx, by, bc) in enumerate(_RELATIONS):
            cp = pltpu.make_async_remote_copy(src_ref=in_ref, dst_ref=out_ref.at[me], send_sem=send.at[r],
                                              recv_sem=recv.at[r], device_id=(_flip(x, bx), _flip(y, by), _flip(c, bc)),
                                              device_id_type=MESH)
            cp.start()
            pending.append(cp)
        for cp in pending:
            cp.wait()

    return pl.pallas_call(
        body, in_specs=[ANY], out_specs=ANY, out_shape=jax.ShapeDtypeStruct((8,) + arr.shape, arr.dtype),
        scratch_shapes=[pltpu.SemaphoreType.DMA((7,)), pltpu.SemaphoreType.DMA((7,)), pltpu.SemaphoreType.DMA(())],
        name=name)(arr)


def pair_exchange(arrs, name):
    n = len(arrs)

    def body(*refs):
        ins, outs = refs[:n], refs[n:2 * n]
        send, recv = refs[2 * n:]
        x, y, c = lax.axis_index("x"), lax.axis_index("y"), lax.axis_index("c")
        pending = []
        for i in range(n):
            for s in range(4):
                cp = pltpu.make_async_remote_copy(src_ref=ins[i].at[1 - c, s], dst_ref=outs[i].at[s],
                                                  send_sem=send.at[i, s], recv_sem=recv.at[i, s],
                                                  device_id=(x, y, 1 - c), device_id_type=MESH)
                cp.start()
                pending.append(cp)
        for cp in pending:
            cp.wait()

    return pl.pallas_call(
        body, in_specs=[ANY] * n, out_specs=[ANY] * n,
        out_shape=[jax.ShapeDtypeStruct(a.shape[1:], a.dtype) for a in arrs],
        scratch_shapes=[pltpu.SemaphoreType.DMA((n, 4)), pltpu.SemaphoreType.DMA((n, 4))],
        name=name)(*arrs)


def reduce_chips(arrs, name):
    n = len(arrs)

    def body(*refs):
        ins, outs = refs[:n], refs[n:2 * n]
        send, recv, loc = refs[2 * n:]
        x, y, c = lax.axis_index("x"), lax.axis_index("y"), lax.axis_index("c")
        me = 2 * x + y
        pending = []
        for i in range(n):
            for r, (bx, by) in enumerate(_CHIP_RELATIONS):
                px, py = _flip(x, bx), _flip(y, by)
                cp = pltpu.make_async_remote_copy(src_ref=ins[i].at[2 * px + py], dst_ref=outs[i].at[me],
                                                  send_sem=send.at[i, r], recv_sem=recv.at[i, r],
                                                  device_id=(px, py, c), device_id_type=MESH)
                cp.start()
                pending.append(cp)
        for i in range(n):
            lc = pltpu.make_async_copy(ins[i].at[me], outs[i].at[me], loc.at[i])
            lc.start()
            pending.append(lc)
        for cp in pending:
            cp.wait()

    return pl.pallas_call(
        body, in_specs=[ANY] * n, out_specs=[ANY] * n,
        out_shape=[jax.ShapeDtypeStruct(a.shape, a.dtype) for a in arrs],
        scratch_shapes=[pltpu.SemaphoreType.DMA((n, 3)), pltpu.SemaphoreType.DMA((n, 3)), pltpu.SemaphoreType.DMA((n,))],
        name=name)(*arrs)


EXCHANGE_PIECES = 8


def sibling_exchange(arrs, name):
    n = len(arrs)

    def body(*refs):
        bufs = refs[n:2 * n]
        send, recv = refs[2 * n:]
        x, y, c = lax.axis_index("x"), lax.axis_index("y"), lax.axis_index("c")
        pending = []
        for i in range(n):
            rows = bufs[i].shape[1] // EXCHANGE_PIECES
            for j in range(EXCHANGE_PIECES):
                piece = bufs[i].at[c, pl.ds(j * rows, rows)]
                cp = pltpu.make_async_remote_copy(src_ref=piece, dst_ref=piece, send_sem=send.at[i, j],
                                                  recv_sem=recv.at[i, j], device_id=(x, y, 1 - c), device_id_type=MESH)
                cp.start()
                pending.append(cp)
        for i in range(n):
            rows = bufs[i].shape[1] // EXCHANGE_PIECES
            for j in range(EXCHANGE_PIECES):
                landed = bufs[i].at[1 - c, pl.ds(j * rows, rows)]
                pltpu.make_async_remote_copy(src_ref=landed, dst_ref=landed, send_sem=send.at[i, j], recv_sem=recv.at[i, j],
                                             device_id=(x, y, 1 - c), device_id_type=MESH).wait_recv()
        for cp in pending:
            cp.wait_send()

    return pl.pallas_call(
        body, in_specs=[ANY] * n, out_specs=[ANY] * n,
        out_shape=[jax.ShapeDtypeStruct(a.shape, a.dtype) for a in arrs], input_output_aliases={i: i for i in range(n)},
        scratch_shapes=[pltpu.SemaphoreType.DMA((n, EXCHANGE_PIECES)), pltpu.SemaphoreType.DMA((n, EXCHANGE_PIECES))],
        name=name)(*arrs)


def rwkv_fwd(l, seg, taps, pars, vfirst):
    fs = fir_fwd(seg, taps, f"rwkv_shift_fwd{l}")
    rows = [fs] + ([vfirst] if l else [])
    R, W, K, V, KK, A, G = tl_fwd(rwkv_pre(l), f"rwkv_pre_fwd{l}", rows, pars["pre"],
                                  [HM64, HM64, HM64, D_GROUP, HM64, HM64, D_GROUP])
    Y, sall = scan_fwd(R, W, K, V, KK, A, f"rwkv_scan_fwd{l}")
    (out,) = tl_fwd(rwkv_post, f"rwkv_post_fwd{l}", [Y, R, K, V, G], pars["post"], [D_GROUP])
    return out, V, (seg, taps, rows, R, W, K, V, KK, A, G, Y, sall)


def rwkv_bwd(l, saved, pars, dout, dv_extra):
    seg, taps, rows, R, W, K, V, KK, A, G, Y, sall = saved
    (dY, dR1, dK1, dV1, dG), dpost = tl_bwd(rwkv_post, f"rwkv_post_bwd{l}", [Y, R, K, V, G], pars["post"], [[dout]])
    (dR2, dW, dK2, dKK, dA), dV2 = scan_bwd(R, W, K, V, KK, A, sall, dY, f"rwkv_scan_bwd{l}")
    cts = [[dR1, dR2], [dW], [dK1, dK2], [dV1, dV2] + dv_extra, [dKK], [dA], [dG]]
    drows, dpre = tl_bwd(rwkv_pre(l), f"rwkv_pre_bwd{l}", rows, pars["pre"], cts)
    dseg, dtaps, _ = fir_bwd(seg, taps, [drows[0]], f"rwkv_shift_bwd{l}")
    return dseg, (drows[1] if l else None), dtaps, dpre, dpost


def attn_mix_fwd(l, proj):
    os_, ls_ = [], []
    for b, (_, d) in enumerate(DILATED_BRANCHES):
        o, lse = attn_fwd(proj, d, f"attn_fwd{l}_{b}")
        os_.append(o)
        ls_.append(lse)
    (out,) = tl_fwd(attn_combine, f"attn_combine_fwd{l}", os_ + ls_, [], [D_GROUP])
    return out, (proj, os_, ls_)


def attn_mix_bwd(l, saved, dout):
    proj, os_, ls_ = saved
    drows, _ = tl_bwd(attn_combine, f"attn_combine_bwd{l}", os_ + ls_, [], [[dout]])
    grads = [attn_bwd(proj, drows[b], drows[3 + b], d, f"attn_bwd{l}_{b}") for b, (_, d) in enumerate(DILATED_BRANCHES)]
    return tuple(add_rows([g[j] for g in grads], f"attn_d{'qkv'[j]}{l}") for j in range(3))


def ssd_fwd(l, z, xbc, dtr, pars):
    xc = fir_fwd(xbc, pars["taps"], f"ssd_conv_fwd{l}")
    rr, w, kk, xdt, xs = tl_fwd(ssd_pre, f"ssd_pre_fwd{l}", [xc, dtr], pars["pre"], [HM128, HM128, HM128, HM64, D_GROUP])
    blocks = [rr, kk, w, xdt]
    ys, states = chunk_fwd(ssd_chunk, f"ssd_scan_fwd{l}", blocks, (HEAD_DIM, SSD_STATE), HEAD_DIM, SSD_GROUP)
    (out,) = tl_fwd(ssd_post, f"ssd_post_fwd{l}", [ys, z, xs], pars["post"], [D_GROUP])
    return out, (z, xbc, dtr, xc, blocks, states, xs, ys)


def ssd_bwd(l, saved, pars, dout):
    z, xbc, dtr, xc, blocks, states, xs, ys = saved
    (dys, dz, dxs), dpost = tl_bwd(ssd_post, f"ssd_post_bwd{l}", [ys, z, xs], pars["post"], [[dout]])
    drr, dkk, dw, dxdt = chunk_bwd(ssd_chunk, f"ssd_scan_bwd{l}", blocks, states, dys, SSD_GROUP)
    (dxc, ddtr), dpre = tl_bwd(ssd_pre, f"ssd_pre_bwd{l}", [xc, dtr], pars["pre"], [[drr], [dw], [dkk], [dxdt], [dxs]])
    dxbc, dtaps, _ = fir_bwd(xbc, pars["taps"], [dxc], f"ssd_conv_bwd{l}")
    return dz, dxbc, ddtr, dtaps, dpre, dpost


def hgrn_fwd(l, seg, pars):
    blocks = tl_fwd(hgrn_pre, f"hgrn_pre_fwd{l}", [seg], pars["pre"], [HM64] * 4)
    o, states = chunk_fwd(hgrn_chunk, f"hgrn_scan_fwd{l}", blocks, (HEAD_DIM, HEAD_DIM), HEAD_DIM, HGRN_GROUP)
    (out,) = tl_fwd(hgrn_post, f"hgrn_post_fwd{l}", [o, seg], pars["post"], [D_GROUP])
    return out, (seg, blocks, states, o)


def hgrn_bwd(l, saved, pars, dout):
    seg, blocks, states, o = saved
    (do, dseg1), dpost = tl_bwd(hgrn_post, f"hgrn_post_bwd{l}", [o, seg], pars["post"], [[dout]])
    dq, dkk, dlf, di = chunk_bwd(hgrn_chunk, f"hgrn_scan_bwd{l}", blocks, states, do, HGRN_GROUP)
    (dseg2,), dpre = tl_bwd(hgrn_pre, f"hgrn_pre_bwd{l}", [seg], pars["pre"], [[dq], [dkk], [dlf], [di]])
    return add_rows([dseg1, dseg2], f"hgrn_dseg{l}"), dpre, dpost


def layer_fwd(l, x, wts, pars, vfirst):
    proj = matmul(x, wts["in"], "nn", f"proj_fwd{l}")
    seg_h, seg_r = Cols(proj, SEG_HGRN, 1024), Cols(proj, SEG_RWKV, 1024)
    z, xbc, dtr = Cols(proj, SEG_Z, D_GROUP), Cols(proj, SEG_XBC, SSD_XBC), Cols(proj, SEG_DT, 128)
    ya, v_rwkv, sa = rwkv_fwd(l, seg_r, pars["rwkv"]["taps"], pars["rwkv"], vfirst)
    yb, sb = attn_mix_fwd(l, proj)
    yc, sc = ssd_fwd(l, z, xbc, dtr, pars["ssd"])
    yd, sd = hgrn_fwd(l, seg_h, pars["hgrn"])
    mix = [ya, yb, yc, yd]
    mo = matmul(mix, wts["out"], "nn", f"out_fwd{l}")
    (x1,) = tl_fwd(ln_res, f"ln1_fwd{l}", [x, mo], pars["ln1"], [D_MODEL])
    u = matmul(x1, wts["up"], "nn", f"up_fwd{l}")
    dn = matmul(u, wts["down"], "nn", f"down_fwd{l}", a_relu2=True)
    (x2,) = tl_fwd(ln_res, f"ln2_fwd{l}", [x1, dn], pars["ln2"], [D_MODEL])
    return x2, v_rwkv, (x, sa, sb, sc, sd, mix, mo, x1, u, dn)


def layer_bwd(l, saved, wts, pars, dx2, dv_extra):
    x, sa, sb, sc, sd, mix, mo, x1, u, dn = saved
    S = x.shape[0]
    g = {}
    (dx1a, ddn), g["ln2"] = tl_bwd(ln_res, f"ln2_bwd{l}", [x1, dn], pars["ln2"], [[dx2]])
    g["down"] = matmul(u, ddn, "tn", f"down_dw{l}", a_relu2=True)
    du = matmul(ddn, wts["down"], "nt", f"down_dx{l}", relu2_grad_of=u)
    g["up"] = matmul(x1, du, "tn", f"up_dw{l}")
    dx1 = matmul(du, wts["up"], "nt", f"up_dx{l}", add=dx1a)
    (dxa, dmo), g["ln1"] = tl_bwd(ln_res, f"ln1_bwd{l}", [x, mo], pars["ln1"], [[dx1]])
    g["out"] = matmul(mix, dmo, "tn", f"out_dw{l}")
    dmix = matmul(dmo, wts["out"], "nt", f"out_dx{l}")
    dya, dyb, dyc, dyd = (Cols(dmix, j * D_GROUP, D_GROUP) for j in range(4))
    dseg_r, dvfirst, g["rwkv_taps"], g["rwkv_pre"], g["rwkv_post"] = rwkv_bwd(l, sa, pars["rwkv"], dya, dv_extra)
    dq, dk, dv = attn_mix_bwd(l, sb, dyb)
    dz, dxbc, ddtr, g["ssd_taps"], g["ssd_pre"], g["ssd_post"] = ssd_bwd(l, sc, pars["ssd"], dyc)
    dseg_h, g["hgrn_pre"], g["hgrn_post"] = hgrn_bwd(l, sd, pars["hgrn"], dyd)
    dproj = jnp.concatenate([dseg_h, dseg_r, dq, dk, dv, dz, dxbc, ddtr, jnp.zeros((S, PROJ_W - SEG_DT - 128), F32)], axis=1)
    g["in"] = matmul(x, dproj, "tn", f"proj_dw{l}")
    dx = matmul(dproj, wts["in"], "nt", f"proj_dx{l}", add=dxa)
    return dx, dvfirst, g


SMALL = ("lower_bounds", "w_in_vres", "mu_shift", "mu_vres", "rwkv_w0", "rwkv_w2", "rwkv_a0", "rwkv_a2", "rwkv_g2",
         "rwkv_k_k", "rwkv_k_a", "rwkv_r_k", "rwkv_lnx_w", "rwkv_lnx_b", "rwkv_v0", "rwkv_v2", "ssd_conv_w",
         "ssd_conv_b", "ssd_dt_bias", "ssd_A_log", "ssd_D", "ssd_norm_w", "hgrn_norm_w", "ln1_w", "ln1_b", "ln2_w", "ln2_b")
BIG = ("w_in", "w_out", "w_up", "w_down")
SMALL_SHARDED = {"w_in_vres": 1, "rwkv_w2": 2, "rwkv_a2": 2, "rwkv_g2": 2, "rwkv_v2": 2, "ssd_conv_w": 2}
WEIGHTS = ("lower_bounds", "w_in", "w_in_vres", "mu_shift", "mu_vres", "rwkv_w0", "rwkv_w2", "rwkv_a0", "rwkv_a2",
           "rwkv_g2", "rwkv_k_k", "rwkv_k_a", "rwkv_r_k", "rwkv_lnx_w", "rwkv_lnx_b", "rwkv_v0", "rwkv_v2",
           "ssd_conv_w", "ssd_conv_b", "ssd_dt_bias", "ssd_A_log", "ssd_D", "ssd_norm_w", "hgrn_norm_w", "w_out",
           "ln1_w", "ln1_b", "w_up", "w_down", "ln2_w", "ln2_b")


def _row(v, width=None):
    v = v.reshape(1, -1).astype(F32)
    if width is not None and v.shape[1] < width:
        v = jnp.pad(v, ((0, 0), (0, width - v.shape[1])))
    return v


def _rows_at(m, rows, at):
    return jnp.pad(m.astype(F32), ((at, rows - at - m.shape[0]), (0, 0)))


SHARD_COLS = IN_COLS // 4


def _shard_runs(start, width):
    runs, pos = [], start
    while pos < start + width:
        s = pos // SHARD_COLS
        end = min(start + width, (s + 1) * SHARD_COLS)
        runs.append((s, pos - s * SHARD_COLS, end - s * SHARD_COLS))
        pos = end
    return runs


def _pad_w_in(shards, vres):
    rows = shards.shape[1]
    out, pos = [], 0
    for start, width, at in sorted(_PIECES, key=lambda p: p[2]):
        if at > pos:
            out.append(jnp.zeros((rows, at - pos), shards.dtype))
        out += [shards[s, :, lo:hi] for s, lo, hi in _shard_runs(start, width)]
        pos = at + width
        if at == SEG_RWKV and vres is not None:
            out.append(vres.astype(shards.dtype))
            pos += vres.shape[1]
    out.append(jnp.zeros((rows, PROJ_W - pos), shards.dtype))
    return jnp.concatenate(out, axis=1)


def _unpad_w_in(g):
    shards = [[] for _ in range(4)]
    for start, width, at in _PIECES:
        for s, lo, hi in _shard_runs(start, width):
            first = at + s * SHARD_COLS + lo - start
            shards[s].append(g[:, first:first + hi - lo])
    return jnp.stack([jnp.concatenate(p, axis=1) for p in shards])


def layer_params(l, sp, prep):
    lb, mu, om = prep[l], prep[2 + 2 * l], prep[3 + 2 * l]
    pre = [_row(sp["rwkv_w0"][l]), _rows_at(sp["rwkv_w2"][l], 128, 0), _row(sp["rwkv_a0"][l]),
           _rows_at(sp["rwkv_a2"][l], 128, 32), _rows_at(sp["rwkv_g2"][l], 128, 64),
           _row(sp["rwkv_k_k"][l]), _row(sp["rwkv_k_a"][l])]
    if l:
        pre += [_row(sp["rwkv_v0"][l - 1]), _rows_at(sp["rwkv_v2"][l - 1], 128, 0)]
    return {
        "rwkv": {"taps": jnp.concatenate([mu, om], axis=0), "pre": pre,
                 "post": [_row(sp["rwkv_lnx_w"][l]), _row(sp["rwkv_lnx_b"][l]), _row(sp["rwkv_r_k"][l])]},
        "ssd": {"taps": sp["ssd_conv_w"][l].astype(F32),
                "pre": [_row(sp["ssd_conv_b"][l]), _row(sp["ssd_dt_bias"][l], 128), _row(sp["ssd_A_log"][l], 128)],
                "post": [_row(sp["ssd_D"][l], 128), _row(sp["ssd_norm_w"][l])]},
        "hgrn": {"pre": [lb], "post": [_row(sp["hgrn_norm_w"][l])]},
        "ln1": [_row(sp["ln1_w"][l]), _row(sp["ln1_b"][l])],
        "ln2": [_row(sp["ln2_w"][l]), _row(sp["ln2_b"][l])],
    }


def _mu_full(sp, l):
    parts = [sp["mu_shift"][l].reshape(1, -1)]
    if l:
        parts.append(sp["mu_vres"][l - 1].reshape(1, -1))
    return _row(jnp.concatenate(parts, axis=1), 1024)


def local_step(x, target, big, sp):
    prep_in = [sp["lower_bounds"].astype(F32), _mu_full(sp, 0), _mu_full(sp, 1)]
    prep = small_fwd(param_prep, "param_prep_fwd", prep_in,
                     [(1, D_GROUP), (1, D_GROUP), (1, 1024), (1, 1024), (1, 1024), (1, 1024)])
    pars, wts = [], []
    for l in range(DEPTH):
        pars.append(layer_params(l, sp, prep))
        vres = sp["w_in_vres"][l - 1].astype(BF16) if l else None
        wts.append({"in": _pad_w_in(big["w_in"][:, l], vres), "out": big["w_out"][l], "up": big["w_up"][l],
                    "down": big["w_down"][l]})
    h, vfirst, saved = x, None, []
    for l in range(DEPTH):
        h, v_l, sv = layer_fwd(l, h, wts[l], pars[l], vfirst)
        vfirst = v_l if l == 0 else vfirst
        saved.append(sv)
    loss_row, dh = loss_and_grad(h, target, "loss")
    grads, dv_extra = [None] * DEPTH, []
    for l in reversed(range(DEPTH)):
        dh, dvfirst, grads[l] = layer_bwd(l, saved[l], wts[l], pars[l], dh, dv_extra)
        dv_extra = [dvfirst] if l else []
    cts = [grads[0]["hgrn_pre"][0], grads[1]["hgrn_pre"][0]]
    for l in range(DEPTH):
        cts += [grads[l]["rwkv_taps"][0:1], grads[l]["rwkv_taps"][1:2]]
    d_lower, d_mu0, d_mu1 = small_bwd(param_prep, "param_prep_bwd", prep_in, cts)
    d_mu = [d_mu0, d_mu1]
    gb = {"w_in": [_unpad_w_in(grads[l]["in"]) for l in range(DEPTH)], "w_out": [grads[l]["out"] for l in range(DEPTH)],
          "w_up": [grads[l]["up"] for l in range(DEPTH)], "w_down": [grads[l]["down"] for l in range(DEPTH)]}
    st = lambda f: jnp.stack([f(l) for l in range(DEPTH)])
    g1 = grads[1]
    gs = {
        "lower_bounds": d_lower,
        "w_in_vres": g1["in"][None, :, VRES_COL:VRES_COL + 32],
        "mu_shift": st(lambda l: d_mu[l][0, :896]),
        "mu_vres": d_mu[1][:, 896:928],
        "rwkv_w0": st(lambda l: grads[l]["rwkv_pre"][0][0]),
        "rwkv_w2": st(lambda l: grads[l]["rwkv_pre"][1][0:32]),
        "rwkv_a0": st(lambda l: grads[l]["rwkv_pre"][2][0]),
        "rwkv_a2": st(lambda l: grads[l]["rwkv_pre"][3][32:64]),
        "rwkv_g2": st(lambda l: grads[l]["rwkv_pre"][4][64:128]),
        "rwkv_k_k": st(lambda l: grads[l]["rwkv_pre"][5][0]),
        "rwkv_k_a": st(lambda l: grads[l]["rwkv_pre"][6][0]),
        "rwkv_r_k": st(lambda l: grads[l]["rwkv_post"][2].reshape(N_HEADS, HEAD_DIM)),
        "rwkv_lnx_w": st(lambda l: grads[l]["rwkv_post"][0][0]),
        "rwkv_lnx_b": st(lambda l: grads[l]["rwkv_post"][1][0]),
        "rwkv_v0": g1["rwkv_pre"][7],
        "rwkv_v2": g1["rwkv_pre"][8][None, 0:32],
        "ssd_conv_w": st(lambda l: grads[l]["ssd_taps"]),
        "ssd_conv_b": st(lambda l: grads[l]["ssd_pre"][0][0]),
        "ssd_dt_bias": st(lambda l: grads[l]["ssd_pre"][1][0, :N_HEADS]),
        "ssd_A_log": st(lambda l: grads[l]["ssd_pre"][2][0, :N_HEADS]),
        "ssd_D": st(lambda l: grads[l]["ssd_post"][0][0, :N_HEADS]),
        "ssd_norm_w": st(lambda l: grads[l]["ssd_post"][1][0]),
        "hgrn_norm_w": st(lambda l: grads[l]["hgrn_post"][0][0]),
        "ln1_w": st(lambda l: grads[l]["ln1"][0][0]),
        "ln1_b": st(lambda l: grads[l]["ln1"][1][0]),
        "ln2_w": st(lambda l: grads[l]["ln2"][0][0]),
        "ln2_b": st(lambda l: grads[l]["ln2"][1][0]),
    }
    return loss_row, dh, gb, gs


def _pack(vecs):
    parts, meta, row = [], [], 0
    for v in vecs:
        rows = -(-v.size // 1024) * 8
        flat = v.reshape(-1).astype(F32)
        parts.append(jnp.pad(flat, (0, rows * 128 - v.size)).reshape(rows, 128))
        meta.append((row, v.shape))
        row += rows
    return jnp.concatenate(parts, axis=0), meta


def _unpack(packed, meta):
    out = []
    for row, shape in meta:
        size = math.prod(shape)
        rows = -(-size // 1024) * 8
        out.append(packed[row:row + rows].reshape(-1)[:size].reshape(shape))
    return out


def _to_shards(name, g):
    if name == "w_in":
        return g
    if name == "w_up":
        return jnp.transpose(g.reshape(g.shape[0], 4, g.shape[1] // 4), (1, 0, 2))
    return g.reshape(4, g.shape[0] // 4, g.shape[1])


def _from_chips(name, g):
    if name == "w_in":
        return g
    if name == "w_up":
        return jnp.transpose(g, (1, 2, 0, 3)).reshape(g.shape[1], g.shape[2], 4 * g.shape[3])
    return jnp.transpose(g, (1, 0, 2, 3)).reshape(g.shape[1], 4 * g.shape[2], g.shape[3])


INPUT_NAMES = ("x",) + WEIGHTS + ("loss_target",) + tuple("m_" + n for n in WEIGHTS) + tuple("v_" + n for n in WEIGHTS)


def _step(*args):
    a = dict(zip(INPUT_NAMES, args, strict=True))
    chip = 2 * lax.axis_index("x") + lax.axis_index("y")

    sharded_names = list(SMALL_SHARDED)
    small_pack, small_meta = _pack([a[n] for n in sharded_names])
    own = [a[n].astype(BF16) for n in BIG]
    gathered = gather_chips(own, small_pack, "gather_weights")
    here = lambda full, mine: lax.dynamic_update_slice(full, mine[None], (chip,) + (0,) * mine.ndim)
    big = {n: _from_chips(n, here(g, o)) for n, g, o in zip(BIG, gathered, own)}
    sp = {n: a[n] for n in SMALL if n not in SMALL_SHARDED}
    small_all = here(gathered[-1], small_pack)
    per_chip = [_unpack(small_all[s], small_meta) for s in range(4)]
    for j, n in enumerate(sharded_names):
        sp[n] = jnp.concatenate([per_chip[s][j] for s in range(4)], axis=SMALL_SHARDED[n])

    loss_row, gx, gb, gs = local_step(a["x"][0], a["loss_target"][0], big, sp)

    partials = [jnp.stack([_to_shards(n, gb[n][l]) for l in range(DEPTH)]) for n in BIG]
    got = pair_exchange(partials, "pair_exchange")
    ids = jnp.stack([lax.axis_index("c"), chip]).astype(jnp.int32)
    chip_sums = [pair_sum(p, q, ids, f"pair_sum_{n}") for n, p, q in zip(BIG, partials, got)]
    slots = reduce_chips(chip_sums, "reduce_big")
    mine = [sum_chips(sl, p, q, ids, f"sum_{n}") for n, sl, p, q in zip(BIG, slots, partials, got)]
    summed = sibling_exchange(mine, "exchange_big")
    out_g, out_d, out_m, out_v = {}, {}, {}, {}
    for n, g in zip(BIG, summed):
        out_g[n] = g
        out_d[n], out_m[n], out_v[n] = adamw(a[n], g, a["m_" + n], a["v_" + n], f"adamw_{n}")

    vec, meta = _pack([loss_row] + [gs[n] for n in SMALL])
    total = sum_parts(gather_devices(vec, "gather_small"), "sum_small")
    parts = _unpack(total, meta)
    loss = parts[0][0, 0]
    g_small = {}
    for n, g in zip(SMALL, parts[1:]):
        if n in SMALL_SHARDED:
            ax = SMALL_SHARDED[n]
            size = a[n].shape[ax]
            g = lax.dynamic_slice_in_dim(g, chip * size, size, axis=ax)
        g_small[n] = g
    pw, pmeta = _pack([a[n] for n in SMALL])
    pg, _ = _pack([g_small[n] for n in SMALL])
    pm, _ = _pack([a["m_" + n] for n in SMALL])
    pv, _ = _pack([a["v_" + n] for n in SMALL])
    d, nm, nv = adamw(pw[None], pg[None], pm[None], pv[None], "adamw_small")
    for n, dd, mm, vv in zip(SMALL, _unpack(d[0], pmeta), _unpack(nm[0], pmeta), _unpack(nv[0], pmeta)):
        out_g[n], out_d[n], out_m[n], out_v[n] = g_small[n], dd, mm, vv

    return (loss, gx[None], *[out_g[n] for n in WEIGHTS], *[out_d[n] for n in WEIGHTS],
            *[out_m[n] for n in WEIGHTS], *[out_v[n] for n in WEIGHTS])


def kernel(x, lower_bounds, w_in, w_in_vres, mu_shift, mu_vres, rwkv_w0, rwkv_w2, rwkv_a0, rwkv_a2, rwkv_g2, rwkv_k_k, rwkv_k_a, rwkv_r_k, rwkv_lnx_w, rwkv_lnx_b, rwkv_v0, rwkv_v2, ssd_conv_w, ssd_conv_b, ssd_dt_bias, ssd_A_log, ssd_D, ssd_norm_w, hgrn_norm_w, w_out, ln1_w, ln1_b, w_up, w_down, ln2_w, ln2_b, loss_target, m_lower_bounds, m_w_in, m_w_in_vres, m_mu_shift, m_mu_vres, m_rwkv_w0, m_rwkv_w2, m_rwkv_a0, m_rwkv_a2, m_rwkv_g2, m_rwkv_k_k, m_rwkv_k_a, m_rwkv_r_k, m_rwkv_lnx_w, m_rwkv_lnx_b, m_rwkv_v0, m_rwkv_v2, m_ssd_conv_w, m_ssd_conv_b, m_ssd_dt_bias, m_ssd_A_log, m_ssd_D, m_ssd_norm_w, m_hgrn_norm_w, m_w_out, m_ln1_w, m_ln1_b, m_w_up, m_w_down, m_ln2_w, m_ln2_b, v_lower_bounds, v_w_in, v_w_in_vres, v_mu_shift, v_mu_vres, v_rwkv_w0, v_rwkv_w2, v_rwkv_a0, v_rwkv_a2, v_rwkv_g2, v_rwkv_k_k, v_rwkv_k_a, v_rwkv_r_k, v_rwkv_lnx_w, v_rwkv_lnx_b, v_rwkv_v0, v_rwkv_v2, v_ssd_conv_w, v_ssd_conv_b, v_ssd_dt_bias, v_ssd_A_log, v_ssd_D, v_ssd_norm_w, v_hgrn_norm_w, v_w_out, v_ln1_w, v_ln1_b, v_w_up, v_w_down, v_ln2_w, v_ln2_b):
    return _step(x, lower_bounds, w_in, w_in_vres, mu_shift, mu_vres, rwkv_w0, rwkv_w2, rwkv_a0, rwkv_a2, rwkv_g2, rwkv_k_k, rwkv_k_a, rwkv_r_k, rwkv_lnx_w, rwkv_lnx_b, rwkv_v0, rwkv_v2, ssd_conv_w, ssd_conv_b, ssd_dt_bias, ssd_A_log, ssd_D, ssd_norm_w, hgrn_norm_w, w_out, ln1_w, ln1_b, w_up, w_down, ln2_w, ln2_b, loss_target, m_lower_bounds, m_w_in, m_w_in_vres, m_mu_shift, m_mu_vres, m_rwkv_w0, m_rwkv_w2, m_rwkv_a0, m_rwkv_a2, m_rwkv_g2, m_rwkv_k_k, m_rwkv_k_a, m_rwkv_r_k, m_rwkv_lnx_w, m_rwkv_lnx_b, m_rwkv_v0, m_rwkv_v2, m_ssd_conv_w, m_ssd_conv_b, m_ssd_dt_bias, m_ssd_A_log, m_ssd_D, m_ssd_norm_w, m_hgrn_norm_w, m_w_out, m_ln1_w, m_ln1_b, m_w_up, m_w_down, m_ln2_w, m_ln2_b, v_lower_bounds, v_w_in, v_w_in_vres, v_mu_shift, v_mu_vres, v_rwkv_w0, v_rwkv_w2, v_rwkv_a0, v_rwkv_a2, v_rwkv_g2, v_rwkv_k_k, v_rwkv_k_a, v_rwkv_r_k, v_rwkv_lnx_w, v_rwkv_lnx_b, v_rwkv_v0, v_rwkv_v2, v_ssd_conv_w, v_ssd_conv_b, v_ssd_dt_bias, v_ssd_A_log, v_ssd_D, v_ssd_norm_w, v_hgrn_norm_w, v_w_out, v_ln1_w, v_ln1_b, v_w_up, v_w_down, v_ln2_w, v_ln2_b)
```

```python
import functools
import math

import jax
import jax.numpy as jnp
from jax import lax
from jax.experimental import pallas as pl
from jax.experimental.pallas import tpu as pltpu

F32 = jnp.float32
BF16 = jnp.bfloat16
HI = lax.Precision.HIGHEST

DEPTH = 2
D_MODEL = 1024
D_GROUP = 256
HEAD_DIM = 64
N_HEADS = 4
SSD_STATE = 128
SSD_XBC = 768
SSD_CONV = 4
D_FF = 4096
ALPHA = (2.0 * DEPTH) ** 0.25
LN_EPS = 1e-5
RMS_EPS = 1e-5
RWKV_GN_EPS = HEAD_DIM * 1e-5
DILATED_BRANCHES = ((128, 1), (512, 4), (2048, 16))
ALIBI_SLOPES = tuple(2.0 ** (-8.0 * (h + 1) / N_HEADS) for h in range(N_HEADS))
ATTN_BLK = 128

ADAM_LR, ADAM_B1, ADAM_B2, ADAM_EPS, ADAM_WD, ADAM_STEP = 0.001, 0.9, 0.999, 1e-08, 0.01, 10

IN_COLS = 3716
PROJ_W = 4096
SEG_HGRN, SEG_RWKV, SEG_Q, SEG_Z, SEG_XBC, SEG_DT = 0, 1024, 2048, 2816, 3072, 3840
_PIECES = ((0, 896, SEG_RWKV), (896, 768, SEG_Q), (1664, 256, SEG_Z), (1920, 768, SEG_XBC),
           (2688, 4, SEG_DT), (2692, 1024, SEG_HGRN))
VRES_COL = SEG_RWKV + 896

HM64 = (N_HEADS, HEAD_DIM)
HM128 = (N_HEADS, SSD_STATE)
ROW_TILE = 256
SCAN_CHUNK = 128
VMEM_LIMIT = 48 * 1024 * 1024


def _cparams(sem=None):
    if sem is None:
        return pltpu.CompilerParams(vmem_limit_bytes=VMEM_LIMIT)
    return pltpu.CompilerParams(dimension_semantics=sem, vmem_limit_bytes=VMEM_LIMIT)


def _pick(n, pref):
    for t in pref:
        if n % t == 0:
            return t
    return n


def _relu2(u):
    r = jnp.maximum(u, 0.0)
    return r * r


class ShardedWeight:
    def __init__(self, arr, layer, axis):
        rows, cols = arr.shape[2:]
        assert (rows, cols)[axis] == 1024
        self.arr, self.layer, self.axis = arr, layer, axis
        self.shape = (4 * rows, cols) if axis == 0 else (rows, 4 * cols)

    def spec(self, mode, tn, tk):
        l, along_rows = self.layer, self.axis == 0
        if mode == "nn":
            assert (tk if along_rows else tn) == 1024
            index = (lambda i, j, k: (k, l, 0, j)) if along_rows else (lambda i, j, k: (j, l, k, 0))
            return pl.BlockSpec((None, None, tk, tn), index)
        assert mode == "nt" and (tn if along_rows else tk) == 1024
        index = (lambda i, j, k: (j, l, 0, k)) if along_rows else (lambda i, j, k: (k, l, j, 0))
        return pl.BlockSpec((None, None, tn, tk), index)


def matmul(a, b, mode, name, add=None, a_relu2=False, relu2_grad_of=None, out_col_shards=False):
    pieces = list(a) if isinstance(a, (list, tuple)) else [a]
    a_rows, a_cols = pieces[0].shape[0], sum(p.shape[1] for p in pieces)
    if mode == "nn":
        (M, K), N = (a_rows, a_cols), b.shape[1]
    elif mode == "nt":
        (M, K), N = (a_rows, a_cols), b.shape[0]
    else:
        (K, M), N = (a_rows, a_cols), b.shape[1]
    tm, tn, tk = _pick(M, (1024, 512, 256, 128)), _pick(N, (1024, 512, 256, 128)), _pick(K, (1024, 512, 256, 128))
    nk = K // tk
    dims = {"nn": (((1,), (0,)), ((), ())), "nt": (((1,), (1,)), ((), ())), "tn": (((0,), (0,)), ((), ()))}[mode]
    extras = [e for e in (add, relu2_grad_of) if e is not None]
    npieces = len(pieces)

    def body(*refs):
        b_ref = refs[npieces]
        o_ref = refs[-1]
        rest = list(refs[npieces + 1:-1])
        add_ref = rest.pop(0) if add is not None else None
        u_ref = rest.pop(0) if relu2_grad_of is not None else None
        k = pl.program_id(2)
        av = refs[0][...] if npieces == 1 else jnp.concatenate([r[...] for r in refs[:npieces]], axis=1)
        if a_relu2:
            av = _relu2(av)
        d = lax.dot_general(av.astype(BF16), b_ref[...].astype(BF16), dims, preferred_element_type=F32)

        @pl.when(k == 0)
        def _():
            o_ref[...] = d if add_ref is None else d + add_ref[...]

        if nk > 1:
            @pl.when(k > 0)
            def _():
                o_ref[...] += d

        if u_ref is not None:
            @pl.when(k == nk - 1)
            def _():
                o_ref[...] = o_ref[...] * (2.0 * jnp.maximum(u_ref[...], 0.0))

    if npieces == 1:
        a_specs = [pl.BlockSpec((tk, tm), lambda i, j, k: (k, i)) if mode == "tn" else pl.BlockSpec((tm, tk), lambda i, j, k: (i, k))]
    else:
        assert a_cols == (tm if mode == "tn" else tk)
        rows_of = (lambda i, j, k: (k, 0)) if mode == "tn" else (lambda i, j, k: (i, 0))
        a_specs = [pl.BlockSpec((tk if mode == "tn" else tm, p.shape[1]), rows_of) for p in pieces]
    if isinstance(b, ShardedWeight):
        b_spec = b.spec(mode, tn, tk)
    else:
        b_spec = pl.BlockSpec((tn, tk), lambda i, j, k: (j, k)) if mode == "nt" else pl.BlockSpec((tk, tn), lambda i, j, k: (k, j))
    o_spec = pl.BlockSpec((tm, tn), lambda i, j, k: (i, j))
    ins, specs = pieces + [_arr(b)] + extras, a_specs + [b_spec] + [o_spec] * len(extras)
    out_shape, out_spec = (M, N), o_spec
    if out_col_shards:
        assert tn == N // 4
        out_shape, out_spec = (4, M, tn), pl.BlockSpec((None, tm, tn), lambda i, j, k: (j, i, 0))
    return pl.pallas_call(
        body, grid=(M // tm, N // tn, nk), in_specs=specs, out_specs=out_spec,
        out_shape=jax.ShapeDtypeStruct(out_shape, F32),
        compiler_params=_cparams(("parallel", "parallel", "arbitrary")), name=name)(*ins)


def _row_spec(w, tile):
    return pl.BlockSpec((tile, w), lambda i: (i, 0))


def _par_spec(shape):
    return pl.BlockSpec(shape, lambda i: (0,) * len(shape))


class Cols:
    def __init__(self, arr, start, width):
        assert start % width == 0 or (start % 128 == 0 and width % 128 == 0)
        self.arr, self.start, self.width = arr, start, width
        self.shape, self.ndim = (arr.shape[0], width), 2


def _arr(r):
    return r.arr if isinstance(r, (Cols, ShardedWeight)) else r


def _rows_spec(a, tile):
    if isinstance(a, Cols):
        assert a.start % a.width == 0
        return pl.BlockSpec((tile, a.width), lambda i, blk=a.start // a.width: (i, blk))
    shape = a if isinstance(a, tuple) else a.shape
    if len(shape) == 3:
        return pl.BlockSpec((shape[0], tile, shape[2]), lambda i: (0, i, 0))
    return _row_spec(shape[1], tile)


def _rows_shape(S, w):
    return (w[0], S, w[1]) if isinstance(w, tuple) else (S, w)


def _rows_load(ref):
    if len(ref.shape) == 3:
        return jnp.concatenate([ref[h] for h in range(ref.shape[0])], axis=1)
    return ref[...]


def _rows_store(ref, val):
    if len(ref.shape) == 3:
        w = ref.shape[2]
        for h in range(ref.shape[0]):
            ref[h] = val[:, h * w:(h + 1) * w]
    else:
        ref[...] = val


def tl_fwd(fn, name, rows, pars, out_widths, tile=ROW_TILE):
    S = rows[0].shape[-2]
    nr = len(rows)

    def body(*refs):
        ins = [_rows_load(r) for r in refs[:nr]] + [r[...] for r in refs[nr:nr + len(pars)]]
        outs = fn(*ins)
        for o_ref, o in zip(refs[nr + len(pars):], outs):
            _rows_store(o_ref, o)

    shapes = [_rows_shape(S, w) for w in out_widths]
    return pl.pallas_call(
        body, grid=(S // tile,),
        in_specs=[_rows_spec(r, tile) for r in rows] + [_par_spec(p.shape) for p in pars],
        out_specs=[_rows_spec(s, tile) for s in shapes],
        out_shape=[jax.ShapeDtypeStruct(s, F32) for s in shapes],
        compiler_params=_cparams(("parallel",)), name=name)(*[_arr(r) for r in rows], *pars)


def tl_bwd(fn, name, rows, pars, cts, tile=ROW_TILE, row_grad=None):
    S = rows[0].shape[-2]
    nr, npar = len(rows), len(pars)
    row_grad = [True] * nr if row_grad is None else row_grad
    flat_cts = [c for group in cts for c in group]
    ncts = len(flat_cts)
    gi = [i for i in range(nr) if row_grad[i]]

    def body(*refs):
        row_v = [_rows_load(r) for r in refs[:nr]]
        par_v = [r[...] for r in refs[nr:nr + npar]]
        ct_refs = refs[nr + npar:nr + npar + ncts]
        out_refs = refs[nr + npar + ncts:]
        ct_v, pos = [], 0
        for group in cts:
            acc = _rows_load(ct_refs[pos])
            for q in range(1, len(group)):
                acc = acc + _rows_load(ct_refs[pos + q])
            pos += len(group)
            ct_v.append(acc)

        def f(diff_rows, par_vals):
            full = list(row_v)
            for idx, val in zip(gi, diff_rows):
                full[idx] = val
            return tuple(fn(*full, *par_vals))

        _, vjp = jax.vjp(f, [row_v[i] for i in gi], par_v)
        d_rows, d_pars = vjp(tuple(ct_v))
        for o_ref, g in zip(out_refs[:len(gi)], d_rows):
            _rows_store(o_ref, g)
        first = pl.program_id(0) == 0
        for o_ref, g in zip(out_refs[len(gi):], d_pars):
            @pl.when(first)
            def _(o_ref=o_ref):
                o_ref[...] = jnp.zeros_like(o_ref)
            o_ref[...] += g

    outs = pl.pallas_call(
        body, grid=(S // tile,),
        in_specs=[_rows_spec(r, tile) for r in rows] + [_par_spec(p.shape) for p in pars]
        + [_rows_spec(c, tile) for c in flat_cts],
        out_specs=[_rows_spec(rows[i].shape, tile) for i in gi] + [_par_spec(p.shape) for p in pars],
        out_shape=[jax.ShapeDtypeStruct(rows[i].shape, F32) for i in gi] + [jax.ShapeDtypeStruct(p.shape, F32) for p in pars],
        compiler_params=_cparams(("arbitrary",)), name=name)(*[_arr(r) for r in rows], *pars, *[_arr(c) for c in flat_cts])
    return list(outs[:len(gi)]), list(outs[len(gi):])


def _shift_rows(x, j):
    if j == 0:
        return x
    rolled = pltpu.roll(x, j, 0)
    row = lax.broadcasted_iota(jnp.int32, x.shape, 0)
    return jnp.where(row >= j, rolled, 0.0)


def _unshift_rows(x, j):
    if j == 0:
        return x
    S = x.shape[0]
    rolled = pltpu.roll(x, S - j, 0)
    row = lax.broadcasted_iota(jnp.int32, x.shape, 0)
    return jnp.where(row < S - j, rolled, 0.0)


def _fir_in_spec(x):
    first = x.start // 128 if isinstance(x, Cols) else 0
    return pl.BlockSpec((x.shape[0], 128), lambda j: (0, first + j))


def fir_fwd(x, taps, name):
    S, C = x.shape
    K = taps.shape[0]

    def body(x_ref, w_ref, y_ref):
        xv = x_ref[...]
        acc = jnp.zeros_like(xv)
        for k in range(K):
            acc = acc + _shift_rows(xv, K - 1 - k) * w_ref[pl.ds(k, 1), :]
        y_ref[...] = acc

    cs = pl.BlockSpec((S, 128), lambda j: (0, j))
    return pl.pallas_call(body, grid=(C // 128,), in_specs=[_fir_in_spec(x), pl.BlockSpec((K, 128), lambda j: (0, j))],
                          out_specs=cs, out_shape=jax.ShapeDtypeStruct((S, C), F32),
                          compiler_params=_cparams(("parallel",)), name=name)(_arr(x), taps)


def fir_bwd(x, taps, dy_list, name):
    S, C = x.shape
    K = taps.shape[0]
    n = len(dy_list)

    def body(*refs):
        x_ref, w_ref = refs[:2]
        dy = refs[2][...]
        for q in range(1, n):
            dy = dy + refs[2 + q][...]
        dx_ref, dw_ref, db_ref = refs[2 + n:]
        xv = x_ref[...]
        dx = jnp.zeros_like(xv)
        for k in range(K):
            j = K - 1 - k
            dx = dx + _unshift_rows(dy, j) * w_ref[pl.ds(k, 1), :]
            dw_ref[pl.ds(k, 1), :] = jnp.sum(dy * _shift_rows(xv, j), axis=0, keepdims=True)
        dx_ref[...] = dx
        db_ref[...] = jnp.sum(dy, axis=0, keepdims=True)

    cs = pl.BlockSpec((S, 128), lambda j: (0, j))
    ks = pl.BlockSpec((K, 128), lambda j: (0, j))
    bs = pl.BlockSpec((1, 128), lambda j: (0, j))
    return pl.pallas_call(body, grid=(C // 128,), in_specs=[_fir_in_spec(x), ks] + [cs] * n, out_specs=[cs, ks, bs],
                          out_shape=[jax.ShapeDtypeStruct((S, C), F32), jax.ShapeDtypeStruct((K, C), F32),
                                     jax.ShapeDtypeStruct((1, C), F32)],
                          compiler_params=_cparams(("parallel",)), name=name)(_arr(x), taps, *dy_list)


def _col(tile, lane, t):
    return jnp.sum(jnp.where(lane == t, tile, 0.0), axis=1, keepdims=True)


def _rwkv_step(s, rv, vcol):
    sa = jnp.sum(s * (-rv[3]), axis=1, keepdims=True)
    return s * rv[1] + sa * (rv[3] * rv[4]) + vcol * rv[2], sa


def _eye(n):
    return (lax.broadcasted_iota(jnp.int32, (n, n), 0) == lax.broadcasted_iota(jnp.int32, (n, n), 1)).astype(F32)


def _transposed(x):
    return lax.dot_general(_eye(x.shape[1]), x, (((1,), (1,)), ((), ())), precision=HI, preferred_element_type=F32)


def scan_fwd(r, w, k, v, kk, a, name):
    H, S, Dk = r.shape
    Dv = v.shape[1] // H
    Tc = SCAN_CHUNK
    nc = S // Tc
    rows = [r, w, k, kk, a]

    def body(*refs):
        row_refs = refs[:5]
        v_ref, y_ref, sall_ref, s_ref, vT_ref, yT_ref = refs[5:]

        @pl.when(pl.program_id(0) == 0)
        def _():
            s_ref[...] = jnp.zeros_like(s_ref)

        for h in range(H):
            vT_ref[h] = _transposed(v_ref[:, h * Dv:(h + 1) * Dv])
        yT_ref[...] = jnp.zeros_like(yT_ref)
        lane = lax.broadcasted_iota(jnp.int32, (Dv, Tc), 1)

        def step(t, states):
            new = []
            for h in range(H):
                s = states[h]
                sall_ref[t, h] = s
                rv = [ref[h, pl.ds(t, 1), :] for ref in row_refs]
                s, _ = _rwkv_step(s, rv, _col(vT_ref[h], lane, t))
                ycol = jnp.sum(s * rv[0], axis=1, keepdims=True)
                yT_ref[h] = jnp.where(lane == t, ycol, yT_ref[h])
                new.append(s)
            return tuple(new)

        states = lax.fori_loop(0, Tc, step, tuple(s_ref[h] for h in range(H)))
        for h in range(H):
            s_ref[h] = states[h]
            y_ref[:, h * Dv:(h + 1) * Dv] = _transposed(yT_ref[h])

    rs = pl.BlockSpec((H, Tc, Dk), lambda c: (0, c, 0))
    vs = pl.BlockSpec((Tc, H * Dv), lambda c: (c, 0))
    return pl.pallas_call(
        body, grid=(nc,), in_specs=[rs] * 5 + [vs],
        out_specs=[vs, pl.BlockSpec((Tc, H, Dv, Dk), lambda c: (c, 0, 0, 0))],
        out_shape=[jax.ShapeDtypeStruct((S, H * Dv), F32), jax.ShapeDtypeStruct((S, H, Dv, Dk), F32)],
        scratch_shapes=[pltpu.VMEM((H, Dv, Dk), F32), pltpu.VMEM((H, Dv, Tc), F32), pltpu.VMEM((H, Dv, Tc), F32)],
        compiler_params=_cparams(("arbitrary",)), name=name)(*rows, v)


def scan_bwd(r, w, k, v, kk, a, sall, dy, name):
    H, S, Dk = r.shape
    Dv = v.shape[1] // H
    Tc = SCAN_CHUNK
    nc = S // Tc
    rows = [r, w, k, kk, a]

    def body(*refs):
        row_refs = refs[:5]
        v_ref, dy_ref, sall_ref = refs[5:8]
        drow_refs = refs[8:13]
        dv_ref, ds_ref, vT_ref, dyT_ref, dvT_ref = refs[13:]

        @pl.when(pl.program_id(0) == 0)
        def _():
            ds_ref[...] = jnp.zeros_like(ds_ref)

        for h in range(H):
            vT_ref[h] = _transposed(v_ref[:, h * Dv:(h + 1) * Dv])
            dyT_ref[h] = _transposed(dy_ref[:, h * Dv:(h + 1) * Dv])
        dvT_ref[...] = jnp.zeros_like(dvT_ref)
        lane = lax.broadcasted_iota(jnp.int32, (Dv, Tc), 1)

        def bstep(i, carry):
            t = Tc - 1 - i
            new = []
            for h in range(H):
                ds = carry[h]
                sp = sall_ref[t, h]
                rv = [ref[h, pl.ds(t, 1), :] for ref in row_refs]
                vcol = _col(vT_ref[h], lane, t)
                dycol = _col(dyT_ref[h], lane, t)
                st, sa = _rwkv_step(sp, rv, vcol)
                drow_refs[0][h, pl.ds(t, 1), :] = jnp.sum(st * dycol, axis=0, keepdims=True)
                g = ds + dycol * rv[0]
                drow_refs[1][h, pl.ds(t, 1), :] = jnp.sum(g * sp, axis=0, keepdims=True)
                drow_refs[2][h, pl.ds(t, 1), :] = jnp.sum(g * vcol, axis=0, keepdims=True)
                dvcol = jnp.sum(g * rv[2], axis=1, keepdims=True)
                dsa = jnp.sum(g * (rv[3] * rv[4]), axis=1, keepdims=True)
                db = jnp.sum(g * sa, axis=0, keepdims=True)
                dnkk = jnp.sum(sp * dsa, axis=0, keepdims=True)
                drow_refs[3][h, pl.ds(t, 1), :] = db * rv[4] - dnkk
                drow_refs[4][h, pl.ds(t, 1), :] = db * rv[3]
                dvT_ref[h] = jnp.where(lane == t, dvcol, dvT_ref[h])
                new.append(g * rv[1] - dsa * rv[3])
            return tuple(new)

        carry = lax.fori_loop(0, Tc, bstep, tuple(ds_ref[h] for h in range(H)))
        for h in range(H):
            ds_ref[h] = carry[h]
            dv_ref[:, h * Dv:(h + 1) * Dv] = _transposed(dvT_ref[h])

    rs = pl.BlockSpec((H, Tc, Dk), lambda c: (0, nc - 1 - c, 0))
    vs = pl.BlockSpec((Tc, H * Dv), lambda c: (nc - 1 - c, 0))
    tile = pltpu.VMEM((H, Dv, Tc), F32)
    outs = pl.pallas_call(
        body, grid=(nc,),
        in_specs=[rs] * 5 + [vs, vs, pl.BlockSpec((Tc, H, Dv, Dk), lambda c: (nc - 1 - c, 0, 0, 0))],
        out_specs=[rs] * 5 + [vs],
        out_shape=[jax.ShapeDtypeStruct((H, S, Dk), F32)] * 5 + [jax.ShapeDtypeStruct((S, H * Dv), F32)],
        scratch_shapes=[pltpu.VMEM((H, Dv, Dk), F32), tile, tile, tile],
        compiler_params=_cparams(("arbitrary",)), name=name)(*rows, v, dy, sall)
    return list(outs[:5]), outs[5]


CHUNK = 128
SSD_GROUP = 2
HGRN_GROUP = 4


def chunk_fwd(fn, name, blocks, state_shape, out_width, group):
    H, S, _ = blocks[0].shape
    nc = S // CHUNK
    nb = len(blocks)

    def body(*refs):
        o_ref, sv_ref, st = refs[nb:]

        @pl.when(pl.program_id(1) == 0)
        def _():
            st[...] = jnp.zeros_like(st)

        for g in range(group):
            s0 = st[g]
            sv_ref[g, 0] = s0
            s1, out = fn(s0, *[r[g] for r in refs[:nb]])
            st[g] = s1
            o_ref[g] = out

    spec = lambda w: pl.BlockSpec((group, CHUNK, w), lambda h, c: (h, c, 0))
    return pl.pallas_call(
        body, grid=(H // group, nc), in_specs=[spec(b.shape[2]) for b in blocks],
        out_specs=[spec(out_width), pl.BlockSpec((group, 1) + state_shape, lambda h, c: (h, c, 0, 0))],
        out_shape=[jax.ShapeDtypeStruct((H, S, out_width), F32), jax.ShapeDtypeStruct((H, nc) + state_shape, F32)],
        scratch_shapes=[pltpu.VMEM((group,) + state_shape, F32)],
        compiler_params=_cparams(("parallel", "arbitrary")), name=name)(*blocks)


def chunk_bwd(fn, name, blocks, states, dout, group):
    H, S, _ = blocks[0].shape
    nc = S // CHUNK
    nb = len(blocks)
    state_shape = states.shape[2:]

    def body(*refs):
        sv_ref, do_ref = refs[nb], refs[nb + 1]
        d_refs = refs[nb + 2:2 * nb + 2]
        dst = refs[2 * nb + 2]

        @pl.when(pl.program_id(1) == 0)
        def _():
            dst[...] = jnp.zeros_like(dst)

        for g in range(group):
            _, vjp = jax.vjp(fn, sv_ref[g, 0], *[r[g] for r in refs[:nb]])
            grads = vjp((dst[g], do_ref[g]))
            dst[g] = grads[0]
            for d_ref, gr in zip(d_refs, grads[1:]):
                d_ref[g] = gr

    spec = lambda w: pl.BlockSpec((group, CHUNK, w), lambda h, c: (h, nc - 1 - c, 0))
    return pl.pallas_call(
        body, grid=(H // group, nc),
        in_specs=[spec(b.shape[2]) for b in blocks]
        + [pl.BlockSpec((group, 1) + state_shape, lambda h, c: (h, nc - 1 - c, 0, 0)), spec(dout.shape[2])],
        out_specs=[spec(b.shape[2]) for b in blocks],
        out_shape=[jax.ShapeDtypeStruct(b.shape, F32) for b in blocks],
        scratch_shapes=[pltpu.VMEM((group,) + state_shape, F32)],
        compiler_params=_cparams(("parallel", "arbitrary")), name=name)(*blocks, states, dout)


def _bdot(a, b, dims):
    return lax.dot_general(a.astype(BF16), b.astype(BF16), (dims, ((), ())), preferred_element_type=F32)


def ssd_chunk(state, cb, bb, da, xdt):
    T = cb.shape[0]
    ti = lax.broadcasted_iota(jnp.int32, (T, T), 0)
    si = lax.broadcasted_iota(jnp.int32, (T, T), 1)
    mask = ti >= si
    cs = jnp.dot(mask.astype(F32), da, precision=HI, preferred_element_type=F32)
    pick = (lax.broadcasted_iota(jnp.int32, cs.shape, 1) == 0).astype(F32)
    cs_row = lax.dot_general(pick, cs, (((1,), (1,)), ((), ())), precision=HI, preferred_element_type=F32)
    lmat = jnp.where(mask, jnp.exp(jnp.where(mask, cs - cs_row, 0.0)), 0.0)
    scores = _bdot(cb, bb, ((1,), (1,))) * lmat
    y = _bdot(scores, xdt, ((1,), (0,))) + _bdot(cb, state, ((1,), (1,))) * jnp.exp(cs[:, :HEAD_DIM])
    last = cs[T - 1:T, :]
    new_state = state * jnp.exp(last) + _bdot(xdt, bb * jnp.exp(last - cs), ((0,), (0,)))
    return new_state, y


HGRN_SUB = 16


def hgrn_chunk(state, q, k, lf, v):
    T, C = q.shape[0], HGRN_SUB
    ti = lax.broadcasted_iota(jnp.int32, (C, C), 0)
    si = lax.broadcasted_iota(jnp.int32, (C, C), 1)
    tril = (ti >= si).astype(F32)
    row = lax.broadcasted_iota(jnp.int32, (C, q.shape[1]), 0)
    outs = []
    for j in range(T // C):
        qj, kj, lj, vj = (a[j * C:(j + 1) * C] for a in (q, k, lf, v))
        b = jnp.dot(tril, lj, precision=HI, preferred_element_type=F32)
        o = _bdot(qj * jnp.exp(b), state, ((1,), (1,)))
        for s in range(C):
            m = row >= s
            e = jnp.where(m, jnp.exp(jnp.where(m, b - b[s:s + 1], 0.0)), 0.0)
            o = o + jnp.sum(qj * kj[s:s + 1] * e, axis=1, keepdims=True) * vj[s:s + 1]
        last = b[C - 1:C]
        state = state * jnp.exp(last) + _bdot(vj, kj * jnp.exp(last - b), ((0,), (0,)))
        outs.append(o)
    return state, jnp.concatenate(outs, axis=0)


def _attn_block(q, kp, kc, vp, vc, n, slope, dilation):
    blk = ATTN_BLK
    k2 = jnp.concatenate([kp, kc], axis=0)
    v2 = jnp.concatenate([vp, vc], axis=0)
    s = _bdot(q, k2, ((1,), (1,))) * (HEAD_DIM ** -0.5)
    i = lax.broadcasted_iota(jnp.int32, (blk, 2 * blk), 0)
    j = lax.broadcasted_iota(jnp.int32, (blk, 2 * blk), 1)
    dist = blk + i - j
    first_key = jnp.where(n > 0, 0, blk)
    valid = (dist >= 0) & (dist <= blk) & (j >= first_key)
    s = s - slope * (dist * dilation).astype(F32)
    s = jnp.where(valid, s, -1e30)
    m = jnp.max(s, axis=-1, keepdims=True)
    p = jnp.exp(s - m)
    l = jnp.sum(p, axis=-1, keepdims=True)
    o = _bdot(p, v2, ((1,), (0,))) / l
    lse = jnp.broadcast_to(m + jnp.log(l), o.shape)
    return o, lse


_QCOL = SEG_Q // 128
PAIR = 2 * HEAD_DIM


def _attn_specs(rows):
    cur = lambda j: pl.BlockSpec((rows, PAIR), lambda p, n: (n, j + p))
    prev = lambda j: pl.BlockSpec((rows, PAIR), lambda p, n: (jnp.maximum(n - 1, 0), j + p))
    return cur, prev


def _pair_slope(pair, h):
    return jnp.where(pair == 0, jnp.float32(ALIBI_SLOPES[h]), jnp.float32(ALIBI_SLOPES[2 + h]))


def _halves(t):
    return [t[:, h * HEAD_DIM:(h + 1) * HEAD_DIM] for h in range(2)]


def _for_classes(dilation, step):
    if dilation == 1:
        step(0)
    else:
        lax.fori_loop(0, dilation, lambda z, c: (step(z), c)[1], 0)


def attn_fwd(proj, dilation, name):
    S = proj.shape[0]
    blk = ATTN_BLK
    rows = blk * dilation
    cur, prev = _attn_specs(rows)

    def body(q_ref, kp_ref, kc_ref, vp_ref, vc_ref, o_ref, l_ref):
        pair, n = pl.program_id(0), pl.program_id(1)

        def one_class(z):
            sel = pl.ds(z, blk, stride=dilation) if dilation > 1 else pl.ds(0, blk)
            q, kp, kc, vp, vc = (_halves(r[sel, :]) for r in (q_ref, kp_ref, kc_ref, vp_ref, vc_ref))
            res = [_attn_block(q[h], kp[h], kc[h], vp[h], vc[h], n, _pair_slope(pair, h), dilation) for h in range(2)]
            o_ref[sel, :] = jnp.concatenate([r[0] for r in res], axis=1)
            l_ref[sel, :] = jnp.concatenate([r[1] for r in res], axis=1)

        _for_classes(dilation, one_class)

    return pl.pallas_call(
        body, grid=(2, S // rows),
        in_specs=[cur(_QCOL), prev(_QCOL + 2), cur(_QCOL + 2), prev(_QCOL + 4), cur(_QCOL + 4)],
        out_specs=[cur(0), cur(0)], out_shape=[jax.ShapeDtypeStruct((S, D_GROUP), F32)] * 2,
        compiler_params=_cparams(("parallel", "arbitrary")), name=name)(proj, proj, proj, proj, proj)


def attn_bwd(proj, do, dlse, dilation, name):
    S = proj.shape[0]
    blk = ATTN_BLK
    rows = blk * dilation
    cur, prev = _attn_specs(rows)
    full = pl.BlockSpec((S, PAIR), lambda p, n: (0, p))

    def body(q_ref, kp_ref, kc_ref, vp_ref, vc_ref, do_ref, dl_ref, dq_ref, dk_ref, dv_ref):
        pair, n = pl.program_id(0), pl.program_id(1)

        @pl.when(n == 0)
        def _():
            dk_ref[...] = jnp.zeros_like(dk_ref)
            dv_ref[...] = jnp.zeros_like(dv_ref)

        def one_class(z):
            sel = pl.ds(z, blk, stride=dilation) if dilation > 1 else pl.ds(0, blk)
            q, kp, kc, vp, vc, do_v, dl_v = (_halves(r[sel, :]) for r in
                                             (q_ref, kp_ref, kc_ref, vp_ref, vc_ref, do_ref, dl_ref))
            grads = []
            for h in range(2):
                f = lambda q_, kp_, kc_, vp_, vc_, h=h: _attn_block(q_, kp_, kc_, vp_, vc_, n, _pair_slope(pair, h), dilation)
                _, vjp = jax.vjp(f, q[h], kp[h], kc[h], vp[h], vc[h])
                grads.append(vjp((do_v[h], dl_v[h])))
            both = lambda j: jnp.concatenate([grads[0][j], grads[1][j]], axis=1)
            dq_ref[sel, :] = both(0)
            if dilation > 1:
                here = pl.ds(n * rows + z, blk, stride=dilation)
                before = pl.ds(jnp.maximum(n - 1, 0) * rows + z, blk, stride=dilation)
            else:
                here = pl.ds(pl.multiple_of(n * blk, blk), blk)
                before = pl.ds(pl.multiple_of(jnp.maximum(n - 1, 0) * blk, blk), blk)
            dk_ref[here, :] = dk_ref[here, :] + both(2)
            dv_ref[here, :] = dv_ref[here, :] + both(4)
            dk_ref[before, :] = dk_ref[before, :] + both(1)
            dv_ref[before, :] = dv_ref[before, :] + both(3)

        _for_classes(dilation, one_class)

    return pl.pallas_call(
        body, grid=(2, S // rows),
        in_specs=[cur(_QCOL), prev(_QCOL + 2), cur(_QCOL + 2), prev(_QCOL + 4), cur(_QCOL + 4), cur(0), cur(0)],
        out_specs=[cur(0), full, full], out_shape=[jax.ShapeDtypeStruct((S, D_GROUP), F32)] * 3,
        compiler_params=_cparams(("parallel", "arbitrary")), name=name)(proj, proj, proj, proj, proj, do, dlse)


def _head_ones(width, group):
    i = lax.broadcasted_iota(jnp.int32, (width, width), 0) // group
    j = lax.broadcasted_iota(jnp.int32, (width, width), 1) // group
    return (i == j).astype(F32)


def _group_sum(x, group):
    return jnp.dot(x, _head_ones(x.shape[1], group), precision=HI, preferred_element_type=F32)


def _spread(width_in, width_out, rep):
    i = lax.broadcasted_iota(jnp.int32, (width_in, width_out), 0)
    j = lax.broadcasted_iota(jnp.int32, (width_in, width_out), 1) // rep
    return (i == j).astype(F32)


def _hdot(a, b):
    return jnp.dot(a, b, precision=HI, preferred_element_type=F32)


def _sigmoid(x):
    return 1.0 / (1.0 + jnp.exp(-x))


def _softplus(x):
    return jnp.maximum(x, 0.0) + jnp.log(1.0 + jnp.exp(jnp.minimum(x, -x)))


def _silu(x):
    return x * _sigmoid(x)


def rwkv_pre(layer):
    def fn(*args):
        if layer == 0:
            fs, w0, w2p, a0, a2p, g2p, k_k, k_a = args
        else:
            fs, vfirst, w0, w2p, a0, a2p, g2p, k_k, k_a, v0, v2p = args
        r, k, v = fs[:, 0:256], fs[:, 256:512], fs[:, 512:768]
        lora = fs[:, 768:896]
        w_log = -_softplus(-(w0 + _hdot(jnp.tanh(lora), w2p))) - 0.5
        decay = jnp.exp(-jnp.exp(w_log))
        a = _sigmoid(a0 + _hdot(lora, a2p))
        g = _hdot(_sigmoid(lora), g2p)
        if layer > 0:
            v = v + (vfirst - v) * _sigmoid(v0 + _hdot(fs[:, 896:1024], v2p))
        kk = k * k_k
        kk = kk / jnp.maximum(jnp.sqrt(_group_sum(kk * kk, HEAD_DIM)), 1e-12)
        k = k * (1.0 + (a - 1.0) * k_a)
        return r, decay, k, v, kk, a, g
    return fn


def rwkv_post(y, r, k, v, g, lnx_w, lnx_b, r_k):
    mu = _group_sum(y, HEAD_DIM) * (1.0 / HEAD_DIM)
    yc = y - mu
    var = _group_sum(yc * yc, HEAD_DIM) * (1.0 / HEAD_DIM)
    yn = yc * lax.rsqrt(var + RWKV_GN_EPS) * lnx_w + lnx_b
    bonus = _group_sum(r * k * r_k, HEAD_DIM) * v
    return ((yn + bonus) * g,)


def attn_combine(o1, o2, o3, l1, l2, l3):
    m = jnp.maximum(jnp.maximum(l1, l2), l3)
    e1, e2, e3 = jnp.exp(l1 - m), jnp.exp(l2 - m), jnp.exp(l3 - m)
    return ((o1 * e1 + o2 * e2 + o3 * e3) / (e1 + e2 + e3),)


def ssd_pre(xc, dtr, conv_b, dt_bias, a_log):
    xbc = _silu(xc + conv_b)
    xs, bm, cm = xbc[:, 0:256], xbc[:, 256:512], xbc[:, 512:768]
    dt = _softplus(dtr + dt_bias)
    a_neg = -jnp.exp(a_log)
    wide = _spread(128, N_HEADS * SSD_STATE, SSD_STATE)
    w = _hdot(dt, wide) * _hdot(a_neg, wide)
    xdt = xs * _hdot(dt, _spread(128, D_GROUP, HEAD_DIM))
    rr = jnp.concatenate([cm[:, 0:128], cm[:, 0:128], cm[:, 128:256], cm[:, 128:256]], axis=1)
    kk = jnp.concatenate([bm[:, 0:128], bm[:, 0:128], bm[:, 128:256], bm[:, 128:256]], axis=1)
    return rr, w, kk, xdt, xs


def ssd_post(ys, z, xs, d_skip, norm_w):
    y = ys + xs * _hdot(d_skip, _spread(128, D_GROUP, HEAD_DIM))
    y = y * _silu(z)
    half = D_GROUP // 2
    parts = []
    for g in range(2):
        t = y[:, g * half:(g + 1) * half]
        parts.append(t * lax.rsqrt(jnp.mean(t * t, axis=-1, keepdims=True) + RMS_EPS))
    return (jnp.concatenate(parts, axis=1) * norm_w,)


def hgrn_pre(seg, lb):
    q, f, i = seg[:, 0:256], seg[:, 256:512], seg[:, 512:768]
    forget = lb + (1.0 - lb) * _sigmoid(f)
    return _silu(q), 1.0 - forget, jnp.log(forget), i


def hgrn_post(o, seg, norm_w):
    g = seg[:, 768:1024]
    ms = _group_sum(o * o, HEAD_DIM) * (1.0 / HEAD_DIM)
    return (o * lax.rsqrt(ms + RMS_EPS) * norm_w * _silu(g),)


def ln_res(x, y, w, b):
    z = ALPHA * x + y
    mu = jnp.mean(z, axis=-1, keepdims=True)
    zc = z - mu
    var = jnp.mean(zc * zc, axis=-1, keepdims=True)
    return (zc * lax.rsqrt(var + LN_EPS) * w + b,)


def loss_and_grad(y, tgt, name):
    S, D = y.shape
    tile = ROW_TILE

    def body(y_ref, t_ref, l_ref, dy_ref):
        e = y_ref[...] - t_ref[...]
        dy_ref[...] = e * (1.0 / D)

        @pl.when(pl.program_id(0) == 0)
        def _():
            l_ref[...] = jnp.zeros_like(l_ref)

        per_row = 0.5 * jnp.mean(e * e, axis=-1, keepdims=True)
        l_ref[...] += jnp.sum(per_row, axis=0, keepdims=True) * jnp.ones((1, 128), F32)

    return pl.pallas_call(body, grid=(S // tile,), in_specs=[_row_spec(D, tile)] * 2,
                          out_specs=[_par_spec((1, 128)), _row_spec(D, tile)],
                          out_shape=[jax.ShapeDtypeStruct((1, 128), F32), jax.ShapeDtypeStruct((S, D), F32)],
                          compiler_params=_cparams(("arbitrary",)), name=name)(y, tgt)


def add_rows(arrs, name):
    (out,) = tl_fwd(lambda *a: (functools.reduce(lambda p, q: p + q, a),), name, arrs, [], [arrs[0].shape[1]])
    return out


def small_fwd(fn, name, ins, out_shapes):
    n = len(ins)

    def body(*refs):
        outs = fn(*[r[...] for r in refs[:n]])
        for o_ref, o in zip(refs[n:], outs):
            o_ref[...] = o

    return pl.pallas_call(body, out_shape=[jax.ShapeDtypeStruct(s, F32) for s in out_shapes], name=name)(*ins)


def small_bwd(fn, name, ins, cts):
    n, m = len(ins), len(cts)

    def body(*refs):
        _, vjp = jax.vjp(lambda *a: tuple(fn(*a)), *[r[...] for r in refs[:n]])
        grads = vjp(tuple(r[...] for r in refs[n:n + m]))
        for o_ref, g in zip(refs[n + m:], grads):
            o_ref[...] = g

    return pl.pallas_call(body, out_shape=[jax.ShapeDtypeStruct(a.shape, F32) for a in ins], name=name)(*ins, *cts)


def param_prep(lower_bounds, mu0, mu1):
    e = jnp.exp(lower_bounds - jnp.max(lower_bounds, axis=0, keepdims=True))
    sm = e / jnp.sum(e, axis=0, keepdims=True)
    lb0 = sm[0:1] - sm[0:1]
    lb1 = sm[0:1] + sm[1:2] - sm[0:1]
    return lb0, lb1, mu0, 1.0 - mu0, mu1, 1.0 - mu1


def _rows_tile(rows):
    return _pick(rows, (256, 128, 64, 32, 16, 8))


def sum_parts(parts, name):
    P, rows, cols = parts.shape
    tile = _rows_tile(rows)

    def body(p_ref, o_ref):
        acc = p_ref[0]
        for p in range(1, P):
            acc = acc + p_ref[p]
        o_ref[...] = acc

    return pl.pallas_call(body, grid=(rows // tile,), in_specs=[pl.BlockSpec((P, tile, cols), lambda i: (0, i, 0))],
                          out_specs=pl.BlockSpec((tile, cols), lambda i: (i, 0)),
                          out_shape=jax.ShapeDtypeStruct((rows, cols), F32),
                          compiler_params=_cparams(("parallel",)), name=name)(parts)


def pair_sum(own, got, ids, name):
    _, P, rows, cols = own.shape
    tile = _rows_tile(rows)

    def body(ids_ref, own_ref, got_ref, o_ref):
        o_ref[0] = (own_ref[0, 0] + got_ref[0]).astype(BF16)

    grid_spec = pltpu.PrefetchScalarGridSpec(
        num_scalar_prefetch=1, grid=(P, rows // tile),
        in_specs=[pl.BlockSpec((1, 1, tile, cols), lambda s, i, ids: (ids[0], s, i, 0)),
                  pl.BlockSpec((1, tile, cols), lambda s, i, ids: (s, i, 0))],
        out_specs=pl.BlockSpec((1, tile, cols), lambda s, i, ids: (s, i, 0)))
    return pl.pallas_call(body, grid_spec=grid_spec, out_shape=jax.ShapeDtypeStruct((P, rows, cols), BF16),
                          compiler_params=_cparams(("parallel", "parallel")), name=name)(ids, own, got)


def sum_chips(slots, own, got, ids, name):
    P, rows, cols = slots.shape
    tile = _rows_tile(rows)

    def body(ids_ref, s_ref, own_ref, got_ref, o_ref):
        chip = ids_ref[1]
        mine = own_ref[0, 0] + got_ref[0]
        acc = None
        for p in range(P):
            term = jnp.where(chip == p, mine, s_ref[p].astype(F32))
            acc = term if acc is None else acc + term
        o_ref[0] = acc

    grid_spec = pltpu.PrefetchScalarGridSpec(
        num_scalar_prefetch=1, grid=(rows // tile,),
        in_specs=[pl.BlockSpec((P, tile, cols), lambda i, ids: (0, i, 0)),
                  pl.BlockSpec((1, 1, tile, cols), lambda i, ids: (ids[0], ids[1], i, 0)),
                  pl.BlockSpec((1, tile, cols), lambda i, ids: (ids[1], i, 0))],
        out_specs=pl.BlockSpec((1, tile, cols), lambda i, ids: (ids[0], i, 0)))
    return pl.pallas_call(body, grid_spec=grid_spec, out_shape=jax.ShapeDtypeStruct((2, rows, cols), F32),
                          compiler_params=_cparams(("parallel",)), name=name)(ids, slots, own, got)


def adamw(w, g, m, v, name):
    layers, rows, cols = w.shape
    tile = _rows_tile(rows)

    def body(w_ref, g_ref, m_ref, v_ref, d_ref, nm_ref, nv_ref):
        gv = g_ref[...]
        nm = ADAM_B1 * m_ref[...] + (1.0 - ADAM_B1) * gv
        nv = ADAM_B2 * v_ref[...] + (1.0 - ADAM_B2) * jnp.square(gv)
        m_hat = nm / (1.0 - ADAM_B1 ** ADAM_STEP)
        v_hat = nv / (1.0 - ADAM_B2 ** ADAM_STEP)
        d_ref[...] = -ADAM_LR * (m_hat / (jnp.sqrt(v_hat) + ADAM_EPS) + ADAM_WD * w_ref[...])
        nm_ref[...] = nm
        nv_ref[...] = nv

    spec = pl.BlockSpec((None, tile, cols), lambda l, i: (l, i, 0))
    return pl.pallas_call(body, grid=(layers, rows // tile), in_specs=[spec] * 4, out_specs=[spec] * 3,
                          out_shape=[jax.ShapeDtypeStruct((layers, rows, cols), F32)] * 3,
                          compiler_params=_cparams(("parallel", "parallel")), name=name)(w, g, m, v)


MESH = pl.DeviceIdType.MESH
ANY = pl.BlockSpec(memory_space=pl.ANY)


def _flip(v, bit):
    return 1 - v if bit else v


_CHIP_RELATIONS = ((1, 0), (0, 1), (1, 1))


def gather_chips(arrs, small, name):
    n = len(arrs)

    def body(*refs):
        ins, small_in = refs[:n], refs[n]
        outs, small_out = refs[n + 1:2 * n + 1], refs[2 * n + 1]
        send, recv, fsend, frecv, ssend, srecv = refs[2 * n + 2:]
        x, y, c = lax.axis_index("x"), lax.axis_index("y"), lax.axis_index("c")
        me = 2 * x + y
        chips = [(_flip(x, bx), _flip(y, by)) for bx, by in _CHIP_RELATIONS]

        def over_ici(i, r, block_chip):
            return pltpu.make_async_remote_copy(src_ref=ins[i].at[c], dst_ref=outs[i].at[block_chip, c],
                                                send_sem=send.at[i, r], recv_sem=recv.at[i, r],
                                                device_id=(chips[r][0], chips[r][1], c), device_id_type=MESH)

        def to_sibling(i, r, layer):
            blk = outs[i].at[2 * chips[r][0] + chips[r][1], layer]
            return pltpu.make_async_remote_copy(src_ref=blk, dst_ref=blk, send_sem=fsend.at[i, r],
                                                recv_sem=frecv.at[i, r], device_id=(x, y, 1 - c), device_id_type=MESH)

        first = [over_ici(i, r, me) for i in range(n) for r in range(3)]
        smalls = [pltpu.make_async_remote_copy(src_ref=small_in, dst_ref=small_out.at[me], send_sem=ssend.at[r],
                                               recv_sem=srecv.at[r], device_id=(chips[r][0], chips[r][1], c),
                                               device_id_type=MESH) for r in range(3)]
        for cp in first + smalls:
            cp.start()
        passed = []
        for r in range(3):
            for i in range(n):
                over_ici(i, r, 2 * chips[r][0] + chips[r][1]).wait_recv()
                fw = to_sibling(i, r, c)
                fw.start()
                passed.append(fw)
        for r in range(3):
            for i in range(n):
                to_sibling(i, r, 1 - c).wait_recv()
        for cp in first + passed:
            cp.wait_send()
        for cp in smalls:
            cp.wait()

    return pl.pallas_call(
        body, in_specs=[ANY] * (n + 1), out_specs=[ANY] * (n + 1),
        out_shape=[jax.ShapeDtypeStruct((4,) + a.shape, a.dtype) for a in arrs]
        + [jax.ShapeDtypeStruct((4,) + small.shape, small.dtype)],
        scratch_shapes=[pltpu.SemaphoreType.DMA((n, 3)), pltpu.SemaphoreType.DMA((n, 3)), pltpu.SemaphoreType.DMA((n, 3)),
                        pltpu.SemaphoreType.DMA((n, 3)), pltpu.SemaphoreType.DMA((3,)), pltpu.SemaphoreType.DMA((3,))],
        name=name)(*arrs, small)


_RELATIONS = tuple((r >> 2 & 1, r >> 1 & 1, r & 1) for r in range(1, 8))


def gather_devices(arr, name):
    def body(in_ref, out_ref, send, recv, loc):
        x, y, c = lax.axis_index("x"), lax.axis_index("y"), lax.axis_index("c")
        me = 4 * x + 2 * y + c
        lc = pltpu.make_async_copy(in_ref, out_ref.at[me], loc)
        lc.start()
        pending = [lc]
        for r, (bx, by, bc) in enumerate(_RELATIONS):
            cp = pltpu.make_async_remote_copy(src_ref=in_ref, dst_ref=out_ref.at[me], send_sem=send.at[r],
                                              recv_sem=recv.at[r], device_id=(_flip(x, bx), _flip(y, by), _flip(c, bc)),
                                              device_id_type=MESH)
            cp.start()
            pending.append(cp)
        for cp in pending:
            cp.wait()

    return pl.pallas_call(
        body, in_specs=[ANY], out_specs=ANY, out_shape=jax.ShapeDtypeStruct((8,) + arr.shape, arr.dtype),
        scratch_shapes=[pltpu.SemaphoreType.DMA((7,)), pltpu.SemaphoreType.DMA((7,)), pltpu.SemaphoreType.DMA(())],
        name=name)(arr)


def pair_exchange(arrs, name):
    n = len(arrs)

    def body(*refs):
        ins, outs = refs[:n], refs[n:2 * n]
        send, recv = refs[2 * n:]
        x, y, c = lax.axis_index("x"), lax.axis_index("y"), lax.axis_index("c")
        pending = []
        for i in range(n):
            for s in range(4):
                cp = pltpu.make_async_remote_copy(src_ref=ins[i].at[1 - c, s], dst_ref=outs[i].at[s],
                                                  send_sem=send.at[i, s], recv_sem=recv.at[i, s],
                                                  device_id=(x, y, 1 - c), device_id_type=MESH)
                cp.start()
                pending.append(cp)
        for cp in pending:
            cp.wait()

    return pl.pallas_call(
        body, in_specs=[ANY] * n, out_specs=[ANY] * n,
        out_shape=[jax.ShapeDtypeStruct(a.shape[1:], a.dtype) for a in arrs],
        scratch_shapes=[pltpu.SemaphoreType.DMA((n, 4)), pltpu.SemaphoreType.DMA((n, 4))],
        name=name)(*arrs)


def reduce_chips(arrs, name):
    n = len(arrs)

    def body(*refs):
        ins, outs = refs[:n], refs[n:2 * n]
        send, recv, loc = refs[2 * n:]
        x, y, c = lax.axis_index("x"), lax.axis_index("y"), lax.axis_index("c")
        me = 2 * x + y
        pending = []
        for i in range(n):
            for r, (bx, by) in enumerate(_CHIP_RELATIONS):
                px, py = _flip(x, bx), _flip(y, by)
                cp = pltpu.make_async_remote_copy(src_ref=ins[i].at[2 * px + py], dst_ref=outs[i].at[me],
                                                  send_sem=send.at[i, r], recv_sem=recv.at[i, r],
                                                  device_id=(px, py, c), device_id_type=MESH)
                cp.start()
                pending.append(cp)
        for i in range(n):
            lc = pltpu.make_async_copy(ins[i].at[me], outs[i].at[me], loc.at[i])
            lc.start()
            pending.append(lc)
        for cp in pending:
            cp.wait()

    return pl.pallas_call(
        body, in_specs=[ANY] * n, out_specs=[ANY] * n,
        out_shape=[jax.ShapeDtypeStruct(a.shape, a.dtype) for a in arrs],
        scratch_shapes=[pltpu.SemaphoreType.DMA((n, 3)), pltpu.SemaphoreType.DMA((n, 3)), pltpu.SemaphoreType.DMA((n,))],
        name=name)(*arrs)


EXCHANGE_PIECES = 8


def sibling_exchange(arrs, name):
    n = len(arrs)

    def body(*refs):
        bufs = refs[n:2 * n]
        send, recv = refs[2 * n:]
        x, y, c = lax.axis_index("x"), lax.axis_index("y"), lax.axis_index("c")
        pending = []
        for i in range(n):
            rows = bufs[i].shape[1] // EXCHANGE_PIECES
            for j in range(EXCHANGE_PIECES):
                piece = bufs[i].at[c, pl.ds(j * rows, rows)]
                cp = pltpu.make_async_remote_copy(src_ref=piece, dst_ref=piece, send_sem=send.at[i, j],
                                                  recv_sem=recv.at[i, j], device_id=(x, y, 1 - c), device_id_type=MESH)
                cp.start()
                pending.append(cp)
        for i in range(n):
            rows = bufs[i].shape[1] // EXCHANGE_PIECES
            for j in range(EXCHANGE_PIECES):
                landed = bufs[i].at[1 - c, pl.ds(j * rows, rows)]
                pltpu.make_async_remote_copy(src_ref=landed, dst_ref=landed, send_sem=send.at[i, j], recv_sem=recv.at[i, j],
                                             device_id=(x, y, 1 - c), device_id_type=MESH).wait_recv()
        for cp in pending:
            cp.wait_send()

    return pl.pallas_call(
        body, in_specs=[ANY] * n, out_specs=[ANY] * n,
        out_shape=[jax.ShapeDtypeStruct(a.shape, a.dtype) for a in arrs], input_output_aliases={i: i for i in range(n)},
        scratch_shapes=[pltpu.SemaphoreType.DMA((n, EXCHANGE_PIECES)), pltpu.SemaphoreType.DMA((n, EXCHANGE_PIECES))],
        name=name)(*arrs)


def rwkv_fwd(l, seg, taps, pars, vfirst):
    fs = fir_fwd(seg, taps, f"rwkv_shift_fwd{l}")
    rows = [fs] + ([vfirst] if l else [])
    R, W, K, V, KK, A, G = tl_fwd(rwkv_pre(l), f"rwkv_pre_fwd{l}", rows, pars["pre"],
                                  [HM64, HM64, HM64, D_GROUP, HM64, HM64, D_GROUP])
    Y, sall = scan_fwd(R, W, K, V, KK, A, f"rwkv_scan_fwd{l}")
    (out,) = tl_fwd(rwkv_post, f"rwkv_post_fwd{l}", [Y, R, K, V, G], pars["post"], [D_GROUP])
    return out, V, (seg, taps, rows, R, W, K, V, KK, A, G, Y, sall)


def rwkv_bwd(l, saved, pars, dout, dv_extra):
    seg, taps, rows, R, W, K, V, KK, A, G, Y, sall = saved
    (dY, dR1, dK1, dV1, dG), dpost = tl_bwd(rwkv_post, f"rwkv_post_bwd{l}", [Y, R, K, V, G], pars["post"], [[dout]])
    (dR2, dW, dK2, dKK, dA), dV2 = scan_bwd(R, W, K, V, KK, A, sall, dY, f"rwkv_scan_bwd{l}")
    cts = [[dR1, dR2], [dW], [dK1, dK2], [dV1, dV2] + dv_extra, [dKK], [dA], [dG]]
    drows, dpre = tl_bwd(rwkv_pre(l), f"rwkv_pre_bwd{l}", rows, pars["pre"], cts)
    dseg, dtaps, _ = fir_bwd(seg, taps, [drows[0]], f"rwkv_shift_bwd{l}")
    return dseg, (drows[1] if l else None), dtaps, dpre, dpost


def attn_mix_fwd(l, proj):
    os_, ls_ = [], []
    for b, (_, d) in enumerate(DILATED_BRANCHES):
        o, lse = attn_fwd(proj, d, f"attn_fwd{l}_{b}")
        os_.append(o)
        ls_.append(lse)
    (out,) = tl_fwd(attn_combine, f"attn_combine_fwd{l}", os_ + ls_, [], [D_GROUP])
    return out, (proj, os_, ls_)


def attn_mix_bwd(l, saved, dout):
    proj, os_, ls_ = saved
    drows, _ = tl_bwd(attn_combine, f"attn_combine_bwd{l}", os_ + ls_, [], [[dout]])
    grads = [attn_bwd(proj, drows[b], drows[3 + b], d, f"attn_bwd{l}_{b}") for b, (_, d) in enumerate(DILATED_BRANCHES)]
    return tuple(add_rows([g[j] for g in grads], f"attn_d{'qkv'[j]}{l}") for j in range(3))


def ssd_fwd(l, z, xbc, dtr, pars):
    xc = fir_fwd(xbc, pars["taps"], f"ssd_conv_fwd{l}")
    rr, w, kk, xdt, xs = tl_fwd(ssd_pre, f"ssd_pre_fwd{l}", [xc, dtr], pars["pre"], [HM128, HM128, HM128, HM64, D_GROUP])
    blocks = [rr, kk, w, xdt]
    ys, states = chunk_fwd(ssd_chunk, f"ssd_scan_fwd{l}", blocks, (HEAD_DIM, SSD_STATE), HEAD_DIM, SSD_GROUP)
    (out,) = tl_fwd(ssd_post, f"ssd_post_fwd{l}", [ys, z, xs], pars["post"], [D_GROUP])
    return out, (z, xbc, dtr, xc, blocks, states, xs, ys)


def ssd_bwd(l, saved, pars, dout):
    z, xbc, dtr, xc, blocks, states, xs, ys = saved
    (dys, dz, dxs), dpost = tl_bwd(ssd_post, f"ssd_post_bwd{l}", [ys, z, xs], pars["post"], [[dout]])
    drr, dkk, dw, dxdt = chunk_bwd(ssd_chunk, f"ssd_scan_bwd{l}", blocks, states, dys, SSD_GROUP)
    (dxc, ddtr), dpre = tl_bwd(ssd_pre, f"ssd_pre_bwd{l}", [xc, dtr], pars["pre"], [[drr], [dw], [dkk], [dxdt], [dxs]])
    dxbc, dtaps, _ = fir_bwd(xbc, pars["taps"], [dxc], f"ssd_conv_bwd{l}")
    return dz, dxbc, ddtr, dtaps, dpre, dpost


def hgrn_fwd(l, seg, pars):
    blocks = tl_fwd(hgrn_pre, f"hgrn_pre_fwd{l}", [seg], pars["pre"], [HM64] * 4)
    o, states = chunk_fwd(hgrn_chunk, f"hgrn_scan_fwd{l}", blocks, (HEAD_DIM, HEAD_DIM), HEAD_DIM, HGRN_GROUP)
    (out,) = tl_fwd(hgrn_post, f"hgrn_post_fwd{l}", [o, seg], pars["post"], [D_GROUP])
    return out, (seg, blocks, states, o)


def hgrn_bwd(l, saved, pars, dout):
    seg, blocks, states, o = saved
    (do, dseg1), dpost = tl_bwd(hgrn_post, f"hgrn_post_bwd{l}", [o, seg], pars["post"], [[dout]])
    dq, dkk, dlf, di = chunk_bwd(hgrn_chunk, f"hgrn_scan_bwd{l}", blocks, states, do, HGRN_GROUP)
    (dseg2,), dpre = tl_bwd(hgrn_pre, f"hgrn_pre_bwd{l}", [seg], pars["pre"], [[dq], [dkk], [dlf], [di]])
    return add_rows([dseg1, dseg2], f"hgrn_dseg{l}"), dpre, dpost


def layer_fwd(l, x, wts, pars, vfirst):
    proj = matmul(x, wts["in"], "nn", f"proj_fwd{l}")
    seg_h, seg_r = Cols(proj, SEG_HGRN, 1024), Cols(proj, SEG_RWKV, 1024)
    z, xbc, dtr = Cols(proj, SEG_Z, D_GROUP), Cols(proj, SEG_XBC, SSD_XBC), Cols(proj, SEG_DT, 128)
    ya, v_rwkv, sa = rwkv_fwd(l, seg_r, pars["rwkv"]["taps"], pars["rwkv"], vfirst)
    yb, sb = attn_mix_fwd(l, proj)
    yc, sc = ssd_fwd(l, z, xbc, dtr, pars["ssd"])
    yd, sd = hgrn_fwd(l, seg_h, pars["hgrn"])
    mix = [ya, yb, yc, yd]
    mo = matmul(mix, wts["out"], "nn", f"out_fwd{l}")
    (x1,) = tl_fwd(ln_res, f"ln1_fwd{l}", [x, mo], pars["ln1"], [D_MODEL])
    u = matmul(x1, wts["up"], "nn", f"up_fwd{l}")
    dn = matmul(u, wts["down"], "nn", f"down_fwd{l}", a_relu2=True)
    (x2,) = tl_fwd(ln_res, f"ln2_fwd{l}", [x1, dn], pars["ln2"], [D_MODEL])
    return x2, v_rwkv, (x, sa, sb, sc, sd, mix, mo, x1, u, dn)


def layer_bwd(l, saved, wts, pars, dx2, dv_extra):
    x, sa, sb, sc, sd, mix, mo, x1, u, dn = saved
    S = x.shape[0]
    g = {}
    (dx1a, ddn), g["ln2"] = tl_bwd(ln_res, f"ln2_bwd{l}", [x1, dn], pars["ln2"], [[dx2]])
    g["down"] = matmul(u, ddn, "tn", f"down_dw{l}", a_relu2=True)
    du = matmul(ddn, wts["down"], "nt", f"down_dx{l}", relu2_grad_of=u)
    g["up"] = matmul(x1, du, "tn", f"up_dw{l}", out_col_shards=True)
    dx1 = matmul(du, wts["up"], "nt", f"up_dx{l}", add=dx1a)
    (dxa, dmo), g["ln1"] = tl_bwd(ln_res, f"ln1_bwd{l}", [x, mo], pars["ln1"], [[dx1]])
    g["out"] = matmul(mix, dmo, "tn", f"out_dw{l}")
    dmix = matmul(dmo, wts["out"], "nt", f"out_dx{l}")
    dya, dyb, dyc, dyd = (Cols(dmix, j * D_GROUP, D_GROUP) for j in range(4))
    dseg_r, dvfirst, g["rwkv_taps"], g["rwkv_pre"], g["rwkv_post"] = rwkv_bwd(l, sa, pars["rwkv"], dya, dv_extra)
    dq, dk, dv = attn_mix_bwd(l, sb, dyb)
    dz, dxbc, ddtr, g["ssd_taps"], g["ssd_pre"], g["ssd_post"] = ssd_bwd(l, sc, pars["ssd"], dyc)
    dseg_h, g["hgrn_pre"], g["hgrn_post"] = hgrn_bwd(l, sd, pars["hgrn"], dyd)
    dproj = jnp.concatenate([dseg_h, dseg_r, dq, dk, dv, dz, dxbc, ddtr, jnp.zeros((S, PROJ_W - SEG_DT - 128), F32)], axis=1)
    g["in"] = matmul(x, dproj, "tn", f"proj_dw{l}")
    dx = matmul(dproj, wts["in"], "nt", f"proj_dx{l}", add=dxa)
    return dx, dvfirst, g


SMALL = ("lower_bounds", "w_in_vres", "mu_shift", "mu_vres", "rwkv_w0", "rwkv_w2", "rwkv_a0", "rwkv_a2", "rwkv_g2",
         "rwkv_k_k", "rwkv_k_a", "rwkv_r_k", "rwkv_lnx_w", "rwkv_lnx_b", "rwkv_v0", "rwkv_v2", "ssd_conv_w",
         "ssd_conv_b", "ssd_dt_bias", "ssd_A_log", "ssd_D", "ssd_norm_w", "hgrn_norm_w", "ln1_w", "ln1_b", "ln2_w", "ln2_b")
BIG = ("w_in", "w_out", "w_up", "w_down")
SMALL_SHARDED = {"w_in_vres": 1, "rwkv_w2": 2, "rwkv_a2": 2, "rwkv_g2": 2, "rwkv_v2": 2, "ssd_conv_w": 2}
WEIGHTS = ("lower_bounds", "w_in", "w_in_vres", "mu_shift", "mu_vres", "rwkv_w0", "rwkv_w2", "rwkv_a0", "rwkv_a2",
           "rwkv_g2", "rwkv_k_k", "rwkv_k_a", "rwkv_r_k", "rwkv_lnx_w", "rwkv_lnx_b", "rwkv_v0", "rwkv_v2",
           "ssd_conv_w", "ssd_conv_b", "ssd_dt_bias", "ssd_A_log", "ssd_D", "ssd_norm_w", "hgrn_norm_w", "w_out",
           "ln1_w", "ln1_b", "w_up", "w_down", "ln2_w", "ln2_b")


def _row(v, width=None):
    v = v.reshape(1, -1).astype(F32)
    if width is not None and v.shape[1] < width:
        v = jnp.pad(v, ((0, 0), (0, width - v.shape[1])))
    return v


def _rows_at(m, rows, at):
    return jnp.pad(m.astype(F32), ((at, rows - at - m.shape[0]), (0, 0)))


SHARD_COLS = IN_COLS // 4


def _shard_runs(start, width):
    runs, pos = [], start
    while pos < start + width:
        s = pos // SHARD_COLS
        end = min(start + width, (s + 1) * SHARD_COLS)
        runs.append((s, pos - s * SHARD_COLS, end - s * SHARD_COLS))
        pos = end
    return runs


def _pad_w_in(shards, vres):
    rows = shards.shape[1]
    out, pos = [], 0
    for start, width, at in sorted(_PIECES, key=lambda p: p[2]):
        if at > pos:
            out.append(jnp.zeros((rows, at - pos), shards.dtype))
        out += [shards[s, :, lo:hi] for s, lo, hi in _shard_runs(start, width)]
        pos = at + width
        if at == SEG_RWKV and vres is not None:
            out.append(vres.astype(shards.dtype))
            pos += vres.shape[1]
    out.append(jnp.zeros((rows, PROJ_W - pos), shards.dtype))
    return jnp.concatenate(out, axis=1)


def _unpad_w_in(g):
    shards = [[] for _ in range(4)]
    for start, width, at in _PIECES:
        for s, lo, hi in _shard_runs(start, width):
            first = at + s * SHARD_COLS + lo - start
            shards[s].append(g[:, first:first + hi - lo])
    return jnp.stack([jnp.concatenate(p, axis=1) for p in shards])


def layer_params(l, sp, prep):
    lb, mu, om = prep[l], prep[2 + 2 * l], prep[3 + 2 * l]
    pre = [_row(sp["rwkv_w0"][l]), _rows_at(sp["rwkv_w2"][l], 128, 0), _row(sp["rwkv_a0"][l]),
           _rows_at(sp["rwkv_a2"][l], 128, 32), _rows_at(sp["rwkv_g2"][l], 128, 64),
           _row(sp["rwkv_k_k"][l]), _row(sp["rwkv_k_a"][l])]
    if l:
        pre += [_row(sp["rwkv_v0"][l - 1]), _rows_at(sp["rwkv_v2"][l - 1], 128, 0)]
    return {
        "rwkv": {"taps": jnp.concatenate([mu, om], axis=0), "pre": pre,
                 "post": [_row(sp["rwkv_lnx_w"][l]), _row(sp["rwkv_lnx_b"][l]), _row(sp["rwkv_r_k"][l])]},
        "ssd": {"taps": sp["ssd_conv_w"][l].astype(F32),
                "pre": [_row(sp["ssd_conv_b"][l]), _row(sp["ssd_dt_bias"][l], 128), _row(sp["ssd_A_log"][l], 128)],
                "post": [_row(sp["ssd_D"][l], 128), _row(sp["ssd_norm_w"][l])]},
        "hgrn": {"pre": [lb], "post": [_row(sp["hgrn_norm_w"][l])]},
        "ln1": [_row(sp["ln1_w"][l]), _row(sp["ln1_b"][l])],
        "ln2": [_row(sp["ln2_w"][l]), _row(sp["ln2_b"][l])],
    }


def _mu_full(sp, l):
    parts = [sp["mu_shift"][l].reshape(1, -1)]
    if l:
        parts.append(sp["mu_vres"][l - 1].reshape(1, -1))
    return _row(jnp.concatenate(parts, axis=1), 1024)


def local_step(x, target, big, sp):
    prep_in = [sp["lower_bounds"].astype(F32), _mu_full(sp, 0), _mu_full(sp, 1)]
    prep = small_fwd(param_prep, "param_prep_fwd", prep_in,
                     [(1, D_GROUP), (1, D_GROUP), (1, 1024), (1, 1024), (1, 1024), (1, 1024)])
    pars, wts = [], []
    for l in range(DEPTH):
        pars.append(layer_params(l, sp, prep))
        vres = sp["w_in_vres"][l - 1].astype(BF16) if l else None
        wts.append({"in": _pad_w_in(big["w_in"][:, l], vres), "out": big["w_out"][l],
                    "up": ShardedWeight(big["w_up"], l, 1), "down": ShardedWeight(big["w_down"], l, 0)})
    h, vfirst, saved = x, None, []
    for l in range(DEPTH):
        h, v_l, sv = layer_fwd(l, h, wts[l], pars[l], vfirst)
        vfirst = v_l if l == 0 else vfirst
        saved.append(sv)
    loss_row, dh = loss_and_grad(h, target, "loss")
    grads, dv_extra = [None] * DEPTH, []
    for l in reversed(range(DEPTH)):
        dh, dvfirst, grads[l] = layer_bwd(l, saved[l], wts[l], pars[l], dh, dv_extra)
        dv_extra = [dvfirst] if l else []
    cts = [grads[0]["hgrn_pre"][0], grads[1]["hgrn_pre"][0]]
    for l in range(DEPTH):
        cts += [grads[l]["rwkv_taps"][0:1], grads[l]["rwkv_taps"][1:2]]
    d_lower, d_mu0, d_mu1 = small_bwd(param_prep, "param_prep_bwd", prep_in, cts)
    d_mu = [d_mu0, d_mu1]
    gb = {"w_in": [_unpad_w_in(grads[l]["in"]) for l in range(DEPTH)], "w_out": [grads[l]["out"] for l in range(DEPTH)],
          "w_up": [grads[l]["up"] for l in range(DEPTH)], "w_down": [grads[l]["down"] for l in range(DEPTH)]}
    st = lambda f: jnp.stack([f(l) for l in range(DEPTH)])
    g1 = grads[1]
    gs = {
        "lower_bounds": d_lower,
        "w_in_vres": g1["in"][None, :, VRES_COL:VRES_COL + 32],
        "mu_shift": st(lambda l: d_mu[l][0, :896]),
        "mu_vres": d_mu[1][:, 896:928],
        "rwkv_w0": st(lambda l: grads[l]["rwkv_pre"][0][0]),
        "rwkv_w2": st(lambda l: grads[l]["rwkv_pre"][1][0:32]),
        "rwkv_a0": st(lambda l: grads[l]["rwkv_pre"][2][0]),
        "rwkv_a2": st(lambda l: grads[l]["rwkv_pre"][3][32:64]),
        "rwkv_g2": st(lambda l: grads[l]["rwkv_pre"][4][64:128]),
        "rwkv_k_k": st(lambda l: grads[l]["rwkv_pre"][5][0]),
        "rwkv_k_a": st(lambda l: grads[l]["rwkv_pre"][6][0]),
        "rwkv_r_k": st(lambda l: grads[l]["rwkv_post"][2].reshape(N_HEADS, HEAD_DIM)),
        "rwkv_lnx_w": st(lambda l: grads[l]["rwkv_post"][0][0]),
        "rwkv_lnx_b": st(lambda l: grads[l]["rwkv_post"][1][0]),
        "rwkv_v0": g1["rwkv_pre"][7],
        "rwkv_v2": g1["rwkv_pre"][8][None, 0:32],
        "ssd_conv_w": st(lambda l: grads[l]["ssd_taps"]),
        "ssd_conv_b": st(lambda l: grads[l]["ssd_pre"][0][0]),
        "ssd_dt_bias": st(lambda l: grads[l]["ssd_pre"][1][0, :N_HEADS]),
        "ssd_A_log": st(lambda l: grads[l]["ssd_pre"][2][0, :N_HEADS]),
        "ssd_D": st(lambda l: grads[l]["ssd_post"][0][0, :N_HEADS]),
        "ssd_norm_w": st(lambda l: grads[l]["ssd_post"][1][0]),
        "hgrn_norm_w": st(lambda l: grads[l]["hgrn_post"][0][0]),
        "ln1_w": st(lambda l: grads[l]["ln1"][0][0]),
        "ln1_b": st(lambda l: grads[l]["ln1"][1][0]),
        "ln2_w": st(lambda l: grads[l]["ln2"][0][0]),
        "ln2_b": st(lambda l: grads[l]["ln2"][1][0]),
    }
    return loss_row, dh, gb, gs


def _pack(vecs):
    parts, meta, row = [], [], 0
    for v in vecs:
        rows = -(-v.size // 1024) * 8
        flat = v.reshape(-1).astype(F32)
        parts.append(jnp.pad(flat, (0, rows * 128 - v.size)).reshape(rows, 128))
        meta.append((row, v.shape))
        row += rows
    return jnp.concatenate(parts, axis=0), meta


def _unpack(packed, meta):
    out = []
    for row, shape in meta:
        size = math.prod(shape)
        rows = -(-size // 1024) * 8
        out.append(packed[row:row + rows].reshape(-1)[:size].reshape(shape))
    return out


def _to_shards(name, g):
    if name in ("w_in", "w_up"):
        return g
    return g.reshape(4, g.shape[0] // 4, g.shape[1])


def _from_chips(name, g):
    if name != "w_out":
        return g
    return jnp.transpose(g, (1, 0, 2, 3)).reshape(g.shape[1], 4 * g.shape[2], g.shape[3])


INPUT_NAMES = ("x",) + WEIGHTS + ("loss_target",) + tuple("m_" + n for n in WEIGHTS) + tuple("v_" + n for n in WEIGHTS)


def _step(*args):
    a = dict(zip(INPUT_NAMES, args, strict=True))
    chip = 2 * lax.axis_index("x") + lax.axis_index("y")

    sharded_names = list(SMALL_SHARDED)
    small_pack, small_meta = _pack([a[n] for n in sharded_names])
    own = [a[n].astype(BF16) for n in BIG]
    gathered = gather_chips(own, small_pack, "gather_weights")
    here = lambda full, mine: lax.dynamic_update_slice(full, mine[None], (chip,) + (0,) * mine.ndim)
    big = {n: _from_chips(n, here(g, o)) for n, g, o in zip(BIG, gathered, own)}
    sp = {n: a[n] for n in SMALL if n not in SMALL_SHARDED}
    small_all = here(gathered[-1], small_pack)
    per_chip = [_unpack(small_all[s], small_meta) for s in range(4)]
    for j, n in enumerate(sharded_names):
        sp[n] = jnp.concatenate([per_chip[s][j] for s in range(4)], axis=SMALL_SHARDED[n])

    loss_row, gx, gb, gs = local_step(a["x"][0], a["loss_target"][0], big, sp)

    partials = [jnp.stack([_to_shards(n, gb[n][l]) for l in range(DEPTH)]) for n in BIG]
    got = pair_exchange(partials, "pair_exchange")
    ids = jnp.stack([lax.axis_index("c"), chip]).astype(jnp.int32)
    chip_sums = [pair_sum(p, q, ids, f"pair_sum_{n}") for n, p, q in zip(BIG, partials, got)]
    slots = reduce_chips(chip_sums, "reduce_big")
    mine = [sum_chips(sl, p, q, ids, f"sum_{n}") for n, sl, p, q in zip(BIG, slots, partials, got)]
    summed = sibling_exchange(mine, "exchange_big")
    out_g, out_d, out_m, out_v = {}, {}, {}, {}
    for n, g in zip(BIG, summed):
        out_g[n] = g
        out_d[n], out_m[n], out_v[n] = adamw(a[n], g, a["m_" + n], a["v_" + n], f"adamw_{n}")

    vec, meta = _pack([loss_row] + [gs[n] for n in SMALL])
    total = sum_parts(gather_devices(vec, "gather_small"), "sum_small")
    parts = _unpack(total, meta)
    loss = parts[0][0, 0]
    g_small = {}
    for n, g in zip(SMALL, parts[1:]):
        if n in SMALL_SHARDED:
            ax = SMALL_SHARDED[n]
            size = a[n].shape[ax]
            g = lax.dynamic_slice_in_dim(g, chip * size, size, axis=ax)
        g_small[n] = g
    pw, pmeta = _pack([a[n] for n in SMALL])
    pg, _ = _pack([g_small[n] for n in SMALL])
    pm, _ = _pack([a["m_" + n] for n in SMALL])
    pv, _ = _pack([a["v_" + n] for n in SMALL])
    d, nm, nv = adamw(pw[None], pg[None], pm[None], pv[None], "adamw_small")
    for n, dd, mm, vv in zip(SMALL, _unpack(d[0], pmeta), _unpack(nm[0], pmeta), _unpack(nv[0], pmeta)):
        out_g[n], out_d[n], out_m[n], out_v[n] = g_small[n], dd, mm, vv

    return (loss, gx[None], *[out_g[n] for n in WEIGHTS], *[out_d[n] for n in WEIGHTS],
            *[out_m[n] for n in WEIGHTS], *[out_v[n] for n in WEIGHTS])


def kernel(x, lower_bounds, w_in, w_in_vres, mu_shift, mu_vres, rwkv_w0, rwkv_w2, rwkv_a0, rwkv_a2, rwkv_g2, rwkv_k_k, rwkv_k_a, rwkv_r_k, rwkv_lnx_w, rwkv_lnx_b, rwkv_v0, rwkv_v2, ssd_conv_w, ssd_conv_b, ssd_dt_bias, ssd_A_log, ssd_D, ssd_norm_w, hgrn_norm_w, w_out, ln1_w, ln1_b, w_up, w_down, ln2_w, ln2_b, loss_target, m_lower_bounds, m_w_in, m_w_in_vres, m_mu_shift, m_mu_vres, m_rwkv_w0, m_rwkv_w2, m_rwkv_a0, m_rwkv_a2, m_rwkv_g2, m_rwkv_k_k, m_rwkv_k_a, m_rwkv_r_k, m_rwkv_lnx_w, m_rwkv_lnx_b, m_rwkv_v0, m_rwkv_v2, m_ssd_conv_w, m_ssd_conv_b, m_ssd_dt_bias, m_ssd_A_log, m_ssd_D, m_ssd_norm_w, m_hgrn_norm_w, m_w_out, m_ln1_w, m_ln1_b, m_w_up, m_w_down, m_ln2_w, m_ln2_b, v_lower_bounds, v_w_in, v_w_in_vres, v_mu_shift, v_mu_vres, v_rwkv_w0, v_rwkv_w2, v_rwkv_a0, v_rwkv_a2, v_rwkv_g2, v_rwkv_k_k, v_rwkv_k_a, v_rwkv_r_k, v_rwkv_lnx_w, v_rwkv_lnx_b, v_rwkv_v0, v_rwkv_v2, v_ssd_conv_w, v_ssd_conv_b, v_ssd_dt_bias, v_ssd_A_log, v_ssd_D, v_ssd_norm_w, v_hgrn_norm_w, v_w_out, v_ln1_w, v_ln1_b, v_w_up, v_w_down, v_ln2_w, v_ln2_b):
    return _step(x, lower_bounds, w_in, w_in_vres, mu_shift, mu_vres, rwkv_w0, rwkv_w2, rwkv_a0, rwkv_a2, rwkv_g2, rwkv_k_k, rwkv_k_a, rwkv_r_k, rwkv_lnx_w, rwkv_lnx_b, rwkv_v0, rwkv_v2, ssd_conv_w, ssd_conv_b, ssd_dt_bias, ssd_A_log, ssd_D, ssd_norm_w, hgrn_norm_w, w_out, ln1_w, ln1_b, w_up, w_down, ln2_w, ln2_b, loss_target, m_lower_bounds, m_w_in, m_w_in_vres, m_mu_shift, m_mu_vres, m_rwkv_w0, m_rwkv_w2, m_rwkv_a0, m_rwkv_a2, m_rwkv_g2, m_rwkv_k_k, m_rwkv_k_a, m_rwkv_r_k, m_rwkv_lnx_w, m_rwkv_lnx_b, m_rwkv_v0, m_rwkv_v2, m_ssd_conv_w, m_ssd_conv_b, m_ssd_dt_bias, m_ssd_A_log, m_ssd_D, m_ssd_norm_w, m_hgrn_norm_w, m_w_out, m_ln1_w, m_ln1_b, m_w_up, m_w_down, m_ln2_w, m_ln2_b, v_lower_bounds, v_w_in, v_w_in_vres, v_mu_shift, v_mu_vres, v_rwkv_w0, v_rwkv_w2, v_rwkv_a0, v_rwkv_a2, v_rwkv_g2, v_rwkv_k_k, v_rwkv_k_a, v_rwkv_r_k, v_rwkv_lnx_w, v_rwkv_lnx_b, v_rwkv_v0, v_rwkv_v2, v_ssd_conv_w, v_ssd_conv_b, v_ssd_dt_bias, v_ssd_A_log, v_ssd_D, v_ssd_norm_w, v_hgrn_norm_w, v_w_out, v_ln1_w, v_ln1_b, v_w_up, v_w_down, v_ln2_w, v_ln2_b)
```

```python
import functools
import math

import jax
import jax.numpy as jnp
from jax import lax
from jax.experimental import pallas as pl
from jax.experimental.pallas import tpu as pltpu

F32 = jnp.float32
BF16 = jnp.bfloat16
HI = lax.Precision.HIGHEST

DEPTH = 2
D_MODEL = 1024
D_GROUP = 256
HEAD_DIM = 64
N_HEADS = 4
SSD_STATE = 128
SSD_XBC = 768
SSD_CONV = 4
D_FF = 4096
ALPHA = (2.0 * DEPTH) ** 0.25
LN_EPS = 1e-5
RMS_EPS = 1e-5
RWKV_GN_EPS = HEAD_DIM * 1e-5
DILATED_BRANCHES = ((128, 1), (512, 4), (2048, 16))
ALIBI_SLOPES = tuple(2.0 ** (-8.0 * (h + 1) / N_HEADS) for h in range(N_HEADS))
ATTN_BLK = 128

ADAM_LR, ADAM_B1, ADAM_B2, ADAM_EPS, ADAM_WD, ADAM_STEP = 0.001, 0.9, 0.999, 1e-08, 0.01, 10

IN_COLS = 3716
PROJ_W = 4096
SEG_HGRN, SEG_RWKV, SEG_Q, SEG_Z, SEG_XBC, SEG_DT = 0, 1024, 2048, 2816, 3072, 3840
_PIECES = ((0, 896, SEG_RWKV), (896, 768, SEG_Q), (1664, 256, SEG_Z), (1920, 768, SEG_XBC),
           (2688, 4, SEG_DT), (2692, 1024, SEG_HGRN))
VRES_COL = SEG_RWKV + 896

HM64 = (N_HEADS, HEAD_DIM)
HM128 = (N_HEADS, SSD_STATE)
ROW_TILE = 256
SCAN_CHUNK = 128
VMEM_LIMIT = 48 * 1024 * 1024


def _cparams(sem=None):
    if sem is None:
        return pltpu.CompilerParams(vmem_limit_bytes=VMEM_LIMIT)
    return pltpu.CompilerParams(dimension_semantics=sem, vmem_limit_bytes=VMEM_LIMIT)


def _pick(n, pref):
    for t in pref:
        if n % t == 0:
            return t
    return n


def _relu2(u):
    r = jnp.maximum(u, 0.0)
    return r * r


class ShardedWeight:
    def __init__(self, arr, layer, axis):
        rows, cols = arr.shape[2:]
        assert (rows, cols)[axis] == 1024
        self.arr, self.layer, self.axis = arr, layer, axis
        self.shape = (4 * rows, cols) if axis == 0 else (rows, 4 * cols)

    def spec(self, mode, tn, tk):
        l, along_rows = self.layer, self.axis == 0
        if mode == "nn":
            assert (tk if along_rows else tn) == 1024
            index = (lambda i, j, k: (k, l, 0, j)) if along_rows else (lambda i, j, k: (j, l, k, 0))
            return pl.BlockSpec((None, None, tk, tn), index)
        assert mode == "nt" and (tn if along_rows else tk) == 1024
        index = (lambda i, j, k: (j, l, 0, k)) if along_rows else (lambda i, j, k: (k, l, j, 0))
        return pl.BlockSpec((None, None, tn, tk), index)


def matmul(a, b, mode, name, add=None, a_relu2=False, relu2_grad_of=None, out_col_shards=False):
    pieces = list(a) if isinstance(a, (list, tuple)) else [a]
    a_rows, a_cols = pieces[0].shape[0], sum(p.shape[1] for p in pieces)
    if mode == "nn":
        (M, K), N = (a_rows, a_cols), b.shape[1]
    elif mode == "nt":
        (M, K), N = (a_rows, a_cols), b.shape[0]
    else:
        (K, M), N = (a_rows, a_cols), b.shape[1]
    tm, tn, tk = _pick(M, (1024, 512, 256, 128)), _pick(N, (1024, 512, 256, 128)), _pick(K, (1024, 512, 256, 128))
    nk = K // tk
    dims = {"nn": (((1,), (0,)), ((), ())), "nt": (((1,), (1,)), ((), ())), "tn": (((0,), (0,)), ((), ()))}[mode]
    extras = [e for e in (add, relu2_grad_of) if e is not None]
    npieces = len(pieces)

    def body(*refs):
        b_ref = refs[npieces]
        o_ref = refs[-1]
        rest = list(refs[npieces + 1:-1])
        add_ref = rest.pop(0) if add is not None else None
        u_ref = rest.pop(0) if relu2_grad_of is not None else None
        k = pl.program_id(2)
        av = refs[0][...] if npieces == 1 else jnp.concatenate([r[...] for r in refs[:npieces]], axis=1)
        if a_relu2:
            av = _relu2(av)
        d = lax.dot_general(av.astype(BF16), b_ref[...].astype(BF16), dims, preferred_element_type=F32)

        @pl.when(k == 0)
        def _():
            o_ref[...] = d if add_ref is None else d + add_ref[...]

        if nk > 1:
            @pl.when(k > 0)
            def _():
                o_ref[...] += d

        if u_ref is not None:
            @pl.when(k == nk - 1)
            def _():
                o_ref[...] = o_ref[...] * (2.0 * jnp.maximum(u_ref[...], 0.0))

    if npieces == 1:
        a_specs = [pl.BlockSpec((tk, tm), lambda i, j, k: (k, i)) if mode == "tn" else pl.BlockSpec((tm, tk), lambda i, j, k: (i, k))]
    else:
        assert a_cols == (tm if mode == "tn" else tk)
        rows_of = (lambda i, j, k: (k, 0)) if mode == "tn" else (lambda i, j, k: (i, 0))
        a_specs = [pl.BlockSpec((tk if mode == "tn" else tm, p.shape[1]), rows_of) for p in pieces]
    if isinstance(b, ShardedWeight):
        b_spec = b.spec(mode, tn, tk)
    else:
        b_spec = pl.BlockSpec((tn, tk), lambda i, j, k: (j, k)) if mode == "nt" else pl.BlockSpec((tk, tn), lambda i, j, k: (k, j))
    o_spec = pl.BlockSpec((tm, tn), lambda i, j, k: (i, j))
    ins, specs = pieces + [_arr(b)] + extras, a_specs + [b_spec] + [o_spec] * len(extras)
    out_shape, out_spec = (M, N), o_spec
    if out_col_shards:
        assert tn == N // 4
        out_shape, out_spec = (4, M, tn), pl.BlockSpec((None, tm, tn), lambda i, j, k: (j, i, 0))
    return pl.pallas_call(
        body, grid=(M // tm, N // tn, nk), in_specs=specs, out_specs=out_spec,
        out_shape=jax.ShapeDtypeStruct(out_shape, F32),
        compiler_params=_cparams(("parallel", "parallel", "arbitrary")), name=name)(*ins)


def _row_spec(w, tile):
    return pl.BlockSpec((tile, w), lambda i: (i, 0))


def _par_spec(shape):
    return pl.BlockSpec(shape, lambda i: (0,) * len(shape))


class Cols:
    def __init__(self, arr, start, width):
        assert start % width == 0 or (start % 128 == 0 and width % 128 == 0)
        self.arr, self.start, self.width = arr, start, width
        self.shape, self.ndim = (arr.shape[0], width), 2


def _arr(r):
    return r.arr if isinstance(r, (Cols, ShardedWeight)) else r


def _rows_spec(a, tile):
    if isinstance(a, Cols):
        assert a.start % a.width == 0
        return pl.BlockSpec((tile, a.width), lambda i, blk=a.start // a.width: (i, blk))
    shape = a if isinstance(a, tuple) else a.shape
    if len(shape) == 3:
        return pl.BlockSpec((shape[0], tile, shape[2]), lambda i: (0, i, 0))
    return _row_spec(shape[1], tile)


def _rows_shape(S, w):
    return (w[0], S, w[1]) if isinstance(w, tuple) else (S, w)


def _rows_load(ref):
    if len(ref.shape) == 3:
        return jnp.concatenate([ref[h] for h in range(ref.shape[0])], axis=1)
    return ref[...]


def _rows_store(ref, val):
    if len(ref.shape) == 3:
        w = ref.shape[2]
        for h in range(ref.shape[0]):
            ref[h] = val[:, h * w:(h + 1) * w]
    else:
        ref[...] = val


def tl_fwd(fn, name, rows, pars, out_widths, tile=ROW_TILE):
    S = rows[0].shape[-2]
    nr = len(rows)

    def body(*refs):
        ins = [_rows_load(r) for r in refs[:nr]] + [r[...] for r in refs[nr:nr + len(pars)]]
        outs = fn(*ins)
        for o_ref, o in zip(refs[nr + len(pars):], outs):
            _rows_store(o_ref, o)

    shapes = [_rows_shape(S, w) for w in out_widths]
    return pl.pallas_call(
        body, grid=(S // tile,),
        in_specs=[_rows_spec(r, tile) for r in rows] + [_par_spec(p.shape) for p in pars],
        out_specs=[_rows_spec(s, tile) for s in shapes],
        out_shape=[jax.ShapeDtypeStruct(s, F32) for s in shapes],
        compiler_params=_cparams(("parallel",)), name=name)(*[_arr(r) for r in rows], *pars)


def tl_bwd(fn, name, rows, pars, cts, tile=ROW_TILE, row_grad=None):
    S = rows[0].shape[-2]
    nr, npar = len(rows), len(pars)
    row_grad = [True] * nr if row_grad is None else row_grad
    flat_cts = [c for group in cts for c in group]
    ncts = len(flat_cts)
    gi = [i for i in range(nr) if row_grad[i]]

    def body(*refs):
        row_v = [_rows_load(r) for r in refs[:nr]]
        par_v = [r[...] for r in refs[nr:nr + npar]]
        ct_refs = refs[nr + npar:nr + npar + ncts]
        out_refs = refs[nr + npar + ncts:]
        ct_v, pos = [], 0
        for group in cts:
            acc = _rows_load(ct_refs[pos])
            for q in range(1, len(group)):
                acc = acc + _rows_load(ct_refs[pos + q])
            pos += len(group)
            ct_v.append(acc)

        def f(diff_rows, par_vals):
            full = list(row_v)
            for idx, val in zip(gi, diff_rows):
                full[idx] = val
            return tuple(fn(*full, *par_vals))

        _, vjp = jax.vjp(f, [row_v[i] for i in gi], par_v)
        d_rows, d_pars = vjp(tuple(ct_v))
        for o_ref, g in zip(out_refs[:len(gi)], d_rows):
            _rows_store(o_ref, g)
        first = pl.program_id(0) == 0
        for o_ref, g in zip(out_refs[len(gi):], d_pars):
            @pl.when(first)
            def _(o_ref=o_ref):
                o_ref[...] = jnp.zeros_like(o_ref)
            o_ref[...] += g

    outs = pl.pallas_call(
        body, grid=(S // tile,),
        in_specs=[_rows_spec(r, tile) for r in rows] + [_par_spec(p.shape) for p in pars]
        + [_rows_spec(c, tile) for c in flat_cts],
        out_specs=[_rows_spec(rows[i].shape, tile) for i in gi] + [_par_spec(p.shape) for p in pars],
        out_shape=[jax.ShapeDtypeStruct(rows[i].shape, F32) for i in gi] + [jax.ShapeDtypeStruct(p.shape, F32) for p in pars],
        compiler_params=_cparams(("arbitrary",)), name=name)(*[_arr(r) for r in rows], *pars, *[_arr(c) for c in flat_cts])
    return list(outs[:len(gi)]), list(outs[len(gi):])


def _shift_rows(x, j):
    if j == 0:
        return x
    rolled = pltpu.roll(x, j, 0)
    row = lax.broadcasted_iota(jnp.int32, x.shape, 0)
    return jnp.where(row >= j, rolled, 0.0)


def _unshift_rows(x, j):
    if j == 0:
        return x
    S = x.shape[0]
    rolled = pltpu.roll(x, S - j, 0)
    row = lax.broadcasted_iota(jnp.int32, x.shape, 0)
    return jnp.where(row < S - j, rolled, 0.0)


def _fir_in_spec(x):
    first = x.start // 128 if isinstance(x, Cols) else 0
    return pl.BlockSpec((x.shape[0], 128), lambda j: (0, first + j))


def fir_fwd(x, taps, name):
    S, C = x.shape
    K = taps.shape[0]

    def body(x_ref, w_ref, y_ref):
        xv = x_ref[...]
        acc = jnp.zeros_like(xv)
        for k in range(K):
            acc = acc + _shift_rows(xv, K - 1 - k) * w_ref[pl.ds(k, 1), :]
        y_ref[...] = acc

    cs = pl.BlockSpec((S, 128), lambda j: (0, j))
    return pl.pallas_call(body, grid=(C // 128,), in_specs=[_fir_in_spec(x), pl.BlockSpec((K, 128), lambda j: (0, j))],
                          out_specs=cs, out_shape=jax.ShapeDtypeStruct((S, C), F32),
                          compiler_params=_cparams(("parallel",)), name=name)(_arr(x), taps)


def fir_bwd(x, taps, dy_list, name):
    S, C = x.shape
    K = taps.shape[0]
    n = len(dy_list)

    def body(*refs):
        x_ref, w_ref = refs[:2]
        dy = refs[2][...]
        for q in range(1, n):
            dy = dy + refs[2 + q][...]
        dx_ref, dw_ref, db_ref = refs[2 + n:]
        xv = x_ref[...]
        dx = jnp.zeros_like(xv)
        for k in range(K):
            j = K - 1 - k
            dx = dx + _unshift_rows(dy, j) * w_ref[pl.ds(k, 1), :]
            dw_ref[pl.ds(k, 1), :] = jnp.sum(dy * _shift_rows(xv, j), axis=0, keepdims=True)
        dx_ref[...] = dx
        db_ref[...] = jnp.sum(dy, axis=0, keepdims=True)

    cs = pl.BlockSpec((S, 128), lambda j: (0, j))
    ks = pl.BlockSpec((K, 128), lambda j: (0, j))
    bs = pl.BlockSpec((1, 128), lambda j: (0, j))
    return pl.pallas_call(body, grid=(C // 128,), in_specs=[_fir_in_spec(x), ks] + [cs] * n, out_specs=[cs, ks, bs],
                          out_shape=[jax.ShapeDtypeStruct((S, C), F32), jax.ShapeDtypeStruct((K, C), F32),
                                     jax.ShapeDtypeStruct((1, C), F32)],
                          compiler_params=_cparams(("parallel",)), name=name)(_arr(x), taps, *dy_list)


def _col(tile, lane, t):
    return jnp.sum(jnp.where(lane == t, tile, 0.0), axis=1, keepdims=True)


def _rwkv_step(s, rv, vcol):
    sa = jnp.sum(s * (-rv[3]), axis=1, keepdims=True)
    return s * rv[1] + sa * (rv[3] * rv[4]) + vcol * rv[2], sa


def _eye(n):
    return (lax.broadcasted_iota(jnp.int32, (n, n), 0) == lax.broadcasted_iota(jnp.int32, (n, n), 1)).astype(F32)


def _transposed(x):
    return lax.dot_general(_eye(x.shape[1]), x, (((1,), (1,)), ((), ())), precision=HI, preferred_element_type=F32)


def scan_fwd(r, w, k, v, kk, a, name):
    H, S, Dk = r.shape
    Dv = v.shape[1] // H
    Tc = SCAN_CHUNK
    nc = S // Tc
    rows = [r, w, k, kk, a]

    def body(*refs):
        row_refs = refs[:5]
        v_ref, y_ref, sall_ref, s_ref, vT_ref, yT_ref = refs[5:]

        @pl.when(pl.program_id(0) == 0)
        def _():
            s_ref[...] = jnp.zeros_like(s_ref)

        for h in range(H):
            vT_ref[h] = _transposed(v_ref[:, h * Dv:(h + 1) * Dv])
        yT_ref[...] = jnp.zeros_like(yT_ref)
        lane = lax.broadcasted_iota(jnp.int32, (Dv, Tc), 1)

        def step(t, states):
            new = []
            for h in range(H):
                s = states[h]
                sall_ref[t, h] = s
                rv = [ref[h, pl.ds(t, 1), :] for ref in row_refs]
                s, _ = _rwkv_step(s, rv, _col(vT_ref[h], lane, t))
                ycol = jnp.sum(s * rv[0], axis=1, keepdims=True)
                yT_ref[h] = jnp.where(lane == t, ycol, yT_ref[h])
                new.append(s)
            return tuple(new)

        states = lax.fori_loop(0, Tc, step, tuple(s_ref[h] for h in range(H)))
        for h in range(H):
            s_ref[h] = states[h]
            y_ref[:, h * Dv:(h + 1) * Dv] = _transposed(yT_ref[h])

    rs = pl.BlockSpec((H, Tc, Dk), lambda c: (0, c, 0))
    vs = pl.BlockSpec((Tc, H * Dv), lambda c: (c, 0))
    return pl.pallas_call(
        body, grid=(nc,), in_specs=[rs] * 5 + [vs],
        out_specs=[vs, pl.BlockSpec((Tc, H, Dv, Dk), lambda c: (c, 0, 0, 0))],
        out_shape=[jax.ShapeDtypeStruct((S, H * Dv), F32), jax.ShapeDtypeStruct((S, H, Dv, Dk), F32)],
        scratch_shapes=[pltpu.VMEM((H, Dv, Dk), F32), pltpu.VMEM((H, Dv, Tc), F32), pltpu.VMEM((H, Dv, Tc), F32)],
        compiler_params=_cparams(("arbitrary",)), name=name)(*rows, v)


def scan_bwd(r, w, k, v, kk, a, sall, dy, name):
    H, S, Dk = r.shape
    Dv = v.shape[1] // H
    Tc = SCAN_CHUNK
    nc = S // Tc
    rows = [r, w, k, kk, a]

    def body(*refs):
        row_refs = refs[:5]
        v_ref, dy_ref, sall_ref = refs[5:8]
        drow_refs = refs[8:13]
        dv_ref, ds_ref, vT_ref, dyT_ref, dvT_ref = refs[13:]

        @pl.when(pl.program_id(0) == 0)
        def _():
            ds_ref[...] = jnp.zeros_like(ds_ref)

        for h in range(H):
            vT_ref[h] = _transposed(v_ref[:, h * Dv:(h + 1) * Dv])
            dyT_ref[h] = _transposed(dy_ref[:, h * Dv:(h + 1) * Dv])
        dvT_ref[...] = jnp.zeros_like(dvT_ref)
        lane = lax.broadcasted_iota(jnp.int32, (Dv, Tc), 1)

        def bstep(i, carry):
            t = Tc - 1 - i
            new = []
            for h in range(H):
                ds = carry[h]
                sp = sall_ref[t, h]
                rv = [ref[h, pl.ds(t, 1), :] for ref in row_refs]
                vcol = _col(vT_ref[h], lane, t)
                dycol = _col(dyT_ref[h], lane, t)
                st, sa = _rwkv_step(sp, rv, vcol)
                drow_refs[0][h, pl.ds(t, 1), :] = jnp.sum(st * dycol, axis=0, keepdims=True)
                g = ds + dycol * rv[0]
                drow_refs[1][h, pl.ds(t, 1), :] = jnp.sum(g * sp, axis=0, keepdims=True)
                drow_refs[2][h, pl.ds(t, 1), :] = jnp.sum(g * vcol, axis=0, keepdims=True)
                dvcol = jnp.sum(g * rv[2], axis=1, keepdims=True)
                dsa = jnp.sum(g * (rv[3] * rv[4]), axis=1, keepdims=True)
                db = jnp.sum(g * sa, axis=0, keepdims=True)
                dnkk = jnp.sum(sp * dsa, axis=0, keepdims=True)
                drow_refs[3][h, pl.ds(t, 1), :] = db * rv[4] - dnkk
                drow_refs[4][h, pl.ds(t, 1), :] = db * rv[3]
                dvT_ref[h] = jnp.where(lane == t, dvcol, dvT_ref[h])
                new.append(g * rv[1] - dsa * rv[3])
            return tuple(new)

        carry = lax.fori_loop(0, Tc, bstep, tuple(ds_ref[h] for h in range(H)))
        for h in range(H):
            ds_ref[h] = carry[h]
            dv_ref[:, h * Dv:(h + 1) * Dv] = _transposed(dvT_ref[h])

    rs = pl.BlockSpec((H, Tc, Dk), lambda c: (0, nc - 1 - c, 0))
    vs = pl.BlockSpec((Tc, H * Dv), lambda c: (nc - 1 - c, 0))
    tile = pltpu.VMEM((H, Dv, Tc), F32)
    outs = pl.pallas_call(
        body, grid=(nc,),
        in_specs=[rs] * 5 + [vs, vs, pl.BlockSpec((Tc, H, Dv, Dk), lambda c: (nc - 1 - c, 0, 0, 0))],
        out_specs=[rs] * 5 + [vs],
        out_shape=[jax.ShapeDtypeStruct((H, S, Dk), F32)] * 5 + [jax.ShapeDtypeStruct((S, H * Dv), F32)],
        scratch_shapes=[pltpu.VMEM((H, Dv, Dk), F32), tile, tile, tile],
        compiler_params=_cparams(("arbitrary",)), name=name)(*rows, v, dy, sall)
    return list(outs[:5]), outs[5]


CHUNK = 128
SSD_GROUP = 2
HGRN_GROUP = 4


def chunk_fwd(fn, name, blocks, state_shape, out_width, group):
    H, S, _ = blocks[0].shape
    nc = S // CHUNK
    nb = len(blocks)

    def body(*refs):
        o_ref, sv_ref, st = refs[nb:]

        @pl.when(pl.program_id(1) == 0)
        def _():
            st[...] = jnp.zeros_like(st)

        for g in range(group):
            s0 = st[g]
            sv_ref[g, 0] = s0
            s1, out = fn(s0, *[r[g] for r in refs[:nb]])
            st[g] = s1
            o_ref[g] = out

    spec = lambda w: pl.BlockSpec((group, CHUNK, w), lambda h, c: (h, c, 0))
    return pl.pallas_call(
        body, grid=(H // group, nc), in_specs=[spec(b.shape[2]) for b in blocks],
        out_specs=[spec(out_width), pl.BlockSpec((group, 1) + state_shape, lambda h, c: (h, c, 0, 0))],
        out_shape=[jax.ShapeDtypeStruct((H, S, out_width), F32), jax.ShapeDtypeStruct((H, nc) + state_shape, F32)],
        scratch_shapes=[pltpu.VMEM((group,) + state_shape, F32)],
        compiler_params=_cparams(("parallel", "arbitrary")), name=name)(*blocks)


def chunk_bwd(fn, name, blocks, states, dout, group):
    H, S, _ = blocks[0].shape
    nc = S // CHUNK
    nb = len(blocks)
    state_shape = states.shape[2:]

    def body(*refs):
        sv_ref, do_ref = refs[nb], refs[nb + 1]
        d_refs = refs[nb + 2:2 * nb + 2]
        dst = refs[2 * nb + 2]

        @pl.when(pl.program_id(1) == 0)
        def _():
            dst[...] = jnp.zeros_like(dst)

        for g in range(group):
            _, vjp = jax.vjp(fn, sv_ref[g, 0], *[r[g] for r in refs[:nb]])
            grads = vjp((dst[g], do_ref[g]))
            dst[g] = grads[0]
            for d_ref, gr in zip(d_refs, grads[1:]):
                d_ref[g] = gr

    spec = lambda w: pl.BlockSpec((group, CHUNK, w), lambda h, c: (h, nc - 1 - c, 0))
    return pl.pallas_call(
        body, grid=(H // group, nc),
        in_specs=[spec(b.shape[2]) for b in blocks]
        + [pl.BlockSpec((group, 1) + state_shape, lambda h, c: (h, nc - 1 - c, 0, 0)), spec(dout.shape[2])],
        out_specs=[spec(b.shape[2]) for b in blocks],
        out_shape=[jax.ShapeDtypeStruct(b.shape, F32) for b in blocks],
        scratch_shapes=[pltpu.VMEM((group,) + state_shape, F32)],
        compiler_params=_cparams(("parallel", "arbitrary")), name=name)(*blocks, states, dout)


def _bdot(a, b, dims):
    return lax.dot_general(a.astype(BF16), b.astype(BF16), (dims, ((), ())), preferred_element_type=F32)


def ssd_chunk(state, cb, bb, da, xdt):
    T = cb.shape[0]
    ti = lax.broadcasted_iota(jnp.int32, (T, T), 0)
    si = lax.broadcasted_iota(jnp.int32, (T, T), 1)
    mask = ti >= si
    cs = jnp.dot(mask.astype(F32), da, precision=HI, preferred_element_type=F32)
    pick = (lax.broadcasted_iota(jnp.int32, cs.shape, 1) == 0).astype(F32)
    cs_row = lax.dot_general(pick, cs, (((1,), (1,)), ((), ())), precision=HI, preferred_element_type=F32)
    lmat = jnp.where(mask, jnp.exp(jnp.where(mask, cs - cs_row, 0.0)), 0.0)
    scores = _bdot(cb, bb, ((1,), (1,))) * lmat
    y = _bdot(scores, xdt, ((1,), (0,))) + _bdot(cb, state, ((1,), (1,))) * jnp.exp(cs[:, :HEAD_DIM])
    last = cs[T - 1:T, :]
    new_state = state * jnp.exp(last) + _bdot(xdt, bb * jnp.exp(last - cs), ((0,), (0,)))
    return new_state, y


HGRN_SUB = 16


def hgrn_chunk(state, q, k, lf, v):
    T, C = q.shape[0], HGRN_SUB
    ti = lax.broadcasted_iota(jnp.int32, (C, C), 0)
    si = lax.broadcasted_iota(jnp.int32, (C, C), 1)
    tril = (ti >= si).astype(F32)
    row = lax.broadcasted_iota(jnp.int32, (C, q.shape[1]), 0)
    outs = []
    for j in range(T // C):
        qj, kj, lj, vj = (a[j * C:(j + 1) * C] for a in (q, k, lf, v))
        b = jnp.dot(tril, lj, precision=HI, preferred_element_type=F32)
        o = _bdot(qj * jnp.exp(b), state, ((1,), (1,)))
        for s in range(C):
            m = row >= s
            e = jnp.where(m, jnp.exp(jnp.where(m, b - b[s:s + 1], 0.0)), 0.0)
            o = o + jnp.sum(qj * kj[s:s + 1] * e, axis=1, keepdims=True) * vj[s:s + 1]
        last = b[C - 1:C]
        state = state * jnp.exp(last) + _bdot(vj, kj * jnp.exp(last - b), ((0,), (0,)))
        outs.append(o)
    return state, jnp.concatenate(outs, axis=0)


def _attn_block(q, kp, kc, vp, vc, n, slope, dilation):
    blk = ATTN_BLK
    k2 = jnp.concatenate([kp, kc], axis=0)
    v2 = jnp.concatenate([vp, vc], axis=0)
    s = _bdot(q, k2, ((1,), (1,))) * (HEAD_DIM ** -0.5)
    i = lax.broadcasted_iota(jnp.int32, (blk, 2 * blk), 0)
    j = lax.broadcasted_iota(jnp.int32, (blk, 2 * blk), 1)
    dist = blk + i - j
    first_key = jnp.where(n > 0, 0, blk)
    valid = (dist >= 0) & (dist <= blk) & (j >= first_key)
    s = s - slope * (dist * dilation).astype(F32)
    s = jnp.where(valid, s, -1e30)
    m = jnp.max(s, axis=-1, keepdims=True)
    p = jnp.exp(s - m)
    l = jnp.sum(p, axis=-1, keepdims=True)
    o = _bdot(p, v2, ((1,), (0,))) / l
    lse = jnp.broadcast_to(m + jnp.log(l), o.shape)
    return o, lse


_QCOL = SEG_Q // 128
PAIR = 2 * HEAD_DIM


def _attn_specs(rows):
    cur = lambda j: pl.BlockSpec((rows, PAIR), lambda p, n: (n, j + p))
    prev = lambda j: pl.BlockSpec((rows, PAIR), lambda p, n: (jnp.maximum(n - 1, 0), j + p))
    return cur, prev


def _pair_slope(pair, h):
    return jnp.where(pair == 0, jnp.float32(ALIBI_SLOPES[h]), jnp.float32(ALIBI_SLOPES[2 + h]))


def _halves(t):
    return [t[:, h * HEAD_DIM:(h + 1) * HEAD_DIM] for h in range(2)]


def _for_classes(dilation, step):
    if dilation == 1:
        step(0)
    else:
        lax.fori_loop(0, dilation, lambda z, c: (step(z), c)[1], 0)


def attn_fwd(proj, dilation, name):
    S = proj.shape[0]
    blk = ATTN_BLK
    rows = blk * dilation
    cur, prev = _attn_specs(rows)

    def body(q_ref, kp_ref, kc_ref, vp_ref, vc_ref, o_ref, l_ref):
        pair, n = pl.program_id(0), pl.program_id(1)

        def one_class(z):
            sel = pl.ds(z, blk, stride=dilation) if dilation > 1 else pl.ds(0, blk)
            q, kp, kc, vp, vc = (_halves(r[sel, :]) for r in (q_ref, kp_ref, kc_ref, vp_ref, vc_ref))
            res = [_attn_block(q[h], kp[h], kc[h], vp[h], vc[h], n, _pair_slope(pair, h), dilation) for h in range(2)]
            o_ref[sel, :] = jnp.concatenate([r[0] for r in res], axis=1)
            l_ref[sel, :] = jnp.concatenate([r[1] for r in res], axis=1)

        _for_classes(dilation, one_class)

    return pl.pallas_call(
        body, grid=(2, S // rows),
        in_specs=[cur(_QCOL), prev(_QCOL + 2), cur(_QCOL + 2), prev(_QCOL + 4), cur(_QCOL + 4)],
        out_specs=[cur(0), cur(0)], out_shape=[jax.ShapeDtypeStruct((S, D_GROUP), F32)] * 2,
        compiler_params=_cparams(("parallel", "arbitrary")), name=name)(proj, proj, proj, proj, proj)


def attn_bwd(proj, do, dlse, dilation, name):
    S = proj.shape[0]
    blk = ATTN_BLK
    rows = blk * dilation
    cur, prev = _attn_specs(rows)
    full = pl.BlockSpec((S, PAIR), lambda p, n: (0, p))

    def body(q_ref, kp_ref, kc_ref, vp_ref, vc_ref, do_ref, dl_ref, dq_ref, dk_ref, dv_ref):
        pair, n = pl.program_id(0), pl.program_id(1)

        @pl.when(n == 0)
        def _():
            dk_ref[...] = jnp.zeros_like(dk_ref)
            dv_ref[...] = jnp.zeros_like(dv_ref)

        def one_class(z):
            sel = pl.ds(z, blk, stride=dilation) if dilation > 1 else pl.ds(0, blk)
            q, kp, kc, vp, vc, do_v, dl_v = (_halves(r[sel, :]) for r in
                                             (q_ref, kp_ref, kc_ref, vp_ref, vc_ref, do_ref, dl_ref))
            grads = []
            for h in range(2):
                f = lambda q_, kp_, kc_, vp_, vc_, h=h: _attn_block(q_, kp_, kc_, vp_, vc_, n, _pair_slope(pair, h), dilation)
                _, vjp = jax.vjp(f, q[h], kp[h], kc[h], vp[h], vc[h])
                grads.append(vjp((do_v[h], dl_v[h])))
            both = lambda j: jnp.concatenate([grads[0][j], grads[1][j]], axis=1)
            dq_ref[sel, :] = both(0)
            if dilation > 1:
                here = pl.ds(n * rows + z, blk, stride=dilation)
                before = pl.ds(jnp.maximum(n - 1, 0) * rows + z, blk, stride=dilation)
            else:
                here = pl.ds(pl.multiple_of(n * blk, blk), blk)
                before = pl.ds(pl.multiple_of(jnp.maximum(n - 1, 0) * blk, blk), blk)
            dk_ref[here, :] = dk_ref[here, :] + both(2)
            dv_ref[here, :] = dv_ref[here, :] + both(4)
            dk_ref[before, :] = dk_ref[before, :] + both(1)
            dv_ref[before, :] = dv_ref[before, :] + both(3)

        _for_classes(dilation, one_class)

    return pl.pallas_call(
        body, grid=(2, S // rows),
        in_specs=[cur(_QCOL), prev(_QCOL + 2), cur(_QCOL + 2), prev(_QCOL + 4), cur(_QCOL + 4), cur(0), cur(0)],
        out_specs=[cur(0), full, full], out_shape=[jax.ShapeDtypeStruct((S, D_GROUP), F32)] * 3,
        compiler_params=_cparams(("parallel", "arbitrary")), name=name)(proj, proj, proj, proj, proj, do, dlse)


def _head_ones(width, group):
    i = lax.broadcasted_iota(jnp.int32, (width, width), 0) // group
    j = lax.broadcasted_iota(jnp.int32, (width, width), 1) // group
    return (i == j).astype(F32)


def _group_sum(x, group):
    return jnp.dot(x, _head_ones(x.shape[1], group), precision=HI, preferred_element_type=F32)


def _spread(width_in, width_out, rep):
    i = lax.broadcasted_iota(jnp.int32, (width_in, width_out), 0)
    j = lax.broadcasted_iota(jnp.int32, (width_in, width_out), 1) // rep
    return (i == j).astype(F32)


def _hdot(a, b):
    return jnp.dot(a, b, precision=HI, preferred_element_type=F32)


def _sigmoid(x):
    return 1.0 / (1.0 + jnp.exp(-x))


def _softplus(x):
    return jnp.maximum(x, 0.0) + jnp.log(1.0 + jnp.exp(jnp.minimum(x, -x)))


def _silu(x):
    return x * _sigmoid(x)


def rwkv_pre(layer):
    def fn(*args):
        if layer == 0:
            fs, w0, w2p, a0, a2p, g2p, k_k, k_a = args
        else:
            fs, vfirst, w0, w2p, a0, a2p, g2p, k_k, k_a, v0, v2p = args
        r, k, v = fs[:, 0:256], fs[:, 256:512], fs[:, 512:768]
        lora = fs[:, 768:896]
        w_log = -_softplus(-(w0 + _hdot(jnp.tanh(lora), w2p))) - 0.5
        decay = jnp.exp(-jnp.exp(w_log))
        a = _sigmoid(a0 + _hdot(lora, a2p))
        g = _hdot(_sigmoid(lora), g2p)
        if layer > 0:
            v = v + (vfirst - v) * _sigmoid(v0 + _hdot(fs[:, 896:1024], v2p))
        kk = k * k_k
        kk = kk / jnp.maximum(jnp.sqrt(_group_sum(kk * kk, HEAD_DIM)), 1e-12)
        k = k * (1.0 + (a - 1.0) * k_a)
        return r, decay, k, v, kk, a, g
    return fn


def rwkv_post(y, r, k, v, g, lnx_w, lnx_b, r_k):
    mu = _group_sum(y, HEAD_DIM) * (1.0 / HEAD_DIM)
    yc = y - mu
    var = _group_sum(yc * yc, HEAD_DIM) * (1.0 / HEAD_DIM)
    yn = yc * lax.rsqrt(var + RWKV_GN_EPS) * lnx_w + lnx_b
    bonus = _group_sum(r * k * r_k, HEAD_DIM) * v
    return ((yn + bonus) * g,)


def attn_combine(o1, o2, o3, l1, l2, l3):
    m = jnp.maximum(jnp.maximum(l1, l2), l3)
    e1, e2, e3 = jnp.exp(l1 - m), jnp.exp(l2 - m), jnp.exp(l3 - m)
    return ((o1 * e1 + o2 * e2 + o3 * e3) / (e1 + e2 + e3),)


def ssd_pre(xc, dtr, conv_b, dt_bias, a_log):
    xbc = _silu(xc + conv_b)
    xs, bm, cm = xbc[:, 0:256], xbc[:, 256:512], xbc[:, 512:768]
    dt = _softplus(dtr + dt_bias)
    a_neg = -jnp.exp(a_log)
    wide = _spread(128, N_HEADS * SSD_STATE, SSD_STATE)
    w = _hdot(dt, wide) * _hdot(a_neg, wide)
    xdt = xs * _hdot(dt, _spread(128, D_GROUP, HEAD_DIM))
    rr = jnp.concatenate([cm[:, 0:128], cm[:, 0:128], cm[:, 128:256], cm[:, 128:256]], axis=1)
    kk = jnp.concatenate([bm[:, 0:128], bm[:, 0:128], bm[:, 128:256], bm[:, 128:256]], axis=1)
    return rr, w, kk, xdt, xs


def ssd_post(ys, z, xs, d_skip, norm_w):
    y = ys + xs * _hdot(d_skip, _spread(128, D_GROUP, HEAD_DIM))
    y = y * _silu(z)
    half = D_GROUP // 2
    parts = []
    for g in range(2):
        t = y[:, g * half:(g + 1) * half]
        parts.append(t * lax.rsqrt(jnp.mean(t * t, axis=-1, keepdims=True) + RMS_EPS))
    return (jnp.concatenate(parts, axis=1) * norm_w,)


def hgrn_pre(seg, lb):
    q, f, i = seg[:, 0:256], seg[:, 256:512], seg[:, 512:768]
    forget = lb + (1.0 - lb) * _sigmoid(f)
    return _silu(q), 1.0 - forget, jnp.log(forget), i


def hgrn_post(o, seg, norm_w):
    g = seg[:, 768:1024]
    ms = _group_sum(o * o, HEAD_DIM) * (1.0 / HEAD_DIM)
    return (o * lax.rsqrt(ms + RMS_EPS) * norm_w * _silu(g),)


def ln_res(x, y, w, b):
    z = ALPHA * x + y
    mu = jnp.mean(z, axis=-1, keepdims=True)
    zc = z - mu
    var = jnp.mean(zc * zc, axis=-1, keepdims=True)
    return (zc * lax.rsqrt(var + LN_EPS) * w + b,)


def loss_and_grad(y, tgt, name):
    S, D = y.shape
    tile = ROW_TILE

    def body(y_ref, t_ref, l_ref, dy_ref):
        e = y_ref[...] - t_ref[...]
        dy_ref[...] = e * (1.0 / D)

        @pl.when(pl.program_id(0) == 0)
        def _():
            l_ref[...] = jnp.zeros_like(l_ref)

        per_row = 0.5 * jnp.mean(e * e, axis=-1, keepdims=True)
        l_ref[...] += jnp.sum(per_row, axis=0, keepdims=True) * jnp.ones((1, 128), F32)

    return pl.pallas_call(body, grid=(S // tile,), in_specs=[_row_spec(D, tile)] * 2,
                          out_specs=[_par_spec((1, 128)), _row_spec(D, tile)],
                          out_shape=[jax.ShapeDtypeStruct((1, 128), F32), jax.ShapeDtypeStruct((S, D), F32)],
                          compiler_params=_cparams(("arbitrary",)), name=name)(y, tgt)


def add_rows(arrs, name):
    (out,) = tl_fwd(lambda *a: (functools.reduce(lambda p, q: p + q, a),), name, arrs, [], [arrs[0].shape[1]])
    return out


def small_fwd(fn, name, ins, out_shapes):
    n = len(ins)

    def body(*refs):
        outs = fn(*[r[...] for r in refs[:n]])
        for o_ref, o in zip(refs[n:], outs):
            o_ref[...] = o

    return pl.pallas_call(body, out_shape=[jax.ShapeDtypeStruct(s, F32) for s in out_shapes], name=name)(*ins)


def small_bwd(fn, name, ins, cts):
    n, m = len(ins), len(cts)

    def body(*refs):
        _, vjp = jax.vjp(lambda *a: tuple(fn(*a)), *[r[...] for r in refs[:n]])
        grads = vjp(tuple(r[...] for r in refs[n:n + m]))
        for o_ref, g in zip(refs[n + m:], grads):
            o_ref[...] = g

    return pl.pallas_call(body, out_shape=[jax.ShapeDtypeStruct(a.shape, F32) for a in ins], name=name)(*ins, *cts)


def param_prep(lower_bounds, mu0, mu1):
    e = jnp.exp(lower_bounds - jnp.max(lower_bounds, axis=0, keepdims=True))
    sm = e / jnp.sum(e, axis=0, keepdims=True)
    lb0 = sm[0:1] - sm[0:1]
    lb1 = sm[0:1] + sm[1:2] - sm[0:1]
    return lb0, lb1, mu0, 1.0 - mu0, mu1, 1.0 - mu1


def _rows_tile(rows):
    return _pick(rows, (256, 128, 64, 32, 16, 8))


def sum_parts(parts, name):
    P, rows, cols = parts.shape
    tile = _rows_tile(rows)

    def body(p_ref, o_ref):
        acc = p_ref[0]
        for p in range(1, P):
            acc = acc + p_ref[p]
        o_ref[...] = acc

    return pl.pallas_call(body, grid=(rows // tile,), in_specs=[pl.BlockSpec((P, tile, cols), lambda i: (0, i, 0))],
                          out_specs=pl.BlockSpec((tile, cols), lambda i: (i, 0)),
                          out_shape=jax.ShapeDtypeStruct((rows, cols), F32),
                          compiler_params=_cparams(("parallel",)), name=name)(parts)


def pair_sum(own, got, ids, name):
    _, P, rows, cols = own.shape
    tile = _rows_tile(rows)

    def body(ids_ref, own_ref, got_ref, o_ref):
        o_ref[0] = (own_ref[0, 0] + got_ref[0]).astype(BF16)

    grid_spec = pltpu.PrefetchScalarGridSpec(
        num_scalar_prefetch=1, grid=(P, rows // tile),
        in_specs=[pl.BlockSpec((1, 1, tile, cols), lambda s, i, ids: (ids[0], s, i, 0)),
                  pl.BlockSpec((1, tile, cols), lambda s, i, ids: (s, i, 0))],
        out_specs=pl.BlockSpec((1, tile, cols), lambda s, i, ids: (s, i, 0)))
    return pl.pallas_call(body, grid_spec=grid_spec, out_shape=jax.ShapeDtypeStruct((P, rows, cols), BF16),
                          compiler_params=_cparams(("parallel", "parallel")), name=name)(ids, own, got)


def sum_chips(slots, own, got, ids, name):
    P, rows, cols = slots.shape
    tile = _rows_tile(rows)

    def body(ids_ref, s_ref, own_ref, got_ref, o_ref):
        chip = ids_ref[1]
        mine = own_ref[0, 0] + got_ref[0]
        acc = None
        for p in range(P):
            term = jnp.where(chip == p, mine, s_ref[p].astype(F32))
            acc = term if acc is None else acc + term
        o_ref[0] = acc

    grid_spec = pltpu.PrefetchScalarGridSpec(
        num_scalar_prefetch=1, grid=(rows // tile,),
        in_specs=[pl.BlockSpec((P, tile, cols), lambda i, ids: (0, i, 0)),
                  pl.BlockSpec((1, 1, tile, cols), lambda i, ids: (ids[0], ids[1], i, 0)),
                  pl.BlockSpec((1, tile, cols), lambda i, ids: (ids[1], i, 0))],
        out_specs=pl.BlockSpec((1, tile, cols), lambda i, ids: (ids[0], i, 0)))
    return pl.pallas_call(body, grid_spec=grid_spec, out_shape=jax.ShapeDtypeStruct((2, rows, cols), F32),
                          compiler_params=_cparams(("parallel",)), name=name)(ids, slots, own, got)


def adamw(w, g, m, v, name, lane_tiled=False):
    layers, rows, cols = w.shape
    tile = _rows_tile(rows)

    def body(w_ref, g_ref, m_ref, v_ref, d_ref, nm_ref, nv_ref):
        gv = g_ref[...]
        nm = ADAM_B1 * m_ref[...] + (1.0 - ADAM_B1) * gv
        nv = ADAM_B2 * v_ref[...] + (1.0 - ADAM_B2) * jnp.square(gv)
        m_hat = nm / (1.0 - ADAM_B1 ** ADAM_STEP)
        v_hat = nv / (1.0 - ADAM_B2 ** ADAM_STEP)
        d_ref[...] = -ADAM_LR * (m_hat / (jnp.sqrt(v_hat) + ADAM_EPS) + ADAM_WD * w_ref[...])
        nm_ref[...] = nm
        nv_ref[...] = nv

    spec, grid = pl.BlockSpec((None, tile, cols), lambda l, i: (l, i, 0)), (layers, rows // tile)
    if lane_tiled:
        spec, grid = pl.BlockSpec((layers, rows, 128), lambda l, i: (0, 0, i)), (1, cols // 128)
    return pl.pallas_call(body, grid=grid, in_specs=[spec] * 4, out_specs=[spec] * 3,
                          out_shape=[jax.ShapeDtypeStruct((layers, rows, cols), F32)] * 3,
                          compiler_params=_cparams(("parallel", "parallel")), name=name)(w, g, m, v)


MESH = pl.DeviceIdType.MESH
ANY = pl.BlockSpec(memory_space=pl.ANY)


def _flip(v, bit):
    return 1 - v if bit else v


_CHIP_RELATIONS = ((1, 0), (0, 1), (1, 1))


def gather_chips(arrs, small, name):
    n = len(arrs)

    def body(*refs):
        ins, small_in = refs[:n], refs[n]
        outs, small_out = refs[n + 1:2 * n + 1], refs[2 * n + 1]
        send, recv, fsend, frecv, ssend, srecv = refs[2 * n + 2:]
        x, y, c = lax.axis_index("x"), lax.axis_index("y"), lax.axis_index("c")
        me = 2 * x + y
        chips = [(_flip(x, bx), _flip(y, by)) for bx, by in _CHIP_RELATIONS]

        def over_ici(i, r, block_chip):
            return pltpu.make_async_remote_copy(src_ref=ins[i].at[c], dst_ref=outs[i].at[block_chip, c],
                                                send_sem=send.at[i, r], recv_sem=recv.at[i, r],
                                                device_id=(chips[r][0], chips[r][1], c), device_id_type=MESH)

        def to_sibling(i, r, layer):
            blk = outs[i].at[2 * chips[r][0] + chips[r][1], layer]
            return pltpu.make_async_remote_copy(src_ref=blk, dst_ref=blk, send_sem=fsend.at[i, r],
                                                recv_sem=frecv.at[i, r], device_id=(x, y, 1 - c), device_id_type=MESH)

        first = [over_ici(i, r, me) for i in range(n) for r in range(3)]
        smalls = [pltpu.make_async_remote_copy(src_ref=small_in, dst_ref=small_out.at[me], send_sem=ssend.at[r],
                                               recv_sem=srecv.at[r], device_id=(chips[r][0], chips[r][1], c),
                                               device_id_type=MESH) for r in range(3)]
        for cp in first + smalls:
            cp.start()
        passed = []
        for r in range(3):
            for i in range(n):
                over_ici(i, r, 2 * chips[r][0] + chips[r][1]).wait_recv()
                fw = to_sibling(i, r, c)
                fw.start()
                passed.append(fw)
        for r in range(3):
            for i in range(n):
                to_sibling(i, r, 1 - c).wait_recv()
        for cp in first + passed:
            cp.wait_send()
        for cp in smalls:
            cp.wait()

    return pl.pallas_call(
        body, in_specs=[ANY] * (n + 1), out_specs=[ANY] * (n + 1),
        out_shape=[jax.ShapeDtypeStruct((4,) + a.shape, a.dtype) for a in arrs]
        + [jax.ShapeDtypeStruct((4,) + small.shape, small.dtype)],
        scratch_shapes=[pltpu.SemaphoreType.DMA((n, 3)), pltpu.SemaphoreType.DMA((n, 3)), pltpu.SemaphoreType.DMA((n, 3)),
                        pltpu.SemaphoreType.DMA((n, 3)), pltpu.SemaphoreType.DMA((3,)), pltpu.SemaphoreType.DMA((3,))],
        name=name)(*arrs, small)


_RELATIONS = tuple((r >> 2 & 1, r >> 1 & 1, r & 1) for r in range(1, 8))


def gather_devices(arr, name):
    def body(in_ref, out_ref, send, recv, loc):
        x, y, c = lax.axis_index("x"), lax.axis_index("y"), lax.axis_index("c")
        me = 4 * x + 2 * y + c
        lc = pltpu.make_async_copy(in_ref, out_ref.at[me], loc)
        lc.start()
        pending = [lc]
        for r, (bx, by, bc) in enumerate(_RELATIONS):
            cp = pltpu.make_async_remote_copy(src_ref=in_ref, dst_ref=out_ref.at[me], send_sem=send.at[r],
                                              recv_sem=recv.at[r], device_id=(_flip(x, bx), _flip(y, by), _flip(c, bc)),
                                              device_id_type=MESH)
            cp.start()
            pending.append(cp)
        for cp in pending:
            cp.wait()

    return pl.pallas_call(
        body, in_specs=[ANY], out_specs=ANY, out_shape=jax.ShapeDtypeStruct((8,) + arr.shape, arr.dtype),
        scratch_shapes=[pltpu.SemaphoreType.DMA((7,)), pltpu.SemaphoreType.DMA((7,)), pltpu.SemaphoreType.DMA(())],
        name=name)(arr)


def pair_exchange(arrs, name):
    n = len(arrs)

    def body(*refs):
        ins, outs = refs[:n], refs[n:2 * n]
        send, recv = refs[2 * n:]
        x, y, c = lax.axis_index("x"), lax.axis_index("y"), lax.axis_index("c")
        pending = []
        for i in range(n):
            for s in range(4):
                cp = pltpu.make_async_remote_copy(src_ref=ins[i].at[1 - c, s], dst_ref=outs[i].at[s],
                                                  send_sem=send.at[i, s], recv_sem=recv.at[i, s],
                                                  device_id=(x, y, 1 - c), device_id_type=MESH)
                cp.start()
                pending.append(cp)
        for cp in pending:
            cp.wait()

    return pl.pallas_call(
        body, in_specs=[ANY] * n, out_specs=[ANY] * n,
        out_shape=[jax.ShapeDtypeStruct(a.shape[1:], a.dtype) for a in arrs],
        scratch_shapes=[pltpu.SemaphoreType.DMA((n, 4)), pltpu.SemaphoreType.DMA((n, 4))],
        name=name)(*arrs)


def reduce_chips(arrs, name):
    n = len(arrs)

    def body(*refs):
        ins, outs = refs[:n], refs[n:2 * n]
        send, recv, loc = refs[2 * n:]
        x, y, c = lax.axis_index("x"), lax.axis_index("y"), lax.axis_index("c")
        me = 2 * x + y
        pending = []
        for i in range(n):
            for r, (bx, by) in enumerate(_CHIP_RELATIONS):
                px, py = _flip(x, bx), _flip(y, by)
                cp = pltpu.make_async_remote_copy(src_ref=ins[i].at[2 * px + py], dst_ref=outs[i].at[me],
                                                  send_sem=send.at[i, r], recv_sem=recv.at[i, r],
                                                  device_id=(px, py, c), device_id_type=MESH)
                cp.start()
                pending.append(cp)
        for i in range(n):
            lc = pltpu.make_async_copy(ins[i].at[me], outs[i].at[me], loc.at[i])
            lc.start()
            pending.append(lc)
        for cp in pending:
            cp.wait()

    return pl.pallas_call(
        body, in_specs=[ANY] * n, out_specs=[ANY] * n,
        out_shape=[jax.ShapeDtypeStruct(a.shape, a.dtype) for a in arrs],
        scratch_shapes=[pltpu.SemaphoreType.DMA((n, 3)), pltpu.SemaphoreType.DMA((n, 3)), pltpu.SemaphoreType.DMA((n,))],
        name=name)(*arrs)


EXCHANGE_PIECES = 8


def sibling_exchange(arrs, name):
    n = len(arrs)

    def body(*refs):
        bufs = refs[n:2 * n]
        send, recv = refs[2 * n:]
        x, y, c = lax.axis_index("x"), lax.axis_index("y"), lax.axis_index("c")
        pending = []
        for i in range(n):
            rows = bufs[i].shape[1] // EXCHANGE_PIECES
            for j in range(EXCHANGE_PIECES):
                piece = bufs[i].at[c, pl.ds(j * rows, rows)]
                cp = pltpu.make_async_remote_copy(src_ref=piece, dst_ref=piece, send_sem=send.at[i, j],
                                                  recv_sem=recv.at[i, j], device_id=(x, y, 1 - c), device_id_type=MESH)
                cp.start()
                pending.append(cp)
        for i in range(n):
            rows = bufs[i].shape[1] // EXCHANGE_PIECES
            for j in range(EXCHANGE_PIECES):
                landed = bufs[i].at[1 - c, pl.ds(j * rows, rows)]
                pltpu.make_async_remote_copy(src_ref=landed, dst_ref=landed, send_sem=send.at[i, j], recv_sem=recv.at[i, j],
                                             device_id=(x, y, 1 - c), device_id_type=MESH).wait_recv()
        for cp in pending:
            cp.wait_send()

    return pl.pallas_call(
        body, in_specs=[ANY] * n, out_specs=[ANY] * n,
        out_shape=[jax.ShapeDtypeStruct(a.shape, a.dtype) for a in arrs], input_output_aliases={i: i for i in range(n)},
        scratch_shapes=[pltpu.SemaphoreType.DMA((n, EXCHANGE_PIECES)), pltpu.SemaphoreType.DMA((n, EXCHANGE_PIECES))],
        name=name)(*arrs)


def rwkv_fwd(l, seg, taps, pars, vfirst):
    fs = fir_fwd(seg, taps, f"rwkv_shift_fwd{l}")
    rows = [fs] + ([vfirst] if l else [])
    R, W, K, V, KK, A, G = tl_fwd(rwkv_pre(l), f"rwkv_pre_fwd{l}", rows, pars["pre"],
                                  [HM64, HM64, HM64, D_GROUP, HM64, HM64, D_GROUP])
    Y, sall = scan_fwd(R, W, K, V, KK, A, f"rwkv_scan_fwd{l}")
    (out,) = tl_fwd(rwkv_post, f"rwkv_post_fwd{l}", [Y, R, K, V, G], pars["post"], [D_GROUP])
    return out, V, (seg, taps, rows, R, W, K, V, KK, A, G, Y, sall)


def rwkv_bwd(l, saved, pars, dout, dv_extra):
    seg, taps, rows, R, W, K, V, KK, A, G, Y, sall = saved
    (dY, dR1, dK1, dV1, dG), dpost = tl_bwd(rwkv_post, f"rwkv_post_bwd{l}", [Y, R, K, V, G], pars["post"], [[dout]])
    (dR2, dW, dK2, dKK, dA), dV2 = scan_bwd(R, W, K, V, KK, A, sall, dY, f"rwkv_scan_bwd{l}")
    cts = [[dR1, dR2], [dW], [dK1, dK2], [dV1, dV2] + dv_extra, [dKK], [dA], [dG]]
    drows, dpre = tl_bwd(rwkv_pre(l), f"rwkv_pre_bwd{l}", rows, pars["pre"], cts)
    dseg, dtaps, _ = fir_bwd(seg, taps, [drows[0]], f"rwkv_shift_bwd{l}")
    return dseg, (drows[1] if l else None), dtaps, dpre, dpost


def attn_mix_fwd(l, proj):
    os_, ls_ = [], []
    for b, (_, d) in enumerate(DILATED_BRANCHES):
        o, lse = attn_fwd(proj, d, f"attn_fwd{l}_{b}")
        os_.append(o)
        ls_.append(lse)
    (out,) = tl_fwd(attn_combine, f"attn_combine_fwd{l}", os_ + ls_, [], [D_GROUP])
    return out, (proj, os_, ls_)


def attn_mix_bwd(l, saved, dout):
    proj, os_, ls_ = saved
    drows, _ = tl_bwd(attn_combine, f"attn_combine_bwd{l}", os_ + ls_, [], [[dout]])
    grads = [attn_bwd(proj, drows[b], drows[3 + b], d, f"attn_bwd{l}_{b}") for b, (_, d) in enumerate(DILATED_BRANCHES)]
    return tuple(add_rows([g[j] for g in grads], f"attn_d{'qkv'[j]}{l}") for j in range(3))


def ssd_fwd(l, z, xbc, dtr, pars):
    xc = fir_fwd(xbc, pars["taps"], f"ssd_conv_fwd{l}")
    rr, w, kk, xdt, xs = tl_fwd(ssd_pre, f"ssd_pre_fwd{l}", [xc, dtr], pars["pre"], [HM128, HM128, HM128, HM64, D_GROUP])
    blocks = [rr, kk, w, xdt]
    ys, states = chunk_fwd(ssd_chunk, f"ssd_scan_fwd{l}", blocks, (HEAD_DIM, SSD_STATE), HEAD_DIM, SSD_GROUP)
    (out,) = tl_fwd(ssd_post, f"ssd_post_fwd{l}", [ys, z, xs], pars["post"], [D_GROUP])
    return out, (z, xbc, dtr, xc, blocks, states, xs, ys)


def ssd_bwd(l, saved, pars, dout):
    z, xbc, dtr, xc, blocks, states, xs, ys = saved
    (dys, dz, dxs), dpost = tl_bwd(ssd_post, f"ssd_post_bwd{l}", [ys, z, xs], pars["post"], [[dout]])
    drr, dkk, dw, dxdt = chunk_bwd(ssd_chunk, f"ssd_scan_bwd{l}", blocks, states, dys, SSD_GROUP)
    (dxc, ddtr), dpre = tl_bwd(ssd_pre, f"ssd_pre_bwd{l}", [xc, dtr], pars["pre"], [[drr], [dw], [dkk], [dxdt], [dxs]])
    dxbc, dtaps, _ = fir_bwd(xbc, pars["taps"], [dxc], f"ssd_conv_bwd{l}")
    return dz, dxbc, ddtr, dtaps, dpre, dpost


def hgrn_fwd(l, seg, pars):
    blocks = tl_fwd(hgrn_pre, f"hgrn_pre_fwd{l}", [seg], pars["pre"], [HM64] * 4)
    o, states = chunk_fwd(hgrn_chunk, f"hgrn_scan_fwd{l}", blocks, (HEAD_DIM, HEAD_DIM), HEAD_DIM, HGRN_GROUP)
    (out,) = tl_fwd(hgrn_post, f"hgrn_post_fwd{l}", [o, seg], pars["post"], [D_GROUP])
    return out, (seg, blocks, states, o)


def hgrn_bwd(l, saved, pars, dout):
    seg, blocks, states, o = saved
    (do, dseg1), dpost = tl_bwd(hgrn_post, f"hgrn_post_bwd{l}", [o, seg], pars["post"], [[dout]])
    dq, dkk, dlf, di = chunk_bwd(hgrn_chunk, f"hgrn_scan_bwd{l}", blocks, states, do, HGRN_GROUP)
    (dseg2,), dpre = tl_bwd(hgrn_pre, f"hgrn_pre_bwd{l}", [seg], pars["pre"], [[dq], [dkk], [dlf], [di]])
    return add_rows([dseg1, dseg2], f"hgrn_dseg{l}"), dpre, dpost


def layer_fwd(l, x, wts, pars, vfirst):
    proj = matmul(x, wts["in"], "nn", f"proj_fwd{l}")
    seg_h, seg_r = Cols(proj, SEG_HGRN, 1024), Cols(proj, SEG_RWKV, 1024)
    z, xbc, dtr = Cols(proj, SEG_Z, D_GROUP), Cols(proj, SEG_XBC, SSD_XBC), Cols(proj, SEG_DT, 128)
    ya, v_rwkv, sa = rwkv_fwd(l, seg_r, pars["rwkv"]["taps"], pars["rwkv"], vfirst)
    yb, sb = attn_mix_fwd(l, proj)
    yc, sc = ssd_fwd(l, z, xbc, dtr, pars["ssd"])
    yd, sd = hgrn_fwd(l, seg_h, pars["hgrn"])
    mix = [ya, yb, yc, yd]
    mo = matmul(mix, wts["out"], "nn", f"out_fwd{l}")
    (x1,) = tl_fwd(ln_res, f"ln1_fwd{l}", [x, mo], pars["ln1"], [D_MODEL])
    u = matmul(x1, wts["up"], "nn", f"up_fwd{l}")
    dn = matmul(u, wts["down"], "nn", f"down_fwd{l}", a_relu2=True)
    (x2,) = tl_fwd(ln_res, f"ln2_fwd{l}", [x1, dn], pars["ln2"], [D_MODEL])
    return x2, v_rwkv, (x, sa, sb, sc, sd, mix, mo, x1, u, dn)


def layer_bwd(l, saved, wts, pars, dx2, dv_extra):
    x, sa, sb, sc, sd, mix, mo, x1, u, dn = saved
    S = x.shape[0]
    g = {}
    (dx1a, ddn), g["ln2"] = tl_bwd(ln_res, f"ln2_bwd{l}", [x1, dn], pars["ln2"], [[dx2]])
    g["down"] = matmul(u, ddn, "tn", f"down_dw{l}", a_relu2=True)
    du = matmul(ddn, wts["down"], "nt", f"down_dx{l}", relu2_grad_of=u)
    g["up"] = matmul(x1, du, "tn", f"up_dw{l}", out_col_shards=True)
    dx1 = matmul(du, wts["up"], "nt", f"up_dx{l}", add=dx1a)
    (dxa, dmo), g["ln1"] = tl_bwd(ln_res, f"ln1_bwd{l}", [x, mo], pars["ln1"], [[dx1]])
    g["out"] = matmul(mix, dmo, "tn", f"out_dw{l}")
    dmix = matmul(dmo, wts["out"], "nt", f"out_dx{l}")
    dya, dyb, dyc, dyd = (Cols(dmix, j * D_GROUP, D_GROUP) for j in range(4))
    dseg_r, dvfirst, g["rwkv_taps"], g["rwkv_pre"], g["rwkv_post"] = rwkv_bwd(l, sa, pars["rwkv"], dya, dv_extra)
    dq, dk, dv = attn_mix_bwd(l, sb, dyb)
    dz, dxbc, ddtr, g["ssd_taps"], g["ssd_pre"], g["ssd_post"] = ssd_bwd(l, sc, pars["ssd"], dyc)
    dseg_h, g["hgrn_pre"], g["hgrn_post"] = hgrn_bwd(l, sd, pars["hgrn"], dyd)
    dproj = jnp.concatenate([dseg_h, dseg_r, dq, dk, dv, dz, dxbc, ddtr, jnp.zeros((S, PROJ_W - SEG_DT - 128), F32)], axis=1)
    g["in"] = matmul(x, dproj, "tn", f"proj_dw{l}")
    dx = matmul(dproj, wts["in"], "nt", f"proj_dx{l}", add=dxa)
    return dx, dvfirst, g


SMALL = ("lower_bounds", "w_in_vres", "mu_shift", "mu_vres", "rwkv_w0", "rwkv_w2", "rwkv_a0", "rwkv_a2", "rwkv_g2",
         "rwkv_k_k", "rwkv_k_a", "rwkv_r_k", "rwkv_lnx_w", "rwkv_lnx_b", "rwkv_v0", "rwkv_v2", "ssd_conv_w",
         "ssd_conv_b", "ssd_dt_bias", "ssd_A_log", "ssd_D", "ssd_norm_w", "hgrn_norm_w", "ln1_w", "ln1_b", "ln2_w", "ln2_b")
BIG = ("w_in", "w_out", "w_up", "w_down")
SMALL_SHARDED = {"w_in_vres": 1, "rwkv_w2": 2, "rwkv_a2": 2, "rwkv_g2": 2, "rwkv_v2": 2, "ssd_conv_w": 2}
WEIGHTS = ("lower_bounds", "w_in", "w_in_vres", "mu_shift", "mu_vres", "rwkv_w0", "rwkv_w2", "rwkv_a0", "rwkv_a2",
           "rwkv_g2", "rwkv_k_k", "rwkv_k_a", "rwkv_r_k", "rwkv_lnx_w", "rwkv_lnx_b", "rwkv_v0", "rwkv_v2",
           "ssd_conv_w", "ssd_conv_b", "ssd_dt_bias", "ssd_A_log", "ssd_D", "ssd_norm_w", "hgrn_norm_w", "w_out",
           "ln1_w", "ln1_b", "w_up", "w_down", "ln2_w", "ln2_b")


def _row(v, width=None):
    v = v.reshape(1, -1).astype(F32)
    if width is not None and v.shape[1] < width:
        v = jnp.pad(v, ((0, 0), (0, width - v.shape[1])))
    return v


def _rows_at(m, rows, at):
    return jnp.pad(m.astype(F32), ((at, rows - at - m.shape[0]), (0, 0)))


SHARD_COLS = IN_COLS // 4


def _shard_runs(start, width):
    runs, pos = [], start
    while pos < start + width:
        s = pos // SHARD_COLS
        end = min(start + width, (s + 1) * SHARD_COLS)
        runs.append((s, pos - s * SHARD_COLS, end - s * SHARD_COLS))
        pos = end
    return runs


def _pad_w_in(shards, vres):
    rows = shards.shape[1]
    out, pos = [], 0
    for start, width, at in sorted(_PIECES, key=lambda p: p[2]):
        if at > pos:
            out.append(jnp.zeros((rows, at - pos), shards.dtype))
        out += [shards[s, :, lo:hi] for s, lo, hi in _shard_runs(start, width)]
        pos = at + width
        if at == SEG_RWKV and vres is not None:
            out.append(vres.astype(shards.dtype))
            pos += vres.shape[1]
    out.append(jnp.zeros((rows, PROJ_W - pos), shards.dtype))
    return jnp.concatenate(out, axis=1)


def _unpad_w_in(g):
    shards = [[] for _ in range(4)]
    for start, width, at in _PIECES:
        for s, lo, hi in _shard_runs(start, width):
            first = at + s * SHARD_COLS + lo - start
            shards[s].append(g[:, first:first + hi - lo])
    return jnp.stack([jnp.concatenate(p, axis=1) for p in shards])


def layer_params(l, sp, prep):
    lb, mu, om = prep[l], prep[2 + 2 * l], prep[3 + 2 * l]
    pre = [_row(sp["rwkv_w0"][l]), _rows_at(sp["rwkv_w2"][l], 128, 0), _row(sp["rwkv_a0"][l]),
           _rows_at(sp["rwkv_a2"][l], 128, 32), _rows_at(sp["rwkv_g2"][l], 128, 64),
           _row(sp["rwkv_k_k"][l]), _row(sp["rwkv_k_a"][l])]
    if l:
        pre += [_row(sp["rwkv_v0"][l - 1]), _rows_at(sp["rwkv_v2"][l - 1], 128, 0)]
    return {
        "rwkv": {"taps": jnp.concatenate([mu, om], axis=0), "pre": pre,
                 "post": [_row(sp["rwkv_lnx_w"][l]), _row(sp["rwkv_lnx_b"][l]), _row(sp["rwkv_r_k"][l])]},
        "ssd": {"taps": sp["ssd_conv_w"][l].astype(F32),
                "pre": [_row(sp["ssd_conv_b"][l]), _row(sp["ssd_dt_bias"][l], 128), _row(sp["ssd_A_log"][l], 128)],
                "post": [_row(sp["ssd_D"][l], 128), _row(sp["ssd_norm_w"][l])]},
        "hgrn": {"pre": [lb], "post": [_row(sp["hgrn_norm_w"][l])]},
        "ln1": [_row(sp["ln1_w"][l]), _row(sp["ln1_b"][l])],
        "ln2": [_row(sp["ln2_w"][l]), _row(sp["ln2_b"][l])],
    }


def _mu_full(sp, l):
    parts = [sp["mu_shift"][l].reshape(1, -1)]
    if l:
        parts.append(sp["mu_vres"][l - 1].reshape(1, -1))
    return _row(jnp.concatenate(parts, axis=1), 1024)


def local_step(x, target, big, sp):
    prep_in = [sp["lower_bounds"].astype(F32), _mu_full(sp, 0), _mu_full(sp, 1)]
    prep = small_fwd(param_prep, "param_prep_fwd", prep_in,
                     [(1, D_GROUP), (1, D_GROUP), (1, 1024), (1, 1024), (1, 1024), (1, 1024)])
    pars, wts = [], []
    for l in range(DEPTH):
        pars.append(layer_params(l, sp, prep))
        vres = sp["w_in_vres"][l - 1].astype(BF16) if l else None
        wts.append({"in": _pad_w_in(big["w_in"][:, l], vres), "out": big["w_out"][l],
                    "up": ShardedWeight(big["w_up"], l, 1), "down": ShardedWeight(big["w_down"], l, 0)})
    h, vfirst, saved = x, None, []
    for l in range(DEPTH):
        h, v_l, sv = layer_fwd(l, h, wts[l], pars[l], vfirst)
        vfirst = v_l if l == 0 else vfirst
        saved.append(sv)
    loss_row, dh = loss_and_grad(h, target, "loss")
    grads, dv_extra = [None] * DEPTH, []
    for l in reversed(range(DEPTH)):
        dh, dvfirst, grads[l] = layer_bwd(l, saved[l], wts[l], pars[l], dh, dv_extra)
        dv_extra = [dvfirst] if l else []
    cts = [grads[0]["hgrn_pre"][0], grads[1]["hgrn_pre"][0]]
    for l in range(DEPTH):
        cts += [grads[l]["rwkv_taps"][0:1], grads[l]["rwkv_taps"][1:2]]
    d_lower, d_mu0, d_mu1 = small_bwd(param_prep, "param_prep_bwd", prep_in, cts)
    d_mu = [d_mu0, d_mu1]
    gb = {"w_in": [_unpad_w_in(grads[l]["in"]) for l in range(DEPTH)], "w_out": [grads[l]["out"] for l in range(DEPTH)],
          "w_up": [grads[l]["up"] for l in range(DEPTH)], "w_down": [grads[l]["down"] for l in range(DEPTH)]}
    st = lambda f: jnp.stack([f(l) for l in range(DEPTH)])
    g1 = grads[1]
    gs = {
        "lower_bounds": d_lower,
        "w_in_vres": g1["in"][None, :, VRES_COL:VRES_COL + 32],
        "mu_shift": st(lambda l: d_mu[l][0, :896]),
        "mu_vres": d_mu[1][:, 896:928],
        "rwkv_w0": st(lambda l: grads[l]["rwkv_pre"][0][0]),
        "rwkv_w2": st(lambda l: grads[l]["rwkv_pre"][1][0:32]),
        "rwkv_a0": st(lambda l: grads[l]["rwkv_pre"][2][0]),
        "rwkv_a2": st(lambda l: grads[l]["rwkv_pre"][3][32:64]),
        "rwkv_g2": st(lambda l: grads[l]["rwkv_pre"][4][64:128]),
        "rwkv_k_k": st(lambda l: grads[l]["rwkv_pre"][5][0]),
        "rwkv_k_a": st(lambda l: grads[l]["rwkv_pre"][6][0]),
        "rwkv_r_k": st(lambda l: grads[l]["rwkv_post"][2].reshape(N_HEADS, HEAD_DIM)),
        "rwkv_lnx_w": st(lambda l: grads[l]["rwkv_post"][0][0]),
        "rwkv_lnx_b": st(lambda l: grads[l]["rwkv_post"][1][0]),
        "rwkv_v0": g1["rwkv_pre"][7],
        "rwkv_v2": g1["rwkv_pre"][8][None, 0:32],
        "ssd_conv_w": st(lambda l: grads[l]["ssd_taps"]),
        "ssd_conv_b": st(lambda l: grads[l]["ssd_pre"][0][0]),
        "ssd_dt_bias": st(lambda l: grads[l]["ssd_pre"][1][0, :N_HEADS]),
        "ssd_A_log": st(lambda l: grads[l]["ssd_pre"][2][0, :N_HEADS]),
        "ssd_D": st(lambda l: grads[l]["ssd_post"][0][0, :N_HEADS]),
        "ssd_norm_w": st(lambda l: grads[l]["ssd_post"][1][0]),
        "hgrn_norm_w": st(lambda l: grads[l]["hgrn_post"][0][0]),
        "ln1_w": st(lambda l: grads[l]["ln1"][0][0]),
        "ln1_b": st(lambda l: grads[l]["ln1"][1][0]),
        "ln2_w": st(lambda l: grads[l]["ln2"][0][0]),
        "ln2_b": st(lambda l: grads[l]["ln2"][1][0]),
    }
    return loss_row, dh, gb, gs


def _pack(vecs):
    parts, meta, row = [], [], 0
    for v in vecs:
        rows = -(-v.size // 1024) * 8
        flat = v.reshape(-1).astype(F32)
        parts.append(jnp.pad(flat, (0, rows * 128 - v.size)).reshape(rows, 128))
        meta.append((row, v.shape))
        row += rows
    return jnp.concatenate(parts, axis=0), meta


def _unpack(packed, meta):
    out = []
    for row, shape in meta:
        size = math.prod(shape)
        rows = -(-size // 1024) * 8
        out.append(packed[row:row + rows].reshape(-1)[:size].reshape(shape))
    return out


def _to_shards(name, g):
    if name in ("w_in", "w_up"):
        return g
    return g.reshape(4, g.shape[0] // 4, g.shape[1])


def _from_chips(name, g):
    if name != "w_out":
        return g
    return jnp.transpose(g, (1, 0, 2, 3)).reshape(g.shape[1], 4 * g.shape[2], g.shape[3])


INPUT_NAMES = ("x",) + WEIGHTS + ("loss_target",) + tuple("m_" + n for n in WEIGHTS) + tuple("v_" + n for n in WEIGHTS)


def _step(*args):
    a = dict(zip(INPUT_NAMES, args, strict=True))
    chip = 2 * lax.axis_index("x") + lax.axis_index("y")

    sharded_names = list(SMALL_SHARDED)
    small_pack, small_meta = _pack([a[n] for n in sharded_names])
    own = [a[n].astype(BF16) for n in BIG]
    gathered = gather_chips(own, small_pack, "gather_weights")
    here = lambda full, mine: lax.dynamic_update_slice(full, mine[None], (chip,) + (0,) * mine.ndim)
    big = {n: _from_chips(n, here(g, o)) for n, g, o in zip(BIG, gathered, own)}
    sp = {n: a[n] for n in SMALL if n not in SMALL_SHARDED}
    small_all = here(gathered[-1], small_pack)
    per_chip = [_unpack(small_all[s], small_meta) for s in range(4)]
    for j, n in enumerate(sharded_names):
        sp[n] = jnp.concatenate([per_chip[s][j] for s in range(4)], axis=SMALL_SHARDED[n])

    loss_row, gx, gb, gs = local_step(a["x"][0], a["loss_target"][0], big, sp)

    partials = [jnp.stack([_to_shards(n, gb[n][l]) for l in range(DEPTH)]) for n in BIG]
    got = pair_exchange(partials, "pair_exchange")
    ids = jnp.stack([lax.axis_index("c"), chip]).astype(jnp.int32)
    chip_sums = [pair_sum(p, q, ids, f"pair_sum_{n}") for n, p, q in zip(BIG, partials, got)]
    slots = reduce_chips(chip_sums, "reduce_big")
    mine = [sum_chips(sl, p, q, ids, f"sum_{n}") for n, sl, p, q in zip(BIG, slots, partials, got)]
    summed = sibling_exchange(mine, "exchange_big")
    out_g, out_d, out_m, out_v = {}, {}, {}, {}
    for n, g in zip(BIG, summed):
        if n == "w_in":
            to_t, from_t = (lambda t: jnp.transpose(t, (2, 0, 1))), (lambda t: jnp.transpose(t, (1, 2, 0)))
            g_t = to_t(g)
            res = adamw(to_t(a[n]), g_t, to_t(a["m_" + n]), to_t(a["v_" + n]), f"adamw_{n}", lane_tiled=True)
            out_g[n], (out_d[n], out_m[n], out_v[n]) = from_t(g_t), [from_t(r) for r in res]
            continue
        out_g[n] = g
        out_d[n], out_m[n], out_v[n] = adamw(a[n], g, a["m_" + n], a["v_" + n], f"adamw_{n}")

    vec, meta = _pack([loss_row] + [gs[n] for n in SMALL])
    total = sum_parts(gather_devices(vec, "gather_small"), "sum_small")
    parts = _unpack(total, meta)
    loss = parts[0][0, 0]
    g_small = {}
    for n, g in zip(SMALL, parts[1:]):
        if n in SMALL_SHARDED:
            ax = SMALL_SHARDED[n]
            size = a[n].shape[ax]
            g = lax.dynamic_slice_in_dim(g, chip * size, size, axis=ax)
        g_small[n] = g
    pw, pmeta = _pack([a[n] for n in SMALL])
    pg, _ = _pack([g_small[n] for n in SMALL])
    pm, _ = _pack([a["m_" + n] for n in SMALL])
    pv, _ = _pack([a["v_" + n] for n in SMALL])
    d, nm, nv = adamw(pw[None], pg[None], pm[None], pv[None], "adamw_small")
    for n, dd, mm, vv in zip(SMALL, _unpack(d[0], pmeta), _unpack(nm[0], pmeta), _unpack(nv[0], pmeta)):
        out_g[n], out_d[n], out_m[n], out_v[n] = g_small[n], dd, mm, vv

    return (loss, gx[None], *[out_g[n] for n in WEIGHTS], *[out_d[n] for n in WEIGHTS],
            *[out_m[n] for n in WEIGHTS], *[out_v[n] for n in WEIGHTS])


def kernel(x, lower_bounds, w_in, w_in_vres, mu_shift, mu_vres, rwkv_w0, rwkv_w2, rwkv_a0, rwkv_a2, rwkv_g2, rwkv_k_k, rwkv_k_a, rwkv_r_k, rwkv_lnx_w, rwkv_lnx_b, rwkv_v0, rwkv_v2, ssd_conv_w, ssd_conv_b, ssd_dt_bias, ssd_A_log, ssd_D, ssd_norm_w, hgrn_norm_w, w_out, ln1_w, ln1_b, w_up, w_down, ln2_w, ln2_b, loss_target, m_lower_bounds, m_w_in, m_w_in_vres, m_mu_shift, m_mu_vres, m_rwkv_w0, m_rwkv_w2, m_rwkv_a0, m_rwkv_a2, m_rwkv_g2, m_rwkv_k_k, m_rwkv_k_a, m_rwkv_r_k, m_rwkv_lnx_w, m_rwkv_lnx_b, m_rwkv_v0, m_rwkv_v2, m_ssd_conv_w, m_ssd_conv_b, m_ssd_dt_bias, m_ssd_A_log, m_ssd_D, m_ssd_norm_w, m_hgrn_norm_w, m_w_out, m_ln1_w, m_ln1_b, m_w_up, m_w_down, m_ln2_w, m_ln2_b, v_lower_bounds, v_w_in, v_w_in_vres, v_mu_shift, v_mu_vres, v_rwkv_w0, v_rwkv_w2, v_rwkv_a0, v_rwkv_a2, v_rwkv_g2, v_rwkv_k_k, v_rwkv_k_a, v_rwkv_r_k, v_rwkv_lnx_w, v_rwkv_lnx_b, v_rwkv_v0, v_rwkv_v2, v_ssd_conv_w, v_ssd_conv_b, v_ssd_dt_bias, v_ssd_A_log, v_ssd_D, v_ssd_norm_w, v_hgrn_norm_w, v_w_out, v_ln1_w, v_ln1_b, v_w_up, v_w_down, v_ln2_w, v_ln2_b):
    return _step(x, lower_bounds, w_in, w_in_vres, mu_shift, mu_vres, rwkv_w0, rwkv_w2, rwkv_a0, rwkv_a2, rwkv_g2, rwkv_k_k, rwkv_k_a, rwkv_r_k, rwkv_lnx_w, rwkv_lnx_b, rwkv_v0, rwkv_v2, ssd_conv_w, ssd_conv_b, ssd_dt_bias, ssd_A_log, ssd_D, ssd_norm_w, hgrn_norm_w, w_out, ln1_w, ln1_b, w_up, w_down, ln2_w, ln2_b, loss_target, m_lower_bounds, m_w_in, m_w_in_vres, m_mu_shift, m_mu_vres, m_rwkv_w0, m_rwkv_w2, m_rwkv_a0, m_rwkv_a2, m_rwkv_g2, m_rwkv_k_k, m_rwkv_k_a, m_rwkv_r_k, m_rwkv_lnx_w, m_rwkv_lnx_b, m_rwkv_v0, m_rwkv_v2, m_ssd_conv_w, m_ssd_conv_b, m_ssd_dt_bias, m_ssd_A_log, m_ssd_D, m_ssd_norm_w, m_hgrn_norm_w, m_w_out, m_ln1_w, m_ln1_b, m_w_up, m_w_down, m_ln2_w, m_ln2_b, v_lower_bounds, v_w_in, v_w_in_vres, v_mu_shift, v_mu_vres, v_rwkv_w0, v_rwkv_w2, v_rwkv_a0, v_rwkv_a2, v_rwkv_g2, v_rwkv_k_k, v_rwkv_k_a, v_rwkv_r_k, v_rwkv_lnx_w, v_rwkv_lnx_b, v_rwkv_v0, v_rwkv_v2, v_ssd_conv_w, v_ssd_conv_b, v_ssd_dt_bias, v_ssd_A_log, v_ssd_D, v_ssd_norm_w, v_hgrn_norm_w, v_w_out, v_ln1_w, v_ln1_b, v_w_up, v_w_down, v_ln2_w, v_ln2_b)
```

```python
import functools
import math

import jax
import jax.numpy as jnp
from jax import lax
from jax.experimental import pallas as pl
from jax.experimental.pallas import tpu as pltpu

F32 = jnp.float32
BF16 = jnp.bfloat16
HI = lax.Precision.HIGHEST

DEPTH = 2
D_MODEL = 1024
D_GROUP = 256
HEAD_DIM = 64
N_HEADS = 4
SSD_STATE = 128
SSD_XBC = 768
SSD_CONV = 4
D_FF = 4096
ALPHA = (2.0 * DEPTH) ** 0.25
LN_EPS = 1e-5
RMS_EPS = 1e-5
RWKV_GN_EPS = HEAD_DIM * 1e-5
DILATED_BRANCHES = ((128, 1), (512, 4), (2048, 16))
ALIBI_SLOPES = tuple(2.0 ** (-8.0 * (h + 1) / N_HEADS) for h in range(N_HEADS))
ATTN_BLK = 128

ADAM_LR, ADAM_B1, ADAM_B2, ADAM_EPS, ADAM_WD, ADAM_STEP = 0.001, 0.9, 0.999, 1e-08, 0.01, 10

IN_COLS = 3716
PROJ_W = 4096
SEG_HGRN, SEG_RWKV, SEG_Q, SEG_Z, SEG_XBC, SEG_DT = 0, 1024, 2048, 2816, 3072, 3840
_PIECES = ((0, 896, SEG_RWKV), (896, 768, SEG_Q), (1664, 256, SEG_Z), (1920, 768, SEG_XBC),
           (2688, 4, SEG_DT), (2692, 1024, SEG_HGRN))
VRES_COL = SEG_RWKV + 896

HM64 = (N_HEADS, HEAD_DIM)
HM128 = (N_HEADS, SSD_STATE)
ROW_TILE = 256
SCAN_CHUNK = 128
VMEM_LIMIT = 48 * 1024 * 1024


def _cparams(sem=None):
    if sem is None:
        return pltpu.CompilerParams(vmem_limit_bytes=VMEM_LIMIT)
    return pltpu.CompilerParams(dimension_semantics=sem, vmem_limit_bytes=VMEM_LIMIT)


def _pick(n, pref):
    for t in pref:
        if n % t == 0:
            return t
    return n


def _relu2(u):
    r = jnp.maximum(u, 0.0)
    return r * r


class ShardedWeight:
    def __init__(self, arr, layer, axis):
        rows, cols = arr.shape[2:]
        assert (rows, cols)[axis] == 1024
        self.arr, self.layer, self.axis = arr, layer, axis
        self.shape = (4 * rows, cols) if axis == 0 else (rows, 4 * cols)

    def spec(self, mode, tn, tk):
        l, along_rows = self.layer, self.axis == 0
        if mode == "nn":
            assert (tk if along_rows else tn) == 1024
            index = (lambda i, j, k: (k, l, 0, j)) if along_rows else (lambda i, j, k: (j, l, k, 0))
            return pl.BlockSpec((None, None, tk, tn), index)
        assert mode == "nt" and (tn if along_rows else tk) == 1024
        index = (lambda i, j, k: (j, l, 0, k)) if along_rows else (lambda i, j, k: (k, l, j, 0))
        return pl.BlockSpec((None, None, tn, tk), index)


def matmul(a, b, mode, name, add=None, a_relu2=False, relu2_grad_of=None, out_col_shards=False):
    pieces = list(a) if isinstance(a, (list, tuple)) else [a]
    a_rows, a_cols = pieces[0].shape[0], sum(p.shape[1] for p in pieces)
    if mode == "nn":
        (M, K), N = (a_rows, a_cols), b.shape[1]
    elif mode == "nt":
        (M, K), N = (a_rows, a_cols), b.shape[0]
    else:
        (K, M), N = (a_rows, a_cols), b.shape[1]
    tm, tn, tk = _pick(M, (1024, 512, 256, 128)), _pick(N, (1024, 512, 256, 128)), _pick(K, (1024, 512, 256, 128))
    nk = K // tk
    dims = {"nn": (((1,), (0,)), ((), ())), "nt": (((1,), (1,)), ((), ())), "tn": (((0,), (0,)), ((), ()))}[mode]
    extras = [e for e in (add, relu2_grad_of) if e is not None]
    npieces = len(pieces)

    def body(*refs):
        b_ref = refs[npieces]
        o_ref = refs[-1]
        rest = list(refs[npieces + 1:-1])
        add_ref = rest.pop(0) if add is not None else None
        u_ref = rest.pop(0) if relu2_grad_of is not None else None
        k = pl.program_id(2)
        av = refs[0][...] if npieces == 1 else jnp.concatenate([r[...] for r in refs[:npieces]], axis=1)
        if a_relu2:
            av = _relu2(av)
        d = lax.dot_general(av.astype(BF16), b_ref[...].astype(BF16), dims, preferred_element_type=F32)

        @pl.when(k == 0)
        def _():
            o_ref[...] = d if add_ref is None else d + add_ref[...]

        if nk > 1:
            @pl.when(k > 0)
            def _():
                o_ref[...] += d

        if u_ref is not None:
            @pl.when(k == nk - 1)
            def _():
                o_ref[...] = o_ref[...] * (2.0 * jnp.maximum(u_ref[...], 0.0))

    if npieces == 1:
        a_specs = [pl.BlockSpec((tk, tm), lambda i, j, k: (k, i)) if mode == "tn" else pl.BlockSpec((tm, tk), lambda i, j, k: (i, k))]
    else:
        assert a_cols == (tm if mode == "tn" else tk)
        rows_of = (lambda i, j, k: (k, 0)) if mode == "tn" else (lambda i, j, k: (i, 0))
        a_specs = [pl.BlockSpec((tk if mode == "tn" else tm, p.shape[1]), rows_of) for p in pieces]
    if isinstance(b, ShardedWeight):
        b_spec = b.spec(mode, tn, tk)
    else:
        b_spec = pl.BlockSpec((tn, tk), lambda i, j, k: (j, k)) if mode == "nt" else pl.BlockSpec((tk, tn), lambda i, j, k: (k, j))
    o_spec = pl.BlockSpec((tm, tn), lambda i, j, k: (i, j))
    ins, specs = pieces + [_arr(b)] + extras, a_specs + [b_spec] + [o_spec] * len(extras)
    out_shape, out_spec = (M, N), o_spec
    if out_col_shards:
        assert tn == N // 4
        out_shape, out_spec = (4, M, tn), pl.BlockSpec((None, tm, tn), lambda i, j, k: (j, i, 0))
    return pl.pallas_call(
        body, grid=(M // tm, N // tn, nk), in_specs=specs, out_specs=out_spec,
        out_shape=jax.ShapeDtypeStruct(out_shape, F32),
        compiler_params=_cparams(("parallel", "parallel", "arbitrary")), name=name)(*ins)


def _row_spec(w, tile):
    return pl.BlockSpec((tile, w), lambda i: (i, 0))


def _par_spec(shape):
    return pl.BlockSpec(shape, lambda i: (0,) * len(shape))


class Cols:
    def __init__(self, arr, start, width):
        assert start % width == 0 or (start % 128 == 0 and width % 128 == 0)
        self.arr, self.start, self.width = arr, start, width
        self.shape, self.ndim = (arr.shape[0], width), 2


def _arr(r):
    return r.arr if isinstance(r, (Cols, ShardedWeight)) else r


def _rows_spec(a, tile):
    if isinstance(a, Cols):
        assert a.start % a.width == 0
        return pl.BlockSpec((tile, a.width), lambda i, blk=a.start // a.width: (i, blk))
    shape = a if isinstance(a, tuple) else a.shape
    if len(shape) == 3:
        return pl.BlockSpec((shape[0], tile, shape[2]), lambda i: (0, i, 0))
    return _row_spec(shape[1], tile)


def _rows_shape(S, w):
    return (w[0], S, w[1]) if isinstance(w, tuple) else (S, w)


def _rows_load(ref):
    if len(ref.shape) == 3:
        return jnp.concatenate([ref[h] for h in range(ref.shape[0])], axis=1)
    return ref[...]


def _rows_store(ref, val):
    if len(ref.shape) == 3:
        w = ref.shape[2]
        for h in range(ref.shape[0]):
            ref[h] = val[:, h * w:(h + 1) * w]
    else:
        ref[...] = val


def tl_fwd(fn, name, rows, pars, out_widths, tile=ROW_TILE):
    S = rows[0].shape[-2]
    nr = len(rows)

    def body(*refs):
        ins = [_rows_load(r) for r in refs[:nr]] + [r[...] for r in refs[nr:nr + len(pars)]]
        outs = fn(*ins)
        for o_ref, o in zip(refs[nr + len(pars):], outs):
            _rows_store(o_ref, o)

    shapes = [_rows_shape(S, w) for w in out_widths]
    return pl.pallas_call(
        body, grid=(S // tile,),
        in_specs=[_rows_spec(r, tile) for r in rows] + [_par_spec(p.shape) for p in pars],
        out_specs=[_rows_spec(s, tile) for s in shapes],
        out_shape=[jax.ShapeDtypeStruct(s, F32) for s in shapes],
        compiler_params=_cparams(("parallel",)), name=name)(*[_arr(r) for r in rows], *pars)


def tl_bwd(fn, name, rows, pars, cts, tile=ROW_TILE, row_grad=None):
    S = rows[0].shape[-2]
    nr, npar = len(rows), len(pars)
    row_grad = [True] * nr if row_grad is None else row_grad
    flat_cts = [c for group in cts for c in group]
    ncts = len(flat_cts)
    gi = [i for i in range(nr) if row_grad[i]]

    def body(*refs):
        row_v = [_rows_load(r) for r in refs[:nr]]
        par_v = [r[...] for r in refs[nr:nr + npar]]
        ct_refs = refs[nr + npar:nr + npar + ncts]
        out_refs = refs[nr + npar + ncts:]
        ct_v, pos = [], 0
        for group in cts:
            acc = _rows_load(ct_refs[pos])
            for q in range(1, len(group)):
                acc = acc + _rows_load(ct_refs[pos + q])
            pos += len(group)
            ct_v.append(acc)

        def f(diff_rows, par_vals):
            full = list(row_v)
            for idx, val in zip(gi, diff_rows):
                full[idx] = val
            return tuple(fn(*full, *par_vals))

        _, vjp = jax.vjp(f, [row_v[i] for i in gi], par_v)
        d_rows, d_pars = vjp(tuple(ct_v))
        for o_ref, g in zip(out_refs[:len(gi)], d_rows):
            _rows_store(o_ref, g)
        first = pl.program_id(0) == 0
        for o_ref, g in zip(out_refs[len(gi):], d_pars):
            @pl.when(first)
            def _(o_ref=o_ref):
                o_ref[...] = jnp.zeros_like(o_ref)
            o_ref[...] += g

    outs = pl.pallas_call(
        body, grid=(S // tile,),
        in_specs=[_rows_spec(r, tile) for r in rows] + [_par_spec(p.shape) for p in pars]
        + [_rows_spec(c, tile) for c in flat_cts],
        out_specs=[_rows_spec(rows[i].shape, tile) for i in gi] + [_par_spec(p.shape) for p in pars],
        out_shape=[jax.ShapeDtypeStruct(rows[i].shape, F32) for i in gi] + [jax.ShapeDtypeStruct(p.shape, F32) for p in pars],
        compiler_params=_cparams(("arbitrary",)), name=name)(*[_arr(r) for r in rows], *pars, *[_arr(c) for c in flat_cts])
    return list(outs[:len(gi)]), list(outs[len(gi):])


def _shift_rows(x, j):
    if j == 0:
        return x
    rolled = pltpu.roll(x, j, 0)
    row = lax.broadcasted_iota(jnp.int32, x.shape, 0)
    return jnp.where(row >= j, rolled, 0.0)


def _unshift_rows(x, j):
    if j == 0:
        return x
    S = x.shape[0]
    rolled = pltpu.roll(x, S - j, 0)
    row = lax.broadcasted_iota(jnp.int32, x.shape, 0)
    return jnp.where(row < S - j, rolled, 0.0)


def _fir_in_spec(x):
    first = x.start // 128 if isinstance(x, Cols) else 0
    return pl.BlockSpec((x.shape[0], 128), lambda j: (0, first + j))


def fir_fwd(x, taps, name):
    S, C = x.shape
    K = taps.shape[0]

    def body(x_ref, w_ref, y_ref):
        xv = x_ref[...]
        acc = jnp.zeros_like(xv)
        for k in range(K):
            acc = acc + _shift_rows(xv, K - 1 - k) * w_ref[pl.ds(k, 1), :]
        y_ref[...] = acc

    cs = pl.BlockSpec((S, 128), lambda j: (0, j))
    return pl.pallas_call(body, grid=(C // 128,), in_specs=[_fir_in_spec(x), pl.BlockSpec((K, 128), lambda j: (0, j))],
                          out_specs=cs, out_shape=jax.ShapeDtypeStruct((S, C), F32),
                          compiler_params=_cparams(("parallel",)), name=name)(_arr(x), taps)


def fir_bwd(x, taps, dy_list, name):
    S, C = x.shape
    K = taps.shape[0]
    n = len(dy_list)

    def body(*refs):
        x_ref, w_ref = refs[:2]
        dy = refs[2][...]
        for q in range(1, n):
            dy = dy + refs[2 + q][...]
        dx_ref, dw_ref, db_ref = refs[2 + n:]
        xv = x_ref[...]
        dx = jnp.zeros_like(xv)
        for k in range(K):
            j = K - 1 - k
            dx = dx + _unshift_rows(dy, j) * w_ref[pl.ds(k, 1), :]
            dw_ref[pl.ds(k, 1), :] = jnp.sum(dy * _shift_rows(xv, j), axis=0, keepdims=True)
        dx_ref[...] = dx
        db_ref[...] = jnp.sum(dy, axis=0, keepdims=True)

    cs = pl.BlockSpec((S, 128), lambda j: (0, j))
    ks = pl.BlockSpec((K, 128), lambda j: (0, j))
    bs = pl.BlockSpec((1, 128), lambda j: (0, j))
    return pl.pallas_call(body, grid=(C // 128,), in_specs=[_fir_in_spec(x), ks] + [cs] * n, out_specs=[cs, ks, bs],
                          out_shape=[jax.ShapeDtypeStruct((S, C), F32), jax.ShapeDtypeStruct((K, C), F32),
                                     jax.ShapeDtypeStruct((1, C), F32)],
                          compiler_params=_cparams(("parallel",)), name=name)(_arr(x), taps, *dy_list)


def _col(tile, lane, t):
    return jnp.sum(jnp.where(lane == t, tile, 0.0), axis=1, keepdims=True)


def _rwkv_step(s, rv, vcol):
    sa = jnp.sum(s * (-rv[3]), axis=1, keepdims=True)
    return s * rv[1] + sa * (rv[3] * rv[4]) + vcol * rv[2], sa


def _eye(n):
    return (lax.broadcasted_iota(jnp.int32, (n, n), 0) == lax.broadcasted_iota(jnp.int32, (n, n), 1)).astype(F32)


def _transposed(x):
    return lax.dot_general(_eye(x.shape[1]), x, (((1,), (1,)), ((), ())), precision=HI, preferred_element_type=F32)


def scan_fwd(r, w, k, v, kk, a, name):
    H, S, Dk = r.shape
    Dv = v.shape[1] // H
    Tc = SCAN_CHUNK
    nc = S // Tc
    rows = [r, w, k, kk, a]

    def body(*refs):
        row_refs = refs[:5]
        v_ref, y_ref, sall_ref, s_ref, vT_ref, yT_ref = refs[5:]

        @pl.when(pl.program_id(0) == 0)
        def _():
            s_ref[...] = jnp.zeros_like(s_ref)

        for h in range(H):
            vT_ref[h] = _transposed(v_ref[:, h * Dv:(h + 1) * Dv])
        yT_ref[...] = jnp.zeros_like(yT_ref)
        lane = lax.broadcasted_iota(jnp.int32, (Dv, Tc), 1)

        def step(t, states):
            new = []
            for h in range(H):
                s = states[h]
                sall_ref[t, h] = s
                rv = [ref[h, pl.ds(t, 1), :] for ref in row_refs]
                s, _ = _rwkv_step(s, rv, _col(vT_ref[h], lane, t))
                ycol = jnp.sum(s * rv[0], axis=1, keepdims=True)
                yT_ref[h] = jnp.where(lane == t, ycol, yT_ref[h])
                new.append(s)
            return tuple(new)

        states = lax.fori_loop(0, Tc, step, tuple(s_ref[h] for h in range(H)))
        for h in range(H):
            s_ref[h] = states[h]
            y_ref[:, h * Dv:(h + 1) * Dv] = _transposed(yT_ref[h])

    rs = pl.BlockSpec((H, Tc, Dk), lambda c: (0, c, 0))
    vs = pl.BlockSpec((Tc, H * Dv), lambda c: (c, 0))
    return pl.pallas_call(
        body, grid=(nc,), in_specs=[rs] * 5 + [vs],
        out_specs=[vs, pl.BlockSpec((Tc, H, Dv, Dk), lambda c: (c, 0, 0, 0))],
        out_shape=[jax.ShapeDtypeStruct((S, H * Dv), F32), jax.ShapeDtypeStruct((S, H, Dv, Dk), F32)],
        scratch_shapes=[pltpu.VMEM((H, Dv, Dk), F32), pltpu.VMEM((H, Dv, Tc), F32), pltpu.VMEM((H, Dv, Tc), F32)],
        compiler_params=_cparams(("arbitrary",)), name=name)(*rows, v)


def scan_bwd(r, w, k, v, kk, a, sall, dy, name):
    H, S, Dk = r.shape
    Dv = v.shape[1] // H
    Tc = SCAN_CHUNK
    nc = S // Tc
    rows = [r, w, k, kk, a]

    def body(*refs):
        row_refs = refs[:5]
        v_ref, dy_ref, sall_ref = refs[5:8]
        drow_refs = refs[8:13]
        dv_ref, ds_ref, vT_ref, dyT_ref, dvT_ref = refs[13:]

        @pl.when(pl.program_id(0) == 0)
        def _():
            ds_ref[...] = jnp.zeros_like(ds_ref)

        for h in range(H):
            vT_ref[h] = _transposed(v_ref[:, h * Dv:(h + 1) * Dv])
            dyT_ref[h] = _transposed(dy_ref[:, h * Dv:(h + 1) * Dv])
        dvT_ref[...] = jnp.zeros_like(dvT_ref)
        lane = lax.broadcasted_iota(jnp.int32, (Dv, Tc), 1)

        def bstep(i, carry):
            t = Tc - 1 - i
            new = []
            for h in range(H):
                ds = carry[h]
                sp = sall_ref[t, h]
                rv = [ref[h, pl.ds(t, 1), :] for ref in row_refs]
                vcol = _col(vT_ref[h], lane, t)
                dycol = _col(dyT_ref[h], lane, t)
                st, sa = _rwkv_step(sp, rv, vcol)
                drow_refs[0][h, pl.ds(t, 1), :] = jnp.sum(st * dycol, axis=0, keepdims=True)
                g = ds + dycol * rv[0]
                drow_refs[1][h, pl.ds(t, 1), :] = jnp.sum(g * sp, axis=0, keepdims=True)
                drow_refs[2][h, pl.ds(t, 1), :] = jnp.sum(g * vcol, axis=0, keepdims=True)
                dvcol = jnp.sum(g * rv[2], axis=1, keepdims=True)
                dsa = jnp.sum(g * (rv[3] * rv[4]), axis=1, keepdims=True)
                db = jnp.sum(g * sa, axis=0, keepdims=True)
                dnkk = jnp.sum(sp * dsa, axis=0, keepdims=True)
                drow_refs[3][h, pl.ds(t, 1), :] = db * rv[4] - dnkk
                drow_refs[4][h, pl.ds(t, 1), :] = db * rv[3]
                dvT_ref[h] = jnp.where(lane == t, dvcol, dvT_ref[h])
                new.append(g * rv[1] - dsa * rv[3])
            return tuple(new)

        carry = lax.fori_loop(0, Tc, bstep, tuple(ds_ref[h] for h in range(H)))
        for h in range(H):
            ds_ref[h] = carry[h]
            dv_ref[:, h * Dv:(h + 1) * Dv] = _transposed(dvT_ref[h])

    rs = pl.BlockSpec((H, Tc, Dk), lambda c: (0, nc - 1 - c, 0))
    vs = pl.BlockSpec((Tc, H * Dv), lambda c: (nc - 1 - c, 0))
    tile = pltpu.VMEM((H, Dv, Tc), F32)
    outs = pl.pallas_call(
        body, grid=(nc,),
        in_specs=[rs] * 5 + [vs, vs, pl.BlockSpec((Tc, H, Dv, Dk), lambda c: (nc - 1 - c, 0, 0, 0))],
        out_specs=[rs] * 5 + [vs],
        out_shape=[jax.ShapeDtypeStruct((H, S, Dk), F32)] * 5 + [jax.ShapeDtypeStruct((S, H * Dv), F32)],
        scratch_shapes=[pltpu.VMEM((H, Dv, Dk), F32), tile, tile, tile],
        compiler_params=_cparams(("arbitrary",)), name=name)(*rows, v, dy, sall)
    return list(outs[:5]), outs[5]


CHUNK = 128
SSD_GROUP = 4
HGRN_GROUP = 4


def chunk_fwd(fn, name, blocks, state_shape, out_width, group):
    H, S, _ = blocks[0].shape
    nc = S // CHUNK
    nb = len(blocks)

    def body(*refs):
        o_ref, sv_ref, st = refs[nb:]

        @pl.when(pl.program_id(1) == 0)
        def _():
            st[...] = jnp.zeros_like(st)

        for g in range(group):
            s0 = st[g]
            sv_ref[g, 0] = s0
            s1, out = fn(s0, *[r[g] for r in refs[:nb]])
            st[g] = s1
            o_ref[g] = out

    spec = lambda w: pl.BlockSpec((group, CHUNK, w), lambda h, c: (h, c, 0))
    return pl.pallas_call(
        body, grid=(H // group, nc), in_specs=[spec(b.shape[2]) for b in blocks],
        out_specs=[spec(out_width), pl.BlockSpec((group, 1) + state_shape, lambda h, c: (h, c, 0, 0))],
        out_shape=[jax.ShapeDtypeStruct((H, S, out_width), F32), jax.ShapeDtypeStruct((H, nc) + state_shape, F32)],
        scratch_shapes=[pltpu.VMEM((group,) + state_shape, F32)],
        compiler_params=_cparams(("parallel", "arbitrary")), name=name)(*blocks)


def chunk_bwd(fn, name, blocks, states, dout, group):
    H, S, _ = blocks[0].shape
    nc = S // CHUNK
    nb = len(blocks)
    state_shape = states.shape[2:]

    def body(*refs):
        sv_ref, do_ref = refs[nb], refs[nb + 1]
        d_refs = refs[nb + 2:2 * nb + 2]
        dst = refs[2 * nb + 2]

        @pl.when(pl.program_id(1) == 0)
        def _():
            dst[...] = jnp.zeros_like(dst)

        for g in range(group):
            _, vjp = jax.vjp(fn, sv_ref[g, 0], *[r[g] for r in refs[:nb]])
            grads = vjp((dst[g], do_ref[g]))
            dst[g] = grads[0]
            for d_ref, gr in zip(d_refs, grads[1:]):
                d_ref[g] = gr

    spec = lambda w: pl.BlockSpec((group, CHUNK, w), lambda h, c: (h, nc - 1 - c, 0))
    return pl.pallas_call(
        body, grid=(H // group, nc),
        in_specs=[spec(b.shape[2]) for b in blocks]
        + [pl.BlockSpec((group, 1) + state_shape, lambda h, c: (h, nc - 1 - c, 0, 0)), spec(dout.shape[2])],
        out_specs=[spec(b.shape[2]) for b in blocks],
        out_shape=[jax.ShapeDtypeStruct(b.shape, F32) for b in blocks],
        scratch_shapes=[pltpu.VMEM((group,) + state_shape, F32)],
        compiler_params=_cparams(("parallel", "arbitrary")), name=name)(*blocks, states, dout)


def _bdot(a, b, dims):
    return lax.dot_general(a.astype(BF16), b.astype(BF16), (dims, ((), ())), preferred_element_type=F32)


def ssd_chunk(state, cb, bb, da, xdt):
    T = cb.shape[0]
    ti = lax.broadcasted_iota(jnp.int32, (T, T), 0)
    si = lax.broadcasted_iota(jnp.int32, (T, T), 1)
    mask = ti >= si
    cs = jnp.dot(mask.astype(F32), da, precision=HI, preferred_element_type=F32)
    pick = (lax.broadcasted_iota(jnp.int32, cs.shape, 1) == 0).astype(F32)
    cs_row = lax.dot_general(pick, cs, (((1,), (1,)), ((), ())), precision=HI, preferred_element_type=F32)
    lmat = jnp.where(mask, jnp.exp(jnp.where(mask, cs - cs_row, 0.0)), 0.0)
    scores = _bdot(cb, bb, ((1,), (1,))) * lmat
    y = _bdot(scores, xdt, ((1,), (0,))) + _bdot(cb, state, ((1,), (1,))) * jnp.exp(cs[:, :HEAD_DIM])
    last = cs[T - 1:T, :]
    new_state = state * jnp.exp(last) + _bdot(xdt, bb * jnp.exp(last - cs), ((0,), (0,)))
    return new_state, y


HGRN_SUB = 16


def hgrn_chunk(state, q, k, lf, v):
    T, C = q.shape[0], HGRN_SUB
    ti = lax.broadcasted_iota(jnp.int32, (C, C), 0)
    si = lax.broadcasted_iota(jnp.int32, (C, C), 1)
    tril = (ti >= si).astype(F32)
    row = lax.broadcasted_iota(jnp.int32, (C, q.shape[1]), 0)
    outs = []
    for j in range(T // C):
        qj, kj, lj, vj = (a[j * C:(j + 1) * C] for a in (q, k, lf, v))
        b = jnp.dot(tril, lj, precision=HI, preferred_element_type=F32)
        o = _bdot(qj * jnp.exp(b), state, ((1,), (1,)))
        for s in range(C):
            m = row >= s
            e = jnp.where(m, jnp.exp(jnp.where(m, b - b[s:s + 1], 0.0)), 0.0)
            o = o + jnp.sum(qj * kj[s:s + 1] * e, axis=1, keepdims=True) * vj[s:s + 1]
        last = b[C - 1:C]
        state = state * jnp.exp(last) + _bdot(vj, kj * jnp.exp(last - b), ((0,), (0,)))
        outs.append(o)
    return state, jnp.concatenate(outs, axis=0)


def _attn_block(q, kp, kc, vp, vc, n, slope, dilation):
    blk = ATTN_BLK
    k2 = jnp.concatenate([kp, kc], axis=0)
    v2 = jnp.concatenate([vp, vc], axis=0)
    s = _bdot(q, k2, ((1,), (1,))) * (HEAD_DIM ** -0.5)
    i = lax.broadcasted_iota(jnp.int32, (blk, 2 * blk), 0)
    j = lax.broadcasted_iota(jnp.int32, (blk, 2 * blk), 1)
    dist = blk + i - j
    first_key = jnp.where(n > 0, 0, blk)
    valid = (dist >= 0) & (dist <= blk) & (j >= first_key)
    s = s - slope * (dist * dilation).astype(F32)
    s = jnp.where(valid, s, -1e30)
    m = jnp.max(s, axis=-1, keepdims=True)
    p = jnp.exp(s - m)
    l = jnp.sum(p, axis=-1, keepdims=True)
    o = _bdot(p, v2, ((1,), (0,))) / l
    lse = jnp.broadcast_to(m + jnp.log(l), o.shape)
    return o, lse


_QCOL = SEG_Q // 128
PAIR = 2 * HEAD_DIM


def _attn_specs(rows):
    cur = lambda j: pl.BlockSpec((rows, PAIR), lambda p, n: (n, j + p))
    prev = lambda j: pl.BlockSpec((rows, PAIR), lambda p, n: (jnp.maximum(n - 1, 0), j + p))
    return cur, prev


def _pair_slope(pair, h):
    return jnp.where(pair == 0, jnp.float32(ALIBI_SLOPES[h]), jnp.float32(ALIBI_SLOPES[2 + h]))


def _halves(t):
    return [t[:, h * HEAD_DIM:(h + 1) * HEAD_DIM] for h in range(2)]


def _for_classes(dilation, step):
    if dilation == 1:
        step(0)
    else:
        lax.fori_loop(0, dilation, lambda z, c: (step(z), c)[1], 0)


def attn_fwd(proj, dilation, name):
    S = proj.shape[0]
    blk = ATTN_BLK
    rows = blk * dilation
    cur, prev = _attn_specs(rows)

    def body(q_ref, kp_ref, kc_ref, vp_ref, vc_ref, o_ref, l_ref):
        pair, n = pl.program_id(0), pl.program_id(1)

        def one_class(z):
            sel = pl.ds(z, blk, stride=dilation) if dilation > 1 else pl.ds(0, blk)
            q, kp, kc, vp, vc = (_halves(r[sel, :]) for r in (q_ref, kp_ref, kc_ref, vp_ref, vc_ref))
            res = [_attn_block(q[h], kp[h], kc[h], vp[h], vc[h], n, _pair_slope(pair, h), dilation) for h in range(2)]
            o_ref[sel, :] = jnp.concatenate([r[0] for r in res], axis=1)
            l_ref[sel, :] = jnp.concatenate([r[1] for r in res], axis=1)

        _for_classes(dilation, one_class)

    return pl.pallas_call(
        body, grid=(2, S // rows),
        in_specs=[cur(_QCOL), prev(_QCOL + 2), cur(_QCOL + 2), prev(_QCOL + 4), cur(_QCOL + 4)],
        out_specs=[cur(0), cur(0)], out_shape=[jax.ShapeDtypeStruct((S, D_GROUP), F32)] * 2,
        compiler_params=_cparams(("parallel", "arbitrary")), name=name)(proj, proj, proj, proj, proj)


def attn_bwd(proj, do, dlse, dilation, name):
    S = proj.shape[0]
    blk = ATTN_BLK
    rows = blk * dilation
    cur, prev = _attn_specs(rows)
    full = pl.BlockSpec((S, PAIR), lambda p, n: (0, p))

    def body(q_ref, kp_ref, kc_ref, vp_ref, vc_ref, do_ref, dl_ref, dq_ref, dk_ref, dv_ref):
        pair, n = pl.program_id(0), pl.program_id(1)

        @pl.when(n == 0)
        def _():
            dk_ref[...] = jnp.zeros_like(dk_ref)
            dv_ref[...] = jnp.zeros_like(dv_ref)

        def one_class(z):
            sel = pl.ds(z, blk, stride=dilation) if dilation > 1 else pl.ds(0, blk)
            q, kp, kc, vp, vc, do_v, dl_v = (_halves(r[sel, :]) for r in
                                             (q_ref, kp_ref, kc_ref, vp_ref, vc_ref, do_ref, dl_ref))
            grads = []
            for h in range(2):
                f = lambda q_, kp_, kc_, vp_, vc_, h=h: _attn_block(q_, kp_, kc_, vp_, vc_, n, _pair_slope(pair, h), dilation)
                _, vjp = jax.vjp(f, q[h], kp[h], kc[h], vp[h], vc[h])
                grads.append(vjp((do_v[h], dl_v[h])))
            both = lambda j: jnp.concatenate([grads[0][j], grads[1][j]], axis=1)
            dq_ref[sel, :] = both(0)
            if dilation > 1:
                here = pl.ds(n * rows + z, blk, stride=dilation)
                before = pl.ds(jnp.maximum(n - 1, 0) * rows + z, blk, stride=dilation)
            else:
                here = pl.ds(pl.multiple_of(n * blk, blk), blk)
                before = pl.ds(pl.multiple_of(jnp.maximum(n - 1, 0) * blk, blk), blk)
            dk_ref[here, :] = dk_ref[here, :] + both(2)
            dv_ref[here, :] = dv_ref[here, :] + both(4)
            dk_ref[before, :] = dk_ref[before, :] + both(1)
            dv_ref[before, :] = dv_ref[before, :] + both(3)

        _for_classes(dilation, one_class)

    return pl.pallas_call(
        body, grid=(2, S // rows),
        in_specs=[cur(_QCOL), prev(_QCOL + 2), cur(_QCOL + 2), prev(_QCOL + 4), cur(_QCOL + 4), cur(0), cur(0)],
        out_specs=[cur(0), full, full], out_shape=[jax.ShapeDtypeStruct((S, D_GROUP), F32)] * 3,
        compiler_params=_cparams(("parallel", "arbitrary")), name=name)(proj, proj, proj, proj, proj, do, dlse)


def _head_ones(width, group):
    i = lax.broadcasted_iota(jnp.int32, (width, width), 0) // group
    j = lax.broadcasted_iota(jnp.int32, (width, width), 1) // group
    return (i == j).astype(F32)


def _group_sum(x, group):
    return jnp.dot(x, _head_ones(x.shape[1], group), precision=HI, preferred_element_type=F32)


def _spread(width_in, width_out, rep):
    i = lax.broadcasted_iota(jnp.int32, (width_in, width_out), 0)
    j = lax.broadcasted_iota(jnp.int32, (width_in, width_out), 1) // rep
    return (i == j).astype(F32)


def _hdot(a, b):
    return jnp.dot(a, b, precision=HI, preferred_element_type=F32)


def _sigmoid(x):
    return 1.0 / (1.0 + jnp.exp(-x))


def _softplus(x):
    return jnp.maximum(x, 0.0) + jnp.log(1.0 + jnp.exp(jnp.minimum(x, -x)))


def _silu(x):
    return x * _sigmoid(x)


def rwkv_pre(layer):
    def fn(*args):
        if layer == 0:
            fs, w0, w2p, a0, a2p, g2p, k_k, k_a = args
        else:
            fs, vfirst, w0, w2p, a0, a2p, g2p, k_k, k_a, v0, v2p = args
        r, k, v = fs[:, 0:256], fs[:, 256:512], fs[:, 512:768]
        lora = fs[:, 768:896]
        w_log = -_softplus(-(w0 + _hdot(jnp.tanh(lora), w2p))) - 0.5
        decay = jnp.exp(-jnp.exp(w_log))
        a = _sigmoid(a0 + _hdot(lora, a2p))
        g = _hdot(_sigmoid(lora), g2p)
        if layer > 0:
            v = v + (vfirst - v) * _sigmoid(v0 + _hdot(fs[:, 896:1024], v2p))
        kk = k * k_k
        kk = kk / jnp.maximum(jnp.sqrt(_group_sum(kk * kk, HEAD_DIM)), 1e-12)
        k = k * (1.0 + (a - 1.0) * k_a)
        return r, decay, k, v, kk, a, g
    return fn


def rwkv_post(y, r, k, v, g, lnx_w, lnx_b, r_k):
    mu = _group_sum(y, HEAD_DIM) * (1.0 / HEAD_DIM)
    yc = y - mu
    var = _group_sum(yc * yc, HEAD_DIM) * (1.0 / HEAD_DIM)
    yn = yc * lax.rsqrt(var + RWKV_GN_EPS) * lnx_w + lnx_b
    bonus = _group_sum(r * k * r_k, HEAD_DIM) * v
    return ((yn + bonus) * g,)


def attn_combine(o1, o2, o3, l1, l2, l3):
    m = jnp.maximum(jnp.maximum(l1, l2), l3)
    e1, e2, e3 = jnp.exp(l1 - m), jnp.exp(l2 - m), jnp.exp(l3 - m)
    return ((o1 * e1 + o2 * e2 + o3 * e3) / (e1 + e2 + e3),)


def ssd_pre(xc, dtr, conv_b, dt_bias, a_log):
    xbc = _silu(xc + conv_b)
    xs, bm, cm = xbc[:, 0:256], xbc[:, 256:512], xbc[:, 512:768]
    dt = _softplus(dtr + dt_bias)
    a_neg = -jnp.exp(a_log)
    wide = _spread(128, N_HEADS * SSD_STATE, SSD_STATE)
    w = _hdot(dt, wide) * _hdot(a_neg, wide)
    xdt = xs * _hdot(dt, _spread(128, D_GROUP, HEAD_DIM))
    rr = jnp.concatenate([cm[:, 0:128], cm[:, 0:128], cm[:, 128:256], cm[:, 128:256]], axis=1)
    kk = jnp.concatenate([bm[:, 0:128], bm[:, 0:128], bm[:, 128:256], bm[:, 128:256]], axis=1)
    return rr, w, kk, xdt, xs


def ssd_post(ys, z, xs, d_skip, norm_w):
    y = ys + xs * _hdot(d_skip, _spread(128, D_GROUP, HEAD_DIM))
    y = y * _silu(z)
    half = D_GROUP // 2
    parts = []
    for g in range(2):
        t = y[:, g * half:(g + 1) * half]
        parts.append(t * lax.rsqrt(jnp.mean(t * t, axis=-1, keepdims=True) + RMS_EPS))
    return (jnp.concatenate(parts, axis=1) * norm_w,)


def hgrn_pre(seg, lb):
    q, f, i = seg[:, 0:256], seg[:, 256:512], seg[:, 512:768]
    forget = lb + (1.0 - lb) * _sigmoid(f)
    return _silu(q), 1.0 - forget, jnp.log(forget), i


def hgrn_post(o, seg, norm_w):
    g = seg[:, 768:1024]
    ms = _group_sum(o * o, HEAD_DIM) * (1.0 / HEAD_DIM)
    return (o * lax.rsqrt(ms + RMS_EPS) * norm_w * _silu(g),)


def ln_res(x, y, w, b):
    z = ALPHA * x + y
    mu = jnp.mean(z, axis=-1, keepdims=True)
    zc = z - mu
    var = jnp.mean(zc * zc, axis=-1, keepdims=True)
    return (zc * lax.rsqrt(var + LN_EPS) * w + b,)


def loss_and_grad(y, tgt, name):
    S, D = y.shape
    tile = ROW_TILE

    def body(y_ref, t_ref, l_ref, dy_ref):
        e = y_ref[...] - t_ref[...]
        dy_ref[...] = e * (1.0 / D)

        @pl.when(pl.program_id(0) == 0)
        def _():
            l_ref[...] = jnp.zeros_like(l_ref)

        per_row = 0.5 * jnp.mean(e * e, axis=-1, keepdims=True)
        l_ref[...] += jnp.sum(per_row, axis=0, keepdims=True) * jnp.ones((1, 128), F32)

    return pl.pallas_call(body, grid=(S // tile,), in_specs=[_row_spec(D, tile)] * 2,
                          out_specs=[_par_spec((1, 128)), _row_spec(D, tile)],
                          out_shape=[jax.ShapeDtypeStruct((1, 128), F32), jax.ShapeDtypeStruct((S, D), F32)],
                          compiler_params=_cparams(("arbitrary",)), name=name)(y, tgt)


def add_rows(arrs, name):
    (out,) = tl_fwd(lambda *a: (functools.reduce(lambda p, q: p + q, a),), name, arrs, [], [arrs[0].shape[1]])
    return out


def small_fwd(fn, name, ins, out_shapes):
    n = len(ins)

    def body(*refs):
        outs = fn(*[r[...] for r in refs[:n]])
        for o_ref, o in zip(refs[n:], outs):
            o_ref[...] = o

    return pl.pallas_call(body, out_shape=[jax.ShapeDtypeStruct(s, F32) for s in out_shapes], name=name)(*ins)


def small_bwd(fn, name, ins, cts):
    n, m = len(ins), len(cts)

    def body(*refs):
        _, vjp = jax.vjp(lambda *a: tuple(fn(*a)), *[r[...] for r in refs[:n]])
        grads = vjp(tuple(r[...] for r in refs[n:n + m]))
        for o_ref, g in zip(refs[n + m:], grads):
            o_ref[...] = g

    return pl.pallas_call(body, out_shape=[jax.ShapeDtypeStruct(a.shape, F32) for a in ins], name=name)(*ins, *cts)


def param_prep(lower_bounds, mu0, mu1):
    e = jnp.exp(lower_bounds - jnp.max(lower_bounds, axis=0, keepdims=True))
    sm = e / jnp.sum(e, axis=0, keepdims=True)
    lb0 = sm[0:1] - sm[0:1]
    lb1 = sm[0:1] + sm[1:2] - sm[0:1]
    return lb0, lb1, mu0, 1.0 - mu0, mu1, 1.0 - mu1


def _rows_tile(rows):
    return _pick(rows, (256, 128, 64, 32, 16, 8))


def sum_parts(parts, name):
    P, rows, cols = parts.shape
    tile = _rows_tile(rows)

    def body(p_ref, o_ref):
        acc = p_ref[0]
        for p in range(1, P):
            acc = acc + p_ref[p]
        o_ref[...] = acc

    return pl.pallas_call(body, grid=(rows // tile,), in_specs=[pl.BlockSpec((P, tile, cols), lambda i: (0, i, 0))],
                          out_specs=pl.BlockSpec((tile, cols), lambda i: (i, 0)),
                          out_shape=jax.ShapeDtypeStruct((rows, cols), F32),
                          compiler_params=_cparams(("parallel",)), name=name)(parts)


def pair_sum(own, got, ids, name):
    _, P, rows, cols = own.shape
    tile = _rows_tile(rows)

    def body(ids_ref, own_ref, got_ref, o_ref):
        o_ref[0] = (own_ref[0, 0] + got_ref[0]).astype(BF16)

    grid_spec = pltpu.PrefetchScalarGridSpec(
        num_scalar_prefetch=1, grid=(P, rows // tile),
        in_specs=[pl.BlockSpec((1, 1, tile, cols), lambda s, i, ids: (ids[0], s, i, 0)),
                  pl.BlockSpec((1, tile, cols), lambda s, i, ids: (s, i, 0))],
        out_specs=pl.BlockSpec((1, tile, cols), lambda s, i, ids: (s, i, 0)))
    return pl.pallas_call(body, grid_spec=grid_spec, out_shape=jax.ShapeDtypeStruct((P, rows, cols), BF16),
                          compiler_params=_cparams(("parallel", "parallel")), name=name)(ids, own, got)


def sum_chips(slots, own, got, ids, name):
    P, rows, cols = slots.shape
    tile = _rows_tile(rows)

    def body(ids_ref, s_ref, own_ref, got_ref, o_ref):
        chip = ids_ref[1]
        mine = own_ref[0, 0] + got_ref[0]
        acc = None
        for p in range(P):
            term = jnp.where(chip == p, mine, s_ref[p].astype(F32))
            acc = term if acc is None else acc + term
        o_ref[0] = acc

    grid_spec = pltpu.PrefetchScalarGridSpec(
        num_scalar_prefetch=1, grid=(rows // tile,),
        in_specs=[pl.BlockSpec((P, tile, cols), lambda i, ids: (0, i, 0)),
                  pl.BlockSpec((1, 1, tile, cols), lambda i, ids: (ids[0], ids[1], i, 0)),
                  pl.BlockSpec((1, tile, cols), lambda i, ids: (ids[1], i, 0))],
        out_specs=pl.BlockSpec((1, tile, cols), lambda i, ids: (ids[0], i, 0)))
    return pl.pallas_call(body, grid_spec=grid_spec, out_shape=jax.ShapeDtypeStruct((2, rows, cols), F32),
                          compiler_params=_cparams(("parallel",)), name=name)(ids, slots, own, got)


def adamw(w, g, m, v, name, lane_tiled=False):
    layers, rows, cols = w.shape
    tile = _rows_tile(rows)

    def body(w_ref, g_ref, m_ref, v_ref, d_ref, nm_ref, nv_ref):
        gv = g_ref[...]
        nm = ADAM_B1 * m_ref[...] + (1.0 - ADAM_B1) * gv
        nv = ADAM_B2 * v_ref[...] + (1.0 - ADAM_B2) * jnp.square(gv)
        m_hat = nm / (1.0 - ADAM_B1 ** ADAM_STEP)
        v_hat = nv / (1.0 - ADAM_B2 ** ADAM_STEP)
        d_ref[...] = -ADAM_LR * (m_hat / (jnp.sqrt(v_hat) + ADAM_EPS) + ADAM_WD * w_ref[...])
        nm_ref[...] = nm
        nv_ref[...] = nv

    spec, grid = pl.BlockSpec((None, tile, cols), lambda l, i: (l, i, 0)), (layers, rows // tile)
    if lane_tiled:
        spec, grid = pl.BlockSpec((layers, rows, 256), lambda l, i: (0, 0, i)), (1, cols // 256)
    return pl.pallas_call(body, grid=grid, in_specs=[spec] * 4, out_specs=[spec] * 3,
                          out_shape=[jax.ShapeDtypeStruct((layers, rows, cols), F32)] * 3,
                          compiler_params=_cparams(("parallel", "parallel")), name=name)(w, g, m, v)


MESH = pl.DeviceIdType.MESH
ANY = pl.BlockSpec(memory_space=pl.ANY)


def _flip(v, bit):
    return 1 - v if bit else v


_CHIP_RELATIONS = ((1, 0), (0, 1), (1, 1))


def gather_chips(arrs, small, name):
    n = len(arrs)

    def body(*refs):
        ins, small_in = refs[:n], refs[n]
        outs, small_out = refs[n + 1:2 * n + 1], refs[2 * n + 1]
        send, recv, fsend, frecv, ssend, srecv = refs[2 * n + 2:]
        x, y, c = lax.axis_index("x"), lax.axis_index("y"), lax.axis_index("c")
        me = 2 * x + y
        chips = [(_flip(x, bx), _flip(y, by)) for bx, by in _CHIP_RELATIONS]

        def over_ici(i, r, block_chip):
            return pltpu.make_async_remote_copy(src_ref=ins[i].at[c], dst_ref=outs[i].at[block_chip, c],
                                                send_sem=send.at[i, r], recv_sem=recv.at[i, r],
                                                device_id=(chips[r][0], chips[r][1], c), device_id_type=MESH)

        def to_sibling(i, r, layer):
            blk = outs[i].at[2 * chips[r][0] + chips[r][1], layer]
            return pltpu.make_async_remote_copy(src_ref=blk, dst_ref=blk, send_sem=fsend.at[i, r],
                                                recv_sem=frecv.at[i, r], device_id=(x, y, 1 - c), device_id_type=MESH)

        first = [over_ici(i, r, me) for i in range(n) for r in range(3)]
        smalls = [pltpu.make_async_remote_copy(src_ref=small_in, dst_ref=small_out.at[me], send_sem=ssend.at[r],
                                               recv_sem=srecv.at[r], device_id=(chips[r][0], chips[r][1], c),
                                               device_id_type=MESH) for r in range(3)]
        for cp in first + smalls:
            cp.start()
        passed = []
        for r in range(3):
            for i in range(n):
                over_ici(i, r, 2 * chips[r][0] + chips[r][1]).wait_recv()
                fw = to_sibling(i, r, c)
                fw.start()
                passed.append(fw)
        for r in range(3):
            for i in range(n):
                to_sibling(i, r, 1 - c).wait_recv()
        for cp in first + passed:
            cp.wait_send()
        for cp in smalls:
            cp.wait()

    return pl.pallas_call(
        body, in_specs=[ANY] * (n + 1), out_specs=[ANY] * (n + 1),
        out_shape=[jax.ShapeDtypeStruct((4,) + a.shape, a.dtype) for a in arrs]
        + [jax.ShapeDtypeStruct((4,) + small.shape, small.dtype)],
        scratch_shapes=[pltpu.SemaphoreType.DMA((n, 3)), pltpu.SemaphoreType.DMA((n, 3)), pltpu.SemaphoreType.DMA((n, 3)),
                        pltpu.SemaphoreType.DMA((n, 3)), pltpu.SemaphoreType.DMA((3,)), pltpu.SemaphoreType.DMA((3,))],
        name=name)(*arrs, small)


_RELATIONS = tuple((r >> 2 & 1, r >> 1 & 1, r & 1) for r in range(1, 8))


def gather_devices(arr, name):
    def body(in_ref, out_ref, send, recv, loc):
        x, y, c = lax.axis_index("x"), lax.axis_index("y"), lax.axis_index("c")
        me = 4 * x + 2 * y + c
        lc = pltpu.make_async_copy(in_ref, out_ref.at[me], loc)
        lc.start()
        pending = [lc]
        for r, (bx, by, bc) in enumerate(_RELATIONS):
            cp = pltpu.make_async_remote_copy(src_ref=in_ref, dst_ref=out_ref.at[me], send_sem=send.at[r],
                                              recv_sem=recv.at[r], device_id=(_flip(x, bx), _flip(y, by), _flip(c, bc)),
                                              device_id_type=MESH)
            cp.start()
            pending.append(cp)
        for cp in pending:
            cp.wait()

    return pl.pallas_call(
        body, in_specs=[ANY], out_specs=ANY, out_shape=jax.ShapeDtypeStruct((8,) + arr.shape, arr.dtype),
        scratch_shapes=[pltpu.SemaphoreType.DMA((7,)), pltpu.SemaphoreType.DMA((7,)), pltpu.SemaphoreType.DMA(())],
        name=name)(arr)


def pair_exchange(arrs, name):
    n = len(arrs)

    def body(*refs):
        ins, outs = refs[:n], refs[n:2 * n]
        send, recv = refs[2 * n:]
        x, y, c = lax.axis_index("x"), lax.axis_index("y"), lax.axis_index("c")
        pending = []
        for i in range(n):
            for s in range(4):
                cp = pltpu.make_async_remote_copy(src_ref=ins[i].at[1 - c, s], dst_ref=outs[i].at[s],
                                                  send_sem=send.at[i, s], recv_sem=recv.at[i, s],
                                                  device_id=(x, y, 1 - c), device_id_type=MESH)
                cp.start()
                pending.append(cp)
        for cp in pending:
            cp.wait()

    return pl.pallas_call(
        body, in_specs=[ANY] * n, out_specs=[ANY] * n,
        out_shape=[jax.ShapeDtypeStruct(a.shape[1:], a.dtype) for a in arrs],
        scratch_shapes=[pltpu.SemaphoreType.DMA((n, 4)), pltpu.SemaphoreType.DMA((n, 4))],
        name=name)(*arrs)


def reduce_chips(arrs, name):
    n = len(arrs)

    def body(*refs):
        ins, outs = refs[:n], refs[n:2 * n]
        send, recv, loc = refs[2 * n:]
        x, y, c = lax.axis_index("x"), lax.axis_index("y"), lax.axis_index("c")
        me = 2 * x + y
        pending = []
        for i in range(n):
            for r, (bx, by) in enumerate(_CHIP_RELATIONS):
                px, py = _flip(x, bx), _flip(y, by)
                cp = pltpu.make_async_remote_copy(src_ref=ins[i].at[2 * px + py], dst_ref=outs[i].at[me],
                                                  send_sem=send.at[i, r], recv_sem=recv.at[i, r],
                                                  device_id=(px, py, c), device_id_type=MESH)
                cp.start()
                pending.append(cp)
        for i in range(n):
            lc = pltpu.make_async_copy(ins[i].at[me], outs[i].at[me], loc.at[i])
            lc.start()
            pending.append(lc)
        for cp in pending:
            cp.wait()

    return pl.pallas_call(
        body, in_specs=[ANY] * n, out_specs=[ANY] * n,
        out_shape=[jax.ShapeDtypeStruct(a.shape, a.dtype) for a in arrs],
        scratch_shapes=[pltpu.SemaphoreType.DMA((n, 3)), pltpu.SemaphoreType.DMA((n, 3)), pltpu.SemaphoreType.DMA((n,))],
        name=name)(*arrs)


EXCHANGE_PIECES = 8


def sibling_exchange(arrs, name):
    n = len(arrs)

    def body(*refs):
        bufs = refs[n:2 * n]
        send, recv = refs[2 * n:]
        x, y, c = lax.axis_index("x"), lax.axis_index("y"), lax.axis_index("c")
        pending = []
        for i in range(n):
            rows = bufs[i].shape[1] // EXCHANGE_PIECES
            for j in range(EXCHANGE_PIECES):
                piece = bufs[i].at[c, pl.ds(j * rows, rows)]
                cp = pltpu.make_async_remote_copy(src_ref=piece, dst_ref=piece, send_sem=send.at[i, j],
                                                  recv_sem=recv.at[i, j], device_id=(x, y, 1 - c), device_id_type=MESH)
                cp.start()
                pending.append(cp)
        for i in range(n):
            rows = bufs[i].shape[1] // EXCHANGE_PIECES
            for j in range(EXCHANGE_PIECES):
                landed = bufs[i].at[1 - c, pl.ds(j * rows, rows)]
                pltpu.make_async_remote_copy(src_ref=landed, dst_ref=landed, send_sem=send.at[i, j], recv_sem=recv.at[i, j],
                                             device_id=(x, y, 1 - c), device_id_type=MESH).wait_recv()
        for cp in pending:
            cp.wait_send()

    return pl.pallas_call(
        body, in_specs=[ANY] * n, out_specs=[ANY] * n,
        out_shape=[jax.ShapeDtypeStruct(a.shape, a.dtype) for a in arrs], input_output_aliases={i: i for i in range(n)},
        scratch_shapes=[pltpu.SemaphoreType.DMA((n, EXCHANGE_PIECES)), pltpu.SemaphoreType.DMA((n, EXCHANGE_PIECES))],
        name=name)(*arrs)


def rwkv_fwd(l, seg, taps, pars, vfirst):
    fs = fir_fwd(seg, taps, f"rwkv_shift_fwd{l}")
    rows = [fs] + ([vfirst] if l else [])
    R, W, K, V, KK, A, G = tl_fwd(rwkv_pre(l), f"rwkv_pre_fwd{l}", rows, pars["pre"],
                                  [HM64, HM64, HM64, D_GROUP, HM64, HM64, D_GROUP])
    Y, sall = scan_fwd(R, W, K, V, KK, A, f"rwkv_scan_fwd{l}")
    (out,) = tl_fwd(rwkv_post, f"rwkv_post_fwd{l}", [Y, R, K, V, G], pars["post"], [D_GROUP])
    return out, V, (seg, taps, rows, R, W, K, V, KK, A, G, Y, sall)


def rwkv_bwd(l, saved, pars, dout, dv_extra):
    seg, taps, rows, R, W, K, V, KK, A, G, Y, sall = saved
    (dY, dR1, dK1, dV1, dG), dpost = tl_bwd(rwkv_post, f"rwkv_post_bwd{l}", [Y, R, K, V, G], pars["post"], [[dout]])
    (dR2, dW, dK2, dKK, dA), dV2 = scan_bwd(R, W, K, V, KK, A, sall, dY, f"rwkv_scan_bwd{l}")
    cts = [[dR1, dR2], [dW], [dK1, dK2], [dV1, dV2] + dv_extra, [dKK], [dA], [dG]]
    drows, dpre = tl_bwd(rwkv_pre(l), f"rwkv_pre_bwd{l}", rows, pars["pre"], cts)
    dseg, dtaps, _ = fir_bwd(seg, taps, [drows[0]], f"rwkv_shift_bwd{l}")
    return dseg, (drows[1] if l else None), dtaps, dpre, dpost


def attn_mix_fwd(l, proj):
    os_, ls_ = [], []
    for b, (_, d) in enumerate(DILATED_BRANCHES):
        o, lse = attn_fwd(proj, d, f"attn_fwd{l}_{b}")
        os_.append(o)
        ls_.append(lse)
    (out,) = tl_fwd(attn_combine, f"attn_combine_fwd{l}", os_ + ls_, [], [D_GROUP])
    return out, (proj, os_, ls_)


def attn_mix_bwd(l, saved, dout):
    proj, os_, ls_ = saved
    drows, _ = tl_bwd(attn_combine, f"attn_combine_bwd{l}", os_ + ls_, [], [[dout]])
    grads = [attn_bwd(proj, drows[b], drows[3 + b], d, f"attn_bwd{l}_{b}") for b, (_, d) in enumerate(DILATED_BRANCHES)]
    return tuple(add_rows([g[j] for g in grads], f"attn_d{'qkv'[j]}{l}") for j in range(3))


def ssd_fwd(l, z, xbc, dtr, pars):
    xc = fir_fwd(xbc, pars["taps"], f"ssd_conv_fwd{l}")
    rr, w, kk, xdt, xs = tl_fwd(ssd_pre, f"ssd_pre_fwd{l}", [xc, dtr], pars["pre"], [HM128, HM128, HM128, HM64, D_GROUP])
    blocks = [rr, kk, w, xdt]
    ys, states = chunk_fwd(ssd_chunk, f"ssd_scan_fwd{l}", blocks, (HEAD_DIM, SSD_STATE), HEAD_DIM, SSD_GROUP)
    (out,) = tl_fwd(ssd_post, f"ssd_post_fwd{l}", [ys, z, xs], pars["post"], [D_GROUP])
    return out, (z, xbc, dtr, xc, blocks, states, xs, ys)


def ssd_bwd(l, saved, pars, dout):
    z, xbc, dtr, xc, blocks, states, xs, ys = saved
    (dys, dz, dxs), dpost = tl_bwd(ssd_post, f"ssd_post_bwd{l}", [ys, z, xs], pars["post"], [[dout]])
    drr, dkk, dw, dxdt = chunk_bwd(ssd_chunk, f"ssd_scan_bwd{l}", blocks, states, dys, SSD_GROUP)
    (dxc, ddtr), dpre = tl_bwd(ssd_pre, f"ssd_pre_bwd{l}", [xc, dtr], pars["pre"], [[drr], [dw], [dkk], [dxdt], [dxs]])
    dxbc, dtaps, _ = fir_bwd(xbc, pars["taps"], [dxc], f"ssd_conv_bwd{l}")
    return dz, dxbc, ddtr, dtaps, dpre, dpost


def hgrn_fwd(l, seg, pars):
    blocks = tl_fwd(hgrn_pre, f"hgrn_pre_fwd{l}", [seg], pars["pre"], [HM64] * 4)
    o, states = chunk_fwd(hgrn_chunk, f"hgrn_scan_fwd{l}", blocks, (HEAD_DIM, HEAD_DIM), HEAD_DIM, HGRN_GROUP)
    (out,) = tl_fwd(hgrn_post, f"hgrn_post_fwd{l}", [o, seg], pars["post"], [D_GROUP])
    return out, (seg, blocks, states, o)


def hgrn_bwd(l, saved, pars, dout):
    seg, blocks, states, o = saved
    (do, dseg1), dpost = tl_bwd(hgrn_post, f"hgrn_post_bwd{l}", [o, seg], pars["post"], [[dout]])
    dq, dkk, dlf, di = chunk_bwd(hgrn_chunk, f"hgrn_scan_bwd{l}", blocks, states, do, HGRN_GROUP)
    (dseg2,), dpre = tl_bwd(hgrn_pre, f"hgrn_pre_bwd{l}", [seg], pars["pre"], [[dq], [dkk], [dlf], [di]])
    return add_rows([dseg1, dseg2], f"hgrn_dseg{l}"), dpre, dpost


def layer_fwd(l, x, wts, pars, vfirst):
    proj = matmul(x, wts["in"], "nn", f"proj_fwd{l}")
    seg_h, seg_r = Cols(proj, SEG_HGRN, 1024), Cols(proj, SEG_RWKV, 1024)
    z, xbc, dtr = Cols(proj, SEG_Z, D_GROUP), Cols(proj, SEG_XBC, SSD_XBC), Cols(proj, SEG_DT, 128)
    ya, v_rwkv, sa = rwkv_fwd(l, seg_r, pars["rwkv"]["taps"], pars["rwkv"], vfirst)
    yb, sb = attn_mix_fwd(l, proj)
    yc, sc = ssd_fwd(l, z, xbc, dtr, pars["ssd"])
    yd, sd = hgrn_fwd(l, seg_h, pars["hgrn"])
    mix = [ya, yb, yc, yd]
    mo = matmul(mix, wts["out"], "nn", f"out_fwd{l}")
    (x1,) = tl_fwd(ln_res, f"ln1_fwd{l}", [x, mo], pars["ln1"], [D_MODEL])
    u = matmul(x1, wts["up"], "nn", f"up_fwd{l}")
    dn = matmul(u, wts["down"], "nn", f"down_fwd{l}", a_relu2=True)
    (x2,) = tl_fwd(ln_res, f"ln2_fwd{l}", [x1, dn], pars["ln2"], [D_MODEL])
    return x2, v_rwkv, (x, sa, sb, sc, sd, mix, mo, x1, u, dn)


def layer_bwd(l, saved, wts, pars, dx2, dv_extra):
    x, sa, sb, sc, sd, mix, mo, x1, u, dn = saved
    S = x.shape[0]
    g = {}
    (dx1a, ddn), g["ln2"] = tl_bwd(ln_res, f"ln2_bwd{l}", [x1, dn], pars["ln2"], [[dx2]])
    g["down"] = matmul(u, ddn, "tn", f"down_dw{l}", a_relu2=True)
    du = matmul(ddn, wts["down"], "nt", f"down_dx{l}", relu2_grad_of=u)
    g["up"] = matmul(x1, du, "tn", f"up_dw{l}", out_col_shards=True)
    dx1 = matmul(du, wts["up"], "nt", f"up_dx{l}", add=dx1a)
    (dxa, dmo), g["ln1"] = tl_bwd(ln_res, f"ln1_bwd{l}", [x, mo], pars["ln1"], [[dx1]])
    g["out"] = matmul(mix, dmo, "tn", f"out_dw{l}")
    dmix = matmul(dmo, wts["out"], "nt", f"out_dx{l}")
    dya, dyb, dyc, dyd = (Cols(dmix, j * D_GROUP, D_GROUP) for j in range(4))
    dseg_r, dvfirst, g["rwkv_taps"], g["rwkv_pre"], g["rwkv_post"] = rwkv_bwd(l, sa, pars["rwkv"], dya, dv_extra)
    dq, dk, dv = attn_mix_bwd(l, sb, dyb)
    dz, dxbc, ddtr, g["ssd_taps"], g["ssd_pre"], g["ssd_post"] = ssd_bwd(l, sc, pars["ssd"], dyc)
    dseg_h, g["hgrn_pre"], g["hgrn_post"] = hgrn_bwd(l, sd, pars["hgrn"], dyd)
    dproj = jnp.concatenate([dseg_h, dseg_r, dq, dk, dv, dz, dxbc, ddtr, jnp.zeros((S, PROJ_W - SEG_DT - 128), F32)], axis=1)
    g["in"] = matmul(x, dproj, "tn", f"proj_dw{l}")
    dx = matmul(dproj, wts["in"], "nt", f"proj_dx{l}", add=dxa)
    return dx, dvfirst, g


SMALL = ("lower_bounds", "w_in_vres", "mu_shift", "mu_vres", "rwkv_w0", "rwkv_w2", "rwkv_a0", "rwkv_a2", "rwkv_g2",
         "rwkv_k_k", "rwkv_k_a", "rwkv_r_k", "rwkv_lnx_w", "rwkv_lnx_b", "rwkv_v0", "rwkv_v2", "ssd_conv_w",
         "ssd_conv_b", "ssd_dt_bias", "ssd_A_log", "ssd_D", "ssd_norm_w", "hgrn_norm_w", "ln1_w", "ln1_b", "ln2_w", "ln2_b")
BIG = ("w_in", "w_out", "w_up", "w_down")
SMALL_SHARDED = {"w_in_vres": 1, "rwkv_w2": 2, "rwkv_a2": 2, "rwkv_g2": 2, "rwkv_v2": 2, "ssd_conv_w": 2}
WEIGHTS = ("lower_bounds", "w_in", "w_in_vres", "mu_shift", "mu_vres", "rwkv_w0", "rwkv_w2", "rwkv_a0", "rwkv_a2",
           "rwkv_g2", "rwkv_k_k", "rwkv_k_a", "rwkv_r_k", "rwkv_lnx_w", "rwkv_lnx_b", "rwkv_v0", "rwkv_v2",
           "ssd_conv_w", "ssd_conv_b", "ssd_dt_bias", "ssd_A_log", "ssd_D", "ssd_norm_w", "hgrn_norm_w", "w_out",
           "ln1_w", "ln1_b", "w_up", "w_down", "ln2_w", "ln2_b")


def _row(v, width=None):
    v = v.reshape(1, -1).astype(F32)
    if width is not None and v.shape[1] < width:
        v = jnp.pad(v, ((0, 0), (0, width - v.shape[1])))
    return v


def _rows_at(m, rows, at):
    return jnp.pad(m.astype(F32), ((at, rows - at - m.shape[0]), (0, 0)))


SHARD_COLS = IN_COLS // 4


def _shard_runs(start, width):
    runs, pos = [], start
    while pos < start + width:
        s = pos // SHARD_COLS
        end = min(start + width, (s + 1) * SHARD_COLS)
        runs.append((s, pos - s * SHARD_COLS, end - s * SHARD_COLS))
        pos = end
    return runs


def _pad_w_in(shards, vres):
    rows = shards.shape[1]
    out, pos = [], 0
    for start, width, at in sorted(_PIECES, key=lambda p: p[2]):
        if at > pos:
            out.append(jnp.zeros((rows, at - pos), shards.dtype))
        out += [shards[s, :, lo:hi] for s, lo, hi in _shard_runs(start, width)]
        pos = at + width
        if at == SEG_RWKV and vres is not None:
            out.append(vres.astype(shards.dtype))
            pos += vres.shape[1]
    out.append(jnp.zeros((rows, PROJ_W - pos), shards.dtype))
    return jnp.concatenate(out, axis=1)


def _unpad_w_in(g):
    shards = [[] for _ in range(4)]
    for start, width, at in _PIECES:
        for s, lo, hi in _shard_runs(start, width):
            first = at + s * SHARD_COLS + lo - start
            shards[s].append(g[:, first:first + hi - lo])
    return jnp.stack([jnp.concatenate(p, axis=1) for p in shards])


def layer_params(l, sp, prep):
    lb, mu, om = prep[l], prep[2 + 2 * l], prep[3 + 2 * l]
    pre = [_row(sp["rwkv_w0"][l]), _rows_at(sp["rwkv_w2"][l], 128, 0), _row(sp["rwkv_a0"][l]),
           _rows_at(sp["rwkv_a2"][l], 128, 32), _rows_at(sp["rwkv_g2"][l], 128, 64),
           _row(sp["rwkv_k_k"][l]), _row(sp["rwkv_k_a"][l])]
    if l:
        pre += [_row(sp["rwkv_v0"][l - 1]), _rows_at(sp["rwkv_v2"][l - 1], 128, 0)]
    return {
        "rwkv": {"taps": jnp.concatenate([mu, om], axis=0), "pre": pre,
                 "post": [_row(sp["rwkv_lnx_w"][l]), _row(sp["rwkv_lnx_b"][l]), _row(sp["rwkv_r_k"][l])]},
        "ssd": {"taps": sp["ssd_conv_w"][l].astype(F32),
                "pre": [_row(sp["ssd_conv_b"][l]), _row(sp["ssd_dt_bias"][l], 128), _row(sp["ssd_A_log"][l], 128)],
                "post": [_row(sp["ssd_D"][l], 128), _row(sp["ssd_norm_w"][l])]},
        "hgrn": {"pre": [lb], "post": [_row(sp["hgrn_norm_w"][l])]},
        "ln1": [_row(sp["ln1_w"][l]), _row(sp["ln1_b"][l])],
        "ln2": [_row(sp["ln2_w"][l]), _row(sp["ln2_b"][l])],
    }


def _mu_full(sp, l):
    parts = [sp["mu_shift"][l].reshape(1, -1)]
    if l:
        parts.append(sp["mu_vres"][l - 1].reshape(1, -1))
    return _row(jnp.concatenate(parts, axis=1), 1024)


def local_step(x, target, big, sp):
    prep_in = [sp["lower_bounds"].astype(F32), _mu_full(sp, 0), _mu_full(sp, 1)]
    prep = small_fwd(param_prep, "param_prep_fwd", prep_in,
                     [(1, D_GROUP), (1, D_GROUP), (1, 1024), (1, 1024), (1, 1024), (1, 1024)])
    pars, wts = [], []
    for l in range(DEPTH):
        pars.append(layer_params(l, sp, prep))
        vres = sp["w_in_vres"][l - 1].astype(BF16) if l else None
        wts.append({"in": _pad_w_in(big["w_in"][:, l], vres), "out": big["w_out"][l],
                    "up": ShardedWeight(big["w_up"], l, 1), "down": ShardedWeight(big["w_down"], l, 0)})
    h, vfirst, saved = x, None, []
    for l in range(DEPTH):
        h, v_l, sv = layer_fwd(l, h, wts[l], pars[l], vfirst)
        vfirst = v_l if l == 0 else vfirst
        saved.append(sv)
    loss_row, dh = loss_and_grad(h, target, "loss")
    grads, dv_extra = [None] * DEPTH, []
    for l in reversed(range(DEPTH)):
        dh, dvfirst, grads[l] = layer_bwd(l, saved[l], wts[l], pars[l], dh, dv_extra)
        dv_extra = [dvfirst] if l else []
    cts = [grads[0]["hgrn_pre"][0], grads[1]["hgrn_pre"][0]]
    for l in range(DEPTH):
        cts += [grads[l]["rwkv_taps"][0:1], grads[l]["rwkv_taps"][1:2]]
    d_lower, d_mu0, d_mu1 = small_bwd(param_prep, "param_prep_bwd", prep_in, cts)
    d_mu = [d_mu0, d_mu1]
    gb = {"w_in": [_unpad_w_in(grads[l]["in"]) for l in range(DEPTH)], "w_out": [grads[l]["out"] for l in range(DEPTH)],
          "w_up": [grads[l]["up"] for l in range(DEPTH)], "w_down": [grads[l]["down"] for l in range(DEPTH)]}
    st = lambda f: jnp.stack([f(l) for l in range(DEPTH)])
    g1 = grads[1]
    gs = {
        "lower_bounds": d_lower,
        "w_in_vres": g1["in"][None, :, VRES_COL:VRES_COL + 32],
        "mu_shift": st(lambda l: d_mu[l][0, :896]),
        "mu_vres": d_mu[1][:, 896:928],
        "rwkv_w0": st(lambda l: grads[l]["rwkv_pre"][0][0]),
        "rwkv_w2": st(lambda l: grads[l]["rwkv_pre"][1][0:32]),
        "rwkv_a0": st(lambda l: grads[l]["rwkv_pre"][2][0]),
        "rwkv_a2": st(lambda l: grads[l]["rwkv_pre"][3][32:64]),
        "rwkv_g2": st(lambda l: grads[l]["rwkv_pre"][4][64:128]),
        "rwkv_k_k": st(lambda l: grads[l]["rwkv_pre"][5][0]),
        "rwkv_k_a": st(lambda l: grads[l]["rwkv_pre"][6][0]),
        "rwkv_r_k": st(lambda l: grads[l]["rwkv_post"][2].reshape(N_HEADS, HEAD_DIM)),
        "rwkv_lnx_w": st(lambda l: grads[l]["rwkv_post"][0][0]),
        "rwkv_lnx_b": st(lambda l: grads[l]["rwkv_post"][1][0]),
        "rwkv_v0": g1["rwkv_pre"][7],
        "rwkv_v2": g1["rwkv_pre"][8][None, 0:32],
        "ssd_conv_w": st(lambda l: grads[l]["ssd_taps"]),
        "ssd_conv_b": st(lambda l: grads[l]["ssd_pre"][0][0]),
        "ssd_dt_bias": st(lambda l: grads[l]["ssd_pre"][1][0, :N_HEADS]),
        "ssd_A_log": st(lambda l: grads[l]["ssd_pre"][2][0, :N_HEADS]),
        "ssd_D": st(lambda l: grads[l]["ssd_post"][0][0, :N_HEADS]),
        "ssd_norm_w": st(lambda l: grads[l]["ssd_post"][1][0]),
        "hgrn_norm_w": st(lambda l: grads[l]["hgrn_post"][0][0]),
        "ln1_w": st(lambda l: grads[l]["ln1"][0][0]),
        "ln1_b": st(lambda l: grads[l]["ln1"][1][0]),
        "ln2_w": st(lambda l: grads[l]["ln2"][0][0]),
        "ln2_b": st(lambda l: grads[l]["ln2"][1][0]),
    }
    return loss_row, dh, gb, gs


def _pack(vecs):
    parts, meta, row = [], [], 0
    for v in vecs:
        rows = -(-v.size // 1024) * 8
        flat = v.reshape(-1).astype(F32)
        parts.append(jnp.pad(flat, (0, rows * 128 - v.size)).reshape(rows, 128))
        meta.append((row, v.shape))
        row += rows
    return jnp.concatenate(parts, axis=0), meta


def _unpack(packed, meta):
    out = []
    for row, shape in meta:
        size = math.prod(shape)
        rows = -(-size // 1024) * 8
        out.append(packed[row:row + rows].reshape(-1)[:size].reshape(shape))
    return out


def _to_shards(name, g):
    if name in ("w_in", "w_up"):
        return g
    return g.reshape(4, g.shape[0] // 4, g.shape[1])


def _from_chips(name, g):
    if name != "w_out":
        return g
    return jnp.transpose(g, (1, 0, 2, 3)).reshape(g.shape[1], 4 * g.shape[2], g.shape[3])


INPUT_NAMES = ("x",) + WEIGHTS + ("loss_target",) + tuple("m_" + n for n in WEIGHTS) + tuple("v_" + n for n in WEIGHTS)


def _step(*args):
    a = dict(zip(INPUT_NAMES, args, strict=True))
    chip = 2 * lax.axis_index("x") + lax.axis_index("y")

    sharded_names = list(SMALL_SHARDED)
    small_pack, small_meta = _pack([a[n] for n in sharded_names])
    own = [a[n].astype(BF16) for n in BIG]
    gathered = gather_chips(own, small_pack, "gather_weights")
    here = lambda full, mine: lax.dynamic_update_slice(full, mine[None], (chip,) + (0,) * mine.ndim)
    big = {n: _from_chips(n, here(g, o)) for n, g, o in zip(BIG, gathered, own)}
    sp = {n: a[n] for n in SMALL if n not in SMALL_SHARDED}
    small_all = here(gathered[-1], small_pack)
    per_chip = [_unpack(small_all[s], small_meta) for s in range(4)]
    for j, n in enumerate(sharded_names):
        sp[n] = jnp.concatenate([per_chip[s][j] for s in range(4)], axis=SMALL_SHARDED[n])

    loss_row, gx, gb, gs = local_step(a["x"][0], a["loss_target"][0], big, sp)

    partials = [jnp.stack([_to_shards(n, gb[n][l]) for l in range(DEPTH)]) for n in BIG]
    got = pair_exchange(partials, "pair_exchange")
    ids = jnp.stack([lax.axis_index("c"), chip]).astype(jnp.int32)
    chip_sums = [pair_sum(p, q, ids, f"pair_sum_{n}") for n, p, q in zip(BIG, partials, got)]
    slots = reduce_chips(chip_sums, "reduce_big")
    mine = [sum_chips(sl, p, q, ids, f"sum_{n}") for n, sl, p, q in zip(BIG, slots, partials, got)]
    summed = sibling_exchange(mine, "exchange_big")
    out_g, out_d, out_m, out_v = {}, {}, {}, {}
    for n, g in zip(BIG, summed):
        if n == "w_in":
            to_t, from_t = (lambda t: jnp.transpose(t, (2, 0, 1))), (lambda t: jnp.transpose(t, (1, 2, 0)))
            g_t = to_t(g)
            res = adamw(to_t(a[n]), g_t, to_t(a["m_" + n]), to_t(a["v_" + n]), f"adamw_{n}", lane_tiled=True)
            out_g[n], (out_d[n], out_m[n], out_v[n]) = from_t(g_t), [from_t(r) for r in res]
            continue
        out_g[n] = g
        out_d[n], out_m[n], out_v[n] = adamw(a[n], g, a["m_" + n], a["v_" + n], f"adamw_{n}")

    vec, meta = _pack([loss_row] + [gs[n] for n in SMALL])
    total = sum_parts(gather_devices(vec, "gather_small"), "sum_small")
    parts = _unpack(total, meta)
    loss = parts[0][0, 0]
    g_small = {}
    for n, g in zip(SMALL, parts[1:]):
        if n in SMALL_SHARDED:
            ax = SMALL_SHARDED[n]
            size = a[n].shape[ax]
            g = lax.dynamic_slice_in_dim(g, chip * size, size, axis=ax)
        g_small[n] = g
    pw, pmeta = _pack([a[n] for n in SMALL])
    pg, _ = _pack([g_small[n] for n in SMALL])
    pm, _ = _pack([a["m_" + n] for n in SMALL])
    pv, _ = _pack([a["v_" + n] for n in SMALL])
    d, nm, nv = adamw(pw[None], pg[None], pm[None], pv[None], "adamw_small")
    for n, dd, mm, vv in zip(SMALL, _unpack(d[0], pmeta), _unpack(nm[0], pmeta), _unpack(nv[0], pmeta)):
        out_g[n], out_d[n], out_m[n], out_v[n] = g_small[n], dd, mm, vv

    return (loss, gx[None], *[out_g[n] for n in WEIGHTS], *[out_d[n] for n in WEIGHTS],
            *[out_m[n] for n in WEIGHTS], *[out_v[n] for n in WEIGHTS])


def kernel(x, lower_bounds, w_in, w_in_vres, mu_shift, mu_vres, rwkv_w0, rwkv_w2, rwkv_a0, rwkv_a2, rwkv_g2, rwkv_k_k, rwkv_k_a, rwkv_r_k, rwkv_lnx_w, rwkv_lnx_b, rwkv_v0, rwkv_v2, ssd_conv_w, ssd_conv_b, ssd_dt_bias, ssd_A_log, ssd_D, ssd_norm_w, hgrn_norm_w, w_out, ln1_w, ln1_b, w_up, w_down, ln2_w, ln2_b, loss_target, m_lower_bounds, m_w_in, m_w_in_vres, m_mu_shift, m_mu_vres, m_rwkv_w0, m_rwkv_w2, m_rwkv_a0, m_rwkv_a2, m_rwkv_g2, m_rwkv_k_k, m_rwkv_k_a, m_rwkv_r_k, m_rwkv_lnx_w, m_rwkv_lnx_b, m_rwkv_v0, m_rwkv_v2, m_ssd_conv_w, m_ssd_conv_b, m_ssd_dt_bias, m_ssd_A_log, m_ssd_D, m_ssd_norm_w, m_hgrn_norm_w, m_w_out, m_ln1_w, m_ln1_b, m_w_up, m_w_down, m_ln2_w, m_ln2_b, v_lower_bounds, v_w_in, v_w_in_vres, v_mu_shift, v_mu_vres, v_rwkv_w0, v_rwkv_w2, v_rwkv_a0, v_rwkv_a2, v_rwkv_g2, v_rwkv_k_k, v_rwkv_k_a, v_rwkv_r_k, v_rwkv_lnx_w, v_rwkv_lnx_b, v_rwkv_v0, v_rwkv_v2, v_ssd_conv_w, v_ssd_conv_b, v_ssd_dt_bias, v_ssd_A_log, v_ssd_D, v_ssd_norm_w, v_hgrn_norm_w, v_w_out, v_ln1_w, v_ln1_b, v_w_up, v_w_down, v_ln2_w, v_ln2_b):
    return _step(x, lower_bounds, w_in, w_in_vres, mu_shift, mu_vres, rwkv_w0, rwkv_w2, rwkv_a0, rwkv_a2, rwkv_g2, rwkv_k_k, rwkv_k_a, rwkv_r_k, rwkv_lnx_w, rwkv_lnx_b, rwkv_v0, rwkv_v2, ssd_conv_w, ssd_conv_b, ssd_dt_bias, ssd_A_log, ssd_D, ssd_norm_w, hgrn_norm_w, w_out, ln1_w, ln1_b, w_up, w_down, ln2_w, ln2_b, loss_target, m_lower_bounds, m_w_in, m_w_in_vres, m_mu_shift, m_mu_vres, m_rwkv_w0, m_rwkv_w2, m_rwkv_a0, m_rwkv_a2, m_rwkv_g2, m_rwkv_k_k, m_rwkv_k_a, m_rwkv_r_k, m_rwkv_lnx_w, m_rwkv_lnx_b, m_rwkv_v0, m_rwkv_v2, m_ssd_conv_w, m_ssd_conv_b, m_ssd_dt_bias, m_ssd_A_log, m_ssd_D, m_ssd_norm_w, m_hgrn_norm_w, m_w_out, m_ln1_w, m_ln1_b, m_w_up, m_w_down, m_ln2_w, m_ln2_b, v_lower_bounds, v_w_in, v_w_in_vres, v_mu_shift, v_mu_vres, v_rwkv_w0, v_rwkv_w2, v_rwkv_a0, v_rwkv_a2, v_rwkv_g2, v_rwkv_k_k, v_rwkv_k_a, v_rwkv_r_k, v_rwkv_lnx_w, v_rwkv_lnx_b, v_rwkv_v0, v_rwkv_v2, v_ssd_conv_w, v_ssd_conv_b, v_ssd_dt_bias, v_ssd_A_log, v_ssd_D, v_ssd_norm_w, v_hgrn_norm_w, v_w_out, v_ln1_w, v_ln1_b, v_w_up, v_w_down, v_ln2_w, v_ln2_b)
```

```python
import functools
import math

import jax
import jax.numpy as jnp
from jax import lax
from jax.experimental import pallas as pl
from jax.experimental.pallas import tpu as pltpu

F32 = jnp.float32
BF16 = jnp.bfloat16
HI = lax.Precision.HIGHEST

DEPTH = 2
D_MODEL = 1024
D_GROUP = 256
HEAD_DIM = 64
N_HEADS = 4
SSD_STATE = 128
SSD_XBC = 768
SSD_CONV = 4
D_FF = 4096
ALPHA = (2.0 * DEPTH) ** 0.25
LN_EPS = 1e-5
RMS_EPS = 1e-5
RWKV_GN_EPS = HEAD_DIM * 1e-5
DILATED_BRANCHES = ((128, 1), (512, 4), (2048, 16))
ALIBI_SLOPES = tuple(2.0 ** (-8.0 * (h + 1) / N_HEADS) for h in range(N_HEADS))
ATTN_BLK = 128

ADAM_LR, ADAM_B1, ADAM_B2, ADAM_EPS, ADAM_WD, ADAM_STEP = 0.001, 0.9, 0.999, 1e-08, 0.01, 10

IN_COLS = 3716
PROJ_W = 4096
SEG_HGRN, SEG_RWKV, SEG_Q, SEG_Z, SEG_XBC, SEG_DT = 0, 1024, 2048, 2816, 3072, 3840
_PIECES = ((0, 896, SEG_RWKV), (896, 768, SEG_Q), (1664, 256, SEG_Z), (1920, 768, SEG_XBC),
           (2688, 4, SEG_DT), (2692, 1024, SEG_HGRN))
VRES_COL = SEG_RWKV + 896

HM64 = (N_HEADS, HEAD_DIM)
HM128 = (N_HEADS, SSD_STATE)
ROW_TILE = 256
SCAN_CHUNK = 128
VMEM_LIMIT = 48 * 1024 * 1024


def _cparams(sem=None):
    if sem is None:
        return pltpu.CompilerParams(vmem_limit_bytes=VMEM_LIMIT)
    return pltpu.CompilerParams(dimension_semantics=sem, vmem_limit_bytes=VMEM_LIMIT)


def _pick(n, pref):
    for t in pref:
        if n % t == 0:
            return t
    return n


def _relu2(u):
    r = jnp.maximum(u, 0.0)
    return r * r


class ShardedWeight:
    def __init__(self, arr, layer, axis):
        rows, cols = arr.shape[2:]
        assert (rows, cols)[axis] == 1024
        self.arr, self.layer, self.axis = arr, layer, axis
        self.shape = (4 * rows, cols) if axis == 0 else (rows, 4 * cols)

    def spec(self, mode, tn, tk):
        l, along_rows = self.layer, self.axis == 0
        if mode == "nn":
            assert (tk if along_rows else tn) == 1024
            index = (lambda i, j, k: (k, l, 0, j)) if along_rows else (lambda i, j, k: (j, l, k, 0))
            return pl.BlockSpec((None, None, tk, tn), index)
        assert mode == "nt" and (tn if along_rows else tk) == 1024
        index = (lambda i, j, k: (j, l, 0, k)) if along_rows else (lambda i, j, k: (k, l, j, 0))
        return pl.BlockSpec((None, None, tn, tk), index)


def matmul(a, b, mode, name, add=None, a_relu2=False, relu2_grad_of=None, out_col_shards=False, into=None, stacked=None):
    pieces = list(a) if isinstance(a, (list, tuple)) else [a]
    a_rows, a_cols = pieces[0].shape[0], sum(p.shape[1] for p in pieces)
    if mode == "nn":
        (M, K), N = (a_rows, a_cols), b.shape[1]
    elif mode == "nt":
        (M, K), N = (a_rows, a_cols), b.shape[0]
    else:
        (K, M), N = (a_rows, a_cols), b.shape[1]
    tm, tn, tk = _pick(M, (1024, 512, 256, 128)), _pick(N, (1024, 512, 256, 128)), _pick(K, (1024, 512, 256, 128))
    nk = K // tk
    dims = {"nn": (((1,), (0,)), ((), ())), "nt": (((1,), (1,)), ((), ())), "tn": (((0,), (0,)), ((), ()))}[mode]
    extras = [e for e in (add, relu2_grad_of) if e is not None]
    npieces = len(pieces)

    def body(*refs):
        b_ref = refs[npieces]
        o_ref = refs[-1]
        rest = list(refs[npieces + 1:-1])
        add_ref = rest.pop(0) if add is not None else None
        u_ref = rest.pop(0) if relu2_grad_of is not None else None
        k = pl.program_id(2)
        av = refs[0][...] if npieces == 1 else jnp.concatenate([r[...] for r in refs[:npieces]], axis=1)
        if a_relu2:
            av = _relu2(av)
        d = lax.dot_general(av.astype(BF16), b_ref[...].astype(BF16), dims, preferred_element_type=F32)

        @pl.when(k == 0)
        def _():
            o_ref[...] = d if add_ref is None else d + add_ref[...]

        if nk > 1:
            @pl.when(k > 0)
            def _():
                o_ref[...] += d

        if u_ref is not None:
            @pl.when(k == nk - 1)
            def _():
                o_ref[...] = o_ref[...] * (2.0 * jnp.maximum(u_ref[...], 0.0))

    if npieces == 1:
        a_specs = [pl.BlockSpec((tk, tm), lambda i, j, k: (k, i)) if mode == "tn" else pl.BlockSpec((tm, tk), lambda i, j, k: (i, k))]
    else:
        assert a_cols == (tm if mode == "tn" else tk)
        rows_of = (lambda i, j, k: (k, 0)) if mode == "tn" else (lambda i, j, k: (i, 0))
        a_specs = [pl.BlockSpec((tk if mode == "tn" else tm, p.shape[1]), rows_of) for p in pieces]
    if isinstance(b, ShardedWeight):
        b_spec = b.spec(mode, tn, tk)
    else:
        b_spec = pl.BlockSpec((tn, tk), lambda i, j, k: (j, k)) if mode == "nt" else pl.BlockSpec((tk, tn), lambda i, j, k: (k, j))
    o_spec = pl.BlockSpec((tm, tn), lambda i, j, k: (i, j))
    ins, specs = pieces + [_arr(b)] + extras, a_specs + [b_spec] + [o_spec] * len(extras)
    out_shape, out_spec = (M, N), o_spec
    if out_col_shards:
        assert tn == N // 4
        out_shape, out_spec = (4, M, tn), pl.BlockSpec((None, tm, tn), lambda i, j, k: (j, i, 0))
    aliases = {}
    if into is not None:
        layers, layer = into
        block, index = out_spec.block_shape, out_spec.index_map
        out_spec = pl.BlockSpec((None,) + tuple(block), lambda i, j, k: (layer,) + tuple(index(i, j, k)))
        out_shape = (layers,) + out_shape
        if stacked is not None:
            aliases = {len(ins): 0}
            ins, specs = ins + [stacked], specs + [pl.BlockSpec(memory_space=pl.ANY)]
    return pl.pallas_call(
        body, grid=(M // tm, N // tn, nk), in_specs=specs, out_specs=out_spec,
        out_shape=jax.ShapeDtypeStruct(out_shape, F32), input_output_aliases=aliases,
        compiler_params=_cparams(("parallel", "parallel", "arbitrary")), name=name)(*ins)


def _row_spec(w, tile):
    return pl.BlockSpec((tile, w), lambda i: (i, 0))


def _par_spec(shape):
    return pl.BlockSpec(shape, lambda i: (0,) * len(shape))


class Cols:
    def __init__(self, arr, start, width):
        assert start % width == 0 or (start % 128 == 0 and width % 128 == 0)
        self.arr, self.start, self.width = arr, start, width
        self.shape, self.ndim = (arr.shape[0], width), 2


def _arr(r):
    return r.arr if isinstance(r, (Cols, ShardedWeight)) else r


def _rows_spec(a, tile):
    if isinstance(a, Cols):
        assert a.start % a.width == 0
        return pl.BlockSpec((tile, a.width), lambda i, blk=a.start // a.width: (i, blk))
    shape = a if isinstance(a, tuple) else a.shape
    if len(shape) == 3:
        return pl.BlockSpec((shape[0], tile, shape[2]), lambda i: (0, i, 0))
    return _row_spec(shape[1], tile)


def _rows_shape(S, w):
    return (w[0], S, w[1]) if isinstance(w, tuple) else (S, w)


def _rows_load(ref):
    if len(ref.shape) == 3:
        return jnp.concatenate([ref[h] for h in range(ref.shape[0])], axis=1)
    return ref[...]


def _rows_store(ref, val):
    if len(ref.shape) == 3:
        w = ref.shape[2]
        for h in range(ref.shape[0]):
            ref[h] = val[:, h * w:(h + 1) * w]
    else:
        ref[...] = val


def tl_fwd(fn, name, rows, pars, out_widths, tile=ROW_TILE):
    S = rows[0].shape[-2]
    nr = len(rows)

    def body(*refs):
        ins = [_rows_load(r) for r in refs[:nr]] + [r[...] for r in refs[nr:nr + len(pars)]]
        outs = fn(*ins)
        for o_ref, o in zip(refs[nr + len(pars):], outs):
            _rows_store(o_ref, o)

    shapes = [_rows_shape(S, w) for w in out_widths]
    return pl.pallas_call(
        body, grid=(S // tile,),
        in_specs=[_rows_spec(r, tile) for r in rows] + [_par_spec(p.shape) for p in pars],
        out_specs=[_rows_spec(s, tile) for s in shapes],
        out_shape=[jax.ShapeDtypeStruct(s, F32) for s in shapes],
        compiler_params=_cparams(("parallel",)), name=name)(*[_arr(r) for r in rows], *pars)


def tl_bwd(fn, name, rows, pars, cts, tile=ROW_TILE, row_grad=None):
    S = rows[0].shape[-2]
    nr, npar = len(rows), len(pars)
    row_grad = [True] * nr if row_grad is None else row_grad
    flat_cts = [c for group in cts for c in group]
    ncts = len(flat_cts)
    gi = [i for i in range(nr) if row_grad[i]]

    def body(*refs):
        row_v = [_rows_load(r) for r in refs[:nr]]
        par_v = [r[...] for r in refs[nr:nr + npar]]
        ct_refs = refs[nr + npar:nr + npar + ncts]
        out_refs = refs[nr + npar + ncts:]
        ct_v, pos = [], 0
        for group in cts:
            acc = _rows_load(ct_refs[pos])
            for q in range(1, len(group)):
                acc = acc + _rows_load(ct_refs[pos + q])
            pos += len(group)
            ct_v.append(acc)

        def f(diff_rows, par_vals):
            full = list(row_v)
            for idx, val in zip(gi, diff_rows):
                full[idx] = val
            return tuple(fn(*full, *par_vals))

        _, vjp = jax.vjp(f, [row_v[i] for i in gi], par_v)
        d_rows, d_pars = vjp(tuple(ct_v))
        for o_ref, g in zip(out_refs[:len(gi)], d_rows):
            _rows_store(o_ref, g)
        first = pl.program_id(0) == 0
        for o_ref, g in zip(out_refs[len(gi):], d_pars):
            @pl.when(first)
            def _(o_ref=o_ref):
                o_ref[...] = jnp.zeros_like(o_ref)
            o_ref[...] += g

    outs = pl.pallas_call(
        body, grid=(S // tile,),
        in_specs=[_rows_spec(r, tile) for r in rows] + [_par_spec(p.shape) for p in pars]
        + [_rows_spec(c, tile) for c in flat_cts],
        out_specs=[_rows_spec(rows[i].shape, tile) for i in gi] + [_par_spec(p.shape) for p in pars],
        out_shape=[jax.ShapeDtypeStruct(rows[i].shape, F32) for i in gi] + [jax.ShapeDtypeStruct(p.shape, F32) for p in pars],
        compiler_params=_cparams(("arbitrary",)), name=name)(*[_arr(r) for r in rows], *pars, *[_arr(c) for c in flat_cts])
    return list(outs[:len(gi)]), list(outs[len(gi):])


def _shift_rows(x, j):
    if j == 0:
        return x
    rolled = pltpu.roll(x, j, 0)
    row = lax.broadcasted_iota(jnp.int32, x.shape, 0)
    return jnp.where(row >= j, rolled, 0.0)


def _unshift_rows(x, j):
    if j == 0:
        return x
    S = x.shape[0]
    rolled = pltpu.roll(x, S - j, 0)
    row = lax.broadcasted_iota(jnp.int32, x.shape, 0)
    return jnp.where(row < S - j, rolled, 0.0)


def _fir_in_spec(x):
    first = x.start // 128 if isinstance(x, Cols) else 0
    return pl.BlockSpec((x.shape[0], 128), lambda j: (0, first + j))


def fir_fwd(x, taps, name):
    S, C = x.shape
    K = taps.shape[0]

    def body(x_ref, w_ref, y_ref):
        xv = x_ref[...]
        acc = jnp.zeros_like(xv)
        for k in range(K):
            acc = acc + _shift_rows(xv, K - 1 - k) * w_ref[pl.ds(k, 1), :]
        y_ref[...] = acc

    cs = pl.BlockSpec((S, 128), lambda j: (0, j))
    return pl.pallas_call(body, grid=(C // 128,), in_specs=[_fir_in_spec(x), pl.BlockSpec((K, 128), lambda j: (0, j))],
                          out_specs=cs, out_shape=jax.ShapeDtypeStruct((S, C), F32),
                          compiler_params=_cparams(("parallel",)), name=name)(_arr(x), taps)


def fir_bwd(x, taps, dy_list, name):
    S, C = x.shape
    K = taps.shape[0]
    n = len(dy_list)

    def body(*refs):
        x_ref, w_ref = refs[:2]
        dy = refs[2][...]
        for q in range(1, n):
            dy = dy + refs[2 + q][...]
        dx_ref, dw_ref, db_ref = refs[2 + n:]
        xv = x_ref[...]
        dx = jnp.zeros_like(xv)
        for k in range(K):
            j = K - 1 - k
            dx = dx + _unshift_rows(dy, j) * w_ref[pl.ds(k, 1), :]
            dw_ref[pl.ds(k, 1), :] = jnp.sum(dy * _shift_rows(xv, j), axis=0, keepdims=True)
        dx_ref[...] = dx
        db_ref[...] = jnp.sum(dy, axis=0, keepdims=True)

    cs = pl.BlockSpec((S, 128), lambda j: (0, j))
    ks = pl.BlockSpec((K, 128), lambda j: (0, j))
    bs = pl.BlockSpec((1, 128), lambda j: (0, j))
    return pl.pallas_call(body, grid=(C // 128,), in_specs=[_fir_in_spec(x), ks] + [cs] * n, out_specs=[cs, ks, bs],
                          out_shape=[jax.ShapeDtypeStruct((S, C), F32), jax.ShapeDtypeStruct((K, C), F32),
                                     jax.ShapeDtypeStruct((1, C), F32)],
                          compiler_params=_cparams(("parallel",)), name=name)(_arr(x), taps, *dy_list)


def _col(tile, lane, t):
    return jnp.sum(jnp.where(lane == t, tile, 0.0), axis=1, keepdims=True)


def _rwkv_step(s, rv, vcol):
    sa = jnp.sum(s * (-rv[3]), axis=1, keepdims=True)
    return s * rv[1] + sa * (rv[3] * rv[4]) + vcol * rv[2], sa


def _eye(n):
    return (lax.broadcasted_iota(jnp.int32, (n, n), 0) == lax.broadcasted_iota(jnp.int32, (n, n), 1)).astype(F32)


def _transposed(x):
    return lax.dot_general(_eye(x.shape[1]), x, (((1,), (1,)), ((), ())), precision=HI, preferred_element_type=F32)


def scan_fwd(r, w, k, v, kk, a, name):
    H, S, Dk = r.shape
    Dv = v.shape[1] // H
    Tc = SCAN_CHUNK
    nc = S // Tc
    rows = [r, w, k, kk, a]

    def body(*refs):
        row_refs = refs[:5]
        v_ref, y_ref, sall_ref, s_ref, vT_ref, yT_ref = refs[5:]

        @pl.when(pl.program_id(0) == 0)
        def _():
            s_ref[...] = jnp.zeros_like(s_ref)

        for h in range(H):
            vT_ref[h] = _transposed(v_ref[:, h * Dv:(h + 1) * Dv])
        yT_ref[...] = jnp.zeros_like(yT_ref)
        lane = lax.broadcasted_iota(jnp.int32, (Dv, Tc), 1)

        def step(t, states):
            new = []
            for h in range(H):
                s = states[h]
                sall_ref[t, h] = s
                rv = [ref[h, pl.ds(t, 1), :] for ref in row_refs]
                s, _ = _rwkv_step(s, rv, _col(vT_ref[h], lane, t))
                ycol = jnp.sum(s * rv[0], axis=1, keepdims=True)
                yT_ref[h] = jnp.where(lane == t, ycol, yT_ref[h])
                new.append(s)
            return tuple(new)

        states = lax.fori_loop(0, Tc, step, tuple(s_ref[h] for h in range(H)))
        for h in range(H):
            s_ref[h] = states[h]
            y_ref[:, h * Dv:(h + 1) * Dv] = _transposed(yT_ref[h])

    rs = pl.BlockSpec((H, Tc, Dk), lambda c: (0, c, 0))
    vs = pl.BlockSpec((Tc, H * Dv), lambda c: (c, 0))
    return pl.pallas_call(
        body, grid=(nc,), in_specs=[rs] * 5 + [vs],
        out_specs=[vs, pl.BlockSpec((Tc, H, Dv, Dk), lambda c: (c, 0, 0, 0))],
        out_shape=[jax.ShapeDtypeStruct((S, H * Dv), F32), jax.ShapeDtypeStruct((S, H, Dv, Dk), F32)],
        scratch_shapes=[pltpu.VMEM((H, Dv, Dk), F32), pltpu.VMEM((H, Dv, Tc), F32), pltpu.VMEM((H, Dv, Tc), F32)],
        compiler_params=_cparams(("arbitrary",)), name=name)(*rows, v)


def scan_bwd(r, w, k, v, kk, a, sall, dy, name):
    H, S, Dk = r.shape
    Dv = v.shape[1] // H
    Tc = SCAN_CHUNK
    nc = S // Tc
    rows = [r, w, k, kk, a]

    def body(*refs):
        row_refs = refs[:5]
        v_ref, dy_ref, sall_ref = refs[5:8]
        drow_refs = refs[8:13]
        dv_ref, ds_ref, vT_ref, dyT_ref, dvT_ref = refs[13:]

        @pl.when(pl.program_id(0) == 0)
        def _():
            ds_ref[...] = jnp.zeros_like(ds_ref)

        for h in range(H):
            vT_ref[h] = _transposed(v_ref[:, h * Dv:(h + 1) * Dv])
            dyT_ref[h] = _transposed(dy_ref[:, h * Dv:(h + 1) * Dv])
        dvT_ref[...] = jnp.zeros_like(dvT_ref)
        lane = lax.broadcasted_iota(jnp.int32, (Dv, Tc), 1)

        def bstep(i, carry):
            t = Tc - 1 - i
            new = []
            for h in range(H):
                ds = carry[h]
                sp = sall_ref[t, h]
                rv = [ref[h, pl.ds(t, 1), :] for ref in row_refs]
                vcol = _col(vT_ref[h], lane, t)
                dycol = _col(dyT_ref[h], lane, t)
                st, sa = _rwkv_step(sp, rv, vcol)
                drow_refs[0][h, pl.ds(t, 1), :] = jnp.sum(st * dycol, axis=0, keepdims=True)
                g = ds + dycol * rv[0]
                drow_refs[1][h, pl.ds(t, 1), :] = jnp.sum(g * sp, axis=0, keepdims=True)
                drow_refs[2][h, pl.ds(t, 1), :] = jnp.sum(g * vcol, axis=0, keepdims=True)
                dvcol = jnp.sum(g * rv[2], axis=1, keepdims=True)
                dsa = jnp.sum(g * (rv[3] * rv[4]), axis=1, keepdims=True)
                db = jnp.sum(g * sa, axis=0, keepdims=True)
                dnkk = jnp.sum(sp * dsa, axis=0, keepdims=True)
                drow_refs[3][h, pl.ds(t, 1), :] = db * rv[4] - dnkk
                drow_refs[4][h, pl.ds(t, 1), :] = db * rv[3]
                dvT_ref[h] = jnp.where(lane == t, dvcol, dvT_ref[h])
                new.append(g * rv[1] - dsa * rv[3])
            return tuple(new)

        carry = lax.fori_loop(0, Tc, bstep, tuple(ds_ref[h] for h in range(H)))
        for h in range(H):
            ds_ref[h] = carry[h]
            dv_ref[:, h * Dv:(h + 1) * Dv] = _transposed(dvT_ref[h])

    rs = pl.BlockSpec((H, Tc, Dk), lambda c: (0, nc - 1 - c, 0))
    vs = pl.BlockSpec((Tc, H * Dv), lambda c: (nc - 1 - c, 0))
    tile = pltpu.VMEM((H, Dv, Tc), F32)
    outs = pl.pallas_call(
        body, grid=(nc,),
        in_specs=[rs] * 5 + [vs, vs, pl.BlockSpec((Tc, H, Dv, Dk), lambda c: (nc - 1 - c, 0, 0, 0))],
        out_specs=[rs] * 5 + [vs],
        out_shape=[jax.ShapeDtypeStruct((H, S, Dk), F32)] * 5 + [jax.ShapeDtypeStruct((S, H * Dv), F32)],
        scratch_shapes=[pltpu.VMEM((H, Dv, Dk), F32), tile, tile, tile],
        compiler_params=_cparams(("arbitrary",)), name=name)(*rows, v, dy, sall)
    return list(outs[:5]), outs[5]


CHUNK = 128
SSD_GROUP = 4
HGRN_GROUP = 4


def chunk_fwd(fn, name, blocks, state_shape, out_width, group):
    H, S, _ = blocks[0].shape
    nc = S // CHUNK
    nb = len(blocks)

    def body(*refs):
        o_ref, sv_ref, st = refs[nb:]

        @pl.when(pl.program_id(1) == 0)
        def _():
            st[...] = jnp.zeros_like(st)

        for g in range(group):
            s0 = st[g]
            sv_ref[g, 0] = s0
            s1, out = fn(s0, *[r[g] for r in refs[:nb]])
            st[g] = s1
            o_ref[g] = out

    spec = lambda w: pl.BlockSpec((group, CHUNK, w), lambda h, c: (h, c, 0))
    return pl.pallas_call(
        body, grid=(H // group, nc), in_specs=[spec(b.shape[2]) for b in blocks],
        out_specs=[spec(out_width), pl.BlockSpec((group, 1) + state_shape, lambda h, c: (h, c, 0, 0))],
        out_shape=[jax.ShapeDtypeStruct((H, S, out_width), F32), jax.ShapeDtypeStruct((H, nc) + state_shape, F32)],
        scratch_shapes=[pltpu.VMEM((group,) + state_shape, F32)],
        compiler_params=_cparams(("parallel", "arbitrary")), name=name)(*blocks)


def chunk_bwd(fn, name, blocks, states, dout, group):
    H, S, _ = blocks[0].shape
    nc = S // CHUNK
    nb = len(blocks)
    state_shape = states.shape[2:]

    def body(*refs):
        sv_ref, do_ref = refs[nb], refs[nb + 1]
        d_refs = refs[nb + 2:2 * nb + 2]
        dst = refs[2 * nb + 2]

        @pl.when(pl.program_id(1) == 0)
        def _():
            dst[...] = jnp.zeros_like(dst)

        for g in range(group):
            _, vjp = jax.vjp(fn, sv_ref[g, 0], *[r[g] for r in refs[:nb]])
            grads = vjp((dst[g], do_ref[g]))
            dst[g] = grads[0]
            for d_ref, gr in zip(d_refs, grads[1:]):
                d_ref[g] = gr

    spec = lambda w: pl.BlockSpec((group, CHUNK, w), lambda h, c: (h, nc - 1 - c, 0))
    return pl.pallas_call(
        body, grid=(H // group, nc),
        in_specs=[spec(b.shape[2]) for b in blocks]
        + [pl.BlockSpec((group, 1) + state_shape, lambda h, c: (h, nc - 1 - c, 0, 0)), spec(dout.shape[2])],
        out_specs=[spec(b.shape[2]) for b in blocks],
        out_shape=[jax.ShapeDtypeStruct(b.shape, F32) for b in blocks],
        scratch_shapes=[pltpu.VMEM((group,) + state_shape, F32)],
        compiler_params=_cparams(("parallel", "arbitrary")), name=name)(*blocks, states, dout)


def _bdot(a, b, dims):
    return lax.dot_general(a.astype(BF16), b.astype(BF16), (dims, ((), ())), preferred_element_type=F32)


def ssd_chunk(state, cb, bb, da, xdt):
    T = cb.shape[0]
    ti = lax.broadcasted_iota(jnp.int32, (T, T), 0)
    si = lax.broadcasted_iota(jnp.int32, (T, T), 1)
    mask = ti >= si
    cs = jnp.dot(mask.astype(F32), da, precision=HI, preferred_element_type=F32)
    pick = (lax.broadcasted_iota(jnp.int32, cs.shape, 1) == 0).astype(F32)
    cs_row = lax.dot_general(pick, cs, (((1,), (1,)), ((), ())), precision=HI, preferred_element_type=F32)
    lmat = jnp.where(mask, jnp.exp(jnp.where(mask, cs - cs_row, 0.0)), 0.0)
    scores = _bdot(cb, bb, ((1,), (1,))) * lmat
    y = _bdot(scores, xdt, ((1,), (0,))) + _bdot(cb, state, ((1,), (1,))) * jnp.exp(cs[:, :HEAD_DIM])
    last = cs[T - 1:T, :]
    new_state = state * jnp.exp(last) + _bdot(xdt, bb * jnp.exp(last - cs), ((0,), (0,)))
    return new_state, y


HGRN_SUB = 16


def hgrn_chunk(state, q, k, lf, v):
    T, C = q.shape[0], HGRN_SUB
    ti = lax.broadcasted_iota(jnp.int32, (C, C), 0)
    si = lax.broadcasted_iota(jnp.int32, (C, C), 1)
    tril = (ti >= si).astype(F32)
    row = lax.broadcasted_iota(jnp.int32, (C, q.shape[1]), 0)
    outs = []
    for j in range(T // C):
        qj, kj, lj, vj = (a[j * C:(j + 1) * C] for a in (q, k, lf, v))
        b = jnp.dot(tril, lj, precision=HI, preferred_element_type=F32)
        o = _bdot(qj * jnp.exp(b), state, ((1,), (1,)))
        for s in range(C):
            m = row >= s
            e = jnp.where(m, jnp.exp(jnp.where(m, b - b[s:s + 1], 0.0)), 0.0)
            o = o + jnp.sum(qj * kj[s:s + 1] * e, axis=1, keepdims=True) * vj[s:s + 1]
        last = b[C - 1:C]
        state = state * jnp.exp(last) + _bdot(vj, kj * jnp.exp(last - b), ((0,), (0,)))
        outs.append(o)
    return state, jnp.concatenate(outs, axis=0)


def _attn_block(q, kp, kc, vp, vc, n, slope, dilation):
    blk = ATTN_BLK
    k2 = jnp.concatenate([kp, kc], axis=0)
    v2 = jnp.concatenate([vp, vc], axis=0)
    s = _bdot(q, k2, ((1,), (1,))) * (HEAD_DIM ** -0.5)
    i = lax.broadcasted_iota(jnp.int32, (blk, 2 * blk), 0)
    j = lax.broadcasted_iota(jnp.int32, (blk, 2 * blk), 1)
    dist = blk + i - j
    first_key = jnp.where(n > 0, 0, blk)
    valid = (dist >= 0) & (dist <= blk) & (j >= first_key)
    s = s - slope * (dist * dilation).astype(F32)
    s = jnp.where(valid, s, -1e30)
    m = jnp.max(s, axis=-1, keepdims=True)
    p = jnp.exp(s - m)
    l = jnp.sum(p, axis=-1, keepdims=True)
    o = _bdot(p, v2, ((1,), (0,))) / l
    lse = jnp.broadcast_to(m + jnp.log(l), o.shape)
    return o, lse


_QCOL = SEG_Q // 128
PAIR = 2 * HEAD_DIM


def _attn_specs(rows):
    cur = lambda j: pl.BlockSpec((rows, PAIR), lambda p, n: (n, j + p))
    prev = lambda j: pl.BlockSpec((rows, PAIR), lambda p, n: (jnp.maximum(n - 1, 0), j + p))
    return cur, prev


def _pair_slope(pair, h):
    return jnp.where(pair == 0, jnp.float32(ALIBI_SLOPES[h]), jnp.float32(ALIBI_SLOPES[2 + h]))


def _halves(t):
    return [t[:, h * HEAD_DIM:(h + 1) * HEAD_DIM] for h in range(2)]


def _for_classes(dilation, step):
    if dilation == 1:
        step(0)
    else:
        lax.fori_loop(0, dilation, lambda z, c: (step(z), c)[1], 0)


def attn_fwd(proj, dilation, name):
    S = proj.shape[0]
    blk = ATTN_BLK
    rows = blk * dilation
    cur, prev = _attn_specs(rows)

    def body(q_ref, kp_ref, kc_ref, vp_ref, vc_ref, o_ref, l_ref):
        pair, n = pl.program_id(0), pl.program_id(1)

        def one_class(z):
            sel = pl.ds(z, blk, stride=dilation) if dilation > 1 else pl.ds(0, blk)
            q, kp, kc, vp, vc = (_halves(r[sel, :]) for r in (q_ref, kp_ref, kc_ref, vp_ref, vc_ref))
            res = [_attn_block(q[h], kp[h], kc[h], vp[h], vc[h], n, _pair_slope(pair, h), dilation) for h in range(2)]
            o_ref[sel, :] = jnp.concatenate([r[0] for r in res], axis=1)
            l_ref[sel, :] = jnp.concatenate([r[1] for r in res], axis=1)

        _for_classes(dilation, one_class)

    return pl.pallas_call(
        body, grid=(2, S // rows),
        in_specs=[cur(_QCOL), prev(_QCOL + 2), cur(_QCOL + 2), prev(_QCOL + 4), cur(_QCOL + 4)],
        out_specs=[cur(0), cur(0)], out_shape=[jax.ShapeDtypeStruct((S, D_GROUP), F32)] * 2,
        compiler_params=_cparams(("parallel", "arbitrary")), name=name)(proj, proj, proj, proj, proj)


def attn_bwd(proj, do, dlse, dilation, name):
    S = proj.shape[0]
    blk = ATTN_BLK
    rows = blk * dilation
    cur, prev = _attn_specs(rows)
    full = pl.BlockSpec((S, PAIR), lambda p, n: (0, p))

    def body(q_ref, kp_ref, kc_ref, vp_ref, vc_ref, do_ref, dl_ref, dq_ref, dk_ref, dv_ref):
        pair, n = pl.program_id(0), pl.program_id(1)

        @pl.when(n == 0)
        def _():
            dk_ref[...] = jnp.zeros_like(dk_ref)
            dv_ref[...] = jnp.zeros_like(dv_ref)

        def one_class(z):
            sel = pl.ds(z, blk, stride=dilation) if dilation > 1 else pl.ds(0, blk)
            q, kp, kc, vp, vc, do_v, dl_v = (_halves(r[sel, :]) for r in
                                             (q_ref, kp_ref, kc_ref, vp_ref, vc_ref, do_ref, dl_ref))
            grads = []
            for h in range(2):
                f = lambda q_, kp_, kc_, vp_, vc_, h=h: _attn_block(q_, kp_, kc_, vp_, vc_, n, _pair_slope(pair, h), dilation)
                _, vjp = jax.vjp(f, q[h], kp[h], kc[h], vp[h], vc[h])
                grads.append(vjp((do_v[h], dl_v[h])))
            both = lambda j: jnp.concatenate([grads[0][j], grads[1][j]], axis=1)
            dq_ref[sel, :] = both(0)
            if dilation > 1:
                here = pl.ds(n * rows + z, blk, stride=dilation)
                before = pl.ds(jnp.maximum(n - 1, 0) * rows + z, blk, stride=dilation)
            else:
                here = pl.ds(pl.multiple_of(n * blk, blk), blk)
                before = pl.ds(pl.multiple_of(jnp.maximum(n - 1, 0) * blk, blk), blk)
            dk_ref[here, :] = dk_ref[here, :] + both(2)
            dv_ref[here, :] = dv_ref[here, :] + both(4)
            dk_ref[before, :] = dk_ref[before, :] + both(1)
            dv_ref[before, :] = dv_ref[before, :] + both(3)

        _for_classes(dilation, one_class)

    return pl.pallas_call(
        body, grid=(2, S // rows),
        in_specs=[cur(_QCOL), prev(_QCOL + 2), cur(_QCOL + 2), prev(_QCOL + 4), cur(_QCOL + 4), cur(0), cur(0)],
        out_specs=[cur(0), full, full], out_shape=[jax.ShapeDtypeStruct((S, D_GROUP), F32)] * 3,
        compiler_params=_cparams(("parallel", "arbitrary")), name=name)(proj, proj, proj, proj, proj, do, dlse)


def _head_ones(width, group):
    i = lax.broadcasted_iota(jnp.int32, (width, width), 0) // group
    j = lax.broadcasted_iota(jnp.int32, (width, width), 1) // group
    return (i == j).astype(F32)


def _group_sum(x, group):
    return jnp.dot(x, _head_ones(x.shape[1], group), precision=HI, preferred_element_type=F32)


def _spread(width_in, width_out, rep):
    i = lax.broadcasted_iota(jnp.int32, (width_in, width_out), 0)
    j = lax.broadcasted_iota(jnp.int32, (width_in, width_out), 1) // rep
    return (i == j).astype(F32)


def _hdot(a, b):
    return jnp.dot(a, b, precision=HI, preferred_element_type=F32)


def _sigmoid(x):
    return 1.0 / (1.0 + jnp.exp(-x))


def _softplus(x):
    return jnp.maximum(x, 0.0) + jnp.log(1.0 + jnp.exp(jnp.minimum(x, -x)))


def _silu(x):
    return x * _sigmoid(x)


def rwkv_pre(layer):
    def fn(*args):
        if layer == 0:
            fs, w0, w2p, a0, a2p, g2p, k_k, k_a = args
        else:
            fs, vfirst, w0, w2p, a0, a2p, g2p, k_k, k_a, v0, v2p = args
        r, k, v = fs[:, 0:256], fs[:, 256:512], fs[:, 512:768]
        lora = fs[:, 768:896]
        w_log = -_softplus(-(w0 + _hdot(jnp.tanh(lora), w2p))) - 0.5
        decay = jnp.exp(-jnp.exp(w_log))
        a = _sigmoid(a0 + _hdot(lora, a2p))
        g = _hdot(_sigmoid(lora), g2p)
        if layer > 0:
            v = v + (vfirst - v) * _sigmoid(v0 + _hdot(fs[:, 896:1024], v2p))
        kk = k * k_k
        kk = kk / jnp.maximum(jnp.sqrt(_group_sum(kk * kk, HEAD_DIM)), 1e-12)
        k = k * (1.0 + (a - 1.0) * k_a)
        return r, decay, k, v, kk, a, g
    return fn


def rwkv_post(y, r, k, v, g, lnx_w, lnx_b, r_k):
    mu = _group_sum(y, HEAD_DIM) * (1.0 / HEAD_DIM)
    yc = y - mu
    var = _group_sum(yc * yc, HEAD_DIM) * (1.0 / HEAD_DIM)
    yn = yc * lax.rsqrt(var + RWKV_GN_EPS) * lnx_w + lnx_b
    bonus = _group_sum(r * k * r_k, HEAD_DIM) * v
    return ((yn + bonus) * g,)


def attn_combine(o1, o2, o3, l1, l2, l3):
    m = jnp.maximum(jnp.maximum(l1, l2), l3)
    e1, e2, e3 = jnp.exp(l1 - m), jnp.exp(l2 - m), jnp.exp(l3 - m)
    return ((o1 * e1 + o2 * e2 + o3 * e3) / (e1 + e2 + e3),)


def ssd_pre(xc, dtr, conv_b, dt_bias, a_log):
    xbc = _silu(xc + conv_b)
    xs, bm, cm = xbc[:, 0:256], xbc[:, 256:512], xbc[:, 512:768]
    dt = _softplus(dtr + dt_bias)
    a_neg = -jnp.exp(a_log)
    wide = _spread(128, N_HEADS * SSD_STATE, SSD_STATE)
    w = _hdot(dt, wide) * _hdot(a_neg, wide)
    xdt = xs * _hdot(dt, _spread(128, D_GROUP, HEAD_DIM))
    rr = jnp.concatenate([cm[:, 0:128], cm[:, 0:128], cm[:, 128:256], cm[:, 128:256]], axis=1)
    kk = jnp.concatenate([bm[:, 0:128], bm[:, 0:128], bm[:, 128:256], bm[:, 128:256]], axis=1)
    return rr, w, kk, xdt, xs


def ssd_post(ys, z, xs, d_skip, norm_w):
    y = ys + xs * _hdot(d_skip, _spread(128, D_GROUP, HEAD_DIM))
    y = y * _silu(z)
    half = D_GROUP // 2
    parts = []
    for g in range(2):
        t = y[:, g * half:(g + 1) * half]
        parts.append(t * lax.rsqrt(jnp.mean(t * t, axis=-1, keepdims=True) + RMS_EPS))
    return (jnp.concatenate(parts, axis=1) * norm_w,)


def hgrn_pre(seg, lb):
    q, f, i = seg[:, 0:256], seg[:, 256:512], seg[:, 512:768]
    forget = lb + (1.0 - lb) * _sigmoid(f)
    return _silu(q), 1.0 - forget, jnp.log(forget), i


def hgrn_post(o, seg, norm_w):
    g = seg[:, 768:1024]
    ms = _group_sum(o * o, HEAD_DIM) * (1.0 / HEAD_DIM)
    return (o * lax.rsqrt(ms + RMS_EPS) * norm_w * _silu(g),)


def ln_res(x, y, w, b):
    z = ALPHA * x + y
    mu = jnp.mean(z, axis=-1, keepdims=True)
    zc = z - mu
    var = jnp.mean(zc * zc, axis=-1, keepdims=True)
    return (zc * lax.rsqrt(var + LN_EPS) * w + b,)


def loss_and_grad(y, tgt, name):
    S, D = y.shape
    tile = ROW_TILE

    def body(y_ref, t_ref, l_ref, dy_ref):
        e = y_ref[...] - t_ref[...]
        dy_ref[...] = e * (1.0 / D)

        @pl.when(pl.program_id(0) == 0)
        def _():
            l_ref[...] = jnp.zeros_like(l_ref)

        per_row = 0.5 * jnp.mean(e * e, axis=-1, keepdims=True)
        l_ref[...] += jnp.sum(per_row, axis=0, keepdims=True) * jnp.ones((1, 128), F32)

    return pl.pallas_call(body, grid=(S // tile,), in_specs=[_row_spec(D, tile)] * 2,
                          out_specs=[_par_spec((1, 128)), _row_spec(D, tile)],
                          out_shape=[jax.ShapeDtypeStruct((1, 128), F32), jax.ShapeDtypeStruct((S, D), F32)],
                          compiler_params=_cparams(("arbitrary",)), name=name)(y, tgt)


def add_rows(arrs, name):
    (out,) = tl_fwd(lambda *a: (functools.reduce(lambda p, q: p + q, a),), name, arrs, [], [arrs[0].shape[1]])
    return out


def small_fwd(fn, name, ins, out_shapes):
    n = len(ins)

    def body(*refs):
        outs = fn(*[r[...] for r in refs[:n]])
        for o_ref, o in zip(refs[n:], outs):
            o_ref[...] = o

    return pl.pallas_call(body, out_shape=[jax.ShapeDtypeStruct(s, F32) for s in out_shapes], name=name)(*ins)


def small_bwd(fn, name, ins, cts):
    n, m = len(ins), len(cts)

    def body(*refs):
        _, vjp = jax.vjp(lambda *a: tuple(fn(*a)), *[r[...] for r in refs[:n]])
        grads = vjp(tuple(r[...] for r in refs[n:n + m]))
        for o_ref, g in zip(refs[n + m:], grads):
            o_ref[...] = g

    return pl.pallas_call(body, out_shape=[jax.ShapeDtypeStruct(a.shape, F32) for a in ins], name=name)(*ins, *cts)


def param_prep(lower_bounds, mu0, mu1):
    e = jnp.exp(lower_bounds - jnp.max(lower_bounds, axis=0, keepdims=True))
    sm = e / jnp.sum(e, axis=0, keepdims=True)
    lb0 = sm[0:1] - sm[0:1]
    lb1 = sm[0:1] + sm[1:2] - sm[0:1]
    return lb0, lb1, mu0, 1.0 - mu0, mu1, 1.0 - mu1


def _rows_tile(rows):
    return _pick(rows, (256, 128, 64, 32, 16, 8))


def sum_parts(parts, name):
    P, rows, cols = parts.shape
    tile = _rows_tile(rows)

    def body(p_ref, o_ref):
        acc = p_ref[0]
        for p in range(1, P):
            acc = acc + p_ref[p]
        o_ref[...] = acc

    return pl.pallas_call(body, grid=(rows // tile,), in_specs=[pl.BlockSpec((P, tile, cols), lambda i: (0, i, 0))],
                          out_specs=pl.BlockSpec((tile, cols), lambda i: (i, 0)),
                          out_shape=jax.ShapeDtypeStruct((rows, cols), F32),
                          compiler_params=_cparams(("parallel",)), name=name)(parts)


def pair_sum(own, got, ids, name):
    _, P, rows, cols = own.shape
    tile = _rows_tile(rows)

    def body(ids_ref, own_ref, got_ref, o_ref):
        o_ref[0] = (own_ref[0, 0] + got_ref[0]).astype(BF16)

    grid_spec = pltpu.PrefetchScalarGridSpec(
        num_scalar_prefetch=1, grid=(P, rows // tile),
        in_specs=[pl.BlockSpec((1, 1, tile, cols), lambda s, i, ids: (ids[0], s, i, 0)),
                  pl.BlockSpec((1, tile, cols), lambda s, i, ids: (s, i, 0))],
        out_specs=pl.BlockSpec((1, tile, cols), lambda s, i, ids: (s, i, 0)))
    return pl.pallas_call(body, grid_spec=grid_spec, out_shape=jax.ShapeDtypeStruct((P, rows, cols), BF16),
                          compiler_params=_cparams(("parallel", "parallel")), name=name)(ids, own, got)


def sum_chips(slots, own, got, ids, name):
    P, rows, cols = slots.shape
    tile = _rows_tile(rows)

    def body(ids_ref, s_ref, own_ref, got_ref, o_ref):
        chip = ids_ref[1]
        mine = own_ref[0, 0] + got_ref[0]
        acc = None
        for p in range(P):
            term = jnp.where(chip == p, mine, s_ref[p].astype(F32))
            acc = term if acc is None else acc + term
        o_ref[0] = acc

    grid_spec = pltpu.PrefetchScalarGridSpec(
        num_scalar_prefetch=1, grid=(rows // tile,),
        in_specs=[pl.BlockSpec((P, tile, cols), lambda i, ids: (0, i, 0)),
                  pl.BlockSpec((1, 1, tile, cols), lambda i, ids: (ids[0], ids[1], i, 0)),
                  pl.BlockSpec((1, tile, cols), lambda i, ids: (ids[1], i, 0))],
        out_specs=pl.BlockSpec((1, tile, cols), lambda i, ids: (ids[0], i, 0)))
    return pl.pallas_call(body, grid_spec=grid_spec, out_shape=jax.ShapeDtypeStruct((2, rows, cols), F32),
                          compiler_params=_cparams(("parallel",)), name=name)(ids, slots, own, got)


def adamw(w, g, m, v, name, lane_tiled=False):
    layers, rows, cols = w.shape
    tile = _rows_tile(rows)

    def body(w_ref, g_ref, m_ref, v_ref, d_ref, nm_ref, nv_ref):
        gv = g_ref[...]
        nm = ADAM_B1 * m_ref[...] + (1.0 - ADAM_B1) * gv
        nv = ADAM_B2 * v_ref[...] + (1.0 - ADAM_B2) * jnp.square(gv)
        m_hat = nm / (1.0 - ADAM_B1 ** ADAM_STEP)
        v_hat = nv / (1.0 - ADAM_B2 ** ADAM_STEP)
        d_ref[...] = -ADAM_LR * (m_hat / (jnp.sqrt(v_hat) + ADAM_EPS) + ADAM_WD * w_ref[...])
        nm_ref[...] = nm
        nv_ref[...] = nv

    spec, grid = pl.BlockSpec((None, tile, cols), lambda l, i: (l, i, 0)), (layers, rows // tile)
    if lane_tiled:
        spec, grid = pl.BlockSpec((layers, rows, 256), lambda l, i: (0, 0, i)), (1, cols // 256)
    return pl.pallas_call(body, grid=grid, in_specs=[spec] * 4, out_specs=[spec] * 3,
                          out_shape=[jax.ShapeDtypeStruct((layers, rows, cols), F32)] * 3,
                          compiler_params=_cparams(("parallel", "parallel")), name=name)(w, g, m, v)


MESH = pl.DeviceIdType.MESH
ANY = pl.BlockSpec(memory_space=pl.ANY)


def _flip(v, bit):
    return 1 - v if bit else v


_CHIP_RELATIONS = ((1, 0), (0, 1), (1, 1))


def gather_chips(arrs, small, name):
    n = len(arrs)

    def body(*refs):
        ins, small_in = refs[:n], refs[n]
        outs, small_out = refs[n + 1:2 * n + 1], refs[2 * n + 1]
        send, recv, fsend, frecv, ssend, srecv = refs[2 * n + 2:]
        x, y, c = lax.axis_index("x"), lax.axis_index("y"), lax.axis_index("c")
        me = 2 * x + y
        chips = [(_flip(x, bx), _flip(y, by)) for bx, by in _CHIP_RELATIONS]

        def over_ici(i, r, block_chip):
            return pltpu.make_async_remote_copy(src_ref=ins[i].at[c], dst_ref=outs[i].at[block_chip, c],
                                                send_sem=send.at[i, r], recv_sem=recv.at[i, r],
                                                device_id=(chips[r][0], chips[r][1], c), device_id_type=MESH)

        def to_sibling(i, r, layer):
            blk = outs[i].at[2 * chips[r][0] + chips[r][1], layer]
            return pltpu.make_async_remote_copy(src_ref=blk, dst_ref=blk, send_sem=fsend.at[i, r],
                                                recv_sem=frecv.at[i, r], device_id=(x, y, 1 - c), device_id_type=MESH)

        first = [over_ici(i, r, me) for i in range(n) for r in range(3)]
        smalls = [pltpu.make_async_remote_copy(src_ref=small_in, dst_ref=small_out.at[me], send_sem=ssend.at[r],
                                               recv_sem=srecv.at[r], device_id=(chips[r][0], chips[r][1], c),
                                               device_id_type=MESH) for r in range(3)]
        for cp in first + smalls:
            cp.start()
        passed = []
        for r in range(3):
            for i in range(n):
                over_ici(i, r, 2 * chips[r][0] + chips[r][1]).wait_recv()
                fw = to_sibling(i, r, c)
                fw.start()
                passed.append(fw)
        for r in range(3):
            for i in range(n):
                to_sibling(i, r, 1 - c).wait_recv()
        for cp in first + passed:
            cp.wait_send()
        for cp in smalls:
            cp.wait()

    return pl.pallas_call(
        body, in_specs=[ANY] * (n + 1), out_specs=[ANY] * (n + 1),
        out_shape=[jax.ShapeDtypeStruct((4,) + a.shape, a.dtype) for a in arrs]
        + [jax.ShapeDtypeStruct((4,) + small.shape, small.dtype)],
        scratch_shapes=[pltpu.SemaphoreType.DMA((n, 3)), pltpu.SemaphoreType.DMA((n, 3)), pltpu.SemaphoreType.DMA((n, 3)),
                        pltpu.SemaphoreType.DMA((n, 3)), pltpu.SemaphoreType.DMA((3,)), pltpu.SemaphoreType.DMA((3,))],
        name=name)(*arrs, small)


_RELATIONS = tuple((r >> 2 & 1, r >> 1 & 1, r & 1) for r in range(1, 8))


def gather_devices(arr, name):
    def body(in_ref, out_ref, send, recv, loc):
        x, y, c = lax.axis_index("x"), lax.axis_index("y"), lax.axis_index("c")
        me = 4 * x + 2 * y + c
        lc = pltpu.make_async_copy(in_ref, out_ref.at[me], loc)
        lc.start()
        pending = [lc]
        for r, (bx, by, bc) in enumerate(_RELATIONS):
            cp = pltpu.make_async_remote_copy(src_ref=in_ref, dst_ref=out_ref.at[me], send_sem=send.at[r],
                                              recv_sem=recv.at[r], device_id=(_flip(x, bx), _flip(y, by), _flip(c, bc)),
                                              device_id_type=MESH)
            cp.start()
            pending.append(cp)
        for cp in pending:
            cp.wait()

    return pl.pallas_call(
        body, in_specs=[ANY], out_specs=ANY, out_shape=jax.ShapeDtypeStruct((8,) + arr.shape, arr.dtype),
        scratch_shapes=[pltpu.SemaphoreType.DMA((7,)), pltpu.SemaphoreType.DMA((7,)), pltpu.SemaphoreType.DMA(())],
        name=name)(arr)


def pair_exchange(arrs, name):
    n = len(arrs)

    def body(*refs):
        ins, outs = refs[:n], refs[n:2 * n]
        send, recv = refs[2 * n:]
        x, y, c = lax.axis_index("x"), lax.axis_index("y"), lax.axis_index("c")
        pending = []
        for i in range(n):
            for s in range(4):
                cp = pltpu.make_async_remote_copy(src_ref=ins[i].at[1 - c, s], dst_ref=outs[i].at[s],
                                                  send_sem=send.at[i, s], recv_sem=recv.at[i, s],
                                                  device_id=(x, y, 1 - c), device_id_type=MESH)
                cp.start()
                pending.append(cp)
        for cp in pending:
            cp.wait()

    return pl.pallas_call(
        body, in_specs=[ANY] * n, out_specs=[ANY] * n,
        out_shape=[jax.ShapeDtypeStruct(a.shape[1:], a.dtype) for a in arrs],
        scratch_shapes=[pltpu.SemaphoreType.DMA((n, 4)), pltpu.SemaphoreType.DMA((n, 4))],
        name=name)(*arrs)


def reduce_chips(arrs, name):
    n = len(arrs)

    def body(*refs):
        ins, outs = refs[:n], refs[n:2 * n]
        send, recv, loc = refs[2 * n:]
        x, y, c = lax.axis_index("x"), lax.axis_index("y"), lax.axis_index("c")
        me = 2 * x + y
        pending = []
        for i in range(n):
            for r, (bx, by) in enumerate(_CHIP_RELATIONS):
                px, py = _flip(x, bx), _flip(y, by)
                cp = pltpu.make_async_remote_copy(src_ref=ins[i].at[2 * px + py], dst_ref=outs[i].at[me],
                                                  send_sem=send.at[i, r], recv_sem=recv.at[i, r],
                                                  device_id=(px, py, c), device_id_type=MESH)
                cp.start()
                pending.append(cp)
        for i in range(n):
            lc = pltpu.make_async_copy(ins[i].at[me], outs[i].at[me], loc.at[i])
            lc.start()
            pending.append(lc)
        for cp in pending:
            cp.wait()

    return pl.pallas_call(
        body, in_specs=[ANY] * n, out_specs=[ANY] * n,
        out_shape=[jax.ShapeDtypeStruct(a.shape, a.dtype) for a in arrs],
        scratch_shapes=[pltpu.SemaphoreType.DMA((n, 3)), pltpu.SemaphoreType.DMA((n, 3)), pltpu.SemaphoreType.DMA((n,))],
        name=name)(*arrs)


EXCHANGE_PIECES = 8


def sibling_exchange(arrs, name):
    n = len(arrs)

    def body(*refs):
        bufs = refs[n:2 * n]
        send, recv = refs[2 * n:]
        x, y, c = lax.axis_index("x"), lax.axis_index("y"), lax.axis_index("c")
        pending = []
        for i in range(n):
            rows = bufs[i].shape[1] // EXCHANGE_PIECES
            for j in range(EXCHANGE_PIECES):
                piece = bufs[i].at[c, pl.ds(j * rows, rows)]
                cp = pltpu.make_async_remote_copy(src_ref=piece, dst_ref=piece, send_sem=send.at[i, j],
                                                  recv_sem=recv.at[i, j], device_id=(x, y, 1 - c), device_id_type=MESH)
                cp.start()
                pending.append(cp)
        for i in range(n):
            rows = bufs[i].shape[1] // EXCHANGE_PIECES
            for j in range(EXCHANGE_PIECES):
                landed = bufs[i].at[1 - c, pl.ds(j * rows, rows)]
                pltpu.make_async_remote_copy(src_ref=landed, dst_ref=landed, send_sem=send.at[i, j], recv_sem=recv.at[i, j],
                                             device_id=(x, y, 1 - c), device_id_type=MESH).wait_recv()
        for cp in pending:
            cp.wait_send()

    return pl.pallas_call(
        body, in_specs=[ANY] * n, out_specs=[ANY] * n,
        out_shape=[jax.ShapeDtypeStruct(a.shape, a.dtype) for a in arrs], input_output_aliases={i: i for i in range(n)},
        scratch_shapes=[pltpu.SemaphoreType.DMA((n, EXCHANGE_PIECES)), pltpu.SemaphoreType.DMA((n, EXCHANGE_PIECES))],
        name=name)(*arrs)


def rwkv_fwd(l, seg, taps, pars, vfirst):
    fs = fir_fwd(seg, taps, f"rwkv_shift_fwd{l}")
    rows = [fs] + ([vfirst] if l else [])
    R, W, K, V, KK, A, G = tl_fwd(rwkv_pre(l), f"rwkv_pre_fwd{l}", rows, pars["pre"],
                                  [HM64, HM64, HM64, D_GROUP, HM64, HM64, D_GROUP])
    Y, sall = scan_fwd(R, W, K, V, KK, A, f"rwkv_scan_fwd{l}")
    (out,) = tl_fwd(rwkv_post, f"rwkv_post_fwd{l}", [Y, R, K, V, G], pars["post"], [D_GROUP])
    return out, V, (seg, taps, rows, R, W, K, V, KK, A, G, Y, sall)


def rwkv_bwd(l, saved, pars, dout, dv_extra):
    seg, taps, rows, R, W, K, V, KK, A, G, Y, sall = saved
    (dY, dR1, dK1, dV1, dG), dpost = tl_bwd(rwkv_post, f"rwkv_post_bwd{l}", [Y, R, K, V, G], pars["post"], [[dout]])
    (dR2, dW, dK2, dKK, dA), dV2 = scan_bwd(R, W, K, V, KK, A, sall, dY, f"rwkv_scan_bwd{l}")
    cts = [[dR1, dR2], [dW], [dK1, dK2], [dV1, dV2] + dv_extra, [dKK], [dA], [dG]]
    drows, dpre = tl_bwd(rwkv_pre(l), f"rwkv_pre_bwd{l}", rows, pars["pre"], cts)
    dseg, dtaps, _ = fir_bwd(seg, taps, [drows[0]], f"rwkv_shift_bwd{l}")
    return dseg, (drows[1] if l else None), dtaps, dpre, dpost


def attn_mix_fwd(l, proj):
    os_, ls_ = [], []
    for b, (_, d) in enumerate(DILATED_BRANCHES):
        o, lse = attn_fwd(proj, d, f"attn_fwd{l}_{b}")
        os_.append(o)
        ls_.append(lse)
    (out,) = tl_fwd(attn_combine, f"attn_combine_fwd{l}", os_ + ls_, [], [D_GROUP])
    return out, (proj, os_, ls_)


def attn_mix_bwd(l, saved, dout):
    proj, os_, ls_ = saved
    drows, _ = tl_bwd(attn_combine, f"attn_combine_bwd{l}", os_ + ls_, [], [[dout]])
    grads = [attn_bwd(proj, drows[b], drows[3 + b], d, f"attn_bwd{l}_{b}") for b, (_, d) in enumerate(DILATED_BRANCHES)]
    return tuple(add_rows([g[j] for g in grads], f"attn_d{'qkv'[j]}{l}") for j in range(3))


def ssd_fwd(l, z, xbc, dtr, pars):
    xc = fir_fwd(xbc, pars["taps"], f"ssd_conv_fwd{l}")
    rr, w, kk, xdt, xs = tl_fwd(ssd_pre, f"ssd_pre_fwd{l}", [xc, dtr], pars["pre"], [HM128, HM128, HM128, HM64, D_GROUP])
    blocks = [rr, kk, w, xdt]
    ys, states = chunk_fwd(ssd_chunk, f"ssd_scan_fwd{l}", blocks, (HEAD_DIM, SSD_STATE), HEAD_DIM, SSD_GROUP)
    (out,) = tl_fwd(ssd_post, f"ssd_post_fwd{l}", [ys, z, xs], pars["post"], [D_GROUP])
    return out, (z, xbc, dtr, xc, blocks, states, xs, ys)


def ssd_bwd(l, saved, pars, dout):
    z, xbc, dtr, xc, blocks, states, xs, ys = saved
    (dys, dz, dxs), dpost = tl_bwd(ssd_post, f"ssd_post_bwd{l}", [ys, z, xs], pars["post"], [[dout]])
    drr, dkk, dw, dxdt = chunk_bwd(ssd_chunk, f"ssd_scan_bwd{l}", blocks, states, dys, SSD_GROUP)
    (dxc, ddtr), dpre = tl_bwd(ssd_pre, f"ssd_pre_bwd{l}", [xc, dtr], pars["pre"], [[drr], [dw], [dkk], [dxdt], [dxs]])
    dxbc, dtaps, _ = fir_bwd(xbc, pars["taps"], [dxc], f"ssd_conv_bwd{l}")
    return dz, dxbc, ddtr, dtaps, dpre, dpost


def hgrn_fwd(l, seg, pars):
    blocks = tl_fwd(hgrn_pre, f"hgrn_pre_fwd{l}", [seg], pars["pre"], [HM64] * 4)
    o, states = chunk_fwd(hgrn_chunk, f"hgrn_scan_fwd{l}", blocks, (HEAD_DIM, HEAD_DIM), HEAD_DIM, HGRN_GROUP)
    (out,) = tl_fwd(hgrn_post, f"hgrn_post_fwd{l}", [o, seg], pars["post"], [D_GROUP])
    return out, (seg, blocks, states, o)


def hgrn_bwd(l, saved, pars, dout):
    seg, blocks, states, o = saved
    (do, dseg1), dpost = tl_bwd(hgrn_post, f"hgrn_post_bwd{l}", [o, seg], pars["post"], [[dout]])
    dq, dkk, dlf, di = chunk_bwd(hgrn_chunk, f"hgrn_scan_bwd{l}", blocks, states, do, HGRN_GROUP)
    (dseg2,), dpre = tl_bwd(hgrn_pre, f"hgrn_pre_bwd{l}", [seg], pars["pre"], [[dq], [dkk], [dlf], [di]])
    return add_rows([dseg1, dseg2], f"hgrn_dseg{l}"), dpre, dpost


def layer_fwd(l, x, wts, pars, vfirst):
    proj = matmul(x, wts["in"], "nn", f"proj_fwd{l}")
    seg_h, seg_r = Cols(proj, SEG_HGRN, 1024), Cols(proj, SEG_RWKV, 1024)
    z, xbc, dtr = Cols(proj, SEG_Z, D_GROUP), Cols(proj, SEG_XBC, SSD_XBC), Cols(proj, SEG_DT, 128)
    ya, v_rwkv, sa = rwkv_fwd(l, seg_r, pars["rwkv"]["taps"], pars["rwkv"], vfirst)
    yb, sb = attn_mix_fwd(l, proj)
    yc, sc = ssd_fwd(l, z, xbc, dtr, pars["ssd"])
    yd, sd = hgrn_fwd(l, seg_h, pars["hgrn"])
    mix = [ya, yb, yc, yd]
    mo = matmul(mix, wts["out"], "nn", f"out_fwd{l}")
    (x1,) = tl_fwd(ln_res, f"ln1_fwd{l}", [x, mo], pars["ln1"], [D_MODEL])
    u = matmul(x1, wts["up"], "nn", f"up_fwd{l}")
    dn = matmul(u, wts["down"], "nn", f"down_fwd{l}", a_relu2=True)
    (x2,) = tl_fwd(ln_res, f"ln2_fwd{l}", [x1, dn], pars["ln2"], [D_MODEL])
    return x2, v_rwkv, (x, sa, sb, sc, sd, mix, mo, x1, u, dn)


def layer_bwd(l, saved, wts, pars, dx2, dv_extra, stacks):
    into = lambda key: {"into": (DEPTH, l), "stacked": stacks.get(key)}
    x, sa, sb, sc, sd, mix, mo, x1, u, dn = saved
    S = x.shape[0]
    g = {}
    (dx1a, ddn), g["ln2"] = tl_bwd(ln_res, f"ln2_bwd{l}", [x1, dn], pars["ln2"], [[dx2]])
    g["down"] = matmul(u, ddn, "tn", f"down_dw{l}", a_relu2=True, **into("down"))
    du = matmul(ddn, wts["down"], "nt", f"down_dx{l}", relu2_grad_of=u)
    g["up"] = matmul(x1, du, "tn", f"up_dw{l}", out_col_shards=True, **into("up"))
    dx1 = matmul(du, wts["up"], "nt", f"up_dx{l}", add=dx1a)
    (dxa, dmo), g["ln1"] = tl_bwd(ln_res, f"ln1_bwd{l}", [x, mo], pars["ln1"], [[dx1]])
    g["out"] = matmul(mix, dmo, "tn", f"out_dw{l}", **into("out"))
    dmix = matmul(dmo, wts["out"], "nt", f"out_dx{l}")
    dya, dyb, dyc, dyd = (Cols(dmix, j * D_GROUP, D_GROUP) for j in range(4))
    dseg_r, dvfirst, g["rwkv_taps"], g["rwkv_pre"], g["rwkv_post"] = rwkv_bwd(l, sa, pars["rwkv"], dya, dv_extra)
    dq, dk, dv = attn_mix_bwd(l, sb, dyb)
    dz, dxbc, ddtr, g["ssd_taps"], g["ssd_pre"], g["ssd_post"] = ssd_bwd(l, sc, pars["ssd"], dyc)
    dseg_h, g["hgrn_pre"], g["hgrn_post"] = hgrn_bwd(l, sd, pars["hgrn"], dyd)
    dproj = jnp.concatenate([dseg_h, dseg_r, dq, dk, dv, dz, dxbc, ddtr, jnp.zeros((S, PROJ_W - SEG_DT - 128), F32)], axis=1)
    g["in"] = matmul(x, dproj, "tn", f"proj_dw{l}")
    dx = matmul(dproj, wts["in"], "nt", f"proj_dx{l}", add=dxa)
    return dx, dvfirst, g


SMALL = ("lower_bounds", "w_in_vres", "mu_shift", "mu_vres", "rwkv_w0", "rwkv_w2", "rwkv_a0", "rwkv_a2", "rwkv_g2",
         "rwkv_k_k", "rwkv_k_a", "rwkv_r_k", "rwkv_lnx_w", "rwkv_lnx_b", "rwkv_v0", "rwkv_v2", "ssd_conv_w",
         "ssd_conv_b", "ssd_dt_bias", "ssd_A_log", "ssd_D", "ssd_norm_w", "hgrn_norm_w", "ln1_w", "ln1_b", "ln2_w", "ln2_b")
BIG = ("w_in", "w_out", "w_up", "w_down")
SMALL_SHARDED = {"w_in_vres": 1, "rwkv_w2": 2, "rwkv_a2": 2, "rwkv_g2": 2, "rwkv_v2": 2, "ssd_conv_w": 2}
WEIGHTS = ("lower_bounds", "w_in", "w_in_vres", "mu_shift", "mu_vres", "rwkv_w0", "rwkv_w2", "rwkv_a0", "rwkv_a2",
           "rwkv_g2", "rwkv_k_k", "rwkv_k_a", "rwkv_r_k", "rwkv_lnx_w", "rwkv_lnx_b", "rwkv_v0", "rwkv_v2",
           "ssd_conv_w", "ssd_conv_b", "ssd_dt_bias", "ssd_A_log", "ssd_D", "ssd_norm_w", "hgrn_norm_w", "w_out",
           "ln1_w", "ln1_b", "w_up", "w_down", "ln2_w", "ln2_b")


def _row(v, width=None):
    v = v.reshape(1, -1).astype(F32)
    if width is not None and v.shape[1] < width:
        v = jnp.pad(v, ((0, 0), (0, width - v.shape[1])))
    return v


def _rows_at(m, rows, at):
    return jnp.pad(m.astype(F32), ((at, rows - at - m.shape[0]), (0, 0)))


SHARD_COLS = IN_COLS // 4


def _shard_runs(start, width):
    runs, pos = [], start
    while pos < start + width:
        s = pos // SHARD_COLS
        end = min(start + width, (s + 1) * SHARD_COLS)
        runs.append((s, pos - s * SHARD_COLS, end - s * SHARD_COLS))
        pos = end
    return runs


def _pad_w_in(shards, vres):
    rows = shards.shape[1]
    out, pos = [], 0
    for start, width, at in sorted(_PIECES, key=lambda p: p[2]):
        if at > pos:
            out.append(jnp.zeros((rows, at - pos), shards.dtype))
        out += [shards[s, :, lo:hi] for s, lo, hi in _shard_runs(start, width)]
        pos = at + width
        if at == SEG_RWKV and vres is not None:
            out.append(vres.astype(shards.dtype))
            pos += vres.shape[1]
    out.append(jnp.zeros((rows, PROJ_W - pos), shards.dtype))
    return jnp.concatenate(out, axis=1)


def _unpad_w_in(g):
    shards = [[] for _ in range(4)]
    for start, width, at in _PIECES:
        for s, lo, hi in _shard_runs(start, width):
            first = at + s * SHARD_COLS + lo - start
            shards[s].append(g[:, first:first + hi - lo])
    return jnp.stack([jnp.concatenate(p, axis=1) for p in shards])


def layer_params(l, sp, prep):
    lb, mu, om = prep[l], prep[2 + 2 * l], prep[3 + 2 * l]
    pre = [_row(sp["rwkv_w0"][l]), _rows_at(sp["rwkv_w2"][l], 128, 0), _row(sp["rwkv_a0"][l]),
           _rows_at(sp["rwkv_a2"][l], 128, 32), _rows_at(sp["rwkv_g2"][l], 128, 64),
           _row(sp["rwkv_k_k"][l]), _row(sp["rwkv_k_a"][l])]
    if l:
        pre += [_row(sp["rwkv_v0"][l - 1]), _rows_at(sp["rwkv_v2"][l - 1], 128, 0)]
    return {
        "rwkv": {"taps": jnp.concatenate([mu, om], axis=0), "pre": pre,
                 "post": [_row(sp["rwkv_lnx_w"][l]), _row(sp["rwkv_lnx_b"][l]), _row(sp["rwkv_r_k"][l])]},
        "ssd": {"taps": sp["ssd_conv_w"][l].astype(F32),
                "pre": [_row(sp["ssd_conv_b"][l]), _row(sp["ssd_dt_bias"][l], 128), _row(sp["ssd_A_log"][l], 128)],
                "post": [_row(sp["ssd_D"][l], 128), _row(sp["ssd_norm_w"][l])]},
        "hgrn": {"pre": [lb], "post": [_row(sp["hgrn_norm_w"][l])]},
        "ln1": [_row(sp["ln1_w"][l]), _row(sp["ln1_b"][l])],
        "ln2": [_row(sp["ln2_w"][l]), _row(sp["ln2_b"][l])],
    }


def _mu_full(sp, l):
    parts = [sp["mu_shift"][l].reshape(1, -1)]
    if l:
        parts.append(sp["mu_vres"][l - 1].reshape(1, -1))
    return _row(jnp.concatenate(parts, axis=1), 1024)


def local_step(x, target, big, sp):
    prep_in = [sp["lower_bounds"].astype(F32), _mu_full(sp, 0), _mu_full(sp, 1)]
    prep = small_fwd(param_prep, "param_prep_fwd", prep_in,
                     [(1, D_GROUP), (1, D_GROUP), (1, 1024), (1, 1024), (1, 1024), (1, 1024)])
    pars, wts = [], []
    for l in range(DEPTH):
        pars.append(layer_params(l, sp, prep))
        vres = sp["w_in_vres"][l - 1].astype(BF16) if l else None
        wts.append({"in": _pad_w_in(big["w_in"][:, l], vres), "out": big["w_out"][l],
                    "up": ShardedWeight(big["w_up"], l, 1), "down": ShardedWeight(big["w_down"], l, 0)})
    h, vfirst, saved = x, None, []
    for l in range(DEPTH):
        h, v_l, sv = layer_fwd(l, h, wts[l], pars[l], vfirst)
        vfirst = v_l if l == 0 else vfirst
        saved.append(sv)
    loss_row, dh = loss_and_grad(h, target, "loss")
    grads, dv_extra = [None] * DEPTH, []
    for l in reversed(range(DEPTH)):
        stacks = {} if l == DEPTH - 1 else {key: grads[l + 1][key] for key in ("down", "up", "out")}
        dh, dvfirst, grads[l] = layer_bwd(l, saved[l], wts[l], pars[l], dh, dv_extra, stacks)
        dv_extra = [dvfirst] if l else []
    cts = [grads[0]["hgrn_pre"][0], grads[1]["hgrn_pre"][0]]
    for l in range(DEPTH):
        cts += [grads[l]["rwkv_taps"][0:1], grads[l]["rwkv_taps"][1:2]]
    d_lower, d_mu0, d_mu1 = small_bwd(param_prep, "param_prep_bwd", prep_in, cts)
    d_mu = [d_mu0, d_mu1]
    gb = {"w_in": jnp.stack([_unpad_w_in(grads[l]["in"]) for l in range(DEPTH)]), "w_out": grads[0]["out"],
          "w_up": grads[0]["up"], "w_down": grads[0]["down"]}
    st = lambda f: jnp.stack([f(l) for l in range(DEPTH)])
    g1 = grads[1]
    gs = {
        "lower_bounds": d_lower,
        "w_in_vres": g1["in"][None, :, VRES_COL:VRES_COL + 32],
        "mu_shift": st(lambda l: d_mu[l][0, :896]),
        "mu_vres": d_mu[1][:, 896:928],
        "rwkv_w0": st(lambda l: grads[l]["rwkv_pre"][0][0]),
        "rwkv_w2": st(lambda l: grads[l]["rwkv_pre"][1][0:32]),
        "rwkv_a0": st(lambda l: grads[l]["rwkv_pre"][2][0]),
        "rwkv_a2": st(lambda l: grads[l]["rwkv_pre"][3][32:64]),
        "rwkv_g2": st(lambda l: grads[l]["rwkv_pre"][4][64:128]),
        "rwkv_k_k": st(lambda l: grads[l]["rwkv_pre"][5][0]),
        "rwkv_k_a": st(lambda l: grads[l]["rwkv_pre"][6][0]),
        "rwkv_r_k": st(lambda l: grads[l]["rwkv_post"][2].reshape(N_HEADS, HEAD_DIM)),
        "rwkv_lnx_w": st(lambda l: grads[l]["rwkv_post"][0][0]),
        "rwkv_lnx_b": st(lambda l: grads[l]["rwkv_post"][1][0]),
        "rwkv_v0": g1["rwkv_pre"][7],
        "rwkv_v2": g1["rwkv_pre"][8][None, 0:32],
        "ssd_conv_w": st(lambda l: grads[l]["ssd_taps"]),
        "ssd_conv_b": st(lambda l: grads[l]["ssd_pre"][0][0]),
        "ssd_dt_bias": st(lambda l: grads[l]["ssd_pre"][1][0, :N_HEADS]),
        "ssd_A_log": st(lambda l: grads[l]["ssd_pre"][2][0, :N_HEADS]),
        "ssd_D": st(lambda l: grads[l]["ssd_post"][0][0, :N_HEADS]),
        "ssd_norm_w": st(lambda l: grads[l]["ssd_post"][1][0]),
        "hgrn_norm_w": st(lambda l: grads[l]["hgrn_post"][0][0]),
        "ln1_w": st(lambda l: grads[l]["ln1"][0][0]),
        "ln1_b": st(lambda l: grads[l]["ln1"][1][0]),
        "ln2_w": st(lambda l: grads[l]["ln2"][0][0]),
        "ln2_b": st(lambda l: grads[l]["ln2"][1][0]),
    }
    return loss_row, dh, gb, gs


def _pack(vecs):
    parts, meta, row = [], [], 0
    for v in vecs:
        rows = -(-v.size // 1024) * 8
        flat = v.reshape(-1).astype(F32)
        parts.append(jnp.pad(flat, (0, rows * 128 - v.size)).reshape(rows, 128))
        meta.append((row, v.shape))
        row += rows
    return jnp.concatenate(parts, axis=0), meta


def _unpack(packed, meta):
    out = []
    for row, shape in meta:
        size = math.prod(shape)
        rows = -(-size // 1024) * 8
        out.append(packed[row:row + rows].reshape(-1)[:size].reshape(shape))
    return out


def _from_chips(name, g):
    if name != "w_out":
        return g
    return jnp.transpose(g, (1, 0, 2, 3)).reshape(g.shape[1], 4 * g.shape[2], g.shape[3])


INPUT_NAMES = ("x",) + WEIGHTS + ("loss_target",) + tuple("m_" + n for n in WEIGHTS) + tuple("v_" + n for n in WEIGHTS)


def _step(*args):
    a = dict(zip(INPUT_NAMES, args, strict=True))
    chip = 2 * lax.axis_index("x") + lax.axis_index("y")

    sharded_names = list(SMALL_SHARDED)
    small_pack, small_meta = _pack([a[n] for n in sharded_names])
    own = [a[n].astype(BF16) for n in BIG]
    gathered = gather_chips(own, small_pack, "gather_weights")
    here = lambda full, mine: lax.dynamic_update_slice(full, mine[None], (chip,) + (0,) * mine.ndim)
    big = {n: _from_chips(n, here(g, o)) for n, g, o in zip(BIG, gathered, own)}
    sp = {n: a[n] for n in SMALL if n not in SMALL_SHARDED}
    small_all = here(gathered[-1], small_pack)
    per_chip = [_unpack(small_all[s], small_meta) for s in range(4)]
    for j, n in enumerate(sharded_names):
        sp[n] = jnp.concatenate([per_chip[s][j] for s in range(4)], axis=SMALL_SHARDED[n])

    loss_row, gx, gb, gs = local_step(a["x"][0], a["loss_target"][0], big, sp)

    partials = [gb[n].reshape((DEPTH, 4) + a[n].shape[1:]) for n in BIG]
    got = pair_exchange(partials, "pair_exchange")
    ids = jnp.stack([lax.axis_index("c"), chip]).astype(jnp.int32)
    chip_sums = [pair_sum(p, q, ids, f"pair_sum_{n}") for n, p, q in zip(BIG, partials, got)]
    slots = reduce_chips(chip_sums, "reduce_big")
    mine = [sum_chips(sl, p, q, ids, f"sum_{n}") for n, sl, p, q in zip(BIG, slots, partials, got)]
    summed = sibling_exchange(mine, "exchange_big")
    out_g, out_d, out_m, out_v = {}, {}, {}, {}
    for n, g in zip(BIG, summed):
        if n == "w_in":
            to_t, from_t = (lambda t: jnp.transpose(t, (2, 0, 1))), (lambda t: jnp.transpose(t, (1, 2, 0)))
            g_t = to_t(g)
            res = adamw(to_t(a[n]), g_t, to_t(a["m_" + n]), to_t(a["v_" + n]), f"adamw_{n}", lane_tiled=True)
            out_g[n], (out_d[n], out_m[n], out_v[n]) = from_t(g_t), [from_t(r) for r in res]
            continue
        out_g[n] = g
        out_d[n], out_m[n], out_v[n] = adamw(a[n], g, a["m_" + n], a["v_" + n], f"adamw_{n}")

    vec, meta = _pack([loss_row] + [gs[n] for n in SMALL])
    total = sum_parts(gather_devices(vec, "gather_small"), "sum_small")
    parts = _unpack(total, meta)
    loss = parts[0][0, 0]
    g_small = {}
    for n, g in zip(SMALL, parts[1:]):
        if n in SMALL_SHARDED:
            ax = SMALL_SHARDED[n]
            size = a[n].shape[ax]
            g = lax.dynamic_slice_in_dim(g, chip * size, size, axis=ax)
        g_small[n] = g
    pw, pmeta = _pack([a[n] for n in SMALL])
    pg, _ = _pack([g_small[n] for n in SMALL])
    pm, _ = _pack([a["m_" + n] for n in SMALL])
    pv, _ = _pack([a["v_" + n] for n in SMALL])
    d, nm, nv = adamw(pw[None], pg[None], pm[None], pv[None], "adamw_small")
    for n, dd, mm, vv in zip(SMALL, _unpack(d[0], pmeta), _unpack(nm[0], pmeta), _unpack(nv[0], pmeta)):
        out_g[n], out_d[n], out_m[n], out_v[n] = g_small[n], dd, mm, vv

    return (loss, gx[None], *[out_g[n] for n in WEIGHTS], *[out_d[n] for n in WEIGHTS],
            *[out_m[n] for n in WEIGHTS], *[out_v[n] for n in WEIGHTS])


def kernel(x, lower_bounds, w_in, w_in_vres, mu_shift, mu_vres, rwkv_w0, rwkv_w2, rwkv_a0, rwkv_a2, rwkv_g2, rwkv_k_k, rwkv_k_a, rwkv_r_k, rwkv_lnx_w, rwkv_lnx_b, rwkv_v0, rwkv_v2, ssd_conv_w, ssd_conv_b, ssd_dt_bias, ssd_A_log, ssd_D, ssd_norm_w, hgrn_norm_w, w_out, ln1_w, ln1_b, w_up, w_down, ln2_w, ln2_b, loss_target, m_lower_bounds, m_w_in, m_w_in_vres, m_mu_shift, m_mu_vres, m_rwkv_w0, m_rwkv_w2, m_rwkv_a0, m_rwkv_a2, m_rwkv_g2, m_rwkv_k_k, m_rwkv_k_a, m_rwkv_r_k, m_rwkv_lnx_w, m_rwkv_lnx_b, m_rwkv_v0, m_rwkv_v2, m_ssd_conv_w, m_ssd_conv_b, m_ssd_dt_bias, m_ssd_A_log, m_ssd_D, m_ssd_norm_w, m_hgrn_norm_w, m_w_out, m_ln1_w, m_ln1_b, m_w_up, m_w_down, m_ln2_w, m_ln2_b, v_lower_bounds, v_w_in, v_w_in_vres, v_mu_shift, v_mu_vres, v_rwkv_w0, v_rwkv_w2, v_rwkv_a0, v_rwkv_a2, v_rwkv_g2, v_rwkv_k_k, v_rwkv_k_a, v_rwkv_r_k, v_rwkv_lnx_w, v_rwkv_lnx_b, v_rwkv_v0, v_rwkv_v2, v_ssd_conv_w, v_ssd_conv_b, v_ssd_dt_bias, v_ssd_A_log, v_ssd_D, v_ssd_norm_w, v_hgrn_norm_w, v_w_out, v_ln1_w, v_ln1_b, v_w_up, v_w_down, v_ln2_w, v_ln2_b):
    return _step(x, lower_bounds, w_in, w_in_vres, mu_shift, mu_vres, rwkv_w0, rwkv_w2, rwkv_a0, rwkv_a2, rwkv_g2, rwkv_k_k, rwkv_k_a, rwkv_r_k, rwkv_lnx_w, rwkv_lnx_b, rwkv_v0, rwkv_v2, ssd_conv_w, ssd_conv_b, ssd_dt_bias, ssd_A_log, ssd_D, ssd_norm_w, hgrn_norm_w, w_out, ln1_w, ln1_b, w_up, w_down, ln2_w, ln2_b, loss_target, m_lower_bounds, m_w_in, m_w_in_vres, m_mu_shift, m_mu_vres, m_rwkv_w0, m_rwkv_w2, m_rwkv_a0, m_rwkv_a2, m_rwkv_g2, m_rwkv_k_k, m_rwkv_k_a, m_rwkv_r_k, m_rwkv_lnx_w, m_rwkv_lnx_b, m_rwkv_v0, m_rwkv_v2, m_ssd_conv_w, m_ssd_conv_b, m_ssd_dt_bias, m_ssd_A_log, m_ssd_D, m_ssd_norm_w, m_hgrn_norm_w, m_w_out, m_ln1_w, m_ln1_b, m_w_up, m_w_down, m_ln2_w, m_ln2_b, v_lower_bounds, v_w_in, v_w_in_vres, v_mu_shift, v_mu_vres, v_rwkv_w0, v_rwkv_w2, v_rwkv_a0, v_rwkv_a2, v_rwkv_g2, v_rwkv_k_k, v_rwkv_k_a, v_rwkv_r_k, v_rwkv_lnx_w, v_rwkv_lnx_b, v_rwkv_v0, v_rwkv_v2, v_ssd_conv_w, v_ssd_conv_b, v_ssd_dt_bias, v_ssd_A_log, v_ssd_D, v_ssd_norm_w, v_hgrn_norm_w, v_w_out, v_ln1_w, v_ln1_b, v_w_up, v_w_down, v_ln2_w, v_ln2_b)
```

```python
import functools
import math

import jax
import jax.numpy as jnp
from jax import lax
from jax.experimental import pallas as pl
from jax.experimental.pallas import tpu as pltpu

F32 = jnp.float32
BF16 = jnp.bfloat16
HI = lax.Precision.HIGHEST

DEPTH = 2
D_MODEL = 1024
D_GROUP = 256
HEAD_DIM = 64
N_HEADS = 4
SSD_STATE = 128
SSD_XBC = 768
SSD_CONV = 4
D_FF = 4096
ALPHA = (2.0 * DEPTH) ** 0.25
LN_EPS = 1e-5
RMS_EPS = 1e-5
RWKV_GN_EPS = HEAD_DIM * 1e-5
DILATED_BRANCHES = ((128, 1), (512, 4), (2048, 16))
ALIBI_SLOPES = tuple(2.0 ** (-8.0 * (h + 1) / N_HEADS) for h in range(N_HEADS))
ATTN_BLK = 128

ADAM_LR, ADAM_B1, ADAM_B2, ADAM_EPS, ADAM_WD, ADAM_STEP = 0.001, 0.9, 0.999, 1e-08, 0.01, 10

IN_COLS = 3716
PROJ_W = 4096
SEG_HGRN, SEG_RWKV, SEG_Q, SEG_Z, SEG_XBC, SEG_DT = 0, 1024, 2048, 2816, 3072, 3840
_PIECES = ((0, 896, SEG_RWKV), (896, 768, SEG_Q), (1664, 256, SEG_Z), (1920, 768, SEG_XBC),
           (2688, 4, SEG_DT), (2692, 1024, SEG_HGRN))
VRES_COL = SEG_RWKV + 896

HM64 = (N_HEADS, HEAD_DIM)
HM128 = (N_HEADS, SSD_STATE)
ROW_TILE = 256
SCAN_CHUNK = 128
VMEM_LIMIT = 48 * 1024 * 1024


def _cparams(sem=None):
    if sem is None:
        return pltpu.CompilerParams(vmem_limit_bytes=VMEM_LIMIT)
    return pltpu.CompilerParams(dimension_semantics=sem, vmem_limit_bytes=VMEM_LIMIT)


def _pick(n, pref):
    for t in pref:
        if n % t == 0:
            return t
    return n


def _relu2(u):
    r = jnp.maximum(u, 0.0)
    return r * r


class ShardedWeight:
    def __init__(self, arr, layer, axis):
        rows, cols = arr.shape[2:]
        assert (rows, cols)[axis] == 1024
        self.arr, self.layer, self.axis = arr, layer, axis
        self.shape = (4 * rows, cols) if axis == 0 else (rows, 4 * cols)

    def spec(self, mode, tn, tk):
        l, along_rows = self.layer, self.axis == 0
        if mode == "nn":
            assert (tk if along_rows else tn) == 1024
            index = (lambda i, j, k: (k, l, 0, j)) if along_rows else (lambda i, j, k: (j, l, k, 0))
            return pl.BlockSpec((None, None, tk, tn), index)
        assert mode == "nt" and (tn if along_rows else tk) == 1024
        index = (lambda i, j, k: (j, l, 0, k)) if along_rows else (lambda i, j, k: (k, l, j, 0))
        return pl.BlockSpec((None, None, tn, tk), index)


def matmul(a, b, mode, name, add=None, a_relu2=False, relu2_grad_of=None, out_col_shards=False, into=None, stacked=None):
    pieces = list(a) if isinstance(a, (list, tuple)) else [a]
    a_rows, a_cols = pieces[0].shape[0], sum(p.shape[1] for p in pieces)
    if mode == "nn":
        (M, K), N = (a_rows, a_cols), b.shape[1]
    elif mode == "nt":
        (M, K), N = (a_rows, a_cols), b.shape[0]
    else:
        (K, M), N = (a_rows, a_cols), b.shape[1]
    tm, tn, tk = _pick(M, (1024, 512, 256, 128)), _pick(N, (1024, 512, 256, 128)), _pick(K, (1024, 512, 256, 128))
    nk = K // tk
    dims = {"nn": (((1,), (0,)), ((), ())), "nt": (((1,), (1,)), ((), ())), "tn": (((0,), (0,)), ((), ()))}[mode]
    extras = [e for e in (add, relu2_grad_of) if e is not None]
    npieces = len(pieces)

    def body(*refs):
        b_ref = refs[npieces]
        o_ref = refs[-1]
        rest = list(refs[npieces + 1:-1])
        add_ref = rest.pop(0) if add is not None else None
        u_ref = rest.pop(0) if relu2_grad_of is not None else None
        k = pl.program_id(2)
        av = refs[0][...] if npieces == 1 else jnp.concatenate([r[...] for r in refs[:npieces]], axis=1)
        if a_relu2:
            av = _relu2(av)
        d = lax.dot_general(av.astype(BF16), b_ref[...].astype(BF16), dims, preferred_element_type=F32)

        @pl.when(k == 0)
        def _():
            o_ref[...] = d if add_ref is None else d + add_ref[...]

        if nk > 1:
            @pl.when(k > 0)
            def _():
                o_ref[...] += d

        if u_ref is not None:
            @pl.when(k == nk - 1)
            def _():
                o_ref[...] = o_ref[...] * (2.0 * jnp.maximum(u_ref[...], 0.0))

    if npieces == 1:
        a_specs = [pl.BlockSpec((tk, tm), lambda i, j, k: (k, i)) if mode == "tn" else pl.BlockSpec((tm, tk), lambda i, j, k: (i, k))]
    else:
        assert a_cols == (tm if mode == "tn" else tk)
        rows_of = (lambda i, j, k: (k, 0)) if mode == "tn" else (lambda i, j, k: (i, 0))
        a_specs = [pl.BlockSpec((tk if mode == "tn" else tm, p.shape[1]), rows_of) for p in pieces]
    if isinstance(b, ShardedWeight):
        b_spec = b.spec(mode, tn, tk)
    else:
        b_spec = pl.BlockSpec((tn, tk), lambda i, j, k: (j, k)) if mode == "nt" else pl.BlockSpec((tk, tn), lambda i, j, k: (k, j))
    o_spec = pl.BlockSpec((tm, tn), lambda i, j, k: (i, j))
    ins, specs = pieces + [_arr(b)] + extras, a_specs + [b_spec] + [o_spec] * len(extras)
    out_shape, out_spec = (M, N), o_spec
    if out_col_shards:
        assert tn == N // 4
        out_shape, out_spec = (4, M, tn), pl.BlockSpec((None, tm, tn), lambda i, j, k: (j, i, 0))
    aliases = {}
    if into is not None:
        layers, layer = into
        block, index = out_spec.block_shape, out_spec.index_map
        out_spec = pl.BlockSpec((None,) + tuple(block), lambda i, j, k: (layer,) + tuple(index(i, j, k)))
        out_shape = (layers,) + out_shape
        if stacked is not None:
            aliases = {len(ins): 0}
            ins, specs = ins + [stacked], specs + [pl.BlockSpec(memory_space=pl.ANY)]
    return pl.pallas_call(
        body, grid=(M // tm, N // tn, nk), in_specs=specs, out_specs=out_spec,
        out_shape=jax.ShapeDtypeStruct(out_shape, F32), input_output_aliases=aliases,
        compiler_params=_cparams(("parallel", "parallel", "arbitrary")), name=name)(*ins)


def _row_spec(w, tile):
    return pl.BlockSpec((tile, w), lambda i: (i, 0))


def _par_spec(shape):
    return pl.BlockSpec(shape, lambda i: (0,) * len(shape))


class Cols:
    def __init__(self, arr, start, width):
        assert start % width == 0 or (start % 128 == 0 and width % 128 == 0)
        self.arr, self.start, self.width = arr, start, width
        self.shape, self.ndim = (arr.shape[0], width), 2


def _arr(r):
    return r.arr if isinstance(r, (Cols, ShardedWeight)) else r


def _rows_spec(a, tile):
    if isinstance(a, Cols):
        assert a.start % a.width == 0
        return pl.BlockSpec((tile, a.width), lambda i, blk=a.start // a.width: (i, blk))
    shape = a if isinstance(a, tuple) else a.shape
    if len(shape) == 3:
        return pl.BlockSpec((shape[0], tile, shape[2]), lambda i: (0, i, 0))
    return _row_spec(shape[1], tile)


def _rows_shape(S, w):
    return (w[0], S, w[1]) if isinstance(w, tuple) else (S, w)


def _rows_load(ref):
    if len(ref.shape) == 3:
        return jnp.concatenate([ref[h] for h in range(ref.shape[0])], axis=1)
    return ref[...]


def _rows_store(ref, val):
    if len(ref.shape) == 3:
        w = ref.shape[2]
        for h in range(ref.shape[0]):
            ref[h] = val[:, h * w:(h + 1) * w]
    else:
        ref[...] = val


def tl_fwd(fn, name, rows, pars, out_widths, tile=ROW_TILE):
    S = rows[0].shape[-2]
    nr = len(rows)

    def body(*refs):
        ins = [_rows_load(r) for r in refs[:nr]] + [r[...] for r in refs[nr:nr + len(pars)]]
        outs = fn(*ins)
        for o_ref, o in zip(refs[nr + len(pars):], outs):
            _rows_store(o_ref, o)

    shapes = [_rows_shape(S, w) for w in out_widths]
    return pl.pallas_call(
        body, grid=(S // tile,),
        in_specs=[_rows_spec(r, tile) for r in rows] + [_par_spec(p.shape) for p in pars],
        out_specs=[_rows_spec(s, tile) for s in shapes],
        out_shape=[jax.ShapeDtypeStruct(s, F32) for s in shapes],
        compiler_params=_cparams(("parallel",)), name=name)(*[_arr(r) for r in rows], *pars)


def tl_bwd(fn, name, rows, pars, cts, tile=ROW_TILE, row_grad=None):
    S = rows[0].shape[-2]
    nr, npar = len(rows), len(pars)
    row_grad = [True] * nr if row_grad is None else row_grad
    flat_cts = [c for group in cts for c in group]
    ncts = len(flat_cts)
    gi = [i for i in range(nr) if row_grad[i]]

    def body(*refs):
        row_v = [_rows_load(r) for r in refs[:nr]]
        par_v = [r[...] for r in refs[nr:nr + npar]]
        ct_refs = refs[nr + npar:nr + npar + ncts]
        out_refs = refs[nr + npar + ncts:]
        ct_v, pos = [], 0
        for group in cts:
            acc = _rows_load(ct_refs[pos])
            for q in range(1, len(group)):
                acc = acc + _rows_load(ct_refs[pos + q])
            pos += len(group)
            ct_v.append(acc)

        def f(diff_rows, par_vals):
            full = list(row_v)
            for idx, val in zip(gi, diff_rows):
                full[idx] = val
            return tuple(fn(*full, *par_vals))

        _, vjp = jax.vjp(f, [row_v[i] for i in gi], par_v)
        d_rows, d_pars = vjp(tuple(ct_v))
        for o_ref, g in zip(out_refs[:len(gi)], d_rows):
            _rows_store(o_ref, g)
        first = pl.program_id(0) == 0
        for o_ref, g in zip(out_refs[len(gi):], d_pars):
            @pl.when(first)
            def _(o_ref=o_ref):
                o_ref[...] = jnp.zeros_like(o_ref)
            o_ref[...] += g

    outs = pl.pallas_call(
        body, grid=(S // tile,),
        in_specs=[_rows_spec(r, tile) for r in rows] + [_par_spec(p.shape) for p in pars]
        + [_rows_spec(c, tile) for c in flat_cts],
        out_specs=[_rows_spec(rows[i].shape, tile) for i in gi] + [_par_spec(p.shape) for p in pars],
        out_shape=[jax.ShapeDtypeStruct(rows[i].shape, F32) for i in gi] + [jax.ShapeDtypeStruct(p.shape, F32) for p in pars],
        compiler_params=_cparams(("arbitrary",)), name=name)(*[_arr(r) for r in rows], *pars, *[_arr(c) for c in flat_cts])
    return list(outs[:len(gi)]), list(outs[len(gi):])


def _shift_rows(x, j):
    if j == 0:
        return x
    rolled = pltpu.roll(x, j, 0)
    row = lax.broadcasted_iota(jnp.int32, x.shape, 0)
    return jnp.where(row >= j, rolled, 0.0)


def _unshift_rows(x, j):
    if j == 0:
        return x
    S = x.shape[0]
    rolled = pltpu.roll(x, S - j, 0)
    row = lax.broadcasted_iota(jnp.int32, x.shape, 0)
    return jnp.where(row < S - j, rolled, 0.0)


def _fir_in_spec(x):
    first = x.start // 128 if isinstance(x, Cols) else 0
    return pl.BlockSpec((x.shape[0], 128), lambda j: (0, first + j))


def fir_fwd(x, taps, name):
    S, C = x.shape
    K = taps.shape[0]

    def body(x_ref, w_ref, y_ref):
        xv = x_ref[...]
        acc = jnp.zeros_like(xv)
        for k in range(K):
            acc = acc + _shift_rows(xv, K - 1 - k) * w_ref[pl.ds(k, 1), :]
        y_ref[...] = acc

    cs = pl.BlockSpec((S, 128), lambda j: (0, j))
    return pl.pallas_call(body, grid=(C // 128,), in_specs=[_fir_in_spec(x), pl.BlockSpec((K, 128), lambda j: (0, j))],
                          out_specs=cs, out_shape=jax.ShapeDtypeStruct((S, C), F32),
                          compiler_params=_cparams(("parallel",)), name=name)(_arr(x), taps)


def fir_bwd(x, taps, dy_list, name):
    S, C = x.shape
    K = taps.shape[0]
    n = len(dy_list)

    def body(*refs):
        x_ref, w_ref = refs[:2]
        dy = refs[2][...]
        for q in range(1, n):
            dy = dy + refs[2 + q][...]
        dx_ref, dw_ref, db_ref = refs[2 + n:]
        xv = x_ref[...]
        dx = jnp.zeros_like(xv)
        for k in range(K):
            j = K - 1 - k
            dx = dx + _unshift_rows(dy, j) * w_ref[pl.ds(k, 1), :]
            dw_ref[pl.ds(k, 1), :] = jnp.sum(dy * _shift_rows(xv, j), axis=0, keepdims=True)
        dx_ref[...] = dx
        db_ref[...] = jnp.sum(dy, axis=0, keepdims=True)

    cs = pl.BlockSpec((S, 128), lambda j: (0, j))
    ks = pl.BlockSpec((K, 128), lambda j: (0, j))
    bs = pl.BlockSpec((1, 128), lambda j: (0, j))
    return pl.pallas_call(body, grid=(C // 128,), in_specs=[_fir_in_spec(x), ks] + [cs] * n, out_specs=[cs, ks, bs],
                          out_shape=[jax.ShapeDtypeStruct((S, C), F32), jax.ShapeDtypeStruct((K, C), F32),
                                     jax.ShapeDtypeStruct((1, C), F32)],
                          compiler_params=_cparams(("parallel",)), name=name)(_arr(x), taps, *dy_list)


def _col(tile, lane, t):
    return jnp.sum(jnp.where(lane == t, tile, 0.0), axis=1, keepdims=True)


def _rwkv_step(s, rv, vcol):
    sa = jnp.sum(s * (-rv[3]), axis=1, keepdims=True)
    return s * rv[1] + sa * (rv[3] * rv[4]) + vcol * rv[2], sa


def _eye(n):
    return (lax.broadcasted_iota(jnp.int32, (n, n), 0) == lax.broadcasted_iota(jnp.int32, (n, n), 1)).astype(F32)


def _transposed(x):
    return lax.dot_general(_eye(x.shape[1]), x, (((1,), (1,)), ((), ())), precision=HI, preferred_element_type=F32)


def scan_fwd(r, w, k, v, kk, a, name):
    H, S, Dk = r.shape
    Dv = v.shape[1] // H
    Tc = SCAN_CHUNK
    nc = S // Tc
    rows = [r, w, k, kk, a]

    def body(*refs):
        row_refs = refs[:5]
        v_ref, y_ref, sall_ref, s_ref, vT_ref, yT_ref = refs[5:]

        @pl.when(pl.program_id(0) == 0)
        def _():
            s_ref[...] = jnp.zeros_like(s_ref)

        for h in range(H):
            vT_ref[h] = _transposed(v_ref[:, h * Dv:(h + 1) * Dv])
        yT_ref[...] = jnp.zeros_like(yT_ref)
        lane = lax.broadcasted_iota(jnp.int32, (Dv, Tc), 1)

        def step(t, states):
            new = []
            for h in range(H):
                s = states[h]
                sall_ref[t, h] = s
                rv = [ref[h, pl.ds(t, 1), :] for ref in row_refs]
                s, _ = _rwkv_step(s, rv, _col(vT_ref[h], lane, t))
                ycol = jnp.sum(s * rv[0], axis=1, keepdims=True)
                yT_ref[h] = jnp.where(lane == t, ycol, yT_ref[h])
                new.append(s)
            return tuple(new)

        states = lax.fori_loop(0, Tc, step, tuple(s_ref[h] for h in range(H)))
        for h in range(H):
            s_ref[h] = states[h]
            y_ref[:, h * Dv:(h + 1) * Dv] = _transposed(yT_ref[h])

    rs = pl.BlockSpec((H, Tc, Dk), lambda c: (0, c, 0))
    vs = pl.BlockSpec((Tc, H * Dv), lambda c: (c, 0))
    return pl.pallas_call(
        body, grid=(nc,), in_specs=[rs] * 5 + [vs],
        out_specs=[vs, pl.BlockSpec((Tc, H, Dv, Dk), lambda c: (c, 0, 0, 0))],
        out_shape=[jax.ShapeDtypeStruct((S, H * Dv), F32), jax.ShapeDtypeStruct((S, H, Dv, Dk), F32)],
        scratch_shapes=[pltpu.VMEM((H, Dv, Dk), F32), pltpu.VMEM((H, Dv, Tc), F32), pltpu.VMEM((H, Dv, Tc), F32)],
        compiler_params=_cparams(("arbitrary",)), name=name)(*rows, v)


def scan_bwd(r, w, k, v, kk, a, sall, dy, name):
    H, S, Dk = r.shape
    Dv = v.shape[1] // H
    Tc = SCAN_CHUNK
    nc = S // Tc
    rows = [r, w, k, kk, a]

    def body(*refs):
        row_refs = refs[:5]
        v_ref, dy_ref, sall_ref = refs[5:8]
        drow_refs = refs[8:13]
        dv_ref, ds_ref, vT_ref, dyT_ref, dvT_ref = refs[13:]

        @pl.when(pl.program_id(0) == 0)
        def _():
            ds_ref[...] = jnp.zeros_like(ds_ref)

        for h in range(H):
            vT_ref[h] = _transposed(v_ref[:, h * Dv:(h + 1) * Dv])
            dyT_ref[h] = _transposed(dy_ref[:, h * Dv:(h + 1) * Dv])
        dvT_ref[...] = jnp.zeros_like(dvT_ref)
        lane = lax.broadcasted_iota(jnp.int32, (Dv, Tc), 1)

        def bstep(i, carry):
            t = Tc - 1 - i
            new = []
            for h in range(H):
                ds = carry[h]
                sp = sall_ref[t, h]
                rv = [ref[h, pl.ds(t, 1), :] for ref in row_refs]
                vcol = _col(vT_ref[h], lane, t)
                dycol = _col(dyT_ref[h], lane, t)
                st, sa = _rwkv_step(sp, rv, vcol)
                drow_refs[0][h, pl.ds(t, 1), :] = jnp.sum(st * dycol, axis=0, keepdims=True)
                g = ds + dycol * rv[0]
                drow_refs[1][h, pl.ds(t, 1), :] = jnp.sum(g * sp, axis=0, keepdims=True)
                drow_refs[2][h, pl.ds(t, 1), :] = jnp.sum(g * vcol, axis=0, keepdims=True)
                dvcol = jnp.sum(g * rv[2], axis=1, keepdims=True)
                dsa = jnp.sum(g * (rv[3] * rv[4]), axis=1, keepdims=True)
                db = jnp.sum(g * sa, axis=0, keepdims=True)
                dnkk = jnp.sum(sp * dsa, axis=0, keepdims=True)
                drow_refs[3][h, pl.ds(t, 1), :] = db * rv[4] - dnkk
                drow_refs[4][h, pl.ds(t, 1), :] = db * rv[3]
                dvT_ref[h] = jnp.where(lane == t, dvcol, dvT_ref[h])
                new.append(g * rv[1] - dsa * rv[3])
            return tuple(new)

        carry = lax.fori_loop(0, Tc, bstep, tuple(ds_ref[h] for h in range(H)))
        for h in range(H):
            ds_ref[h] = carry[h]
            dv_ref[:, h * Dv:(h + 1) * Dv] = _transposed(dvT_ref[h])

    rs = pl.BlockSpec((H, Tc, Dk), lambda c: (0, nc - 1 - c, 0))
    vs = pl.BlockSpec((Tc, H * Dv), lambda c: (nc - 1 - c, 0))
    tile = pltpu.VMEM((H, Dv, Tc), F32)
    outs = pl.pallas_call(
        body, grid=(nc,),
        in_specs=[rs] * 5 + [vs, vs, pl.BlockSpec((Tc, H, Dv, Dk), lambda c: (nc - 1 - c, 0, 0, 0))],
        out_specs=[rs] * 5 + [vs],
        out_shape=[jax.ShapeDtypeStruct((H, S, Dk), F32)] * 5 + [jax.ShapeDtypeStruct((S, H * Dv), F32)],
        scratch_shapes=[pltpu.VMEM((H, Dv, Dk), F32), tile, tile, tile],
        compiler_params=_cparams(("arbitrary",)), name=name)(*rows, v, dy, sall)
    return list(outs[:5]), outs[5]


CHUNK = 128
SSD_GROUP = 4
HGRN_GROUP = 4
HGRN_CHUNK = 64


def chunk_fwd(fn, name, blocks, state_shape, out_width, group, CHUNK=CHUNK):
    H, S, _ = blocks[0].shape
    nc = S // CHUNK
    nb = len(blocks)

    def body(*refs):
        o_ref, sv_ref, st = refs[nb:]

        @pl.when(pl.program_id(1) == 0)
        def _():
            st[...] = jnp.zeros_like(st)

        for g in range(group):
            s0 = st[g]
            sv_ref[g, 0] = s0
            s1, out = fn(s0, *[r[g] for r in refs[:nb]])
            st[g] = s1
            o_ref[g] = out

    spec = lambda w: pl.BlockSpec((group, CHUNK, w), lambda h, c: (h, c, 0))
    return pl.pallas_call(
        body, grid=(H // group, nc), in_specs=[spec(b.shape[2]) for b in blocks],
        out_specs=[spec(out_width), pl.BlockSpec((group, 1) + state_shape, lambda h, c: (h, c, 0, 0))],
        out_shape=[jax.ShapeDtypeStruct((H, S, out_width), F32), jax.ShapeDtypeStruct((H, nc) + state_shape, F32)],
        scratch_shapes=[pltpu.VMEM((group,) + state_shape, F32)],
        compiler_params=_cparams(("parallel", "arbitrary")), name=name)(*blocks)


def chunk_bwd(fn, name, blocks, states, dout, group, CHUNK=CHUNK):
    H, S, _ = blocks[0].shape
    nc = S // CHUNK
    nb = len(blocks)
    state_shape = states.shape[2:]

    def body(*refs):
        sv_ref, do_ref = refs[nb], refs[nb + 1]
        d_refs = refs[nb + 2:2 * nb + 2]
        dst = refs[2 * nb + 2]

        @pl.when(pl.program_id(1) == 0)
        def _():
            dst[...] = jnp.zeros_like(dst)

        for g in range(group):
            _, vjp = jax.vjp(fn, sv_ref[g, 0], *[r[g] for r in refs[:nb]])
            grads = vjp((dst[g], do_ref[g]))
            dst[g] = grads[0]
            for d_ref, gr in zip(d_refs, grads[1:]):
                d_ref[g] = gr

    spec = lambda w: pl.BlockSpec((group, CHUNK, w), lambda h, c: (h, nc - 1 - c, 0))
    return pl.pallas_call(
        body, grid=(H // group, nc),
        in_specs=[spec(b.shape[2]) for b in blocks]
        + [pl.BlockSpec((group, 1) + state_shape, lambda h, c: (h, nc - 1 - c, 0, 0)), spec(dout.shape[2])],
        out_specs=[spec(b.shape[2]) for b in blocks],
        out_shape=[jax.ShapeDtypeStruct(b.shape, F32) for b in blocks],
        scratch_shapes=[pltpu.VMEM((group,) + state_shape, F32)],
        compiler_params=_cparams(("parallel", "arbitrary")), name=name)(*blocks, states, dout)


def _bdot(a, b, dims):
    return lax.dot_general(a.astype(BF16), b.astype(BF16), (dims, ((), ())), preferred_element_type=F32)


def ssd_chunk(state, cb, bb, da, xdt):
    T = cb.shape[0]
    ti = lax.broadcasted_iota(jnp.int32, (T, T), 0)
    si = lax.broadcasted_iota(jnp.int32, (T, T), 1)
    mask = ti >= si
    cs = jnp.dot(mask.astype(F32), da, precision=HI, preferred_element_type=F32)
    pick = (lax.broadcasted_iota(jnp.int32, cs.shape, 1) == 0).astype(F32)
    cs_row = lax.dot_general(pick, cs, (((1,), (1,)), ((), ())), precision=HI, preferred_element_type=F32)
    lmat = jnp.where(mask, jnp.exp(jnp.where(mask, cs - cs_row, 0.0)), 0.0)
    scores = _bdot(cb, bb, ((1,), (1,))) * lmat
    y = _bdot(scores, xdt, ((1,), (0,))) + _bdot(cb, state, ((1,), (1,))) * jnp.exp(cs[:, :HEAD_DIM])
    last = cs[T - 1:T, :]
    new_state = state * jnp.exp(last) + _bdot(xdt, bb * jnp.exp(last - cs), ((0,), (0,)))
    return new_state, y


HGRN_SUB = 16


def hgrn_chunk(state, q, k, lf, v):
    T, C = q.shape[0], HGRN_SUB
    ti = lax.broadcasted_iota(jnp.int32, (C, C), 0)
    si = lax.broadcasted_iota(jnp.int32, (C, C), 1)
    tril = (ti >= si).astype(F32)
    row = lax.broadcasted_iota(jnp.int32, (C, q.shape[1]), 0)
    outs = []
    for j in range(T // C):
        qj, kj, lj, vj = (a[j * C:(j + 1) * C] for a in (q, k, lf, v))
        b = jnp.dot(tril, lj, precision=HI, preferred_element_type=F32)
        o = _bdot(qj * jnp.exp(b), state, ((1,), (1,)))
        for s in range(C):
            m = row >= s
            e = jnp.where(m, jnp.exp(jnp.where(m, b - b[s:s + 1], 0.0)), 0.0)
            o = o + jnp.sum(qj * kj[s:s + 1] * e, axis=1, keepdims=True) * vj[s:s + 1]
        last = b[C - 1:C]
        state = state * jnp.exp(last) + _bdot(vj, kj * jnp.exp(last - b), ((0,), (0,)))
        outs.append(o)
    return state, jnp.concatenate(outs, axis=0)


def _attn_block(q, kp, kc, vp, vc, n, slope, dilation):
    blk = ATTN_BLK
    k2 = jnp.concatenate([kp, kc], axis=0)
    v2 = jnp.concatenate([vp, vc], axis=0)
    s = _bdot(q, k2, ((1,), (1,))) * (HEAD_DIM ** -0.5)
    i = lax.broadcasted_iota(jnp.int32, (blk, 2 * blk), 0)
    j = lax.broadcasted_iota(jnp.int32, (blk, 2 * blk), 1)
    dist = blk + i - j
    first_key = jnp.where(n > 0, 0, blk)
    valid = (dist >= 0) & (dist <= blk) & (j >= first_key)
    s = s - slope * (dist * dilation).astype(F32)
    s = jnp.where(valid, s, -1e30)
    m = jnp.max(s, axis=-1, keepdims=True)
    p = jnp.exp(s - m)
    l = jnp.sum(p, axis=-1, keepdims=True)
    o = _bdot(p, v2, ((1,), (0,))) / l
    lse = jnp.broadcast_to(m + jnp.log(l), o.shape)
    return o, lse


_QCOL = SEG_Q // 128
PAIR = 2 * HEAD_DIM


def _attn_specs(rows):
    cur = lambda j: pl.BlockSpec((rows, PAIR), lambda p, n: (n, j + p))
    prev = lambda j: pl.BlockSpec((rows, PAIR), lambda p, n: (jnp.maximum(n - 1, 0), j + p))
    return cur, prev


def _pair_slope(pair, h):
    return jnp.where(pair == 0, jnp.float32(ALIBI_SLOPES[h]), jnp.float32(ALIBI_SLOPES[2 + h]))


def _halves(t):
    return [t[:, h * HEAD_DIM:(h + 1) * HEAD_DIM] for h in range(2)]


def _for_classes(dilation, step):
    if dilation == 1:
        step(0)
    else:
        lax.fori_loop(0, dilation, lambda z, c: (step(z), c)[1], 0)


def attn_fwd(proj, dilation, name):
    S = proj.shape[0]
    blk = ATTN_BLK
    rows = blk * dilation
    cur, prev = _attn_specs(rows)

    def body(q_ref, kp_ref, kc_ref, vp_ref, vc_ref, o_ref, l_ref):
        pair, n = pl.program_id(0), pl.program_id(1)

        def one_class(z):
            sel = pl.ds(z, blk, stride=dilation) if dilation > 1 else pl.ds(0, blk)
            q, kp, kc, vp, vc = (_halves(r[sel, :]) for r in (q_ref, kp_ref, kc_ref, vp_ref, vc_ref))
            res = [_attn_block(q[h], kp[h], kc[h], vp[h], vc[h], n, _pair_slope(pair, h), dilation) for h in range(2)]
            o_ref[sel, :] = jnp.concatenate([r[0] for r in res], axis=1)
            l_ref[sel, :] = jnp.concatenate([r[1] for r in res], axis=1)

        _for_classes(dilation, one_class)

    return pl.pallas_call(
        body, grid=(2, S // rows),
        in_specs=[cur(_QCOL), prev(_QCOL + 2), cur(_QCOL + 2), prev(_QCOL + 4), cur(_QCOL + 4)],
        out_specs=[cur(0), cur(0)], out_shape=[jax.ShapeDtypeStruct((S, D_GROUP), F32)] * 2,
        compiler_params=_cparams(("parallel", "arbitrary")), name=name)(proj, proj, proj, proj, proj)


def attn_bwd(proj, do, dlse, dilation, name):
    S = proj.shape[0]
    blk = ATTN_BLK
    rows = blk * dilation
    cur, prev = _attn_specs(rows)
    full = pl.BlockSpec((S, PAIR), lambda p, n: (0, p))

    def body(q_ref, kp_ref, kc_ref, vp_ref, vc_ref, do_ref, dl_ref, dq_ref, dk_ref, dv_ref):
        pair, n = pl.program_id(0), pl.program_id(1)

        @pl.when(n == 0)
        def _():
            dk_ref[...] = jnp.zeros_like(dk_ref)
            dv_ref[...] = jnp.zeros_like(dv_ref)

        def one_class(z):
            sel = pl.ds(z, blk, stride=dilation) if dilation > 1 else pl.ds(0, blk)
            q, kp, kc, vp, vc, do_v, dl_v = (_halves(r[sel, :]) for r in
                                             (q_ref, kp_ref, kc_ref, vp_ref, vc_ref, do_ref, dl_ref))
            grads = []
            for h in range(2):
                f = lambda q_, kp_, kc_, vp_, vc_, h=h: _attn_block(q_, kp_, kc_, vp_, vc_, n, _pair_slope(pair, h), dilation)
                _, vjp = jax.vjp(f, q[h], kp[h], kc[h], vp[h], vc[h])
                grads.append(vjp((do_v[h], dl_v[h])))
            both = lambda j: jnp.concatenate([grads[0][j], grads[1][j]], axis=1)
            dq_ref[sel, :] = both(0)
            if dilation > 1:
                here = pl.ds(n * rows + z, blk, stride=dilation)
                before = pl.ds(jnp.maximum(n - 1, 0) * rows + z, blk, stride=dilation)
            else:
                here = pl.ds(pl.multiple_of(n * blk, blk), blk)
                before = pl.ds(pl.multiple_of(jnp.maximum(n - 1, 0) * blk, blk), blk)
            dk_ref[here, :] = dk_ref[here, :] + both(2)
            dv_ref[here, :] = dv_ref[here, :] + both(4)
            dk_ref[before, :] = dk_ref[before, :] + both(1)
            dv_ref[before, :] = dv_ref[before, :] + both(3)

        _for_classes(dilation, one_class)

    return pl.pallas_call(
        body, grid=(2, S // rows),
        in_specs=[cur(_QCOL), prev(_QCOL + 2), cur(_QCOL + 2), prev(_QCOL + 4), cur(_QCOL + 4), cur(0), cur(0)],
        out_specs=[cur(0), full, full], out_shape=[jax.ShapeDtypeStruct((S, D_GROUP), F32)] * 3,
        compiler_params=_cparams(("parallel", "arbitrary")), name=name)(proj, proj, proj, proj, proj, do, dlse)


def _head_ones(width, group):
    i = lax.broadcasted_iota(jnp.int32, (width, width), 0) // group
    j = lax.broadcasted_iota(jnp.int32, (width, width), 1) // group
    return (i == j).astype(F32)


def _group_sum(x, group):
    return jnp.dot(x, _head_ones(x.shape[1], group), precision=HI, preferred_element_type=F32)


def _spread(width_in, width_out, rep):
    i = lax.broadcasted_iota(jnp.int32, (width_in, width_out), 0)
    j = lax.broadcasted_iota(jnp.int32, (width_in, width_out), 1) // rep
    return (i == j).astype(F32)


def _hdot(a, b):
    return jnp.dot(a, b, precision=HI, preferred_element_type=F32)


def _sigmoid(x):
    return 1.0 / (1.0 + jnp.exp(-x))


def _softplus(x):
    return jnp.maximum(x, 0.0) + jnp.log(1.0 + jnp.exp(jnp.minimum(x, -x)))


def _silu(x):
    return x * _sigmoid(x)


def rwkv_pre(layer):
    def fn(*args):
        if layer == 0:
            fs, w0, w2p, a0, a2p, g2p, k_k, k_a = args
        else:
            fs, vfirst, w0, w2p, a0, a2p, g2p, k_k, k_a, v0, v2p = args
        r, k, v = fs[:, 0:256], fs[:, 256:512], fs[:, 512:768]
        lora = fs[:, 768:896]
        w_log = -_softplus(-(w0 + _hdot(jnp.tanh(lora), w2p))) - 0.5
        decay = jnp.exp(-jnp.exp(w_log))
        a = _sigmoid(a0 + _hdot(lora, a2p))
        g = _hdot(_sigmoid(lora), g2p)
        if layer > 0:
            v = v + (vfirst - v) * _sigmoid(v0 + _hdot(fs[:, 896:1024], v2p))
        kk = k * k_k
        kk = kk / jnp.maximum(jnp.sqrt(_group_sum(kk * kk, HEAD_DIM)), 1e-12)
        k = k * (1.0 + (a - 1.0) * k_a)
        return r, decay, k, v, kk, a, g
    return fn


def rwkv_post(y, r, k, v, g, lnx_w, lnx_b, r_k):
    mu = _group_sum(y, HEAD_DIM) * (1.0 / HEAD_DIM)
    yc = y - mu
    var = _group_sum(yc * yc, HEAD_DIM) * (1.0 / HEAD_DIM)
    yn = yc * lax.rsqrt(var + RWKV_GN_EPS) * lnx_w + lnx_b
    bonus = _group_sum(r * k * r_k, HEAD_DIM) * v
    return ((yn + bonus) * g,)


def attn_combine(o1, o2, o3, l1, l2, l3):
    m = jnp.maximum(jnp.maximum(l1, l2), l3)
    e1, e2, e3 = jnp.exp(l1 - m), jnp.exp(l2 - m), jnp.exp(l3 - m)
    return ((o1 * e1 + o2 * e2 + o3 * e3) / (e1 + e2 + e3),)


def ssd_pre(xc, dtr, conv_b, dt_bias, a_log):
    xbc = _silu(xc + conv_b)
    xs, bm, cm = xbc[:, 0:256], xbc[:, 256:512], xbc[:, 512:768]
    dt = _softplus(dtr + dt_bias)
    a_neg = -jnp.exp(a_log)
    wide = _spread(128, N_HEADS * SSD_STATE, SSD_STATE)
    w = _hdot(dt, wide) * _hdot(a_neg, wide)
    xdt = xs * _hdot(dt, _spread(128, D_GROUP, HEAD_DIM))
    rr = jnp.concatenate([cm[:, 0:128], cm[:, 0:128], cm[:, 128:256], cm[:, 128:256]], axis=1)
    kk = jnp.concatenate([bm[:, 0:128], bm[:, 0:128], bm[:, 128:256], bm[:, 128:256]], axis=1)
    return rr, w, kk, xdt, xs


def ssd_post(ys, z, xs, d_skip, norm_w):
    y = ys + xs * _hdot(d_skip, _spread(128, D_GROUP, HEAD_DIM))
    y = y * _silu(z)
    half = D_GROUP // 2
    parts = []
    for g in range(2):
        t = y[:, g * half:(g + 1) * half]
        parts.append(t * lax.rsqrt(jnp.mean(t * t, axis=-1, keepdims=True) + RMS_EPS))
    return (jnp.concatenate(parts, axis=1) * norm_w,)


def hgrn_pre(seg, lb):
    q, f, i = seg[:, 0:256], seg[:, 256:512], seg[:, 512:768]
    forget = lb + (1.0 - lb) * _sigmoid(f)
    return _silu(q), 1.0 - forget, jnp.log(forget), i


def hgrn_post(o, seg, norm_w):
    g = seg[:, 768:1024]
    ms = _group_sum(o * o, HEAD_DIM) * (1.0 / HEAD_DIM)
    return (o * lax.rsqrt(ms + RMS_EPS) * norm_w * _silu(g),)


def ln_res(x, y, w, b):
    z = ALPHA * x + y
    mu = jnp.mean(z, axis=-1, keepdims=True)
    zc = z - mu
    var = jnp.mean(zc * zc, axis=-1, keepdims=True)
    return (zc * lax.rsqrt(var + LN_EPS) * w + b,)


def loss_and_grad(y, tgt, name):
    S, D = y.shape
    tile = ROW_TILE

    def body(y_ref, t_ref, l_ref, dy_ref):
        e = y_ref[...] - t_ref[...]
        dy_ref[...] = e * (1.0 / D)

        @pl.when(pl.program_id(0) == 0)
        def _():
            l_ref[...] = jnp.zeros_like(l_ref)

        per_row = 0.5 * jnp.mean(e * e, axis=-1, keepdims=True)
        l_ref[...] += jnp.sum(per_row, axis=0, keepdims=True) * jnp.ones((1, 128), F32)

    return pl.pallas_call(body, grid=(S // tile,), in_specs=[_row_spec(D, tile)] * 2,
                          out_specs=[_par_spec((1, 128)), _row_spec(D, tile)],
                          out_shape=[jax.ShapeDtypeStruct((1, 128), F32), jax.ShapeDtypeStruct((S, D), F32)],
                          compiler_params=_cparams(("arbitrary",)), name=name)(y, tgt)


def add_rows(arrs, name):
    (out,) = tl_fwd(lambda *a: (functools.reduce(lambda p, q: p + q, a),), name, arrs, [], [arrs[0].shape[1]])
    return out


def small_fwd(fn, name, ins, out_shapes):
    n = len(ins)

    def body(*refs):
        outs = fn(*[r[...] for r in refs[:n]])
        for o_ref, o in zip(refs[n:], outs):
            o_ref[...] = o

    return pl.pallas_call(body, out_shape=[jax.ShapeDtypeStruct(s, F32) for s in out_shapes], name=name)(*ins)


def small_bwd(fn, name, ins, cts):
    n, m = len(ins), len(cts)

    def body(*refs):
        _, vjp = jax.vjp(lambda *a: tuple(fn(*a)), *[r[...] for r in refs[:n]])
        grads = vjp(tuple(r[...] for r in refs[n:n + m]))
        for o_ref, g in zip(refs[n + m:], grads):
            o_ref[...] = g

    return pl.pallas_call(body, out_shape=[jax.ShapeDtypeStruct(a.shape, F32) for a in ins], name=name)(*ins, *cts)


def param_prep(lower_bounds, mu0, mu1):
    e = jnp.exp(lower_bounds - jnp.max(lower_bounds, axis=0, keepdims=True))
    sm = e / jnp.sum(e, axis=0, keepdims=True)
    lb0 = sm[0:1] - sm[0:1]
    lb1 = sm[0:1] + sm[1:2] - sm[0:1]
    return lb0, lb1, mu0, 1.0 - mu0, mu1, 1.0 - mu1


def _rows_tile(rows):
    return _pick(rows, (256, 128, 64, 32, 16, 8))


def sum_parts(parts, name):
    P, rows, cols = parts.shape
    tile = _rows_tile(rows)

    def body(p_ref, o_ref):
        acc = p_ref[0]
        for p in range(1, P):
            acc = acc + p_ref[p]
        o_ref[...] = acc

    return pl.pallas_call(body, grid=(rows // tile,), in_specs=[pl.BlockSpec((P, tile, cols), lambda i: (0, i, 0))],
                          out_specs=pl.BlockSpec((tile, cols), lambda i: (i, 0)),
                          out_shape=jax.ShapeDtypeStruct((rows, cols), F32),
                          compiler_params=_cparams(("parallel",)), name=name)(parts)


def pair_sum(own, got, ids, name):
    _, P, rows, cols = own.shape
    tile = _rows_tile(rows)

    def body(ids_ref, own_ref, got_ref, o_ref):
        o_ref[0] = (own_ref[0, 0] + got_ref[0]).astype(BF16)

    grid_spec = pltpu.PrefetchScalarGridSpec(
        num_scalar_prefetch=1, grid=(P, rows // tile),
        in_specs=[pl.BlockSpec((1, 1, tile, cols), lambda s, i, ids: (ids[0], s, i, 0)),
                  pl.BlockSpec((1, tile, cols), lambda s, i, ids: (s, i, 0))],
        out_specs=pl.BlockSpec((1, tile, cols), lambda s, i, ids: (s, i, 0)))
    return pl.pallas_call(body, grid_spec=grid_spec, out_shape=jax.ShapeDtypeStruct((P, rows, cols), BF16),
                          compiler_params=_cparams(("parallel", "parallel")), name=name)(ids, own, got)


def sum_chips(slots, own, got, ids, name):
    P, rows, cols = slots.shape
    tile = _rows_tile(rows)

    def body(ids_ref, s_ref, own_ref, got_ref, o_ref):
        chip = ids_ref[1]
        mine = own_ref[0, 0] + got_ref[0]
        acc = None
        for p in range(P):
            term = jnp.where(chip == p, mine, s_ref[p].astype(F32))
            acc = term if acc is None else acc + term
        o_ref[0] = acc

    grid_spec = pltpu.PrefetchScalarGridSpec(
        num_scalar_prefetch=1, grid=(rows // tile,),
        in_specs=[pl.BlockSpec((P, tile, cols), lambda i, ids: (0, i, 0)),
                  pl.BlockSpec((1, 1, tile, cols), lambda i, ids: (ids[0], ids[1], i, 0)),
                  pl.BlockSpec((1, tile, cols), lambda i, ids: (ids[1], i, 0))],
        out_specs=pl.BlockSpec((1, tile, cols), lambda i, ids: (ids[0], i, 0)))
    return pl.pallas_call(body, grid_spec=grid_spec, out_shape=jax.ShapeDtypeStruct((2, rows, cols), F32),
                          compiler_params=_cparams(("parallel",)), name=name)(ids, slots, own, got)


def adamw(w, g, m, v, name, lane_tiled=False):
    layers, rows, cols = w.shape
    tile = _rows_tile(rows)

    def body(w_ref, g_ref, m_ref, v_ref, d_ref, nm_ref, nv_ref):
        gv = g_ref[...]
        nm = ADAM_B1 * m_ref[...] + (1.0 - ADAM_B1) * gv
        nv = ADAM_B2 * v_ref[...] + (1.0 - ADAM_B2) * jnp.square(gv)
        m_hat = nm / (1.0 - ADAM_B1 ** ADAM_STEP)
        v_hat = nv / (1.0 - ADAM_B2 ** ADAM_STEP)
        d_ref[...] = -ADAM_LR * (m_hat / (jnp.sqrt(v_hat) + ADAM_EPS) + ADAM_WD * w_ref[...])
        nm_ref[...] = nm
        nv_ref[...] = nv

    spec, grid = pl.BlockSpec((None, tile, cols), lambda l, i: (l, i, 0)), (layers, rows // tile)
    if lane_tiled:
        spec, grid = pl.BlockSpec((layers, rows, 256), lambda l, i: (0, 0, i)), (1, cols // 256)
    return pl.pallas_call(body, grid=grid, in_specs=[spec] * 4, out_specs=[spec] * 3,
                          out_shape=[jax.ShapeDtypeStruct((layers, rows, cols), F32)] * 3,
                          compiler_params=_cparams(("parallel", "parallel")), name=name)(w, g, m, v)


MESH = pl.DeviceIdType.MESH
ANY = pl.BlockSpec(memory_space=pl.ANY)


def _flip(v, bit):
    return 1 - v if bit else v


_CHIP_RELATIONS = ((1, 0), (0, 1), (1, 1))


def gather_chips(arrs, small, name):
    n = len(arrs)

    def body(*refs):
        ins, small_in = refs[:n], refs[n]
        outs, small_out = refs[n + 1:2 * n + 1], refs[2 * n + 1]
        send, recv, fsend, frecv, ssend, srecv = refs[2 * n + 2:]
        x, y, c = lax.axis_index("x"), lax.axis_index("y"), lax.axis_index("c")
        me = 2 * x + y
        chips = [(_flip(x, bx), _flip(y, by)) for bx, by in _CHIP_RELATIONS]

        def over_ici(i, r, block_chip):
            return pltpu.make_async_remote_copy(src_ref=ins[i].at[c], dst_ref=outs[i].at[block_chip, c],
                                                send_sem=send.at[i, r], recv_sem=recv.at[i, r],
                                                device_id=(chips[r][0], chips[r][1], c), device_id_type=MESH)

        def to_sibling(i, r, layer):
            blk = outs[i].at[2 * chips[r][0] + chips[r][1], layer]
            return pltpu.make_async_remote_copy(src_ref=blk, dst_ref=blk, send_sem=fsend.at[i, r],
                                                recv_sem=frecv.at[i, r], device_id=(x, y, 1 - c), device_id_type=MESH)

        first = [over_ici(i, r, me) for i in range(n) for r in range(3)]
        smalls = [pltpu.make_async_remote_copy(src_ref=small_in, dst_ref=small_out.at[me], send_sem=ssend.at[r],
                                               recv_sem=srecv.at[r], device_id=(chips[r][0], chips[r][1], c),
                                               device_id_type=MESH) for r in range(3)]
        for cp in first + smalls:
            cp.start()
        passed = []
        for r in range(3):
            for i in range(n):
                over_ici(i, r, 2 * chips[r][0] + chips[r][1]).wait_recv()
                fw = to_sibling(i, r, c)
                fw.start()
                passed.append(fw)
        for r in range(3):
            for i in range(n):
                to_sibling(i, r, 1 - c).wait_recv()
        for cp in first + passed:
            cp.wait_send()
        for cp in smalls:
            cp.wait()

    return pl.pallas_call(
        body, in_specs=[ANY] * (n + 1), out_specs=[ANY] * (n + 1),
        out_shape=[jax.ShapeDtypeStruct((4,) + a.shape, a.dtype) for a in arrs]
        + [jax.ShapeDtypeStruct((4,) + small.shape, small.dtype)],
        scratch_shapes=[pltpu.SemaphoreType.DMA((n, 3)), pltpu.SemaphoreType.DMA((n, 3)), pltpu.SemaphoreType.DMA((n, 3)),
                        pltpu.SemaphoreType.DMA((n, 3)), pltpu.SemaphoreType.DMA((3,)), pltpu.SemaphoreType.DMA((3,))],
        name=name)(*arrs, small)


_RELATIONS = tuple((r >> 2 & 1, r >> 1 & 1, r & 1) for r in range(1, 8))


def gather_devices(arr, name):
    def body(in_ref, out_ref, send, recv, loc):
        x, y, c = lax.axis_index("x"), lax.axis_index("y"), lax.axis_index("c")
        me = 4 * x + 2 * y + c
        lc = pltpu.make_async_copy(in_ref, out_ref.at[me], loc)
        lc.start()
        pending = [lc]
        for r, (bx, by, bc) in enumerate(_RELATIONS):
            cp = pltpu.make_async_remote_copy(src_ref=in_ref, dst_ref=out_ref.at[me], send_sem=send.at[r],
                                              recv_sem=recv.at[r], device_id=(_flip(x, bx), _flip(y, by), _flip(c, bc)),
                                              device_id_type=MESH)
            cp.start()
            pending.append(cp)
        for cp in pending:
            cp.wait()

    return pl.pallas_call(
        body, in_specs=[ANY], out_specs=ANY, out_shape=jax.ShapeDtypeStruct((8,) + arr.shape, arr.dtype),
        scratch_shapes=[pltpu.SemaphoreType.DMA((7,)), pltpu.SemaphoreType.DMA((7,)), pltpu.SemaphoreType.DMA(())],
        name=name)(arr)


def pair_exchange(arrs, name):
    n = len(arrs)

    def body(*refs):
        ins, outs = refs[:n], refs[n:2 * n]
        send, recv = refs[2 * n:]
        x, y, c = lax.axis_index("x"), lax.axis_index("y"), lax.axis_index("c")
        pending = []
        for i in range(n):
            for s in range(4):
                cp = pltpu.make_async_remote_copy(src_ref=ins[i].at[1 - c, s], dst_ref=outs[i].at[s],
                                                  send_sem=send.at[i, s], recv_sem=recv.at[i, s],
                                                  device_id=(x, y, 1 - c), device_id_type=MESH)
                cp.start()
                pending.append(cp)
        for cp in pending:
            cp.wait()

    return pl.pallas_call(
        body, in_specs=[ANY] * n, out_specs=[ANY] * n,
        out_shape=[jax.ShapeDtypeStruct(a.shape[1:], a.dtype) for a in arrs],
        scratch_shapes=[pltpu.SemaphoreType.DMA((n, 4)), pltpu.SemaphoreType.DMA((n, 4))],
        name=name)(*arrs)


def reduce_chips(arrs, name):
    n = len(arrs)

    def body(*refs):
        ins, outs = refs[:n], refs[n:2 * n]
        send, recv, loc = refs[2 * n:]
        x, y, c = lax.axis_index("x"), lax.axis_index("y"), lax.axis_index("c")
        me = 2 * x + y
        pending = []
        for i in range(n):
            for r, (bx, by) in enumerate(_CHIP_RELATIONS):
                px, py = _flip(x, bx), _flip(y, by)
                cp = pltpu.make_async_remote_copy(src_ref=ins[i].at[2 * px + py], dst_ref=outs[i].at[me],
                                                  send_sem=send.at[i, r], recv_sem=recv.at[i, r],
                                                  device_id=(px, py, c), device_id_type=MESH)
                cp.start()
                pending.append(cp)
        for i in range(n):
            lc = pltpu.make_async_copy(ins[i].at[me], outs[i].at[me], loc.at[i])
            lc.start()
            pending.append(lc)
        for cp in pending:
            cp.wait()

    return pl.pallas_call(
        body, in_specs=[ANY] * n, out_specs=[ANY] * n,
        out_shape=[jax.ShapeDtypeStruct(a.shape, a.dtype) for a in arrs],
        scratch_shapes=[pltpu.SemaphoreType.DMA((n, 3)), pltpu.SemaphoreType.DMA((n, 3)), pltpu.SemaphoreType.DMA((n,))],
        name=name)(*arrs)


EXCHANGE_PIECES = 8


def sibling_exchange(arrs, name):
    n = len(arrs)

    def body(*refs):
        bufs = refs[n:2 * n]
        send, recv = refs[2 * n:]
        x, y, c = lax.axis_index("x"), lax.axis_index("y"), lax.axis_index("c")
        pending = []
        for i in range(n):
            rows = bufs[i].shape[1] // EXCHANGE_PIECES
            for j in range(EXCHANGE_PIECES):
                piece = bufs[i].at[c, pl.ds(j * rows, rows)]
                cp = pltpu.make_async_remote_copy(src_ref=piece, dst_ref=piece, send_sem=send.at[i, j],
                                                  recv_sem=recv.at[i, j], device_id=(x, y, 1 - c), device_id_type=MESH)
                cp.start()
                pending.append(cp)
        for i in range(n):
            rows = bufs[i].shape[1] // EXCHANGE_PIECES
            for j in range(EXCHANGE_PIECES):
                landed = bufs[i].at[1 - c, pl.ds(j * rows, rows)]
                pltpu.make_async_remote_copy(src_ref=landed, dst_ref=landed, send_sem=send.at[i, j], recv_sem=recv.at[i, j],
                                             device_id=(x, y, 1 - c), device_id_type=MESH).wait_recv()
        for cp in pending:
            cp.wait_send()

    return pl.pallas_call(
        body, in_specs=[ANY] * n, out_specs=[ANY] * n,
        out_shape=[jax.ShapeDtypeStruct(a.shape, a.dtype) for a in arrs], input_output_aliases={i: i for i in range(n)},
        scratch_shapes=[pltpu.SemaphoreType.DMA((n, EXCHANGE_PIECES)), pltpu.SemaphoreType.DMA((n, EXCHANGE_PIECES))],
        name=name)(*arrs)


def rwkv_fwd(l, seg, taps, pars, vfirst):
    fs = fir_fwd(seg, taps, f"rwkv_shift_fwd{l}")
    rows = [fs] + ([vfirst] if l else [])
    R, W, K, V, KK, A, G = tl_fwd(rwkv_pre(l), f"rwkv_pre_fwd{l}", rows, pars["pre"],
                                  [HM64, HM64, HM64, D_GROUP, HM64, HM64, D_GROUP])
    Y, sall = scan_fwd(R, W, K, V, KK, A, f"rwkv_scan_fwd{l}")
    (out,) = tl_fwd(rwkv_post, f"rwkv_post_fwd{l}", [Y, R, K, V, G], pars["post"], [D_GROUP])
    return out, V, (seg, taps, rows, R, W, K, V, KK, A, G, Y, sall)


def rwkv_bwd(l, saved, pars, dout, dv_extra):
    seg, taps, rows, R, W, K, V, KK, A, G, Y, sall = saved
    (dY, dR1, dK1, dV1, dG), dpost = tl_bwd(rwkv_post, f"rwkv_post_bwd{l}", [Y, R, K, V, G], pars["post"], [[dout]])
    (dR2, dW, dK2, dKK, dA), dV2 = scan_bwd(R, W, K, V, KK, A, sall, dY, f"rwkv_scan_bwd{l}")
    cts = [[dR1, dR2], [dW], [dK1, dK2], [dV1, dV2] + dv_extra, [dKK], [dA], [dG]]
    drows, dpre = tl_bwd(rwkv_pre(l), f"rwkv_pre_bwd{l}", rows, pars["pre"], cts)
    dseg, dtaps, _ = fir_bwd(seg, taps, [drows[0]], f"rwkv_shift_bwd{l}")
    return dseg, (drows[1] if l else None), dtaps, dpre, dpost


def attn_mix_fwd(l, proj):
    os_, ls_ = [], []
    for b, (_, d) in enumerate(DILATED_BRANCHES):
        o, lse = attn_fwd(proj, d, f"attn_fwd{l}_{b}")
        os_.append(o)
        ls_.append(lse)
    (out,) = tl_fwd(attn_combine, f"attn_combine_fwd{l}", os_ + ls_, [], [D_GROUP])
    return out, (proj, os_, ls_)


def attn_mix_bwd(l, saved, dout):
    proj, os_, ls_ = saved
    drows, _ = tl_bwd(attn_combine, f"attn_combine_bwd{l}", os_ + ls_, [], [[dout]])
    grads = [attn_bwd(proj, drows[b], drows[3 + b], d, f"attn_bwd{l}_{b}") for b, (_, d) in enumerate(DILATED_BRANCHES)]
    return tuple(add_rows([g[j] for g in grads], f"attn_d{'qkv'[j]}{l}") for j in range(3))


def ssd_fwd(l, z, xbc, dtr, pars):
    xc = fir_fwd(xbc, pars["taps"], f"ssd_conv_fwd{l}")
    rr, w, kk, xdt, xs = tl_fwd(ssd_pre, f"ssd_pre_fwd{l}", [xc, dtr], pars["pre"], [HM128, HM128, HM128, HM64, D_GROUP])
    blocks = [rr, kk, w, xdt]
    ys, states = chunk_fwd(ssd_chunk, f"ssd_scan_fwd{l}", blocks, (HEAD_DIM, SSD_STATE), HEAD_DIM, SSD_GROUP)
    (out,) = tl_fwd(ssd_post, f"ssd_post_fwd{l}", [ys, z, xs], pars["post"], [D_GROUP])
    return out, (z, xbc, dtr, xc, blocks, states, xs, ys)


def ssd_bwd(l, saved, pars, dout):
    z, xbc, dtr, xc, blocks, states, xs, ys = saved
    (dys, dz, dxs), dpost = tl_bwd(ssd_post, f"ssd_post_bwd{l}", [ys, z, xs], pars["post"], [[dout]])
    drr, dkk, dw, dxdt = chunk_bwd(ssd_chunk, f"ssd_scan_bwd{l}", blocks, states, dys, SSD_GROUP)
    (dxc, ddtr), dpre = tl_bwd(ssd_pre, f"ssd_pre_bwd{l}", [xc, dtr], pars["pre"], [[drr], [dw], [dkk], [dxdt], [dxs]])
    dxbc, dtaps, _ = fir_bwd(xbc, pars["taps"], [dxc], f"ssd_conv_bwd{l}")
    return dz, dxbc, ddtr, dtaps, dpre, dpost


def hgrn_fwd(l, seg, pars):
    blocks = tl_fwd(hgrn_pre, f"hgrn_pre_fwd{l}", [seg], pars["pre"], [HM64] * 4)
    o, states = chunk_fwd(hgrn_chunk, f"hgrn_scan_fwd{l}", blocks, (HEAD_DIM, HEAD_DIM), HEAD_DIM, HGRN_GROUP, HGRN_CHUNK)
    (out,) = tl_fwd(hgrn_post, f"hgrn_post_fwd{l}", [o, seg], pars["post"], [D_GROUP])
    return out, (seg, blocks, states, o)


def hgrn_bwd(l, saved, pars, dout):
    seg, blocks, states, o = saved
    (do, dseg1), dpost = tl_bwd(hgrn_post, f"hgrn_post_bwd{l}", [o, seg], pars["post"], [[dout]])
    dq, dkk, dlf, di = chunk_bwd(hgrn_chunk, f"hgrn_scan_bwd{l}", blocks, states, do, HGRN_GROUP, HGRN_CHUNK)
    (dseg2,), dpre = tl_bwd(hgrn_pre, f"hgrn_pre_bwd{l}", [seg], pars["pre"], [[dq], [dkk], [dlf], [di]])
    return add_rows([dseg1, dseg2], f"hgrn_dseg{l}"), dpre, dpost


def layer_fwd(l, x, wts, pars, vfirst):
    proj = matmul(x, wts["in"], "nn", f"proj_fwd{l}")
    seg_h, seg_r = Cols(proj, SEG_HGRN, 1024), Cols(proj, SEG_RWKV, 1024)
    z, xbc, dtr = Cols(proj, SEG_Z, D_GROUP), Cols(proj, SEG_XBC, SSD_XBC), Cols(proj, SEG_DT, 128)
    ya, v_rwkv, sa = rwkv_fwd(l, seg_r, pars["rwkv"]["taps"], pars["rwkv"], vfirst)
    yb, sb = attn_mix_fwd(l, proj)
    yc, sc = ssd_fwd(l, z, xbc, dtr, pars["ssd"])
    yd, sd = hgrn_fwd(l, seg_h, pars["hgrn"])
    mix = [ya, yb, yc, yd]
    mo = matmul(mix, wts["out"], "nn", f"out_fwd{l}")
    (x1,) = tl_fwd(ln_res, f"ln1_fwd{l}", [x, mo], pars["ln1"], [D_MODEL])
    u = matmul(x1, wts["up"], "nn", f"up_fwd{l}")
    dn = matmul(u, wts["down"], "nn", f"down_fwd{l}", a_relu2=True)
    (x2,) = tl_fwd(ln_res, f"ln2_fwd{l}", [x1, dn], pars["ln2"], [D_MODEL])
    return x2, v_rwkv, (x, sa, sb, sc, sd, mix, mo, x1, u, dn)


def layer_bwd(l, saved, wts, pars, dx2, dv_extra, stacks):
    into = lambda key: {"into": (DEPTH, l), "stacked": stacks.get(key)}
    x, sa, sb, sc, sd, mix, mo, x1, u, dn = saved
    S = x.shape[0]
    g = {}
    (dx1a, ddn), g["ln2"] = tl_bwd(ln_res, f"ln2_bwd{l}", [x1, dn], pars["ln2"], [[dx2]])
    g["down"] = matmul(u, ddn, "tn", f"down_dw{l}", a_relu2=True, **into("down"))
    du = matmul(ddn, wts["down"], "nt", f"down_dx{l}", relu2_grad_of=u)
    g["up"] = matmul(x1, du, "tn", f"up_dw{l}", out_col_shards=True, **into("up"))
    dx1 = matmul(du, wts["up"], "nt", f"up_dx{l}", add=dx1a)
    (dxa, dmo), g["ln1"] = tl_bwd(ln_res, f"ln1_bwd{l}", [x, mo], pars["ln1"], [[dx1]])
    g["out"] = matmul(mix, dmo, "tn", f"out_dw{l}", **into("out"))
    dmix = matmul(dmo, wts["out"], "nt", f"out_dx{l}")
    dya, dyb, dyc, dyd = (Cols(dmix, j * D_GROUP, D_GROUP) for j in range(4))
    dseg_r, dvfirst, g["rwkv_taps"], g["rwkv_pre"], g["rwkv_post"] = rwkv_bwd(l, sa, pars["rwkv"], dya, dv_extra)
    dq, dk, dv = attn_mix_bwd(l, sb, dyb)
    dz, dxbc, ddtr, g["ssd_taps"], g["ssd_pre"], g["ssd_post"] = ssd_bwd(l, sc, pars["ssd"], dyc)
    dseg_h, g["hgrn_pre"], g["hgrn_post"] = hgrn_bwd(l, sd, pars["hgrn"], dyd)
    dproj = jnp.concatenate([dseg_h, dseg_r, dq, dk, dv, dz, dxbc, ddtr, jnp.zeros((S, PROJ_W - SEG_DT - 128), F32)], axis=1)
    g["in"] = matmul(x, dproj, "tn", f"proj_dw{l}")
    dx = matmul(dproj, wts["in"], "nt", f"proj_dx{l}", add=dxa)
    return dx, dvfirst, g


SMALL = ("lower_bounds", "w_in_vres", "mu_shift", "mu_vres", "rwkv_w0", "rwkv_w2", "rwkv_a0", "rwkv_a2", "rwkv_g2",
         "rwkv_k_k", "rwkv_k_a", "rwkv_r_k", "rwkv_lnx_w", "rwkv_lnx_b", "rwkv_v0", "rwkv_v2", "ssd_conv_w",
         "ssd_conv_b", "ssd_dt_bias", "ssd_A_log", "ssd_D", "ssd_norm_w", "hgrn_norm_w", "ln1_w", "ln1_b", "ln2_w", "ln2_b")
BIG = ("w_in", "w_out", "w_up", "w_down")
SMALL_SHARDED = {"w_in_vres": 1, "rwkv_w2": 2, "rwkv_a2": 2, "rwkv_g2": 2, "rwkv_v2": 2, "ssd_conv_w": 2}
WEIGHTS = ("lower_bounds", "w_in", "w_in_vres", "mu_shift", "mu_vres", "rwkv_w0", "rwkv_w2", "rwkv_a0", "rwkv_a2",
           "rwkv_g2", "rwkv_k_k", "rwkv_k_a", "rwkv_r_k", "rwkv_lnx_w", "rwkv_lnx_b", "rwkv_v0", "rwkv_v2",
           "ssd_conv_w", "ssd_conv_b", "ssd_dt_bias", "ssd_A_log", "ssd_D", "ssd_norm_w", "hgrn_norm_w", "w_out",
           "ln1_w", "ln1_b", "w_up", "w_down", "ln2_w", "ln2_b")


def _row(v, width=None):
    v = v.reshape(1, -1).astype(F32)
    if width is not None and v.shape[1] < width:
        v = jnp.pad(v, ((0, 0), (0, width - v.shape[1])))
    return v


def _rows_at(m, rows, at):
    return jnp.pad(m.astype(F32), ((at, rows - at - m.shape[0]), (0, 0)))


SHARD_COLS = IN_COLS // 4


def _shard_runs(start, width):
    runs, pos = [], start
    while pos < start + width:
        s = pos // SHARD_COLS
        end = min(start + width, (s + 1) * SHARD_COLS)
        runs.append((s, pos - s * SHARD_COLS, end - s * SHARD_COLS))
        pos = end
    return runs


def _pad_w_in(shards, vres):
    rows = shards.shape[1]
    out, pos = [], 0
    for start, width, at in sorted(_PIECES, key=lambda p: p[2]):
        if at > pos:
            out.append(jnp.zeros((rows, at - pos), shards.dtype))
        out += [shards[s, :, lo:hi] for s, lo, hi in _shard_runs(start, width)]
        pos = at + width
        if at == SEG_RWKV and vres is not None:
            out.append(vres.astype(shards.dtype))
            pos += vres.shape[1]
    out.append(jnp.zeros((rows, PROJ_W - pos), shards.dtype))
    return jnp.concatenate(out, axis=1)


def _unpad_w_in(g):
    shards = [[] for _ in range(4)]
    for start, width, at in _PIECES:
        for s, lo, hi in _shard_runs(start, width):
            first = at + s * SHARD_COLS + lo - start
            shards[s].append(g[:, first:first + hi - lo])
    return jnp.stack([jnp.concatenate(p, axis=1) for p in shards])


def layer_params(l, sp, prep):
    lb, mu, om = prep[l], prep[2 + 2 * l], prep[3 + 2 * l]
    pre = [_row(sp["rwkv_w0"][l]), _rows_at(sp["rwkv_w2"][l], 128, 0), _row(sp["rwkv_a0"][l]),
           _rows_at(sp["rwkv_a2"][l], 128, 32), _rows_at(sp["rwkv_g2"][l], 128, 64),
           _row(sp["rwkv_k_k"][l]), _row(sp["rwkv_k_a"][l])]
    if l:
        pre += [_row(sp["rwkv_v0"][l - 1]), _rows_at(sp["rwkv_v2"][l - 1], 128, 0)]
    return {
        "rwkv": {"taps": jnp.concatenate([mu, om], axis=0), "pre": pre,
                 "post": [_row(sp["rwkv_lnx_w"][l]), _row(sp["rwkv_lnx_b"][l]), _row(sp["rwkv_r_k"][l])]},
        "ssd": {"taps": sp["ssd_conv_w"][l].astype(F32),
                "pre": [_row(sp["ssd_conv_b"][l]), _row(sp["ssd_dt_bias"][l], 128), _row(sp["ssd_A_log"][l], 128)],
                "post": [_row(sp["ssd_D"][l], 128), _row(sp["ssd_norm_w"][l])]},
        "hgrn": {"pre": [lb], "post": [_row(sp["hgrn_norm_w"][l])]},
        "ln1": [_row(sp["ln1_w"][l]), _row(sp["ln1_b"][l])],
        "ln2": [_row(sp["ln2_w"][l]), _row(sp["ln2_b"][l])],
    }


def _mu_full(sp, l):
    parts = [sp["mu_shift"][l].reshape(1, -1)]
    if l:
        parts.append(sp["mu_vres"][l - 1].reshape(1, -1))
    return _row(jnp.concatenate(parts, axis=1), 1024)


def local_step(x, target, big, sp):
    prep_in = [sp["lower_bounds"].astype(F32), _mu_full(sp, 0), _mu_full(sp, 1)]
    prep = small_fwd(param_prep, "param_prep_fwd", prep_in,
                     [(1, D_GROUP), (1, D_GROUP), (1, 1024), (1, 1024), (1, 1024), (1, 1024)])
    pars, wts = [], []
    for l in range(DEPTH):
        pars.append(layer_params(l, sp, prep))
        vres = sp["w_in_vres"][l - 1].astype(BF16) if l else None
        wts.append({"in": _pad_w_in(big["w_in"][:, l], vres), "out": big["w_out"][l],
                    "up": ShardedWeight(big["w_up"], l, 1), "down": ShardedWeight(big["w_down"], l, 0)})
    h, vfirst, saved = x, None, []
    for l in range(DEPTH):
        h, v_l, sv = layer_fwd(l, h, wts[l], pars[l], vfirst)
        vfirst = v_l if l == 0 else vfirst
        saved.append(sv)
    loss_row, dh = loss_and_grad(h, target, "loss")
    grads, dv_extra = [None] * DEPTH, []
    for l in reversed(range(DEPTH)):
        stacks = {} if l == DEPTH - 1 else {key: grads[l + 1][key] for key in ("down", "up", "out")}
        dh, dvfirst, grads[l] = layer_bwd(l, saved[l], wts[l], pars[l], dh, dv_extra, stacks)
        dv_extra = [dvfirst] if l else []
    cts = [grads[0]["hgrn_pre"][0], grads[1]["hgrn_pre"][0]]
    for l in range(DEPTH):
        cts += [grads[l]["rwkv_taps"][0:1], grads[l]["rwkv_taps"][1:2]]
    d_lower, d_mu0, d_mu1 = small_bwd(param_prep, "param_prep_bwd", prep_in, cts)
    d_mu = [d_mu0, d_mu1]
    gb = {"w_in": jnp.stack([_unpad_w_in(grads[l]["in"]) for l in range(DEPTH)]), "w_out": grads[0]["out"],
          "w_up": grads[0]["up"], "w_down": grads[0]["down"]}
    st = lambda f: jnp.stack([f(l) for l in range(DEPTH)])
    g1 = grads[1]
    gs = {
        "lower_bounds": d_lower,
        "w_in_vres": g1["in"][None, :, VRES_COL:VRES_COL + 32],
        "mu_shift": st(lambda l: d_mu[l][0, :896]),
        "mu_vres": d_mu[1][:, 896:928],
        "rwkv_w0": st(lambda l: grads[l]["rwkv_pre"][0][0]),
        "rwkv_w2": st(lambda l: grads[l]["rwkv_pre"][1][0:32]),
        "rwkv_a0": st(lambda l: grads[l]["rwkv_pre"][2][0]),
        "rwkv_a2": st(lambda l: grads[l]["rwkv_pre"][3][32:64]),
        "rwkv_g2": st(lambda l: grads[l]["rwkv_pre"][4][64:128]),
        "rwkv_k_k": st(lambda l: grads[l]["rwkv_pre"][5][0]),
        "rwkv_k_a": st(lambda l: grads[l]["rwkv_pre"][6][0]),
        "rwkv_r_k": st(lambda l: grads[l]["rwkv_post"][2].reshape(N_HEADS, HEAD_DIM)),
        "rwkv_lnx_w": st(lambda l: grads[l]["rwkv_post"][0][0]),
        "rwkv_lnx_b": st(lambda l: grads[l]["rwkv_post"][1][0]),
        "rwkv_v0": g1["rwkv_pre"][7],
        "rwkv_v2": g1["rwkv_pre"][8][None, 0:32],
        "ssd_conv_w": st(lambda l: grads[l]["ssd_taps"]),
        "ssd_conv_b": st(lambda l: grads[l]["ssd_pre"][0][0]),
        "ssd_dt_bias": st(lambda l: grads[l]["ssd_pre"][1][0, :N_HEADS]),
        "ssd_A_log": st(lambda l: grads[l]["ssd_pre"][2][0, :N_HEADS]),
        "ssd_D": st(lambda l: grads[l]["ssd_post"][0][0, :N_HEADS]),
        "ssd_norm_w": st(lambda l: grads[l]["ssd_post"][1][0]),
        "hgrn_norm_w": st(lambda l: grads[l]["hgrn_post"][0][0]),
        "ln1_w": st(lambda l: grads[l]["ln1"][0][0]),
        "ln1_b": st(lambda l: grads[l]["ln1"][1][0]),
        "ln2_w": st(lambda l: grads[l]["ln2"][0][0]),
        "ln2_b": st(lambda l: grads[l]["ln2"][1][0]),
    }
    return loss_row, dh, gb, gs


def _pack(vecs):
    parts, meta, row = [], [], 0
    for v in vecs:
        rows = -(-v.size // 1024) * 8
        flat = v.reshape(-1).astype(F32)
        parts.append(jnp.pad(flat, (0, rows * 128 - v.size)).reshape(rows, 128))
        meta.append((row, v.shape))
        row += rows
    return jnp.concatenate(parts, axis=0), meta


def _unpack(packed, meta):
    out = []
    for row, shape in meta:
        size = math.prod(shape)
        rows = -(-size // 1024) * 8
        out.append(packed[row:row + rows].reshape(-1)[:size].reshape(shape))
    return out


def _from_chips(name, g):
    if name != "w_out":
        return g
    return jnp.transpose(g, (1, 0, 2, 3)).reshape(g.shape[1], 4 * g.shape[2], g.shape[3])


INPUT_NAMES = ("x",) + WEIGHTS + ("loss_target",) + tuple("m_" + n for n in WEIGHTS) + tuple("v_" + n for n in WEIGHTS)


def _step(*args):
    a = dict(zip(INPUT_NAMES, args, strict=True))
    chip = 2 * lax.axis_index("x") + lax.axis_index("y")

    sharded_names = list(SMALL_SHARDED)
    small_pack, small_meta = _pack([a[n] for n in sharded_names])
    own = [a[n].astype(BF16) for n in BIG]
    gathered = gather_chips(own, small_pack, "gather_weights")
    here = lambda full, mine: lax.dynamic_update_slice(full, mine[None], (chip,) + (0,) * mine.ndim)
    big = {n: _from_chips(n, here(g, o)) for n, g, o in zip(BIG, gathered, own)}
    sp = {n: a[n] for n in SMALL if n not in SMALL_SHARDED}
    small_all = here(gathered[-1], small_pack)
    per_chip = [_unpack(small_all[s], small_meta) for s in range(4)]
    for j, n in enumerate(sharded_names):
        sp[n] = jnp.concatenate([per_chip[s][j] for s in range(4)], axis=SMALL_SHARDED[n])

    loss_row, gx, gb, gs = local_step(a["x"][0], a["loss_target"][0], big, sp)

    partials = [gb[n].reshape((DEPTH, 4) + a[n].shape[1:]) for n in BIG]
    got = pair_exchange(partials, "pair_exchange")
    ids = jnp.stack([lax.axis_index("c"), chip]).astype(jnp.int32)
    chip_sums = [pair_sum(p, q, ids, f"pair_sum_{n}") for n, p, q in zip(BIG, partials, got)]
    slots = reduce_chips(chip_sums, "reduce_big")
    mine = [sum_chips(sl, p, q, ids, f"sum_{n}") for n, sl, p, q in zip(BIG, slots, partials, got)]
    summed = sibling_exchange(mine, "exchange_big")
    out_g, out_d, out_m, out_v = {}, {}, {}, {}
    for n, g in zip(BIG, summed):
        if n == "w_in":
            to_t, from_t = (lambda t: jnp.transpose(t, (2, 0, 1))), (lambda t: jnp.transpose(t, (1, 2, 0)))
            g_t = to_t(g)
            res = adamw(to_t(a[n]), g_t, to_t(a["m_" + n]), to_t(a["v_" + n]), f"adamw_{n}", lane_tiled=True)
            out_g[n], (out_d[n], out_m[n], out_v[n]) = from_t(g_t), [from_t(r) for r in res]
            continue
        out_g[n] = g
        out_d[n], out_m[n], out_v[n] = adamw(a[n], g, a["m_" + n], a["v_" + n], f"adamw_{n}")

    vec, meta = _pack([loss_row] + [gs[n] for n in SMALL])
    total = sum_parts(gather_devices(vec, "gather_small"), "sum_small")
    parts = _unpack(total, meta)
    loss = parts[0][0, 0]
    g_small = {}
    for n, g in zip(SMALL, parts[1:]):
        if n in SMALL_SHARDED:
            ax = SMALL_SHARDED[n]
            size = a[n].shape[ax]
            g = lax.dynamic_slice_in_dim(g, chip * size, size, axis=ax)
        g_small[n] = g
    pw, pmeta = _pack([a[n] for n in SMALL])
    pg, _ = _pack([g_small[n] for n in SMALL])
    pm, _ = _pack([a["m_" + n] for n in SMALL])
    pv, _ = _pack([a["v_" + n] for n in SMALL])
    d, nm, nv = adamw(pw[None], pg[None], pm[None], pv[None], "adamw_small")
    for n, dd, mm, vv in zip(SMALL, _unpack(d[0], pmeta), _unpack(nm[0], pmeta), _unpack(nv[0], pmeta)):
        out_g[n], out_d[n], out_m[n], out_v[n] = g_small[n], dd, mm, vv

    return (loss, gx[None], *[out_g[n] for n in WEIGHTS], *[out_d[n] for n in WEIGHTS],
            *[out_m[n] for n in WEIGHTS], *[out_v[n] for n in WEIGHTS])


def kernel(x, lower_bounds, w_in, w_in_vres, mu_shift, mu_vres, rwkv_w0, rwkv_w2, rwkv_a0, rwkv_a2, rwkv_g2, rwkv_k_k, rwkv_k_a, rwkv_r_k, rwkv_lnx_w, rwkv_lnx_b, rwkv_v0, rwkv_v2, ssd_conv_w, ssd_conv_b, ssd_dt_bias, ssd_A_log, ssd_D, ssd_norm_w, hgrn_norm_w, w_out, ln1_w, ln1_b, w_up, w_down, ln2_w, ln2_b, loss_target, m_lower_bounds, m_w_in, m_w_in_vres, m_mu_shift, m_mu_vres, m_rwkv_w0, m_rwkv_w2, m_rwkv_a0, m_rwkv_a2, m_rwkv_g2, m_rwkv_k_k, m_rwkv_k_a, m_rwkv_r_k, m_rwkv_lnx_w, m_rwkv_lnx_b, m_rwkv_v0, m_rwkv_v2, m_ssd_conv_w, m_ssd_conv_b, m_ssd_dt_bias, m_ssd_A_log, m_ssd_D, m_ssd_norm_w, m_hgrn_norm_w, m_w_out, m_ln1_w, m_ln1_b, m_w_up, m_w_down, m_ln2_w, m_ln2_b, v_lower_bounds, v_w_in, v_w_in_vres, v_mu_shift, v_mu_vres, v_rwkv_w0, v_rwkv_w2, v_rwkv_a0, v_rwkv_a2, v_rwkv_g2, v_rwkv_k_k, v_rwkv_k_a, v_rwkv_r_k, v_rwkv_lnx_w, v_rwkv_lnx_b, v_rwkv_v0, v_rwkv_v2, v_ssd_conv_w, v_ssd_conv_b, v_ssd_dt_bias, v_ssd_A_log, v_ssd_D, v_ssd_norm_w, v_hgrn_norm_w, v_w_out, v_ln1_w, v_ln1_b, v_w_up, v_w_down, v_ln2_w, v_ln2_b):
    return _step(x, lower_bounds, w_in, w_in_vres, mu_shift, mu_vres, rwkv_w0, rwkv_w2, rwkv_a0, rwkv_a2, rwkv_g2, rwkv_k_k, rwkv_k_a, rwkv_r_k, rwkv_lnx_w, rwkv_lnx_b, rwkv_v0, rwkv_v2, ssd_conv_w, ssd_conv_b, ssd_dt_bias, ssd_A_log, ssd_D, ssd_norm_w, hgrn_norm_w, w_out, ln1_w, ln1_b, w_up, w_down, ln2_w, ln2_b, loss_target, m_lower_bounds, m_w_in, m_w_in_vres, m_mu_shift, m_mu_vres, m_rwkv_w0, m_rwkv_w2, m_rwkv_a0, m_rwkv_a2, m_rwkv_g2, m_rwkv_k_k, m_rwkv_k_a, m_rwkv_r_k, m_rwkv_lnx_w, m_rwkv_lnx_b, m_rwkv_v0, m_rwkv_v2, m_ssd_conv_w, m_ssd_conv_b, m_ssd_dt_bias, m_ssd_A_log, m_ssd_D, m_ssd_norm_w, m_hgrn_norm_w, m_w_out, m_ln1_w, m_ln1_b, m_w_up, m_w_down, m_ln2_w, m_ln2_b, v_lower_bounds, v_w_in, v_w_in_vres, v_mu_shift, v_mu_vres, v_rwkv_w0, v_rwkv_w2, v_rwkv_a0, v_rwkv_a2, v_rwkv_g2, v_rwkv_k_k, v_rwkv_k_a, v_rwkv_r_k, v_rwkv_lnx_w, v_rwkv_lnx_b, v_rwkv_v0, v_rwkv_v2, v_ssd_conv_w, v_ssd_conv_b, v_ssd_dt_bias, v_ssd_A_log, v_ssd_D, v_ssd_norm_w, v_hgrn_norm_w, v_w_out, v_ln1_w, v_ln1_b, v_w_up, v_w_down, v_ln2_w, v_ln2_b)
```
